```python
import jax
import jax.numpy as jnp
from jax import lax
import numpy as np

D_MODEL = 1024
BATCH = 8
SEQ = 16384
DEPTH = 4

N_MIXERS = 2
N_ATT_LAYERS = (DEPTH + 1) // N_MIXERS
N_HGRN_LAYERS = DEPTH // N_MIXERS
HEAD_DIM = 64
N_Q_HEADS = D_MODEL // HEAD_DIM
N_KV_HEADS = N_Q_HEADS // 4
GROUP = N_Q_HEADS // N_KV_HEADS
Q_DIM = N_Q_HEADS * HEAD_DIM
KV_DIM = N_KV_HEADS * HEAD_DIM
WINDOW = 128
ATT_BLOCK = 128
ROPE_DIM = HEAD_DIM // 4
ROPE_THETA = 500000.0
HGRN_EXPAND = 128
HGRN_HEADS = D_MODEL // HGRN_EXPAND
HGRN_KEY = HGRN_EXPAND
HGRN_VAL = D_MODEL // HGRN_HEADS
HGRN_KW = HGRN_HEADS * HGRN_KEY
HGRN_VW = HGRN_HEADS * HGRN_VAL
HGRN_IN_DIM = 3 * HGRN_KW + 2 * HGRN_VW
HGRN_CHUNK = 64
D_FF = -(-8 * D_MODEL // (3 * 256)) * 256
PLE_DIM = 256
ALPHA = (2 * DEPTH) ** 0.25
BETA = (8 * DEPTH) ** -0.25
LN_EPS = 1e-5

kernel_name = 'hybrid_swa_hgrn2_deepnorm_encoder'


def layer_norm(x, g, b):
    xf = x.astype(jnp.float32)
    mu = xf.mean(-1, keepdims=True)
    var = jnp.square(xf - mu).mean(-1, keepdims=True)
    y = (xf - mu) * lax.rsqrt(var + LN_EPS) * g.astype(jnp.float32) + b.astype(jnp.float32)
    return y.astype(x.dtype)


def rope_tables(seq):
    inv = ROPE_THETA ** (-jnp.arange(0, ROPE_DIM, 2, dtype=jnp.float32) / ROPE_DIM)
    ang = jnp.arange(seq, dtype=jnp.float32)[:, None] * inv[None, :]
    return jnp.cos(ang), jnp.sin(ang)


def apply_partial_rope(x, cos, sin):
    half = ROPE_DIM // 2
    c = cos[None, :, None, :]
    s = sin[None, :, None, :]
    xr = x[..., :ROPE_DIM].astype(jnp.float32)
    x1, x2 = xr[..., :half], xr[..., half:]
    rot = jnp.concatenate([x1 * c - x2 * s, x2 * c + x1 * s], axis=-1).astype(x.dtype)
    return jnp.concatenate([rot, x[..., ROPE_DIM:]], axis=-1)


def windowed_gqa(h, w_qkv, sink, w_o, cos, sin):
    B, S, _ = h.shape
    qkv = h @ w_qkv
    q = qkv[..., :Q_DIM].reshape(B, S, N_Q_HEADS, HEAD_DIM)
    k = qkv[..., Q_DIM:Q_DIM + KV_DIM].reshape(B, S, N_KV_HEADS, HEAD_DIM)
    v = qkv[..., Q_DIM + KV_DIM:].reshape(B, S, N_KV_HEADS, HEAD_DIM)
    q = apply_partial_rope(q, cos, sin).reshape(B, S, N_KV_HEADS, GROUP, HEAD_DIM)
    k = apply_partial_rope(k, cos, sin)
    pad = ((0, 0), (ATT_BLOCK, ATT_BLOCK), (0, 0), (0, 0))
    kp = jnp.pad(k, pad)
    vp = jnp.pad(v, pad)
    n_blocks = S // ATT_BLOCK
    scale = HEAD_DIM ** -0.5
    sink_f = sink.astype(jnp.float32).reshape(N_KV_HEADS, GROUP)[None, :, :, None, None]

    def block_attn(bi):
        start = bi * ATT_BLOCK
        qb = lax.dynamic_slice_in_dim(q, start, ATT_BLOCK, axis=1).astype(jnp.float32)
        kb = lax.dynamic_slice_in_dim(kp, start, 3 * ATT_BLOCK, axis=1).astype(jnp.float32)
        vb = lax.dynamic_slice_in_dim(vp, start, 3 * ATT_BLOCK, axis=1)
        s = jnp.einsum('bqhgd,bkhd->bhgqk', qb, kb) * scale
        qpos = start + jnp.arange(ATT_BLOCK)
        kpos = start - ATT_BLOCK + jnp.arange(3 * ATT_BLOCK)
        valid = (jnp.abs(qpos[:, None] - kpos[None, :]) <= WINDOW) & (kpos >= 0)[None, :] & (kpos < S)[None, :]
        s = jnp.where(valid, s, -jnp.inf)
        m = jnp.maximum(s.max(-1, keepdims=True), sink_f)
        pr = jnp.exp(s - m)
        den = pr.sum(-1, keepdims=True) + jnp.exp(sink_f - m)
        return jnp.einsum('bhgqk,bkhd->bqhgd', (pr / den).astype(vb.dtype), vb)

    o = lax.map(block_attn, jnp.arange(n_blocks))
    o = o.transpose(1, 0, 2, 3, 4, 5).reshape(B, S, Q_DIM)
    return o @ w_o


def gated_linear_scan(q, k, v, log_f):
    B, S, H, K = q.shape
    V = v.shape[-1]
    nc = S // HGRN_CHUNK

    def to_chunks(t):
        return t.reshape(B, nc, HGRN_CHUNK, H, t.shape[-1]).transpose(1, 0, 3, 2, 4)

    lower = jnp.tril(jnp.ones((HGRN_CHUNK, HGRN_CHUNK), dtype=bool))

    def step(state, inp):
        qc, kc, vc, gc = inp
        b = jnp.cumsum(gc, axis=2)
        inter = jnp.einsum('bhtk,bhkv->bhtv', qc * jnp.exp(b), state)
        diff = b[:, :, :, None, :] - b[:, :, None, :, :]
        decay = jnp.exp(jnp.where(lower[:, :, None], diff, -jnp.inf))
        scores = jnp.einsum('bhtk,bhsk,bhtsk->bhts', qc, kc, decay)
        intra = jnp.einsum('bhts,bhsv->bhtv', scores, vc)
        b_last = b[:, :, -1:, :]
        new_state = jnp.exp(b_last[:, :, 0, :])[..., None] * state + jnp.einsum('bhsk,bhsv->bhkv', kc * jnp.exp(b_last - b), vc)
        return new_state, inter + intra

    state0 = jnp.zeros((B, H, K, V), q.dtype)
    _, out = lax.scan(step, state0, (to_chunks(q), to_chunks(k), to_chunks(v), to_chunks(log_f)))
    return out.transpose(1, 0, 3, 2, 4).reshape(B, S, H, V)


def hgrn2_bidirectional(h, w_in, lb_fwd, lb_bwd, norm_g, w_o):
    B, S, _ = h.shape
    z = h @ w_in
    o1, o2, o3, o4 = HGRN_KW, 2 * HGRN_KW, 3 * HGRN_KW, 3 * HGRN_KW + HGRN_VW
    q = jax.nn.silu(z[..., :o1].astype(jnp.float32)).reshape(B, S, HGRN_HEADS, HGRN_KEY)
    v = z[..., o3:o4].astype(jnp.float32).reshape(B, S, HGRN_HEADS, HGRN_VAL)
    gate = z[..., o4:].astype(jnp.float32).reshape(B, S, HGRN_HEADS, HGRN_VAL)

    def forget(zf, lb):
        sig = jax.nn.sigmoid(zf.astype(jnp.float32))
        f = lb + (1.0 - lb) * sig
        k = (1.0 - lb) * (1.0 - sig)
        shape = (B, S, HGRN_HEADS, HGRN_KEY)
        return jnp.log(f).reshape(shape), k.reshape(shape)

    logf_f, k_f = forget(z[..., o1:o2], lb_fwd)
    logf_b, k_b = forget(z[..., o2:o3], lb_bwd)
    out_f = gated_linear_scan(q, k_f, v, logf_f)
    flip = lambda t: jnp.flip(t, axis=1)
    out_b = flip(gated_linear_scan(flip(q), flip(k_b), flip(v), flip(logf_b)))
    o = out_f + out_b
    o = o * lax.rsqrt(jnp.mean(o * o, axis=-1, keepdims=True) + LN_EPS) * norm_g.astype(jnp.float32)
    o = o * jax.nn.silu(gate)
    return o.reshape(B, S, HGRN_VW).astype(h.dtype) @ w_o


def swiglu(x, w_in, w_out):
    gu = x @ w_in
    g, u = jnp.split(gu, 2, axis=-1)
    return (jax.nn.silu(g) * u) @ w_out


def _fwd_setup_inputs(seed: int = 0) -> dict:
    key = jax.random.key(seed)
    ks = jax.random.split(key, 17)
    f32 = jnp.float32
    D = D_MODEL

    def nrm(k, shape, scale):
        return jax.random.normal(k, shape, f32) * scale

    return {
        'x': nrm(ks[0], (BATCH, SEQ, D), 1.0),
        'p': nrm(ks[1], (DEPTH, BATCH, SEQ, PLE_DIM), 1.0),
        'att_w_qkv': nrm(ks[2], (N_ATT_LAYERS, D, Q_DIM + 2 * KV_DIM), D ** -0.5),
        'att_sink': nrm(ks[3], (N_ATT_LAYERS, N_Q_HEADS), 0.5),
        'att_w_o': nrm(ks[4], (N_ATT_LAYERS, Q_DIM, D), BETA * Q_DIM ** -0.5),
        'hgrn_w_in': nrm(ks[5], (N_HGRN_LAYERS, D, HGRN_IN_DIM), D ** -0.5),
        'hgrn_lb_logits': nrm(ks[6], (DEPTH, 2, HGRN_KW), 0.1),
        'hgrn_norm_g': 1.0 + nrm(ks[7], (N_HGRN_LAYERS, HGRN_VAL), 0.01),
        'hgrn_w_o': nrm(ks[8], (N_HGRN_LAYERS, HGRN_VW, D), BETA * HGRN_VW ** -0.5),
        'ln_mix_g': 1.0 + nrm(ks[9], (DEPTH, D), 0.01),
        'ln_mix_b': nrm(ks[10], (DEPTH, D), 0.01),
        'ffn_w_in': nrm(ks[11], (DEPTH, D, 2 * D_FF), D ** -0.5),
        'ffn_w_out': nrm(ks[12], (DEPTH, D_FF, D), BETA * D_FF ** -0.5),
        'ln_ffn_g': 1.0 + nrm(ks[13], (DEPTH, D), 0.01),
        'ln_ffn_b': nrm(ks[14], (DEPTH, D), 0.01),
        'ple_w_gate': nrm(ks[15], (DEPTH, D, D), D ** -0.5),
        'ple_w_proj': nrm(ks[16], (DEPTH, PLE_DIM, D), BETA * PLE_DIM ** -0.5),
    }


def _fwd_reference(x, p, att_w_qkv, att_sink, att_w_o, hgrn_w_in, hgrn_lb_logits, hgrn_norm_g, hgrn_w_o,
              ln_mix_g, ln_mix_b, ffn_w_in, ffn_w_out, ln_ffn_g, ln_ffn_b, ple_w_gate, ple_w_proj):
    S = x.shape[1]
    cos, sin = rope_tables(S)
    lb_sm = jax.nn.softmax(hgrn_lb_logits.astype(jnp.float32), axis=0)
    lb_all = jnp.cumsum(lb_sm, axis=0) - lb_sm[0:1]
    for i in range(DEPTH):
        j = i // N_MIXERS
        if i % N_MIXERS == 0:
            mix = windowed_gqa(x, att_w_qkv[j], att_sink[j], att_w_o[j], cos, sin)
        else:
            mix = hgrn2_bidirectional(x, hgrn_w_in[j], lb_all[i, 0], lb_all[i, 1], hgrn_norm_g[j], hgrn_w_o[j])
        x = layer_norm(ALPHA * x + mix, ln_mix_g[i], ln_mix_b[i])
        x = layer_norm(ALPHA * x + swiglu(x, ffn_w_in[i], ffn_w_out[i]), ln_ffn_g[i], ln_ffn_b[i])
        x = x + jax.nn.sigmoid(x @ ple_w_gate[i]) * (p[i] @ ple_w_proj[i])
    return x


import jax as _jax
import jax.numpy as _jnp

TWIN_FORMAT = 'train_step'
FWD_PARAMS = ['x', 'p', 'att_w_qkv', 'att_sink', 'att_w_o', 'hgrn_w_in', 'hgrn_lb_logits', 'hgrn_norm_g', 'hgrn_w_o', 'ln_mix_g', 'ln_mix_b', 'ffn_w_in', 'ffn_w_out', 'ln_ffn_g', 'ln_ffn_b', 'ple_w_gate', 'ple_w_proj']
TWIN_WEIGHTS = ['att_w_qkv', 'att_sink', 'att_w_o', 'hgrn_w_in', 'hgrn_lb_logits', 'hgrn_norm_g', 'hgrn_w_o', 'ln_mix_g', 'ln_mix_b', 'ffn_w_in', 'ffn_w_out', 'ln_ffn_g', 'ln_ffn_b', 'ple_w_gate', 'ple_w_proj']
TWIN_DIFF_INPUT = 'x'
TWIN_INPUTS = ['x', 'p', 'att_w_qkv', 'att_sink', 'att_w_o', 'hgrn_w_in', 'hgrn_lb_logits', 'hgrn_norm_g', 'hgrn_w_o', 'ln_mix_g', 'ln_mix_b', 'ffn_w_in', 'ffn_w_out', 'ln_ffn_g', 'ln_ffn_b', 'ple_w_gate', 'ple_w_proj', 'loss_target', 'm_att_w_qkv', 'm_att_sink', 'm_att_w_o', 'm_hgrn_w_in', 'm_hgrn_lb_logits', 'm_hgrn_norm_g', 'm_hgrn_w_o', 'm_ln_mix_g', 'm_ln_mix_b', 'm_ffn_w_in', 'm_ffn_w_out', 'm_ln_ffn_g', 'm_ln_ffn_b', 'm_ple_w_gate', 'm_ple_w_proj', 'v_att_w_qkv', 'v_att_sink', 'v_att_w_o', 'v_hgrn_w_in', 'v_hgrn_lb_logits', 'v_hgrn_norm_g', 'v_hgrn_w_o', 'v_ln_mix_g', 'v_ln_mix_b', 'v_ffn_w_in', 'v_ffn_w_out', 'v_ln_ffn_g', 'v_ln_ffn_b', 'v_ple_w_gate', 'v_ple_w_proj']
TWIN_OUTPUTS = ['loss', 'grad_x', 'grad_att_w_qkv', 'grad_att_sink', 'grad_att_w_o', 'grad_hgrn_w_in', 'grad_hgrn_lb_logits', 'grad_hgrn_norm_g', 'grad_hgrn_w_o', 'grad_ln_mix_g', 'grad_ln_mix_b', 'grad_ffn_w_in', 'grad_ffn_w_out', 'grad_ln_ffn_g', 'grad_ln_ffn_b', 'grad_ple_w_gate', 'grad_ple_w_proj', 'delta_att_w_qkv', 'delta_att_sink', 'delta_att_w_o', 'delta_hgrn_w_in', 'delta_hgrn_lb_logits', 'delta_hgrn_norm_g', 'delta_hgrn_w_o', 'delta_ln_mix_g', 'delta_ln_mix_b', 'delta_ffn_w_in', 'delta_ffn_w_out', 'delta_ln_ffn_g', 'delta_ln_ffn_b', 'delta_ple_w_gate', 'delta_ple_w_proj', 'new_m_att_w_qkv', 'new_m_att_sink', 'new_m_att_w_o', 'new_m_hgrn_w_in', 'new_m_hgrn_lb_logits', 'new_m_hgrn_norm_g', 'new_m_hgrn_w_o', 'new_m_ln_mix_g', 'new_m_ln_mix_b', 'new_m_ffn_w_in', 'new_m_ffn_w_out', 'new_m_ln_ffn_g', 'new_m_ln_ffn_b', 'new_m_ple_w_gate', 'new_m_ple_w_proj', 'new_v_att_w_qkv', 'new_v_att_sink', 'new_v_att_w_o', 'new_v_hgrn_w_in', 'new_v_hgrn_lb_logits', 'new_v_hgrn_norm_g', 'new_v_hgrn_w_o', 'new_v_ln_mix_g', 'new_v_ln_mix_b', 'new_v_ffn_w_in', 'new_v_ffn_w_out', 'new_v_ln_ffn_g', 'new_v_ln_ffn_b', 'new_v_ple_w_gate', 'new_v_ple_w_proj']
TWIN_LEAF_KINDS = {'loss': 'loss', 'grad_x': 'grad_x', 'grad_att_w_qkv': 'grad_w', 'grad_att_sink': 'grad_w', 'grad_att_w_o': 'grad_w', 'grad_hgrn_w_in': 'grad_w', 'grad_hgrn_lb_logits': 'grad_w', 'grad_hgrn_norm_g': 'grad_w', 'grad_hgrn_w_o': 'grad_w', 'grad_ln_mix_g': 'grad_w', 'grad_ln_mix_b': 'grad_w', 'grad_ffn_w_in': 'grad_w', 'grad_ffn_w_out': 'grad_w', 'grad_ln_ffn_g': 'grad_w', 'grad_ln_ffn_b': 'grad_w', 'grad_ple_w_gate': 'grad_w', 'grad_ple_w_proj': 'grad_w', 'delta_att_w_qkv': 'delta_w', 'delta_att_sink': 'delta_w', 'delta_att_w_o': 'delta_w', 'delta_hgrn_w_in': 'delta_w', 'delta_hgrn_lb_logits': 'delta_w', 'delta_hgrn_norm_g': 'delta_w', 'delta_hgrn_w_o': 'delta_w', 'delta_ln_mix_g': 'delta_w', 'delta_ln_mix_b': 'delta_w', 'delta_ffn_w_in': 'delta_w', 'delta_ffn_w_out': 'delta_w', 'delta_ln_ffn_g': 'delta_w', 'delta_ln_ffn_b': 'delta_w', 'delta_ple_w_gate': 'delta_w', 'delta_ple_w_proj': 'delta_w', 'new_m_att_w_qkv': 'new_m', 'new_m_att_sink': 'new_m', 'new_m_att_w_o': 'new_m', 'new_m_hgrn_w_in': 'new_m', 'new_m_hgrn_lb_logits': 'new_m', 'new_m_hgrn_norm_g': 'new_m', 'new_m_hgrn_w_o': 'new_m', 'new_m_ln_mix_g': 'new_m', 'new_m_ln_mix_b': 'new_m', 'new_m_ffn_w_in': 'new_m', 'new_m_ffn_w_out': 'new_m', 'new_m_ln_ffn_g': 'new_m', 'new_m_ln_ffn_b': 'new_m', 'new_m_ple_w_gate': 'new_m', 'new_m_ple_w_proj': 'new_m', 'new_v_att_w_qkv': 'new_v', 'new_v_att_sink': 'new_v', 'new_v_att_w_o': 'new_v', 'new_v_hgrn_w_in': 'new_v', 'new_v_hgrn_lb_logits': 'new_v', 'new_v_hgrn_norm_g': 'new_v', 'new_v_hgrn_w_o': 'new_v', 'new_v_ln_mix_g': 'new_v', 'new_v_ln_mix_b': 'new_v', 'new_v_ffn_w_in': 'new_v', 'new_v_ffn_w_out': 'new_v', 'new_v_ln_ffn_g': 'new_v', 'new_v_ln_ffn_b': 'new_v', 'new_v_ple_w_gate': 'new_v', 'new_v_ple_w_proj': 'new_v'}


def _forward(args):
    return _fwd_reference(*[args[k] for k in FWD_PARAMS])


def _output_shape():
    def fwd():
        inp = _fwd_setup_inputs(0)
        return _fwd_reference(*[inp[k] for k in FWD_PARAMS])
    out = _jax.eval_shape(fwd)
    return out.shape, out.dtype

N_MICROBATCH = 1
ADAM_LR = 0.001
ADAM_B1 = 0.9
ADAM_B2 = 0.999
ADAM_EPS = 1e-08
ADAM_WD = 0.01
ADAM_STEP = 10
PER_EXAMPLE_BATCH_AXIS = {'x': 0, 'p': 1, 'loss_target': 0}
SHARED_INPUTS = []
_WEIGHT_DTYPES = {'att_w_qkv': _jnp.float32, 'att_sink': _jnp.float32, 'att_w_o': _jnp.float32, 'hgrn_w_in': _jnp.float32, 'hgrn_lb_logits': _jnp.float32, 'hgrn_norm_g': _jnp.float32, 'hgrn_w_o': _jnp.float32, 'ln_mix_g': _jnp.float32, 'ln_mix_b': _jnp.float32, 'ffn_w_in': _jnp.float32, 'ffn_w_out': _jnp.float32, 'ln_ffn_g': _jnp.float32, 'ln_ffn_b': _jnp.float32, 'ple_w_gate': _jnp.float32, 'ple_w_proj': _jnp.float32}
MOMENT_SCALE = {'att_w_qkv': 1.708709e-02, 'att_sink': 8.514387e-04, 'att_w_o': 3.033871e-02, 'hgrn_w_in': 3.532687e-02, 'hgrn_lb_logits': 2.852819e-03, 'hgrn_norm_g': 1.642856e-01, 'hgrn_w_o': 1.302133e-01, 'ln_mix_g': 1.345082e+00, 'ln_mix_b': 1.647355e+00, 'ffn_w_in': 3.295048e-02, 'ffn_w_out': 1.280226e-01, 'ln_ffn_g': 6.421617e+01, 'ln_ffn_b': 2.012741e+00, 'ple_w_gate': 3.876494e-02, 'ple_w_proj': 3.059522e-01}


def _to_microbatches(a, axis):
    t = _jnp.moveaxis(a, axis, 0)
    t = t.reshape((N_MICROBATCH, t.shape[0] // N_MICROBATCH) + t.shape[1:])
    return _jnp.moveaxis(t, 1, axis + 1)


def setup_inputs(seed: int = 0) -> dict:
    inp = _fwd_setup_inputs(seed)
    key = _jax.random.fold_in(_jax.random.key(seed), 7919)
    shape, _ = _output_shape()
    out = dict(inp)
    out["loss_target"] = _jax.random.normal(_jax.random.fold_in(key, 0), shape, _jnp.float32)
    for i, name in enumerate(TWIN_WEIGHTS):
        w = inp[name].astype(_jnp.float32)
        if MOMENT_SCALE is None:
            s = _jnp.sqrt(_jnp.mean(_jnp.square(w)) + 1e-30)
        else:
            s = MOMENT_SCALE[name]
        km, kv = _jax.random.split(_jax.random.fold_in(key, i + 1))
        out[name] = w
        out["m_" + name] = s * _jax.random.normal(km, w.shape, _jnp.float32)
        out["v_" + name] = (s * s) * _jax.random.uniform(kv, w.shape, _jnp.float32, 0.5, 1.5)
    if N_MICROBATCH > 1:
        for name, axis in PER_EXAMPLE_BATCH_AXIS.items():
            out[name] = _to_microbatches(out[name], axis)
    return {'x': out['x'], 'p': out['p'], 'att_w_qkv': out['att_w_qkv'], 'att_sink': out['att_sink'], 'att_w_o': out['att_w_o'], 'hgrn_w_in': out['hgrn_w_in'], 'hgrn_lb_logits': out['hgrn_lb_logits'], 'hgrn_norm_g': out['hgrn_norm_g'], 'hgrn_w_o': out['hgrn_w_o'], 'ln_mix_g': out['ln_mix_g'], 'ln_mix_b': out['ln_mix_b'], 'ffn_w_in': out['ffn_w_in'], 'ffn_w_out': out['ffn_w_out'], 'ln_ffn_g': out['ln_ffn_g'], 'ln_ffn_b': out['ln_ffn_b'], 'ple_w_gate': out['ple_w_gate'], 'ple_w_proj': out['ple_w_proj'], 'loss_target': out['loss_target'], 'm_att_w_qkv': out['m_att_w_qkv'], 'm_att_sink': out['m_att_sink'], 'm_att_w_o': out['m_att_w_o'], 'm_hgrn_w_in': out['m_hgrn_w_in'], 'm_hgrn_lb_logits': out['m_hgrn_lb_logits'], 'm_hgrn_norm_g': out['m_hgrn_norm_g'], 'm_hgrn_w_o': out['m_hgrn_w_o'], 'm_ln_mix_g': out['m_ln_mix_g'], 'm_ln_mix_b': out['m_ln_mix_b'], 'm_ffn_w_in': out['m_ffn_w_in'], 'm_ffn_w_out': out['m_ffn_w_out'], 'm_ln_ffn_g': out['m_ln_ffn_g'], 'm_ln_ffn_b': out['m_ln_ffn_b'], 'm_ple_w_gate': out['m_ple_w_gate'], 'm_ple_w_proj': out['m_ple_w_proj'], 'v_att_w_qkv': out['v_att_w_qkv'], 'v_att_sink': out['v_att_sink'], 'v_att_w_o': out['v_att_w_o'], 'v_hgrn_w_in': out['v_hgrn_w_in'], 'v_hgrn_lb_logits': out['v_hgrn_lb_logits'], 'v_hgrn_norm_g': out['v_hgrn_norm_g'], 'v_hgrn_w_o': out['v_hgrn_w_o'], 'v_ln_mix_g': out['v_ln_mix_g'], 'v_ln_mix_b': out['v_ln_mix_b'], 'v_ffn_w_in': out['v_ffn_w_in'], 'v_ffn_w_out': out['v_ffn_w_out'], 'v_ln_ffn_g': out['v_ln_ffn_g'], 'v_ln_ffn_b': out['v_ln_ffn_b'], 'v_ple_w_gate': out['v_ple_w_gate'], 'v_ple_w_proj': out['v_ple_w_proj']}


def _loss(weights, diff, rest, loss_target):
    with _jax.named_scope("forward"):
        args = {**rest, TWIN_DIFF_INPUT: diff, **{k: w.astype(_WEIGHT_DTYPES[k]) for k, w in weights.items()}}
        y = _forward(args)
    with _jax.named_scope("loss_head"):
        err = _jnp.square(y.astype(_jnp.float32) - loss_target)
        return 0.5 * _jnp.sum(_jnp.mean(err, axis=-1)) if err.ndim else 0.5 * err


def _adamw(w, g, m, v):
    m = ADAM_B1 * m + (1.0 - ADAM_B1) * g
    v = ADAM_B2 * v + (1.0 - ADAM_B2) * _jnp.square(g)
    m_hat = m / (1.0 - ADAM_B1 ** ADAM_STEP)
    v_hat = v / (1.0 - ADAM_B2 ** ADAM_STEP)
    delta = -ADAM_LR * (m_hat / (_jnp.sqrt(v_hat) + ADAM_EPS) + ADAM_WD * w)
    return delta, m, v


def reference(x, p, att_w_qkv, att_sink, att_w_o, hgrn_w_in, hgrn_lb_logits, hgrn_norm_g, hgrn_w_o, ln_mix_g, ln_mix_b, ffn_w_in, ffn_w_out, ln_ffn_g, ln_ffn_b, ple_w_gate, ple_w_proj, loss_target, m_att_w_qkv, m_att_sink, m_att_w_o, m_hgrn_w_in, m_hgrn_lb_logits, m_hgrn_norm_g, m_hgrn_w_o, m_ln_mix_g, m_ln_mix_b, m_ffn_w_in, m_ffn_w_out, m_ln_ffn_g, m_ln_ffn_b, m_ple_w_gate, m_ple_w_proj, v_att_w_qkv, v_att_sink, v_att_w_o, v_hgrn_w_in, v_hgrn_lb_logits, v_hgrn_norm_g, v_hgrn_w_o, v_ln_mix_g, v_ln_mix_b, v_ffn_w_in, v_ffn_w_out, v_ln_ffn_g, v_ln_ffn_b, v_ple_w_gate, v_ple_w_proj):
    given = dict(x=x, p=p, att_w_qkv=att_w_qkv, att_sink=att_sink, att_w_o=att_w_o, hgrn_w_in=hgrn_w_in, hgrn_lb_logits=hgrn_lb_logits, hgrn_norm_g=hgrn_norm_g, hgrn_w_o=hgrn_w_o, ln_mix_g=ln_mix_g, ln_mix_b=ln_mix_b, ffn_w_in=ffn_w_in, ffn_w_out=ffn_w_out, ln_ffn_g=ln_ffn_g, ln_ffn_b=ln_ffn_b, ple_w_gate=ple_w_gate, ple_w_proj=ple_w_proj, loss_target=loss_target, m_att_w_qkv=m_att_w_qkv, m_att_sink=m_att_sink, m_att_w_o=m_att_w_o, m_hgrn_w_in=m_hgrn_w_in, m_hgrn_lb_logits=m_hgrn_lb_logits, m_hgrn_norm_g=m_hgrn_norm_g, m_hgrn_w_o=m_hgrn_w_o, m_ln_mix_g=m_ln_mix_g, m_ln_mix_b=m_ln_mix_b, m_ffn_w_in=m_ffn_w_in, m_ffn_w_out=m_ffn_w_out, m_ln_ffn_g=m_ln_ffn_g, m_ln_ffn_b=m_ln_ffn_b, m_ple_w_gate=m_ple_w_gate, m_ple_w_proj=m_ple_w_proj, v_att_w_qkv=v_att_w_qkv, v_att_sink=v_att_sink, v_att_w_o=v_att_w_o, v_hgrn_w_in=v_hgrn_w_in, v_hgrn_lb_logits=v_hgrn_lb_logits, v_hgrn_norm_g=v_hgrn_norm_g, v_hgrn_w_o=v_hgrn_w_o, v_ln_mix_g=v_ln_mix_g, v_ln_mix_b=v_ln_mix_b, v_ffn_w_in=v_ffn_w_in, v_ffn_w_out=v_ffn_w_out, v_ln_ffn_g=v_ln_ffn_g, v_ln_ffn_b=v_ln_ffn_b, v_ple_w_gate=v_ple_w_gate, v_ple_w_proj=v_ple_w_proj)
    weights = {n: given[n] for n in TWIN_WEIGHTS}
    shared = {n: given[n] for n in SHARED_INPUTS}
    per_example = {n: given[n] for n in ['x', 'p']}
    grad_fn = _jax.value_and_grad(_loss, argnums=(0, 1))

    def one_microbatch(ex, loss_target):
        ex = dict(ex)
        diff = ex.pop(TWIN_DIFF_INPUT)
        return grad_fn(weights, diff, {**shared, **ex}, loss_target)

    if N_MICROBATCH == 1:
        loss, (grad_w, grad_x) = one_microbatch(per_example, given["loss_target"])
    else:
        def body(carry, xs):
            loss_sum, grad_sum = carry
            l_k, (gw_k, gx_k) = one_microbatch(xs[0], xs[1])
            with _jax.named_scope("update"):
                return (loss_sum + l_k, _jax.tree.map(_jnp.add, grad_sum, gw_k)), gx_k

        init = (_jnp.zeros((), _jnp.float32), _jax.tree.map(_jnp.zeros_like, weights))
        (loss, grad_w), grad_x = _jax.lax.scan(body, init, (per_example, given["loss_target"]))
    with _jax.named_scope("update"):
        delta_w, new_m, new_v = {}, {}, {}
        for n in TWIN_WEIGHTS:
            delta_w[n], new_m[n], new_v[n] = _adamw(weights[n], grad_w[n], given["m_" + n], given["v_" + n])
    return (loss, grad_x, *[grad_w[n] for n in TWIN_WEIGHTS], *[delta_w[n] for n in TWIN_WEIGHTS],
            *[new_m[n] for n in TWIN_WEIGHTS], *[new_v[n] for n in TWIN_WEIGHTS])
```

```python
import functools

import jax
import jax.numpy as jnp
from jax import lax
from jax.experimental import pallas as pl
from jax.experimental.pallas import tpu as pltpu

F32, BF16 = jnp.float32, jnp.bfloat16

D_MODEL = 1024
DEPTH = 4
HEAD_DIM = 64
N_Q_HEADS = 16
N_KV_HEADS = 4
GROUP = 4
Q_DIM = 1024
KV_DIM = 256
WINDOW = 128
ROPE_DIM = 16
ROPE_THETA = 500000.0
HGRN_HEADS = 8
HGRN_KEY = 128
HGRN_CHUNK = 64
D_FF = 2816
PLE_DIM = 256
ALPHA = (2 * DEPTH) ** 0.25
LN_EPS = 1e-5
ADAM_LR, ADAM_B1, ADAM_B2, ADAM_EPS, ADAM_WD, ADAM_STEP = 0.001, 0.9, 0.999, 1e-08, 0.01, 10

LANES = 128
VMEM_LIMIT_BYTES = 56 * 2**20
FACTORED_DECAY_LIMIT = -80.0

NN = ((1,), (0,))
NT = ((1,), (1,))
TN = ((0,), (0,))

N_CHIPS = 4
MESH = pl.DeviceIdType.MESH


def _dot(a, b, dims):
    return lax.dot_general(a, b, (dims, ((), ())), preferred_element_type=F32)


def _cp(*sem):
    return pltpu.CompilerParams(dimension_semantics=sem, vmem_limit_bytes=VMEM_LIMIT_BYTES)


def _rows(t, cap=512):
    return min(cap, t)


def _pick(n, cap):
    best = None
    for d in range(LANES, min(n, cap) + 1, LANES):
        if n % d == 0:
            best = d
    assert best is not None, (n, cap)
    return best


def _sigmoid(x):
    return jax.nn.sigmoid(x)


def _ln_fwd(u, g, b):
    mu = jnp.mean(u, axis=-1, keepdims=True)
    xc = u - mu
    var = jnp.mean(xc * xc, axis=-1, keepdims=True)
    rstd = lax.rsqrt(var + LN_EPS)
    xhat = xc * rstd
    return xhat * g + b, xhat, rstd


def _ln_bwd(dy, xhat, rstd, g):
    dxh = dy * g
    m1 = jnp.mean(dxh, axis=-1, keepdims=True)
    m2 = jnp.mean(dxh * xhat, axis=-1, keepdims=True)
    du = rstd * (dxh - m1 - xhat * m2)
    return du, jnp.sum(dy * xhat, axis=0, keepdims=True), jnp.sum(dy, axis=0, keepdims=True)


def _full(shape):
    nd = len(shape)
    return pl.BlockSpec(shape, lambda *_: (0,) * nd)


def _mm_nn(a, w, *, out_dtype, name):
    t, k = a.shape
    n = w.shape[1]
    tm, tn = _rows(t), _pick(n, 1408)

    def body(a_ref, w_ref, o_ref):
        o_ref[...] = _dot(a_ref[...], w_ref[...], NN).astype(out_dtype)

    return pl.pallas_call(
        body, grid=(n // tn, t // tm),
        in_specs=[pl.BlockSpec((tm, k), lambda j, i: (i, 0)), pl.BlockSpec((k, tn), lambda j, i: (0, j))],
        out_specs=pl.BlockSpec((tm, tn), lambda j, i: (i, j)),
        out_shape=jax.ShapeDtypeStruct((t, n), out_dtype),
        compiler_params=_cp("parallel", "parallel"), name=name)(a, w)


def _mm_tn(a, b, *, name):
    r, m = a.shape
    n = b.shape[1]
    tr, tm, tn = _rows(r), _pick(m, 1408), _pick(n, 1408)

    def body(a_ref, b_ref, o_ref):
        @pl.when(pl.program_id(2) == 0)
        def _():
            o_ref[...] = jnp.zeros_like(o_ref)
        o_ref[...] += _dot(a_ref[...].astype(BF16), b_ref[...].astype(BF16), TN)

    return pl.pallas_call(
        body, grid=(m // tm, n // tn, r // tr),
        in_specs=[pl.BlockSpec((tr, tm), lambda i, j, k: (k, i)), pl.BlockSpec((tr, tn), lambda i, j, k: (k, j))],
        out_specs=pl.BlockSpec((tm, tn), lambda i, j, k: (i, j)),
        out_shape=jax.ShapeDtypeStruct((m, n), F32),
        compiler_params=_cp("parallel", "parallel", "arbitrary"), name=name)(a, b)


def _mm_tn_p(p, layer, b, *, name):
    t = p.shape[2]
    n = b.shape[1]
    tr = _rows(t)

    def body(a_ref, b_ref, o_ref):
        @pl.when(pl.program_id(0) == 0)
        def _():
            o_ref[...] = jnp.zeros_like(o_ref)
        o_ref[...] += _dot(a_ref[...].astype(BF16), b_ref[...], TN)

    return pl.pallas_call(
        body, grid=(t // tr,),
        in_specs=[pl.BlockSpec((None, None, tr, PLE_DIM), lambda k: (layer, 0, k, 0)),
                  pl.BlockSpec((tr, n), lambda k: (k, 0))],
        out_specs=pl.BlockSpec((PLE_DIM, n), lambda k: (0, 0)),
        out_shape=jax.ShapeDtypeStruct((PLE_DIM, n), F32),
        compiler_params=_cp("arbitrary"), name=name)(p, b)


def _mm_res_ln(a, w, resid, g, b, *, name):
    t, k = a.shape
    tm = _rows(t)

    def body(a_ref, w_ref, r_ref, g_ref, b_ref, y_ref, ybf_ref, xh_ref, rs_ref):
        acc = _dot(a_ref[...], w_ref[...], NN)
        y, xh, rs = _ln_fwd(ALPHA * r_ref[...] + acc, g_ref[...], b_ref[...])
        y_ref[...] = y
        ybf_ref[...] = y.astype(BF16)
        xh_ref[...] = xh
        rs_ref[...] = rs

    row = pl.BlockSpec((tm, D_MODEL), lambda i: (i, 0))
    return pl.pallas_call(
        body, grid=(t // tm,),
        in_specs=[pl.BlockSpec((tm, k), lambda i: (i, 0)), _full((k, D_MODEL)), row, _full((1, D_MODEL)), _full((1, D_MODEL))],
        out_specs=[row, row, row, pl.BlockSpec((tm, 1), lambda i: (i, 0))],
        out_shape=[jax.ShapeDtypeStruct((t, D_MODEL), F32), jax.ShapeDtypeStruct((t, D_MODEL), BF16),
                   jax.ShapeDtypeStruct((t, D_MODEL), F32), jax.ShapeDtypeStruct((t, 1), F32)],
        compiler_params=_cp("parallel"), name=name)(a, w, resid, g, b)


def _ffn_in(x_bf, w, *, name):
    t = x_bf.shape[0]
    tm, tn = _rows(t), _pick(D_FF, 1408)
    nb = D_FF // tn

    def body(x_ref, wg_ref, wu_ref, g_ref, u_ref, h_ref):
        x = x_ref[...]
        g = _dot(x, wg_ref[...], NN)
        u = _dot(x, wu_ref[...], NN)
        g_ref[...] = g.astype(BF16)
        u_ref[...] = u.astype(BF16)
        h_ref[...] = (g * _sigmoid(g) * u).astype(BF16)

    tile = pl.BlockSpec((tm, tn), lambda j, i: (i, j))
    shp = jax.ShapeDtypeStruct((t, D_FF), BF16)
    return pl.pallas_call(
        body, grid=(nb, t // tm),
        in_specs=[pl.BlockSpec((tm, D_MODEL), lambda j, i: (i, 0)),
                  pl.BlockSpec((D_MODEL, tn), lambda j, i: (0, j)),
                  pl.BlockSpec((D_MODEL, tn), lambda j, i: (0, j + nb))],
        out_specs=[tile, tile, tile], out_shape=[shp, shp, shp],
        compiler_params=_cp("parallel", "parallel"), name=name)(x_bf, w, w)


def _ple_fwd(x, x_bf, p, layer, wg, wp, *, name):
    t = x.shape[0]
    tm = _rows(t)

    def body(x_ref, xbf_ref, p_ref, wg_ref, wp_ref, y_ref, ybf_ref, gate_ref, pp_ref):
        gate = _sigmoid(_dot(xbf_ref[...], wg_ref[...], NN))
        pp = _dot(p_ref[...].astype(BF16), wp_ref[...], NN)
        y = x_ref[...] + gate * pp
        y_ref[...] = y
        ybf_ref[...] = y.astype(BF16)
        gate_ref[...] = gate.astype(BF16)
        pp_ref[...] = pp.astype(BF16)

    row = pl.BlockSpec((tm, D_MODEL), lambda i: (i, 0))
    f32, bf = jax.ShapeDtypeStruct((t, D_MODEL), F32), jax.ShapeDtypeStruct((t, D_MODEL), BF16)
    return pl.pallas_call(
        body, grid=(t // tm,),
        in_specs=[row, row, pl.BlockSpec((None, None, tm, PLE_DIM), lambda i: (layer, 0, i, 0)),
                  _full((D_MODEL, D_MODEL)), _full((PLE_DIM, D_MODEL))],
        out_specs=[row, row, row, row], out_shape=[f32, bf, bf, bf],
        compiler_params=_cp("parallel"), name=name)(x, x_bf, p, wg, wp)


def _loss_head(y, target, *, name):
    t = y.shape[0]
    tm = _rows(t)

    def body(y_ref, t_ref, loss_ref, dy_ref):
        e = y_ref[...] - t_ref[...]

        @pl.when(pl.program_id(0) == 0)
        def _():
            loss_ref[...] = jnp.zeros_like(loss_ref)
        sq = jnp.sum(jnp.sum(e * e, axis=1, keepdims=True), axis=0, keepdims=True)
        loss_ref[...] += sq * (0.5 / D_MODEL)
        dy_ref[...] = e * (1.0 / D_MODEL)

    row = pl.BlockSpec((tm, D_MODEL), lambda i: (i, 0))
    return pl.pallas_call(
        body, grid=(t // tm,), in_specs=[row, row],
        out_specs=[_full((1, 1)), row],
        out_shape=[jax.ShapeDtypeStruct((1, 1), F32), jax.ShapeDtypeStruct((t, D_MODEL), F32)],
        compiler_params=_cp("arbitrary"), name=name)(y, target)


def _ple_bwd(dx3, gate, pp, wg, xhat, rstd, g, *, name):
    t = dx3.shape[0]
    tm = _rows(t)

    def body(dx_ref, gate_ref, pp_ref, wg_ref, xh_ref, rs_ref, g_ref,
             du_ref, dubf_ref, dpre_ref, dpp_ref, dg_ref, db_ref):
        dx = dx_ref[...]
        gate = gate_ref[...].astype(F32)
        dpre = (dx * pp_ref[...].astype(F32) * gate * (1.0 - gate)).astype(BF16)
        dpre_ref[...] = dpre
        dpp_ref[...] = (dx * gate).astype(BF16)
        dx2 = dx + _dot(dpre, wg_ref[...], NT)
        du, dg, db = _ln_bwd(dx2, xh_ref[...], rs_ref[...], g_ref[...])
        du_ref[...] = du
        dubf_ref[...] = du.astype(BF16)

        @pl.when(pl.program_id(0) == 0)
        def _():
            dg_ref[...] = jnp.zeros_like(dg_ref)
            db_ref[...] = jnp.zeros_like(db_ref)
        dg_ref[...] += dg
        db_ref[...] += db

    row = pl.BlockSpec((tm, D_MODEL), lambda i: (i, 0))
    vec = _full((1, D_MODEL))
    f32, bf = jax.ShapeDtypeStruct((t, D_MODEL), F32), jax.ShapeDtypeStruct((t, D_MODEL), BF16)
    v32 = jax.ShapeDtypeStruct((1, D_MODEL), F32)
    return pl.pallas_call(
        body, grid=(t // tm,),
        in_specs=[row, row, row, _full((D_MODEL, D_MODEL)), row, pl.BlockSpec((tm, 1), lambda i: (i, 0)), vec],
        out_specs=[row, row, row, row, vec, vec], out_shape=[f32, bf, bf, bf, v32, v32],
        compiler_params=_cp("arbitrary"), name=name)(dx3, gate, pp, wg, xhat, rstd, g)


def _ffn_out_bwd(du_bf, w_out, g, u, *, name):
    t = du_bf.shape[0]
    tm, tn = _rows(t), _pick(D_FF, 1408)

    def body(du_ref, w_ref, g_ref, u_ref, dg_ref, dup_ref):
        dh = _dot(du_ref[...], w_ref[...], NT)
        gv = g_ref[...].astype(F32)
        sig = _sigmoid(gv)
        dg_ref[...] = (dh * u_ref[...].astype(F32) * sig * (1.0 + gv * (1.0 - sig))).astype(BF16)
        dup_ref[...] = (dh * gv * sig).astype(BF16)

    tile = pl.BlockSpec((tm, tn), lambda j, i: (i, j))
    shp = jax.ShapeDtypeStruct((t, D_FF), BF16)
    return pl.pallas_call(
        body, grid=(D_FF // tn, t // tm),
        in_specs=[pl.BlockSpec((tm, D_MODEL), lambda j, i: (i, 0)), pl.BlockSpec((tn, D_MODEL), lambda j, i: (j, 0)),
                  tile, tile],
        out_specs=[tile, tile], out_shape=[shp, shp],
        compiler_params=_cp("parallel", "parallel"), name=name)(du_bf, w_out, g, u)


def _mm_nt(parts, w, *, tk, resid=None, ln=None, out_dtype=F32, name):
    t, kp = parts[0].shape
    npart = len(parts)
    nk = kp // tk
    nkt = nk * npart
    tm = _rows(t)
    n_in = npart + 1 + (resid is not None) + (3 if ln is not None else 0)

    def body(*refs):
        a_refs, w_ref = refs[:npart], refs[npart]
        pos = npart + 1
        r_ref = None
        if resid is not None:
            r_ref = refs[pos]
            pos += 1
        if ln is not None:
            xh_ref, rs_ref, g_ref = refs[pos:pos + 3]
        outs, acc_ref = refs[n_in:-1], refs[-1]
        i, k = pl.program_id(0), pl.program_id(1)

        @pl.when(k == 0)
        def _():
            acc_ref[...] = jnp.zeros_like(acc_ref)

        for pi in range(npart):
            @pl.when((k >= pi * nk) & (k < (pi + 1) * nk))
            def _(pi=pi):
                acc_ref[...] += _dot(a_refs[pi][...], w_ref[...], NT)

        @pl.when(k == nkt - 1)
        def _():
            val = acc_ref[...]
            if r_ref is not None:
                val = val + ALPHA * r_ref[...]
            if ln is None:
                outs[0][...] = val.astype(out_dtype)
            else:
                du, dg, db = _ln_bwd(val, xh_ref[...], rs_ref[...], g_ref[...])
                outs[0][...] = du
                outs[1][...] = du.astype(BF16)

                @pl.when(i == 0)
                def _():
                    outs[2][...] = jnp.zeros_like(outs[2])
                    outs[3][...] = jnp.zeros_like(outs[3])
                outs[2][...] += dg
                outs[3][...] += db

    row = pl.BlockSpec((tm, D_MODEL), lambda i, k: (i, 0))
    vec = pl.BlockSpec((1, D_MODEL), lambda i, k: (0, 0))
    in_specs = [pl.BlockSpec((tm, tk), functools.partial(lambda i, k, lo: (i, jnp.clip(k - lo, 0, nk - 1)), lo=pi * nk))
                for pi in range(npart)]
    in_specs.append(pl.BlockSpec((D_MODEL, tk), lambda i, k: (0, k)))
    args = list(parts) + [w]
    if resid is not None:
        in_specs.append(row)
        args.append(resid)
    if ln is not None:
        in_specs += [row, pl.BlockSpec((tm, 1), lambda i, k: (i, 0)), vec]
        args += list(ln)
        out_specs = [row, row, vec, vec]
        out_shape = [jax.ShapeDtypeStruct((t, D_MODEL), F32), jax.ShapeDtypeStruct((t, D_MODEL), BF16),
                     jax.ShapeDtypeStruct((1, D_MODEL), F32), jax.ShapeDtypeStruct((1, D_MODEL), F32)]
    else:
        out_specs = [row]
        out_shape = [jax.ShapeDtypeStruct((t, D_MODEL), out_dtype)]
    res = pl.pallas_call(
        body, grid=(t // tm, nkt), in_specs=in_specs, out_specs=out_specs, out_shape=out_shape,
        scratch_shapes=[pltpu.VMEM((tm, D_MODEL), F32)],
        compiler_params=_cp("arbitrary", "arbitrary"), name=name)(*args)
    return res if ln is not None else res[0]


def _rope_tables(t):
    half = ROPE_DIM // 2
    inv = ROPE_THETA ** (-jnp.arange(0, ROPE_DIM, 2, dtype=F32) / ROPE_DIM)
    ang = jnp.arange(t, dtype=F32)[:, None] * inv[None, :]
    cos, sin = jnp.cos(ang), jnp.sin(ang)
    pad = HEAD_DIM - ROPE_DIM
    c = jnp.concatenate([cos, cos, jnp.ones((t, pad), F32)], axis=1)
    s_hi = jnp.concatenate([jnp.zeros((t, half), F32), sin, jnp.zeros((t, pad), F32)], axis=1)
    s_lo = jnp.concatenate([-sin, jnp.zeros((t, half + pad), F32)], axis=1)
    rep = LANES // HEAD_DIM
    return jnp.tile(c, (1, rep)), jnp.tile(s_hi, (1, rep)), jnp.tile(s_lo, (1, rep))


def _rope(x, c, s_hi, s_lo, sign):
    half = ROPE_DIM // 2
    parts = []
    for j in range(x.shape[1] // LANES):
        xg = x[:, j * LANES:(j + 1) * LANES]
        rot = pltpu.roll(xg, half, 1) * s_hi + pltpu.roll(xg, LANES - half, 1) * s_lo
        parts.append(xg * c + sign * rot)
    return jnp.concatenate(parts, axis=1)


def _qkv_rope(x_bf, w, tables, *, name):
    t = x_bf.shape[0]
    tm = _rows(t)
    n = Q_DIM + 2 * KV_DIM

    def body(x_ref, w_ref, c_ref, sh_ref, sl_ref, q_ref, k_ref, v_ref):
        acc = _dot(x_ref[...], w_ref[...], NN)
        c, sh, sl = c_ref[...], sh_ref[...], sl_ref[...]
        q_ref[...] = (_rope(acc[:, :Q_DIM], c, sh, sl, 1.0) * HEAD_DIM ** -0.5).astype(BF16)
        k_ref[...] = _rope(acc[:, Q_DIM:Q_DIM + KV_DIM], c, sh, sl, 1.0).astype(BF16)
        v_ref[...] = acc[:, Q_DIM + KV_DIM:].astype(BF16)

    tab = pl.BlockSpec((tm, LANES), lambda i: (i, 0))
    return pl.pallas_call(
        body, grid=(t // tm,),
        in_specs=[pl.BlockSpec((tm, D_MODEL), lambda i: (i, 0)), _full((D_MODEL, n)), tab, tab, tab],
        out_specs=[pl.BlockSpec((tm, Q_DIM), lambda i: (i, 0)), pl.BlockSpec((tm, KV_DIM), lambda i: (i, 0)),
                   pl.BlockSpec((tm, KV_DIM), lambda i: (i, 0))],
        out_shape=[jax.ShapeDtypeStruct((t, Q_DIM), BF16), jax.ShapeDtypeStruct((t, KV_DIM), BF16),
                   jax.ShapeDtypeStruct((t, KV_DIM), BF16)],
        compiler_params=_cp("parallel"), name=name)(x_bf, w, *tables)


ATT_ROWS = 256


def _window_specs(t, tq):
    r = tq // WINDOW
    last = t // WINDOW - 1
    return [pl.BlockSpec((WINDOW, KV_DIM), lambda i: (jnp.maximum(i * r - 1, 0), 0)),
            pl.BlockSpec((tq, KV_DIM), lambda i: (i, 0)),
            pl.BlockSpec((WINDOW, KV_DIM), lambda i: (jnp.minimum((i + 1) * r, last), 0))]


def _att_valid(base, t):
    rows = GROUP * WINDOW
    qpos = base + (lax.broadcasted_iota(jnp.int32, (rows, 3 * WINDOW), 0) & (WINDOW - 1))
    kpos = base - WINDOW + lax.broadcasted_iota(jnp.int32, (rows, 3 * WINDOW), 1)
    return (jnp.abs(qpos - kpos) <= WINDOW) & (kpos >= 0) & (kpos < t)


def _stack_heads(x, r0, g):
    return jnp.concatenate(
        [x[r0:r0 + WINDOW, (GROUP * g + h) * HEAD_DIM:(GROUP * g + h + 1) * HEAD_DIM] for h in range(GROUP)], axis=0)


def _sink_column(sink_ref, g):
    return jnp.concatenate([jnp.full((WINDOW, 1), sink_ref[GROUP * g + h], F32) for h in range(GROUP)], axis=0)


def _unstack_heads(pieces):
    return jnp.concatenate([pieces[g][h * WINDOW:(h + 1) * WINDOW] for g in range(N_KV_HEADS) for h in range(GROUP)], axis=1)


def _attn_fwd(q, k, v, sink, *, name):
    t = q.shape[0]
    tq = min(ATT_ROWS, t)
    nsb = tq // WINDOW

    def body(sink_ref, q_ref, kp_ref, kc_ref, kn_ref, vp_ref, vc_ref, vn_ref, o_ref, l_ref):
        i = pl.program_id(0)
        qv = q_ref[...]
        kw = jnp.concatenate([kp_ref[...], kc_ref[...], kn_ref[...]], axis=0)
        vw = jnp.concatenate([vp_ref[...], vc_ref[...], vn_ref[...]], axis=0)
        for sb in range(nsb):
            r0 = sb * WINDOW
            valid = _att_valid(i * tq + r0, t)
            pieces = []
            for g in range(N_KV_HEADS):
                qs = _stack_heads(qv, r0, g)
                kg = kw[r0:r0 + 3 * WINDOW, g * HEAD_DIM:(g + 1) * HEAD_DIM]
                vg = vw[r0:r0 + 3 * WINDOW, g * HEAD_DIM:(g + 1) * HEAD_DIM]
                s = jnp.where(valid, _dot(qs, kg, NT), -1e30)
                snk = _sink_column(sink_ref, g)
                m = jnp.maximum(jnp.max(s, axis=1, keepdims=True), snk)
                pr = jnp.exp(s - m)
                den = jnp.sum(pr, axis=1, keepdims=True) + jnp.exp(snk - m)
                pieces.append(_dot((pr * (1.0 / den)).astype(BF16), vg, NN))
                l_ref[g, sb] = m + jnp.log(den)
            o_ref[r0:r0 + WINDOW, :] = _unstack_heads(pieces).astype(BF16)

    return pl.pallas_call(
        body, grid=(t // tq,),
        in_specs=[pl.BlockSpec(memory_space=pltpu.SMEM), pl.BlockSpec((tq, Q_DIM), lambda i: (i, 0))]
        + _window_specs(t, tq) + _window_specs(t, tq),
        out_specs=[pl.BlockSpec((tq, Q_DIM), lambda i: (i, 0)),
                   pl.BlockSpec((N_KV_HEADS, nsb, GROUP * WINDOW, 1), lambda i: (0, i, 0, 0))],
        out_shape=[jax.ShapeDtypeStruct((t, Q_DIM), BF16),
                   jax.ShapeDtypeStruct((N_KV_HEADS, t // WINDOW, GROUP * WINDOW, 1), F32)],
        compiler_params=_cp("parallel"), name=name)(sink, q, k, k, k, v, v, v)


def _attn_bwd(q, k, v, o, do, lse, sink, *, name):
    t = q.shape[0]
    tq = min(ATT_ROWS, t)
    nsb = tq // WINDOW

    def body(sink_ref, q_ref, o_ref, do_ref, l_ref, kp_ref, kc_ref, kn_ref, vp_ref, vc_ref, vn_ref,
             dq_ref, dkw_ref, dvw_ref, dsink_ref):
        i = pl.program_id(0)
        qv, ov, dov = q_ref[...], o_ref[...], do_ref[...]
        kw = jnp.concatenate([kp_ref[...], kc_ref[...], kn_ref[...]], axis=0)
        vw = jnp.concatenate([vp_ref[...], vc_ref[...], vn_ref[...]], axis=0)
        lane16 = lax.broadcasted_iota(jnp.int32, (1, N_Q_HEADS), 1)
        dsink = jnp.zeros((1, N_Q_HEADS), F32)
        for sb in range(nsb):
            r0 = sb * WINDOW
            valid = _att_valid(i * tq + r0, t)
            dq_p, dk_p, dv_p = [], [], []
            for g in range(N_KV_HEADS):
                qs, dos = _stack_heads(qv, r0, g), _stack_heads(dov, r0, g)
                delta = jnp.sum(dos.astype(F32) * _stack_heads(ov, r0, g).astype(F32), axis=1, keepdims=True)
                kg = kw[r0:r0 + 3 * WINDOW, g * HEAD_DIM:(g + 1) * HEAD_DIM]
                vg = vw[r0:r0 + 3 * WINDOW, g * HEAD_DIM:(g + 1) * HEAD_DIM]
                lse_col = l_ref[g, sb]
                pr = jnp.where(valid, jnp.exp(_dot(qs, kg, NT) - lse_col), 0.0)
                ds = (pr * (_dot(dos, vg, NT) - delta)).astype(BF16)
                dq_p.append(_dot(ds, kg, NN))
                dk_p.append(_dot(ds, qs, TN))
                dv_p.append(_dot(pr.astype(BF16), dos, TN))
                ps = jnp.exp(_sink_column(sink_ref, g) - lse_col) * delta
                for h in range(GROUP):
                    tot = jnp.sum(ps[h * WINDOW:(h + 1) * WINDOW], axis=0, keepdims=True)
                    dsink = dsink - jnp.where(lane16 == GROUP * g + h, tot, 0.0)
            dq_ref[r0:r0 + WINDOW, :] = _unstack_heads(dq_p)
            dkw_ref[sb] = jnp.concatenate(dk_p, axis=1)
            dvw_ref[sb] = jnp.concatenate(dv_p, axis=1)

        @pl.when(i == 0)
        def _():
            dsink_ref[...] = jnp.zeros_like(dsink_ref)
        dsink_ref[...] += dsink

    rowq = pl.BlockSpec((tq, Q_DIM), lambda i: (i, 0))
    win = pl.BlockSpec((nsb, 3 * WINDOW, KV_DIM), lambda i: (i, 0, 0))
    wshape = jax.ShapeDtypeStruct((t // WINDOW, 3 * WINDOW, KV_DIM), F32)
    return pl.pallas_call(
        body, grid=(t // tq,),
        in_specs=[pl.BlockSpec(memory_space=pltpu.SMEM), rowq, rowq, rowq,
                  pl.BlockSpec((N_KV_HEADS, nsb, GROUP * WINDOW, 1), lambda i: (0, i, 0, 0))]
        + _window_specs(t, tq) + _window_specs(t, tq),
        out_specs=[rowq, win, win, _full((1, N_Q_HEADS))],
        out_shape=[jax.ShapeDtypeStruct((t, Q_DIM), F32), wshape, wshape, jax.ShapeDtypeStruct((1, N_Q_HEADS), F32)],
        compiler_params=_cp("arbitrary"), name=name)(sink, q, o, do, lse, k, k, k, v, v, v)


def _attn_post_bwd(dq, dkw, dvw, tables, *, name):
    t = dq.shape[0]
    nb = t // WINDOW
    n = Q_DIM + 2 * KV_DIM

    def body(dq_ref, ka_ref, kb_ref, kc_ref, va_ref, vb_ref, vc_ref, c_ref, sh_ref, sl_ref, o_ref):
        j = pl.program_id(0)
        lo = (j > 0).astype(F32)
        hi = (j < nb - 1).astype(F32)
        c, sh, sl = c_ref[...], sh_ref[...], sl_ref[...]
        dk = ka_ref[...] * hi + kb_ref[...] + kc_ref[...] * lo
        dv = va_ref[...] * hi + vb_ref[...] + vc_ref[...] * lo
        o_ref[:, :Q_DIM] = (_rope(dq_ref[...], c, sh, sl, -1.0) * HEAD_DIM ** -0.5).astype(BF16)
        o_ref[:, Q_DIM:Q_DIM + KV_DIM] = _rope(dk, c, sh, sl, -1.0).astype(BF16)
        o_ref[:, Q_DIM + KV_DIM:] = dv.astype(BF16)

    wins = [pl.BlockSpec((None, WINDOW, KV_DIM), lambda j: (jnp.minimum(j + 1, nb - 1), 0, 0)),
            pl.BlockSpec((None, WINDOW, KV_DIM), lambda j: (j, 1, 0)),
            pl.BlockSpec((None, WINDOW, KV_DIM), lambda j: (jnp.maximum(j - 1, 0), 2, 0))]
    tab = pl.BlockSpec((WINDOW, LANES), lambda j: (j, 0))
    return pl.pallas_call(
        body, grid=(nb,),
        in_specs=[pl.BlockSpec((WINDOW, Q_DIM), lambda j: (j, 0))] + wins + wins + [tab, tab, tab],
        out_specs=pl.BlockSpec((WINDOW, n), lambda j: (j, 0)),
        out_shape=jax.ShapeDtypeStruct((t, n), BF16),
        compiler_params=_cp("parallel"), name=name)(dq, dkw, dkw, dkw, dvw, dvw, dvw, *tables)


HGRN_ROWS = 512


def _cum_mask(rev, transpose=False):
    row = lax.broadcasted_iota(jnp.int32, (HGRN_CHUNK, HGRN_CHUNK), 0)
    col = lax.broadcasted_iota(jnp.int32, (HGRN_CHUNK, HGRN_CHUNK), 1)
    return (row <= col) if rev != transpose else (row >= col)


def _gates(zq, zf, lb):
    q = zq * _sigmoid(zq)
    sig = _sigmoid(zf)
    f = lb + (1.0 - lb) * sig
    k = (1.0 - lb) * (1.0 - sig)
    return q, sig, f, k


def _intra_exact(q, k, b, mask):
    rows = lax.broadcasted_iota(jnp.int32, (HGRN_CHUNK, 1), 0)
    cols = lax.broadcasted_iota(jnp.int32, (HGRN_CHUNK, HGRN_CHUNK), 1)

    def it(s, a):
        pick = rows == s
        b_s = jnp.sum(jnp.where(pick, b, 0.0), axis=0, keepdims=True)
        k_s = jnp.sum(jnp.where(pick, k, 0.0), axis=0, keepdims=True)
        col = jnp.sum(q * jnp.exp(jnp.minimum(b - b_s, 0.0)) * k_s, axis=1, keepdims=True)
        return jnp.where(cols == s, col, a)

    a = lax.fori_loop(0, HGRN_CHUNK, it, jnp.zeros((HGRN_CHUNK, HGRN_CHUNK), F32))
    return jnp.where(mask, a, 0.0)


def _intra_exact_bwd(q, k, b, da):
    rows = lax.broadcasted_iota(jnp.int32, (HGRN_CHUNK, 1), 0)
    cols = lax.broadcasted_iota(jnp.int32, (HGRN_CHUNK, HGRN_CHUNK), 1)

    def it(s, carry):
        dq, dk = carry
        pick = rows == s
        b_s = jnp.sum(jnp.where(pick, b, 0.0), axis=0, keepdims=True)
        k_s = jnp.sum(jnp.where(pick, k, 0.0), axis=0, keepdims=True)
        w = jnp.sum(jnp.where(cols == s, da, 0.0), axis=1, keepdims=True) * jnp.exp(jnp.minimum(b - b_s, 0.0))
        dq = dq + w * k_s
        dk = jnp.where(pick, jnp.sum(w * q, axis=0, keepdims=True), dk)
        return dq, dk

    z = jnp.zeros((HGRN_CHUNK, HGRN_KEY), F32)
    return lax.fori_loop(0, HGRN_CHUNK, it, (z, z))


def _hgrn_scan_fwd(z, lb, rev, *, name):
    t = z.shape[0]
    rb = min(HGRN_ROWS, t)
    nb, nch = t // rb, rb // HGRN_CHUNK
    fcol = HGRN_HEADS * (2 if rev else 1)

    def blk(n):
        return nb - 1 - n if rev else n

    def body(zq_ref, zf_ref, zv_ref, lb_ref, o_ref, s_ref, st_ref):
        @pl.when(pl.program_id(1) == 0)
        def _():
            st_ref[...] = jnp.zeros_like(st_ref)
        lbv = lb_ref[...]
        cmask = _cum_mask(rev)
        cum = cmask.astype(F32)

        def chunk(c, carry):
            cc = nch - 1 - c if rev else c
            r0 = pl.multiple_of(cc * HGRN_CHUNK, HGRN_CHUNK)
            sl = pl.ds(r0, HGRN_CHUNK)
            q, _, f, k = _gates(zq_ref[sl, :], zf_ref[sl, :], lbv)
            v = zv_ref[sl, :].astype(BF16)
            g = jnp.log(f)
            b = jnp.dot(cum, g, precision=lax.Precision.HIGHEST, preferred_element_type=F32)
            tot = jnp.sum(g, axis=0, keepdims=True)
            qd = q * jnp.exp(b)
            st = st_ref[...]
            st_bf = st.astype(BF16)
            s_ref[cc] = st_bf
            a = lax.cond(
                jnp.min(tot) >= FACTORED_DECAY_LIMIT,
                lambda: jnp.where(cmask, _dot(qd.astype(BF16), (k * jnp.exp(-b)).astype(BF16), NT), 0.0),
                lambda: _intra_exact(q, k, b, cmask))
            o_ref[sl, :] = _dot(qd.astype(BF16), st_bf, NT) + _dot(a.astype(BF16), v, NN)
            kd = (k * jnp.exp(tot - b)).astype(BF16)
            st_ref[...] = st * jnp.exp(tot) + _dot(v, kd, TN)
            return carry

        lax.fori_loop(0, nch, chunk, 0)

    def col(off):
        return pl.BlockSpec((rb, LANES), lambda h, n: (blk(n), off + h))

    return pl.pallas_call(
        body, grid=(HGRN_HEADS, nb),
        in_specs=[col(0), col(fcol), col(3 * HGRN_HEADS), pl.BlockSpec((None, 1, LANES), lambda h, n: (h, 0, 0))],
        out_specs=[pl.BlockSpec((rb, LANES), lambda h, n: (blk(n), h)),
                   pl.BlockSpec((None, nch, LANES, LANES), lambda h, n: (h, blk(n), 0, 0))],
        out_shape=[jax.ShapeDtypeStruct((t, D_MODEL), F32),
                   jax.ShapeDtypeStruct((HGRN_HEADS, t // HGRN_CHUNK, LANES, LANES), BF16)],
        scratch_shapes=[pltpu.VMEM((LANES, LANES), F32)],
        compiler_params=_cp("parallel", "arbitrary"), name=name)(z, z, z, lb)


def _hgrn_scan_bwd(z, lb, d_o, states, rev, *, name):
    t = z.shape[0]
    rb = min(HGRN_ROWS, t)
    nb, nch = t // rb, rb // HGRN_CHUNK
    fcol = HGRN_HEADS * (2 if rev else 1)

    def blk(n):
        return n if rev else nb - 1 - n

    def body(zq_ref, zf_ref, zv_ref, lb_ref, do_ref, s_ref, dq_ref, dzf_ref, dv_ref, dlb_ref, dst_ref):
        @pl.when(pl.program_id(1) == 0)
        def _():
            dst_ref[...] = jnp.zeros_like(dst_ref)
            dlb_ref[...] = jnp.zeros_like(dlb_ref)
        lbv = lb_ref[...]
        cmask = _cum_mask(rev)
        cum = cmask.astype(F32)
        cum_t = _cum_mask(rev, transpose=True).astype(F32)

        def chunk(c, carry):
            cc = c if rev else nch - 1 - c
            r0 = pl.multiple_of(cc * HGRN_CHUNK, HGRN_CHUNK)
            sl = pl.ds(r0, HGRN_CHUNK)
            zq = zq_ref[sl, :]
            q, sig, f, k = _gates(zq, zf_ref[sl, :], lbv)
            v = zv_ref[sl, :].astype(BF16)
            g = jnp.log(f)
            b = jnp.dot(cum, g, precision=lax.Precision.HIGHEST, preferred_element_type=F32)
            tot = jnp.sum(g, axis=0, keepdims=True)
            eb = jnp.exp(b)
            qd = (q * eb).astype(BF16)
            kd = (k * jnp.exp(tot - b)).astype(BF16)
            do = do_ref[sl, :].astype(BF16)
            st0 = s_ref[cc]
            dst = dst_ref[...]
            dst_bf = dst.astype(BF16)
            da = jnp.where(cmask, _dot(do, v, NT), 0.0)
            factored = jnp.min(tot) >= FACTORED_DECAY_LIMIT

            def fast():
                ke = jnp.exp(-b)
                kf = (k * ke).astype(BF16)
                a = jnp.where(cmask, _dot(qd, kf, NT), 0.0)
                da_bf = da.astype(BF16)
                dqf, dkf = _dot(da_bf, kf, NN), _dot(da_bf, qd, TN)
                return a, dqf * eb, dkf * ke, qd.astype(F32) * dqf - kf.astype(F32) * dkf

            def slow():
                dq_i, dk_i = _intra_exact_bwd(q, k, b, da)
                return _intra_exact(q, k, b, cmask), dq_i, dk_i, q * dq_i - k * dk_i

            a, dq_i, dk_i, db_i = lax.cond(factored, fast, slow)
            dv_ref[sl, :] = _dot(a.astype(BF16), do, TN) + _dot(kd, dst_bf, NT)
            dq_x = _dot(do, st0, NN) * eb
            dq = dq_i + dq_x
            dk_x = _dot(v, dst_bf, NN) * jnp.exp(tot - b)
            dk = dk_i + dk_x
            etot = jnp.exp(tot)
            dtot = jnp.sum(k * dk_x, axis=0, keepdims=True) + etot * jnp.sum(dst * st0.astype(F32), axis=0, keepdims=True)
            dst_ref[...] = dst * etot + _dot(do, qd, TN)
            db = db_i + q * dq_x - k * dk_x
            dg =jnp.dot(cum_t, db, precision=lax.Precision.HIGHEST, preferred_element_type=F32) + dtot
            sq = _sigmoid(zq)
            dq_ref[sl, :] = dq * sq * (1.0 + zq * (1.0 - sq))
            dfk = dg / f - dk
            dzf_ref[sl, :] = (dfk * (1.0 - lbv) * sig * (1.0 - sig)).astype(BF16)
            dlb_ref[...] += jnp.sum(dfk * (1.0 - sig), axis=0, keepdims=True)
            return carry

        lax.fori_loop(0, nch, chunk, 0)

    def col(off):
        return pl.BlockSpec((rb, LANES), lambda h, n: (blk(n), off + h))

    out_col = pl.BlockSpec((rb, LANES), lambda h, n: (blk(n), h))
    return pl.pallas_call(
        body, grid=(HGRN_HEADS, nb),
        in_specs=[col(0), col(fcol), col(3 * HGRN_HEADS), pl.BlockSpec((None, 1, LANES), lambda h, n: (h, 0, 0)),
                  out_col, pl.BlockSpec((None, nch, LANES, LANES), lambda h, n: (h, blk(n), 0, 0))],
        out_specs=[out_col, out_col, out_col, pl.BlockSpec((None, 1, LANES), lambda h, n: (h, 0, 0))],
        out_shape=[jax.ShapeDtypeStruct((t, D_MODEL), F32), jax.ShapeDtypeStruct((t, D_MODEL), BF16),
                   jax.ShapeDtypeStruct((t, D_MODEL), F32), jax.ShapeDtypeStruct((HGRN_HEADS, 1, LANES), F32)],
        scratch_shapes=[pltpu.VMEM((LANES, LANES), F32)],
        compiler_params=_cp("parallel", "arbitrary"), name=name)(z, z, z, lb, d_o, states)


def _hgrn_post_fwd(o_f, o_b, z, norm_g, *, name):
    t = o_f.shape[0]
    tm = _rows(t)

    def body(of_ref, ob_ref, gate_ref, ng_ref, y_ref):
        ng = ng_ref[...]
        for h in range(HGRN_HEADS):
            sl = slice(h * LANES, (h + 1) * LANES)
            o = of_ref[:, sl] + ob_ref[:, sl]
            r = lax.rsqrt(jnp.mean(o * o, axis=1, keepdims=True) + LN_EPS)
            gt = gate_ref[:, sl]
            y_ref[:, sl] = (o * r * ng * (gt * _sigmoid(gt))).astype(BF16)

    row = pl.BlockSpec((tm, D_MODEL), lambda i: (i, 0))
    return pl.pallas_call(
        body, grid=(t // tm,),
        in_specs=[row, row, pl.BlockSpec((tm, D_MODEL), lambda i: (i, 4)), _full((1, LANES))],
        out_specs=row, out_shape=jax.ShapeDtypeStruct((t, D_MODEL), BF16),
        compiler_params=_cp("parallel"), name=name)(o_f, o_b, z, norm_g)


def _hgrn_post_bwd(dy, o_f, o_b, z, norm_g, *, name):
    t = o_f.shape[0]
    tm = _rows(t)

    def body(dy_ref, of_ref, ob_ref, gate_ref, ng_ref, do_ref, dgate_ref, dng_ref):
        ng = ng_ref[...]
        dng = jnp.zeros((1, LANES), F32)
        for h in range(HGRN_HEADS):
            sl = slice(h * LANES, (h + 1) * LANES)
            o = of_ref[:, sl] + ob_ref[:, sl]
            r = lax.rsqrt(jnp.mean(o * o, axis=1, keepdims=True) + LN_EPS)
            gt = gate_ref[:, sl]
            sg = _sigmoid(gt)
            dyv = dy_ref[:, sl].astype(F32)
            xn = o * r
            dgate_ref[:, sl] = (dyv * xn * ng * sg * (1.0 + gt * (1.0 - sg))).astype(BF16)
            dn = dyv * gt * sg
            dng = dng + jnp.sum(dn * xn, axis=0, keepdims=True)
            dxn = dn * ng
            do_ref[:, sl] = r * (dxn - xn * jnp.mean(dxn * xn, axis=1, keepdims=True))

        @pl.when(pl.program_id(0) == 0)
        def _():
            dng_ref[...] = jnp.zeros_like(dng_ref)
        dng_ref[...] += dng

    row = pl.BlockSpec((tm, D_MODEL), lambda i: (i, 0))
    return pl.pallas_call(
        body, grid=(t // tm,),
        in_specs=[row, row, row, pl.BlockSpec((tm, D_MODEL), lambda i: (i, 4)), _full((1, LANES))],
        out_specs=[row, row, _full((1, LANES))],
        out_shape=[jax.ShapeDtypeStruct((t, D_MODEL), F32), jax.ShapeDtypeStruct((t, D_MODEL), BF16),
                   jax.ShapeDtypeStruct((1, LANES), F32)],
        compiler_params=_cp("arbitrary"), name=name)(dy, o_f, o_b, z, norm_g)


def _hgrn_combine(dq_f, dq_b, dzf_f, dzf_b, dv_f, dv_b, dgate, *, name):
    t = dq_f.shape[0]
    tm = _rows(t)

    def body(qf, qb, ff, fb, vf, vb, gt, o_ref):
        o_ref[:, 0 * D_MODEL:1 * D_MODEL] = (qf[...] + qb[...]).astype(BF16)
        o_ref[:, 1 * D_MODEL:2 * D_MODEL] = ff[...]
        o_ref[:, 2 * D_MODEL:3 * D_MODEL] = fb[...]
        o_ref[:, 3 * D_MODEL:4 * D_MODEL] = (vf[...] + vb[...]).astype(BF16)
        o_ref[:, 4 * D_MODEL:5 * D_MODEL] = gt[...]

    row = pl.BlockSpec((tm, D_MODEL), lambda i: (i, 0))
    return pl.pallas_call(
        body, grid=(t // tm,), in_specs=[row] * 7,
        out_specs=pl.BlockSpec((tm, 5 * D_MODEL), lambda i: (i, 0)),
        out_shape=jax.ShapeDtypeStruct((t, 5 * D_MODEL), BF16),
        compiler_params=_cp("parallel"), name=name)(dq_f, dq_b, dzf_f, dzf_b, dv_f, dv_b, dgate)


def _lower_bounds(logits2d, *, name):
    def body(l_ref, lb_ref):
        l = l_ref[...]
        e = jnp.exp(l - jnp.max(l, axis=0, keepdims=True))
        sm = e / jnp.sum(e, axis=0, keepdims=True)
        s1, s2, s3 = sm[1:2], sm[2:3], sm[3:4]
        lb_ref[...] = jnp.concatenate([jnp.zeros_like(s1), s1, s1 + s2, s1 + s2 + s3], axis=0)

    return pl.pallas_call(body, out_shape=jax.ShapeDtypeStruct(logits2d.shape, F32), name=name)(logits2d)


def _lower_bounds_bwd(logits2d, dlb, *, name):
    def body(l_ref, d_ref, o_ref):
        l = l_ref[...]
        e = jnp.exp(l - jnp.max(l, axis=0, keepdims=True))
        sm = e / jnp.sum(e, axis=0, keepdims=True)
        d = d_ref[...]
        d1, d2, d3 = d[1:2], d[2:3], d[3:4]
        dsm = jnp.concatenate([jnp.zeros_like(d1), d1 + d2 + d3, d2 + d3, d3], axis=0)
        o_ref[...] = sm * (dsm - jnp.sum(sm * dsm, axis=0, keepdims=True))

    return pl.pallas_call(body, out_shape=jax.ShapeDtypeStruct(logits2d.shape, F32), name=name)(logits2d, dlb)


def _adamw(w, g, m, v, *, name):
    r, c = w.shape
    tr = r if r <= 512 else _pick_rows(r)

    def body(w_ref, g_ref, m_ref, v_ref, d_ref, nm_ref, nv_ref):
        gv = g_ref[...]
        nm = ADAM_B1 * m_ref[...] + (1.0 - ADAM_B1) * gv
        nv = ADAM_B2 * v_ref[...] + (1.0 - ADAM_B2) * (gv * gv)
        m_hat = nm / (1.0 - ADAM_B1 ** ADAM_STEP)
        v_hat = nv / (1.0 - ADAM_B2 ** ADAM_STEP)
        d_ref[...] = -ADAM_LR * (m_hat / (jnp.sqrt(v_hat) + ADAM_EPS) + ADAM_WD * w_ref[...])
        nm_ref[...] = nm
        nv_ref[...] = nv

    blk = pl.BlockSpec((tr, c), lambda i: (i, 0))
    shp = jax.ShapeDtypeStruct((r, c), F32)
    return pl.pallas_call(body, grid=(r // tr,), in_specs=[blk] * 4, out_specs=[blk] * 3, out_shape=[shp] * 3,
                          compiler_params=_cp("parallel"), name=name)(w, g, m, v)


def _pick_rows(r, cap=512):
    best = 8
    for d in range(8, cap + 1, 8):
        if r % d == 0:
            best = d
    assert r % best == 0, r
    return best


def _local_step(x, p, target, w, sp):
    t = x.shape[0]
    tables = _rope_tables(t)
    logits2d = sp["hgrn_lb_logits"].reshape(DEPTH, 2 * D_MODEL)
    lb_all = _lower_bounds(logits2d, name="lb_fwd").reshape(DEPTH, 2, HGRN_HEADS, 1, LANES)
    row = lambda a, i: a[i][None]

    saved = []
    xf, xb = x, x.astype(BF16)
    for i in range(DEPTH):
        j = i // 2
        s = dict(xin_bf=xb)
        if i % 2 == 0:
            q, k, v = _qkv_rope(xb, w["att_w_qkv"][j], tables, name=f"qkv_rope_{i}")
            o, lse = _attn_fwd(q, k, v, sp["att_sink"][j], name=f"attn_fwd_{i}")
            s.update(q=q, k=k, v=v, o=o, lse=lse)
            mix_in, w_o = o, w["att_w_o"][j]
        else:
            z = _mm_nn(xb, w["hgrn_w_in"][j], out_dtype=F32, name=f"hgrn_in_{i}")
            o_f, s_f = _hgrn_scan_fwd(z, lb_all[i, 0], False, name=f"hgrn_scan_f_{i}")
            o_b, s_b = _hgrn_scan_fwd(z, lb_all[i, 1], True, name=f"hgrn_scan_b_{i}")
            og = _hgrn_post_fwd(o_f, o_b, z, row(sp["hgrn_norm_g"], j), name=f"hgrn_post_{i}")
            s.update(z=z, o_f=o_f, o_b=o_b, s_f=s_f, s_b=s_b, og=og)
            mix_in, w_o = og, w["hgrn_w_o"][j]
        x1, x1b, xh1, rs1 = _mm_res_ln(mix_in, w_o, xf, row(sp["ln_mix_g"], i), row(sp["ln_mix_b"], i), name=f"mix_out_ln_{i}")
        g, u, h = _ffn_in(x1b, w["ffn_w_in"][i], name=f"ffn_in_{i}")
        x2, x2b, xh2, rs2 = _mm_res_ln(h, w["ffn_w_out"][i], x1, row(sp["ln_ffn_g"], i), row(sp["ln_ffn_b"], i), name=f"ffn_out_ln_{i}")
        xf, xb, gate, pp = _ple_fwd(x2, x2b, p, i, w["ple_w_gate"][i], w["ple_w_proj"][i], name=f"ple_fwd_{i}")
        s.update(x1b=x1b, xh1=xh1, rs1=rs1, g=g, u=u, h=h, x2b=x2b, xh2=xh2, rs2=rs2, gate=gate, pp=pp)
        saved.append(s)

    loss, dx = _loss_head(xf, target, name="loss_head")

    gw = {n: [None] * w[n].shape[0] for n in w}
    gs = {n: [None] * DEPTH for n in ("ln_mix_g", "ln_mix_b", "ln_ffn_g", "ln_ffn_b")}
    gs.update(att_sink=[None] * 2, hgrn_norm_g=[None] * 2)
    dlb = [jnp.zeros((2, D_MODEL), F32)] * DEPTH
    for i in reversed(range(DEPTH)):
        j = i // 2
        s = saved[i]
        du2, du2b, dpre, dpp, gs["ln_ffn_g"][i], gs["ln_ffn_b"][i] = _ple_bwd(
            dx, s["gate"], s["pp"], w["ple_w_gate"][i], s["xh2"], s["rs2"], row(sp["ln_ffn_g"], i), name=f"ple_bwd_{i}")
        gw["ple_w_gate"][i] = _mm_tn(s["x2b"], dpre, name=f"dw_ple_gate_{i}")
        gw["ple_w_proj"][i] = _mm_tn_p(p, i, dpp, name=f"dw_ple_proj_{i}")
        dgg, dgu = _ffn_out_bwd(du2b, w["ffn_w_out"][i], s["g"], s["u"], name=f"ffn_out_bwd_{i}")
        gw["ffn_w_out"][i] = _mm_tn(s["h"], du2b, name=f"dw_ffn_out_{i}")
        du1, du1b, gs["ln_mix_g"][i], gs["ln_mix_b"][i] = _mm_nt(
            [dgg, dgu], w["ffn_w_in"][i], tk=_pick(D_FF, 1408), resid=du2,
            ln=(s["xh1"], s["rs1"], row(sp["ln_mix_g"], i)), name=f"ffn_in_bwd_{i}")
        gw["ffn_w_in"][i] = jnp.concatenate(
            [_mm_tn(s["x1b"], dgg, name=f"dw_ffn_gate_{i}"), _mm_tn(s["x1b"], dgu, name=f"dw_ffn_up_{i}")], axis=1)
        if i % 2 == 0:
            gw["att_w_o"][j] = _mm_tn(s["o"], du1b, name=f"dw_att_o_{i}")
            do = _mm_nt([du1b], w["att_w_o"][j], tk=Q_DIM, out_dtype=BF16, name=f"att_o_bwd_{i}")
            dq, dkw, dvw, gs["att_sink"][j] = _attn_bwd(
                s["q"], s["k"], s["v"], s["o"], do, s["lse"], sp["att_sink"][j], name=f"attn_bwd_{i}")
            dqkv = _attn_post_bwd(dq, dkw, dvw, tables, name=f"attn_post_bwd_{i}")
            gw["att_w_qkv"][j] = _mm_tn(s["xin_bf"], dqkv, name=f"dw_att_qkv_{i}")
            dx = _mm_nt([dqkv], w["att_w_qkv"][j], tk=Q_DIM + 2 * KV_DIM, resid=du1, name=f"qkv_bwd_{i}")
        else:
            gw["hgrn_w_o"][j] = _mm_tn(s["og"], du1b, name=f"dw_hgrn_o_{i}")
            dy = _mm_nt([du1b], w["hgrn_w_o"][j], tk=D_MODEL, out_dtype=BF16, name=f"hgrn_o_bwd_{i}")
            d_o, dgate, gs["hgrn_norm_g"][j] = _hgrn_post_bwd(
                dy, s["o_f"], s["o_b"], s["z"], row(sp["hgrn_norm_g"], j), name=f"hgrn_post_bwd_{i}")
            dq_f, dzf_f, dv_f, dlb_f = _hgrn_scan_bwd(s["z"], lb_all[i, 0], d_o, s["s_f"], False, name=f"hgrn_scan_f_bwd_{i}")
            dq_b, dzf_b, dv_b, dlb_b = _hgrn_scan_bwd(s["z"], lb_all[i, 1], d_o, s["s_b"], True, name=f"hgrn_scan_b_bwd_{i}")
            dlb[i] = jnp.stack([dlb_f.reshape(D_MODEL), dlb_b.reshape(D_MODEL)])
            dz = _hgrn_combine(dq_f, dq_b, dzf_f, dzf_b, dv_f, dv_b, dgate, name=f"hgrn_combine_{i}")
            gw["hgrn_w_in"][j] = _mm_tn(s["xin_bf"], dz, name=f"dw_hgrn_in_{i}")
            dx = _mm_nt([dz], w["hgrn_w_in"][j], tk=_pick(5 * D_MODEL, 1408), resid=du1, name=f"hgrn_in_bwd_{i}")

    gw = {n: jnp.stack(v) for n, v in gw.items()}
    gsmall = {n: jnp.concatenate(v, axis=0) for n, v in gs.items()}
    gsmall["hgrn_lb_logits"] = _lower_bounds_bwd(
        logits2d, jnp.stack(dlb).reshape(DEPTH, 2 * D_MODEL), name="lb_bwd").reshape(DEPTH, 2, D_MODEL)
    return loss, dx, gw, gsmall


ANY = pl.BlockSpec(memory_space=pl.ANY)


def _place():
    x, y, c = lax.axis_index("x"), lax.axis_index("y"), lax.axis_index("c")
    chips = [(1 - x, y), (x, 1 - y), (1 - x, 1 - y)]
    return x, y, c, chips


def _remote(src, dst, send_sem, recv_sem, to):
    return pltpu.make_async_remote_copy(src_ref=src, dst_ref=dst, send_sem=send_sem, recv_sem=recv_sem,
                                        device_id=to, device_id_type=MESH)


def _allgather_weights(packed, *, name):
    r = packed.shape[0]
    hr = r // 2

    def body(src_ref, out_ref, send_sems, recv_sems, local_sem):
        x, y, c, chips = _place()
        sibling = (x, y, 1 - c)

        def half(cx, cy, hc):
            return out_ref.at[2 * cx + cy, pl.ds(hc * hr, hr), :]

        mine = pltpu.make_async_copy(src_ref, out_ref.at[2 * x + y], local_sem)
        mine.start()
        my_half = src_ref.at[pl.ds(c * hr, hr), :]
        first = [_remote(my_half, half(x, y, c), send_sems.at[j], recv_sems.at[j], (*chip, c)) for j, chip in enumerate(chips)]
        for cp in first:
            cp.start()
        passed = [_remote(half(*chip, c), half(*chip, c), send_sems.at[3 + j], recv_sems.at[3 + j], sibling)
                  for j, chip in enumerate(chips)]
        for j, chip in enumerate(chips):
            _remote(my_half, half(*chip, c), send_sems.at[j], recv_sems.at[j], (*chip, c)).wait_recv()
            passed[j].start()
        for j, chip in enumerate(chips):
            _remote(my_half, half(*chip, 1 - c), send_sems.at[3 + j], recv_sems.at[3 + j], sibling).wait_recv()
        for cp in first + passed:
            cp.wait_send()
        mine.wait()

    return pl.pallas_call(
        body, in_specs=[ANY], out_specs=ANY, out_shape=jax.ShapeDtypeStruct((N_CHIPS, r, packed.shape[1]), packed.dtype),
        scratch_shapes=[pltpu.SemaphoreType.DMA((6,)), pltpu.SemaphoreType.DMA((6,)), pltpu.SemaphoreType.DMA],
        name=name)(packed)


def _allreduce_small(block, *, name):
    m, n = block.shape

    def body(x_ref, sum_ref, all_ref, send_sems, recv_sems):
        x, y, c, chips = _place()
        me, sibling = (x, y, c), (x, y, 1 - c)

        def rows(px, py, pc):
            return all_ref.at[pl.ds((4 * px + 2 * py + pc) * m, m), :]

        all_ref[pl.ds((4 * x + 2 * y + c) * m, m), :] = x_ref[...]
        first = [_remote(x_ref, rows(*me), send_sems.at[0], recv_sems.at[0], sibling)]
        first += [_remote(x_ref, rows(*me), send_sems.at[1 + j], recv_sems.at[1 + j], (*chip, c)) for j, chip in enumerate(chips)]
        for cp in first:
            cp.start()
        passed = [_remote(rows(*chip, c), rows(*chip, c), send_sems.at[4 + j], recv_sems.at[4 + j], sibling)
                  for j, chip in enumerate(chips)]
        for j, chip in enumerate(chips):
            _remote(x_ref, rows(*chip, c), send_sems.at[1 + j], recv_sems.at[1 + j], me).wait_recv()
            passed[j].start()
        _remote(x_ref, rows(*sibling), send_sems.at[0], recv_sems.at[0], me).wait_recv()
        for j, chip in enumerate(chips):
            _remote(x_ref, rows(*chip, 1 - c), send_sems.at[4 + j], recv_sems.at[4 + j], me).wait_recv()
        for cp in first + passed:
            cp.wait_send()
        acc = all_ref[pl.ds(0, m), :]
        for d in range(1, 8):
            acc = acc + all_ref[pl.ds(d * m, m), :]
        sum_ref[...] = acc

    vmem = pl.BlockSpec(memory_space=pltpu.VMEM)
    return pl.pallas_call(
        body, in_specs=[vmem], out_specs=vmem, out_shape=jax.ShapeDtypeStruct((m, n), F32),
        scratch_shapes=[pltpu.VMEM((8 * m, n), F32), pltpu.SemaphoreType.DMA((7,)), pltpu.SemaphoreType.DMA((7,))],
        name=name)(block)


def _pair_exchange(g, *, name):
    n, r, w = g.shape
    hr = r // 2

    def body(g_ref, out_ref, send_sem, recv_sem):
        x, y, c, _ = _place()
        cp = _remote(g_ref.at[:, pl.ds((1 - c) * hr, hr), :], out_ref, send_sem, recv_sem, (x, y, 1 - c))
        cp.start()
        cp.wait()

    return pl.pallas_call(
        body, in_specs=[ANY], out_specs=ANY, out_shape=jax.ShapeDtypeStruct((n, hr, w), g.dtype),
        scratch_shapes=[pltpu.SemaphoreType.DMA, pltpu.SemaphoreType.DMA], name=name)(g)


def _chip_scatter(part, *, name):
    n, h, w = part.shape

    def body(p_ref, out_ref, send_sems, recv_sems, local_sem):
        x, y, c, chips = _place()
        me = 2 * x + y
        mine = pltpu.make_async_copy(p_ref.at[me], out_ref.at[me], local_sem)
        mine.start()
        sends = [_remote(p_ref.at[2 * cx + cy], out_ref.at[me], send_sems.at[j], recv_sems.at[j], (cx, cy, c))
                 for j, (cx, cy) in enumerate(chips)]
        for cp in sends:
            cp.start()
        for j, (cx, cy) in enumerate(chips):
            _remote(p_ref.at[me], out_ref.at[2 * cx + cy], send_sems.at[j], recv_sems.at[j], (cx, cy, c)).wait_recv()
        for cp in sends:
            cp.wait_send()
        mine.wait()

    return pl.pallas_call(
        body, in_specs=[ANY], out_specs=ANY, out_shape=jax.ShapeDtypeStruct((n, h, w), part.dtype),
        scratch_shapes=[pltpu.SemaphoreType.DMA((3,)), pltpu.SemaphoreType.DMA((3,)), pltpu.SemaphoreType.DMA],
        name=name)(part)


def _pair_share(half, *, name):
    h, w = half.shape

    def body(h_ref, out_ref, send_sem, recv_sem, local_sem):
        x, y, c, _ = _place()
        dst = out_ref.at[pl.ds(c * h, h), :]
        mine = pltpu.make_async_copy(h_ref, dst, local_sem)
        mine.start()
        cp = _remote(h_ref, dst, send_sem, recv_sem, (x, y, 1 - c))
        cp.start()
        cp.wait_send()
        _remote(h_ref, out_ref.at[pl.ds((1 - c) * h, h), :], send_sem, recv_sem, (x, y, 1 - c)).wait_recv()
        mine.wait()

    return pl.pallas_call(
        body, in_specs=[ANY], out_specs=ANY, out_shape=jax.ShapeDtypeStruct((2 * h, w), half.dtype),
        scratch_shapes=[pltpu.SemaphoreType.DMA, pltpu.SemaphoreType.DMA, pltpu.SemaphoreType.DMA], name=name)(half)


def _add_own_half(g, recv, c, *, name):
    n, r, w = g.shape
    hr = r // 2
    tr = _pick_rows(hr, 1024)
    nr = hr // tr

    def body(c_ref, g_ref, r_ref, o_ref):
        o_ref[...] = g_ref[...] + r_ref[...]

    return pl.pallas_call(
        body,
        grid_spec=pltpu.PrefetchScalarGridSpec(
            num_scalar_prefetch=1, grid=(n, nr),
            in_specs=[pl.BlockSpec((None, tr, w), lambda s, i, c_ref: (s, c_ref[0] * nr + i, 0)),
                      pl.BlockSpec((None, tr, w), lambda s, i, c_ref: (s, i, 0))],
            out_specs=pl.BlockSpec((None, tr, w), lambda s, i, c_ref: (s, i, 0))),
        out_shape=jax.ShapeDtypeStruct((n, hr, w), F32),
        compiler_params=_cp("parallel", "parallel"), name=name)(c, g, recv)


def _sum_chips(q, *, name):
    n, h, w = q.shape
    tr = _pick_rows(h, 1024)

    def body(a, b, c, d, o_ref):
        o_ref[...] = ((a[...] + b[...]) + c[...]) + d[...]

    specs = [pl.BlockSpec((None, tr, w), functools.partial(lambda i, s: (s, i, 0), s=s)) for s in range(n)]
    return pl.pallas_call(
        body, grid=(h // tr,), in_specs=specs, out_specs=pl.BlockSpec((tr, w), lambda i: (i, 0)),
        out_shape=jax.ShapeDtypeStruct((h, w), F32), compiler_params=_cp("parallel"), name=name)(q, q, q, q)


PACK_W = 1024
BIG = (("att_w_qkv", 2), ("att_w_o", 1), ("hgrn_w_in", 2), ("hgrn_w_o", 1),
       ("ffn_w_in", 2), ("ffn_w_out", 1), ("ple_w_gate", 1), ("ple_w_proj", 2))
LB_ROWS = 32


def _pack_shards(shards):
    return jnp.concatenate([shards[n].reshape(-1, PACK_W) for n, _ in BIG], axis=0)


def _unpack_shards(buf, shapes):
    out, r0 = {}, 0
    for n, _ in BIG:
        nr = shapes[n][0] * shapes[n][1] * shapes[n][2] // PACK_W
        out[n] = buf[r0:r0 + nr].reshape(shapes[n])
        r0 += nr
    return out


def _gathered_to_full(buf, shapes):
    out, r0 = {}, 0
    for n, axis in BIG:
        l, r, c = shapes[n]
        nr = l * r * c // PACK_W
        sh = buf[:, r0:r0 + nr].reshape(N_CHIPS, l, r, c)
        out[n] = (sh.transpose(1, 0, 2, 3).reshape(l, N_CHIPS * r, c) if axis == 1
                  else sh.transpose(1, 2, 0, 3).reshape(l, r, N_CHIPS * c))
        r0 += nr
    return out, r0


def _full_to_slabs(full, shapes):
    parts = []
    for n, axis in BIG:
        l, r, c = shapes[n]
        g = full[n]
        g = (g.reshape(l, N_CHIPS, r, c).transpose(1, 0, 2, 3) if axis == 1
             else g.reshape(l, r, N_CHIPS, c).transpose(2, 0, 1, 3))
        parts.append(g.reshape(N_CHIPS, -1, PACK_W))
    return jnp.concatenate(parts, axis=1)


SMALL = ("ln_mix_g", "ln_mix_b", "ln_ffn_g", "ln_ffn_b")
SMALL_ROWS = 32


def _pack_small(vals, lb):
    rows = [vals[n] for n in SMALL] + [lb.reshape(-1, PACK_W)]
    for n in ("att_sink", "hgrn_norm_g"):
        flat = vals[n].reshape(1, -1)
        rows.append(jnp.pad(flat, ((0, 0), (0, PACK_W - flat.shape[1]))))
    buf = jnp.concatenate(rows, axis=0)
    return jnp.pad(buf, ((0, SMALL_ROWS - buf.shape[0]), (0, 0)))


def _unpack_small(buf, lb_shape):
    out, r0 = {}, 0
    for n in SMALL:
        out[n] = buf[r0:r0 + DEPTH]
        r0 += DEPTH
    nlb = lb_shape[0] * lb_shape[1] * lb_shape[2] // PACK_W
    out["hgrn_lb_logits"] = buf[r0:r0 + nlb].reshape(lb_shape)
    r0 += nlb
    out["att_sink"] = buf[r0, :2 * N_Q_HEADS].reshape(2, N_Q_HEADS)
    out["hgrn_norm_g"] = buf[r0 + 1, :2 * LANES].reshape(2, LANES)
    return out


WEIGHTS = ("att_w_qkv", "att_sink", "att_w_o", "hgrn_w_in", "hgrn_lb_logits", "hgrn_norm_g", "hgrn_w_o", "ln_mix_g",
           "ln_mix_b", "ffn_w_in", "ffn_w_out", "ln_ffn_g", "ln_ffn_b", "ple_w_gate", "ple_w_proj")


def kernel(x, p, att_w_qkv, att_sink, att_w_o, hgrn_w_in, hgrn_lb_logits, hgrn_norm_g, hgrn_w_o, ln_mix_g, ln_mix_b, ffn_w_in, ffn_w_out, ln_ffn_g, ln_ffn_b, ple_w_gate, ple_w_proj, loss_target, m_att_w_qkv, m_att_sink, m_att_w_o, m_hgrn_w_in, m_hgrn_lb_logits, m_hgrn_norm_g, m_hgrn_w_o, m_ln_mix_g, m_ln_mix_b, m_ffn_w_in, m_ffn_w_out, m_ln_ffn_g, m_ln_ffn_b, m_ple_w_gate, m_ple_w_proj, v_att_w_qkv, v_att_sink, v_att_w_o, v_hgrn_w_in, v_hgrn_lb_logits, v_hgrn_norm_g, v_hgrn_w_o, v_ln_mix_g, v_ln_mix_b, v_ffn_w_in, v_ffn_w_out, v_ln_ffn_g, v_ln_ffn_b, v_ple_w_gate, v_ple_w_proj):
    wts = dict(att_w_qkv=att_w_qkv, att_sink=att_sink, att_w_o=att_w_o, hgrn_w_in=hgrn_w_in, hgrn_lb_logits=hgrn_lb_logits,
               hgrn_norm_g=hgrn_norm_g, hgrn_w_o=hgrn_w_o, ln_mix_g=ln_mix_g, ln_mix_b=ln_mix_b, ffn_w_in=ffn_w_in,
               ffn_w_out=ffn_w_out, ln_ffn_g=ln_ffn_g, ln_ffn_b=ln_ffn_b, ple_w_gate=ple_w_gate, ple_w_proj=ple_w_proj)
    mom = dict(att_w_qkv=m_att_w_qkv, att_sink=m_att_sink, att_w_o=m_att_w_o, hgrn_w_in=m_hgrn_w_in, hgrn_lb_logits=m_hgrn_lb_logits,
               hgrn_norm_g=m_hgrn_norm_g, hgrn_w_o=m_hgrn_w_o, ln_mix_g=m_ln_mix_g, ln_mix_b=m_ln_mix_b, ffn_w_in=m_ffn_w_in,
               ffn_w_out=m_ffn_w_out, ln_ffn_g=m_ln_ffn_g, ln_ffn_b=m_ln_ffn_b, ple_w_gate=m_ple_w_gate, ple_w_proj=m_ple_w_proj)
    var = dict(att_w_qkv=v_att_w_qkv, att_sink=v_att_sink, att_w_o=v_att_w_o, hgrn_w_in=v_hgrn_w_in, hgrn_lb_logits=v_hgrn_lb_logits,
               hgrn_norm_g=v_hgrn_norm_g, hgrn_w_o=v_hgrn_w_o, ln_mix_g=v_ln_mix_g, ln_mix_b=v_ln_mix_b, ffn_w_in=v_ffn_w_in,
               ffn_w_out=v_ffn_w_out, ln_ffn_g=v_ln_ffn_g, ln_ffn_b=v_ln_ffn_b, ple_w_gate=v_ple_w_gate, ple_w_proj=v_ple_w_proj)
    shapes = {n: wts[n].shape for n, _ in BIG}
    chip = 2 * lax.axis_index("x") + lax.axis_index("y")
    core = lax.axis_index("c")

    lb_bits = lax.bitcast_convert_type(hgrn_lb_logits.reshape(-1), BF16).reshape(-1, PACK_W)
    packed = jnp.concatenate([_pack_shards({n: wts[n].astype(BF16) for n, _ in BIG}), lb_bits,
                              jnp.zeros((LB_ROWS - lb_bits.shape[0], PACK_W), BF16)], axis=0)
    gathered = _allgather_weights(packed, name="allgather_weights")
    w_full, r_big = _gathered_to_full(gathered, shapes)
    lb_sh = hgrn_lb_logits.shape
    lb_rows = gathered[:, r_big:r_big + lb_bits.shape[0]].reshape(N_CHIPS, -1, 2)
    lb_full = lax.bitcast_convert_type(lb_rows, F32).reshape(N_CHIPS, lb_sh[0], lb_sh[1], lb_sh[2])
    lb_full = lb_full.transpose(1, 2, 0, 3).reshape(lb_sh[0], lb_sh[1], N_CHIPS * lb_sh[2])
    sp = dict(att_sink=att_sink, hgrn_lb_logits=lb_full, hgrn_norm_g=hgrn_norm_g, ln_mix_g=ln_mix_g, ln_mix_b=ln_mix_b,
              ln_ffn_g=ln_ffn_g, ln_ffn_b=ln_ffn_b)

    loss, grad_x, gw, gs = _local_step(x[0], p, loss_target[0], w_full, sp)
    loss = lax.psum(loss[0, 0], ("x", "y", "c"))

    slabs = _full_to_slabs(gw, shapes)
    from_sibling = _pair_exchange(slabs, name="grad_pair_exchange")
    chip_part = _add_own_half(slabs, from_sibling, core.reshape(1).astype(jnp.int32), name="grad_pair_add")
    from_chips = _chip_scatter(chip_part, name="grad_chip_scatter")
    half = _sum_chips(from_chips, name="grad_chip_sum")
    g_big = _unpack_shards(_pair_share(half, name="grad_pair_share"), shapes)

    g_small_full = _unpack_small(_allreduce_small(_pack_small(gs, gs["hgrn_lb_logits"]), name="allreduce_small"),
                                 (lb_sh[0], lb_sh[1], N_CHIPS * lb_sh[2]))
    g_lb = lax.dynamic_slice_in_dim(g_small_full["hgrn_lb_logits"], chip * lb_sh[2], lb_sh[2], axis=2)
    grads = dict(g_big)
    grads.update({n: g_small_full[n] for n in SMALL + ("att_sink", "hgrn_norm_g")})
    grads["hgrn_lb_logits"] = g_lb

    delta, new_m, new_v = {}, {}, {}
    for n, _ in BIG:
        two_d = lambda a: a.reshape(-1, a.shape[-1])
        d, nm, nv = _adamw(two_d(wts[n]), two_d(grads[n]), two_d(mom[n]), two_d(var[n]), name=f"adamw_{n}")
        delta[n], new_m[n], new_v[n] = d.reshape(shapes[n]), nm.reshape(shapes[n]), nv.reshape(shapes[n])
    packs = [_pack_small(src, src["hgrn_lb_logits"]) for src in (wts, grads, mom, var)]
    outs = _adamw(*packs, name="adamw_small")
    for dst, buf in zip((delta, new_m, new_v), outs):
        dst.update(_unpack_small(buf, lb_sh))

    return (loss, grad_x[None], *[grads[n] for n in WEIGHTS], *[delta[n] for n in WEIGHTS],
            *[new_m[n] for n in WEIGHTS], *[new_v[n] for n in WEIGHTS])
```

```python
import functools

import jax
import jax.numpy as jnp
from jax import lax
from jax.experimental import pallas as pl
from jax.experimental.pallas import tpu as pltpu

F32, BF16 = jnp.float32, jnp.bfloat16

D_MODEL = 1024
DEPTH = 4
HEAD_DIM = 64
N_Q_HEADS = 16
N_KV_HEADS = 4
GROUP = 4
Q_DIM = 1024
KV_DIM = 256
WINDOW = 128
ROPE_DIM = 16
ROPE_THETA = 500000.0
HGRN_HEADS = 8
HGRN_KEY = 128
HGRN_CHUNK = 64
D_FF = 2816
PLE_DIM = 256
ALPHA = (2 * DEPTH) ** 0.25
LN_EPS = 1e-5
ADAM_LR, ADAM_B1, ADAM_B2, ADAM_EPS, ADAM_WD, ADAM_STEP = 0.001, 0.9, 0.999, 1e-08, 0.01, 10

LANES = 128
VMEM_LIMIT_BYTES = 56 * 2**20
FACTORED_DECAY_LIMIT = -80.0

NN = ((1,), (0,))
NT = ((1,), (1,))
TN = ((0,), (0,))

N_CHIPS = 4
MESH = pl.DeviceIdType.MESH


def _dot(a, b, dims):
    return lax.dot_general(a, b, (dims, ((), ())), preferred_element_type=F32)


def _cp(*sem):
    return pltpu.CompilerParams(dimension_semantics=sem, vmem_limit_bytes=VMEM_LIMIT_BYTES)


def _rows(t, cap=512):
    return min(cap, t)


def _pick(n, cap):
    best = None
    for d in range(LANES, min(n, cap) + 1, LANES):
        if n % d == 0:
            best = d
    assert best is not None, (n, cap)
    return best


def _sigmoid(x):
    return jax.nn.sigmoid(x)


def _ln_fwd(u, g, b):
    mu = jnp.mean(u, axis=-1, keepdims=True)
    xc = u - mu
    var = jnp.mean(xc * xc, axis=-1, keepdims=True)
    rstd = lax.rsqrt(var + LN_EPS)
    xhat = xc * rstd
    return xhat * g + b, xhat, rstd


def _ln_bwd(dy, xhat, rstd, g):
    dxh = dy * g
    m1 = jnp.mean(dxh, axis=-1, keepdims=True)
    m2 = jnp.mean(dxh * xhat, axis=-1, keepdims=True)
    du = rstd * (dxh - m1 - xhat * m2)
    return du, jnp.sum(dy * xhat, axis=0, keepdims=True), jnp.sum(dy, axis=0, keepdims=True)


def _full(shape):
    nd = len(shape)
    return pl.BlockSpec(shape, lambda *_: (0,) * nd)


def _mm_nn(a, w, *, out_dtype, name):
    t, k = a.shape
    n = w.shape[1]
    tm, tn = _rows(t), _pick(n, 1408)

    def body(a_ref, w_ref, o_ref):
        o_ref[...] = _dot(a_ref[...], w_ref[...], NN).astype(out_dtype)

    return pl.pallas_call(
        body, grid=(n // tn, t // tm),
        in_specs=[pl.BlockSpec((tm, k), lambda j, i: (i, 0)), pl.BlockSpec((k, tn), lambda j, i: (0, j))],
        out_specs=pl.BlockSpec((tm, tn), lambda j, i: (i, j)),
        out_shape=jax.ShapeDtypeStruct((t, n), out_dtype),
        compiler_params=_cp("parallel", "parallel"), name=name)(a, w)


def _mm_tn(a, b, *, name):
    r, m = a.shape
    n = b.shape[1]
    tr, tm, tn = _rows(r), _pick(m, 1408), _pick(n, 1408)

    def body(a_ref, b_ref, o_ref):
        @pl.when(pl.program_id(2) == 0)
        def _():
            o_ref[...] = jnp.zeros_like(o_ref)
        o_ref[...] += _dot(a_ref[...].astype(BF16), b_ref[...].astype(BF16), TN)

    return pl.pallas_call(
        body, grid=(m // tm, n // tn, r // tr),
        in_specs=[pl.BlockSpec((tr, tm), lambda i, j, k: (k, i)), pl.BlockSpec((tr, tn), lambda i, j, k: (k, j))],
        out_specs=pl.BlockSpec((tm, tn), lambda i, j, k: (i, j)),
        out_shape=jax.ShapeDtypeStruct((m, n), F32),
        compiler_params=_cp("parallel", "parallel", "arbitrary"), name=name)(a, b)


def _mm_tn_p(p, layer, b, *, name):
    t = p.shape[2]
    n = b.shape[1]
    tr = _rows(t)

    def body(a_ref, b_ref, o_ref):
        @pl.when(pl.program_id(0) == 0)
        def _():
            o_ref[...] = jnp.zeros_like(o_ref)
        o_ref[...] += _dot(a_ref[...].astype(BF16), b_ref[...], TN)

    return pl.pallas_call(
        body, grid=(t // tr,),
        in_specs=[pl.BlockSpec((None, None, tr, PLE_DIM), lambda k: (layer, 0, k, 0)),
                  pl.BlockSpec((tr, n), lambda k: (k, 0))],
        out_specs=pl.BlockSpec((PLE_DIM, n), lambda k: (0, 0)),
        out_shape=jax.ShapeDtypeStruct((PLE_DIM, n), F32),
        compiler_params=_cp("arbitrary"), name=name)(p, b)


def _mm_res_ln(a, w, resid, g, b, *, name):
    t, k = a.shape
    tm = _rows(t)

    def body(a_ref, w_ref, r_ref, g_ref, b_ref, y_ref, ybf_ref, xh_ref, rs_ref):
        acc = _dot(a_ref[...], w_ref[...], NN)
        y, xh, rs = _ln_fwd(ALPHA * r_ref[...] + acc, g_ref[...], b_ref[...])
        y_ref[...] = y
        ybf_ref[...] = y.astype(BF16)
        xh_ref[...] = xh
        rs_ref[...] = rs

    row = pl.BlockSpec((tm, D_MODEL), lambda i: (i, 0))
    return pl.pallas_call(
        body, grid=(t // tm,),
        in_specs=[pl.BlockSpec((tm, k), lambda i: (i, 0)), _full((k, D_MODEL)), row, _full((1, D_MODEL)), _full((1, D_MODEL))],
        out_specs=[row, row, row, pl.BlockSpec((tm, 1), lambda i: (i, 0))],
        out_shape=[jax.ShapeDtypeStruct((t, D_MODEL), F32), jax.ShapeDtypeStruct((t, D_MODEL), BF16),
                   jax.ShapeDtypeStruct((t, D_MODEL), F32), jax.ShapeDtypeStruct((t, 1), F32)],
        compiler_params=_cp("parallel"), name=name)(a, w, resid, g, b)


def _ffn_in(x_bf, w, *, name):
    t = x_bf.shape[0]
    tm, tn = _rows(t), _pick(D_FF, 1408)
    nb = D_FF // tn

    def body(x_ref, wg_ref, wu_ref, g_ref, u_ref, h_ref):
        x = x_ref[...]
        g = _dot(x, wg_ref[...], NN)
        u = _dot(x, wu_ref[...], NN)
        g_ref[...] = g.astype(BF16)
        u_ref[...] = u.astype(BF16)
        h_ref[...] = (g * _sigmoid(g) * u).astype(BF16)

    tile = pl.BlockSpec((tm, tn), lambda j, i: (i, j))
    shp = jax.ShapeDtypeStruct((t, D_FF), BF16)
    return pl.pallas_call(
        body, grid=(nb, t // tm),
        in_specs=[pl.BlockSpec((tm, D_MODEL), lambda j, i: (i, 0)),
                  pl.BlockSpec((D_MODEL, tn), lambda j, i: (0, j)),
                  pl.BlockSpec((D_MODEL, tn), lambda j, i: (0, j + nb))],
        out_specs=[tile, tile, tile], out_shape=[shp, shp, shp],
        compiler_params=_cp("parallel", "parallel"), name=name)(x_bf, w, w)


def _ple_fwd(x, x_bf, p, layer, wg, wp, *, name):
    t = x.shape[0]
    tm = _rows(t)

    def body(x_ref, xbf_ref, p_ref, wg_ref, wp_ref, y_ref, ybf_ref, gate_ref, pp_ref):
        gate = _sigmoid(_dot(xbf_ref[...], wg_ref[...], NN))
        pp = _dot(p_ref[...].astype(BF16), wp_ref[...], NN)
        y = x_ref[...] + gate * pp
        y_ref[...] = y
        ybf_ref[...] = y.astype(BF16)
        gate_ref[...] = gate.astype(BF16)
        pp_ref[...] = pp.astype(BF16)

    row = pl.BlockSpec((tm, D_MODEL), lambda i: (i, 0))
    f32, bf = jax.ShapeDtypeStruct((t, D_MODEL), F32), jax.ShapeDtypeStruct((t, D_MODEL), BF16)
    return pl.pallas_call(
        body, grid=(t // tm,),
        in_specs=[row, row, pl.BlockSpec((None, None, tm, PLE_DIM), lambda i: (layer, 0, i, 0)),
                  _full((D_MODEL, D_MODEL)), _full((PLE_DIM, D_MODEL))],
        out_specs=[row, row, row, row], out_shape=[f32, bf, bf, bf],
        compiler_params=_cp("parallel"), name=name)(x, x_bf, p, wg, wp)


def _loss_head(y, target, *, name):
    t = y.shape[0]
    tm = _rows(t)

    def body(y_ref, t_ref, loss_ref, dy_ref):
        e = y_ref[...] - t_ref[...]

        @pl.when(pl.program_id(0) == 0)
        def _():
            loss_ref[...] = jnp.zeros_like(loss_ref)
        sq = jnp.sum(jnp.sum(e * e, axis=1, keepdims=True), axis=0, keepdims=True)
        loss_ref[...] += sq * (0.5 / D_MODEL)
        dy_ref[...] = e * (1.0 / D_MODEL)

    row = pl.BlockSpec((tm, D_MODEL), lambda i: (i, 0))
    return pl.pallas_call(
        body, grid=(t // tm,), in_specs=[row, row],
        out_specs=[_full((1, 1)), row],
        out_shape=[jax.ShapeDtypeStruct((1, 1), F32), jax.ShapeDtypeStruct((t, D_MODEL), F32)],
        compiler_params=_cp("arbitrary"), name=name)(y, target)


def _ple_bwd(dx3, gate, pp, wg, xhat, rstd, g, *, name):
    t = dx3.shape[0]
    tm = _rows(t)

    def body(dx_ref, gate_ref, pp_ref, wg_ref, xh_ref, rs_ref, g_ref,
             du_ref, dubf_ref, dpre_ref, dpp_ref, dg_ref, db_ref):
        dx = dx_ref[...]
        gate = gate_ref[...].astype(F32)
        dpre = (dx * pp_ref[...].astype(F32) * gate * (1.0 - gate)).astype(BF16)
        dpre_ref[...] = dpre
        dpp_ref[...] = (dx * gate).astype(BF16)
        dx2 = dx + _dot(dpre, wg_ref[...], NT)
        du, dg, db = _ln_bwd(dx2, xh_ref[...], rs_ref[...], g_ref[...])
        du_ref[...] = du
        dubf_ref[...] = du.astype(BF16)

        @pl.when(pl.program_id(0) == 0)
        def _():
            dg_ref[...] = jnp.zeros_like(dg_ref)
            db_ref[...] = jnp.zeros_like(db_ref)
        dg_ref[...] += dg
        db_ref[...] += db

    row = pl.BlockSpec((tm, D_MODEL), lambda i: (i, 0))
    vec = _full((1, D_MODEL))
    f32, bf = jax.ShapeDtypeStruct((t, D_MODEL), F32), jax.ShapeDtypeStruct((t, D_MODEL), BF16)
    v32 = jax.ShapeDtypeStruct((1, D_MODEL), F32)
    return pl.pallas_call(
        body, grid=(t // tm,),
        in_specs=[row, row, row, _full((D_MODEL, D_MODEL)), row, pl.BlockSpec((tm, 1), lambda i: (i, 0)), vec],
        out_specs=[row, row, row, row, vec, vec], out_shape=[f32, bf, bf, bf, v32, v32],
        compiler_params=_cp("arbitrary"), name=name)(dx3, gate, pp, wg, xhat, rstd, g)


def _ffn_out_bwd(du_bf, w_out, g, u, *, name):
    t = du_bf.shape[0]
    tm, tn = _rows(t), _pick(D_FF, 1408)

    def body(du_ref, w_ref, g_ref, u_ref, dg_ref, dup_ref):
        dh = _dot(du_ref[...], w_ref[...], NT)
        gv = g_ref[...].astype(F32)
        sig = _sigmoid(gv)
        dg_ref[...] = (dh * u_ref[...].astype(F32) * sig * (1.0 + gv * (1.0 - sig))).astype(BF16)
        dup_ref[...] = (dh * gv * sig).astype(BF16)

    tile = pl.BlockSpec((tm, tn), lambda j, i: (i, j))
    shp = jax.ShapeDtypeStruct((t, D_FF), BF16)
    return pl.pallas_call(
        body, grid=(D_FF // tn, t // tm),
        in_specs=[pl.BlockSpec((tm, D_MODEL), lambda j, i: (i, 0)), pl.BlockSpec((tn, D_MODEL), lambda j, i: (j, 0)),
                  tile, tile],
        out_specs=[tile, tile], out_shape=[shp, shp],
        compiler_params=_cp("parallel", "parallel"), name=name)(du_bf, w_out, g, u)


def _mm_nt(parts, w, *, tk, resid=None, ln=None, out_dtype=F32, name):
    t, kp = parts[0].shape
    npart = len(parts)
    nk = kp // tk
    nkt = nk * npart
    tm = _rows(t)
    n_in = npart + 1 + (resid is not None) + (3 if ln is not None else 0)

    def body(*refs):
        a_refs, w_ref = refs[:npart], refs[npart]
        pos = npart + 1
        r_ref = None
        if resid is not None:
            r_ref = refs[pos]
            pos += 1
        if ln is not None:
            xh_ref, rs_ref, g_ref = refs[pos:pos + 3]
        outs, acc_ref = refs[n_in:-1], refs[-1]
        i, k = pl.program_id(0), pl.program_id(1)

        @pl.when(k == 0)
        def _():
            acc_ref[...] = jnp.zeros_like(acc_ref)

        for pi in range(npart):
            @pl.when((k >= pi * nk) & (k < (pi + 1) * nk))
            def _(pi=pi):
                acc_ref[...] += _dot(a_refs[pi][...], w_ref[...], NT)

        @pl.when(k == nkt - 1)
        def _():
            val = acc_ref[...]
            if r_ref is not None:
                val = val + ALPHA * r_ref[...]
            if ln is None:
                outs[0][...] = val.astype(out_dtype)
            else:
                du, dg, db = _ln_bwd(val, xh_ref[...], rs_ref[...], g_ref[...])
                outs[0][...] = du
                outs[1][...] = du.astype(BF16)

                @pl.when(i == 0)
                def _():
                    outs[2][...] = jnp.zeros_like(outs[2])
                    outs[3][...] = jnp.zeros_like(outs[3])
                outs[2][...] += dg
                outs[3][...] += db

    row = pl.BlockSpec((tm, D_MODEL), lambda i, k: (i, 0))
    vec = pl.BlockSpec((1, D_MODEL), lambda i, k: (0, 0))
    in_specs = [pl.BlockSpec((tm, tk), functools.partial(lambda i, k, lo: (i, jnp.clip(k - lo, 0, nk - 1)), lo=pi * nk))
                for pi in range(npart)]
    in_specs.append(pl.BlockSpec((D_MODEL, tk), lambda i, k: (0, k)))
    args = list(parts) + [w]
    if resid is not None:
        in_specs.append(row)
        args.append(resid)
    if ln is not None:
        in_specs += [row, pl.BlockSpec((tm, 1), lambda i, k: (i, 0)), vec]
        args += list(ln)
        out_specs = [row, row, vec, vec]
        out_shape = [jax.ShapeDtypeStruct((t, D_MODEL), F32), jax.ShapeDtypeStruct((t, D_MODEL), BF16),
                     jax.ShapeDtypeStruct((1, D_MODEL), F32), jax.ShapeDtypeStruct((1, D_MODEL), F32)]
    else:
        out_specs = [row]
        out_shape = [jax.ShapeDtypeStruct((t, D_MODEL), out_dtype)]
    res = pl.pallas_call(
        body, grid=(t // tm, nkt), in_specs=in_specs, out_specs=out_specs, out_shape=out_shape,
        scratch_shapes=[pltpu.VMEM((tm, D_MODEL), F32)],
        compiler_params=_cp("arbitrary", "arbitrary"), name=name)(*args)
    return res if ln is not None else res[0]


def _rope_tables(t):
    half = ROPE_DIM // 2
    inv = ROPE_THETA ** (-jnp.arange(0, ROPE_DIM, 2, dtype=F32) / ROPE_DIM)
    ang = jnp.arange(t, dtype=F32)[:, None] * inv[None, :]
    cos, sin = jnp.cos(ang), jnp.sin(ang)
    pad = HEAD_DIM - ROPE_DIM
    c = jnp.concatenate([cos, cos, jnp.ones((t, pad), F32)], axis=1)
    s_hi = jnp.concatenate([jnp.zeros((t, half), F32), sin, jnp.zeros((t, pad), F32)], axis=1)
    s_lo = jnp.concatenate([-sin, jnp.zeros((t, half + pad), F32)], axis=1)
    rep = LANES // HEAD_DIM
    return jnp.tile(c, (1, rep)), jnp.tile(s_hi, (1, rep)), jnp.tile(s_lo, (1, rep))


def _rope(x, c, s_hi, s_lo, sign):
    half = ROPE_DIM // 2
    parts = []
    for j in range(x.shape[1] // LANES):
        xg = x[:, j * LANES:(j + 1) * LANES]
        rot = pltpu.roll(xg, half, 1) * s_hi + pltpu.roll(xg, LANES - half, 1) * s_lo
        parts.append(xg * c + sign * rot)
    return jnp.concatenate(parts, axis=1)


def _qkv_rope(x_bf, w, tables, *, name):
    t = x_bf.shape[0]
    tm = _rows(t)
    n = Q_DIM + 2 * KV_DIM

    def body(x_ref, w_ref, c_ref, sh_ref, sl_ref, q_ref, k_ref, v_ref):
        acc = _dot(x_ref[...], w_ref[...], NN)
        c, sh, sl = c_ref[...], sh_ref[...], sl_ref[...]
        q_ref[...] = (_rope(acc[:, :Q_DIM], c, sh, sl, 1.0) * HEAD_DIM ** -0.5).astype(BF16)
        k_ref[...] = _rope(acc[:, Q_DIM:Q_DIM + KV_DIM], c, sh, sl, 1.0).astype(BF16)
        v_ref[...] = acc[:, Q_DIM + KV_DIM:].astype(BF16)

    tab = pl.BlockSpec((tm, LANES), lambda i: (i, 0))
    return pl.pallas_call(
        body, grid=(t // tm,),
        in_specs=[pl.BlockSpec((tm, D_MODEL), lambda i: (i, 0)), _full((D_MODEL, n)), tab, tab, tab],
        out_specs=[pl.BlockSpec((tm, Q_DIM), lambda i: (i, 0)), pl.BlockSpec((tm, KV_DIM), lambda i: (i, 0)),
                   pl.BlockSpec((tm, KV_DIM), lambda i: (i, 0))],
        out_shape=[jax.ShapeDtypeStruct((t, Q_DIM), BF16), jax.ShapeDtypeStruct((t, KV_DIM), BF16),
                   jax.ShapeDtypeStruct((t, KV_DIM), BF16)],
        compiler_params=_cp("parallel"), name=name)(x_bf, w, *tables)


ATT_ROWS = 256


def _window_specs(t, tq):
    r = tq // WINDOW
    last = t // WINDOW - 1
    return [pl.BlockSpec((WINDOW, KV_DIM), lambda i: (jnp.maximum(i * r - 1, 0), 0)),
            pl.BlockSpec((tq, KV_DIM), lambda i: (i, 0)),
            pl.BlockSpec((WINDOW, KV_DIM), lambda i: (jnp.minimum((i + 1) * r, last), 0))]


def _att_valid(base, t):
    rows = GROUP * WINDOW
    qpos = base + (lax.broadcasted_iota(jnp.int32, (rows, 3 * WINDOW), 0) & (WINDOW - 1))
    kpos = base - WINDOW + lax.broadcasted_iota(jnp.int32, (rows, 3 * WINDOW), 1)
    return (jnp.abs(qpos - kpos) <= WINDOW) & (kpos >= 0) & (kpos < t)


def _stack_heads(x, r0, g):
    return jnp.concatenate(
        [x[r0:r0 + WINDOW, (GROUP * g + h) * HEAD_DIM:(GROUP * g + h + 1) * HEAD_DIM] for h in range(GROUP)], axis=0)


def _sink_column(sink_ref, g):
    return jnp.concatenate([jnp.full((WINDOW, 1), sink_ref[GROUP * g + h], F32) for h in range(GROUP)], axis=0)


def _unstack_heads(pieces):
    return jnp.concatenate([pieces[g][h * WINDOW:(h + 1) * WINDOW] for g in range(N_KV_HEADS) for h in range(GROUP)], axis=1)


def _attn_fwd(q, k, v, sink, *, name):
    t = q.shape[0]
    tq = min(ATT_ROWS, t)
    nsb = tq // WINDOW

    def body(sink_ref, q_ref, kp_ref, kc_ref, kn_ref, vp_ref, vc_ref, vn_ref, o_ref, l_ref):
        i = pl.program_id(0)
        qv = q_ref[...]
        kw = jnp.concatenate([kp_ref[...], kc_ref[...], kn_ref[...]], axis=0)
        vw = jnp.concatenate([vp_ref[...], vc_ref[...], vn_ref[...]], axis=0)
        for sb in range(nsb):
            r0 = sb * WINDOW
            valid = _att_valid(i * tq + r0, t)
            pieces = []
            for g in range(N_KV_HEADS):
                qs = _stack_heads(qv, r0, g)
                kg = kw[r0:r0 + 3 * WINDOW, g * HEAD_DIM:(g + 1) * HEAD_DIM]
                vg = vw[r0:r0 + 3 * WINDOW, g * HEAD_DIM:(g + 1) * HEAD_DIM]
                s = jnp.where(valid, _dot(qs, kg, NT), -1e30)
                snk = _sink_column(sink_ref, g)
                m = jnp.maximum(jnp.max(s, axis=1, keepdims=True), snk)
                pr = jnp.exp(s - m)
                den = jnp.sum(pr, axis=1, keepdims=True) + jnp.exp(snk - m)
                pieces.append(_dot((pr * (1.0 / den)).astype(BF16), vg, NN))
                l_ref[g, sb] = m + jnp.log(den)
            o_ref[r0:r0 + WINDOW, :] = _unstack_heads(pieces).astype(BF16)

    return pl.pallas_call(
        body, grid=(t // tq,),
        in_specs=[pl.BlockSpec(memory_space=pltpu.SMEM), pl.BlockSpec((tq, Q_DIM), lambda i: (i, 0))]
        + _window_specs(t, tq) + _window_specs(t, tq),
        out_specs=[pl.BlockSpec((tq, Q_DIM), lambda i: (i, 0)),
                   pl.BlockSpec((N_KV_HEADS, nsb, GROUP * WINDOW, 1), lambda i: (0, i, 0, 0))],
        out_shape=[jax.ShapeDtypeStruct((t, Q_DIM), BF16),
                   jax.ShapeDtypeStruct((N_KV_HEADS, t // WINDOW, GROUP * WINDOW, 1), F32)],
        compiler_params=_cp("parallel"), name=name)(sink, q, k, k, k, v, v, v)


def _attn_bwd(q, k, v, o, do, lse, sink, *, name):
    t = q.shape[0]
    tq = min(ATT_ROWS, t)
    nsb = tq // WINDOW

    def body(sink_ref, q_ref, o_ref, do_ref, l_ref, kp_ref, kc_ref, kn_ref, vp_ref, vc_ref, vn_ref,
             dq_ref, dkw_ref, dvw_ref, dsink_ref):
        i = pl.program_id(0)
        qv, ov, dov = q_ref[...], o_ref[...], do_ref[...]
        kw = jnp.concatenate([kp_ref[...], kc_ref[...], kn_ref[...]], axis=0)
        vw = jnp.concatenate([vp_ref[...], vc_ref[...], vn_ref[...]], axis=0)
        lane16 = lax.broadcasted_iota(jnp.int32, (1, N_Q_HEADS), 1)
        dsink = jnp.zeros((1, N_Q_HEADS), F32)
        for sb in range(nsb):
            r0 = sb * WINDOW
            valid = _att_valid(i * tq + r0, t)
            dq_p, dk_p, dv_p = [], [], []
            for g in range(N_KV_HEADS):
                qs, dos = _stack_heads(qv, r0, g), _stack_heads(dov, r0, g)
                delta = jnp.sum(dos.astype(F32) * _stack_heads(ov, r0, g).astype(F32), axis=1, keepdims=True)
                kg = kw[r0:r0 + 3 * WINDOW, g * HEAD_DIM:(g + 1) * HEAD_DIM]
                vg = vw[r0:r0 + 3 * WINDOW, g * HEAD_DIM:(g + 1) * HEAD_DIM]
                lse_col = l_ref[g, sb]
                pr = jnp.where(valid, jnp.exp(_dot(qs, kg, NT) - lse_col), 0.0)
                ds = (pr * (_dot(dos, vg, NT) - delta)).astype(BF16)
                dq_p.append(_dot(ds, kg, NN))
                dk_p.append(_dot(ds, qs, TN))
                dv_p.append(_dot(pr.astype(BF16), dos, TN))
                ps = jnp.exp(_sink_column(sink_ref, g) - lse_col) * delta
                for h in range(GROUP):
                    tot = jnp.sum(ps[h * WINDOW:(h + 1) * WINDOW], axis=0, keepdims=True)
                    dsink = dsink - jnp.where(lane16 == GROUP * g + h, tot, 0.0)
            dq_ref[r0:r0 + WINDOW, :] = _unstack_heads(dq_p)
            dkw_ref[sb] = jnp.concatenate(dk_p, axis=1)
            dvw_ref[sb] = jnp.concatenate(dv_p, axis=1)

        @pl.when(i == 0)
        def _():
            dsink_ref[...] = jnp.zeros_like(dsink_ref)
        dsink_ref[...] += dsink

    rowq = pl.BlockSpec((tq, Q_DIM), lambda i: (i, 0))
    win = pl.BlockSpec((nsb, 3 * WINDOW, KV_DIM), lambda i: (i, 0, 0))
    wshape = jax.ShapeDtypeStruct((t // WINDOW, 3 * WINDOW, KV_DIM), F32)
    return pl.pallas_call(
        body, grid=(t // tq,),
        in_specs=[pl.BlockSpec(memory_space=pltpu.SMEM), rowq, rowq, rowq,
                  pl.BlockSpec((N_KV_HEADS, nsb, GROUP * WINDOW, 1), lambda i: (0, i, 0, 0))]
        + _window_specs(t, tq) + _window_specs(t, tq),
        out_specs=[rowq, win, win, _full((1, N_Q_HEADS))],
        out_shape=[jax.ShapeDtypeStruct((t, Q_DIM), F32), wshape, wshape, jax.ShapeDtypeStruct((1, N_Q_HEADS), F32)],
        compiler_params=_cp("arbitrary"), name=name)(sink, q, o, do, lse, k, k, k, v, v, v)


def _attn_post_bwd(dq, dkw, dvw, tables, *, name):
    t = dq.shape[0]
    nb = t // WINDOW
    n = Q_DIM + 2 * KV_DIM

    def body(dq_ref, ka_ref, kb_ref, kc_ref, va_ref, vb_ref, vc_ref, c_ref, sh_ref, sl_ref, o_ref):
        j = pl.program_id(0)
        lo = (j > 0).astype(F32)
        hi = (j < nb - 1).astype(F32)
        c, sh, sl = c_ref[...], sh_ref[...], sl_ref[...]
        dk = ka_ref[...] * hi + kb_ref[...] + kc_ref[...] * lo
        dv = va_ref[...] * hi + vb_ref[...] + vc_ref[...] * lo
        o_ref[:, :Q_DIM] = (_rope(dq_ref[...], c, sh, sl, -1.0) * HEAD_DIM ** -0.5).astype(BF16)
        o_ref[:, Q_DIM:Q_DIM + KV_DIM] = _rope(dk, c, sh, sl, -1.0).astype(BF16)
        o_ref[:, Q_DIM + KV_DIM:] = dv.astype(BF16)

    wins = [pl.BlockSpec((None, WINDOW, KV_DIM), lambda j: (jnp.minimum(j + 1, nb - 1), 0, 0)),
            pl.BlockSpec((None, WINDOW, KV_DIM), lambda j: (j, 1, 0)),
            pl.BlockSpec((None, WINDOW, KV_DIM), lambda j: (jnp.maximum(j - 1, 0), 2, 0))]
    tab = pl.BlockSpec((WINDOW, LANES), lambda j: (j, 0))
    return pl.pallas_call(
        body, grid=(nb,),
        in_specs=[pl.BlockSpec((WINDOW, Q_DIM), lambda j: (j, 0))] + wins + wins + [tab, tab, tab],
        out_specs=pl.BlockSpec((WINDOW, n), lambda j: (j, 0)),
        out_shape=jax.ShapeDtypeStruct((t, n), BF16),
        compiler_params=_cp("parallel"), name=name)(dq, dkw, dkw, dkw, dvw, dvw, dvw, *tables)


HGRN_ROWS = 512


def _cum_mask(rev, transpose=False):
    row = lax.broadcasted_iota(jnp.int32, (HGRN_CHUNK, HGRN_CHUNK), 0)
    col = lax.broadcasted_iota(jnp.int32, (HGRN_CHUNK, HGRN_CHUNK), 1)
    return (row <= col) if rev != transpose else (row >= col)


def _gates(zq, zf, lb):
    q = zq * _sigmoid(zq)
    sig = _sigmoid(zf)
    f = lb + (1.0 - lb) * sig
    k = (1.0 - lb) * (1.0 - sig)
    return q, sig, f, k


def _intra_exact(q, k, b, mask):
    rows = lax.broadcasted_iota(jnp.int32, (HGRN_CHUNK, 1), 0)
    cols = lax.broadcasted_iota(jnp.int32, (HGRN_CHUNK, HGRN_CHUNK), 1)

    def it(s, a):
        pick = rows == s
        b_s = jnp.sum(jnp.where(pick, b, 0.0), axis=0, keepdims=True)
        k_s = jnp.sum(jnp.where(pick, k, 0.0), axis=0, keepdims=True)
        col = jnp.sum(q * jnp.exp(jnp.minimum(b - b_s, 0.0)) * k_s, axis=1, keepdims=True)
        return jnp.where(cols == s, col, a)

    a = lax.fori_loop(0, HGRN_CHUNK, it, jnp.zeros((HGRN_CHUNK, HGRN_CHUNK), F32))
    return jnp.where(mask, a, 0.0)


def _intra_exact_bwd(q, k, b, da):
    rows = lax.broadcasted_iota(jnp.int32, (HGRN_CHUNK, 1), 0)
    cols = lax.broadcasted_iota(jnp.int32, (HGRN_CHUNK, HGRN_CHUNK), 1)

    def it(s, carry):
        dq, dk = carry
        pick = rows == s
        b_s = jnp.sum(jnp.where(pick, b, 0.0), axis=0, keepdims=True)
        k_s = jnp.sum(jnp.where(pick, k, 0.0), axis=0, keepdims=True)
        w = jnp.sum(jnp.where(cols == s, da, 0.0), axis=1, keepdims=True) * jnp.exp(jnp.minimum(b - b_s, 0.0))
        dq = dq + w * k_s
        dk = jnp.where(pick, jnp.sum(w * q, axis=0, keepdims=True), dk)
        return dq, dk

    z = jnp.zeros((HGRN_CHUNK, HGRN_KEY), F32)
    return lax.fori_loop(0, HGRN_CHUNK, it, (z, z))


def _hgrn_scan_fwd(z, lb, rev, *, name):
    t = z.shape[0]
    rb = min(HGRN_ROWS, t)
    nb, nch = t // rb, rb // HGRN_CHUNK
    fcol = HGRN_HEADS * (2 if rev else 1)

    def blk(n):
        return nb - 1 - n if rev else n

    def body(zq_ref, zf_ref, zv_ref, lb_ref, o_ref, s_ref, st_ref):
        @pl.when(pl.program_id(1) == 0)
        def _():
            st_ref[...] = jnp.zeros_like(st_ref)
        lbv = lb_ref[...]
        cmask = _cum_mask(rev)
        cum = cmask.astype(F32)

        def chunk(c, carry):
            cc = nch - 1 - c if rev else c
            r0 = pl.multiple_of(cc * HGRN_CHUNK, HGRN_CHUNK)
            sl = pl.ds(r0, HGRN_CHUNK)
            q, _, f, k = _gates(zq_ref[sl, :], zf_ref[sl, :], lbv)
            v = zv_ref[sl, :].astype(BF16)
            g = jnp.log(f)
            b = jnp.dot(cum, g, precision=lax.Precision.HIGHEST, preferred_element_type=F32)
            tot = jnp.sum(g, axis=0, keepdims=True)
            qd = q * jnp.exp(b)
            st = st_ref[...]
            st_bf = st.astype(BF16)
            s_ref[cc] = st_bf
            a = lax.cond(
                jnp.min(tot) >= FACTORED_DECAY_LIMIT,
                lambda: jnp.where(cmask, _dot(qd.astype(BF16), (k * jnp.exp(-b)).astype(BF16), NT), 0.0),
                lambda: _intra_exact(q, k, b, cmask))
            o_ref[sl, :] = _dot(qd.astype(BF16), st_bf, NT) + _dot(a.astype(BF16), v, NN)
            kd = (k * jnp.exp(tot - b)).astype(BF16)
            st_ref[...] = st * jnp.exp(tot) + _dot(v, kd, TN)
            return carry

        lax.fori_loop(0, nch, chunk, 0)

    def col(off):
        return pl.BlockSpec((rb, LANES), lambda h, n: (blk(n), off + h))

    return pl.pallas_call(
        body, grid=(HGRN_HEADS, nb),
        in_specs=[col(0), col(fcol), col(3 * HGRN_HEADS), pl.BlockSpec((None, 1, LANES), lambda h, n: (h, 0, 0))],
        out_specs=[pl.BlockSpec((rb, LANES), lambda h, n: (blk(n), h)),
                   pl.BlockSpec((None, nch, LANES, LANES), lambda h, n: (h, blk(n), 0, 0))],
        out_shape=[jax.ShapeDtypeStruct((t, D_MODEL), F32),
                   jax.ShapeDtypeStruct((HGRN_HEADS, t // HGRN_CHUNK, LANES, LANES), BF16)],
        scratch_shapes=[pltpu.VMEM((LANES, LANES), F32)],
        compiler_params=_cp("parallel", "arbitrary"), name=name)(z, z, z, lb)


def _hgrn_scan_bwd(z, lb, d_o, states, rev, *, name):
    t = z.shape[0]
    rb = min(HGRN_ROWS, t)
    nb, nch = t // rb, rb // HGRN_CHUNK
    fcol = HGRN_HEADS * (2 if rev else 1)

    def blk(n):
        return n if rev else nb - 1 - n

    def body(zq_ref, zf_ref, zv_ref, lb_ref, do_ref, s_ref, dq_ref, dzf_ref, dv_ref, dlb_ref, dst_ref):
        @pl.when(pl.program_id(1) == 0)
        def _():
            dst_ref[...] = jnp.zeros_like(dst_ref)
            dlb_ref[...] = jnp.zeros_like(dlb_ref)
        lbv = lb_ref[...]
        cmask = _cum_mask(rev)
        cum = cmask.astype(F32)
        cum_t = _cum_mask(rev, transpose=True).astype(F32)

        def chunk(c, carry):
            cc = c if rev else nch - 1 - c
            r0 = pl.multiple_of(cc * HGRN_CHUNK, HGRN_CHUNK)
            sl = pl.ds(r0, HGRN_CHUNK)
            zq = zq_ref[sl, :]
            q, sig, f, k = _gates(zq, zf_ref[sl, :], lbv)
            v = zv_ref[sl, :].astype(BF16)
            g = jnp.log(f)
            b = jnp.dot(cum, g, precision=lax.Precision.HIGHEST, preferred_element_type=F32)
            tot = jnp.sum(g, axis=0, keepdims=True)
            eb = jnp.exp(b)
            qd = (q * eb).astype(BF16)
            kd = (k * jnp.exp(tot - b)).astype(BF16)
            do = do_ref[sl, :].astype(BF16)
            st0 = s_ref[cc]
            dst = dst_ref[...]
            dst_bf = dst.astype(BF16)
            da = jnp.where(cmask, _dot(do, v, NT), 0.0)
            factored = jnp.min(tot) >= FACTORED_DECAY_LIMIT

            def fast():
                ke = jnp.exp(-b)
                kf = (k * ke).astype(BF16)
                a = jnp.where(cmask, _dot(qd, kf, NT), 0.0)
                da_bf = da.astype(BF16)
                dqf, dkf = _dot(da_bf, kf, NN), _dot(da_bf, qd, TN)
                return a, dqf * eb, dkf * ke, qd.astype(F32) * dqf - kf.astype(F32) * dkf

            def slow():
                dq_i, dk_i = _intra_exact_bwd(q, k, b, da)
                return _intra_exact(q, k, b, cmask), dq_i, dk_i, q * dq_i - k * dk_i

            a, dq_i, dk_i, db_i = lax.cond(factored, fast, slow)
            dv_ref[sl, :] = _dot(a.astype(BF16), do, TN) + _dot(kd, dst_bf, NT)
            dq_x = _dot(do, st0, NN) * eb
            dq = dq_i + dq_x
            dk_x = _dot(v, dst_bf, NN) * jnp.exp(tot - b)
            dk = dk_i + dk_x
            etot = jnp.exp(tot)
            dtot = jnp.sum(k * dk_x, axis=0, keepdims=True) + etot * jnp.sum(dst * st0.astype(F32), axis=0, keepdims=True)
            dst_ref[...] = dst * etot + _dot(do, qd, TN)
            db = db_i + q * dq_x - k * dk_x
            dg =jnp.dot(cum_t, db, precision=lax.Precision.HIGHEST, preferred_element_type=F32) + dtot
            sq = _sigmoid(zq)
            dq_ref[sl, :] = dq * sq * (1.0 + zq * (1.0 - sq))
            dfk = dg / f - dk
            dzf_ref[sl, :] = (dfk * (1.0 - lbv) * sig * (1.0 - sig)).astype(BF16)
            dlb_ref[...] += jnp.sum(dfk * (1.0 - sig), axis=0, keepdims=True)
            return carry

        lax.fori_loop(0, nch, chunk, 0)

    def col(off):
        return pl.BlockSpec((rb, LANES), lambda h, n: (blk(n), off + h))

    out_col = pl.BlockSpec((rb, LANES), lambda h, n: (blk(n), h))
    return pl.pallas_call(
        body, grid=(HGRN_HEADS, nb),
        in_specs=[col(0), col(fcol), col(3 * HGRN_HEADS), pl.BlockSpec((None, 1, LANES), lambda h, n: (h, 0, 0)),
                  out_col, pl.BlockSpec((None, nch, LANES, LANES), lambda h, n: (h, blk(n), 0, 0))],
        out_specs=[out_col, out_col, out_col, pl.BlockSpec((None, 1, LANES), lambda h, n: (h, 0, 0))],
        out_shape=[jax.ShapeDtypeStruct((t, D_MODEL), F32), jax.ShapeDtypeStruct((t, D_MODEL), BF16),
                   jax.ShapeDtypeStruct((t, D_MODEL), F32), jax.ShapeDtypeStruct((HGRN_HEADS, 1, LANES), F32)],
        scratch_shapes=[pltpu.VMEM((LANES, LANES), F32)],
        compiler_params=_cp("parallel", "arbitrary"), name=name)(z, z, z, lb, d_o, states)


def _head(a, h):
    return a[:, h * LANES:(h + 1) * LANES]


def _hgrn_fwd(z, lb, rev, *, name):
    t = z.shape[0]
    rb = min(HGRN_ROWS, t)
    nb, nch = t // rb, rb // HGRN_CHUNK
    heads = range(HGRN_HEADS)

    def blk(n):
        return nb - 1 - n if rev else n

    def body(zq_ref, zf_ref, zv_ref, lb_ref, o_ref, s_ref, st_ref):
        @pl.when(pl.program_id(0) == 0)
        def _():
            st_ref[...] = jnp.zeros_like(st_ref)
        lbv = lb_ref[...]
        cmask = _cum_mask(rev)
        cum = cmask.astype(F32)

        def chunk(c, carry):
            cc = nch - 1 - c if rev else c
            sl = pl.ds(pl.multiple_of(cc * HGRN_CHUNK, HGRN_CHUNK), HGRN_CHUNK)
            q, _, f, k = _gates(zq_ref[sl, :], zf_ref[sl, :], lbv)
            v = zv_ref[sl, :].astype(BF16)
            g = jnp.log(f)
            b = jnp.dot(cum, g, precision=lax.Precision.HIGHEST, preferred_element_type=F32)
            tot = jnp.sum(g, axis=0, keepdims=True)
            qd = (q * jnp.exp(b)).astype(BF16)
            kd = (k * jnp.exp(tot - b)).astype(BF16)
            etot = jnp.exp(tot)

            def factored():
                kf = (k * jnp.exp(-b)).astype(BF16)
                return tuple(jnp.where(cmask, _dot(_head(qd, h), _head(kf, h), NT), 0.0) for h in heads)

            def exact():
                return tuple(_intra_exact(_head(q, h), _head(k, h), _head(b, h), cmask) for h in heads)

            a = lax.cond(jnp.min(tot) >= FACTORED_DECAY_LIMIT, factored, exact)
            for h in heads:
                st = st_ref[h]
                st_bf = st.astype(BF16)
                s_ref[h, cc] = st_bf
                o_ref[sl, h * LANES:(h + 1) * LANES] = (
                    _dot(_head(qd, h), st_bf, NT) + _dot(a[h].astype(BF16), _head(v, h), NN))
                st_ref[h] = st * _head(etot, h) + _dot(_head(v, h), _head(kd, h), TN)
            return carry

        lax.fori_loop(0, nch, chunk, 0)

    def col(j):
        return pl.BlockSpec((rb, D_MODEL), lambda n: (blk(n), j))

    return pl.pallas_call(
        body, grid=(nb,),
        in_specs=[col(0), col(2 if rev else 1), col(3), _full((1, D_MODEL))],
        out_specs=[col(0), pl.BlockSpec((HGRN_HEADS, nch, LANES, LANES), lambda n: (0, blk(n), 0, 0))],
        out_shape=[jax.ShapeDtypeStruct((t, D_MODEL), F32),
                   jax.ShapeDtypeStruct((HGRN_HEADS, t // HGRN_CHUNK, LANES, LANES), BF16)],
        scratch_shapes=[pltpu.VMEM((HGRN_HEADS, LANES, LANES), F32)],
        compiler_params=_cp("arbitrary"), name=name)(z, z, z, lb)


def _hgrn_bwd(z, lb, d_o, states, rev, *, name):
    t = z.shape[0]
    rb = min(HGRN_ROWS, t)
    nb, nch = t // rb, rb // HGRN_CHUNK
    heads = range(HGRN_HEADS)

    def blk(n):
        return n if rev else nb - 1 - n

    def body(zq_ref, zf_ref, zv_ref, lb_ref, do_ref, s_ref, dq_ref, dzf_ref, dv_ref, dlb_ref, dst_ref):
        @pl.when(pl.program_id(0) == 0)
        def _():
            dst_ref[...] = jnp.zeros_like(dst_ref)
            dlb_ref[...] = jnp.zeros_like(dlb_ref)
        lbv = lb_ref[...]
        cmask = _cum_mask(rev)
        cum = cmask.astype(F32)
        cum_t = _cum_mask(rev, transpose=True).astype(F32)

        def chunk(c, carry):
            cc = c if rev else nch - 1 - c
            sl = pl.ds(pl.multiple_of(cc * HGRN_CHUNK, HGRN_CHUNK), HGRN_CHUNK)
            zq = zq_ref[sl, :]
            q, sig, f, k = _gates(zq, zf_ref[sl, :], lbv)
            v = zv_ref[sl, :].astype(BF16)
            g = jnp.log(f)
            b = jnp.dot(cum, g, precision=lax.Precision.HIGHEST, preferred_element_type=F32)
            tot = jnp.sum(g, axis=0, keepdims=True)
            eb = jnp.exp(b)
            etb = jnp.exp(tot - b)
            etot = jnp.exp(tot)
            qd = (q * eb).astype(BF16)
            kd = (k * etb).astype(BF16)
            do = do_ref[sl, :].astype(BF16)
            da = [jnp.where(cmask, _dot(_head(do, h), _head(v, h), NT), 0.0) for h in heads]

            def factored():
                ke = jnp.exp(-b)
                kf = (k * ke).astype(BF16)
                res = []
                for h in heads:
                    qh, kh = _head(qd, h), _head(kf, h)
                    da_bf = da[h].astype(BF16)
                    dqf, dkf = _dot(da_bf, kh, NN), _dot(da_bf, qh, TN)
                    res.append((jnp.where(cmask, _dot(qh, kh, NT), 0.0), dqf * _head(eb, h), dkf * _head(ke, h),
                                qh.astype(F32) * dqf - kh.astype(F32) * dkf))
                return tuple(res)

            def exact():
                res = []
                for h in heads:
                    qh, kh, bh = _head(q, h), _head(k, h), _head(b, h)
                    dq_i, dk_i = _intra_exact_bwd(qh, kh, bh, da[h])
                    res.append((_intra_exact(qh, kh, bh, cmask), dq_i, dk_i, qh * dq_i - kh * dk_i))
                return tuple(res)

            intra = lax.cond(jnp.min(tot) >= FACTORED_DECAY_LIMIT, factored, exact)
            dq_p, dk_p, db_p, dtot_p = [], [], [], []
            for h in heads:
                a, dq_i, dk_i, db_i = intra[h]
                doh, vh, qh, kh = _head(do, h), _head(v, h), _head(q, h), _head(k, h)
                st0 = s_ref[h, cc]
                dst = dst_ref[h]
                dst_bf = dst.astype(BF16)
                dv_ref[sl, h * LANES:(h + 1) * LANES] = _dot(a.astype(BF16), doh, TN) + _dot(_head(kd, h), dst_bf, NT)
                dq_x = _dot(doh, st0, NN) * _head(eb, h)
                dk_x = _dot(vh, dst_bf, NN) * _head(etb, h)
                dtot_p.append(jnp.sum(kh * dk_x, axis=0, keepdims=True)
                              + _head(etot, h) * jnp.sum(dst * st0.astype(F32), axis=0, keepdims=True))
                dst_ref[h] = dst * _head(etot, h) + _dot(doh, _head(qd, h), TN)
                dq_p.append(dq_i + dq_x)
                dk_p.append(dk_i + dk_x)
                db_p.append(db_i + qh * dq_x - kh * dk_x)
            dq, dk, db = (jnp.concatenate(x, axis=1) for x in (dq_p, dk_p, db_p))
            dg = jnp.dot(cum_t, db, precision=lax.Precision.HIGHEST, preferred_element_type=F32) + jnp.concatenate(dtot_p, axis=1)
            sq = _sigmoid(zq)
            dq_ref[sl, :] = dq * sq * (1.0 + zq * (1.0 - sq))
            dfk = dg / f - dk
            dzf_ref[sl, :] = (dfk * (1.0 - lbv) * sig * (1.0 - sig)).astype(BF16)
            dlb_ref[...] += jnp.sum(dfk * (1.0 - sig), axis=0, keepdims=True)
            return carry

        lax.fori_loop(0, nch, chunk, 0)

    def col(j):
        return pl.BlockSpec((rb, D_MODEL), lambda n: (blk(n), j))

    return pl.pallas_call(
        body, grid=(nb,),
        in_specs=[col(0), col(2 if rev else 1), col(3), _full((1, D_MODEL)), col(0),
                  pl.BlockSpec((HGRN_HEADS, nch, LANES, LANES), lambda n: (0, blk(n), 0, 0))],
        out_specs=[col(0), col(0), col(0), _full((1, D_MODEL))],
        out_shape=[jax.ShapeDtypeStruct((t, D_MODEL), F32), jax.ShapeDtypeStruct((t, D_MODEL), BF16),
                   jax.ShapeDtypeStruct((t, D_MODEL), F32), jax.ShapeDtypeStruct((1, D_MODEL), F32)],
        scratch_shapes=[pltpu.VMEM((HGRN_HEADS, LANES, LANES), F32)],
        compiler_params=_cp("arbitrary"), name=name)(z, z, z, lb, d_o, states)


def _hgrn_post_fwd(o_f, o_b, z, norm_g, *, name):
    t = o_f.shape[0]
    tm = _rows(t)

    def body(of_ref, ob_ref, gate_ref, ng_ref, y_ref):
        ng = ng_ref[...]
        for h in range(HGRN_HEADS):
            sl = slice(h * LANES, (h + 1) * LANES)
            o = of_ref[:, sl] + ob_ref[:, sl]
            r = lax.rsqrt(jnp.mean(o * o, axis=1, keepdims=True) + LN_EPS)
            gt = gate_ref[:, sl]
            y_ref[:, sl] = (o * r * ng * (gt * _sigmoid(gt))).astype(BF16)

    row = pl.BlockSpec((tm, D_MODEL), lambda i: (i, 0))
    return pl.pallas_call(
        body, grid=(t // tm,),
        in_specs=[row, row, pl.BlockSpec((tm, D_MODEL), lambda i: (i, 4)), _full((1, LANES))],
        out_specs=row, out_shape=jax.ShapeDtypeStruct((t, D_MODEL), BF16),
        compiler_params=_cp("parallel"), name=name)(o_f, o_b, z, norm_g)


def _hgrn_post_bwd(dy, o_f, o_b, z, norm_g, *, name):
    t = o_f.shape[0]
    tm = _rows(t)

    def body(dy_ref, of_ref, ob_ref, gate_ref, ng_ref, do_ref, dgate_ref, dng_ref):
        ng = ng_ref[...]
        dng = jnp.zeros((1, LANES), F32)
        for h in range(HGRN_HEADS):
            sl = slice(h * LANES, (h + 1) * LANES)
            o = of_ref[:, sl] + ob_ref[:, sl]
            r = lax.rsqrt(jnp.mean(o * o, axis=1, keepdims=True) + LN_EPS)
            gt = gate_ref[:, sl]
            sg = _sigmoid(gt)
            dyv = dy_ref[:, sl].astype(F32)
            xn = o * r
            dgate_ref[:, sl] = (dyv * xn * ng * sg * (1.0 + gt * (1.0 - sg))).astype(BF16)
            dn = dyv * gt * sg
            dng = dng + jnp.sum(dn * xn, axis=0, keepdims=True)
            dxn = dn * ng
            do_ref[:, sl] = r * (dxn - xn * jnp.mean(dxn * xn, axis=1, keepdims=True))

        @pl.when(pl.program_id(0) == 0)
        def _():
            dng_ref[...] = jnp.zeros_like(dng_ref)
        dng_ref[...] += dng

    row = pl.BlockSpec((tm, D_MODEL), lambda i: (i, 0))
    return pl.pallas_call(
        body, grid=(t // tm,),
        in_specs=[row, row, row, pl.BlockSpec((tm, D_MODEL), lambda i: (i, 4)), _full((1, LANES))],
        out_specs=[row, row, _full((1, LANES))],
        out_shape=[jax.ShapeDtypeStruct((t, D_MODEL), F32), jax.ShapeDtypeStruct((t, D_MODEL), BF16),
                   jax.ShapeDtypeStruct((1, LANES), F32)],
        compiler_params=_cp("arbitrary"), name=name)(dy, o_f, o_b, z, norm_g)


def _hgrn_combine(dq_f, dq_b, dzf_f, dzf_b, dv_f, dv_b, dgate, *, name):
    t = dq_f.shape[0]
    tm = _rows(t)

    def body(qf, qb, ff, fb, vf, vb, gt, o_ref):
        o_ref[:, 0 * D_MODEL:1 * D_MODEL] = (qf[...] + qb[...]).astype(BF16)
        o_ref[:, 1 * D_MODEL:2 * D_MODEL] = ff[...]
        o_ref[:, 2 * D_MODEL:3 * D_MODEL] = fb[...]
        o_ref[:, 3 * D_MODEL:4 * D_MODEL] = (vf[...] + vb[...]).astype(BF16)
        o_ref[:, 4 * D_MODEL:5 * D_MODEL] = gt[...]

    row = pl.BlockSpec((tm, D_MODEL), lambda i: (i, 0))
    return pl.pallas_call(
        body, grid=(t // tm,), in_specs=[row] * 7,
        out_specs=pl.BlockSpec((tm, 5 * D_MODEL), lambda i: (i, 0)),
        out_shape=jax.ShapeDtypeStruct((t, 5 * D_MODEL), BF16),
        compiler_params=_cp("parallel"), name=name)(dq_f, dq_b, dzf_f, dzf_b, dv_f, dv_b, dgate)


def _lower_bounds(logits2d, *, name):
    def body(l_ref, lb_ref):
        l = l_ref[...]
        e = jnp.exp(l - jnp.max(l, axis=0, keepdims=True))
        sm = e / jnp.sum(e, axis=0, keepdims=True)
        s1, s2, s3 = sm[1:2], sm[2:3], sm[3:4]
        lb_ref[...] = jnp.concatenate([jnp.zeros_like(s1), s1, s1 + s2, s1 + s2 + s3], axis=0)

    return pl.pallas_call(body, out_shape=jax.ShapeDtypeStruct(logits2d.shape, F32), name=name)(logits2d)


def _lower_bounds_bwd(logits2d, dlb, *, name):
    def body(l_ref, d_ref, o_ref):
        l = l_ref[...]
        e = jnp.exp(l - jnp.max(l, axis=0, keepdims=True))
        sm = e / jnp.sum(e, axis=0, keepdims=True)
        d = d_ref[...]
        d1, d2, d3 = d[1:2], d[2:3], d[3:4]
        dsm = jnp.concatenate([jnp.zeros_like(d1), d1 + d2 + d3, d2 + d3, d3], axis=0)
        o_ref[...] = sm * (dsm - jnp.sum(sm * dsm, axis=0, keepdims=True))

    return pl.pallas_call(body, out_shape=jax.ShapeDtypeStruct(logits2d.shape, F32), name=name)(logits2d, dlb)


def _adamw(w, g, m, v, *, name):
    r, c = w.shape
    tr = r if r <= 512 else _pick_rows(r)

    def body(w_ref, g_ref, m_ref, v_ref, d_ref, nm_ref, nv_ref):
        gv = g_ref[...]
        nm = ADAM_B1 * m_ref[...] + (1.0 - ADAM_B1) * gv
        nv = ADAM_B2 * v_ref[...] + (1.0 - ADAM_B2) * (gv * gv)
        m_hat = nm / (1.0 - ADAM_B1 ** ADAM_STEP)
        v_hat = nv / (1.0 - ADAM_B2 ** ADAM_STEP)
        d_ref[...] = -ADAM_LR * (m_hat / (jnp.sqrt(v_hat) + ADAM_EPS) + ADAM_WD * w_ref[...])
        nm_ref[...] = nm
        nv_ref[...] = nv

    blk = pl.BlockSpec((tr, c), lambda i: (i, 0))
    shp = jax.ShapeDtypeStruct((r, c), F32)
    return pl.pallas_call(body, grid=(r // tr,), in_specs=[blk] * 4, out_specs=[blk] * 3, out_shape=[shp] * 3,
                          compiler_params=_cp("parallel"), name=name)(w, g, m, v)


def _pick_rows(r, cap=512):
    best = 8
    for d in range(8, cap + 1, 8):
        if r % d == 0:
            best = d
    assert r % best == 0, r
    return best


def _local_step(x, p, target, w, sp):
    t = x.shape[0]
    tables = _rope_tables(t)
    logits2d = sp["hgrn_lb_logits"].reshape(DEPTH, 2 * D_MODEL)
    lb_all = _lower_bounds(logits2d, name="lb_fwd").reshape(DEPTH, 2, 1, D_MODEL)
    row = lambda a, i: a[i][None]

    saved = []
    xf, xb = x, x.astype(BF16)
    for i in range(DEPTH):
        j = i // 2
        s = dict(xin_bf=xb)
        if i % 2 == 0:
            q, k, v = _qkv_rope(xb, w["att_w_qkv"][j], tables, name=f"qkv_rope_{i}")
            o, lse = _attn_fwd(q, k, v, sp["att_sink"][j], name=f"attn_fwd_{i}")
            s.update(q=q, k=k, v=v, o=o, lse=lse)
            mix_in, w_o = o, w["att_w_o"][j]
        else:
            z = _mm_nn(xb, w["hgrn_w_in"][j], out_dtype=F32, name=f"hgrn_in_{i}")
            o_f, s_f = _hgrn_fwd(z, lb_all[i, 0], False, name=f"hgrn_scan_f_{i}")
            o_b, s_b = _hgrn_fwd(z, lb_all[i, 1], True, name=f"hgrn_scan_b_{i}")
            og = _hgrn_post_fwd(o_f, o_b, z, row(sp["hgrn_norm_g"], j), name=f"hgrn_post_{i}")
            s.update(z=z, o_f=o_f, o_b=o_b, s_f=s_f, s_b=s_b, og=og)
            mix_in, w_o = og, w["hgrn_w_o"][j]
        x1, x1b, xh1, rs1 = _mm_res_ln(mix_in, w_o, xf, row(sp["ln_mix_g"], i), row(sp["ln_mix_b"], i), name=f"mix_out_ln_{i}")
        g, u, h = _ffn_in(x1b, w["ffn_w_in"][i], name=f"ffn_in_{i}")
        x2, x2b, xh2, rs2 = _mm_res_ln(h, w["ffn_w_out"][i], x1, row(sp["ln_ffn_g"], i), row(sp["ln_ffn_b"], i), name=f"ffn_out_ln_{i}")
        xf, xb, gate, pp = _ple_fwd(x2, x2b, p, i, w["ple_w_gate"][i], w["ple_w_proj"][i], name=f"ple_fwd_{i}")
        s.update(x1b=x1b, xh1=xh1, rs1=rs1, g=g, u=u, h=h, x2b=x2b, xh2=xh2, rs2=rs2, gate=gate, pp=pp)
        saved.append(s)

    loss, dx = _loss_head(xf, target, name="loss_head")

    gw = {n: [None] * w[n].shape[0] for n in w}
    gs = {n: [None] * DEPTH for n in ("ln_mix_g", "ln_mix_b", "ln_ffn_g", "ln_ffn_b")}
    gs.update(att_sink=[None] * 2, hgrn_norm_g=[None] * 2)
    dlb = [jnp.zeros((2, D_MODEL), F32)] * DEPTH
    for i in reversed(range(DEPTH)):
        j = i // 2
        s = saved[i]
        du2, du2b, dpre, dpp, gs["ln_ffn_g"][i], gs["ln_ffn_b"][i] = _ple_bwd(
            dx, s["gate"], s["pp"], w["ple_w_gate"][i], s["xh2"], s["rs2"], row(sp["ln_ffn_g"], i), name=f"ple_bwd_{i}")
        gw["ple_w_gate"][i] = _mm_tn(s["x2b"], dpre, name=f"dw_ple_gate_{i}")
        gw["ple_w_proj"][i] = _mm_tn_p(p, i, dpp, name=f"dw_ple_proj_{i}")
        dgg, dgu = _ffn_out_bwd(du2b, w["ffn_w_out"][i], s["g"], s["u"], name=f"ffn_out_bwd_{i}")
        gw["ffn_w_out"][i] = _mm_tn(s["h"], du2b, name=f"dw_ffn_out_{i}")
        du1, du1b, gs["ln_mix_g"][i], gs["ln_mix_b"][i] = _mm_nt(
            [dgg, dgu], w["ffn_w_in"][i], tk=_pick(D_FF, 1408), resid=du2,
            ln=(s["xh1"], s["rs1"], row(sp["ln_mix_g"], i)), name=f"ffn_in_bwd_{i}")
        gw["ffn_w_in"][i] = jnp.concatenate(
            [_mm_tn(s["x1b"], dgg, name=f"dw_ffn_gate_{i}"), _mm_tn(s["x1b"], dgu, name=f"dw_ffn_up_{i}")], axis=1)
        if i % 2 == 0:
            gw["att_w_o"][j] = _mm_tn(s["o"], du1b, name=f"dw_att_o_{i}")
            do = _mm_nt([du1b], w["att_w_o"][j], tk=Q_DIM, out_dtype=BF16, name=f"att_o_bwd_{i}")
            dq, dkw, dvw, gs["att_sink"][j] = _attn_bwd(
                s["q"], s["k"], s["v"], s["o"], do, s["lse"], sp["att_sink"][j], name=f"attn_bwd_{i}")
            dqkv = _attn_post_bwd(dq, dkw, dvw, tables, name=f"attn_post_bwd_{i}")
            gw["att_w_qkv"][j] = _mm_tn(s["xin_bf"], dqkv, name=f"dw_att_qkv_{i}")
            dx = _mm_nt([dqkv], w["att_w_qkv"][j], tk=Q_DIM + 2 * KV_DIM, resid=du1, name=f"qkv_bwd_{i}")
        else:
            gw["hgrn_w_o"][j] = _mm_tn(s["og"], du1b, name=f"dw_hgrn_o_{i}")
            dy = _mm_nt([du1b], w["hgrn_w_o"][j], tk=D_MODEL, out_dtype=BF16, name=f"hgrn_o_bwd_{i}")
            d_o, dgate, gs["hgrn_norm_g"][j] = _hgrn_post_bwd(
                dy, s["o_f"], s["o_b"], s["z"], row(sp["hgrn_norm_g"], j), name=f"hgrn_post_bwd_{i}")
            dq_f, dzf_f, dv_f, dlb_f = _hgrn_bwd(s["z"], lb_all[i, 0], d_o, s["s_f"], False, name=f"hgrn_scan_f_bwd_{i}")
            dq_b, dzf_b, dv_b, dlb_b = _hgrn_bwd(s["z"], lb_all[i, 1], d_o, s["s_b"], True, name=f"hgrn_scan_b_bwd_{i}")
            dlb[i] = jnp.stack([dlb_f.reshape(D_MODEL), dlb_b.reshape(D_MODEL)])
            dz = _hgrn_combine(dq_f, dq_b, dzf_f, dzf_b, dv_f, dv_b, dgate, name=f"hgrn_combine_{i}")
            gw["hgrn_w_in"][j] = _mm_tn(s["xin_bf"], dz, name=f"dw_hgrn_in_{i}")
            dx = _mm_nt([dz], w["hgrn_w_in"][j], tk=_pick(5 * D_MODEL, 1408), resid=du1, name=f"hgrn_in_bwd_{i}")

    gw = {n: jnp.stack(v) for n, v in gw.items()}
    gsmall = {n: jnp.concatenate(v, axis=0) for n, v in gs.items()}
    gsmall["hgrn_lb_logits"] = _lower_bounds_bwd(
        logits2d, jnp.stack(dlb).reshape(DEPTH, 2 * D_MODEL), name="lb_bwd").reshape(DEPTH, 2, D_MODEL)
    return loss, dx, gw, gsmall


ANY = pl.BlockSpec(memory_space=pl.ANY)


def _place():
    x, y, c = lax.axis_index("x"), lax.axis_index("y"), lax.axis_index("c")
    chips = [(1 - x, y), (x, 1 - y), (1 - x, 1 - y)]
    return x, y, c, chips


def _remote(src, dst, send_sem, recv_sem, to):
    return pltpu.make_async_remote_copy(src_ref=src, dst_ref=dst, send_sem=send_sem, recv_sem=recv_sem,
                                        device_id=to, device_id_type=MESH)


def _allgather_weights(packed, *, name):
    r = packed.shape[0]
    hr = r // 2

    def body(src_ref, out_ref, send_sems, recv_sems, local_sem):
        x, y, c, chips = _place()
        sibling = (x, y, 1 - c)

        def half(cx, cy, hc):
            return out_ref.at[2 * cx + cy, pl.ds(hc * hr, hr), :]

        mine = pltpu.make_async_copy(src_ref, out_ref.at[2 * x + y], local_sem)
        mine.start()
        my_half = src_ref.at[pl.ds(c * hr, hr), :]
        first = [_remote(my_half, half(x, y, c), send_sems.at[j], recv_sems.at[j], (*chip, c)) for j, chip in enumerate(chips)]
        for cp in first:
            cp.start()
        passed = [_remote(half(*chip, c), half(*chip, c), send_sems.at[3 + j], recv_sems.at[3 + j], sibling)
                  for j, chip in enumerate(chips)]
        for j, chip in enumerate(chips):
            _remote(my_half, half(*chip, c), send_sems.at[j], recv_sems.at[j], (*chip, c)).wait_recv()
            passed[j].start()
        for j, chip in enumerate(chips):
            _remote(my_half, half(*chip, 1 - c), send_sems.at[3 + j], recv_sems.at[3 + j], sibling).wait_recv()
        for cp in first + passed:
            cp.wait_send()
        mine.wait()

    return pl.pallas_call(
        body, in_specs=[ANY], out_specs=ANY, out_shape=jax.ShapeDtypeStruct((N_CHIPS, r, packed.shape[1]), packed.dtype),
        scratch_shapes=[pltpu.SemaphoreType.DMA((6,)), pltpu.SemaphoreType.DMA((6,)), pltpu.SemaphoreType.DMA],
        name=name)(packed)


def _allreduce_small(block, *, name):
    m, n = block.shape

    def body(x_ref, sum_ref, all_ref, send_sems, recv_sems):
        x, y, c, chips = _place()
        me, sibling = (x, y, c), (x, y, 1 - c)

        def rows(px, py, pc):
            return all_ref.at[pl.ds((4 * px + 2 * py + pc) * m, m), :]

        all_ref[pl.ds((4 * x + 2 * y + c) * m, m), :] = x_ref[...]
        first = [_remote(x_ref, rows(*me), send_sems.at[0], recv_sems.at[0], sibling)]
        first += [_remote(x_ref, rows(*me), send_sems.at[1 + j], recv_sems.at[1 + j], (*chip, c)) for j, chip in enumerate(chips)]
        for cp in first:
            cp.start()
        passed = [_remote(rows(*chip, c), rows(*chip, c), send_sems.at[4 + j], recv_sems.at[4 + j], sibling)
                  for j, chip in enumerate(chips)]
        for j, chip in enumerate(chips):
            _remote(x_ref, rows(*chip, c), send_sems.at[1 + j], recv_sems.at[1 + j], me).wait_recv()
            passed[j].start()
        _remote(x_ref, rows(*sibling), send_sems.at[0], recv_sems.at[0], me).wait_recv()
        for j, chip in enumerate(chips):
            _remote(x_ref, rows(*chip, 1 - c), send_sems.at[4 + j], recv_sems.at[4 + j], me).wait_recv()
        for cp in first + passed:
            cp.wait_send()
        acc = all_ref[pl.ds(0, m), :]
        for d in range(1, 8):
            acc = acc + all_ref[pl.ds(d * m, m), :]
        sum_ref[...] = acc

    vmem = pl.BlockSpec(memory_space=pltpu.VMEM)
    return pl.pallas_call(
        body, in_specs=[vmem], out_specs=vmem, out_shape=jax.ShapeDtypeStruct((m, n), F32),
        scratch_shapes=[pltpu.VMEM((8 * m, n), F32), pltpu.SemaphoreType.DMA((7,)), pltpu.SemaphoreType.DMA((7,))],
        name=name)(block)


def _pair_exchange(g, *, name):
    n, r, w = g.shape
    hr = r // 2

    def body(g_ref, out_ref, send_sem, recv_sem):
        x, y, c, _ = _place()
        cp = _remote(g_ref.at[:, pl.ds((1 - c) * hr, hr), :], out_ref, send_sem, recv_sem, (x, y, 1 - c))
        cp.start()
        cp.wait()

    return pl.pallas_call(
        body, in_specs=[ANY], out_specs=ANY, out_shape=jax.ShapeDtypeStruct((n, hr, w), g.dtype),
        scratch_shapes=[pltpu.SemaphoreType.DMA, pltpu.SemaphoreType.DMA], name=name)(g)


def _chip_scatter(part, *, name):
    n, h, w = part.shape

    def body(p_ref, out_ref, send_sems, recv_sems, local_sem):
        x, y, c, chips = _place()
        me = 2 * x + y
        mine = pltpu.make_async_copy(p_ref.at[me], out_ref.at[me], local_sem)
        mine.start()
        sends = [_remote(p_ref.at[2 * cx + cy], out_ref.at[me], send_sems.at[j], recv_sems.at[j], (cx, cy, c))
                 for j, (cx, cy) in enumerate(chips)]
        for cp in sends:
            cp.start()
        for j, (cx, cy) in enumerate(chips):
            _remote(p_ref.at[me], out_ref.at[2 * cx + cy], send_sems.at[j], recv_sems.at[j], (cx, cy, c)).wait_recv()
        for cp in sends:
            cp.wait_send()
        mine.wait()

    return pl.pallas_call(
        body, in_specs=[ANY], out_specs=ANY, out_shape=jax.ShapeDtypeStruct((n, h, w), part.dtype),
        scratch_shapes=[pltpu.SemaphoreType.DMA((3,)), pltpu.SemaphoreType.DMA((3,)), pltpu.SemaphoreType.DMA],
        name=name)(part)


def _pair_share(half, *, name):
    h, w = half.shape

    def body(h_ref, out_ref, send_sem, recv_sem, local_sem):
        x, y, c, _ = _place()
        dst = out_ref.at[pl.ds(c * h, h), :]
        mine = pltpu.make_async_copy(h_ref, dst, local_sem)
        mine.start()
        cp = _remote(h_ref, dst, send_sem, recv_sem, (x, y, 1 - c))
        cp.start()
        cp.wait_send()
        _remote(h_ref, out_ref.at[pl.ds((1 - c) * h, h), :], send_sem, recv_sem, (x, y, 1 - c)).wait_recv()
        mine.wait()

    return pl.pallas_call(
        body, in_specs=[ANY], out_specs=ANY, out_shape=jax.ShapeDtypeStruct((2 * h, w), half.dtype),
        scratch_shapes=[pltpu.SemaphoreType.DMA, pltpu.SemaphoreType.DMA, pltpu.SemaphoreType.DMA], name=name)(half)


def _add_own_half(g, recv, c, *, name):
    n, r, w = g.shape
    hr = r // 2
    tr = _pick_rows(hr, 1024)
    nr = hr // tr

    def body(c_ref, g_ref, r_ref, o_ref):
        o_ref[...] = g_ref[...] + r_ref[...]

    return pl.pallas_call(
        body,
        grid_spec=pltpu.PrefetchScalarGridSpec(
            num_scalar_prefetch=1, grid=(n, nr),
            in_specs=[pl.BlockSpec((None, tr, w), lambda s, i, c_ref: (s, c_ref[0] * nr + i, 0)),
                      pl.BlockSpec((None, tr, w), lambda s, i, c_ref: (s, i, 0))],
            out_specs=pl.BlockSpec((None, tr, w), lambda s, i, c_ref: (s, i, 0))),
        out_shape=jax.ShapeDtypeStruct((n, hr, w), F32),
        compiler_params=_cp("parallel", "parallel"), name=name)(c, g, recv)


def _sum_chips(q, *, name):
    n, h, w = q.shape
    tr = _pick_rows(h, 1024)

    def body(a, b, c, d, o_ref):
        o_ref[...] = ((a[...] + b[...]) + c[...]) + d[...]

    specs = [pl.BlockSpec((None, tr, w), functools.partial(lambda i, s: (s, i, 0), s=s)) for s in range(n)]
    return pl.pallas_call(
        body, grid=(h // tr,), in_specs=specs, out_specs=pl.BlockSpec((tr, w), lambda i: (i, 0)),
        out_shape=jax.ShapeDtypeStruct((h, w), F32), compiler_params=_cp("parallel"), name=name)(q, q, q, q)


PACK_W = 1024
BIG = (("att_w_qkv", 2), ("att_w_o", 1), ("hgrn_w_in", 2), ("hgrn_w_o", 1),
       ("ffn_w_in", 2), ("ffn_w_out", 1), ("ple_w_gate", 1), ("ple_w_proj", 2))
LB_ROWS = 32


def _pack_shards(shards):
    return jnp.concatenate([shards[n].reshape(-1, PACK_W) for n, _ in BIG], axis=0)


def _unpack_shards(buf, shapes):
    out, r0 = {}, 0
    for n, _ in BIG:
        nr = shapes[n][0] * shapes[n][1] * shapes[n][2] // PACK_W
        out[n] = buf[r0:r0 + nr].reshape(shapes[n])
        r0 += nr
    return out


def _gathered_to_full(buf, shapes):
    out, r0 = {}, 0
    for n, axis in BIG:
        l, r, c = shapes[n]
        nr = l * r * c // PACK_W
        sh = buf[:, r0:r0 + nr].reshape(N_CHIPS, l, r, c)
        out[n] = (sh.transpose(1, 0, 2, 3).reshape(l, N_CHIPS * r, c) if axis == 1
                  else sh.transpose(1, 2, 0, 3).reshape(l, r, N_CHIPS * c))
        r0 += nr
    return out, r0


def _full_to_slabs(full, shapes):
    parts = []
    for n, axis in BIG:
        l, r, c = shapes[n]
        g = full[n]
        g = (g.reshape(l, N_CHIPS, r, c).transpose(1, 0, 2, 3) if axis == 1
             else g.reshape(l, r, N_CHIPS, c).transpose(2, 0, 1, 3))
        parts.append(g.reshape(N_CHIPS, -1, PACK_W))
    return jnp.concatenate(parts, axis=1)


SMALL = ("ln_mix_g", "ln_mix_b", "ln_ffn_g", "ln_ffn_b")
SMALL_ROWS = 32


def _pack_small(vals, lb):
    rows = [vals[n] for n in SMALL] + [lb.reshape(-1, PACK_W)]
    for n in ("att_sink", "hgrn_norm_g"):
        flat = vals[n].reshape(1, -1)
        rows.append(jnp.pad(flat, ((0, 0), (0, PACK_W - flat.shape[1]))))
    buf = jnp.concatenate(rows, axis=0)
    return jnp.pad(buf, ((0, SMALL_ROWS - buf.shape[0]), (0, 0)))


def _unpack_small(buf, lb_shape):
    out, r0 = {}, 0
    for n in SMALL:
        out[n] = buf[r0:r0 + DEPTH]
        r0 += DEPTH
    nlb = lb_shape[0] * lb_shape[1] * lb_shape[2] // PACK_W
    out["hgrn_lb_logits"] = buf[r0:r0 + nlb].reshape(lb_shape)
    r0 += nlb
    out["att_sink"] = buf[r0, :2 * N_Q_HEADS].reshape(2, N_Q_HEADS)
    out["hgrn_norm_g"] = buf[r0 + 1, :2 * LANES].reshape(2, LANES)
    return out


WEIGHTS = ("att_w_qkv", "att_sink", "att_w_o", "hgrn_w_in", "hgrn_lb_logits", "hgrn_norm_g", "hgrn_w_o", "ln_mix_g",
           "ln_mix_b", "ffn_w_in", "ffn_w_out", "ln_ffn_g", "ln_ffn_b", "ple_w_gate", "ple_w_proj")


def kernel(x, p, att_w_qkv, att_sink, att_w_o, hgrn_w_in, hgrn_lb_logits, hgrn_norm_g, hgrn_w_o, ln_mix_g, ln_mix_b, ffn_w_in, ffn_w_out, ln_ffn_g, ln_ffn_b, ple_w_gate, ple_w_proj, loss_target, m_att_w_qkv, m_att_sink, m_att_w_o, m_hgrn_w_in, m_hgrn_lb_logits, m_hgrn_norm_g, m_hgrn_w_o, m_ln_mix_g, m_ln_mix_b, m_ffn_w_in, m_ffn_w_out, m_ln_ffn_g, m_ln_ffn_b, m_ple_w_gate, m_ple_w_proj, v_att_w_qkv, v_att_sink, v_att_w_o, v_hgrn_w_in, v_hgrn_lb_logits, v_hgrn_norm_g, v_hgrn_w_o, v_ln_mix_g, v_ln_mix_b, v_ffn_w_in, v_ffn_w_out, v_ln_ffn_g, v_ln_ffn_b, v_ple_w_gate, v_ple_w_proj):
    wts = dict(att_w_qkv=att_w_qkv, att_sink=att_sink, att_w_o=att_w_o, hgrn_w_in=hgrn_w_in, hgrn_lb_logits=hgrn_lb_logits,
               hgrn_norm_g=hgrn_norm_g, hgrn_w_o=hgrn_w_o, ln_mix_g=ln_mix_g, ln_mix_b=ln_mix_b, ffn_w_in=ffn_w_in,
               ffn_w_out=ffn_w_out, ln_ffn_g=ln_ffn_g, ln_ffn_b=ln_ffn_b, ple_w_gate=ple_w_gate, ple_w_proj=ple_w_proj)
    mom = dict(att_w_qkv=m_att_w_qkv, att_sink=m_att_sink, att_w_o=m_att_w_o, hgrn_w_in=m_hgrn_w_in, hgrn_lb_logits=m_hgrn_lb_logits,
               hgrn_norm_g=m_hgrn_norm_g, hgrn_w_o=m_hgrn_w_o, ln_mix_g=m_ln_mix_g, ln_mix_b=m_ln_mix_b, ffn_w_in=m_ffn_w_in,
               ffn_w_out=m_ffn_w_out, ln_ffn_g=m_ln_ffn_g, ln_ffn_b=m_ln_ffn_b, ple_w_gate=m_ple_w_gate, ple_w_proj=m_ple_w_proj)
    var = dict(att_w_qkv=v_att_w_qkv, att_sink=v_att_sink, att_w_o=v_att_w_o, hgrn_w_in=v_hgrn_w_in, hgrn_lb_logits=v_hgrn_lb_logits,
               hgrn_norm_g=v_hgrn_norm_g, hgrn_w_o=v_hgrn_w_o, ln_mix_g=v_ln_mix_g, ln_mix_b=v_ln_mix_b, ffn_w_in=v_ffn_w_in,
               ffn_w_out=v_ffn_w_out, ln_ffn_g=v_ln_ffn_g, ln_ffn_b=v_ln_ffn_b, ple_w_gate=v_ple_w_gate, ple_w_proj=v_ple_w_proj)
    shapes = {n: wts[n].shape for n, _ in BIG}
    chip = 2 * lax.axis_index("x") + lax.axis_index("y")
    core = lax.axis_index("c")

    lb_bits = lax.bitcast_convert_type(hgrn_lb_logits.reshape(-1), BF16).reshape(-1, PACK_W)
    packed = jnp.concatenate([_pack_shards({n: wts[n].astype(BF16) for n, _ in BIG}), lb_bits,
                              jnp.zeros((LB_ROWS - lb_bits.shape[0], PACK_W), BF16)], axis=0)
    gathered = _allgather_weights(packed, name="allgather_weights")
    w_full, r_big = _gathered_to_full(gathered, shapes)
    lb_sh = hgrn_lb_logits.shape
    lb_rows = gathered[:, r_big:r_big + lb_bits.shape[0]].reshape(N_CHIPS, -1, 2)
    lb_full = lax.bitcast_convert_type(lb_rows, F32).reshape(N_CHIPS, lb_sh[0], lb_sh[1], lb_sh[2])
    lb_full = lb_full.transpose(1, 2, 0, 3).reshape(lb_sh[0], lb_sh[1], N_CHIPS * lb_sh[2])
    sp = dict(att_sink=att_sink, hgrn_lb_logits=lb_full, hgrn_norm_g=hgrn_norm_g, ln_mix_g=ln_mix_g, ln_mix_b=ln_mix_b,
              ln_ffn_g=ln_ffn_g, ln_ffn_b=ln_ffn_b)

    loss, grad_x, gw, gs = _local_step(x[0], p, loss_target[0], w_full, sp)
    loss = lax.psum(loss[0, 0], ("x", "y", "c"))

    slabs = _full_to_slabs(gw, shapes)
    from_sibling = _pair_exchange(slabs, name="grad_pair_exchange")
    chip_part = _add_own_half(slabs, from_sibling, core.reshape(1).astype(jnp.int32), name="grad_pair_add")
    from_chips = _chip_scatter(chip_part, name="grad_chip_scatter")
    half = _sum_chips(from_chips, name="grad_chip_sum")
    g_big = _unpack_shards(_pair_share(half, name="grad_pair_share"), shapes)

    g_small_full = _unpack_small(_allreduce_small(_pack_small(gs, gs["hgrn_lb_logits"]), name="allreduce_small"),
                                 (lb_sh[0], lb_sh[1], N_CHIPS * lb_sh[2]))
    g_lb = lax.dynamic_slice_in_dim(g_small_full["hgrn_lb_logits"], chip * lb_sh[2], lb_sh[2], axis=2)
    grads = dict(g_big)
    grads.update({n: g_small_full[n] for n in SMALL + ("att_sink", "hgrn_norm_g")})
    grads["hgrn_lb_logits"] = g_lb

    delta, new_m, new_v = {}, {}, {}
    for n, _ in BIG:
        two_d = lambda a: a.reshape(-1, a.shape[-1])
        d, nm, nv = _adamw(two_d(wts[n]), two_d(grads[n]), two_d(mom[n]), two_d(var[n]), name=f"adamw_{n}")
        delta[n], new_m[n], new_v[n] = d.reshape(shapes[n]), nm.reshape(shapes[n]), nv.reshape(shapes[n])
    packs = [_pack_small(src, src["hgrn_lb_logits"]) for src in (wts, grads, mom, var)]
    outs = _adamw(*packs, name="adamw_small")
    for dst, buf in zip((delta, new_m, new_v), outs):
        dst.update(_unpack_small(buf, lb_sh))

    return (loss, grad_x[None], *[grads[n] for n in WEIGHTS], *[delta[n] for n in WEIGHTS],
            *[new_m[n] for n in WEIGHTS], *[new_v[n] for n in WEIGHTS])
```

```python
import functools

import jax
import jax.numpy as jnp
from jax import lax
from jax.experimental import pallas as pl
from jax.experimental.pallas import tpu as pltpu

F32, BF16 = jnp.float32, jnp.bfloat16

D_MODEL = 1024
DEPTH = 4
HEAD_DIM = 64
N_Q_HEADS = 16
N_KV_HEADS = 4
GROUP = 4
Q_DIM = 1024
KV_DIM = 256
WINDOW = 128
ROPE_DIM = 16
ROPE_THETA = 500000.0
HGRN_HEADS = 8
HGRN_KEY = 128
HGRN_CHUNK = 64
D_FF = 2816
PLE_DIM = 256
ALPHA = (2 * DEPTH) ** 0.25
LN_EPS = 1e-5
ADAM_LR, ADAM_B1, ADAM_B2, ADAM_EPS, ADAM_WD, ADAM_STEP = 0.001, 0.9, 0.999, 1e-08, 0.01, 10

LANES = 128
VMEM_LIMIT_BYTES = 56 * 2**20
FACTORED_DECAY_LIMIT = -80.0

NN = ((1,), (0,))
NT = ((1,), (1,))
TN = ((0,), (0,))

N_CHIPS = 4
MESH = pl.DeviceIdType.MESH


def _dot(a, b, dims):
    return lax.dot_general(a, b, (dims, ((), ())), preferred_element_type=F32)


def _cp(*sem):
    return pltpu.CompilerParams(dimension_semantics=sem, vmem_limit_bytes=VMEM_LIMIT_BYTES)


def _rows(t, cap=512):
    return min(cap, t)


def _pick(n, cap):
    best = None
    for d in range(LANES, min(n, cap) + 1, LANES):
        if n % d == 0:
            best = d
    assert best is not None, (n, cap)
    return best


def _sigmoid(x):
    return jax.nn.sigmoid(x)


def _ln_fwd(u, g, b):
    mu = jnp.mean(u, axis=-1, keepdims=True)
    xc = u - mu
    var = jnp.mean(xc * xc, axis=-1, keepdims=True)
    rstd = lax.rsqrt(var + LN_EPS)
    xhat = xc * rstd
    return xhat * g + b, xhat, rstd


def _ln_bwd(dy, xhat, rstd, g):
    dxh = dy * g
    m1 = jnp.mean(dxh, axis=-1, keepdims=True)
    m2 = jnp.mean(dxh * xhat, axis=-1, keepdims=True)
    du = rstd * (dxh - m1 - xhat * m2)
    return du, jnp.sum(dy * xhat, axis=0, keepdims=True), jnp.sum(dy, axis=0, keepdims=True)


def _full(shape):
    nd = len(shape)
    return pl.BlockSpec(shape, lambda *_: (0,) * nd)


def _mm_nn(a, w, *, out_dtype, name):
    t, k = a.shape
    n = w.shape[1]
    tm, tn = _rows(t), _pick(n, 1408)

    def body(a_ref, w_ref, o_ref):
        o_ref[...] = _dot(a_ref[...], w_ref[...], NN).astype(out_dtype)

    return pl.pallas_call(
        body, grid=(n // tn, t // tm),
        in_specs=[pl.BlockSpec((tm, k), lambda j, i: (i, 0)), pl.BlockSpec((k, tn), lambda j, i: (0, j))],
        out_specs=pl.BlockSpec((tm, tn), lambda j, i: (i, j)),
        out_shape=jax.ShapeDtypeStruct((t, n), out_dtype),
        compiler_params=_cp("parallel", "parallel"), name=name)(a, w)


def _mm_tn(a, b, *, name):
    r, m = a.shape
    n = b.shape[1]
    tr, tm, tn = _rows(r), _pick(m, 1408), _pick(n, 1408)

    def body(a_ref, b_ref, o_ref):
        @pl.when(pl.program_id(2) == 0)
        def _():
            o_ref[...] = jnp.zeros_like(o_ref)
        o_ref[...] += _dot(a_ref[...].astype(BF16), b_ref[...].astype(BF16), TN)

    return pl.pallas_call(
        body, grid=(m // tm, n // tn, r // tr),
        in_specs=[pl.BlockSpec((tr, tm), lambda i, j, k: (k, i)), pl.BlockSpec((tr, tn), lambda i, j, k: (k, j))],
        out_specs=pl.BlockSpec((tm, tn), lambda i, j, k: (i, j)),
        out_shape=jax.ShapeDtypeStruct((m, n), F32),
        compiler_params=_cp("parallel", "parallel", "arbitrary"), name=name)(a, b)


def _mm_tn_p(p, layer, b, *, name):
    t = p.shape[2]
    n = b.shape[1]
    tr = _rows(t)

    def body(a_ref, b_ref, o_ref):
        @pl.when(pl.program_id(0) == 0)
        def _():
            o_ref[...] = jnp.zeros_like(o_ref)
        o_ref[...] += _dot(a_ref[...].astype(BF16), b_ref[...], TN)

    return pl.pallas_call(
        body, grid=(t // tr,),
        in_specs=[pl.BlockSpec((None, None, tr, PLE_DIM), lambda k: (layer, 0, k, 0)),
                  pl.BlockSpec((tr, n), lambda k: (k, 0))],
        out_specs=pl.BlockSpec((PLE_DIM, n), lambda k: (0, 0)),
        out_shape=jax.ShapeDtypeStruct((PLE_DIM, n), F32),
        compiler_params=_cp("arbitrary"), name=name)(p, b)


def _mm_res_ln(a, w, resid, g, b, *, name):
    t, k = a.shape
    tm = _rows(t)

    def body(a_ref, w_ref, r_ref, g_ref, b_ref, y_ref, ybf_ref, xh_ref, rs_ref):
        acc = _dot(a_ref[...], w_ref[...], NN)
        y, xh, rs = _ln_fwd(ALPHA * r_ref[...] + acc, g_ref[...], b_ref[...])
        y_ref[...] = y
        ybf_ref[...] = y.astype(BF16)
        xh_ref[...] = xh
        rs_ref[...] = rs

    row = pl.BlockSpec((tm, D_MODEL), lambda i: (i, 0))
    return pl.pallas_call(
        body, grid=(t // tm,),
        in_specs=[pl.BlockSpec((tm, k), lambda i: (i, 0)), _full((k, D_MODEL)), row, _full((1, D_MODEL)), _full((1, D_MODEL))],
        out_specs=[row, row, row, pl.BlockSpec((tm, 1), lambda i: (i, 0))],
        out_shape=[jax.ShapeDtypeStruct((t, D_MODEL), F32), jax.ShapeDtypeStruct((t, D_MODEL), BF16),
                   jax.ShapeDtypeStruct((t, D_MODEL), F32), jax.ShapeDtypeStruct((t, 1), F32)],
        compiler_params=_cp("parallel"), name=name)(a, w, resid, g, b)


def _ffn_in(x_bf, w, *, name):
    t = x_bf.shape[0]
    tm, tn = _rows(t), _pick(D_FF, 1408)
    nb = D_FF // tn

    def body(x_ref, wg_ref, wu_ref, g_ref, u_ref, h_ref):
        x = x_ref[...]
        g = _dot(x, wg_ref[...], NN)
        u = _dot(x, wu_ref[...], NN)
        g_ref[...] = g.astype(BF16)
        u_ref[...] = u.astype(BF16)
        h_ref[...] = (g * _sigmoid(g) * u).astype(BF16)

    tile = pl.BlockSpec((tm, tn), lambda j, i: (i, j))
    shp = jax.ShapeDtypeStruct((t, D_FF), BF16)
    return pl.pallas_call(
        body, grid=(nb, t // tm),
        in_specs=[pl.BlockSpec((tm, D_MODEL), lambda j, i: (i, 0)),
                  pl.BlockSpec((D_MODEL, tn), lambda j, i: (0, j)),
                  pl.BlockSpec((D_MODEL, tn), lambda j, i: (0, j + nb))],
        out_specs=[tile, tile, tile], out_shape=[shp, shp, shp],
        compiler_params=_cp("parallel", "parallel"), name=name)(x_bf, w, w)


def _ple_fwd(x, x_bf, p, layer, wg, wp, *, name):
    t = x.shape[0]
    tm = _rows(t)

    def body(x_ref, xbf_ref, p_ref, wg_ref, wp_ref, y_ref, ybf_ref, gate_ref, pp_ref):
        gate = _sigmoid(_dot(xbf_ref[...], wg_ref[...], NN))
        pp = _dot(p_ref[...].astype(BF16), wp_ref[...], NN)
        y = x_ref[...] + gate * pp
        y_ref[...] = y
        ybf_ref[...] = y.astype(BF16)
        gate_ref[...] = gate.astype(BF16)
        pp_ref[...] = pp.astype(BF16)

    row = pl.BlockSpec((tm, D_MODEL), lambda i: (i, 0))
    f32, bf = jax.ShapeDtypeStruct((t, D_MODEL), F32), jax.ShapeDtypeStruct((t, D_MODEL), BF16)
    return pl.pallas_call(
        body, grid=(t // tm,),
        in_specs=[row, row, pl.BlockSpec((None, None, tm, PLE_DIM), lambda i: (layer, 0, i, 0)),
                  _full((D_MODEL, D_MODEL)), _full((PLE_DIM, D_MODEL))],
        out_specs=[row, row, row, row], out_shape=[f32, bf, bf, bf],
        compiler_params=_cp("parallel"), name=name)(x, x_bf, p, wg, wp)


def _loss_head(y, target, *, name):
    t = y.shape[0]
    tm = _rows(t)

    def body(y_ref, t_ref, loss_ref, dy_ref):
        e = y_ref[...] - t_ref[...]

        @pl.when(pl.program_id(0) == 0)
        def _():
            loss_ref[...] = jnp.zeros_like(loss_ref)
        sq = jnp.sum(jnp.sum(e * e, axis=1, keepdims=True), axis=0, keepdims=True)
        loss_ref[...] += sq * (0.5 / D_MODEL)
        dy_ref[...] = e * (1.0 / D_MODEL)

    row = pl.BlockSpec((tm, D_MODEL), lambda i: (i, 0))
    return pl.pallas_call(
        body, grid=(t // tm,), in_specs=[row, row],
        out_specs=[_full((1, 1)), row],
        out_shape=[jax.ShapeDtypeStruct((1, 1), F32), jax.ShapeDtypeStruct((t, D_MODEL), F32)],
        compiler_params=_cp("arbitrary"), name=name)(y, target)


def _ple_bwd(dx3, gate, pp, wg, xhat, rstd, g, *, name):
    t = dx3.shape[0]
    tm = _rows(t)

    def body(dx_ref, gate_ref, pp_ref, wg_ref, xh_ref, rs_ref, g_ref,
             du_ref, dubf_ref, dpre_ref, dpp_ref, dg_ref, db_ref):
        dx = dx_ref[...]
        gate = gate_ref[...].astype(F32)
        dpre = (dx * pp_ref[...].astype(F32) * gate * (1.0 - gate)).astype(BF16)
        dpre_ref[...] = dpre
        dpp_ref[...] = (dx * gate).astype(BF16)
        dx2 = dx + _dot(dpre, wg_ref[...], NT)
        du, dg, db = _ln_bwd(dx2, xh_ref[...], rs_ref[...], g_ref[...])
        du_ref[...] = du
        dubf_ref[...] = du.astype(BF16)

        @pl.when(pl.program_id(0) == 0)
        def _():
            dg_ref[...] = jnp.zeros_like(dg_ref)
            db_ref[...] = jnp.zeros_like(db_ref)
        dg_ref[...] += dg
        db_ref[...] += db

    row = pl.BlockSpec((tm, D_MODEL), lambda i: (i, 0))
    vec = _full((1, D_MODEL))
    f32, bf = jax.ShapeDtypeStruct((t, D_MODEL), F32), jax.ShapeDtypeStruct((t, D_MODEL), BF16)
    v32 = jax.ShapeDtypeStruct((1, D_MODEL), F32)
    return pl.pallas_call(
        body, grid=(t // tm,),
        in_specs=[row, row, row, _full((D_MODEL, D_MODEL)), row, pl.BlockSpec((tm, 1), lambda i: (i, 0)), vec],
        out_specs=[row, row, row, row, vec, vec], out_shape=[f32, bf, bf, bf, v32, v32],
        compiler_params=_cp("arbitrary"), name=name)(dx3, gate, pp, wg, xhat, rstd, g)


def _ffn_out_bwd(du_bf, w_out, g, u, *, name):
    t = du_bf.shape[0]
    tm, tn = _rows(t), _pick(D_FF, 1408)

    def body(du_ref, w_ref, g_ref, u_ref, dg_ref, dup_ref):
        dh = _dot(du_ref[...], w_ref[...], NT)
        gv = g_ref[...].astype(F32)
        sig = _sigmoid(gv)
        dg_ref[...] = (dh * u_ref[...].astype(F32) * sig * (1.0 + gv * (1.0 - sig))).astype(BF16)
        dup_ref[...] = (dh * gv * sig).astype(BF16)

    tile = pl.BlockSpec((tm, tn), lambda j, i: (i, j))
    shp = jax.ShapeDtypeStruct((t, D_FF), BF16)
    return pl.pallas_call(
        body, grid=(D_FF // tn, t // tm),
        in_specs=[pl.BlockSpec((tm, D_MODEL), lambda j, i: (i, 0)), pl.BlockSpec((tn, D_MODEL), lambda j, i: (j, 0)),
                  tile, tile],
        out_specs=[tile, tile], out_shape=[shp, shp],
        compiler_params=_cp("parallel", "parallel"), name=name)(du_bf, w_out, g, u)


def _mm_nt(parts, w, *, tk, resid=None, ln=None, out_dtype=F32, name):
    t, kp = parts[0].shape
    npart = len(parts)
    nk = kp // tk
    nkt = nk * npart
    tm = _rows(t)
    n_in = npart + 1 + (resid is not None) + (3 if ln is not None else 0)

    def body(*refs):
        a_refs, w_ref = refs[:npart], refs[npart]
        pos = npart + 1
        r_ref = None
        if resid is not None:
            r_ref = refs[pos]
            pos += 1
        if ln is not None:
            xh_ref, rs_ref, g_ref = refs[pos:pos + 3]
        outs, acc_ref = refs[n_in:-1], refs[-1]
        i, k = pl.program_id(0), pl.program_id(1)

        @pl.when(k == 0)
        def _():
            acc_ref[...] = jnp.zeros_like(acc_ref)

        for pi in range(npart):
            @pl.when((k >= pi * nk) & (k < (pi + 1) * nk))
            def _(pi=pi):
                acc_ref[...] += _dot(a_refs[pi][...], w_ref[...], NT)

        @pl.when(k == nkt - 1)
        def _():
            val = acc_ref[...]
            if r_ref is not None:
                val = val + ALPHA * r_ref[...]
            if ln is None:
                outs[0][...] = val.astype(out_dtype)
            else:
                du, dg, db = _ln_bwd(val, xh_ref[...], rs_ref[...], g_ref[...])
                outs[0][...] = du
                outs[1][...] = du.astype(BF16)

                @pl.when(i == 0)
                def _():
                    outs[2][...] = jnp.zeros_like(outs[2])
                    outs[3][...] = jnp.zeros_like(outs[3])
                outs[2][...] += dg
                outs[3][...] += db

    row = pl.BlockSpec((tm, D_MODEL), lambda i, k: (i, 0))
    vec = pl.BlockSpec((1, D_MODEL), lambda i, k: (0, 0))
    in_specs = [pl.BlockSpec((tm, tk), functools.partial(lambda i, k, lo: (i, jnp.clip(k - lo, 0, nk - 1)), lo=pi * nk))
                for pi in range(npart)]
    in_specs.append(pl.BlockSpec((D_MODEL, tk), lambda i, k: (0, k)))
    args = list(parts) + [w]
    if resid is not None:
        in_specs.append(row)
        args.append(resid)
    if ln is not None:
        in_specs += [row, pl.BlockSpec((tm, 1), lambda i, k: (i, 0)), vec]
        args += list(ln)
        out_specs = [row, row, vec, vec]
        out_shape = [jax.ShapeDtypeStruct((t, D_MODEL), F32), jax.ShapeDtypeStruct((t, D_MODEL), BF16),
                     jax.ShapeDtypeStruct((1, D_MODEL), F32), jax.ShapeDtypeStruct((1, D_MODEL), F32)]
    else:
        out_specs = [row]
        out_shape = [jax.ShapeDtypeStruct((t, D_MODEL), out_dtype)]
    res = pl.pallas_call(
        body, grid=(t // tm, nkt), in_specs=in_specs, out_specs=out_specs, out_shape=out_shape,
        scratch_shapes=[pltpu.VMEM((tm, D_MODEL), F32)],
        compiler_params=_cp("arbitrary", "arbitrary"), name=name)(*args)
    return res if ln is not None else res[0]


def _rope_tables(t):
    half = ROPE_DIM // 2
    inv = ROPE_THETA ** (-jnp.arange(0, ROPE_DIM, 2, dtype=F32) / ROPE_DIM)
    ang = jnp.arange(t, dtype=F32)[:, None] * inv[None, :]
    cos, sin = jnp.cos(ang), jnp.sin(ang)
    pad = HEAD_DIM - ROPE_DIM
    c = jnp.concatenate([cos, cos, jnp.ones((t, pad), F32)], axis=1)
    s_hi = jnp.concatenate([jnp.zeros((t, half), F32), sin, jnp.zeros((t, pad), F32)], axis=1)
    s_lo = jnp.concatenate([-sin, jnp.zeros((t, half + pad), F32)], axis=1)
    rep = LANES // HEAD_DIM
    return jnp.tile(c, (1, rep)), jnp.tile(s_hi, (1, rep)), jnp.tile(s_lo, (1, rep))


def _rope(x, c, s_hi, s_lo, sign):
    half = ROPE_DIM // 2
    parts = []
    for j in range(x.shape[1] // LANES):
        xg = x[:, j * LANES:(j + 1) * LANES]
        rot = pltpu.roll(xg, half, 1) * s_hi + pltpu.roll(xg, LANES - half, 1) * s_lo
        parts.append(xg * c + sign * rot)
    return jnp.concatenate(parts, axis=1)


def _qkv_rope(x_bf, w, tables, *, name):
    t = x_bf.shape[0]
    tm = _rows(t)
    n = Q_DIM + 2 * KV_DIM

    def body(x_ref, w_ref, c_ref, sh_ref, sl_ref, q_ref, k_ref, v_ref):
        acc = _dot(x_ref[...], w_ref[...], NN)
        c, sh, sl = c_ref[...], sh_ref[...], sl_ref[...]
        q_ref[...] = (_rope(acc[:, :Q_DIM], c, sh, sl, 1.0) * HEAD_DIM ** -0.5).astype(BF16)
        k_ref[...] = _rope(acc[:, Q_DIM:Q_DIM + KV_DIM], c, sh, sl, 1.0).astype(BF16)
        v_ref[...] = acc[:, Q_DIM + KV_DIM:].astype(BF16)

    tab = pl.BlockSpec((tm, LANES), lambda i: (i, 0))
    return pl.pallas_call(
        body, grid=(t // tm,),
        in_specs=[pl.BlockSpec((tm, D_MODEL), lambda i: (i, 0)), _full((D_MODEL, n)), tab, tab, tab],
        out_specs=[pl.BlockSpec((tm, Q_DIM), lambda i: (i, 0)), pl.BlockSpec((tm, KV_DIM), lambda i: (i, 0)),
                   pl.BlockSpec((tm, KV_DIM), lambda i: (i, 0))],
        out_shape=[jax.ShapeDtypeStruct((t, Q_DIM), BF16), jax.ShapeDtypeStruct((t, KV_DIM), BF16),
                   jax.ShapeDtypeStruct((t, KV_DIM), BF16)],
        compiler_params=_cp("parallel"), name=name)(x_bf, w, *tables)


ATT_ROWS = 256
ATT_STRIP = 64


def _window_specs(t, tq):
    r = tq // WINDOW
    last = t // WINDOW - 1
    return [pl.BlockSpec((WINDOW, KV_DIM), lambda i: (jnp.maximum(i * r - 1, 0), 0)),
            pl.BlockSpec((tq, KV_DIM), lambda i: (i, 0)),
            pl.BlockSpec((WINDOW, KV_DIM), lambda i: (jnp.minimum((i + 1) * r, last), 0))]


def _att_valid(base, t):
    rows = GROUP * WINDOW
    qpos = base + (lax.broadcasted_iota(jnp.int32, (rows, 3 * WINDOW), 0) & (WINDOW - 1))
    kpos = base - WINDOW + lax.broadcasted_iota(jnp.int32, (rows, 3 * WINDOW), 1)
    return (jnp.abs(qpos - kpos) <= WINDOW) & (kpos >= 0) & (kpos < t)


def _stack_heads(x, r0, g):
    return jnp.concatenate(
        [x[r0:r0 + WINDOW, (GROUP * g + h) * HEAD_DIM:(GROUP * g + h + 1) * HEAD_DIM] for h in range(GROUP)], axis=0)


def _sink_column(sink_ref, g):
    return jnp.concatenate([jnp.full((WINDOW, 1), sink_ref[GROUP * g + h], F32) for h in range(GROUP)], axis=0)


def _unstack_heads(pieces):
    return jnp.concatenate([pieces[g][h * WINDOW:(h + 1) * WINDOW] for g in range(N_KV_HEADS) for h in range(GROUP)], axis=1)


def _attn_fwd(q, k, v, sink, *, name):
    t = q.shape[0]
    tq = min(ATT_ROWS, t)
    nsb = tq // WINDOW

    def body(sink_ref, q_ref, kp_ref, kc_ref, kn_ref, vp_ref, vc_ref, vn_ref, o_ref, l_ref):
        i = pl.program_id(0)
        qv = q_ref[...]
        kw = jnp.concatenate([kp_ref[...], kc_ref[...], kn_ref[...]], axis=0)
        vw = jnp.concatenate([vp_ref[...], vc_ref[...], vn_ref[...]], axis=0)
        ones = jnp.ones((3 * WINDOW, HEAD_DIM), BF16)
        for sb in range(nsb):
            r0 = sb * WINDOW
            bias =jnp.where(_att_valid(i * tq + r0, t)[:WINDOW], 0.0, -1e30)
            pieces = []
            for hh in range(N_Q_HEADS):
                g, h = hh // GROUP, hh % GROUP
                qh = qv[r0:r0 + WINDOW, hh * HEAD_DIM:(hh + 1) * HEAD_DIM]
                kg = kw[r0:r0 + 3 * WINDOW, g * HEAD_DIM:(g + 1) * HEAD_DIM]
                vg = jnp.concatenate([vw[r0:r0 + 3 * WINDOW, g * HEAD_DIM:(g + 1) * HEAD_DIM], ones], axis=1)
                s = _dot(qh, kg, NT) + bias
                snk = sink_ref[hh]
                m = jnp.maximum(jnp.max(s, axis=1, keepdims=True), snk)
                ov = _dot(jnp.exp(s - m).astype(BF16), vg, NN)
                den = ov[:, HEAD_DIM:HEAD_DIM + 1] + jnp.exp(snk - m)
                pieces.append(ov[:, :HEAD_DIM] * (1.0 / den))
                l_ref[g, sb, h * WINDOW:(h + 1) * WINDOW, :] = m + jnp.log(den)
            o_ref[r0:r0 + WINDOW, :] = jnp.concatenate(pieces, axis=1).astype(BF16)

    return pl.pallas_call(
        body, grid=(t // tq,),
        in_specs=[pl.BlockSpec(memory_space=pltpu.SMEM), pl.BlockSpec((tq, Q_DIM), lambda i: (i, 0))]
        + _window_specs(t, tq) + _window_specs(t, tq),
        out_specs=[pl.BlockSpec((tq, Q_DIM), lambda i: (i, 0)),
                   pl.BlockSpec((N_KV_HEADS, nsb, GROUP * WINDOW, 1), lambda i: (0, i, 0, 0))],
        out_shape=[jax.ShapeDtypeStruct((t, Q_DIM), BF16),
                   jax.ShapeDtypeStruct((N_KV_HEADS, t // WINDOW, GROUP * WINDOW, 1), F32)],
        compiler_params=_cp("parallel"), name=name)(sink, q, k, k, k, v, v, v)


def _attn_bwd(q, k, v, o, do, lse, sink, *, name):
    t = q.shape[0]
    tq = min(ATT_ROWS, t)
    nsb = tq // WINDOW

    def body(sink_ref, q_ref, o_ref, do_ref, l_ref, kp_ref, kc_ref, kn_ref, vp_ref, vc_ref, vn_ref,
             dq_ref, dkw_ref, dvw_ref, dsink_ref):
        i = pl.program_id(0)
        qv, ov, dov = q_ref[...], o_ref[...], do_ref[...]
        kw = jnp.concatenate([kp_ref[...], kc_ref[...], kn_ref[...]], axis=0)
        vw = jnp.concatenate([vp_ref[...], vc_ref[...], vn_ref[...]], axis=0)
        lane16 = lax.broadcasted_iota(jnp.int32, (1, N_Q_HEADS), 1)
        dsink = jnp.zeros((1, N_Q_HEADS), F32)
        for sb in range(nsb):
            r0 = sb * WINDOW
            valid = _att_valid(i * tq + r0, t)
            dq_p, dk_p, dv_p = [], [], []
            for g in range(N_KV_HEADS):
                qs, dos = _stack_heads(qv, r0, g), _stack_heads(dov, r0, g)
                delta = jnp.sum(dos.astype(F32) * _stack_heads(ov, r0, g).astype(F32), axis=1, keepdims=True)
                kg = kw[r0:r0 + 3 * WINDOW, g * HEAD_DIM:(g + 1) * HEAD_DIM]
                vg = vw[r0:r0 + 3 * WINDOW, g * HEAD_DIM:(g + 1) * HEAD_DIM]
                lse_col = l_ref[g, sb]
                pr = jnp.where(valid, jnp.exp(_dot(qs, kg, NT) - lse_col), 0.0)
                ds = (pr * (_dot(dos, vg, NT) - delta)).astype(BF16)
                dq_p.append(_dot(ds, kg, NN))
                dk_p.append(_dot(ds, qs, TN))
                dv_p.append(_dot(pr.astype(BF16), dos, TN))
                ps = jnp.exp(_sink_column(sink_ref, g) - lse_col) * delta
                for h in range(GROUP):
                    tot = jnp.sum(ps[h * WINDOW:(h + 1) * WINDOW], axis=0, keepdims=True)
                    dsink = dsink - jnp.where(lane16 == GROUP * g + h, tot, 0.0)
            dq_ref[r0:r0 + WINDOW, :] = _unstack_heads(dq_p)
            dkw_ref[sb] = jnp.concatenate(dk_p, axis=1)
            dvw_ref[sb] = jnp.concatenate(dv_p, axis=1)

        @pl.when(i == 0)
        def _():
            dsink_ref[...] = jnp.zeros_like(dsink_ref)
        dsink_ref[...] += dsink

    rowq = pl.BlockSpec((tq, Q_DIM), lambda i: (i, 0))
    win = pl.BlockSpec((nsb, 3 * WINDOW, KV_DIM), lambda i: (i, 0, 0))
    wshape = jax.ShapeDtypeStruct((t // WINDOW, 3 * WINDOW, KV_DIM), F32)
    return pl.pallas_call(
        body, grid=(t // tq,),
        in_specs=[pl.BlockSpec(memory_space=pltpu.SMEM), rowq, rowq, rowq,
                  pl.BlockSpec((N_KV_HEADS, nsb, GROUP * WINDOW, 1), lambda i: (0, i, 0, 0))]
        + _window_specs(t, tq) + _window_specs(t, tq),
        out_specs=[rowq, win, win, _full((1, N_Q_HEADS))],
        out_shape=[jax.ShapeDtypeStruct((t, Q_DIM), F32), wshape, wshape, jax.ShapeDtypeStruct((1, N_Q_HEADS), F32)],
        compiler_params=_cp("arbitrary"), name=name)(sink, q, o, do, lse, k, k, k, v, v, v)


def _attn_post_bwd(dq, dkw, dvw, tables, *, name):
    t = dq.shape[0]
    nb = t // WINDOW
    n = Q_DIM + 2 * KV_DIM

    def body(dq_ref, ka_ref, kb_ref, kc_ref, va_ref, vb_ref, vc_ref, c_ref, sh_ref, sl_ref, o_ref):
        j = pl.program_id(0)
        lo = (j > 0).astype(F32)
        hi = (j < nb - 1).astype(F32)
        c, sh, sl = c_ref[...], sh_ref[...], sl_ref[...]
        dk = ka_ref[...] * hi + kb_ref[...] + kc_ref[...] * lo
        dv = va_ref[...] * hi + vb_ref[...] + vc_ref[...] * lo
        o_ref[:, :Q_DIM] = (_rope(dq_ref[...], c, sh, sl, -1.0) * HEAD_DIM ** -0.5).astype(BF16)
        o_ref[:, Q_DIM:Q_DIM + KV_DIM] = _rope(dk, c, sh, sl, -1.0).astype(BF16)
        o_ref[:, Q_DIM + KV_DIM:] = dv.astype(BF16)

    wins = [pl.BlockSpec((None, WINDOW, KV_DIM), lambda j: (jnp.minimum(j + 1, nb - 1), 0, 0)),
            pl.BlockSpec((None, WINDOW, KV_DIM), lambda j: (j, 1, 0)),
            pl.BlockSpec((None, WINDOW, KV_DIM), lambda j: (jnp.maximum(j - 1, 0), 2, 0))]
    tab = pl.BlockSpec((WINDOW, LANES), lambda j: (j, 0))
    return pl.pallas_call(
        body, grid=(nb,),
        in_specs=[pl.BlockSpec((WINDOW, Q_DIM), lambda j: (j, 0))] + wins + wins + [tab, tab, tab],
        out_specs=pl.BlockSpec((WINDOW, n), lambda j: (j, 0)),
        out_shape=jax.ShapeDtypeStruct((t, n), BF16),
        compiler_params=_cp("parallel"), name=name)(dq, dkw, dkw, dkw, dvw, dvw, dvw, *tables)


HGRN_ROWS = 512


def _cum_mask(rev, transpose=False):
    row = lax.broadcasted_iota(jnp.int32, (HGRN_CHUNK, HGRN_CHUNK), 0)
    col = lax.broadcasted_iota(jnp.int32, (HGRN_CHUNK, HGRN_CHUNK), 1)
    return (row <= col) if rev != transpose else (row >= col)


def _gates(zq, zf, lb):
    q = zq * _sigmoid(zq)
    sig = _sigmoid(zf)
    f = lb + (1.0 - lb) * sig
    k = (1.0 - lb) * (1.0 - sig)
    return q, sig, f, k


def _intra_exact(q, k, b, mask):
    rows = lax.broadcasted_iota(jnp.int32, (HGRN_CHUNK, 1), 0)
    cols = lax.broadcasted_iota(jnp.int32, (HGRN_CHUNK, HGRN_CHUNK), 1)

    def it(s, a):
        pick = rows == s
        b_s = jnp.sum(jnp.where(pick, b, 0.0), axis=0, keepdims=True)
        k_s = jnp.sum(jnp.where(pick, k, 0.0), axis=0, keepdims=True)
        col = jnp.sum(q * jnp.exp(jnp.minimum(b - b_s, 0.0)) * k_s, axis=1, keepdims=True)
        return jnp.where(cols == s, col, a)

    a = lax.fori_loop(0, HGRN_CHUNK, it, jnp.zeros((HGRN_CHUNK, HGRN_CHUNK), F32))
    return jnp.where(mask, a, 0.0)


def _intra_exact_bwd(q, k, b, da):
    rows = lax.broadcasted_iota(jnp.int32, (HGRN_CHUNK, 1), 0)
    cols = lax.broadcasted_iota(jnp.int32, (HGRN_CHUNK, HGRN_CHUNK), 1)

    def it(s, carry):
        dq, dk = carry
        pick = rows == s
        b_s = jnp.sum(jnp.where(pick, b, 0.0), axis=0, keepdims=True)
        k_s = jnp.sum(jnp.where(pick, k, 0.0), axis=0, keepdims=True)
        w = jnp.sum(jnp.where(cols == s, da, 0.0), axis=1, keepdims=True) * jnp.exp(jnp.minimum(b - b_s, 0.0))
        dq = dq + w * k_s
        dk = jnp.where(pick, jnp.sum(w * q, axis=0, keepdims=True), dk)
        return dq, dk

    z = jnp.zeros((HGRN_CHUNK, HGRN_KEY), F32)
    return lax.fori_loop(0, HGRN_CHUNK, it, (z, z))


def _hgrn_scan_fwd(z, lb, rev, *, name):
    t = z.shape[0]
    rb = min(HGRN_ROWS, t)
    nb, nch = t // rb, rb // HGRN_CHUNK
    fcol = HGRN_HEADS * (2 if rev else 1)

    def blk(n):
        return nb - 1 - n if rev else n

    def body(zq_ref, zf_ref, zv_ref, lb_ref, o_ref, s_ref, st_ref):
        @pl.when(pl.program_id(1) == 0)
        def _():
            st_ref[...] = jnp.zeros_like(st_ref)
        lbv = lb_ref[...]
        cmask = _cum_mask(rev)
        cum = cmask.astype(F32)

        def chunk(c, carry):
            cc = nch - 1 - c if rev else c
            r0 = pl.multiple_of(cc * HGRN_CHUNK, HGRN_CHUNK)
            sl = pl.ds(r0, HGRN_CHUNK)
            q, _, f, k = _gates(zq_ref[sl, :], zf_ref[sl, :], lbv)
            v = zv_ref[sl, :].astype(BF16)
            g = jnp.log(f)
            b = jnp.dot(cum, g, precision=lax.Precision.HIGHEST, preferred_element_type=F32)
            tot = jnp.sum(g, axis=0, keepdims=True)
            qd = q * jnp.exp(b)
            st = st_ref[...]
            st_bf = st.astype(BF16)
            s_ref[cc] = st_bf
            a = lax.cond(
                jnp.min(tot) >= FACTORED_DECAY_LIMIT,
                lambda: jnp.where(cmask, _dot(qd.astype(BF16), (k * jnp.exp(-b)).astype(BF16), NT), 0.0),
                lambda: _intra_exact(q, k, b, cmask))
            o_ref[sl, :] = _dot(qd.astype(BF16), st_bf, NT) + _dot(a.astype(BF16), v, NN)
            kd = (k * jnp.exp(tot - b)).astype(BF16)
            st_ref[...] = st * jnp.exp(tot) + _dot(v, kd, TN)
            return carry

        lax.fori_loop(0, nch, chunk, 0)

    def col(off):
        return pl.BlockSpec((rb, LANES), lambda h, n: (blk(n), off + h))

    return pl.pallas_call(
        body, grid=(HGRN_HEADS, nb),
        in_specs=[col(0), col(fcol), col(3 * HGRN_HEADS), pl.BlockSpec((None, 1, LANES), lambda h, n: (h, 0, 0))],
        out_specs=[pl.BlockSpec((rb, LANES), lambda h, n: (blk(n), h)),
                   pl.BlockSpec((None, nch, LANES, LANES), lambda h, n: (h, blk(n), 0, 0))],
        out_shape=[jax.ShapeDtypeStruct((t, D_MODEL), F32),
                   jax.ShapeDtypeStruct((HGRN_HEADS, t // HGRN_CHUNK, LANES, LANES), BF16)],
        scratch_shapes=[pltpu.VMEM((LANES, LANES), F32)],
        compiler_params=_cp("parallel", "arbitrary"), name=name)(z, z, z, lb)


def _hgrn_scan_bwd(z, lb, d_o, states, rev, *, name):
    t = z.shape[0]
    rb = min(HGRN_ROWS, t)
    nb, nch = t // rb, rb // HGRN_CHUNK
    fcol = HGRN_HEADS * (2 if rev else 1)

    def blk(n):
        return n if rev else nb - 1 - n

    def body(zq_ref, zf_ref, zv_ref, lb_ref, do_ref, s_ref, dq_ref, dzf_ref, dv_ref, dlb_ref, dst_ref):
        @pl.when(pl.program_id(1) == 0)
        def _():
            dst_ref[...] = jnp.zeros_like(dst_ref)
            dlb_ref[...] = jnp.zeros_like(dlb_ref)
        lbv = lb_ref[...]
        cmask = _cum_mask(rev)
        cum = cmask.astype(F32)
        cum_t = _cum_mask(rev, transpose=True).astype(F32)

        def chunk(c, carry):
            cc = c if rev else nch - 1 - c
            r0 = pl.multiple_of(cc * HGRN_CHUNK, HGRN_CHUNK)
            sl = pl.ds(r0, HGRN_CHUNK)
            zq = zq_ref[sl, :]
            q, sig, f, k = _gates(zq, zf_ref[sl, :], lbv)
            v = zv_ref[sl, :].astype(BF16)
            g = jnp.log(f)
            b = jnp.dot(cum, g, precision=lax.Precision.HIGHEST, preferred_element_type=F32)
            tot = jnp.sum(g, axis=0, keepdims=True)
            eb = jnp.exp(b)
            qd = (q * eb).astype(BF16)
            kd = (k * jnp.exp(tot - b)).astype(BF16)
            do = do_ref[sl, :].astype(BF16)
            st0 = s_ref[cc]
            dst = dst_ref[...]
            dst_bf = dst.astype(BF16)
            da = jnp.where(cmask, _dot(do, v, NT), 0.0)
            factored = jnp.min(tot) >= FACTORED_DECAY_LIMIT

            def fast():
                ke = jnp.exp(-b)
                kf = (k * ke).astype(BF16)
                a = jnp.where(cmask, _dot(qd, kf, NT), 0.0)
                da_bf = da.astype(BF16)
                dqf, dkf = _dot(da_bf, kf, NN), _dot(da_bf, qd, TN)
                return a, dqf * eb, dkf * ke, qd.astype(F32) * dqf - kf.astype(F32) * dkf

            def slow():
                dq_i, dk_i = _intra_exact_bwd(q, k, b, da)
                return _intra_exact(q, k, b, cmask), dq_i, dk_i, q * dq_i - k * dk_i

            a, dq_i, dk_i, db_i = lax.cond(factored, fast, slow)
            dv_ref[sl, :] = _dot(a.astype(BF16), do, TN) + _dot(kd, dst_bf, NT)
            dq_x = _dot(do, st0, NN) * eb
            dq = dq_i + dq_x
            dk_x = _dot(v, dst_bf, NN) * jnp.exp(tot - b)
            dk = dk_i + dk_x
            etot = jnp.exp(tot)
            dtot = jnp.sum(k * dk_x, axis=0, keepdims=True) + etot * jnp.sum(dst * st0.astype(F32), axis=0, keepdims=True)
            dst_ref[...] = dst * etot + _dot(do, qd, TN)
            db = db_i + q * dq_x - k * dk_x
            dg =jnp.dot(cum_t, db, precision=lax.Precision.HIGHEST, preferred_element_type=F32) + dtot
            sq = _sigmoid(zq)
            dq_ref[sl, :] = dq * sq * (1.0 + zq * (1.0 - sq))
            dfk = dg / f - dk
            dzf_ref[sl, :] = (dfk * (1.0 - lbv) * sig * (1.0 - sig)).astype(BF16)
            dlb_ref[...] += jnp.sum(dfk * (1.0 - sig), axis=0, keepdims=True)
            return carry

        lax.fori_loop(0, nch, chunk, 0)

    def col(off):
        return pl.BlockSpec((rb, LANES), lambda h, n: (blk(n), off + h))

    out_col = pl.BlockSpec((rb, LANES), lambda h, n: (blk(n), h))
    return pl.pallas_call(
        body, grid=(HGRN_HEADS, nb),
        in_specs=[col(0), col(fcol), col(3 * HGRN_HEADS), pl.BlockSpec((None, 1, LANES), lambda h, n: (h, 0, 0)),
                  out_col, pl.BlockSpec((None, nch, LANES, LANES), lambda h, n: (h, blk(n), 0, 0))],
        out_specs=[out_col, out_col, out_col, pl.BlockSpec((None, 1, LANES), lambda h, n: (h, 0, 0))],
        out_shape=[jax.ShapeDtypeStruct((t, D_MODEL), F32), jax.ShapeDtypeStruct((t, D_MODEL), BF16),
                   jax.ShapeDtypeStruct((t, D_MODEL), F32), jax.ShapeDtypeStruct((HGRN_HEADS, 1, LANES), F32)],
        scratch_shapes=[pltpu.VMEM((LANES, LANES), F32)],
        compiler_params=_cp("parallel", "arbitrary"), name=name)(z, z, z, lb, d_o, states)


def _head(a, h):
    return a[:, h * LANES:(h + 1) * LANES]


def _hgrn_fwd(z, lb, rev, *, name):
    t = z.shape[0]
    rb = min(HGRN_ROWS, t)
    nb, nch = t // rb, rb // HGRN_CHUNK
    heads = range(HGRN_HEADS)

    def blk(n):
        return nb - 1 - n if rev else n

    def body(zq_ref, zf_ref, zv_ref, lb_ref, o_ref, s_ref, st_ref):
        @pl.when(pl.program_id(0) == 0)
        def _():
            st_ref[...] = jnp.zeros_like(st_ref)
        lbv = lb_ref[...]
        cmask = _cum_mask(rev)
        cum = cmask.astype(F32)

        def chunk(c, carry):
            cc = nch - 1 - c if rev else c
            sl = pl.ds(pl.multiple_of(cc * HGRN_CHUNK, HGRN_CHUNK), HGRN_CHUNK)
            q, _, f, k = _gates(zq_ref[sl, :], zf_ref[sl, :], lbv)
            v = zv_ref[sl, :].astype(BF16)
            g = jnp.log(f)
            b = jnp.dot(cum, g, precision=lax.Precision.HIGHEST, preferred_element_type=F32)
            tot = jnp.sum(g, axis=0, keepdims=True)
            qd = (q * jnp.exp(b)).astype(BF16)
            kd = (k * jnp.exp(tot - b)).astype(BF16)
            etot = jnp.exp(tot)

            def factored():
                kf = (k * jnp.exp(-b)).astype(BF16)
                return tuple(jnp.where(cmask, _dot(_head(qd, h), _head(kf, h), NT), 0.0) for h in heads)

            def exact():
                return tuple(_intra_exact(_head(q, h), _head(k, h), _head(b, h), cmask) for h in heads)

            a = lax.cond(jnp.min(tot) >= FACTORED_DECAY_LIMIT, factored, exact)
            for h in heads:
                st = st_ref[h]
                st_bf = st.astype(BF16)
                s_ref[h, cc] = st_bf
                o_ref[sl, h * LANES:(h + 1) * LANES] = (
                    _dot(_head(qd, h), st_bf, NT) + _dot(a[h].astype(BF16), _head(v, h), NN))
                st_ref[h] = st * _head(etot, h) + _dot(_head(v, h), _head(kd, h), TN)
            return carry

        lax.fori_loop(0, nch, chunk, 0)

    def col(j):
        return pl.BlockSpec((rb, D_MODEL), lambda n: (blk(n), j))

    return pl.pallas_call(
        body, grid=(nb,),
        in_specs=[col(0), col(2 if rev else 1), col(3), _full((1, D_MODEL))],
        out_specs=[col(0), pl.BlockSpec((HGRN_HEADS, nch, LANES, LANES), lambda n: (0, blk(n), 0, 0))],
        out_shape=[jax.ShapeDtypeStruct((t, D_MODEL), F32),
                   jax.ShapeDtypeStruct((HGRN_HEADS, t // HGRN_CHUNK, LANES, LANES), BF16)],
        scratch_shapes=[pltpu.VMEM((HGRN_HEADS, LANES, LANES), F32)],
        compiler_params=_cp("arbitrary"), name=name)(z, z, z, lb)


def _hgrn_bwd(z, lb, d_o, states, rev, *, name):
    t = z.shape[0]
    rb = min(HGRN_ROWS, t)
    nb, nch = t // rb, rb // HGRN_CHUNK
    heads = range(HGRN_HEADS)

    def blk(n):
        return n if rev else nb - 1 - n

    def body(zq_ref, zf_ref, zv_ref, lb_ref, do_ref, s_ref, dq_ref, dzf_ref, dv_ref, dlb_ref, dst_ref):
        @pl.when(pl.program_id(0) == 0)
        def _():
            dst_ref[...] = jnp.zeros_like(dst_ref)
            dlb_ref[...] = jnp.zeros_like(dlb_ref)
        lbv = lb_ref[...]
        cmask = _cum_mask(rev)
        cum = cmask.astype(F32)
        cum_t = _cum_mask(rev, transpose=True).astype(F32)

        def chunk(c, carry):
            cc = c if rev else nch - 1 - c
            sl = pl.ds(pl.multiple_of(cc * HGRN_CHUNK, HGRN_CHUNK), HGRN_CHUNK)
            zq = zq_ref[sl, :]
            q, sig, f, k = _gates(zq, zf_ref[sl, :], lbv)
            v = zv_ref[sl, :].astype(BF16)
            g = jnp.log(f)
            b = jnp.dot(cum, g, precision=lax.Precision.HIGHEST, preferred_element_type=F32)
            tot = jnp.sum(g, axis=0, keepdims=True)
            eb = jnp.exp(b)
            etb = jnp.exp(tot - b)
            etot = jnp.exp(tot)
            qd = (q * eb).astype(BF16)
            kd = (k * etb).astype(BF16)
            do = do_ref[sl, :].astype(BF16)
            da = [jnp.where(cmask, _dot(_head(do, h), _head(v, h), NT), 0.0) for h in heads]

            def factored():
                ke = jnp.exp(-b)
                kf = (k * ke).astype(BF16)
                res = []
                for h in heads:
                    qh, kh = _head(qd, h), _head(kf, h)
                    da_bf = da[h].astype(BF16)
                    dqf, dkf = _dot(da_bf, kh, NN), _dot(da_bf, qh, TN)
                    res.append((jnp.where(cmask, _dot(qh, kh, NT), 0.0), dqf * _head(eb, h), dkf * _head(ke, h),
                                qh.astype(F32) * dqf - kh.astype(F32) * dkf))
                return tuple(res)

            def exact():
                res = []
                for h in heads:
                    qh, kh, bh = _head(q, h), _head(k, h), _head(b, h)
                    dq_i, dk_i = _intra_exact_bwd(qh, kh, bh, da[h])
                    res.append((_intra_exact(qh, kh, bh, cmask), dq_i, dk_i, qh * dq_i - kh * dk_i))
                return tuple(res)

            intra = lax.cond(jnp.min(tot) >= FACTORED_DECAY_LIMIT, factored, exact)
            dq_p, dk_p, db_p, dtot_p = [], [], [], []
            for h in heads:
                a, dq_i, dk_i, db_i = intra[h]
                doh, vh, qh, kh = _head(do, h), _head(v, h), _head(q, h), _head(k, h)
                st0 = s_ref[h, cc]
                dst = dst_ref[h]
                dst_bf = dst.astype(BF16)
                dv_ref[sl, h * LANES:(h + 1) * LANES] = _dot(a.astype(BF16), doh, TN) + _dot(_head(kd, h), dst_bf, NT)
                dq_x = _dot(doh, st0, NN) * _head(eb, h)
                dk_x = _dot(vh, dst_bf, NN) * _head(etb, h)
                dtot_p.append(jnp.sum(kh * dk_x, axis=0, keepdims=True)
                              + _head(etot, h) * jnp.sum(dst * st0.astype(F32), axis=0, keepdims=True))
                dst_ref[h] = dst * _head(etot, h) + _dot(doh, _head(qd, h), TN)
                dq_p.append(dq_i + dq_x)
                dk_p.append(dk_i + dk_x)
                db_p.append(db_i + qh * dq_x - kh * dk_x)
            dq, dk, db = (jnp.concatenate(x, axis=1) for x in (dq_p, dk_p, db_p))
            dg = jnp.dot(cum_t, db, precision=lax.Precision.HIGHEST, preferred_element_type=F32) + jnp.concatenate(dtot_p, axis=1)
            sq = _sigmoid(zq)
            dq_ref[sl, :] = dq * sq * (1.0 + zq * (1.0 - sq))
            dfk = dg / f - dk
            dzf_ref[sl, :] = (dfk * (1.0 - lbv) * sig * (1.0 - sig)).astype(BF16)
            dlb_ref[...] += jnp.sum(dfk * (1.0 - sig), axis=0, keepdims=True)
            return carry

        lax.fori_loop(0, nch, chunk, 0)

    def col(j):
        return pl.BlockSpec((rb, D_MODEL), lambda n: (blk(n), j))

    return pl.pallas_call(
        body, grid=(nb,),
        in_specs=[col(0), col(2 if rev else 1), col(3), _full((1, D_MODEL)), col(0),
                  pl.BlockSpec((HGRN_HEADS, nch, LANES, LANES), lambda n: (0, blk(n), 0, 0))],
        out_specs=[col(0), col(0), col(0), _full((1, D_MODEL))],
        out_shape=[jax.ShapeDtypeStruct((t, D_MODEL), F32), jax.ShapeDtypeStruct((t, D_MODEL), BF16),
                   jax.ShapeDtypeStruct((t, D_MODEL), F32), jax.ShapeDtypeStruct((1, D_MODEL), F32)],
        scratch_shapes=[pltpu.VMEM((HGRN_HEADS, LANES, LANES), F32)],
        compiler_params=_cp("arbitrary"), name=name)(z, z, z, lb, d_o, states)


def _hgrn_post_fwd(o_f, o_b, z, norm_g, *, name):
    t = o_f.shape[0]
    tm = _rows(t)

    def body(of_ref, ob_ref, gate_ref, ng_ref, y_ref):
        ng = ng_ref[...]
        for h in range(HGRN_HEADS):
            sl = slice(h * LANES, (h + 1) * LANES)
            o = of_ref[:, sl] + ob_ref[:, sl]
            r = lax.rsqrt(jnp.mean(o * o, axis=1, keepdims=True) + LN_EPS)
            gt = gate_ref[:, sl]
            y_ref[:, sl] = (o * r * ng * (gt * _sigmoid(gt))).astype(BF16)

    row = pl.BlockSpec((tm, D_MODEL), lambda i: (i, 0))
    return pl.pallas_call(
        body, grid=(t // tm,),
        in_specs=[row, row, pl.BlockSpec((tm, D_MODEL), lambda i: (i, 4)), _full((1, LANES))],
        out_specs=row, out_shape=jax.ShapeDtypeStruct((t, D_MODEL), BF16),
        compiler_params=_cp("parallel"), name=name)(o_f, o_b, z, norm_g)


def _hgrn_post_bwd(dy, o_f, o_b, z, norm_g, *, name):
    t = o_f.shape[0]
    tm = _rows(t)

    def body(dy_ref, of_ref, ob_ref, gate_ref, ng_ref, do_ref, dgate_ref, dng_ref):
        ng = ng_ref[...]
        dng = jnp.zeros((1, LANES), F32)
        for h in range(HGRN_HEADS):
            sl = slice(h * LANES, (h + 1) * LANES)
            o = of_ref[:, sl] + ob_ref[:, sl]
            r = lax.rsqrt(jnp.mean(o * o, axis=1, keepdims=True) + LN_EPS)
            gt = gate_ref[:, sl]
            sg = _sigmoid(gt)
            dyv = dy_ref[:, sl].astype(F32)
            xn = o * r
            dgate_ref[:, sl] = (dyv * xn * ng * sg * (1.0 + gt * (1.0 - sg))).astype(BF16)
            dn = dyv * gt * sg
            dng = dng + jnp.sum(dn * xn, axis=0, keepdims=True)
            dxn = dn * ng
            do_ref[:, sl] = r * (dxn - xn * jnp.mean(dxn * xn, axis=1, keepdims=True))

        @pl.when(pl.program_id(0) == 0)
        def _():
            dng_ref[...] = jnp.zeros_like(dng_ref)
        dng_ref[...] += dng

    row = pl.BlockSpec((tm, D_MODEL), lambda i: (i, 0))
    return pl.pallas_call(
        body, grid=(t // tm,),
        in_specs=[row, row, row, pl.BlockSpec((tm, D_MODEL), lambda i: (i, 4)), _full((1, LANES))],
        out_specs=[row, row, _full((1, LANES))],
        out_shape=[jax.ShapeDtypeStruct((t, D_MODEL), F32), jax.ShapeDtypeStruct((t, D_MODEL), BF16),
                   jax.ShapeDtypeStruct((1, LANES), F32)],
        compiler_params=_cp("arbitrary"), name=name)(dy, o_f, o_b, z, norm_g)


def _hgrn_combine(dq_f, dq_b, dzf_f, dzf_b, dv_f, dv_b, dgate, *, name):
    t = dq_f.shape[0]
    tm = _rows(t)

    def body(qf, qb, ff, fb, vf, vb, gt, o_ref):
        o_ref[:, 0 * D_MODEL:1 * D_MODEL] = (qf[...] + qb[...]).astype(BF16)
        o_ref[:, 1 * D_MODEL:2 * D_MODEL] = ff[...]
        o_ref[:, 2 * D_MODEL:3 * D_MODEL] = fb[...]
        o_ref[:, 3 * D_MODEL:4 * D_MODEL] = (vf[...] + vb[...]).astype(BF16)
        o_ref[:, 4 * D_MODEL:5 * D_MODEL] = gt[...]

    row = pl.BlockSpec((tm, D_MODEL), lambda i: (i, 0))
    return pl.pallas_call(
        body, grid=(t // tm,), in_specs=[row] * 7,
        out_specs=pl.BlockSpec((tm, 5 * D_MODEL), lambda i: (i, 0)),
        out_shape=jax.ShapeDtypeStruct((t, 5 * D_MODEL), BF16),
        compiler_params=_cp("parallel"), name=name)(dq_f, dq_b, dzf_f, dzf_b, dv_f, dv_b, dgate)


def _lower_bounds(logits2d, *, name):
    def body(l_ref, lb_ref):
        l = l_ref[...]
        e = jnp.exp(l - jnp.max(l, axis=0, keepdims=True))
        sm = e / jnp.sum(e, axis=0, keepdims=True)
        s1, s2, s3 = sm[1:2], sm[2:3], sm[3:4]
        lb_ref[...] = jnp.concatenate([jnp.zeros_like(s1), s1, s1 + s2, s1 + s2 + s3], axis=0)

    return pl.pallas_call(body, out_shape=jax.ShapeDtypeStruct(logits2d.shape, F32), name=name)(logits2d)


def _lower_bounds_bwd(logits2d, dlb, *, name):
    def body(l_ref, d_ref, o_ref):
        l = l_ref[...]
        e = jnp.exp(l - jnp.max(l, axis=0, keepdims=True))
        sm = e / jnp.sum(e, axis=0, keepdims=True)
        d = d_ref[...]
        d1, d2, d3 = d[1:2], d[2:3], d[3:4]
        dsm = jnp.concatenate([jnp.zeros_like(d1), d1 + d2 + d3, d2 + d3, d3], axis=0)
        o_ref[...] = sm * (dsm - jnp.sum(sm * dsm, axis=0, keepdims=True))

    return pl.pallas_call(body, out_shape=jax.ShapeDtypeStruct(logits2d.shape, F32), name=name)(logits2d, dlb)


def _adamw(w, g, m, v, *, name):
    r, c = w.shape
    tr = r if r <= 512 else _pick_rows(r)

    def body(w_ref, g_ref, m_ref, v_ref, d_ref, nm_ref, nv_ref):
        gv = g_ref[...]
        nm = ADAM_B1 * m_ref[...] + (1.0 - ADAM_B1) * gv
        nv = ADAM_B2 * v_ref[...] + (1.0 - ADAM_B2) * (gv * gv)
        m_hat = nm / (1.0 - ADAM_B1 ** ADAM_STEP)
        v_hat = nv / (1.0 - ADAM_B2 ** ADAM_STEP)
        d_ref[...] = -ADAM_LR * (m_hat / (jnp.sqrt(v_hat) + ADAM_EPS) + ADAM_WD * w_ref[...])
        nm_ref[...] = nm
        nv_ref[...] = nv

    blk = pl.BlockSpec((tr, c), lambda i: (i, 0))
    shp = jax.ShapeDtypeStruct((r, c), F32)
    return pl.pallas_call(body, grid=(r // tr,), in_specs=[blk] * 4, out_specs=[blk] * 3, out_shape=[shp] * 3,
                          compiler_params=_cp("parallel"), name=name)(w, g, m, v)


def _pick_rows(r, cap=512):
    best = 8
    for d in range(8, cap + 1, 8):
        if r % d == 0:
            best = d
    assert r % best == 0, r
    return best


def _local_step(x, p, target, w, sp):
    t = x.shape[0]
    tables = _rope_tables(t)
    logits2d = sp["hgrn_lb_logits"].reshape(DEPTH, 2 * D_MODEL)
    lb_all = _lower_bounds(logits2d, name="lb_fwd").reshape(DEPTH, 2, 1, D_MODEL)
    row = lambda a, i: a[i][None]

    saved = []
    xf, xb = x, x.astype(BF16)
    for i in range(DEPTH):
        j = i // 2
        s = dict(xin_bf=xb)
        if i % 2 == 0:
            q, k, v = _qkv_rope(xb, w["att_w_qkv"][j], tables, name=f"qkv_rope_{i}")
            o, lse = _attn_fwd(q, k, v, sp["att_sink"][j], name=f"attn_fwd_{i}")
            s.update(q=q, k=k, v=v, o=o, lse=lse)
            mix_in, w_o = o, w["att_w_o"][j]
        else:
            z = _mm_nn(xb, w["hgrn_w_in"][j], out_dtype=F32, name=f"hgrn_in_{i}")
            o_f, s_f = _hgrn_fwd(z, lb_all[i, 0], False, name=f"hgrn_scan_f_{i}")
            o_b, s_b = _hgrn_fwd(z, lb_all[i, 1], True, name=f"hgrn_scan_b_{i}")
            og = _hgrn_post_fwd(o_f, o_b, z, row(sp["hgrn_norm_g"], j), name=f"hgrn_post_{i}")
            s.update(z=z, o_f=o_f, o_b=o_b, s_f=s_f, s_b=s_b, og=og)
            mix_in, w_o = og, w["hgrn_w_o"][j]
        x1, x1b, xh1, rs1 = _mm_res_ln(mix_in, w_o, xf, row(sp["ln_mix_g"], i), row(sp["ln_mix_b"], i), name=f"mix_out_ln_{i}")
        g, u, h = _ffn_in(x1b, w["ffn_w_in"][i], name=f"ffn_in_{i}")
        x2, x2b, xh2, rs2 = _mm_res_ln(h, w["ffn_w_out"][i], x1, row(sp["ln_ffn_g"], i), row(sp["ln_ffn_b"], i), name=f"ffn_out_ln_{i}")
        xf, xb, gate, pp = _ple_fwd(x2, x2b, p, i, w["ple_w_gate"][i], w["ple_w_proj"][i], name=f"ple_fwd_{i}")
        s.update(x1b=x1b, xh1=xh1, rs1=rs1, g=g, u=u, h=h, x2b=x2b, xh2=xh2, rs2=rs2, gate=gate, pp=pp)
        saved.append(s)

    loss, dx = _loss_head(xf, target, name="loss_head")

    gw = {n: [None] * w[n].shape[0] for n in w}
    gs = {n: [None] * DEPTH for n in ("ln_mix_g", "ln_mix_b", "ln_ffn_g", "ln_ffn_b")}
    gs.update(att_sink=[None] * 2, hgrn_norm_g=[None] * 2)
    dlb = [jnp.zeros((2, D_MODEL), F32)] * DEPTH
    for i in reversed(range(DEPTH)):
        j = i // 2
        s = saved[i]
        du2, du2b, dpre, dpp, gs["ln_ffn_g"][i], gs["ln_ffn_b"][i] = _ple_bwd(
            dx, s["gate"], s["pp"], w["ple_w_gate"][i], s["xh2"], s["rs2"], row(sp["ln_ffn_g"], i), name=f"ple_bwd_{i}")
        gw["ple_w_gate"][i] = _mm_tn(s["x2b"], dpre, name=f"dw_ple_gate_{i}")
        gw["ple_w_proj"][i] = _mm_tn_p(p, i, dpp, name=f"dw_ple_proj_{i}")
        dgg, dgu = _ffn_out_bwd(du2b, w["ffn_w_out"][i], s["g"], s["u"], name=f"ffn_out_bwd_{i}")
        gw["ffn_w_out"][i] = _mm_tn(s["h"], du2b, name=f"dw_ffn_out_{i}")
        du1, du1b, gs["ln_mix_g"][i], gs["ln_mix_b"][i] = _mm_nt(
            [dgg, dgu], w["ffn_w_in"][i], tk=_pick(D_FF, 1408), resid=du2,
            ln=(s["xh1"], s["rs1"], row(sp["ln_mix_g"], i)), name=f"ffn_in_bwd_{i}")
        gw["ffn_w_in"][i] = jnp.concatenate(
            [_mm_tn(s["x1b"], dgg, name=f"dw_ffn_gate_{i}"), _mm_tn(s["x1b"], dgu, name=f"dw_ffn_up_{i}")], axis=1)
        if i % 2 == 0:
            gw["att_w_o"][j] = _mm_tn(s["o"], du1b, name=f"dw_att_o_{i}")
            do = _mm_nt([du1b], w["att_w_o"][j], tk=Q_DIM, out_dtype=BF16, name=f"att_o_bwd_{i}")
            dq, dkw, dvw, gs["att_sink"][j] = _attn_bwd(
                s["q"], s["k"], s["v"], s["o"], do, s["lse"], sp["att_sink"][j], name=f"attn_bwd_{i}")
            dqkv = _attn_post_bwd(dq, dkw, dvw, tables, name=f"attn_post_bwd_{i}")
            gw["att_w_qkv"][j] = _mm_tn(s["xin_bf"], dqkv, name=f"dw_att_qkv_{i}")
            dx = _mm_nt([dqkv], w["att_w_qkv"][j], tk=Q_DIM + 2 * KV_DIM, resid=du1, name=f"qkv_bwd_{i}")
        else:
            gw["hgrn_w_o"][j] = _mm_tn(s["og"], du1b, name=f"dw_hgrn_o_{i}")
            dy = _mm_nt([du1b], w["hgrn_w_o"][j], tk=D_MODEL, out_dtype=BF16, name=f"hgrn_o_bwd_{i}")
            d_o, dgate, gs["hgrn_norm_g"][j] = _hgrn_post_bwd(
                dy, s["o_f"], s["o_b"], s["z"], row(sp["hgrn_norm_g"], j), name=f"hgrn_post_bwd_{i}")
            dq_f, dzf_f, dv_f, dlb_f = _hgrn_bwd(s["z"], lb_all[i, 0], d_o, s["s_f"], False, name=f"hgrn_scan_f_bwd_{i}")
            dq_b, dzf_b, dv_b, dlb_b = _hgrn_bwd(s["z"], lb_all[i, 1], d_o, s["s_b"], True, name=f"hgrn_scan_b_bwd_{i}")
            dlb[i] = jnp.stack([dlb_f.reshape(D_MODEL), dlb_b.reshape(D_MODEL)])
            dz = _hgrn_combine(dq_f, dq_b, dzf_f, dzf_b, dv_f, dv_b, dgate, name=f"hgrn_combine_{i}")
            gw["hgrn_w_in"][j] = _mm_tn(s["xin_bf"], dz, name=f"dw_hgrn_in_{i}")
            dx = _mm_nt([dz], w["hgrn_w_in"][j], tk=_pick(5 * D_MODEL, 1408), resid=du1, name=f"hgrn_in_bwd_{i}")

    gw = {n: jnp.stack(v) for n, v in gw.items()}
    gsmall = {n: jnp.concatenate(v, axis=0) for n, v in gs.items()}
    gsmall["hgrn_lb_logits"] = _lower_bounds_bwd(
        logits2d, jnp.stack(dlb).reshape(DEPTH, 2 * D_MODEL), name="lb_bwd").reshape(DEPTH, 2, D_MODEL)
    return loss, dx, gw, gsmall


ANY = pl.BlockSpec(memory_space=pl.ANY)


def _place():
    x, y, c = lax.axis_index("x"), lax.axis_index("y"), lax.axis_index("c")
    chips = [(1 - x, y), (x, 1 - y), (1 - x, 1 - y)]
    return x, y, c, chips


def _remote(src, dst, send_sem, recv_sem, to):
    return pltpu.make_async_remote_copy(src_ref=src, dst_ref=dst, send_sem=send_sem, recv_sem=recv_sem,
                                        device_id=to, device_id_type=MESH)


def _allgather_weights(packed, *, name):
    r = packed.shape[0]
    hr = r // 2

    def body(src_ref, out_ref, send_sems, recv_sems, local_sem):
        x, y, c, chips = _place()
        sibling = (x, y, 1 - c)

        def half(cx, cy, hc):
            return out_ref.at[2 * cx + cy, pl.ds(hc * hr, hr), :]

        mine = pltpu.make_async_copy(src_ref, out_ref.at[2 * x + y], local_sem)
        mine.start()
        my_half = src_ref.at[pl.ds(c * hr, hr), :]
        first = [_remote(my_half, half(x, y, c), send_sems.at[j], recv_sems.at[j], (*chip, c)) for j, chip in enumerate(chips)]
        for cp in first:
            cp.start()
        passed = [_remote(half(*chip, c), half(*chip, c), send_sems.at[3 + j], recv_sems.at[3 + j], sibling)
                  for j, chip in enumerate(chips)]
        for j, chip in enumerate(chips):
            _remote(my_half, half(*chip, c), send_sems.at[j], recv_sems.at[j], (*chip, c)).wait_recv()
            passed[j].start()
        for j, chip in enumerate(chips):
            _remote(my_half, half(*chip, 1 - c), send_sems.at[3 + j], recv_sems.at[3 + j], sibling).wait_recv()
        for cp in first + passed:
            cp.wait_send()
        mine.wait()

    return pl.pallas_call(
        body, in_specs=[ANY], out_specs=ANY, out_shape=jax.ShapeDtypeStruct((N_CHIPS, r, packed.shape[1]), packed.dtype),
        scratch_shapes=[pltpu.SemaphoreType.DMA((6,)), pltpu.SemaphoreType.DMA((6,)), pltpu.SemaphoreType.DMA],
        name=name)(packed)


def _allreduce_small(block, *, name):
    m, n = block.shape

    def body(x_ref, sum_ref, all_ref, send_sems, recv_sems):
        x, y, c, chips = _place()
        me, sibling = (x, y, c), (x, y, 1 - c)

        def rows(px, py, pc):
            return all_ref.at[pl.ds((4 * px + 2 * py + pc) * m, m), :]

        all_ref[pl.ds((4 * x + 2 * y + c) * m, m), :] = x_ref[...]
        first = [_remote(x_ref, rows(*me), send_sems.at[0], recv_sems.at[0], sibling)]
        first += [_remote(x_ref, rows(*me), send_sems.at[1 + j], recv_sems.at[1 + j], (*chip, c)) for j, chip in enumerate(chips)]
        for cp in first:
            cp.start()
        passed = [_remote(rows(*chip, c), rows(*chip, c), send_sems.at[4 + j], recv_sems.at[4 + j], sibling)
                  for j, chip in enumerate(chips)]
        for j, chip in enumerate(chips):
            _remote(x_ref, rows(*chip, c), send_sems.at[1 + j], recv_sems.at[1 + j], me).wait_recv()
            passed[j].start()
        _remote(x_ref, rows(*sibling), send_sems.at[0], recv_sems.at[0], me).wait_recv()
        for j, chip in enumerate(chips):
            _remote(x_ref, rows(*chip, 1 - c), send_sems.at[4 + j], recv_sems.at[4 + j], me).wait_recv()
        for cp in first + passed:
            cp.wait_send()
        acc = all_ref[pl.ds(0, m), :]
        for d in range(1, 8):
            acc = acc + all_ref[pl.ds(d * m, m), :]
        sum_ref[...] = acc

    vmem = pl.BlockSpec(memory_space=pltpu.VMEM)
    return pl.pallas_call(
        body, in_specs=[vmem], out_specs=vmem, out_shape=jax.ShapeDtypeStruct((m, n), F32),
        scratch_shapes=[pltpu.VMEM((8 * m, n), F32), pltpu.SemaphoreType.DMA((7,)), pltpu.SemaphoreType.DMA((7,))],
        name=name)(block)


def _pair_exchange(g, *, name):
    n, r, w = g.shape
    hr = r // 2

    def body(g_ref, out_ref, send_sem, recv_sem):
        x, y, c, _ = _place()
        cp = _remote(g_ref.at[:, pl.ds((1 - c) * hr, hr), :], out_ref, send_sem, recv_sem, (x, y, 1 - c))
        cp.start()
        cp.wait()

    return pl.pallas_call(
        body, in_specs=[ANY], out_specs=ANY, out_shape=jax.ShapeDtypeStruct((n, hr, w), g.dtype),
        scratch_shapes=[pltpu.SemaphoreType.DMA, pltpu.SemaphoreType.DMA], name=name)(g)


def _chip_scatter(part, *, name):
    n, h, w = part.shape

    def body(p_ref, out_ref, send_sems, recv_sems, local_sem):
        x, y, c, chips = _place()
        me = 2 * x + y
        mine = pltpu.make_async_copy(p_ref.at[me], out_ref.at[me], local_sem)
        mine.start()
        sends = [_remote(p_ref.at[2 * cx + cy], out_ref.at[me], send_sems.at[j], recv_sems.at[j], (cx, cy, c))
                 for j, (cx, cy) in enumerate(chips)]
        for cp in sends:
            cp.start()
        for j, (cx, cy) in enumerate(chips):
            _remote(p_ref.at[me], out_ref.at[2 * cx + cy], send_sems.at[j], recv_sems.at[j], (cx, cy, c)).wait_recv()
        for cp in sends:
            cp.wait_send()
        mine.wait()

    return pl.pallas_call(
        body, in_specs=[ANY], out_specs=ANY, out_shape=jax.ShapeDtypeStruct((n, h, w), part.dtype),
        scratch_shapes=[pltpu.SemaphoreType.DMA((3,)), pltpu.SemaphoreType.DMA((3,)), pltpu.SemaphoreType.DMA],
        name=name)(part)


def _pair_share(half, *, name):
    h, w = half.shape

    def body(h_ref, out_ref, send_sem, recv_sem, local_sem):
        x, y, c, _ = _place()
        dst = out_ref.at[pl.ds(c * h, h), :]
        mine = pltpu.make_async_copy(h_ref, dst, local_sem)
        mine.start()
        cp = _remote(h_ref, dst, send_sem, recv_sem, (x, y, 1 - c))
        cp.start()
        cp.wait_send()
        _remote(h_ref, out_ref.at[pl.ds((1 - c) * h, h), :], send_sem, recv_sem, (x, y, 1 - c)).wait_recv()
        mine.wait()

    return pl.pallas_call(
        body, in_specs=[ANY], out_specs=ANY, out_shape=jax.ShapeDtypeStruct((2 * h, w), half.dtype),
        scratch_shapes=[pltpu.SemaphoreType.DMA, pltpu.SemaphoreType.DMA, pltpu.SemaphoreType.DMA], name=name)(half)


def _add_own_half(g, recv, c, *, name):
    n, r, w = g.shape
    hr = r // 2
    tr = _pick_rows(hr, 1024)
    nr = hr // tr

    def body(c_ref, g_ref, r_ref, o_ref):
        o_ref[...] = (g_ref[...] + r_ref[...]).astype(BF16)

    return pl.pallas_call(
        body,
        grid_spec=pltpu.PrefetchScalarGridSpec(
            num_scalar_prefetch=1, grid=(n, nr),
            in_specs=[pl.BlockSpec((None, tr, w), lambda s, i, c_ref: (s, c_ref[0] * nr + i, 0)),
                      pl.BlockSpec((None, tr, w), lambda s, i, c_ref: (s, i, 0))],
            out_specs=pl.BlockSpec((None, tr, w), lambda s, i, c_ref: (s, i, 0))),
        out_shape=jax.ShapeDtypeStruct((n, hr, w), BF16),
        compiler_params=_cp("parallel", "parallel"), name=name)(c, g, recv)


def _sum_chips(q, *, name):
    n, h, w = q.shape
    tr = _pick_rows(h, 1024)

    def body(a, b, c, d, o_ref):
        o_ref[...] = ((a[...].astype(F32) + b[...].astype(F32)) + c[...].astype(F32)) + d[...].astype(F32)

    specs = [pl.BlockSpec((None, tr, w), functools.partial(lambda i, s: (s, i, 0), s=s)) for s in range(n)]
    return pl.pallas_call(
        body, grid=(h // tr,), in_specs=specs, out_specs=pl.BlockSpec((tr, w), lambda i: (i, 0)),
        out_shape=jax.ShapeDtypeStruct((h, w), F32), compiler_params=_cp("parallel"), name=name)(q, q, q, q)


PACK_W = 1024
BIG = (("att_w_qkv", 2), ("att_w_o", 1), ("hgrn_w_in", 2), ("hgrn_w_o", 1),
       ("ffn_w_in", 2), ("ffn_w_out", 1), ("ple_w_gate", 1), ("ple_w_proj", 2))
LB_ROWS = 32


def _pack_shards(shards):
    return jnp.concatenate([shards[n].reshape(-1, PACK_W) for n, _ in BIG], axis=0)


def _unpack_shards(buf, shapes):
    out, r0 = {}, 0
    for n, _ in BIG:
        nr = shapes[n][0] * shapes[n][1] * shapes[n][2] // PACK_W
        out[n] = buf[r0:r0 + nr].reshape(shapes[n])
        r0 += nr
    return out


def _gathered_to_full(buf, shapes):
    out, r0 = {}, 0
    for n, axis in BIG:
        l, r, c = shapes[n]
        nr = l * r * c // PACK_W
        sh = buf[:, r0:r0 + nr].reshape(N_CHIPS, l, r, c)
        out[n] = (sh.transpose(1, 0, 2, 3).reshape(l, N_CHIPS * r, c) if axis == 1
                  else sh.transpose(1, 2, 0, 3).reshape(l, r, N_CHIPS * c))
        r0 += nr
    return out, r0


def _full_to_slabs(full, shapes):
    parts = []
    for n, axis in BIG:
        l, r, c = shapes[n]
        g = full[n]
        g = (g.reshape(l, N_CHIPS, r, c).transpose(1, 0, 2, 3) if axis == 1
             else g.reshape(l, r, N_CHIPS, c).transpose(2, 0, 1, 3))
        parts.append(g.reshape(N_CHIPS, -1, PACK_W))
    return jnp.concatenate(parts, axis=1)


SMALL = ("ln_mix_g", "ln_mix_b", "ln_ffn_g", "ln_ffn_b")
SMALL_ROWS = 32


def _pack_small(vals, lb):
    rows = [vals[n] for n in SMALL] + [lb.reshape(-1, PACK_W)]
    for n in ("att_sink", "hgrn_norm_g"):
        flat = vals[n].reshape(1, -1)
        rows.append(jnp.pad(flat, ((0, 0), (0, PACK_W - flat.shape[1]))))
    buf = jnp.concatenate(rows, axis=0)
    return jnp.pad(buf, ((0, SMALL_ROWS - buf.shape[0]), (0, 0)))


def _unpack_small(buf, lb_shape):
    out, r0 = {}, 0
    for n in SMALL:
        out[n] = buf[r0:r0 + DEPTH]
        r0 += DEPTH
    nlb = lb_shape[0] * lb_shape[1] * lb_shape[2] // PACK_W
    out["hgrn_lb_logits"] = buf[r0:r0 + nlb].reshape(lb_shape)
    r0 += nlb
    out["att_sink"] = buf[r0, :2 * N_Q_HEADS].reshape(2, N_Q_HEADS)
    out["hgrn_norm_g"] = buf[r0 + 1, :2 * LANES].reshape(2, LANES)
    return out


WEIGHTS = ("att_w_qkv", "att_sink", "att_w_o", "hgrn_w_in", "hgrn_lb_logits", "hgrn_norm_g", "hgrn_w_o", "ln_mix_g",
           "ln_mix_b", "ffn_w_in", "ffn_w_out", "ln_ffn_g", "ln_ffn_b", "ple_w_gate", "ple_w_proj")


def kernel(x, p, att_w_qkv, att_sink, att_w_o, hgrn_w_in, hgrn_lb_logits, hgrn_norm_g, hgrn_w_o, ln_mix_g, ln_mix_b, ffn_w_in, ffn_w_out, ln_ffn_g, ln_ffn_b, ple_w_gate, ple_w_proj, loss_target, m_att_w_qkv, m_att_sink, m_att_w_o, m_hgrn_w_in, m_hgrn_lb_logits, m_hgrn_norm_g, m_hgrn_w_o, m_ln_mix_g, m_ln_mix_b, m_ffn_w_in, m_ffn_w_out, m_ln_ffn_g, m_ln_ffn_b, m_ple_w_gate, m_ple_w_proj, v_att_w_qkv, v_att_sink, v_att_w_o, v_hgrn_w_in, v_hgrn_lb_logits, v_hgrn_norm_g, v_hgrn_w_o, v_ln_mix_g, v_ln_mix_b, v_ffn_w_in, v_ffn_w_out, v_ln_ffn_g, v_ln_ffn_b, v_ple_w_gate, v_ple_w_proj):
    wts = dict(att_w_qkv=att_w_qkv, att_sink=att_sink, att_w_o=att_w_o, hgrn_w_in=hgrn_w_in, hgrn_lb_logits=hgrn_lb_logits,
               hgrn_norm_g=hgrn_norm_g, hgrn_w_o=hgrn_w_o, ln_mix_g=ln_mix_g, ln_mix_b=ln_mix_b, ffn_w_in=ffn_w_in,
               ffn_w_out=ffn_w_out, ln_ffn_g=ln_ffn_g, ln_ffn_b=ln_ffn_b, ple_w_gate=ple_w_gate, ple_w_proj=ple_w_proj)
    mom = dict(att_w_qkv=m_att_w_qkv, att_sink=m_att_sink, att_w_o=m_att_w_o, hgrn_w_in=m_hgrn_w_in, hgrn_lb_logits=m_hgrn_lb_logits,
               hgrn_norm_g=m_hgrn_norm_g, hgrn_w_o=m_hgrn_w_o, ln_mix_g=m_ln_mix_g, ln_mix_b=m_ln_mix_b, ffn_w_in=m_ffn_w_in,
               ffn_w_out=m_ffn_w_out, ln_ffn_g=m_ln_ffn_g, ln_ffn_b=m_ln_ffn_b, ple_w_gate=m_ple_w_gate, ple_w_proj=m_ple_w_proj)
    var = dict(att_w_qkv=v_att_w_qkv, att_sink=v_att_sink, att_w_o=v_att_w_o, hgrn_w_in=v_hgrn_w_in, hgrn_lb_logits=v_hgrn_lb_logits,
               hgrn_norm_g=v_hgrn_norm_g, hgrn_w_o=v_hgrn_w_o, ln_mix_g=v_ln_mix_g, ln_mix_b=v_ln_mix_b, ffn_w_in=v_ffn_w_in,
               ffn_w_out=v_ffn_w_out, ln_ffn_g=v_ln_ffn_g, ln_ffn_b=v_ln_ffn_b, ple_w_gate=v_ple_w_gate, ple_w_proj=v_ple_w_proj)
    shapes = {n: wts[n].shape for n, _ in BIG}
    chip = 2 * lax.axis_index("x") + lax.axis_index("y")
    core = lax.axis_index("c")

    lb_bits = lax.bitcast_convert_type(hgrn_lb_logits.reshape(-1), BF16).reshape(-1, PACK_W)
    packed = jnp.concatenate([_pack_shards({n: wts[n].astype(BF16) for n, _ in BIG}), lb_bits,
                              jnp.zeros((LB_ROWS - lb_bits.shape[0], PACK_W), BF16)], axis=0)
    gathered = _allgather_weights(packed, name="allgather_weights")
    w_full, r_big = _gathered_to_full(gathered, shapes)
    lb_sh = hgrn_lb_logits.shape
    lb_rows = gathered[:, r_big:r_big + lb_bits.shape[0]].reshape(N_CHIPS, -1, 2)
    lb_full = lax.bitcast_convert_type(lb_rows, F32).reshape(N_CHIPS, lb_sh[0], lb_sh[1], lb_sh[2])
    lb_full = lb_full.transpose(1, 2, 0, 3).reshape(lb_sh[0], lb_sh[1], N_CHIPS * lb_sh[2])
    sp = dict(att_sink=att_sink, hgrn_lb_logits=lb_full, hgrn_norm_g=hgrn_norm_g, ln_mix_g=ln_mix_g, ln_mix_b=ln_mix_b,
              ln_ffn_g=ln_ffn_g, ln_ffn_b=ln_ffn_b)

    loss, grad_x, gw, gs = _local_step(x[0], p, loss_target[0], w_full, sp)
    loss = lax.psum(loss[0, 0], ("x", "y", "c"))

    slabs = _full_to_slabs(gw, shapes)
    from_sibling = _pair_exchange(slabs, name="grad_pair_exchange")
    chip_part = _add_own_half(slabs, from_sibling, core.reshape(1).astype(jnp.int32), name="grad_pair_add")
    from_chips = _chip_scatter(chip_part, name="grad_chip_scatter")
    half = _sum_chips(from_chips, name="grad_chip_sum")
    g_big = _unpack_shards(_pair_share(half, name="grad_pair_share"), shapes)

    g_small_full = _unpack_small(_allreduce_small(_pack_small(gs, gs["hgrn_lb_logits"]), name="allreduce_small"),
                                 (lb_sh[0], lb_sh[1], N_CHIPS * lb_sh[2]))
    g_lb = lax.dynamic_slice_in_dim(g_small_full["hgrn_lb_logits"], chip * lb_sh[2], lb_sh[2], axis=2)
    grads = dict(g_big)
    grads.update({n: g_small_full[n] for n in SMALL + ("att_sink", "hgrn_norm_g")})
    grads["hgrn_lb_logits"] = g_lb

    delta, new_m, new_v = {}, {}, {}
    for n, _ in BIG:
        two_d = lambda a: a.reshape(-1, a.shape[-1])
        d, nm, nv = _adamw(two_d(wts[n]), two_d(grads[n]), two_d(mom[n]), two_d(var[n]), name=f"adamw_{n}")
        delta[n], new_m[n], new_v[n] = d.reshape(shapes[n]), nm.reshape(shapes[n]), nv.reshape(shapes[n])
    packs = [_pack_small(src, src["hgrn_lb_logits"]) for src in (wts, grads, mom, var)]
    outs = _adamw(*packs, name="adamw_small")
    for dst, buf in zip((delta, new_m, new_v), outs):
        dst.update(_unpack_small(buf, lb_sh))

    return (loss, grad_x[None], *[grads[n] for n in WEIGHTS], *[delta[n] for n in WEIGHTS],
            *[new_m[n] for n in WEIGHTS], *[new_v[n] for n in WEIGHTS])
```

```python
import functools
from typing import Callable, NamedTuple

import jax
import jax.numpy as jnp
from jax import lax
from jax.experimental import pallas as pl
from jax.experimental.pallas import tpu as pltpu

F32, BF16 = jnp.float32, jnp.bfloat16

D_MODEL = 1024
DEPTH = 4
HEAD_DIM = 64
N_Q_HEADS = 16
N_KV_HEADS = 4
GROUP = 4
Q_DIM = 1024
KV_DIM = 256
WINDOW = 128
ROPE_DIM = 16
ROPE_THETA = 500000.0
HGRN_HEADS = 8
HGRN_KEY = 128
HGRN_CHUNK = 64
D_FF = 2816
PLE_DIM = 256
ALPHA = (2 * DEPTH) ** 0.25
LN_EPS = 1e-5
ADAM_LR, ADAM_B1, ADAM_B2, ADAM_EPS, ADAM_WD, ADAM_STEP = 0.001, 0.9, 0.999, 1e-08, 0.01, 10

LANES = 128
VMEM_LIMIT_BYTES = 56 * 2**20
FACTORED_DECAY_LIMIT = -80.0

NN = ((1,), (0,))
NT = ((1,), (1,))
TN = ((0,), (0,))

N_CHIPS = 4
MESH = pl.DeviceIdType.MESH


def _dot(a, b, dims):
    return lax.dot_general(a, b, (dims, ((), ())), preferred_element_type=F32)


def _cp(*sem):
    return pltpu.CompilerParams(dimension_semantics=sem, vmem_limit_bytes=VMEM_LIMIT_BYTES)


def _rows(t, cap=512):
    return min(cap, t)


def _pick(n, cap):
    best = None
    for d in range(LANES, min(n, cap) + 1, LANES):
        if n % d == 0:
            best = d
    assert best is not None, (n, cap)
    return best


def _sigmoid(x):
    return jax.nn.sigmoid(x)


def _ln_fwd(u, g, b):
    mu = jnp.mean(u, axis=-1, keepdims=True)
    xc = u - mu
    var = jnp.mean(xc * xc, axis=-1, keepdims=True)
    rstd = lax.rsqrt(var + LN_EPS)
    xhat = xc * rstd
    return xhat * g + b, xhat, rstd


def _ln_bwd(dy, xhat, rstd, g):
    dxh = dy * g
    m1 = jnp.mean(dxh, axis=-1, keepdims=True)
    m2 = jnp.mean(dxh * xhat, axis=-1, keepdims=True)
    du = rstd * (dxh - m1 - xhat * m2)
    return du, jnp.sum(dy * xhat, axis=0, keepdims=True), jnp.sum(dy, axis=0, keepdims=True)


def _full(shape):
    nd = len(shape)
    return pl.BlockSpec(shape, lambda *_: (0,) * nd)


ANY = pl.BlockSpec(memory_space=pl.ANY)


class _Guest(NamedTuple):
    args: tuple
    out_shape: tuple
    sems: tuple
    start: Callable
    finish: Callable


def _call(body, *, grid, in_specs, out_specs, out_shape, args, sem, name, scratch_shapes=(), guest=None):
    n_in, n_out, n_scr = len(args), len(out_shape), len(scratch_shapes)
    if guest is None:
        res = pl.pallas_call(body, grid=grid, in_specs=list(in_specs), out_specs=list(out_specs), out_shape=list(out_shape),
                             scratch_shapes=list(scratch_shapes), compiler_params=_cp(*sem), name=name)(*args)
        return list(res), []
    g_in, g_out = len(guest.args), len(guest.out_shape)

    def hosted(*refs):
        cuts = [n_in, g_in, n_out, g_out, n_scr]
        parts, pos = [], 0
        for n in cuts:
            parts.append(refs[pos:pos + n])
            pos += n
        h_in, gi, h_out, go, h_scr = parts
        gs = refs[pos:]
        ids = [pl.program_id(d) for d in range(len(grid))]
        first = functools.reduce(jnp.logical_and, [i == 0 for i in ids])
        last = functools.reduce(jnp.logical_and, [i == n - 1 for i, n in zip(ids, grid)])

        @pl.when(first)
        def _():
            guest.start(gi, go, gs)
        body(*h_in, *h_out, *h_scr)

        @pl.when(last)
        def _():
            guest.finish(gi, go, gs)

    res = pl.pallas_call(
        hosted, grid=grid, in_specs=list(in_specs) + [ANY] * g_in, out_specs=list(out_specs) + [ANY] * g_out,
        out_shape=list(out_shape) + list(guest.out_shape), scratch_shapes=list(scratch_shapes) + list(guest.sems),
        compiler_params=_cp(*(("arbitrary",) * len(grid))), name=name)(*args, *guest.args)
    return list(res[:n_out]), list(res[n_out:])


def _run_guest(guest, *, name):
    g_in = len(guest.args)

    def body(*refs):
        gi, go, gs = refs[:g_in], refs[g_in:g_in + len(guest.out_shape)], refs[g_in + len(guest.out_shape):]
        guest.start(gi, go, gs)
        guest.finish(gi, go, gs)

    return pl.pallas_call(body, in_specs=[ANY] * g_in, out_specs=[ANY] * len(guest.out_shape), out_shape=list(guest.out_shape),
                          scratch_shapes=list(guest.sems), name=name)(*guest.args)


def _mm_nn(a, w, *, out_dtype, name):
    t, k = a.shape
    n = w.shape[1]
    tm, tn = _rows(t), _pick(n, 1408)

    def body(a_ref, w_ref, o_ref):
        o_ref[...] = _dot(a_ref[...], w_ref[...], NN).astype(out_dtype)

    return pl.pallas_call(
        body, grid=(n // tn, t // tm),
        in_specs=[pl.BlockSpec((tm, k), lambda j, i: (i, 0)), pl.BlockSpec((k, tn), lambda j, i: (0, j))],
        out_specs=pl.BlockSpec((tm, tn), lambda j, i: (i, j)),
        out_shape=jax.ShapeDtypeStruct((t, n), out_dtype),
        compiler_params=_cp("parallel", "parallel"), name=name)(a, w)


def _mm_tn(a, b, *, name, guest=None):
    r, m = a.shape
    n = b.shape[1]
    tr, tm, tn = _rows(r), _pick(m, 1408), _pick(n, 1408)

    def body(a_ref, b_ref, o_ref):
        @pl.when(pl.program_id(2) == 0)
        def _():
            o_ref[...] = jnp.zeros_like(o_ref)
        o_ref[...] += _dot(a_ref[...].astype(BF16), b_ref[...].astype(BF16), TN)

    res, extra = _call(
        body, grid=(m // tm, n // tn, r // tr),
        in_specs=[pl.BlockSpec((tr, tm), lambda i, j, k: (k, i)), pl.BlockSpec((tr, tn), lambda i, j, k: (k, j))],
        out_specs=[pl.BlockSpec((tm, tn), lambda i, j, k: (i, j))],
        out_shape=[jax.ShapeDtypeStruct((m, n), F32)], args=(a, b),
        sem=("parallel", "parallel", "arbitrary"), name=name, guest=guest)
    return res[0] if guest is None else (res[0], extra)


def _mm_tn_p(p, layer, b, *, name):
    t = p.shape[2]
    n = b.shape[1]
    tr = _rows(t)

    def body(a_ref, b_ref, o_ref):
        @pl.when(pl.program_id(0) == 0)
        def _():
            o_ref[...] = jnp.zeros_like(o_ref)
        o_ref[...] += _dot(a_ref[...].astype(BF16), b_ref[...], TN)

    return pl.pallas_call(
        body, grid=(t // tr,),
        in_specs=[pl.BlockSpec((None, None, tr, PLE_DIM), lambda k: (layer, 0, k, 0)),
                  pl.BlockSpec((tr, n), lambda k: (k, 0))],
        out_specs=pl.BlockSpec((PLE_DIM, n), lambda k: (0, 0)),
        out_shape=jax.ShapeDtypeStruct((PLE_DIM, n), F32),
        compiler_params=_cp("arbitrary"), name=name)(p, b)


def _mm_res_ln(a, w, resid, g, b, *, name):
    t, k = a.shape
    tm = _rows(t)

    def body(a_ref, w_ref, r_ref, g_ref, b_ref, y_ref, ybf_ref, xh_ref, rs_ref):
        acc = _dot(a_ref[...], w_ref[...], NN)
        y, xh, rs = _ln_fwd(ALPHA * r_ref[...] + acc, g_ref[...], b_ref[...])
        y_ref[...] = y
        ybf_ref[...] = y.astype(BF16)
        xh_ref[...] = xh
        rs_ref[...] = rs

    row = pl.BlockSpec((tm, D_MODEL), lambda i: (i, 0))
    return pl.pallas_call(
        body, grid=(t // tm,),
        in_specs=[pl.BlockSpec((tm, k), lambda i: (i, 0)), _full((k, D_MODEL)), row, _full((1, D_MODEL)), _full((1, D_MODEL))],
        out_specs=[row, row, row, pl.BlockSpec((tm, 1), lambda i: (i, 0))],
        out_shape=[jax.ShapeDtypeStruct((t, D_MODEL), F32), jax.ShapeDtypeStruct((t, D_MODEL), BF16),
                   jax.ShapeDtypeStruct((t, D_MODEL), F32), jax.ShapeDtypeStruct((t, 1), F32)],
        compiler_params=_cp("parallel"), name=name)(a, w, resid, g, b)


def _ffn_in(x_bf, w, *, name):
    t = x_bf.shape[0]
    tm, tn = _rows(t), _pick(D_FF, 1408)
    nb = D_FF // tn

    def body(x_ref, wg_ref, wu_ref, g_ref, u_ref, h_ref):
        x = x_ref[...]
        g = _dot(x, wg_ref[...], NN)
        u = _dot(x, wu_ref[...], NN)
        g_ref[...] = g.astype(BF16)
        u_ref[...] = u.astype(BF16)
        h_ref[...] = (g * _sigmoid(g) * u).astype(BF16)

    tile = pl.BlockSpec((tm, tn), lambda j, i: (i, j))
    shp = jax.ShapeDtypeStruct((t, D_FF), BF16)
    return pl.pallas_call(
        body, grid=(nb, t // tm),
        in_specs=[pl.BlockSpec((tm, D_MODEL), lambda j, i: (i, 0)),
                  pl.BlockSpec((D_MODEL, tn), lambda j, i: (0, j)),
                  pl.BlockSpec((D_MODEL, tn), lambda j, i: (0, j + nb))],
        out_specs=[tile, tile, tile], out_shape=[shp, shp, shp],
        compiler_params=_cp("parallel", "parallel"), name=name)(x_bf, w, w)


def _ple_fwd(x, x_bf, p, layer, wg, wp, *, name):
    t = x.shape[0]
    tm = _rows(t)

    def body(x_ref, xbf_ref, p_ref, wg_ref, wp_ref, y_ref, ybf_ref, gate_ref, pp_ref):
        gate = _sigmoid(_dot(xbf_ref[...], wg_ref[...], NN))
        pp = _dot(p_ref[...].astype(BF16), wp_ref[...], NN)
        y = x_ref[...] + gate * pp
        y_ref[...] = y
        ybf_ref[...] = y.astype(BF16)
        gate_ref[...] = gate.astype(BF16)
        pp_ref[...] = pp.astype(BF16)

    row = pl.BlockSpec((tm, D_MODEL), lambda i: (i, 0))
    f32, bf = jax.ShapeDtypeStruct((t, D_MODEL), F32), jax.ShapeDtypeStruct((t, D_MODEL), BF16)
    return pl.pallas_call(
        body, grid=(t // tm,),
        in_specs=[row, row, pl.BlockSpec((None, None, tm, PLE_DIM), lambda i: (layer, 0, i, 0)),
                  _full((D_MODEL, D_MODEL)), _full((PLE_DIM, D_MODEL))],
        out_specs=[row, row, row, row], out_shape=[f32, bf, bf, bf],
        compiler_params=_cp("parallel"), name=name)(x, x_bf, p, wg, wp)


def _loss_head(y, target, *, name):
    t = y.shape[0]
    tm = _rows(t)

    def body(y_ref, t_ref, loss_ref, dy_ref):
        e = y_ref[...] - t_ref[...]

        @pl.when(pl.program_id(0) == 0)
        def _():
            loss_ref[...] = jnp.zeros_like(loss_ref)
        sq = jnp.sum(jnp.sum(e * e, axis=1, keepdims=True), axis=0, keepdims=True)
        loss_ref[...] += sq * (0.5 / D_MODEL)
        dy_ref[...] = e * (1.0 / D_MODEL)

    row = pl.BlockSpec((tm, D_MODEL), lambda i: (i, 0))
    return pl.pallas_call(
        body, grid=(t // tm,), in_specs=[row, row],
        out_specs=[_full((1, 1)), row],
        out_shape=[jax.ShapeDtypeStruct((1, 1), F32), jax.ShapeDtypeStruct((t, D_MODEL), F32)],
        compiler_params=_cp("arbitrary"), name=name)(y, target)


def _ple_bwd(dx3, gate, pp, wg, xhat, rstd, g, *, name, guest=None):
    t = dx3.shape[0]
    tm = _rows(t)

    def body(dx_ref, gate_ref, pp_ref, wg_ref, xh_ref, rs_ref, g_ref,
             du_ref, dubf_ref, dpre_ref, dpp_ref, dg_ref, db_ref):
        dx = dx_ref[...]
        gate = gate_ref[...].astype(F32)
        dpre = (dx * pp_ref[...].astype(F32) * gate * (1.0 - gate)).astype(BF16)
        dpre_ref[...] = dpre
        dpp_ref[...] = (dx * gate).astype(BF16)
        dx2 = dx + _dot(dpre, wg_ref[...], NT)
        du, dg, db = _ln_bwd(dx2, xh_ref[...], rs_ref[...], g_ref[...])
        du_ref[...] = du
        dubf_ref[...] = du.astype(BF16)

        @pl.when(pl.program_id(0) == 0)
        def _():
            dg_ref[...] = jnp.zeros_like(dg_ref)
            db_ref[...] = jnp.zeros_like(db_ref)
        dg_ref[...] += dg
        db_ref[...] += db

    row = pl.BlockSpec((tm, D_MODEL), lambda i: (i, 0))
    vec = _full((1, D_MODEL))
    f32, bf = jax.ShapeDtypeStruct((t, D_MODEL), F32), jax.ShapeDtypeStruct((t, D_MODEL), BF16)
    v32 = jax.ShapeDtypeStruct((1, D_MODEL), F32)
    res, extra = _call(
        body, grid=(t // tm,),
        in_specs=[row, row, row, _full((D_MODEL, D_MODEL)), row, pl.BlockSpec((tm, 1), lambda i: (i, 0)), vec],
        out_specs=[row, row, row, row, vec, vec], out_shape=[f32, bf, bf, bf, v32, v32],
        args=(dx3, gate, pp, wg, xhat, rstd, g), sem=("arbitrary",), name=name, guest=guest)
    return res, extra


def _ffn_out_bwd(du_bf, w_out, g, u, *, name):
    t = du_bf.shape[0]
    tm, tn = _rows(t), _pick(D_FF, 1408)

    def body(du_ref, w_ref, g_ref, u_ref, dg_ref, dup_ref):
        dh = _dot(du_ref[...], w_ref[...], NT)
        gv = g_ref[...].astype(F32)
        sig = _sigmoid(gv)
        dg_ref[...] = (dh * u_ref[...].astype(F32) * sig * (1.0 + gv * (1.0 - sig))).astype(BF16)
        dup_ref[...] = (dh * gv * sig).astype(BF16)

    tile = pl.BlockSpec((tm, tn), lambda j, i: (i, j))
    shp = jax.ShapeDtypeStruct((t, D_FF), BF16)
    return pl.pallas_call(
        body, grid=(D_FF // tn, t // tm),
        in_specs=[pl.BlockSpec((tm, D_MODEL), lambda j, i: (i, 0)), pl.BlockSpec((tn, D_MODEL), lambda j, i: (j, 0)),
                  tile, tile],
        out_specs=[tile, tile], out_shape=[shp, shp],
        compiler_params=_cp("parallel", "parallel"), name=name)(du_bf, w_out, g, u)


def _mm_nt(parts, w, *, tk, resid=None, ln=None, out_dtype=F32, name, guest=None):
    t, kp = parts[0].shape
    npart = len(parts)
    nk = kp // tk
    nkt = nk * npart
    tm = _rows(t)
    n_in = npart + 1 + (resid is not None) + (3 if ln is not None else 0)

    def body(*refs):
        a_refs, w_ref = refs[:npart], refs[npart]
        pos = npart + 1
        r_ref = None
        if resid is not None:
            r_ref = refs[pos]
            pos += 1
        if ln is not None:
            xh_ref, rs_ref, g_ref = refs[pos:pos + 3]
        outs, acc_ref = refs[n_in:-1], refs[-1]
        i, k = pl.program_id(0), pl.program_id(1)

        @pl.when(k == 0)
        def _():
            acc_ref[...] = jnp.zeros_like(acc_ref)

        for pi in range(npart):
            @pl.when((k >= pi * nk) & (k < (pi + 1) * nk))
            def _(pi=pi):
                acc_ref[...] += _dot(a_refs[pi][...], w_ref[...], NT)

        @pl.when(k == nkt - 1)
        def _():
            val = acc_ref[...]
            if r_ref is not None:
                val = val + ALPHA * r_ref[...]
            if ln is None:
                outs[0][...] = val.astype(out_dtype)
            else:
                du, dg, db = _ln_bwd(val, xh_ref[...], rs_ref[...], g_ref[...])
                outs[0][...] = du
                outs[1][...] = du.astype(BF16)

                @pl.when(i == 0)
                def _():
                    outs[2][...] = jnp.zeros_like(outs[2])
                    outs[3][...] = jnp.zeros_like(outs[3])
                outs[2][...] += dg
                outs[3][...] += db

    row = pl.BlockSpec((tm, D_MODEL), lambda i, k: (i, 0))
    vec = pl.BlockSpec((1, D_MODEL), lambda i, k: (0, 0))
    in_specs = [pl.BlockSpec((tm, tk), functools.partial(lambda i, k, lo: (i, jnp.clip(k - lo, 0, nk - 1)), lo=pi * nk))
                for pi in range(npart)]
    in_specs.append(pl.BlockSpec((D_MODEL, tk), lambda i, k: (0, k)))
    args = list(parts) + [w]
    if resid is not None:
        in_specs.append(row)
        args.append(resid)
    if ln is not None:
        in_specs += [row, pl.BlockSpec((tm, 1), lambda i, k: (i, 0)), vec]
        args += list(ln)
        out_specs = [row, row, vec, vec]
        out_shape = [jax.ShapeDtypeStruct((t, D_MODEL), F32), jax.ShapeDtypeStruct((t, D_MODEL), BF16),
                     jax.ShapeDtypeStruct((1, D_MODEL), F32), jax.ShapeDtypeStruct((1, D_MODEL), F32)]
    else:
        out_specs = [row]
        out_shape = [jax.ShapeDtypeStruct((t, D_MODEL), out_dtype)]
    res, extra = _call(
        body, grid=(t // tm, nkt), in_specs=in_specs, out_specs=out_specs, out_shape=out_shape,
        scratch_shapes=[pltpu.VMEM((tm, D_MODEL), F32)], args=tuple(args),
        sem=("arbitrary", "arbitrary"), name=name, guest=guest)
    res = res if ln is not None else res[0]
    return res if guest is None else (res, extra)


def _rope_tables(t):
    half = ROPE_DIM // 2
    inv = ROPE_THETA ** (-jnp.arange(0, ROPE_DIM, 2, dtype=F32) / ROPE_DIM)
    ang = jnp.arange(t, dtype=F32)[:, None] * inv[None, :]
    cos, sin = jnp.cos(ang), jnp.sin(ang)
    pad = HEAD_DIM - ROPE_DIM
    c = jnp.concatenate([cos, cos, jnp.ones((t, pad), F32)], axis=1)
    s_hi = jnp.concatenate([jnp.zeros((t, half), F32), sin, jnp.zeros((t, pad), F32)], axis=1)
    s_lo = jnp.concatenate([-sin, jnp.zeros((t, half + pad), F32)], axis=1)
    rep = LANES // HEAD_DIM
    return jnp.tile(c, (1, rep)), jnp.tile(s_hi, (1, rep)), jnp.tile(s_lo, (1, rep))


def _rope(x, c, s_hi, s_lo, sign):
    half = ROPE_DIM // 2
    parts = []
    for j in range(x.shape[1] // LANES):
        xg = x[:, j * LANES:(j + 1) * LANES]
        rot = pltpu.roll(xg, half, 1) * s_hi + pltpu.roll(xg, LANES - half, 1) * s_lo
        parts.append(xg * c + sign * rot)
    return jnp.concatenate(parts, axis=1)


def _qkv_rope(x_bf, w, tables, *, name):
    t = x_bf.shape[0]
    tm = _rows(t)
    n = Q_DIM + 2 * KV_DIM

    def body(x_ref, w_ref, c_ref, sh_ref, sl_ref, q_ref, k_ref, v_ref):
        acc = _dot(x_ref[...], w_ref[...], NN)
        c, sh, sl = c_ref[...], sh_ref[...], sl_ref[...]
        q_ref[...] = (_rope(acc[:, :Q_DIM], c, sh, sl, 1.0) * HEAD_DIM ** -0.5).astype(BF16)
        k_ref[...] = _rope(acc[:, Q_DIM:Q_DIM + KV_DIM], c, sh, sl, 1.0).astype(BF16)
        v_ref[...] = acc[:, Q_DIM + KV_DIM:].astype(BF16)

    tab = pl.BlockSpec((tm, LANES), lambda i: (i, 0))
    return pl.pallas_call(
        body, grid=(t // tm,),
        in_specs=[pl.BlockSpec((tm, D_MODEL), lambda i: (i, 0)), _full((D_MODEL, n)), tab, tab, tab],
        out_specs=[pl.BlockSpec((tm, Q_DIM), lambda i: (i, 0)), pl.BlockSpec((tm, KV_DIM), lambda i: (i, 0)),
                   pl.BlockSpec((tm, KV_DIM), lambda i: (i, 0))],
        out_shape=[jax.ShapeDtypeStruct((t, Q_DIM), BF16), jax.ShapeDtypeStruct((t, KV_DIM), BF16),
                   jax.ShapeDtypeStruct((t, KV_DIM), BF16)],
        compiler_params=_cp("parallel"), name=name)(x_bf, w, *tables)


ATT_ROWS = 256
ATT_STRIP = 64


def _window_specs(t, tq):
    r = tq // WINDOW
    last = t // WINDOW - 1
    return [pl.BlockSpec((WINDOW, KV_DIM), lambda i: (jnp.maximum(i * r - 1, 0), 0)),
            pl.BlockSpec((tq, KV_DIM), lambda i: (i, 0)),
            pl.BlockSpec((WINDOW, KV_DIM), lambda i: (jnp.minimum((i + 1) * r, last), 0))]


def _att_valid(base, t):
    rows = GROUP * WINDOW
    qpos = base + (lax.broadcasted_iota(jnp.int32, (rows, 3 * WINDOW), 0) & (WINDOW - 1))
    kpos = base - WINDOW + lax.broadcasted_iota(jnp.int32, (rows, 3 * WINDOW), 1)
    return (jnp.abs(qpos - kpos) <= WINDOW) & (kpos >= 0) & (kpos < t)


def _stack_heads(x, r0, g):
    return jnp.concatenate(
        [x[r0:r0 + WINDOW, (GROUP * g + h) * HEAD_DIM:(GROUP * g + h + 1) * HEAD_DIM] for h in range(GROUP)], axis=0)


def _sink_column(sink_ref, g):
    return jnp.concatenate([jnp.full((WINDOW, 1), sink_ref[GROUP * g + h], F32) for h in range(GROUP)], axis=0)


def _unstack_heads(pieces):
    return jnp.concatenate([pieces[g][h * WINDOW:(h + 1) * WINDOW] for g in range(N_KV_HEADS) for h in range(GROUP)], axis=1)


def _attn_fwd(q, k, v, sink, *, name, guest=None):
    t = q.shape[0]
    tq = min(ATT_ROWS, t)
    nsb = tq // WINDOW

    def body(sink_ref, q_ref, kp_ref, kc_ref, kn_ref, vp_ref, vc_ref, vn_ref, o_ref, l_ref):
        i = pl.program_id(0)
        qv = q_ref[...]
        kw = jnp.concatenate([kp_ref[...], kc_ref[...], kn_ref[...]], axis=0)
        vw = jnp.concatenate([vp_ref[...], vc_ref[...], vn_ref[...]], axis=0)
        ones = jnp.ones((3 * WINDOW, HEAD_DIM), BF16)
        for sb in range(nsb):
            r0 = sb * WINDOW
            bias =jnp.where(_att_valid(i * tq + r0, t)[:WINDOW], 0.0, -1e30)
            pieces = []
            for hh in range(N_Q_HEADS):
                g, h = hh // GROUP, hh % GROUP
                qh = qv[r0:r0 + WINDOW, hh * HEAD_DIM:(hh + 1) * HEAD_DIM]
                kg = kw[r0:r0 + 3 * WINDOW, g * HEAD_DIM:(g + 1) * HEAD_DIM]
                vg = jnp.concatenate([vw[r0:r0 + 3 * WINDOW, g * HEAD_DIM:(g + 1) * HEAD_DIM], ones], axis=1)
                s = _dot(qh, kg, NT) + bias
                snk = sink_ref[hh]
                m = jnp.maximum(jnp.max(s, axis=1, keepdims=True), snk)
                ov = _dot(jnp.exp(s - m).astype(BF16), vg, NN)
                den = ov[:, HEAD_DIM:HEAD_DIM + 1] + jnp.exp(snk - m)
                pieces.append(ov[:, :HEAD_DIM] * (1.0 / den))
                l_ref[g, sb, h * WINDOW:(h + 1) * WINDOW, :] = m + jnp.log(den)
            o_ref[r0:r0 + WINDOW, :] = jnp.concatenate(pieces, axis=1).astype(BF16)

    return _call(
        body, grid=(t // tq,),
        in_specs=[pl.BlockSpec(memory_space=pltpu.SMEM), pl.BlockSpec((tq, Q_DIM), lambda i: (i, 0))]
        + _window_specs(t, tq) + _window_specs(t, tq),
        out_specs=[pl.BlockSpec((tq, Q_DIM), lambda i: (i, 0)),
                   pl.BlockSpec((N_KV_HEADS, nsb, GROUP * WINDOW, 1), lambda i: (0, i, 0, 0))],
        out_shape=[jax.ShapeDtypeStruct((t, Q_DIM), BF16),
                   jax.ShapeDtypeStruct((N_KV_HEADS, t // WINDOW, GROUP * WINDOW, 1), F32)],
        args=(sink, q, k, k, k, v, v, v), sem=("parallel",), name=name, guest=guest)


def _attn_bwd(q, k, v, o, do, lse, sink, *, name):
    t = q.shape[0]
    tq = min(ATT_ROWS, t)
    nsb = tq // WINDOW

    def body(sink_ref, q_ref, o_ref, do_ref, l_ref, kp_ref, kc_ref, kn_ref, vp_ref, vc_ref, vn_ref,
             dq_ref, dkw_ref, dvw_ref, dsink_ref):
        i = pl.program_id(0)
        qv, ov, dov = q_ref[...], o_ref[...], do_ref[...]
        kw = jnp.concatenate([kp_ref[...], kc_ref[...], kn_ref[...]], axis=0)
        vw = jnp.concatenate([vp_ref[...], vc_ref[...], vn_ref[...]], axis=0)
        lane16 = lax.broadcasted_iota(jnp.int32, (1, N_Q_HEADS), 1)
        dsink = jnp.zeros((1, N_Q_HEADS), F32)
        for sb in range(nsb):
            r0 = sb * WINDOW
            valid = _att_valid(i * tq + r0, t)
            dq_p, dk_p, dv_p = [], [], []
            for g in range(N_KV_HEADS):
                qs, dos = _stack_heads(qv, r0, g), _stack_heads(dov, r0, g)
                delta = jnp.sum(dos.astype(F32) * _stack_heads(ov, r0, g).astype(F32), axis=1, keepdims=True)
                kg = kw[r0:r0 + 3 * WINDOW, g * HEAD_DIM:(g + 1) * HEAD_DIM]
                vg = vw[r0:r0 + 3 * WINDOW, g * HEAD_DIM:(g + 1) * HEAD_DIM]
                lse_col = l_ref[g, sb]
                pr = jnp.where(valid, jnp.exp(_dot(qs, kg, NT) - lse_col), 0.0)
                ds = (pr * (_dot(dos, vg, NT) - delta)).astype(BF16)
                dq_p.append(_dot(ds, kg, NN))
                dk_p.append(_dot(ds, qs, TN))
                dv_p.append(_dot(pr.astype(BF16), dos, TN))
                ps = jnp.exp(_sink_column(sink_ref, g) - lse_col) * delta
                for h in range(GROUP):
                    tot = jnp.sum(ps[h * WINDOW:(h + 1) * WINDOW], axis=0, keepdims=True)
                    dsink = dsink - jnp.where(lane16 == GROUP * g + h, tot, 0.0)
            dq_ref[r0:r0 + WINDOW, :] = _unstack_heads(dq_p)
            dkw_ref[sb] = jnp.concatenate(dk_p, axis=1)
            dvw_ref[sb] = jnp.concatenate(dv_p, axis=1)

        @pl.when(i == 0)
        def _():
            dsink_ref[...] = jnp.zeros_like(dsink_ref)
        dsink_ref[...] += dsink

    rowq = pl.BlockSpec((tq, Q_DIM), lambda i: (i, 0))
    win = pl.BlockSpec((nsb, 3 * WINDOW, KV_DIM), lambda i: (i, 0, 0))
    wshape = jax.ShapeDtypeStruct((t // WINDOW, 3 * WINDOW, KV_DIM), F32)
    return pl.pallas_call(
        body, grid=(t // tq,),
        in_specs=[pl.BlockSpec(memory_space=pltpu.SMEM), rowq, rowq, rowq,
                  pl.BlockSpec((N_KV_HEADS, nsb, GROUP * WINDOW, 1), lambda i: (0, i, 0, 0))]
        + _window_specs(t, tq) + _window_specs(t, tq),
        out_specs=[rowq, win, win, _full((1, N_Q_HEADS))],
        out_shape=[jax.ShapeDtypeStruct((t, Q_DIM), F32), wshape, wshape, jax.ShapeDtypeStruct((1, N_Q_HEADS), F32)],
        compiler_params=_cp("arbitrary"), name=name)(sink, q, o, do, lse, k, k, k, v, v, v)


def _attn_post_bwd(dq, dkw, dvw, tables, *, name):
    t = dq.shape[0]
    nb = t // WINDOW
    n = Q_DIM + 2 * KV_DIM

    def body(dq_ref, ka_ref, kb_ref, kc_ref, va_ref, vb_ref, vc_ref, c_ref, sh_ref, sl_ref, o_ref):
        j = pl.program_id(0)
        lo = (j > 0).astype(F32)
        hi = (j < nb - 1).astype(F32)
        c, sh, sl = c_ref[...], sh_ref[...], sl_ref[...]
        dk = ka_ref[...] * hi + kb_ref[...] + kc_ref[...] * lo
        dv = va_ref[...] * hi + vb_ref[...] + vc_ref[...] * lo
        o_ref[:, :Q_DIM] = (_rope(dq_ref[...], c, sh, sl, -1.0) * HEAD_DIM ** -0.5).astype(BF16)
        o_ref[:, Q_DIM:Q_DIM + KV_DIM] = _rope(dk, c, sh, sl, -1.0).astype(BF16)
        o_ref[:, Q_DIM + KV_DIM:] = dv.astype(BF16)

    wins = [pl.BlockSpec((None, WINDOW, KV_DIM), lambda j: (jnp.minimum(j + 1, nb - 1), 0, 0)),
            pl.BlockSpec((None, WINDOW, KV_DIM), lambda j: (j, 1, 0)),
            pl.BlockSpec((None, WINDOW, KV_DIM), lambda j: (jnp.maximum(j - 1, 0), 2, 0))]
    tab = pl.BlockSpec((WINDOW, LANES), lambda j: (j, 0))
    return pl.pallas_call(
        body, grid=(nb,),
        in_specs=[pl.BlockSpec((WINDOW, Q_DIM), lambda j: (j, 0))] + wins + wins + [tab, tab, tab],
        out_specs=pl.BlockSpec((WINDOW, n), lambda j: (j, 0)),
        out_shape=jax.ShapeDtypeStruct((t, n), BF16),
        compiler_params=_cp("parallel"), name=name)(dq, dkw, dkw, dkw, dvw, dvw, dvw, *tables)


HGRN_ROWS = 512


def _cum_mask(rev, transpose=False):
    row = lax.broadcasted_iota(jnp.int32, (HGRN_CHUNK, HGRN_CHUNK), 0)
    col = lax.broadcasted_iota(jnp.int32, (HGRN_CHUNK, HGRN_CHUNK), 1)
    return (row <= col) if rev != transpose else (row >= col)


def _gates(zq, zf, lb):
    q = zq * _sigmoid(zq)
    sig = _sigmoid(zf)
    f = lb + (1.0 - lb) * sig
    k = (1.0 - lb) * (1.0 - sig)
    return q, sig, f, k


def _intra_exact(q, k, b, mask):
    rows = lax.broadcasted_iota(jnp.int32, (HGRN_CHUNK, 1), 0)
    cols = lax.broadcasted_iota(jnp.int32, (HGRN_CHUNK, HGRN_CHUNK), 1)

    def it(s, a):
        pick = rows == s
        b_s = jnp.sum(jnp.where(pick, b, 0.0), axis=0, keepdims=True)
        k_s = jnp.sum(jnp.where(pick, k, 0.0), axis=0, keepdims=True)
        col = jnp.sum(q * jnp.exp(jnp.minimum(b - b_s, 0.0)) * k_s, axis=1, keepdims=True)
        return jnp.where(cols == s, col, a)

    a = lax.fori_loop(0, HGRN_CHUNK, it, jnp.zeros((HGRN_CHUNK, HGRN_CHUNK), F32))
    return jnp.where(mask, a, 0.0)


def _intra_exact_bwd(q, k, b, da):
    rows = lax.broadcasted_iota(jnp.int32, (HGRN_CHUNK, 1), 0)
    cols = lax.broadcasted_iota(jnp.int32, (HGRN_CHUNK, HGRN_CHUNK), 1)

    def it(s, carry):
        dq, dk = carry
        pick = rows == s
        b_s = jnp.sum(jnp.where(pick, b, 0.0), axis=0, keepdims=True)
        k_s = jnp.sum(jnp.where(pick, k, 0.0), axis=0, keepdims=True)
        w = jnp.sum(jnp.where(cols == s, da, 0.0), axis=1, keepdims=True) * jnp.exp(jnp.minimum(b - b_s, 0.0))
        dq = dq + w * k_s
        dk = jnp.where(pick, jnp.sum(w * q, axis=0, keepdims=True), dk)
        return dq, dk

    z = jnp.zeros((HGRN_CHUNK, HGRN_KEY), F32)
    return lax.fori_loop(0, HGRN_CHUNK, it, (z, z))


def _hgrn_scan_fwd(z, lb, rev, *, name):
    t = z.shape[0]
    rb = min(HGRN_ROWS, t)
    nb, nch = t // rb, rb // HGRN_CHUNK
    fcol = HGRN_HEADS * (2 if rev else 1)

    def blk(n):
        return nb - 1 - n if rev else n

    def body(zq_ref, zf_ref, zv_ref, lb_ref, o_ref, s_ref, st_ref):
        @pl.when(pl.program_id(1) == 0)
        def _():
            st_ref[...] = jnp.zeros_like(st_ref)
        lbv = lb_ref[...]
        cmask = _cum_mask(rev)
        cum = cmask.astype(F32)

        def chunk(c, carry):
            cc = nch - 1 - c if rev else c
            r0 = pl.multiple_of(cc * HGRN_CHUNK, HGRN_CHUNK)
            sl = pl.ds(r0, HGRN_CHUNK)
            q, _, f, k = _gates(zq_ref[sl, :], zf_ref[sl, :], lbv)
            v = zv_ref[sl, :].astype(BF16)
            g = jnp.log(f)
            b = jnp.dot(cum, g, precision=lax.Precision.HIGHEST, preferred_element_type=F32)
            tot = jnp.sum(g, axis=0, keepdims=True)
            qd = q * jnp.exp(b)
            st = st_ref[...]
            st_bf = st.astype(BF16)
            s_ref[cc] = st_bf
            a = lax.cond(
                jnp.min(tot) >= FACTORED_DECAY_LIMIT,
                lambda: jnp.where(cmask, _dot(qd.astype(BF16), (k * jnp.exp(-b)).astype(BF16), NT), 0.0),
                lambda: _intra_exact(q, k, b, cmask))
            o_ref[sl, :] = _dot(qd.astype(BF16), st_bf, NT) + _dot(a.astype(BF16), v, NN)
            kd = (k * jnp.exp(tot - b)).astype(BF16)
            st_ref[...] = st * jnp.exp(tot) + _dot(v, kd, TN)
            return carry

        lax.fori_loop(0, nch, chunk, 0)

    def col(off):
        return pl.BlockSpec((rb, LANES), lambda h, n: (blk(n), off + h))

    return pl.pallas_call(
        body, grid=(HGRN_HEADS, nb),
        in_specs=[col(0), col(fcol), col(3 * HGRN_HEADS), pl.BlockSpec((None, 1, LANES), lambda h, n: (h, 0, 0))],
        out_specs=[pl.BlockSpec((rb, LANES), lambda h, n: (blk(n), h)),
                   pl.BlockSpec((None, nch, LANES, LANES), lambda h, n: (h, blk(n), 0, 0))],
        out_shape=[jax.ShapeDtypeStruct((t, D_MODEL), F32),
                   jax.ShapeDtypeStruct((HGRN_HEADS, t // HGRN_CHUNK, LANES, LANES), BF16)],
        scratch_shapes=[pltpu.VMEM((LANES, LANES), F32)],
        compiler_params=_cp("parallel", "arbitrary"), name=name)(z, z, z, lb)


def _hgrn_scan_bwd(z, lb, d_o, states, rev, *, name):
    t = z.shape[0]
    rb = min(HGRN_ROWS, t)
    nb, nch = t // rb, rb // HGRN_CHUNK
    fcol = HGRN_HEADS * (2 if rev else 1)

    def blk(n):
        return n if rev else nb - 1 - n

    def body(zq_ref, zf_ref, zv_ref, lb_ref, do_ref, s_ref, dq_ref, dzf_ref, dv_ref, dlb_ref, dst_ref):
        @pl.when(pl.program_id(1) == 0)
        def _():
            dst_ref[...] = jnp.zeros_like(dst_ref)
            dlb_ref[...] = jnp.zeros_like(dlb_ref)
        lbv = lb_ref[...]
        cmask = _cum_mask(rev)
        cum = cmask.astype(F32)
        cum_t = _cum_mask(rev, transpose=True).astype(F32)

        def chunk(c, carry):
            cc = c if rev else nch - 1 - c
            r0 = pl.multiple_of(cc * HGRN_CHUNK, HGRN_CHUNK)
            sl = pl.ds(r0, HGRN_CHUNK)
            zq = zq_ref[sl, :]
            q, sig, f, k = _gates(zq, zf_ref[sl, :], lbv)
            v = zv_ref[sl, :].astype(BF16)
            g = jnp.log(f)
            b = jnp.dot(cum, g, precision=lax.Precision.HIGHEST, preferred_element_type=F32)
            tot = jnp.sum(g, axis=0, keepdims=True)
            eb = jnp.exp(b)
            qd = (q * eb).astype(BF16)
            kd = (k * jnp.exp(tot - b)).astype(BF16)
            do = do_ref[sl, :].astype(BF16)
            st0 = s_ref[cc]
            dst = dst_ref[...]
            dst_bf = dst.astype(BF16)
            da = jnp.where(cmask, _dot(do, v, NT), 0.0)
            factored = jnp.min(tot) >= FACTORED_DECAY_LIMIT

            def fast():
                ke = jnp.exp(-b)
                kf = (k * ke).astype(BF16)
                a = jnp.where(cmask, _dot(qd, kf, NT), 0.0)
                da_bf = da.astype(BF16)
                dqf, dkf = _dot(da_bf, kf, NN), _dot(da_bf, qd, TN)
                return a, dqf * eb, dkf * ke, qd.astype(F32) * dqf - kf.astype(F32) * dkf

            def slow():
                dq_i, dk_i = _intra_exact_bwd(q, k, b, da)
                return _intra_exact(q, k, b, cmask), dq_i, dk_i, q * dq_i - k * dk_i

            a, dq_i, dk_i, db_i = lax.cond(factored, fast, slow)
            dv_ref[sl, :] = _dot(a.astype(BF16), do, TN) + _dot(kd, dst_bf, NT)
            dq_x = _dot(do, st0, NN) * eb
            dq = dq_i + dq_x
            dk_x = _dot(v, dst_bf, NN) * jnp.exp(tot - b)
            dk = dk_i + dk_x
            etot = jnp.exp(tot)
            dtot = jnp.sum(k * dk_x, axis=0, keepdims=True) + etot * jnp.sum(dst * st0.astype(F32), axis=0, keepdims=True)
            dst_ref[...] = dst * etot + _dot(do, qd, TN)
            db = db_i + q * dq_x - k * dk_x
            dg =jnp.dot(cum_t, db, precision=lax.Precision.HIGHEST, preferred_element_type=F32) + dtot
            sq = _sigmoid(zq)
            dq_ref[sl, :] = dq * sq * (1.0 + zq * (1.0 - sq))
            dfk = dg / f - dk
            dzf_ref[sl, :] = (dfk * (1.0 - lbv) * sig * (1.0 - sig)).astype(BF16)
            dlb_ref[...] += jnp.sum(dfk * (1.0 - sig), axis=0, keepdims=True)
            return carry

        lax.fori_loop(0, nch, chunk, 0)

    def col(off):
        return pl.BlockSpec((rb, LANES), lambda h, n: (blk(n), off + h))

    out_col = pl.BlockSpec((rb, LANES), lambda h, n: (blk(n), h))
    return pl.pallas_call(
        body, grid=(HGRN_HEADS, nb),
        in_specs=[col(0), col(fcol), col(3 * HGRN_HEADS), pl.BlockSpec((None, 1, LANES), lambda h, n: (h, 0, 0)),
                  out_col, pl.BlockSpec((None, nch, LANES, LANES), lambda h, n: (h, blk(n), 0, 0))],
        out_specs=[out_col, out_col, out_col, pl.BlockSpec((None, 1, LANES), lambda h, n: (h, 0, 0))],
        out_shape=[jax.ShapeDtypeStruct((t, D_MODEL), F32), jax.ShapeDtypeStruct((t, D_MODEL), BF16),
                   jax.ShapeDtypeStruct((t, D_MODEL), F32), jax.ShapeDtypeStruct((HGRN_HEADS, 1, LANES), F32)],
        scratch_shapes=[pltpu.VMEM((LANES, LANES), F32)],
        compiler_params=_cp("parallel", "arbitrary"), name=name)(z, z, z, lb, d_o, states)


def _head(a, h):
    return a[:, h * LANES:(h + 1) * LANES]


def _hgrn_fwd(z, lb, rev, *, name, guest=None):
    t = z.shape[0]
    rb = min(HGRN_ROWS, t)
    nb, nch = t // rb, rb // HGRN_CHUNK
    heads = range(HGRN_HEADS)

    def blk(n):
        return nb - 1 - n if rev else n

    def body(zq_ref, zf_ref, zv_ref, lb_ref, o_ref, s_ref, st_ref):
        @pl.when(pl.program_id(0) == 0)
        def _():
            st_ref[...] = jnp.zeros_like(st_ref)
        lbv = lb_ref[...]
        cmask = _cum_mask(rev)
        cum = cmask.astype(F32)

        def chunk(c, carry):
            cc = nch - 1 - c if rev else c
            sl = pl.ds(pl.multiple_of(cc * HGRN_CHUNK, HGRN_CHUNK), HGRN_CHUNK)
            q, _, f, k = _gates(zq_ref[sl, :], zf_ref[sl, :], lbv)
            v = zv_ref[sl, :].astype(BF16)
            g = jnp.log(f)
            b = jnp.dot(cum, g, precision=lax.Precision.HIGHEST, preferred_element_type=F32)
            tot = jnp.sum(g, axis=0, keepdims=True)
            qd = (q * jnp.exp(b)).astype(BF16)
            kd = (k * jnp.exp(tot - b)).astype(BF16)
            etot = jnp.exp(tot)

            def factored():
                kf = (k * jnp.exp(-b)).astype(BF16)
                return tuple(jnp.where(cmask, _dot(_head(qd, h), _head(kf, h), NT), 0.0) for h in heads)

            def exact():
                return tuple(_intra_exact(_head(q, h), _head(k, h), _head(b, h), cmask) for h in heads)

            a = lax.cond(jnp.min(tot) >= FACTORED_DECAY_LIMIT, factored, exact)
            for h in heads:
                st = st_ref[h]
                st_bf = st.astype(BF16)
                s_ref[h, cc] = st_bf
                o_ref[sl, h * LANES:(h + 1) * LANES] = (
                    _dot(_head(qd, h), st_bf, NT) + _dot(a[h].astype(BF16), _head(v, h), NN))
                st_ref[h] = st * _head(etot, h) + _dot(_head(v, h), _head(kd, h), TN)
            return carry

        lax.fori_loop(0, nch, chunk, 0)

    def col(j):
        return pl.BlockSpec((rb, D_MODEL), lambda n: (blk(n), j))

    return _call(
        body, grid=(nb,),
        in_specs=[col(0), col(2 if rev else 1), col(3), _full((1, D_MODEL))],
        out_specs=[col(0), pl.BlockSpec((HGRN_HEADS, nch, LANES, LANES), lambda n: (0, blk(n), 0, 0))],
        out_shape=[jax.ShapeDtypeStruct((t, D_MODEL), F32),
                   jax.ShapeDtypeStruct((HGRN_HEADS, t // HGRN_CHUNK, LANES, LANES), BF16)],
        scratch_shapes=[pltpu.VMEM((HGRN_HEADS, LANES, LANES), F32)],
        args=(z, z, z, lb), sem=("arbitrary",), name=name, guest=guest)


def _hgrn_bwd(z, lb, d_o, states, rev, *, name):
    t = z.shape[0]
    rb = min(HGRN_ROWS, t)
    nb, nch = t // rb, rb // HGRN_CHUNK
    heads = range(HGRN_HEADS)

    def blk(n):
        return n if rev else nb - 1 - n

    def body(zq_ref, zf_ref, zv_ref, lb_ref, do_ref, s_ref, dq_ref, dzf_ref, dv_ref, dlb_ref, dst_ref):
        @pl.when(pl.program_id(0) == 0)
        def _():
            dst_ref[...] = jnp.zeros_like(dst_ref)
            dlb_ref[...] = jnp.zeros_like(dlb_ref)
        lbv = lb_ref[...]
        cmask = _cum_mask(rev)
        cum = cmask.astype(F32)
        cum_t = _cum_mask(rev, transpose=True).astype(F32)

        def chunk(c, carry):
            cc = c if rev else nch - 1 - c
            sl = pl.ds(pl.multiple_of(cc * HGRN_CHUNK, HGRN_CHUNK), HGRN_CHUNK)
            zq = zq_ref[sl, :]
            q, sig, f, k = _gates(zq, zf_ref[sl, :], lbv)
            v = zv_ref[sl, :].astype(BF16)
            g = jnp.log(f)
            b = jnp.dot(cum, g, precision=lax.Precision.HIGHEST, preferred_element_type=F32)
            tot = jnp.sum(g, axis=0, keepdims=True)
            eb = jnp.exp(b)
            etb = jnp.exp(tot - b)
            etot = jnp.exp(tot)
            qd = (q * eb).astype(BF16)
            kd = (k * etb).astype(BF16)
            do = do_ref[sl, :].astype(BF16)
            da = [jnp.where(cmask, _dot(_head(do, h), _head(v, h), NT), 0.0) for h in heads]

            def factored():
                ke = jnp.exp(-b)
                kf = (k * ke).astype(BF16)
                res = []
                for h in heads:
                    qh, kh = _head(qd, h), _head(kf, h)
                    da_bf = da[h].astype(BF16)
                    dqf, dkf = _dot(da_bf, kh, NN), _dot(da_bf, qh, TN)
                    res.append((jnp.where(cmask, _dot(qh, kh, NT), 0.0), dqf * _head(eb, h), dkf * _head(ke, h),
                                qh.astype(F32) * dqf - kh.astype(F32) * dkf))
                return tuple(res)

            def exact():
                res = []
                for h in heads:
                    qh, kh, bh = _head(q, h), _head(k, h), _head(b, h)
                    dq_i, dk_i = _intra_exact_bwd(qh, kh, bh, da[h])
                    res.append((_intra_exact(qh, kh, bh, cmask), dq_i, dk_i, qh * dq_i - kh * dk_i))
                return tuple(res)

            intra = lax.cond(jnp.min(tot) >= FACTORED_DECAY_LIMIT, factored, exact)
            dq_p, dk_p, db_p, dtot_p = [], [], [], []
            for h in heads:
                a, dq_i, dk_i, db_i = intra[h]
                doh, vh, qh, kh = _head(do, h), _head(v, h), _head(q, h), _head(k, h)
                st0 = s_ref[h, cc]
                dst = dst_ref[h]
                dst_bf = dst.astype(BF16)
                dv_ref[sl, h * LANES:(h + 1) * LANES] = _dot(a.astype(BF16), doh, TN) + _dot(_head(kd, h), dst_bf, NT)
                dq_x = _dot(doh, st0, NN) * _head(eb, h)
                dk_x = _dot(vh, dst_bf, NN) * _head(etb, h)
                dtot_p.append(jnp.sum(kh * dk_x, axis=0, keepdims=True)
                              + _head(etot, h) * jnp.sum(dst * st0.astype(F32), axis=0, keepdims=True))
                dst_ref[h] = dst * _head(etot, h) + _dot(doh, _head(qd, h), TN)
                dq_p.append(dq_i + dq_x)
                dk_p.append(dk_i + dk_x)
                db_p.append(db_i + qh * dq_x - kh * dk_x)
            dq, dk, db = (jnp.concatenate(x, axis=1) for x in (dq_p, dk_p, db_p))
            dg = jnp.dot(cum_t, db, precision=lax.Precision.HIGHEST, preferred_element_type=F32) + jnp.concatenate(dtot_p, axis=1)
            sq = _sigmoid(zq)
            dq_ref[sl, :] = dq * sq * (1.0 + zq * (1.0 - sq))
            dfk = dg / f - dk
            dzf_ref[sl, :] = (dfk * (1.0 - lbv) * sig * (1.0 - sig)).astype(BF16)
            dlb_ref[...] += jnp.sum(dfk * (1.0 - sig), axis=0, keepdims=True)
            return carry

        lax.fori_loop(0, nch, chunk, 0)

    def col(j):
        return pl.BlockSpec((rb, D_MODEL), lambda n: (blk(n), j))

    return pl.pallas_call(
        body, grid=(nb,),
        in_specs=[col(0), col(2 if rev else 1), col(3), _full((1, D_MODEL)), col(0),
                  pl.BlockSpec((HGRN_HEADS, nch, LANES, LANES), lambda n: (0, blk(n), 0, 0))],
        out_specs=[col(0), col(0), col(0), _full((1, D_MODEL))],
        out_shape=[jax.ShapeDtypeStruct((t, D_MODEL), F32), jax.ShapeDtypeStruct((t, D_MODEL), BF16),
                   jax.ShapeDtypeStruct((t, D_MODEL), F32), jax.ShapeDtypeStruct((1, D_MODEL), F32)],
        scratch_shapes=[pltpu.VMEM((HGRN_HEADS, LANES, LANES), F32)],
        compiler_params=_cp("arbitrary"), name=name)(z, z, z, lb, d_o, states)


def _hgrn_post_fwd(o_f, o_b, z, norm_g, *, name):
    t = o_f.shape[0]
    tm = _rows(t)

    def body(of_ref, ob_ref, gate_ref, ng_ref, y_ref):
        ng = ng_ref[...]
        for h in range(HGRN_HEADS):
            sl = slice(h * LANES, (h + 1) * LANES)
            o = of_ref[:, sl] + ob_ref[:, sl]
            r = lax.rsqrt(jnp.mean(o * o, axis=1, keepdims=True) + LN_EPS)
            gt = gate_ref[:, sl]
            y_ref[:, sl] = (o * r * ng * (gt * _sigmoid(gt))).astype(BF16)

    row = pl.BlockSpec((tm, D_MODEL), lambda i: (i, 0))
    return pl.pallas_call(
        body, grid=(t // tm,),
        in_specs=[row, row, pl.BlockSpec((tm, D_MODEL), lambda i: (i, 4)), _full((1, LANES))],
        out_specs=row, out_shape=jax.ShapeDtypeStruct((t, D_MODEL), BF16),
        compiler_params=_cp("parallel"), name=name)(o_f, o_b, z, norm_g)


def _hgrn_post_bwd(dy, o_f, o_b, z, norm_g, *, name):
    t = o_f.shape[0]
    tm = _rows(t)

    def body(dy_ref, of_ref, ob_ref, gate_ref, ng_ref, do_ref, dgate_ref, dng_ref):
        ng = ng_ref[...]
        dng = jnp.zeros((1, LANES), F32)
        for h in range(HGRN_HEADS):
            sl = slice(h * LANES, (h + 1) * LANES)
            o = of_ref[:, sl] + ob_ref[:, sl]
            r = lax.rsqrt(jnp.mean(o * o, axis=1, keepdims=True) + LN_EPS)
            gt = gate_ref[:, sl]
            sg = _sigmoid(gt)
            dyv = dy_ref[:, sl].astype(F32)
            xn = o * r
            dgate_ref[:, sl] = (dyv * xn * ng * sg * (1.0 + gt * (1.0 - sg))).astype(BF16)
            dn = dyv * gt * sg
            dng = dng + jnp.sum(dn * xn, axis=0, keepdims=True)
            dxn = dn * ng
            do_ref[:, sl] = r * (dxn - xn * jnp.mean(dxn * xn, axis=1, keepdims=True))

        @pl.when(pl.program_id(0) == 0)
        def _():
            dng_ref[...] = jnp.zeros_like(dng_ref)
        dng_ref[...] += dng

    row = pl.BlockSpec((tm, D_MODEL), lambda i: (i, 0))
    return pl.pallas_call(
        body, grid=(t // tm,),
        in_specs=[row, row, row, pl.BlockSpec((tm, D_MODEL), lambda i: (i, 4)), _full((1, LANES))],
        out_specs=[row, row, _full((1, LANES))],
        out_shape=[jax.ShapeDtypeStruct((t, D_MODEL), F32), jax.ShapeDtypeStruct((t, D_MODEL), BF16),
                   jax.ShapeDtypeStruct((1, LANES), F32)],
        compiler_params=_cp("arbitrary"), name=name)(dy, o_f, o_b, z, norm_g)


def _hgrn_combine(dq_f, dq_b, dzf_f, dzf_b, dv_f, dv_b, dgate, *, name):
    t = dq_f.shape[0]
    tm = _rows(t)

    def body(qf, qb, ff, fb, vf, vb, gt, o_ref):
        o_ref[:, 0 * D_MODEL:1 * D_MODEL] = (qf[...] + qb[...]).astype(BF16)
        o_ref[:, 1 * D_MODEL:2 * D_MODEL] = ff[...]
        o_ref[:, 2 * D_MODEL:3 * D_MODEL] = fb[...]
        o_ref[:, 3 * D_MODEL:4 * D_MODEL] = (vf[...] + vb[...]).astype(BF16)
        o_ref[:, 4 * D_MODEL:5 * D_MODEL] = gt[...]

    row = pl.BlockSpec((tm, D_MODEL), lambda i: (i, 0))
    return pl.pallas_call(
        body, grid=(t // tm,), in_specs=[row] * 7,
        out_specs=pl.BlockSpec((tm, 5 * D_MODEL), lambda i: (i, 0)),
        out_shape=jax.ShapeDtypeStruct((t, 5 * D_MODEL), BF16),
        compiler_params=_cp("parallel"), name=name)(dq_f, dq_b, dzf_f, dzf_b, dv_f, dv_b, dgate)


def _lower_bounds(logits2d, *, name):
    def body(l_ref, lb_ref):
        l = l_ref[...]
        e = jnp.exp(l - jnp.max(l, axis=0, keepdims=True))
        sm = e / jnp.sum(e, axis=0, keepdims=True)
        s1, s2, s3 = sm[1:2], sm[2:3], sm[3:4]
        lb_ref[...] = jnp.concatenate([jnp.zeros_like(s1), s1, s1 + s2, s1 + s2 + s3], axis=0)

    return pl.pallas_call(body, out_shape=jax.ShapeDtypeStruct(logits2d.shape, F32), name=name)(logits2d)


def _lower_bounds_bwd(logits2d, dlb, *, name):
    def body(l_ref, d_ref, o_ref):
        l = l_ref[...]
        e = jnp.exp(l - jnp.max(l, axis=0, keepdims=True))
        sm = e / jnp.sum(e, axis=0, keepdims=True)
        d = d_ref[...]
        d1, d2, d3 = d[1:2], d[2:3], d[3:4]
        dsm = jnp.concatenate([jnp.zeros_like(d1), d1 + d2 + d3, d2 + d3, d3], axis=0)
        o_ref[...] = sm * (dsm - jnp.sum(sm * dsm, axis=0, keepdims=True))

    return pl.pallas_call(body, out_shape=jax.ShapeDtypeStruct(logits2d.shape, F32), name=name)(logits2d, dlb)


def _adamw(w, g, m, v, *, name):
    r, c = w.shape
    tr = r if r <= 512 else _pick_rows(r)

    def body(w_ref, g_ref, m_ref, v_ref, d_ref, nm_ref, nv_ref):
        gv = g_ref[...]
        nm = ADAM_B1 * m_ref[...] + (1.0 - ADAM_B1) * gv
        nv = ADAM_B2 * v_ref[...] + (1.0 - ADAM_B2) * (gv * gv)
        m_hat = nm / (1.0 - ADAM_B1 ** ADAM_STEP)
        v_hat = nv / (1.0 - ADAM_B2 ** ADAM_STEP)
        d_ref[...] = -ADAM_LR * (m_hat / (jnp.sqrt(v_hat) + ADAM_EPS) + ADAM_WD * w_ref[...])
        nm_ref[...] = nm
        nv_ref[...] = nv

    blk = pl.BlockSpec((tr, c), lambda i: (i, 0))
    shp = jax.ShapeDtypeStruct((r, c), F32)
    return pl.pallas_call(body, grid=(r // tr,), in_specs=[blk] * 4, out_specs=[blk] * 3, out_shape=[shp] * 3,
                          compiler_params=_cp("parallel"), name=name)(w, g, m, v)


def _pick_rows(r, cap=512):
    best = 8
    for d in range(8, cap + 1, 8):
        if r % d == 0:
            best = d
    assert r % best == 0, r
    return best


def _local_step(x, p, target, w, sp):
    t = x.shape[0]
    tables = _rope_tables(t)
    logits2d = sp["hgrn_lb_logits"].reshape(DEPTH, 2 * D_MODEL)
    lb_all = _lower_bounds(logits2d, name="lb_fwd").reshape(DEPTH, 2, 1, D_MODEL)
    row = lambda a, i: a[i][None]

    saved = []
    xf, xb = x, x.astype(BF16)
    for i in range(DEPTH):
        j = i // 2
        s = dict(xin_bf=xb)
        if i % 2 == 0:
            q, k, v = _qkv_rope(xb, w["att_w_qkv"][j], tables, name=f"qkv_rope_{i}")
            o, lse = _attn_fwd(q, k, v, sp["att_sink"][j], name=f"attn_fwd_{i}")
            s.update(q=q, k=k, v=v, o=o, lse=lse)
            mix_in, w_o = o, w["att_w_o"][j]
        else:
            z = _mm_nn(xb, w["hgrn_w_in"][j], out_dtype=F32, name=f"hgrn_in_{i}")
            o_f, s_f = _hgrn_fwd(z, lb_all[i, 0], False, name=f"hgrn_scan_f_{i}")
            o_b, s_b = _hgrn_fwd(z, lb_all[i, 1], True, name=f"hgrn_scan_b_{i}")
            og = _hgrn_post_fwd(o_f, o_b, z, row(sp["hgrn_norm_g"], j), name=f"hgrn_post_{i}")
            s.update(z=z, o_f=o_f, o_b=o_b, s_f=s_f, s_b=s_b, og=og)
            mix_in, w_o = og, w["hgrn_w_o"][j]
        x1, x1b, xh1, rs1 = _mm_res_ln(mix_in, w_o, xf, row(sp["ln_mix_g"], i), row(sp["ln_mix_b"], i), name=f"mix_out_ln_{i}")
        g, u, h = _ffn_in(x1b, w["ffn_w_in"][i], name=f"ffn_in_{i}")
        x2, x2b, xh2, rs2 = _mm_res_ln(h, w["ffn_w_out"][i], x1, row(sp["ln_ffn_g"], i), row(sp["ln_ffn_b"], i), name=f"ffn_out_ln_{i}")
        xf, xb, gate, pp = _ple_fwd(x2, x2b, p, i, w["ple_w_gate"][i], w["ple_w_proj"][i], name=f"ple_fwd_{i}")
        s.update(x1b=x1b, xh1=xh1, rs1=rs1, g=g, u=u, h=h, x2b=x2b, xh2=xh2, rs2=rs2, gate=gate, pp=pp)
        saved.append(s)

    loss, dx = _loss_head(xf, target, name="loss_head")

    gw = {n: [None] * w[n].shape[0] for n in w}
    gs = {n: [None] * DEPTH for n in ("ln_mix_g", "ln_mix_b", "ln_ffn_g", "ln_ffn_b")}
    gs.update(att_sink=[None] * 2, hgrn_norm_g=[None] * 2)
    dlb = [jnp.zeros((2, D_MODEL), F32)] * DEPTH
    for i in reversed(range(DEPTH)):
        j = i // 2
        s = saved[i]
        du2, du2b, dpre, dpp, gs["ln_ffn_g"][i], gs["ln_ffn_b"][i] = _ple_bwd(
            dx, s["gate"], s["pp"], w["ple_w_gate"][i], s["xh2"], s["rs2"], row(sp["ln_ffn_g"], i), name=f"ple_bwd_{i}")
        gw["ple_w_gate"][i] = _mm_tn(s["x2b"], dpre, name=f"dw_ple_gate_{i}")
        gw["ple_w_proj"][i] = _mm_tn_p(p, i, dpp, name=f"dw_ple_proj_{i}")
        dgg, dgu = _ffn_out_bwd(du2b, w["ffn_w_out"][i], s["g"], s["u"], name=f"ffn_out_bwd_{i}")
        gw["ffn_w_out"][i] = _mm_tn(s["h"], du2b, name=f"dw_ffn_out_{i}")
        du1, du1b, gs["ln_mix_g"][i], gs["ln_mix_b"][i] = _mm_nt(
            [dgg, dgu], w["ffn_w_in"][i], tk=_pick(D_FF, 1408), resid=du2,
            ln=(s["xh1"], s["rs1"], row(sp["ln_mix_g"], i)), name=f"ffn_in_bwd_{i}")
        gw["ffn_w_in"][i] = jnp.concatenate(
            [_mm_tn(s["x1b"], dgg, name=f"dw_ffn_gate_{i}"), _mm_tn(s["x1b"], dgu, name=f"dw_ffn_up_{i}")], axis=1)
        if i % 2 == 0:
            gw["att_w_o"][j] = _mm_tn(s["o"], du1b, name=f"dw_att_o_{i}")
            do = _mm_nt([du1b], w["att_w_o"][j], tk=Q_DIM, out_dtype=BF16, name=f"att_o_bwd_{i}")
            dq, dkw, dvw, gs["att_sink"][j] = _attn_bwd(
                s["q"], s["k"], s["v"], s["o"], do, s["lse"], sp["att_sink"][j], name=f"attn_bwd_{i}")
            dqkv = _attn_post_bwd(dq, dkw, dvw, tables, name=f"attn_post_bwd_{i}")
            gw["att_w_qkv"][j] = _mm_tn(s["xin_bf"], dqkv, name=f"dw_att_qkv_{i}")
            dx = _mm_nt([dqkv], w["att_w_qkv"][j], tk=Q_DIM + 2 * KV_DIM, resid=du1, name=f"qkv_bwd_{i}")
        else:
            gw["hgrn_w_o"][j] = _mm_tn(s["og"], du1b, name=f"dw_hgrn_o_{i}")
            dy = _mm_nt([du1b], w["hgrn_w_o"][j], tk=D_MODEL, out_dtype=BF16, name=f"hgrn_o_bwd_{i}")
            d_o, dgate, gs["hgrn_norm_g"][j] = _hgrn_post_bwd(
                dy, s["o_f"], s["o_b"], s["z"], row(sp["hgrn_norm_g"], j), name=f"hgrn_post_bwd_{i}")
            dq_f, dzf_f, dv_f, dlb_f = _hgrn_bwd(s["z"], lb_all[i, 0], d_o, s["s_f"], False, name=f"hgrn_scan_f_bwd_{i}")
            dq_b, dzf_b, dv_b, dlb_b = _hgrn_bwd(s["z"], lb_all[i, 1], d_o, s["s_b"], True, name=f"hgrn_scan_b_bwd_{i}")
            dlb[i] = jnp.stack([dlb_f.reshape(D_MODEL), dlb_b.reshape(D_MODEL)])
            dz = _hgrn_combine(dq_f, dq_b, dzf_f, dzf_b, dv_f, dv_b, dgate, name=f"hgrn_combine_{i}")
            gw["hgrn_w_in"][j] = _mm_tn(s["xin_bf"], dz, name=f"dw_hgrn_in_{i}")
            dx = _mm_nt([dz], w["hgrn_w_in"][j], tk=_pick(5 * D_MODEL, 1408), resid=du1, name=f"hgrn_in_bwd_{i}")

    gw = {n: jnp.stack(v) for n, v in gw.items()}
    gsmall = {n: jnp.concatenate(v, axis=0) for n, v in gs.items()}
    gsmall["hgrn_lb_logits"] = _lower_bounds_bwd(
        logits2d, jnp.stack(dlb).reshape(DEPTH, 2 * D_MODEL), name="lb_bwd").reshape(DEPTH, 2, D_MODEL)
    return loss, dx, gw, gsmall


ANY = pl.BlockSpec(memory_space=pl.ANY)


def _place():
    x, y, c = lax.axis_index("x"), lax.axis_index("y"), lax.axis_index("c")
    chips = [(1 - x, y), (x, 1 - y), (1 - x, 1 - y)]
    return x, y, c, chips


def _remote(src, dst, send_sem, recv_sem, to):
    return pltpu.make_async_remote_copy(src_ref=src, dst_ref=dst, send_sem=send_sem, recv_sem=recv_sem,
                                        device_id=to, device_id_type=MESH)


def _allgather_weights(packed, *, name):
    r = packed.shape[0]
    hr = r // 2

    def body(src_ref, out_ref, send_sems, recv_sems, local_sem):
        x, y, c, chips = _place()
        sibling = (x, y, 1 - c)

        def half(cx, cy, hc):
            return out_ref.at[2 * cx + cy, pl.ds(hc * hr, hr), :]

        mine = pltpu.make_async_copy(src_ref, out_ref.at[2 * x + y], local_sem)
        mine.start()
        my_half = src_ref.at[pl.ds(c * hr, hr), :]
        first = [_remote(my_half, half(x, y, c), send_sems.at[j], recv_sems.at[j], (*chip, c)) for j, chip in enumerate(chips)]
        for cp in first:
            cp.start()
        passed = [_remote(half(*chip, c), half(*chip, c), send_sems.at[3 + j], recv_sems.at[3 + j], sibling)
                  for j, chip in enumerate(chips)]
        for j, chip in enumerate(chips):
            _remote(my_half, half(*chip, c), send_sems.at[j], recv_sems.at[j], (*chip, c)).wait_recv()
            passed[j].start()
        for j, chip in enumerate(chips):
            _remote(my_half, half(*chip, 1 - c), send_sems.at[3 + j], recv_sems.at[3 + j], sibling).wait_recv()
        for cp in first + passed:
            cp.wait_send()
        mine.wait()

    return pl.pallas_call(
        body, in_specs=[ANY], out_specs=ANY, out_shape=jax.ShapeDtypeStruct((N_CHIPS, r, packed.shape[1]), packed.dtype),
        scratch_shapes=[pltpu.SemaphoreType.DMA((6,)), pltpu.SemaphoreType.DMA((6,)), pltpu.SemaphoreType.DMA],
        name=name)(packed)


def _allreduce_small(block, *, name):
    m, n = block.shape

    def body(x_ref, sum_ref, all_ref, send_sems, recv_sems):
        x, y, c, chips = _place()
        me, sibling = (x, y, c), (x, y, 1 - c)

        def rows(px, py, pc):
            return all_ref.at[pl.ds((4 * px + 2 * py + pc) * m, m), :]

        all_ref[pl.ds((4 * x + 2 * y + c) * m, m), :] = x_ref[...]
        first = [_remote(x_ref, rows(*me), send_sems.at[0], recv_sems.at[0], sibling)]
        first += [_remote(x_ref, rows(*me), send_sems.at[1 + j], recv_sems.at[1 + j], (*chip, c)) for j, chip in enumerate(chips)]
        for cp in first:
            cp.start()
        passed = [_remote(rows(*chip, c), rows(*chip, c), send_sems.at[4 + j], recv_sems.at[4 + j], sibling)
                  for j, chip in enumerate(chips)]
        for j, chip in enumerate(chips):
            _remote(x_ref, rows(*chip, c), send_sems.at[1 + j], recv_sems.at[1 + j], me).wait_recv()
            passed[j].start()
        _remote(x_ref, rows(*sibling), send_sems.at[0], recv_sems.at[0], me).wait_recv()
        for j, chip in enumerate(chips):
            _remote(x_ref, rows(*chip, 1 - c), send_sems.at[4 + j], recv_sems.at[4 + j], me).wait_recv()
        for cp in first + passed:
            cp.wait_send()
        acc = all_ref[pl.ds(0, m), :]
        for d in range(1, 8):
            acc = acc + all_ref[pl.ds(d * m, m), :]
        sum_ref[...] = acc

    vmem = pl.BlockSpec(memory_space=pltpu.VMEM)
    return pl.pallas_call(
        body, in_specs=[vmem], out_specs=vmem, out_shape=jax.ShapeDtypeStruct((m, n), F32),
        scratch_shapes=[pltpu.VMEM((8 * m, n), F32), pltpu.SemaphoreType.DMA((7,)), pltpu.SemaphoreType.DMA((7,))],
        name=name)(block)


def _pair_exchange(g, *, name):
    n, r, w = g.shape
    hr = r // 2

    def body(g_ref, out_ref, send_sem, recv_sem):
        x, y, c, _ = _place()
        cp = _remote(g_ref.at[:, pl.ds((1 - c) * hr, hr), :], out_ref, send_sem, recv_sem, (x, y, 1 - c))
        cp.start()
        cp.wait()

    return pl.pallas_call(
        body, in_specs=[ANY], out_specs=ANY, out_shape=jax.ShapeDtypeStruct((n, hr, w), g.dtype),
        scratch_shapes=[pltpu.SemaphoreType.DMA, pltpu.SemaphoreType.DMA], name=name)(g)


def _chip_scatter(part, *, name):
    n, h, w = part.shape

    def body(p_ref, out_ref, send_sems, recv_sems, local_sem):
        x, y, c, chips = _place()
        me = 2 * x + y
        mine = pltpu.make_async_copy(p_ref.at[me], out_ref.at[me], local_sem)
        mine.start()
        sends = [_remote(p_ref.at[2 * cx + cy], out_ref.at[me], send_sems.at[j], recv_sems.at[j], (cx, cy, c))
                 for j, (cx, cy) in enumerate(chips)]
        for cp in sends:
            cp.start()
        for j, (cx, cy) in enumerate(chips):
            _remote(p_ref.at[me], out_ref.at[2 * cx + cy], send_sems.at[j], recv_sems.at[j], (cx, cy, c)).wait_recv()
        for cp in sends:
            cp.wait_send()
        mine.wait()

    return pl.pallas_call(
        body, in_specs=[ANY], out_specs=ANY, out_shape=jax.ShapeDtypeStruct((n, h, w), part.dtype),
        scratch_shapes=[pltpu.SemaphoreType.DMA((3,)), pltpu.SemaphoreType.DMA((3,)), pltpu.SemaphoreType.DMA],
        name=name)(part)


def _pair_share(half, *, name):
    h, w = half.shape

    def body(h_ref, out_ref, send_sem, recv_sem, local_sem):
        x, y, c, _ = _place()
        dst = out_ref.at[pl.ds(c * h, h), :]
        mine = pltpu.make_async_copy(h_ref, dst, local_sem)
        mine.start()
        cp = _remote(h_ref, dst, send_sem, recv_sem, (x, y, 1 - c))
        cp.start()
        cp.wait_send()
        _remote(h_ref, out_ref.at[pl.ds((1 - c) * h, h), :], send_sem, recv_sem, (x, y, 1 - c)).wait_recv()
        mine.wait()

    return pl.pallas_call(
        body, in_specs=[ANY], out_specs=ANY, out_shape=jax.ShapeDtypeStruct((2 * h, w), half.dtype),
        scratch_shapes=[pltpu.SemaphoreType.DMA, pltpu.SemaphoreType.DMA, pltpu.SemaphoreType.DMA], name=name)(half)


def _add_own_half(g, recv, c, *, name):
    n, r, w = g.shape
    hr = r // 2
    tr = _pick_rows(hr, 1024)
    nr = hr // tr

    def body(c_ref, g_ref, r_ref, o_ref):
        o_ref[...] = (g_ref[...] + r_ref[...]).astype(BF16)

    return pl.pallas_call(
        body,
        grid_spec=pltpu.PrefetchScalarGridSpec(
            num_scalar_prefetch=1, grid=(n, nr),
            in_specs=[pl.BlockSpec((None, tr, w), lambda s, i, c_ref: (s, c_ref[0] * nr + i, 0)),
                      pl.BlockSpec((None, tr, w), lambda s, i, c_ref: (s, i, 0))],
            out_specs=pl.BlockSpec((None, tr, w), lambda s, i, c_ref: (s, i, 0))),
        out_shape=jax.ShapeDtypeStruct((n, hr, w), BF16),
        compiler_params=_cp("parallel", "parallel"), name=name)(c, g, recv)


def _sum_chips(q, *, name):
    n, h, w = q.shape
    tr = _pick_rows(h, 1024)

    def body(a, b, c, d, o_ref):
        o_ref[...] = ((a[...].astype(F32) + b[...].astype(F32)) + c[...].astype(F32)) + d[...].astype(F32)

    specs = [pl.BlockSpec((None, tr, w), functools.partial(lambda i, s: (s, i, 0), s=s)) for s in range(n)]
    return pl.pallas_call(
        body, grid=(h // tr,), in_specs=specs, out_specs=pl.BlockSpec((tr, w), lambda i: (i, 0)),
        out_shape=jax.ShapeDtypeStruct((h, w), F32), compiler_params=_cp("parallel"), name=name)(q, q, q, q)


PACK_W = 1024
BIG = (("att_w_qkv", 2), ("att_w_o", 1), ("hgrn_w_in", 2), ("hgrn_w_o", 1),
       ("ffn_w_in", 2), ("ffn_w_out", 1), ("ple_w_gate", 1), ("ple_w_proj", 2))
LB_ROWS = 32


def _pack_shards(shards):
    return jnp.concatenate([shards[n].reshape(-1, PACK_W) for n, _ in BIG], axis=0)


def _unpack_shards(buf, shapes):
    out, r0 = {}, 0
    for n, _ in BIG:
        nr = shapes[n][0] * shapes[n][1] * shapes[n][2] // PACK_W
        out[n] = buf[r0:r0 + nr].reshape(shapes[n])
        r0 += nr
    return out


def _gathered_to_full(buf, shapes):
    out, r0 = {}, 0
    for n, axis in BIG:
        l, r, c = shapes[n]
        nr = l * r * c // PACK_W
        sh = buf[:, r0:r0 + nr].reshape(N_CHIPS, l, r, c)
        out[n] = (sh.transpose(1, 0, 2, 3).reshape(l, N_CHIPS * r, c) if axis == 1
                  else sh.transpose(1, 2, 0, 3).reshape(l, r, N_CHIPS * c))
        r0 += nr
    return out, r0


def _full_to_slabs(full, shapes):
    parts = []
    for n, axis in BIG:
        l, r, c = shapes[n]
        g = full[n]
        g = (g.reshape(l, N_CHIPS, r, c).transpose(1, 0, 2, 3) if axis == 1
             else g.reshape(l, r, N_CHIPS, c).transpose(2, 0, 1, 3))
        parts.append(g.reshape(N_CHIPS, -1, PACK_W))
    return jnp.concatenate(parts, axis=1)


SMALL = ("ln_mix_g", "ln_mix_b", "ln_ffn_g", "ln_ffn_b")
SMALL_ROWS = 32


def _pack_small(vals, lb):
    rows = [vals[n] for n in SMALL] + [lb.reshape(-1, PACK_W)]
    for n in ("att_sink", "hgrn_norm_g"):
        flat = vals[n].reshape(1, -1)
        rows.append(jnp.pad(flat, ((0, 0), (0, PACK_W - flat.shape[1]))))
    buf = jnp.concatenate(rows, axis=0)
    return jnp.pad(buf, ((0, SMALL_ROWS - buf.shape[0]), (0, 0)))


def _unpack_small(buf, lb_shape):
    out, r0 = {}, 0
    for n in SMALL:
        out[n] = buf[r0:r0 + DEPTH]
        r0 += DEPTH
    nlb = lb_shape[0] * lb_shape[1] * lb_shape[2] // PACK_W
    out["hgrn_lb_logits"] = buf[r0:r0 + nlb].reshape(lb_shape)
    r0 += nlb
    out["att_sink"] = buf[r0, :2 * N_Q_HEADS].reshape(2, N_Q_HEADS)
    out["hgrn_norm_g"] = buf[r0 + 1, :2 * LANES].reshape(2, LANES)
    return out


WEIGHTS = ("att_w_qkv", "att_sink", "att_w_o", "hgrn_w_in", "hgrn_lb_logits", "hgrn_norm_g", "hgrn_w_o", "ln_mix_g",
           "ln_mix_b", "ffn_w_in", "ffn_w_out", "ln_ffn_g", "ln_ffn_b", "ple_w_gate", "ple_w_proj")


def kernel(x, p, att_w_qkv, att_sink, att_w_o, hgrn_w_in, hgrn_lb_logits, hgrn_norm_g, hgrn_w_o, ln_mix_g, ln_mix_b, ffn_w_in, ffn_w_out, ln_ffn_g, ln_ffn_b, ple_w_gate, ple_w_proj, loss_target, m_att_w_qkv, m_att_sink, m_att_w_o, m_hgrn_w_in, m_hgrn_lb_logits, m_hgrn_norm_g, m_hgrn_w_o, m_ln_mix_g, m_ln_mix_b, m_ffn_w_in, m_ffn_w_out, m_ln_ffn_g, m_ln_ffn_b, m_ple_w_gate, m_ple_w_proj, v_att_w_qkv, v_att_sink, v_att_w_o, v_hgrn_w_in, v_hgrn_lb_logits, v_hgrn_norm_g, v_hgrn_w_o, v_ln_mix_g, v_ln_mix_b, v_ffn_w_in, v_ffn_w_out, v_ln_ffn_g, v_ln_ffn_b, v_ple_w_gate, v_ple_w_proj):
    wts = dict(att_w_qkv=att_w_qkv, att_sink=att_sink, att_w_o=att_w_o, hgrn_w_in=hgrn_w_in, hgrn_lb_logits=hgrn_lb_logits,
               hgrn_norm_g=hgrn_norm_g, hgrn_w_o=hgrn_w_o, ln_mix_g=ln_mix_g, ln_mix_b=ln_mix_b, ffn_w_in=ffn_w_in,
               ffn_w_out=ffn_w_out, ln_ffn_g=ln_ffn_g, ln_ffn_b=ln_ffn_b, ple_w_gate=ple_w_gate, ple_w_proj=ple_w_proj)
    mom = dict(att_w_qkv=m_att_w_qkv, att_sink=m_att_sink, att_w_o=m_att_w_o, hgrn_w_in=m_hgrn_w_in, hgrn_lb_logits=m_hgrn_lb_logits,
               hgrn_norm_g=m_hgrn_norm_g, hgrn_w_o=m_hgrn_w_o, ln_mix_g=m_ln_mix_g, ln_mix_b=m_ln_mix_b, ffn_w_in=m_ffn_w_in,
               ffn_w_out=m_ffn_w_out, ln_ffn_g=m_ln_ffn_g, ln_ffn_b=m_ln_ffn_b, ple_w_gate=m_ple_w_gate, ple_w_proj=m_ple_w_proj)
    var = dict(att_w_qkv=v_att_w_qkv, att_sink=v_att_sink, att_w_o=v_att_w_o, hgrn_w_in=v_hgrn_w_in, hgrn_lb_logits=v_hgrn_lb_logits,
               hgrn_norm_g=v_hgrn_norm_g, hgrn_w_o=v_hgrn_w_o, ln_mix_g=v_ln_mix_g, ln_mix_b=v_ln_mix_b, ffn_w_in=v_ffn_w_in,
               ffn_w_out=v_ffn_w_out, ln_ffn_g=v_ln_ffn_g, ln_ffn_b=v_ln_ffn_b, ple_w_gate=v_ple_w_gate, ple_w_proj=v_ple_w_proj)
    shapes = {n: wts[n].shape for n, _ in BIG}
    chip = 2 * lax.axis_index("x") + lax.axis_index("y")
    core = lax.axis_index("c")

    lb_bits = lax.bitcast_convert_type(hgrn_lb_logits.reshape(-1), BF16).reshape(-1, PACK_W)
    packed = jnp.concatenate([_pack_shards({n: wts[n].astype(BF16) for n, _ in BIG}), lb_bits,
                              jnp.zeros((LB_ROWS - lb_bits.shape[0], PACK_W), BF16)], axis=0)
    gathered = _allgather_weights(packed, name="allgather_weights")
    w_full, r_big = _gathered_to_full(gathered, shapes)
    lb_sh = hgrn_lb_logits.shape
    lb_rows = gathered[:, r_big:r_big + lb_bits.shape[0]].reshape(N_CHIPS, -1, 2)
    lb_full = lax.bitcast_convert_type(lb_rows, F32).reshape(N_CHIPS, lb_sh[0], lb_sh[1], lb_sh[2])
    lb_full = lb_full.transpose(1, 2, 0, 3).reshape(lb_sh[0], lb_sh[1], N_CHIPS * lb_sh[2])
    sp = dict(att_sink=att_sink, hgrn_lb_logits=lb_full, hgrn_norm_g=hgrn_norm_g, ln_mix_g=ln_mix_g, ln_mix_b=ln_mix_b,
              ln_ffn_g=ln_ffn_g, ln_ffn_b=ln_ffn_b)

    loss, grad_x, gw, gs = _local_step(x[0], p, loss_target[0], w_full, sp)
    loss = lax.psum(loss[0, 0], ("x", "y", "c"))

    slabs = _full_to_slabs(gw, shapes)
    from_sibling = _pair_exchange(slabs, name="grad_pair_exchange")
    chip_part = _add_own_half(slabs, from_sibling, core.reshape(1).astype(jnp.int32), name="grad_pair_add")
    from_chips = _chip_scatter(chip_part, name="grad_chip_scatter")
    half = _sum_chips(from_chips, name="grad_chip_sum")
    g_big = _unpack_shards(_pair_share(half, name="grad_pair_share"), shapes)

    g_small_full = _unpack_small(_allreduce_small(_pack_small(gs, gs["hgrn_lb_logits"]), name="allreduce_small"),
                                 (lb_sh[0], lb_sh[1], N_CHIPS * lb_sh[2]))
    g_lb = lax.dynamic_slice_in_dim(g_small_full["hgrn_lb_logits"], chip * lb_sh[2], lb_sh[2], axis=2)
    grads = dict(g_big)
    grads.update({n: g_small_full[n] for n in SMALL + ("att_sink", "hgrn_norm_g")})
    grads["hgrn_lb_logits"] = g_lb

    delta, new_m, new_v = {}, {}, {}
    for n, _ in BIG:
        two_d = lambda a: a.reshape(-1, a.shape[-1])
        d, nm, nv = _adamw(two_d(wts[n]), two_d(grads[n]), two_d(mom[n]), two_d(var[n]), name=f"adamw_{n}")
        delta[n], new_m[n], new_v[n] = d.reshape(shapes[n]), nm.reshape(shapes[n]), nv.reshape(shapes[n])
    packs = [_pack_small(src, src["hgrn_lb_logits"]) for src in (wts, grads, mom, var)]
    outs = _adamw(*packs, name="adamw_small")
    for dst, buf in zip((delta, new_m, new_v), outs):
        dst.update(_unpack_small(buf, lb_sh))

    return (loss, grad_x[None], *[grads[n] for n in WEIGHTS], *[delta[n] for n in WEIGHTS],
            *[new_m[n] for n in WEIGHTS], *[new_v[n] for n in WEIGHTS])


def _gather_guest(packed):
    r, w = packed.shape
    hr = r // 2

    def copies(src_ref, out_ref, send_sems, recv_sems, local_sem):
        x, y, c, chips = _place()
        sibling = (x, y, 1 - c)

        def half(cx, cy, hc):
            return out_ref.at[2 * cx + cy, pl.ds(hc * hr, hr), :]

        my_half = src_ref.at[pl.ds(c * hr, hr), :]
        mine = pltpu.make_async_copy(src_ref, out_ref.at[2 * x + y], local_sem)
        first = [_remote(my_half, half(x, y, c), send_sems.at[j], recv_sems.at[j], (*chip, c)) for j, chip in enumerate(chips)]
        passed = [_remote(half(*chip, c), half(*chip, c), send_sems.at[3 + j], recv_sems.at[3 + j], sibling)
                  for j, chip in enumerate(chips)]
        from_chips = [_remote(my_half, half(*chip, c), send_sems.at[j], recv_sems.at[j], (*chip, c)) for j, chip in enumerate(chips)]
        from_sibling = [_remote(my_half, half(*chip, 1 - c), send_sems.at[3 + j], recv_sems.at[3 + j], sibling)
                        for j, chip in enumerate(chips)]
        return mine, first, passed, from_chips, from_sibling

    def start(gi, go, gs):
        mine, first, _, _, _ = copies(gi[0], go[0], *gs)
        mine.start()
        for cp in first:
            cp.start()

    def finish(gi, go, gs):
        mine, first, passed, from_chips, from_sibling = copies(gi[0], go[0], *gs)
        for arrival, onward in zip(from_chips, passed):
            arrival.wait_recv()
            onward.start()
        for arrival in from_sibling:
            arrival.wait_recv()
        for cp in first + passed:
            cp.wait_send()
        mine.wait()

    return _Guest((packed,), (jax.ShapeDtypeStruct((N_CHIPS, r, w), packed.dtype),),
                  (pltpu.SemaphoreType.DMA((6,)), pltpu.SemaphoreType.DMA((6,)), pltpu.SemaphoreType.DMA), start, finish)


def _pair_exchange_guest(g):
    n, r, w = g.shape
    hr = r // 2

    def copy(g_ref, out_ref, send_sem, recv_sem):
        x, y, c, _ = _place()
        return _remote(g_ref.at[:, pl.ds((1 - c) * hr, hr), :], out_ref, send_sem, recv_sem, (x, y, 1 - c))

    return _Guest((g,), (jax.ShapeDtypeStruct((n, hr, w), g.dtype),), (pltpu.SemaphoreType.DMA, pltpu.SemaphoreType.DMA),
                  lambda gi, go, gs: copy(gi[0], go[0], *gs).start(),
                  lambda gi, go, gs: copy(gi[0], go[0], *gs).wait())


def _chip_scatter_guest(part):
    n, h, w = part.shape

    def copies(p_ref, out_ref, send_sems, recv_sems, local_sem):
        x, y, c, chips = _place()
        me = 2 * x + y
        mine = pltpu.make_async_copy(p_ref.at[me], out_ref.at[me], local_sem)
        sends = [_remote(p_ref.at[2 * cx + cy], out_ref.at[me], send_sems.at[j], recv_sems.at[j], (cx, cy, c))
                 for j, (cx, cy) in enumerate(chips)]
        arrivals = [_remote(p_ref.at[me], out_ref.at[2 * cx + cy], send_sems.at[j], recv_sems.at[j], (cx, cy, c))
                    for j, (cx, cy) in enumerate(chips)]
        return mine, sends, arrivals

    def start(gi, go, gs):
        mine, sends, _ = copies(gi[0], go[0], *gs)
        mine.start()
        for cp in sends:
            cp.start()

    def finish(gi, go, gs):
        mine, sends, arrivals = copies(gi[0], go[0], *gs)
        for cp in arrivals:
            cp.wait_recv()
        for cp in sends:
            cp.wait_send()
        mine.wait()

    return _Guest((part,), (jax.ShapeDtypeStruct((n, h, w), part.dtype),),
                  (pltpu.SemaphoreType.DMA((3,)), pltpu.SemaphoreType.DMA((3,)), pltpu.SemaphoreType.DMA), start, finish)


def _pair_share_guest(half):
    h, w = half.shape

    def copies(h_ref, out_ref, send_sem, recv_sem, local_sem):
        x, y, c, _ = _place()
        dst = out_ref.at[pl.ds(c * h, h), :]
        mine = pltpu.make_async_copy(h_ref, dst, local_sem)
        send = _remote(h_ref, dst, send_sem, recv_sem, (x, y, 1 - c))
        arrival = _remote(h_ref, out_ref.at[pl.ds((1 - c) * h, h), :], send_sem, recv_sem, (x, y, 1 - c))
        return mine, send, arrival

    def start(gi, go, gs):
        mine, send, _ = copies(gi[0], go[0], *gs)
        mine.start()
        send.start()

    def finish(gi, go, gs):
        mine, send, arrival = copies(gi[0], go[0], *gs)
        send.wait_send()
        arrival.wait_recv()
        mine.wait()

    return _Guest((half,), (jax.ShapeDtypeStruct((2 * h, w), half.dtype),),
                  (pltpu.SemaphoreType.DMA, pltpu.SemaphoreType.DMA, pltpu.SemaphoreType.DMA), start, finish)


AXIS = dict(BIG)


def _layer_parts(i):
    j = i // 2
    mixer = (("att_w_qkv", j), ("att_w_o", j)) if i % 2 == 0 else (("hgrn_w_in", j), ("hgrn_w_o", j))
    return mixer + (("ffn_w_in", i), ("ffn_w_out", i), ("ple_w_gate", i), ("ple_w_proj", i))


def _pack_layer(shards, i):
    return jnp.concatenate([shards[n][l].reshape(-1, PACK_W) for n, l in _layer_parts(i)], axis=0)


def _unpack_layer(buf, i, shapes):
    out, r0 = {}, 0
    for n, _ in _layer_parts(i):
        r, c = shapes[n][1:]
        nr = r * c // PACK_W
        out[n] = buf[r0:r0 + nr].reshape(r, c)
        r0 += nr
    return out


def _gathered_layer(buf, i, shapes):
    out, r0 = {}, 0
    for n, _ in _layer_parts(i):
        r, c = shapes[n][1:]
        nr = r * c // PACK_W
        sh = buf[:, r0:r0 + nr].reshape(N_CHIPS, r, c)
        out[n] = sh.reshape(N_CHIPS * r, c) if AXIS[n] == 1 else sh.transpose(1, 0, 2).reshape(r, N_CHIPS * c)
        r0 += nr
    return out, r0


def _layer_slabs(full, i, shapes):
    parts = []
    for n, _ in _layer_parts(i):
        r, c = shapes[n][1:]
        g = full[n]
        g = g.reshape(N_CHIPS, -1, PACK_W) if AXIS[n] == 1 else g.reshape(r, N_CHIPS, c).transpose(1, 0, 2).reshape(N_CHIPS, -1, PACK_W)
        parts.append(g)
    return jnp.concatenate(parts, axis=1)


def kernel(x, p, att_w_qkv, att_sink, att_w_o, hgrn_w_in, hgrn_lb_logits, hgrn_norm_g, hgrn_w_o, ln_mix_g, ln_mix_b, ffn_w_in, ffn_w_out, ln_ffn_g, ln_ffn_b, ple_w_gate, ple_w_proj, loss_target, m_att_w_qkv, m_att_sink, m_att_w_o, m_hgrn_w_in, m_hgrn_lb_logits, m_hgrn_norm_g, m_hgrn_w_o, m_ln_mix_g, m_ln_mix_b, m_ffn_w_in, m_ffn_w_out, m_ln_ffn_g, m_ln_ffn_b, m_ple_w_gate, m_ple_w_proj, v_att_w_qkv, v_att_sink, v_att_w_o, v_hgrn_w_in, v_hgrn_lb_logits, v_hgrn_norm_g, v_hgrn_w_o, v_ln_mix_g, v_ln_mix_b, v_ffn_w_in, v_ffn_w_out, v_ln_ffn_g, v_ln_ffn_b, v_ple_w_gate, v_ple_w_proj):
    wts = dict(att_w_qkv=att_w_qkv, att_sink=att_sink, att_w_o=att_w_o, hgrn_w_in=hgrn_w_in, hgrn_lb_logits=hgrn_lb_logits,
               hgrn_norm_g=hgrn_norm_g, hgrn_w_o=hgrn_w_o, ln_mix_g=ln_mix_g, ln_mix_b=ln_mix_b, ffn_w_in=ffn_w_in,
               ffn_w_out=ffn_w_out, ln_ffn_g=ln_ffn_g, ln_ffn_b=ln_ffn_b, ple_w_gate=ple_w_gate, ple_w_proj=ple_w_proj)
    mom = dict(att_w_qkv=m_att_w_qkv, att_sink=m_att_sink, att_w_o=m_att_w_o, hgrn_w_in=m_hgrn_w_in, hgrn_lb_logits=m_hgrn_lb_logits,
               hgrn_norm_g=m_hgrn_norm_g, hgrn_w_o=m_hgrn_w_o, ln_mix_g=m_ln_mix_g, ln_mix_b=m_ln_mix_b, ffn_w_in=m_ffn_w_in,
               ffn_w_out=m_ffn_w_out, ln_ffn_g=m_ln_ffn_g, ln_ffn_b=m_ln_ffn_b, ple_w_gate=m_ple_w_gate, ple_w_proj=m_ple_w_proj)
    var = dict(att_w_qkv=v_att_w_qkv, att_sink=v_att_sink, att_w_o=v_att_w_o, hgrn_w_in=v_hgrn_w_in, hgrn_lb_logits=v_hgrn_lb_logits,
               hgrn_norm_g=v_hgrn_norm_g, hgrn_w_o=v_hgrn_w_o, ln_mix_g=v_ln_mix_g, ln_mix_b=v_ln_mix_b, ffn_w_in=v_ffn_w_in,
               ffn_w_out=v_ffn_w_out, ln_ffn_g=v_ln_ffn_g, ln_ffn_b=v_ln_ffn_b, ple_w_gate=v_ple_w_gate, ple_w_proj=v_ple_w_proj)
    shapes = {n: wts[n].shape for n, _ in BIG}
    chip = 2 * lax.axis_index("x") + lax.axis_index("y")
    core = lax.axis_index("c").reshape(1).astype(jnp.int32)
    xin, target = x[0], loss_target[0]
    t = xin.shape[0]
    row = lambda a, i: a[i][None]

    bf_shards = {n: wts[n].astype(BF16) for n, _ in BIG}
    lb_sh = hgrn_lb_logits.shape
    lb_bits = lax.bitcast_convert_type(hgrn_lb_logits.reshape(-1), BF16).reshape(-1, PACK_W)
    packed = [_pack_layer(bf_shards, i) for i in range(DEPTH)]
    packed[0] = jnp.concatenate([packed[0], lb_bits, jnp.zeros((LB_ROWS - lb_bits.shape[0], PACK_W), BF16)], axis=0)

    gathered = _run_guest(_gather_guest(packed[0]), name="gather_layer_0")[0]
    w, r_big = _gathered_layer(gathered, 0, shapes)
    lb_rows = gathered[:, r_big:r_big + lb_bits.shape[0]].reshape(N_CHIPS, -1, 2)
    lb_full = lax.bitcast_convert_type(lb_rows, F32).reshape(N_CHIPS, lb_sh[0], lb_sh[1], lb_sh[2])
    lb_full = lb_full.transpose(1, 2, 0, 3).reshape(lb_sh[0], lb_sh[1], N_CHIPS * lb_sh[2])
    logits2d = lb_full.reshape(DEPTH, 2 * D_MODEL)
    lb_all = _lower_bounds(logits2d, name="lb_fwd").reshape(DEPTH, 2, 1, D_MODEL)
    tables = _rope_tables(t)

    saved, weights = [], []
    xf, xb = xin, xin.astype(BF16)
    for i in range(DEPTH):
        j = i // 2
        nxt = _gather_guest(packed[i + 1]) if i + 1 < DEPTH else None
        s = dict(xin_bf=xb)
        if i % 2 == 0:
            q, k, v = _qkv_rope(xb, w["att_w_qkv"], tables, name=f"qkv_rope_{i}")
            (o, lse), got = _attn_fwd(q, k, v, att_sink[j], name=f"attn_fwd_{i}", guest=nxt)
            s.update(q=q, k=k, v=v, o=o, lse=lse)
            mix_in, w_o = o, w["att_w_o"]
        else:
            z = _mm_nn(xb, w["hgrn_w_in"], out_dtype=F32, name=f"hgrn_in_{i}")
            (o_f, s_f), got = _hgrn_fwd(z, lb_all[i, 0], False, name=f"hgrn_scan_f_{i}", guest=nxt)
            (o_b, s_b), _ = _hgrn_fwd(z, lb_all[i, 1], True, name=f"hgrn_scan_b_{i}")
            og = _hgrn_post_fwd(o_f, o_b, z, row(hgrn_norm_g, j), name=f"hgrn_post_{i}")
            s.update(z=z, o_f=o_f, o_b=o_b, s_f=s_f, s_b=s_b, og=og)
            mix_in, w_o = og, w["hgrn_w_o"]
        x1, x1b, xh1, rs1 = _mm_res_ln(mix_in, w_o, xf, row(ln_mix_g, i), row(ln_mix_b, i), name=f"mix_out_ln_{i}")
        g, u, h = _ffn_in(x1b, w["ffn_w_in"], name=f"ffn_in_{i}")
        x2, x2b, xh2, rs2 = _mm_res_ln(h, w["ffn_w_out"], x1, row(ln_ffn_g, i), row(ln_ffn_b, i), name=f"ffn_out_ln_{i}")
        xf, xb, gate, pp = _ple_fwd(x2, x2b, p, i, w["ple_w_gate"], w["ple_w_proj"], name=f"ple_fwd_{i}")
        s.update(x1b=x1b, xh1=xh1, rs1=rs1, g=g, u=u, h=h, x2b=x2b, xh2=xh2, rs2=rs2, gate=gate, pp=pp)
        saved.append(s)
        weights.append(w)
        if nxt is not None:
            w, _ = _gathered_layer(got[0], i + 1, shapes)

    loss, dx = _loss_head(xf, target, name="loss_head")
    loss = lax.psum(loss[0, 0], ("x", "y", "c"))

    gs = {n: [None] * DEPTH for n in SMALL}
    gs.update(att_sink=[None] * 2, hgrn_norm_g=[None] * 2)
    dlb = [jnp.zeros((2, D_MODEL), F32)] * DEPTH
    reduced = [None] * DEPTH
    slabs = None
    for i in reversed(range(DEPTH)):
        j = i // 2
        s, w = saved[i], weights[i]
        gw = {}
        res, got = _ple_bwd(dx, s["gate"], s["pp"], w["ple_w_gate"], s["xh2"], s["rs2"], row(ln_ffn_g, i), name=f"ple_bwd_{i}",
                            guest=None if slabs is None else _pair_exchange_guest(slabs))
        du2, du2b, dpre, dpp, gs["ln_ffn_g"][i], gs["ln_ffn_b"][i] = res
        part = None if slabs is None else _add_own_half(slabs, got[0], core, name=f"grad_pair_add_{i + 1}")
        gw["ple_w_gate"] = _mm_tn(s["x2b"], dpre, name=f"dw_ple_gate_{i}")
        gw["ple_w_proj"] = _mm_tn_p(p, i, dpp, name=f"dw_ple_proj_{i}")
        dgg, dgu = _ffn_out_bwd(du2b, w["ffn_w_out"], s["g"], s["u"], name=f"ffn_out_bwd_{i}")
        gw["ffn_w_out"] = _mm_tn(s["h"], du2b, name=f"dw_ffn_out_{i}")
        res = _mm_nt([dgg, dgu], w["ffn_w_in"], tk=_pick(D_FF, 1408), resid=du2, ln=(s["xh1"], s["rs1"], row(ln_mix_g, i)),
                     name=f"ffn_in_bwd_{i}", guest=None if part is None else _chip_scatter_guest(part))
        (du1, du1b, gs["ln_mix_g"][i], gs["ln_mix_b"][i]), got = res if part is not None else (res, None)
        half = None if part is None else _sum_chips(got[0], name=f"grad_chip_sum_{i + 1}")
        res = _mm_tn(s["x1b"], dgg, name=f"dw_ffn_gate_{i}", guest=None if half is None else _pair_share_guest(half))
        dw_gate, got = res if half is not None else (res, None)
        if half is not None:
            reduced[i + 1] = got[0]
        gw["ffn_w_in"] = jnp.concatenate([dw_gate, _mm_tn(s["x1b"], dgu, name=f"dw_ffn_up_{i}")], axis=1)
        if i % 2 == 0:
            gw["att_w_o"] = _mm_tn(s["o"], du1b, name=f"dw_att_o_{i}")
            do = _mm_nt([du1b], w["att_w_o"], tk=Q_DIM, out_dtype=BF16, name=f"att_o_bwd_{i}")
            dq, dkw, dvw, gs["att_sink"][j] = _attn_bwd(s["q"], s["k"], s["v"], s["o"], do, s["lse"], att_sink[j], name=f"attn_bwd_{i}")
            dqkv = _attn_post_bwd(dq, dkw, dvw, tables, name=f"attn_post_bwd_{i}")
            gw["att_w_qkv"] = _mm_tn(s["xin_bf"], dqkv, name=f"dw_att_qkv_{i}")
            dx = _mm_nt([dqkv], w["att_w_qkv"], tk=Q_DIM + 2 * KV_DIM, resid=du1, name=f"qkv_bwd_{i}")
        else:
            gw["hgrn_w_o"] = _mm_tn(s["og"], du1b, name=f"dw_hgrn_o_{i}")
            dy = _mm_nt([du1b], w["hgrn_w_o"], tk=D_MODEL, out_dtype=BF16, name=f"hgrn_o_bwd_{i}")
            d_o, dgate, gs["hgrn_norm_g"][j] = _hgrn_post_bwd(dy, s["o_f"], s["o_b"], s["z"], row(hgrn_norm_g, j), name=f"hgrn_post_bwd_{i}")
            dq_f, dzf_f, dv_f, dlb_f = _hgrn_bwd(s["z"], lb_all[i, 0], d_o, s["s_f"], False, name=f"hgrn_scan_f_bwd_{i}")
            dq_b, dzf_b, dv_b, dlb_b = _hgrn_bwd(s["z"], lb_all[i, 1], d_o, s["s_b"], True, name=f"hgrn_scan_b_bwd_{i}")
            dlb[i] = jnp.stack([dlb_f.reshape(D_MODEL), dlb_b.reshape(D_MODEL)])
            dz = _hgrn_combine(dq_f, dq_b, dzf_f, dzf_b, dv_f, dv_b, dgate, name=f"hgrn_combine_{i}")
            gw["hgrn_w_in"] = _mm_tn(s["xin_bf"], dz, name=f"dw_hgrn_in_{i}")
            dx = _mm_nt([dz], w["hgrn_w_in"], tk=_pick(5 * D_MODEL, 1408), resid=du1, name=f"hgrn_in_bwd_{i}")
        slabs = _layer_slabs(gw, i, shapes)

    from_sibling = _run_guest(_pair_exchange_guest(slabs), name="grad_pair_exchange_0")[0]
    part = _add_own_half(slabs, from_sibling, core, name="grad_pair_add_0")
    half = _sum_chips(_run_guest(_chip_scatter_guest(part), name="grad_chip_scatter_0")[0], name="grad_chip_sum_0")
    reduced[0] = _run_guest(_pair_share_guest(half), name="grad_pair_share_0")[0]

    g_layers = [_unpack_layer(reduced[i], i, shapes) for i in range(DEPTH)]
    grads = {}
    for n, _ in BIG:
        per_layer = [g_layers[i][n] for i in range(DEPTH) if n in g_layers[i]]
        grads[n] = jnp.stack(per_layer)

    gsmall = {n: jnp.concatenate(v, axis=0) for n, v in gs.items()}
    dlogits = _lower_bounds_bwd(logits2d, jnp.stack(dlb).reshape(DEPTH, 2 * D_MODEL), name="lb_bwd").reshape(DEPTH, 2, D_MODEL)
    g_small_full = _unpack_small(_allreduce_small(_pack_small(gsmall, dlogits), name="allreduce_small"),
                                 (lb_sh[0], lb_sh[1], N_CHIPS * lb_sh[2]))
    grads.update({n: g_small_full[n] for n in SMALL + ("att_sink", "hgrn_norm_g")})
    grads["hgrn_lb_logits"] = lax.dynamic_slice_in_dim(g_small_full["hgrn_lb_logits"], chip * lb_sh[2], lb_sh[2], axis=2)

    delta, new_m, new_v = {}, {}, {}
    for n, _ in BIG:
        two_d = lambda a: a.reshape(-1, a.shape[-1])
        d, nm, nv = _adamw(two_d(wts[n]), two_d(grads[n]), two_d(mom[n]), two_d(var[n]), name=f"adamw_{n}")
        delta[n], new_m[n], new_v[n] = d.reshape(shapes[n]), nm.reshape(shapes[n]), nv.reshape(shapes[n])
    packs = [_pack_small(src, src["hgrn_lb_logits"]) for src in (wts, grads, mom, var)]
    outs = _adamw(*packs, name="adamw_small")
    for dst, buf in zip((delta, new_m, new_v), outs):
        dst.update(_unpack_small(buf, lb_sh))

    return (loss, dx[None], *[grads[n] for n in WEIGHTS], *[delta[n] for n in WEIGHTS],
            *[new_m[n] for n in WEIGHTS], *[new_v[n] for n in WEIGHTS])
```

```python
import functools
from typing import Callable, NamedTuple

import jax
import jax.numpy as jnp
from jax import lax
from jax.experimental import pallas as pl
from jax.experimental.pallas import tpu as pltpu

F32, BF16 = jnp.float32, jnp.bfloat16

D_MODEL = 1024
DEPTH = 4
HEAD_DIM = 64
N_Q_HEADS = 16
N_KV_HEADS = 4
GROUP = 4
Q_DIM = 1024
KV_DIM = 256
WINDOW = 128
ROPE_DIM = 16
ROPE_THETA = 500000.0
HGRN_HEADS = 8
HGRN_KEY = 128
HGRN_CHUNK = 64
D_FF = 2816
PLE_DIM = 256
ALPHA = (2 * DEPTH) ** 0.25
LN_EPS = 1e-5
ADAM_LR, ADAM_B1, ADAM_B2, ADAM_EPS, ADAM_WD, ADAM_STEP = 0.001, 0.9, 0.999, 1e-08, 0.01, 10

LANES = 128
VMEM_LIMIT_BYTES = 56 * 2**20
FACTORED_DECAY_LIMIT = -80.0

NN = ((1,), (0,))
NT = ((1,), (1,))
TN = ((0,), (0,))

N_CHIPS = 4
MESH = pl.DeviceIdType.MESH


def _dot(a, b, dims):
    return lax.dot_general(a, b, (dims, ((), ())), preferred_element_type=F32)


def _cp(*sem):
    return pltpu.CompilerParams(dimension_semantics=sem, vmem_limit_bytes=VMEM_LIMIT_BYTES)


def _rows(t, cap=512):
    return min(cap, t)


def _pick(n, cap):
    best = None
    for d in range(LANES, min(n, cap) + 1, LANES):
        if n % d == 0:
            best = d
    assert best is not None, (n, cap)
    return best


def _sigmoid(x):
    return jax.nn.sigmoid(x)


def _ln_fwd(u, g, b):
    mu = jnp.mean(u, axis=-1, keepdims=True)
    xc = u - mu
    var = jnp.mean(xc * xc, axis=-1, keepdims=True)
    rstd = lax.rsqrt(var + LN_EPS)
    xhat = xc * rstd
    return xhat * g + b, xhat, rstd


def _ln_bwd(dy, xhat, rstd, g):
    dxh = dy * g
    m1 = jnp.mean(dxh, axis=-1, keepdims=True)
    m2 = jnp.mean(dxh * xhat, axis=-1, keepdims=True)
    du = rstd * (dxh - m1 - xhat * m2)
    return du, jnp.sum(dy * xhat, axis=0, keepdims=True), jnp.sum(dy, axis=0, keepdims=True)


def _full(shape):
    nd = len(shape)
    return pl.BlockSpec(shape, lambda *_: (0,) * nd)


ANY = pl.BlockSpec(memory_space=pl.ANY)


class _Guest(NamedTuple):
    args: tuple
    out_shape: tuple
    sems: tuple
    start: Callable
    finish: Callable


def _call(body, *, grid, in_specs, out_specs, out_shape, args, sem, name, scratch_shapes=(), guest=None):
    n_in, n_out, n_scr = len(args), len(out_shape), len(scratch_shapes)
    if guest is None:
        res = pl.pallas_call(body, grid=grid, in_specs=list(in_specs), out_specs=list(out_specs), out_shape=list(out_shape),
                             scratch_shapes=list(scratch_shapes), compiler_params=_cp(*sem), name=name)(*args)
        return list(res), []
    g_in, g_out = len(guest.args), len(guest.out_shape)

    def hosted(*refs):
        cuts = [n_in, g_in, n_out, g_out, n_scr]
        parts, pos = [], 0
        for n in cuts:
            parts.append(refs[pos:pos + n])
            pos += n
        h_in, gi, h_out, go, h_scr = parts
        gs = refs[pos:]
        ids = [pl.program_id(d) for d in range(len(grid))]
        first = functools.reduce(jnp.logical_and, [i == 0 for i in ids])
        last = functools.reduce(jnp.logical_and, [i == n - 1 for i, n in zip(ids, grid)])

        @pl.when(first)
        def _():
            guest.start(gi, go, gs)
        body(*h_in, *h_out, *h_scr)

        @pl.when(last)
        def _():
            guest.finish(gi, go, gs)

    res = pl.pallas_call(
        hosted, grid=grid, in_specs=list(in_specs) + [ANY] * g_in, out_specs=list(out_specs) + [ANY] * g_out,
        out_shape=list(out_shape) + list(guest.out_shape), scratch_shapes=list(scratch_shapes) + list(guest.sems),
        compiler_params=_cp(*(("arbitrary",) * len(grid))), name=name)(*args, *guest.args)
    return list(res[:n_out]), list(res[n_out:])


def _run_guest(guest, *, name):
    g_in = len(guest.args)

    def body(*refs):
        gi, go, gs = refs[:g_in], refs[g_in:g_in + len(guest.out_shape)], refs[g_in + len(guest.out_shape):]
        guest.start(gi, go, gs)
        guest.finish(gi, go, gs)

    return pl.pallas_call(body, in_specs=[ANY] * g_in, out_specs=[ANY] * len(guest.out_shape), out_shape=list(guest.out_shape),
                          scratch_shapes=list(guest.sems), name=name)(*guest.args)


def _mm_nn(a, w, *, out_dtype, name):
    t, k = a.shape
    n = w.shape[1]
    tm, tn = _rows(t), _pick(n, 1408)

    def body(a_ref, w_ref, o_ref):
        o_ref[...] = _dot(a_ref[...], w_ref[...], NN).astype(out_dtype)

    return pl.pallas_call(
        body, grid=(n // tn, t // tm),
        in_specs=[pl.BlockSpec((tm, k), lambda j, i: (i, 0)), pl.BlockSpec((k, tn), lambda j, i: (0, j))],
        out_specs=pl.BlockSpec((tm, tn), lambda j, i: (i, j)),
        out_shape=jax.ShapeDtypeStruct((t, n), out_dtype),
        compiler_params=_cp("parallel", "parallel"), name=name)(a, w)


def _mm_tn(a, b, *, name, guest=None):
    r, m = a.shape
    n = b.shape[1]
    tr, tm, tn = _rows(r), _pick(m, 1408), _pick(n, 1408)

    def body(a_ref, b_ref, o_ref):
        @pl.when(pl.program_id(2) == 0)
        def _():
            o_ref[...] = jnp.zeros_like(o_ref)
        o_ref[...] += _dot(a_ref[...].astype(BF16), b_ref[...].astype(BF16), TN)

    res, extra = _call(
        body, grid=(m // tm, n // tn, r // tr),
        in_specs=[pl.BlockSpec((tr, tm), lambda i, j, k: (k, i)), pl.BlockSpec((tr, tn), lambda i, j, k: (k, j))],
        out_specs=[pl.BlockSpec((tm, tn), lambda i, j, k: (i, j))],
        out_shape=[jax.ShapeDtypeStruct((m, n), F32)], args=(a, b),
        sem=("parallel", "parallel", "arbitrary"), name=name, guest=guest)
    return res[0] if guest is None else (res[0], extra)


def _mm_tn_p(p, layer, b, *, name):
    t = p.shape[2]
    n = b.shape[1]
    tr = _rows(t)

    def body(a_ref, b_ref, o_ref):
        @pl.when(pl.program_id(0) == 0)
        def _():
            o_ref[...] = jnp.zeros_like(o_ref)
        o_ref[...] += _dot(a_ref[...].astype(BF16), b_ref[...], TN)

    return pl.pallas_call(
        body, grid=(t // tr,),
        in_specs=[pl.BlockSpec((None, None, tr, PLE_DIM), lambda k: (layer, 0, k, 0)),
                  pl.BlockSpec((tr, n), lambda k: (k, 0))],
        out_specs=pl.BlockSpec((PLE_DIM, n), lambda k: (0, 0)),
        out_shape=jax.ShapeDtypeStruct((PLE_DIM, n), F32),
        compiler_params=_cp("arbitrary"), name=name)(p, b)


def _mm_res_ln(a, w, resid, g, b, *, name):
    t, k = a.shape
    tm = _rows(t)

    def body(a_ref, w_ref, r_ref, g_ref, b_ref, y_ref, ybf_ref, xh_ref, rs_ref):
        acc = _dot(a_ref[...], w_ref[...], NN)
        y, xh, rs = _ln_fwd(ALPHA * r_ref[...] + acc, g_ref[...], b_ref[...])
        y_ref[...] = y
        ybf_ref[...] = y.astype(BF16)
        xh_ref[...] = xh
        rs_ref[...] = rs

    row = pl.BlockSpec((tm, D_MODEL), lambda i: (i, 0))
    return pl.pallas_call(
        body, grid=(t // tm,),
        in_specs=[pl.BlockSpec((tm, k), lambda i: (i, 0)), _full((k, D_MODEL)), row, _full((1, D_MODEL)), _full((1, D_MODEL))],
        out_specs=[row, row, row, pl.BlockSpec((tm, 1), lambda i: (i, 0))],
        out_shape=[jax.ShapeDtypeStruct((t, D_MODEL), F32), jax.ShapeDtypeStruct((t, D_MODEL), BF16),
                   jax.ShapeDtypeStruct((t, D_MODEL), F32), jax.ShapeDtypeStruct((t, 1), F32)],
        compiler_params=_cp("parallel"), name=name)(a, w, resid, g, b)


def _ffn_in(x_bf, w, *, name):
    t = x_bf.shape[0]
    tm, tn = _rows(t), _pick(D_FF, 1408)
    nb = D_FF // tn

    def body(x_ref, wg_ref, wu_ref, dg_ref, du_ref, h_ref):
        x = x_ref[...]
        g = _dot(x, wg_ref[...], NN)
        u = _dot(x, wu_ref[...], NN)
        sig = _sigmoid(g)
        silu = g * sig
        dg_ref[...] = (u * sig * (1.0 + g * (1.0 - sig))).astype(BF16)
        du_ref[...] = silu.astype(BF16)
        h_ref[...] = (silu * u).astype(BF16)

    tile = pl.BlockSpec((tm, tn), lambda j, i: (i, j))
    shp = jax.ShapeDtypeStruct((t, D_FF), BF16)
    return pl.pallas_call(
        body, grid=(nb, t // tm),
        in_specs=[pl.BlockSpec((tm, D_MODEL), lambda j, i: (i, 0)),
                  pl.BlockSpec((D_MODEL, tn), lambda j, i: (0, j)),
                  pl.BlockSpec((D_MODEL, tn), lambda j, i: (0, j + nb))],
        out_specs=[tile, tile, tile], out_shape=[shp, shp, shp],
        compiler_params=_cp("parallel", "parallel"), name=name)(x_bf, w, w)


def _ple_fwd(x, x_bf, p, layer, wg, wp, *, name):
    t = x.shape[0]
    tm = _rows(t)

    def body(x_ref, xbf_ref, p_ref, wg_ref, wp_ref, y_ref, ybf_ref, gate_ref, pp_ref):
        gate = _sigmoid(_dot(xbf_ref[...], wg_ref[...], NN))
        pp = _dot(p_ref[...].astype(BF16), wp_ref[...], NN)
        y = x_ref[...] + gate * pp
        y_ref[...] = y
        ybf_ref[...] = y.astype(BF16)
        gate_ref[...] = gate.astype(BF16)
        pp_ref[...] = pp.astype(BF16)

    row = pl.BlockSpec((tm, D_MODEL), lambda i: (i, 0))
    f32, bf = jax.ShapeDtypeStruct((t, D_MODEL), F32), jax.ShapeDtypeStruct((t, D_MODEL), BF16)
    return pl.pallas_call(
        body, grid=(t // tm,),
        in_specs=[row, row, pl.BlockSpec((None, None, tm, PLE_DIM), lambda i: (layer, 0, i, 0)),
                  _full((D_MODEL, D_MODEL)), _full((PLE_DIM, D_MODEL))],
        out_specs=[row, row, row, row], out_shape=[f32, bf, bf, bf],
        compiler_params=_cp("parallel"), name=name)(x, x_bf, p, wg, wp)


def _loss_head(y, target, *, name):
    t = y.shape[0]
    tm = _rows(t)

    def body(y_ref, t_ref, loss_ref, dy_ref):
        e = y_ref[...] - t_ref[...]

        @pl.when(pl.program_id(0) == 0)
        def _():
            loss_ref[...] = jnp.zeros_like(loss_ref)
        sq = jnp.sum(jnp.sum(e * e, axis=1, keepdims=True), axis=0, keepdims=True)
        loss_ref[...] += sq * (0.5 / D_MODEL)
        dy_ref[...] = e * (1.0 / D_MODEL)

    row = pl.BlockSpec((tm, D_MODEL), lambda i: (i, 0))
    return pl.pallas_call(
        body, grid=(t // tm,), in_specs=[row, row],
        out_specs=[_full((1, 1)), row],
        out_shape=[jax.ShapeDtypeStruct((1, 1), F32), jax.ShapeDtypeStruct((t, D_MODEL), F32)],
        compiler_params=_cp("arbitrary"), name=name)(y, target)


def _ple_bwd(dx3, gate, pp, wg, xhat, rstd, g, *, name, guest=None):
    t = dx3.shape[0]
    tm = _rows(t)

    def body(dx_ref, gate_ref, pp_ref, wg_ref, xh_ref, rs_ref, g_ref,
             du_ref, dubf_ref, dpre_ref, dpp_ref, dg_ref, db_ref):
        dx = dx_ref[...]
        gate = gate_ref[...].astype(F32)
        dpre = (dx * pp_ref[...].astype(F32) * gate * (1.0 - gate)).astype(BF16)
        dpre_ref[...] = dpre
        dpp_ref[...] = (dx * gate).astype(BF16)
        dx2 = dx + _dot(dpre, wg_ref[...], NT)
        du, dg, db = _ln_bwd(dx2, xh_ref[...], rs_ref[...], g_ref[...])
        du_ref[...] = du
        dubf_ref[...] = du.astype(BF16)

        @pl.when(pl.program_id(0) == 0)
        def _():
            dg_ref[...] = jnp.zeros_like(dg_ref)
            db_ref[...] = jnp.zeros_like(db_ref)
        dg_ref[...] += dg
        db_ref[...] += db

    row = pl.BlockSpec((tm, D_MODEL), lambda i: (i, 0))
    vec = _full((1, D_MODEL))
    f32, bf = jax.ShapeDtypeStruct((t, D_MODEL), F32), jax.ShapeDtypeStruct((t, D_MODEL), BF16)
    v32 = jax.ShapeDtypeStruct((1, D_MODEL), F32)
    res, extra = _call(
        body, grid=(t // tm,),
        in_specs=[row, row, row, _full((D_MODEL, D_MODEL)), row, pl.BlockSpec((tm, 1), lambda i: (i, 0)), vec],
        out_specs=[row, row, row, row, vec, vec], out_shape=[f32, bf, bf, bf, v32, v32],
        args=(dx3, gate, pp, wg, xhat, rstd, g), sem=("arbitrary",), name=name, guest=guest)
    return res, extra


def _ffn_out_bwd(du_bf, w_out, g, u, *, name):
    t = du_bf.shape[0]
    tm, tn = _rows(t), _pick(D_FF, 1408)

    def body(du_ref, w_ref, hg_ref, hu_ref, dg_ref, dup_ref):
        dh = _dot(du_ref[...], w_ref[...], NT)
        dg_ref[...] = (dh * hg_ref[...].astype(F32)).astype(BF16)
        dup_ref[...] = (dh * hu_ref[...].astype(F32)).astype(BF16)

    tile = pl.BlockSpec((tm, tn), lambda j, i: (i, j))
    shp = jax.ShapeDtypeStruct((t, D_FF), BF16)
    return pl.pallas_call(
        body, grid=(D_FF // tn, t // tm),
        in_specs=[pl.BlockSpec((tm, D_MODEL), lambda j, i: (i, 0)), pl.BlockSpec((tn, D_MODEL), lambda j, i: (j, 0)),
                  tile, tile],
        out_specs=[tile, tile], out_shape=[shp, shp],
        compiler_params=_cp("parallel", "parallel"), name=name)(du_bf, w_out, g, u)


def _mm_nt(parts, w, *, resid=None, ln=None, out_dtype=F32, name, guest=None):
    t, kp = parts[0].shape
    npart = len(parts)
    tm = _rows(t)
    n_in = npart + 1 + (resid is not None) + (3 if ln is not None else 0)

    def body(*refs):
        a_refs, w_ref = refs[:npart], refs[npart]
        pos = npart + 1
        r_ref = None
        if resid is not None:
            r_ref = refs[pos]
            pos += 1
        if ln is not None:
            xh_ref, rs_ref, g_ref = refs[pos:pos + 3]
        outs = refs[n_in:]
        val = _dot(a_refs[0][...], w_ref[:, :kp], NT)
        for pi in range(1, npart):
            val = val + _dot(a_refs[pi][...], w_ref[:, pi * kp:(pi + 1) * kp], NT)
        if r_ref is not None:
            val = val + ALPHA * r_ref[...]
        if ln is None:
            outs[0][...] = val.astype(out_dtype)
        else:
            du, dg, db = _ln_bwd(val, xh_ref[...], rs_ref[...], g_ref[...])
            outs[0][...] = du
            outs[1][...] = du.astype(BF16)

            @pl.when(pl.program_id(0) == 0)
            def _():
                outs[2][...] = jnp.zeros_like(outs[2])
                outs[3][...] = jnp.zeros_like(outs[3])
            outs[2][...] += dg
            outs[3][...] += db

    row = pl.BlockSpec((tm, D_MODEL), lambda i: (i, 0))
    vec = pl.BlockSpec((1, D_MODEL), lambda i: (0, 0))
    in_specs = [pl.BlockSpec((tm, kp), lambda i: (i, 0)) for _ in range(npart)]
    in_specs.append(pl.BlockSpec((D_MODEL, npart * kp), lambda i: (0, 0), pipeline_mode=pl.Buffered(1)))
    args = list(parts) + [w]
    if resid is not None:
        in_specs.append(row)
        args.append(resid)
    if ln is not None:
        in_specs += [row, pl.BlockSpec((tm, 1), lambda i: (i, 0)), vec]
        args += list(ln)
        out_specs = [row, row, vec, vec]
        out_shape = [jax.ShapeDtypeStruct((t, D_MODEL), F32), jax.ShapeDtypeStruct((t, D_MODEL), BF16),
                     jax.ShapeDtypeStruct((1, D_MODEL), F32), jax.ShapeDtypeStruct((1, D_MODEL), F32)]
    else:
        out_specs = [row]
        out_shape = [jax.ShapeDtypeStruct((t, D_MODEL), out_dtype)]
    res, extra = _call(
        body, grid=(t // tm,), in_specs=in_specs, out_specs=out_specs, out_shape=out_shape, args=tuple(args),
        sem=("arbitrary" if ln is not None else "parallel",), name=name, guest=guest)
    res = res if ln is not None else res[0]
    return res if guest is None else (res, extra)


def _rope_tables(t):
    half = ROPE_DIM // 2
    inv = ROPE_THETA ** (-jnp.arange(0, ROPE_DIM, 2, dtype=F32) / ROPE_DIM)
    ang = jnp.arange(t, dtype=F32)[:, None] * inv[None, :]
    cos, sin = jnp.cos(ang), jnp.sin(ang)
    pad = HEAD_DIM - ROPE_DIM
    c = jnp.concatenate([cos, cos, jnp.ones((t, pad), F32)], axis=1)
    s_hi = jnp.concatenate([jnp.zeros((t, half), F32), sin, jnp.zeros((t, pad), F32)], axis=1)
    s_lo = jnp.concatenate([-sin, jnp.zeros((t, half + pad), F32)], axis=1)
    rep = LANES // HEAD_DIM
    return jnp.tile(c, (1, rep)), jnp.tile(s_hi, (1, rep)), jnp.tile(s_lo, (1, rep))


def _rope(x, c, s_hi, s_lo, sign):
    half = ROPE_DIM // 2
    parts = []
    for j in range(x.shape[1] // LANES):
        xg = x[:, j * LANES:(j + 1) * LANES]
        rot = pltpu.roll(xg, half, 1) * s_hi + pltpu.roll(xg, LANES - half, 1) * s_lo
        parts.append(xg * c + sign * rot)
    return jnp.concatenate(parts, axis=1)


def _qkv_rope(x_bf, w, tables, *, name):
    t = x_bf.shape[0]
    tm = _rows(t)
    n = Q_DIM + 2 * KV_DIM

    def body(x_ref, w_ref, c_ref, sh_ref, sl_ref, q_ref, k_ref, v_ref):
        acc = _dot(x_ref[...], w_ref[...], NN)
        c, sh, sl = c_ref[...], sh_ref[...], sl_ref[...]
        q_ref[...] = (_rope(acc[:, :Q_DIM], c, sh, sl, 1.0) * HEAD_DIM ** -0.5).astype(BF16)
        k_ref[...] = _rope(acc[:, Q_DIM:Q_DIM + KV_DIM], c, sh, sl, 1.0).astype(BF16)
        v_ref[...] = acc[:, Q_DIM + KV_DIM:].astype(BF16)

    tab = pl.BlockSpec((tm, LANES), lambda i: (i, 0))
    return pl.pallas_call(
        body, grid=(t // tm,),
        in_specs=[pl.BlockSpec((tm, D_MODEL), lambda i: (i, 0)), _full((D_MODEL, n)), tab, tab, tab],
        out_specs=[pl.BlockSpec((tm, Q_DIM), lambda i: (i, 0)), pl.BlockSpec((tm, KV_DIM), lambda i: (i, 0)),
                   pl.BlockSpec((tm, KV_DIM), lambda i: (i, 0))],
        out_shape=[jax.ShapeDtypeStruct((t, Q_DIM), BF16), jax.ShapeDtypeStruct((t, KV_DIM), BF16),
                   jax.ShapeDtypeStruct((t, KV_DIM), BF16)],
        compiler_params=_cp("parallel"), name=name)(x_bf, w, *tables)


ATT_ROWS = 256
ATT_STRIP = 64


def _window_specs(t, tq):
    r = tq // WINDOW
    last = t // WINDOW - 1
    return [pl.BlockSpec((WINDOW, KV_DIM), lambda i: (jnp.maximum(i * r - 1, 0), 0)),
            pl.BlockSpec((tq, KV_DIM), lambda i: (i, 0)),
            pl.BlockSpec((WINDOW, KV_DIM), lambda i: (jnp.minimum((i + 1) * r, last), 0))]


def _att_valid(base, t):
    rows = GROUP * WINDOW
    qpos = base + (lax.broadcasted_iota(jnp.int32, (rows, 3 * WINDOW), 0) & (WINDOW - 1))
    kpos = base - WINDOW + lax.broadcasted_iota(jnp.int32, (rows, 3 * WINDOW), 1)
    return (jnp.abs(qpos - kpos) <= WINDOW) & (kpos >= 0) & (kpos < t)


def _stack_heads(x, r0, g):
    return jnp.concatenate(
        [x[r0:r0 + WINDOW, (GROUP * g + h) * HEAD_DIM:(GROUP * g + h + 1) * HEAD_DIM] for h in range(GROUP)], axis=0)


def _sink_column(sink_ref, g):
    return jnp.concatenate([jnp.full((WINDOW, 1), sink_ref[GROUP * g + h], F32) for h in range(GROUP)], axis=0)


def _unstack_heads(pieces):
    return jnp.concatenate([pieces[g][h * WINDOW:(h + 1) * WINDOW] for g in range(N_KV_HEADS) for h in range(GROUP)], axis=1)


def _attn_fwd(q, k, v, sink, *, name, guest=None):
    t = q.shape[0]
    tq = min(ATT_ROWS, t)
    nsb = tq // WINDOW

    def body(sink_ref, q_ref, kp_ref, kc_ref, kn_ref, vp_ref, vc_ref, vn_ref, o_ref, l_ref):
        i = pl.program_id(0)
        qv = q_ref[...]
        kw = jnp.concatenate([kp_ref[...], kc_ref[...], kn_ref[...]], axis=0)
        vw = jnp.concatenate([vp_ref[...], vc_ref[...], vn_ref[...]], axis=0)
        ones = jnp.ones((3 * WINDOW, HEAD_DIM), BF16)
        for sb in range(nsb):
            r0 = sb * WINDOW
            bias =jnp.where(_att_valid(i * tq + r0, t)[:WINDOW], 0.0, -1e30)
            pieces = []
            for hh in range(N_Q_HEADS):
                g, h = hh // GROUP, hh % GROUP
                qh = qv[r0:r0 + WINDOW, hh * HEAD_DIM:(hh + 1) * HEAD_DIM]
                kg = kw[r0:r0 + 3 * WINDOW, g * HEAD_DIM:(g + 1) * HEAD_DIM]
                vg = jnp.concatenate([vw[r0:r0 + 3 * WINDOW, g * HEAD_DIM:(g + 1) * HEAD_DIM], ones], axis=1)
                s = _dot(qh, kg, NT) + bias
                snk = sink_ref[hh]
                m = jnp.maximum(jnp.max(s, axis=1, keepdims=True), snk)
                ov = _dot(jnp.exp(s - m).astype(BF16), vg, NN)
                den = ov[:, HEAD_DIM:HEAD_DIM + 1] + jnp.exp(snk - m)
                pieces.append(ov[:, :HEAD_DIM] * (1.0 / den))
                l_ref[g, sb, h * WINDOW:(h + 1) * WINDOW, :] = m + jnp.log(den)
            o_ref[r0:r0 + WINDOW, :] = jnp.concatenate(pieces, axis=1).astype(BF16)

    return _call(
        body, grid=(t // tq,),
        in_specs=[pl.BlockSpec(memory_space=pltpu.SMEM), pl.BlockSpec((tq, Q_DIM), lambda i: (i, 0))]
        + _window_specs(t, tq) + _window_specs(t, tq),
        out_specs=[pl.BlockSpec((tq, Q_DIM), lambda i: (i, 0)),
                   pl.BlockSpec((N_KV_HEADS, nsb, GROUP * WINDOW, 1), lambda i: (0, i, 0, 0))],
        out_shape=[jax.ShapeDtypeStruct((t, Q_DIM), BF16),
                   jax.ShapeDtypeStruct((N_KV_HEADS, t // WINDOW, GROUP * WINDOW, 1), F32)],
        args=(sink, q, k, k, k, v, v, v), sem=("parallel",), name=name, guest=guest)


def _attn_bwd(q, k, v, o, do, lse, sink, *, name):
    t = q.shape[0]
    tq = min(ATT_ROWS, t)
    nsb = tq // WINDOW

    def body(sink_ref, q_ref, o_ref, do_ref, l_ref, kp_ref, kc_ref, kn_ref, vp_ref, vc_ref, vn_ref,
             dq_ref, dkw_ref, dvw_ref, dsink_ref):
        i = pl.program_id(0)
        qv, ov, dov = q_ref[...], o_ref[...], do_ref[...]
        kw = jnp.concatenate([kp_ref[...], kc_ref[...], kn_ref[...]], axis=0)
        vw = jnp.concatenate([vp_ref[...], vc_ref[...], vn_ref[...]], axis=0)
        lane16 = lax.broadcasted_iota(jnp.int32, (1, N_Q_HEADS), 1)
        dsink = jnp.zeros((1, N_Q_HEADS), F32)
        for sb in range(nsb):
            r0 = sb * WINDOW
            valid = _att_valid(i * tq + r0, t)
            dq_p, dk_p, dv_p = [], [], []
            for g in range(N_KV_HEADS):
                qs, dos = _stack_heads(qv, r0, g), _stack_heads(dov, r0, g)
                delta = jnp.sum(dos.astype(F32) * _stack_heads(ov, r0, g).astype(F32), axis=1, keepdims=True)
                kg = kw[r0:r0 + 3 * WINDOW, g * HEAD_DIM:(g + 1) * HEAD_DIM]
                vg = vw[r0:r0 + 3 * WINDOW, g * HEAD_DIM:(g + 1) * HEAD_DIM]
                lse_col = l_ref[g, sb]
                pr = jnp.where(valid, jnp.exp(_dot(qs, kg, NT) - lse_col), 0.0)
                ds = (pr * (_dot(dos, vg, NT) - delta)).astype(BF16)
                dq_p.append(_dot(ds, kg, NN))
                dk_p.append(_dot(ds, qs, TN))
                dv_p.append(_dot(pr.astype(BF16), dos, TN))
                ps = jnp.exp(_sink_column(sink_ref, g) - lse_col) * delta
                for h in range(GROUP):
                    tot = jnp.sum(ps[h * WINDOW:(h + 1) * WINDOW], axis=0, keepdims=True)
                    dsink = dsink - jnp.where(lane16 == GROUP * g + h, tot, 0.0)
            dq_ref[r0:r0 + WINDOW, :] = _unstack_heads(dq_p)
            dkw_ref[sb] = jnp.concatenate(dk_p, axis=1)
            dvw_ref[sb] = jnp.concatenate(dv_p, axis=1)

        @pl.when(i == 0)
        def _():
            dsink_ref[...] = jnp.zeros_like(dsink_ref)
        dsink_ref[...] += dsink

    rowq = pl.BlockSpec((tq, Q_DIM), lambda i: (i, 0))
    win = pl.BlockSpec((nsb, 3 * WINDOW, KV_DIM), lambda i: (i, 0, 0))
    wshape = jax.ShapeDtypeStruct((t // WINDOW, 3 * WINDOW, KV_DIM), F32)
    return pl.pallas_call(
        body, grid=(t // tq,),
        in_specs=[pl.BlockSpec(memory_space=pltpu.SMEM), rowq, rowq, rowq,
                  pl.BlockSpec((N_KV_HEADS, nsb, GROUP * WINDOW, 1), lambda i: (0, i, 0, 0))]
        + _window_specs(t, tq) + _window_specs(t, tq),
        out_specs=[rowq, win, win, _full((1, N_Q_HEADS))],
        out_shape=[jax.ShapeDtypeStruct((t, Q_DIM), F32), wshape, wshape, jax.ShapeDtypeStruct((1, N_Q_HEADS), F32)],
        compiler_params=_cp("arbitrary"), name=name)(sink, q, o, do, lse, k, k, k, v, v, v)


def _attn_post_bwd(dq, dkw, dvw, tables, *, name):
    t = dq.shape[0]
    nb = t // WINDOW
    n = Q_DIM + 2 * KV_DIM

    def body(dq_ref, ka_ref, kb_ref, kc_ref, va_ref, vb_ref, vc_ref, c_ref, sh_ref, sl_ref, o_ref):
        j = pl.program_id(0)
        lo = (j > 0).astype(F32)
        hi = (j < nb - 1).astype(F32)
        c, sh, sl = c_ref[...], sh_ref[...], sl_ref[...]
        dk = ka_ref[...] * hi + kb_ref[...] + kc_ref[...] * lo
        dv = va_ref[...] * hi + vb_ref[...] + vc_ref[...] * lo
        o_ref[:, :Q_DIM] = (_rope(dq_ref[...], c, sh, sl, -1.0) * HEAD_DIM ** -0.5).astype(BF16)
        o_ref[:, Q_DIM:Q_DIM + KV_DIM] = _rope(dk, c, sh, sl, -1.0).astype(BF16)
        o_ref[:, Q_DIM + KV_DIM:] = dv.astype(BF16)

    wins = [pl.BlockSpec((None, WINDOW, KV_DIM), lambda j: (jnp.minimum(j + 1, nb - 1), 0, 0)),
            pl.BlockSpec((None, WINDOW, KV_DIM), lambda j: (j, 1, 0)),
            pl.BlockSpec((None, WINDOW, KV_DIM), lambda j: (jnp.maximum(j - 1, 0), 2, 0))]
    tab = pl.BlockSpec((WINDOW, LANES), lambda j: (j, 0))
    return pl.pallas_call(
        body, grid=(nb,),
        in_specs=[pl.BlockSpec((WINDOW, Q_DIM), lambda j: (j, 0))] + wins + wins + [tab, tab, tab],
        out_specs=pl.BlockSpec((WINDOW, n), lambda j: (j, 0)),
        out_shape=jax.ShapeDtypeStruct((t, n), BF16),
        compiler_params=_cp("parallel"), name=name)(dq, dkw, dkw, dkw, dvw, dvw, dvw, *tables)


HGRN_ROWS = 512


def _cum_mask(rev, transpose=False):
    row = lax.broadcasted_iota(jnp.int32, (HGRN_CHUNK, HGRN_CHUNK), 0)
    col = lax.broadcasted_iota(jnp.int32, (HGRN_CHUNK, HGRN_CHUNK), 1)
    return (row <= col) if rev != transpose else (row >= col)


def _gates(zq, zf, lb):
    q = zq * _sigmoid(zq)
    sig = _sigmoid(zf)
    f = lb + (1.0 - lb) * sig
    k = (1.0 - lb) * (1.0 - sig)
    return q, sig, f, k


def _intra_exact(q, k, b, mask):
    rows = lax.broadcasted_iota(jnp.int32, (HGRN_CHUNK, 1), 0)
    cols = lax.broadcasted_iota(jnp.int32, (HGRN_CHUNK, HGRN_CHUNK), 1)

    def it(s, a):
        pick = rows == s
        b_s = jnp.sum(jnp.where(pick, b, 0.0), axis=0, keepdims=True)
        k_s = jnp.sum(jnp.where(pick, k, 0.0), axis=0, keepdims=True)
        col = jnp.sum(q * jnp.exp(jnp.minimum(b - b_s, 0.0)) * k_s, axis=1, keepdims=True)
        return jnp.where(cols == s, col, a)

    a = lax.fori_loop(0, HGRN_CHUNK, it, jnp.zeros((HGRN_CHUNK, HGRN_CHUNK), F32))
    return jnp.where(mask, a, 0.0)


def _intra_exact_bwd(q, k, b, da):
    rows = lax.broadcasted_iota(jnp.int32, (HGRN_CHUNK, 1), 0)
    cols = lax.broadcasted_iota(jnp.int32, (HGRN_CHUNK, HGRN_CHUNK), 1)

    def it(s, carry):
        dq, dk = carry
        pick = rows == s
        b_s = jnp.sum(jnp.where(pick, b, 0.0), axis=0, keepdims=True)
        k_s = jnp.sum(jnp.where(pick, k, 0.0), axis=0, keepdims=True)
        w = jnp.sum(jnp.where(cols == s, da, 0.0), axis=1, keepdims=True) * jnp.exp(jnp.minimum(b - b_s, 0.0))
        dq = dq + w * k_s
        dk = jnp.where(pick, jnp.sum(w * q, axis=0, keepdims=True), dk)
        return dq, dk

    z = jnp.zeros((HGRN_CHUNK, HGRN_KEY), F32)
    return lax.fori_loop(0, HGRN_CHUNK, it, (z, z))


def _hgrn_scan_fwd(z, lb, rev, *, name):
    t = z.shape[0]
    rb = min(HGRN_ROWS, t)
    nb, nch = t // rb, rb // HGRN_CHUNK
    fcol = HGRN_HEADS * (2 if rev else 1)

    def blk(n):
        return nb - 1 - n if rev else n

    def body(zq_ref, zf_ref, zv_ref, lb_ref, o_ref, s_ref, st_ref):
        @pl.when(pl.program_id(1) == 0)
        def _():
            st_ref[...] = jnp.zeros_like(st_ref)
        lbv = lb_ref[...]
        cmask = _cum_mask(rev)
        cum = cmask.astype(F32)

        def chunk(c, carry):
            cc = nch - 1 - c if rev else c
            r0 = pl.multiple_of(cc * HGRN_CHUNK, HGRN_CHUNK)
            sl = pl.ds(r0, HGRN_CHUNK)
            q, _, f, k = _gates(zq_ref[sl, :], zf_ref[sl, :], lbv)
            v = zv_ref[sl, :].astype(BF16)
            g = jnp.log(f)
            b = jnp.dot(cum, g, precision=lax.Precision.HIGHEST, preferred_element_type=F32)
            tot = jnp.sum(g, axis=0, keepdims=True)
            qd = q * jnp.exp(b)
            st = st_ref[...]
            st_bf = st.astype(BF16)
            s_ref[cc] = st_bf
            a = lax.cond(
                jnp.min(tot) >= FACTORED_DECAY_LIMIT,
                lambda: jnp.where(cmask, _dot(qd.astype(BF16), (k * jnp.exp(-b)).astype(BF16), NT), 0.0),
                lambda: _intra_exact(q, k, b, cmask))
            o_ref[sl, :] = _dot(qd.astype(BF16), st_bf, NT) + _dot(a.astype(BF16), v, NN)
            kd = (k * jnp.exp(tot - b)).astype(BF16)
            st_ref[...] = st * jnp.exp(tot) + _dot(v, kd, TN)
            return carry

        lax.fori_loop(0, nch, chunk, 0)

    def col(off):
        return pl.BlockSpec((rb, LANES), lambda h, n: (blk(n), off + h))

    return pl.pallas_call(
        body, grid=(HGRN_HEADS, nb),
        in_specs=[col(0), col(fcol), col(3 * HGRN_HEADS), pl.BlockSpec((None, 1, LANES), lambda h, n: (h, 0, 0))],
        out_specs=[pl.BlockSpec((rb, LANES), lambda h, n: (blk(n), h)),
                   pl.BlockSpec((None, nch, LANES, LANES), lambda h, n: (h, blk(n), 0, 0))],
        out_shape=[jax.ShapeDtypeStruct((t, D_MODEL), F32),
                   jax.ShapeDtypeStruct((HGRN_HEADS, t // HGRN_CHUNK, LANES, LANES), BF16)],
        scratch_shapes=[pltpu.VMEM((LANES, LANES), F32)],
        compiler_params=_cp("parallel", "arbitrary"), name=name)(z, z, z, lb)


def _hgrn_scan_bwd(z, lb, d_o, states, rev, *, name):
    t = z.shape[0]
    rb = min(HGRN_ROWS, t)
    nb, nch = t // rb, rb // HGRN_CHUNK
    fcol = HGRN_HEADS * (2 if rev else 1)

    def blk(n):
        return n if rev else nb - 1 - n

    def body(zq_ref, zf_ref, zv_ref, lb_ref, do_ref, s_ref, dq_ref, dzf_ref, dv_ref, dlb_ref, dst_ref):
        @pl.when(pl.program_id(1) == 0)
        def _():
            dst_ref[...] = jnp.zeros_like(dst_ref)
            dlb_ref[...] = jnp.zeros_like(dlb_ref)
        lbv = lb_ref[...]
        cmask = _cum_mask(rev)
        cum = cmask.astype(F32)
        cum_t = _cum_mask(rev, transpose=True).astype(F32)

        def chunk(c, carry):
            cc = c if rev else nch - 1 - c
            r0 = pl.multiple_of(cc * HGRN_CHUNK, HGRN_CHUNK)
            sl = pl.ds(r0, HGRN_CHUNK)
            zq = zq_ref[sl, :]
            q, sig, f, k = _gates(zq, zf_ref[sl, :], lbv)
            v = zv_ref[sl, :].astype(BF16)
            g = jnp.log(f)
            b = jnp.dot(cum, g, precision=lax.Precision.HIGHEST, preferred_element_type=F32)
            tot = jnp.sum(g, axis=0, keepdims=True)
            eb = jnp.exp(b)
            qd = (q * eb).astype(BF16)
            kd = (k * jnp.exp(tot - b)).astype(BF16)
            do = do_ref[sl, :].astype(BF16)
            st0 = s_ref[cc]
            dst = dst_ref[...]
            dst_bf = dst.astype(BF16)
            da = jnp.where(cmask, _dot(do, v, NT), 0.0)
            factored = jnp.min(tot) >= FACTORED_DECAY_LIMIT

            def fast():
                ke = jnp.exp(-b)
                kf = (k * ke).astype(BF16)
                a = jnp.where(cmask, _dot(qd, kf, NT), 0.0)
                da_bf = da.astype(BF16)
                dqf, dkf = _dot(da_bf, kf, NN), _dot(da_bf, qd, TN)
                return a, dqf * eb, dkf * ke, qd.astype(F32) * dqf - kf.astype(F32) * dkf

            def slow():
                dq_i, dk_i = _intra_exact_bwd(q, k, b, da)
                return _intra_exact(q, k, b, cmask), dq_i, dk_i, q * dq_i - k * dk_i

            a, dq_i, dk_i, db_i = lax.cond(factored, fast, slow)
            dv_ref[sl, :] = _dot(a.astype(BF16), do, TN) + _dot(kd, dst_bf, NT)
            dq_x = _dot(do, st0, NN) * eb
            dq = dq_i + dq_x
            dk_x = _dot(v, dst_bf, NN) * jnp.exp(tot - b)
            dk = dk_i + dk_x
            etot = jnp.exp(tot)
            dtot = jnp.sum(k * dk_x, axis=0, keepdims=True) + etot * jnp.sum(dst * st0.astype(F32), axis=0, keepdims=True)
            dst_ref[...] = dst * etot + _dot(do, qd, TN)
            db = db_i + q * dq_x - k * dk_x
            dg =jnp.dot(cum_t, db, precision=lax.Precision.HIGHEST, preferred_element_type=F32) + dtot
            sq = _sigmoid(zq)
            dq_ref[sl, :] = dq * sq * (1.0 + zq * (1.0 - sq))
            dfk = dg / f - dk
            dzf_ref[sl, :] = (dfk * (1.0 - lbv) * sig * (1.0 - sig)).astype(BF16)
            dlb_ref[...] += jnp.sum(dfk * (1.0 - sig), axis=0, keepdims=True)
            return carry

        lax.fori_loop(0, nch, chunk, 0)

    def col(off):
        return pl.BlockSpec((rb, LANES), lambda h, n: (blk(n), off + h))

    out_col = pl.BlockSpec((rb, LANES), lambda h, n: (blk(n), h))
    return pl.pallas_call(
        body, grid=(HGRN_HEADS, nb),
        in_specs=[col(0), col(fcol), col(3 * HGRN_HEADS), pl.BlockSpec((None, 1, LANES), lambda h, n: (h, 0, 0)),
                  out_col, pl.BlockSpec((None, nch, LANES, LANES), lambda h, n: (h, blk(n), 0, 0))],
        out_specs=[out_col, out_col, out_col, pl.BlockSpec((None, 1, LANES), lambda h, n: (h, 0, 0))],
        out_shape=[jax.ShapeDtypeStruct((t, D_MODEL), F32), jax.ShapeDtypeStruct((t, D_MODEL), BF16),
                   jax.ShapeDtypeStruct((t, D_MODEL), F32), jax.ShapeDtypeStruct((HGRN_HEADS, 1, LANES), F32)],
        scratch_shapes=[pltpu.VMEM((LANES, LANES), F32)],
        compiler_params=_cp("parallel", "arbitrary"), name=name)(z, z, z, lb, d_o, states)


def _head(a, h):
    return a[:, h * LANES:(h + 1) * LANES]


def _hgrn_fwd(z, lb, rev, *, name, guest=None):
    t = z.shape[0]
    rb = min(HGRN_ROWS, t)
    nb, nch = t // rb, rb // HGRN_CHUNK
    heads = range(HGRN_HEADS)

    def blk(n):
        return nb - 1 - n if rev else n

    def body(zq_ref, zf_ref, zv_ref, lb_ref, o_ref, s_ref, st_ref):
        @pl.when(pl.program_id(0) == 0)
        def _():
            st_ref[...] = jnp.zeros_like(st_ref)
        lbv = lb_ref[...]
        cmask = _cum_mask(rev)
        cum = cmask.astype(F32)

        def chunk(c, carry):
            cc = nch - 1 - c if rev else c
            sl = pl.ds(pl.multiple_of(cc * HGRN_CHUNK, HGRN_CHUNK), HGRN_CHUNK)
            q, _, f, k = _gates(zq_ref[sl, :], zf_ref[sl, :], lbv)
            v = zv_ref[sl, :].astype(BF16)
            g = jnp.log(f)
            b = jnp.dot(cum, g, precision=lax.Precision.HIGHEST, preferred_element_type=F32)
            tot = jnp.sum(g, axis=0, keepdims=True)
            qd = (q * jnp.exp(b)).astype(BF16)
            kd = (k * jnp.exp(tot - b)).astype(BF16)
            etot = jnp.exp(tot)

            def factored():
                kf = (k * jnp.exp(-b)).astype(BF16)
                return tuple(jnp.where(cmask, _dot(_head(qd, h), _head(kf, h), NT), 0.0) for h in heads)

            def exact():
                return tuple(_intra_exact(_head(q, h), _head(k, h), _head(b, h), cmask) for h in heads)

            a = lax.cond(jnp.min(tot) >= FACTORED_DECAY_LIMIT, factored, exact)
            for h in heads:
                st = st_ref[h]
                st_bf = st.astype(BF16)
                s_ref[h, cc] = st_bf
                o_ref[sl, h * LANES:(h + 1) * LANES] = (
                    _dot(_head(qd, h), st_bf, NT) + _dot(a[h].astype(BF16), _head(v, h), NN))
                st_ref[h] = st * _head(etot, h) + _dot(_head(v, h), _head(kd, h), TN)
            return carry

        lax.fori_loop(0, nch, chunk, 0)

    def col(j):
        return pl.BlockSpec((rb, D_MODEL), lambda n: (blk(n), j))

    return _call(
        body, grid=(nb,),
        in_specs=[col(0), col(2 if rev else 1), col(3), _full((1, D_MODEL))],
        out_specs=[col(0), pl.BlockSpec((HGRN_HEADS, nch, LANES, LANES), lambda n: (0, blk(n), 0, 0))],
        out_shape=[jax.ShapeDtypeStruct((t, D_MODEL), F32),
                   jax.ShapeDtypeStruct((HGRN_HEADS, t // HGRN_CHUNK, LANES, LANES), BF16)],
        scratch_shapes=[pltpu.VMEM((HGRN_HEADS, LANES, LANES), F32)],
        args=(z, z, z, lb), sem=("arbitrary",), name=name, guest=guest)


def _hgrn_bwd(z, lb, d_o, states, rev, *, name):
    t = z.shape[0]
    rb = min(HGRN_ROWS, t)
    nb, nch = t // rb, rb // HGRN_CHUNK
    heads = range(HGRN_HEADS)

    def blk(n):
        return n if rev else nb - 1 - n

    def body(zq_ref, zf_ref, zv_ref, lb_ref, do_ref, s_ref, dq_ref, dzf_ref, dv_ref, dlb_ref, dst_ref):
        @pl.when(pl.program_id(0) == 0)
        def _():
            dst_ref[...] = jnp.zeros_like(dst_ref)
            dlb_ref[...] = jnp.zeros_like(dlb_ref)
        lbv = lb_ref[...]
        cmask = _cum_mask(rev)
        cum = cmask.astype(F32)
        cum_t = _cum_mask(rev, transpose=True).astype(F32)

        def chunk(c, carry):
            cc = c if rev else nch - 1 - c
            sl = pl.ds(pl.multiple_of(cc * HGRN_CHUNK, HGRN_CHUNK), HGRN_CHUNK)
            zq = zq_ref[sl, :]
            q, sig, f, k = _gates(zq, zf_ref[sl, :], lbv)
            v = zv_ref[sl, :].astype(BF16)
            g = jnp.log(f)
            b = jnp.dot(cum, g, precision=lax.Precision.HIGHEST, preferred_element_type=F32)
            tot = jnp.sum(g, axis=0, keepdims=True)
            eb = jnp.exp(b)
            etb = jnp.exp(tot - b)
            etot = jnp.exp(tot)
            qd = (q * eb).astype(BF16)
            kd = (k * etb).astype(BF16)
            do = do_ref[sl, :].astype(BF16)
            da = [jnp.where(cmask, _dot(_head(do, h), _head(v, h), NT), 0.0) for h in heads]

            def factored():
                ke = jnp.exp(-b)
                kf = (k * ke).astype(BF16)
                res = []
                for h in heads:
                    qh, kh = _head(qd, h), _head(kf, h)
                    da_bf = da[h].astype(BF16)
                    dqf, dkf = _dot(da_bf, kh, NN), _dot(da_bf, qh, TN)
                    res.append((jnp.where(cmask, _dot(qh, kh, NT), 0.0), dqf * _head(eb, h), dkf * _head(ke, h),
                                qh.astype(F32) * dqf - kh.astype(F32) * dkf))
                return tuple(res)

            def exact():
                res = []
                for h in heads:
                    qh, kh, bh = _head(q, h), _head(k, h), _head(b, h)
                    dq_i, dk_i = _intra_exact_bwd(qh, kh, bh, da[h])
                    res.append((_intra_exact(qh, kh, bh, cmask), dq_i, dk_i, qh * dq_i - kh * dk_i))
                return tuple(res)

            intra = lax.cond(jnp.min(tot) >= FACTORED_DECAY_LIMIT, factored, exact)
            dq_p, dk_p, db_p, dtot_p = [], [], [], []
            for h in heads:
                a, dq_i, dk_i, db_i = intra[h]
                doh, vh, qh, kh = _head(do, h), _head(v, h), _head(q, h), _head(k, h)
                st0 = s_ref[h, cc]
                dst = dst_ref[h]
                dst_bf = dst.astype(BF16)
                dv_ref[sl, h * LANES:(h + 1) * LANES] = _dot(a.astype(BF16), doh, TN) + _dot(_head(kd, h), dst_bf, NT)
                dq_x = _dot(doh, st0, NN) * _head(eb, h)
                dk_x = _dot(vh, dst_bf, NN) * _head(etb, h)
                dtot_p.append(jnp.sum(kh * dk_x, axis=0, keepdims=True)
                              + _head(etot, h) * jnp.sum(dst * st0.astype(F32), axis=0, keepdims=True))
                dst_ref[h] = dst * _head(etot, h) + _dot(doh, _head(qd, h), TN)
                dq_p.append(dq_i + dq_x)
                dk_p.append(dk_i + dk_x)
                db_p.append(db_i + qh * dq_x - kh * dk_x)
            dq, dk, db = (jnp.concatenate(x, axis=1) for x in (dq_p, dk_p, db_p))
            dg = jnp.dot(cum_t, db, precision=lax.Precision.HIGHEST, preferred_element_type=F32) + jnp.concatenate(dtot_p, axis=1)
            sq = _sigmoid(zq)
            dq_ref[sl, :] = dq * sq * (1.0 + zq * (1.0 - sq))
            dfk = dg / f - dk
            dzf_ref[sl, :] = (dfk * (1.0 - lbv) * sig * (1.0 - sig)).astype(BF16)
            dlb_ref[...] += jnp.sum(dfk * (1.0 - sig), axis=0, keepdims=True)
            return carry

        lax.fori_loop(0, nch, chunk, 0)

    def col(j):
        return pl.BlockSpec((rb, D_MODEL), lambda n: (blk(n), j))

    return pl.pallas_call(
        body, grid=(nb,),
        in_specs=[col(0), col(2 if rev else 1), col(3), _full((1, D_MODEL)), col(0),
                  pl.BlockSpec((HGRN_HEADS, nch, LANES, LANES), lambda n: (0, blk(n), 0, 0))],
        out_specs=[col(0), col(0), col(0), _full((1, D_MODEL))],
        out_shape=[jax.ShapeDtypeStruct((t, D_MODEL), F32), jax.ShapeDtypeStruct((t, D_MODEL), BF16),
                   jax.ShapeDtypeStruct((t, D_MODEL), F32), jax.ShapeDtypeStruct((1, D_MODEL), F32)],
        scratch_shapes=[pltpu.VMEM((HGRN_HEADS, LANES, LANES), F32)],
        compiler_params=_cp("arbitrary"), name=name)(z, z, z, lb, d_o, states)


def _hgrn_post_fwd(o_f, o_b, z, norm_g, *, name):
    t = o_f.shape[0]
    tm = _rows(t)

    def body(of_ref, ob_ref, gate_ref, ng_ref, y_ref):
        ng = ng_ref[...]
        for h in range(HGRN_HEADS):
            sl = slice(h * LANES, (h + 1) * LANES)
            o = of_ref[:, sl] + ob_ref[:, sl]
            r = lax.rsqrt(jnp.mean(o * o, axis=1, keepdims=True) + LN_EPS)
            gt = gate_ref[:, sl]
            y_ref[:, sl] = (o * r * ng * (gt * _sigmoid(gt))).astype(BF16)

    row = pl.BlockSpec((tm, D_MODEL), lambda i: (i, 0))
    return pl.pallas_call(
        body, grid=(t // tm,),
        in_specs=[row, row, pl.BlockSpec((tm, D_MODEL), lambda i: (i, 4)), _full((1, LANES))],
        out_specs=row, out_shape=jax.ShapeDtypeStruct((t, D_MODEL), BF16),
        compiler_params=_cp("parallel"), name=name)(o_f, o_b, z, norm_g)


def _hgrn_post_bwd(dy, o_f, o_b, z, norm_g, *, name):
    t = o_f.shape[0]
    tm = _rows(t)

    def body(dy_ref, of_ref, ob_ref, gate_ref, ng_ref, do_ref, dgate_ref, dng_ref):
        ng = ng_ref[...]
        dng = jnp.zeros((1, LANES), F32)
        for h in range(HGRN_HEADS):
            sl = slice(h * LANES, (h + 1) * LANES)
            o = of_ref[:, sl] + ob_ref[:, sl]
            r = lax.rsqrt(jnp.mean(o * o, axis=1, keepdims=True) + LN_EPS)
            gt = gate_ref[:, sl]
            sg = _sigmoid(gt)
            dyv = dy_ref[:, sl].astype(F32)
            xn = o * r
            dgate_ref[:, sl] = (dyv * xn * ng * sg * (1.0 + gt * (1.0 - sg))).astype(BF16)
            dn = dyv * gt * sg
            dng = dng + jnp.sum(dn * xn, axis=0, keepdims=True)
            dxn = dn * ng
            do_ref[:, sl] = r * (dxn - xn * jnp.mean(dxn * xn, axis=1, keepdims=True))

        @pl.when(pl.program_id(0) == 0)
        def _():
            dng_ref[...] = jnp.zeros_like(dng_ref)
        dng_ref[...] += dng

    row = pl.BlockSpec((tm, D_MODEL), lambda i: (i, 0))
    return pl.pallas_call(
        body, grid=(t // tm,),
        in_specs=[row, row, row, pl.BlockSpec((tm, D_MODEL), lambda i: (i, 4)), _full((1, LANES))],
        out_specs=[row, row, _full((1, LANES))],
        out_shape=[jax.ShapeDtypeStruct((t, D_MODEL), F32), jax.ShapeDtypeStruct((t, D_MODEL), BF16),
                   jax.ShapeDtypeStruct((1, LANES), F32)],
        compiler_params=_cp("arbitrary"), name=name)(dy, o_f, o_b, z, norm_g)


def _hgrn_combine(dq_f, dq_b, dzf_f, dzf_b, dv_f, dv_b, dgate, *, name):
    t = dq_f.shape[0]
    tm = _rows(t)

    def body(qf, qb, ff, fb, vf, vb, gt, o_ref):
        o_ref[:, 0 * D_MODEL:1 * D_MODEL] = (qf[...] + qb[...]).astype(BF16)
        o_ref[:, 1 * D_MODEL:2 * D_MODEL] = ff[...]
        o_ref[:, 2 * D_MODEL:3 * D_MODEL] = fb[...]
        o_ref[:, 3 * D_MODEL:4 * D_MODEL] = (vf[...] + vb[...]).astype(BF16)
        o_ref[:, 4 * D_MODEL:5 * D_MODEL] = gt[...]

    row = pl.BlockSpec((tm, D_MODEL), lambda i: (i, 0))
    return pl.pallas_call(
        body, grid=(t // tm,), in_specs=[row] * 7,
        out_specs=pl.BlockSpec((tm, 5 * D_MODEL), lambda i: (i, 0)),
        out_shape=jax.ShapeDtypeStruct((t, 5 * D_MODEL), BF16),
        compiler_params=_cp("parallel"), name=name)(dq_f, dq_b, dzf_f, dzf_b, dv_f, dv_b, dgate)


def _lower_bounds(logits2d, *, name):
    def body(l_ref, lb_ref):
        l = l_ref[...]
        e = jnp.exp(l - jnp.max(l, axis=0, keepdims=True))
        sm = e / jnp.sum(e, axis=0, keepdims=True)
        s1, s2, s3 = sm[1:2], sm[2:3], sm[3:4]
        lb_ref[...] = jnp.concatenate([jnp.zeros_like(s1), s1, s1 + s2, s1 + s2 + s3], axis=0)

    return pl.pallas_call(body, out_shape=jax.ShapeDtypeStruct(logits2d.shape, F32), name=name)(logits2d)


def _lower_bounds_bwd(logits2d, dlb, *, name):
    def body(l_ref, d_ref, o_ref):
        l = l_ref[...]
        e = jnp.exp(l - jnp.max(l, axis=0, keepdims=True))
        sm = e / jnp.sum(e, axis=0, keepdims=True)
        d = d_ref[...]
        d1, d2, d3 = d[1:2], d[2:3], d[3:4]
        dsm = jnp.concatenate([jnp.zeros_like(d1), d1 + d2 + d3, d2 + d3, d3], axis=0)
        o_ref[...] = sm * (dsm - jnp.sum(sm * dsm, axis=0, keepdims=True))

    return pl.pallas_call(body, out_shape=jax.ShapeDtypeStruct(logits2d.shape, F32), name=name)(logits2d, dlb)


def _adamw(w, g, m, v, *, name):
    r, c = w.shape
    tr = r if r <= 512 else _pick_rows(r)

    def body(w_ref, g_ref, m_ref, v_ref, d_ref, nm_ref, nv_ref):
        gv = g_ref[...]
        nm = ADAM_B1 * m_ref[...] + (1.0 - ADAM_B1) * gv
        nv = ADAM_B2 * v_ref[...] + (1.0 - ADAM_B2) * (gv * gv)
        m_hat = nm / (1.0 - ADAM_B1 ** ADAM_STEP)
        v_hat = nv / (1.0 - ADAM_B2 ** ADAM_STEP)
        d_ref[...] = -ADAM_LR * (m_hat / (jnp.sqrt(v_hat) + ADAM_EPS) + ADAM_WD * w_ref[...])
        nm_ref[...] = nm
        nv_ref[...] = nv

    blk = pl.BlockSpec((tr, c), lambda i: (i, 0))
    shp = jax.ShapeDtypeStruct((r, c), F32)
    return pl.pallas_call(body, grid=(r // tr,), in_specs=[blk] * 4, out_specs=[blk] * 3, out_shape=[shp] * 3,
                          compiler_params=_cp("parallel"), name=name)(w, g, m, v)


def _pick_rows(r, cap=512):
    best = 8
    for d in range(8, cap + 1, 8):
        if r % d == 0:
            best = d
    assert r % best == 0, r
    return best


def _local_step(x, p, target, w, sp):
    t = x.shape[0]
    tables = _rope_tables(t)
    logits2d = sp["hgrn_lb_logits"].reshape(DEPTH, 2 * D_MODEL)
    lb_all = _lower_bounds(logits2d, name="lb_fwd").reshape(DEPTH, 2, 1, D_MODEL)
    row = lambda a, i: a[i][None]

    saved = []
    xf, xb = x, x.astype(BF16)
    for i in range(DEPTH):
        j = i // 2
        s = dict(xin_bf=xb)
        if i % 2 == 0:
            q, k, v = _qkv_rope(xb, w["att_w_qkv"][j], tables, name=f"qkv_rope_{i}")
            o, lse = _attn_fwd(q, k, v, sp["att_sink"][j], name=f"attn_fwd_{i}")
            s.update(q=q, k=k, v=v, o=o, lse=lse)
            mix_in, w_o = o, w["att_w_o"][j]
        else:
            z = _mm_nn(xb, w["hgrn_w_in"][j], out_dtype=F32, name=f"hgrn_in_{i}")
            o_f, s_f = _hgrn_fwd(z, lb_all[i, 0], False, name=f"hgrn_scan_f_{i}")
            o_b, s_b = _hgrn_fwd(z, lb_all[i, 1], True, name=f"hgrn_scan_b_{i}")
            og = _hgrn_post_fwd(o_f, o_b, z, row(sp["hgrn_norm_g"], j), name=f"hgrn_post_{i}")
            s.update(z=z, o_f=o_f, o_b=o_b, s_f=s_f, s_b=s_b, og=og)
            mix_in, w_o = og, w["hgrn_w_o"][j]
        x1, x1b, xh1, rs1 = _mm_res_ln(mix_in, w_o, xf, row(sp["ln_mix_g"], i), row(sp["ln_mix_b"], i), name=f"mix_out_ln_{i}")
        g, u, h = _ffn_in(x1b, w["ffn_w_in"][i], name=f"ffn_in_{i}")
        x2, x2b, xh2, rs2 = _mm_res_ln(h, w["ffn_w_out"][i], x1, row(sp["ln_ffn_g"], i), row(sp["ln_ffn_b"], i), name=f"ffn_out_ln_{i}")
        xf, xb, gate, pp = _ple_fwd(x2, x2b, p, i, w["ple_w_gate"][i], w["ple_w_proj"][i], name=f"ple_fwd_{i}")
        s.update(x1b=x1b, xh1=xh1, rs1=rs1, g=g, u=u, h=h, x2b=x2b, xh2=xh2, rs2=rs2, gate=gate, pp=pp)
        saved.append(s)

    loss, dx = _loss_head(xf, target, name="loss_head")

    gw = {n: [None] * w[n].shape[0] for n in w}
    gs = {n: [None] * DEPTH for n in ("ln_mix_g", "ln_mix_b", "ln_ffn_g", "ln_ffn_b")}
    gs.update(att_sink=[None] * 2, hgrn_norm_g=[None] * 2)
    dlb = [jnp.zeros((2, D_MODEL), F32)] * DEPTH
    for i in reversed(range(DEPTH)):
        j = i // 2
        s = saved[i]
        du2, du2b, dpre, dpp, gs["ln_ffn_g"][i], gs["ln_ffn_b"][i] = _ple_bwd(
            dx, s["gate"], s["pp"], w["ple_w_gate"][i], s["xh2"], s["rs2"], row(sp["ln_ffn_g"], i), name=f"ple_bwd_{i}")
        gw["ple_w_gate"][i] = _mm_tn(s["x2b"], dpre, name=f"dw_ple_gate_{i}")
        gw["ple_w_proj"][i] = _mm_tn_p(p, i, dpp, name=f"dw_ple_proj_{i}")
        dgg, dgu = _ffn_out_bwd(du2b, w["ffn_w_out"][i], s["g"], s["u"], name=f"ffn_out_bwd_{i}")
        gw["ffn_w_out"][i] = _mm_tn(s["h"], du2b, name=f"dw_ffn_out_{i}")
        du1, du1b, gs["ln_mix_g"][i], gs["ln_mix_b"][i] = _mm_nt(
            [dgg, dgu], w["ffn_w_in"][i], tk=_pick(D_FF, 1408), resid=du2,
            ln=(s["xh1"], s["rs1"], row(sp["ln_mix_g"], i)), name=f"ffn_in_bwd_{i}")
        gw["ffn_w_in"][i] = jnp.concatenate(
            [_mm_tn(s["x1b"], dgg, name=f"dw_ffn_gate_{i}"), _mm_tn(s["x1b"], dgu, name=f"dw_ffn_up_{i}")], axis=1)
        if i % 2 == 0:
            gw["att_w_o"][j] = _mm_tn(s["o"], du1b, name=f"dw_att_o_{i}")
            do = _mm_nt([du1b], w["att_w_o"][j], tk=Q_DIM, out_dtype=BF16, name=f"att_o_bwd_{i}")
            dq, dkw, dvw, gs["att_sink"][j] = _attn_bwd(
                s["q"], s["k"], s["v"], s["o"], do, s["lse"], sp["att_sink"][j], name=f"attn_bwd_{i}")
            dqkv = _attn_post_bwd(dq, dkw, dvw, tables, name=f"attn_post_bwd_{i}")
            gw["att_w_qkv"][j] = _mm_tn(s["xin_bf"], dqkv, name=f"dw_att_qkv_{i}")
            dx = _mm_nt([dqkv], w["att_w_qkv"][j], tk=Q_DIM + 2 * KV_DIM, resid=du1, name=f"qkv_bwd_{i}")
        else:
            gw["hgrn_w_o"][j] = _mm_tn(s["og"], du1b, name=f"dw_hgrn_o_{i}")
            dy = _mm_nt([du1b], w["hgrn_w_o"][j], tk=D_MODEL, out_dtype=BF16, name=f"hgrn_o_bwd_{i}")
            d_o, dgate, gs["hgrn_norm_g"][j] = _hgrn_post_bwd(
                dy, s["o_f"], s["o_b"], s["z"], row(sp["hgrn_norm_g"], j), name=f"hgrn_post_bwd_{i}")
            dq_f, dzf_f, dv_f, dlb_f = _hgrn_bwd(s["z"], lb_all[i, 0], d_o, s["s_f"], False, name=f"hgrn_scan_f_bwd_{i}")
            dq_b, dzf_b, dv_b, dlb_b = _hgrn_bwd(s["z"], lb_all[i, 1], d_o, s["s_b"], True, name=f"hgrn_scan_b_bwd_{i}")
            dlb[i] = jnp.stack([dlb_f.reshape(D_MODEL), dlb_b.reshape(D_MODEL)])
            dz = _hgrn_combine(dq_f, dq_b, dzf_f, dzf_b, dv_f, dv_b, dgate, name=f"hgrn_combine_{i}")
            gw["hgrn_w_in"][j] = _mm_tn(s["xin_bf"], dz, name=f"dw_hgrn_in_{i}")
            dx = _mm_nt([dz], w["hgrn_w_in"][j], tk=_pick(5 * D_MODEL, 1408), resid=du1, name=f"hgrn_in_bwd_{i}")

    gw = {n: jnp.stack(v) for n, v in gw.items()}
    gsmall = {n: jnp.concatenate(v, axis=0) for n, v in gs.items()}
    gsmall["hgrn_lb_logits"] = _lower_bounds_bwd(
        logits2d, jnp.stack(dlb).reshape(DEPTH, 2 * D_MODEL), name="lb_bwd").reshape(DEPTH, 2, D_MODEL)
    return loss, dx, gw, gsmall


ANY = pl.BlockSpec(memory_space=pl.ANY)


def _place():
    x, y, c = lax.axis_index("x"), lax.axis_index("y"), lax.axis_index("c")
    chips = [(1 - x, y), (x, 1 - y), (1 - x, 1 - y)]
    return x, y, c, chips


def _remote(src, dst, send_sem, recv_sem, to):
    return pltpu.make_async_remote_copy(src_ref=src, dst_ref=dst, send_sem=send_sem, recv_sem=recv_sem,
                                        device_id=to, device_id_type=MESH)


def _allgather_weights(packed, *, name):
    r = packed.shape[0]
    hr = r // 2

    def body(src_ref, out_ref, send_sems, recv_sems, local_sem):
        x, y, c, chips = _place()
        sibling = (x, y, 1 - c)

        def half(cx, cy, hc):
            return out_ref.at[2 * cx + cy, pl.ds(hc * hr, hr), :]

        mine = pltpu.make_async_copy(src_ref, out_ref.at[2 * x + y], local_sem)
        mine.start()
        my_half = src_ref.at[pl.ds(c * hr, hr), :]
        first = [_remote(my_half, half(x, y, c), send_sems.at[j], recv_sems.at[j], (*chip, c)) for j, chip in enumerate(chips)]
        for cp in first:
            cp.start()
        passed = [_remote(half(*chip, c), half(*chip, c), send_sems.at[3 + j], recv_sems.at[3 + j], sibling)
                  for j, chip in enumerate(chips)]
        for j, chip in enumerate(chips):
            _remote(my_half, half(*chip, c), send_sems.at[j], recv_sems.at[j], (*chip, c)).wait_recv()
            passed[j].start()
        for j, chip in enumerate(chips):
            _remote(my_half, half(*chip, 1 - c), send_sems.at[3 + j], recv_sems.at[3 + j], sibling).wait_recv()
        for cp in first + passed:
            cp.wait_send()
        mine.wait()

    return pl.pallas_call(
        body, in_specs=[ANY], out_specs=ANY, out_shape=jax.ShapeDtypeStruct((N_CHIPS, r, packed.shape[1]), packed.dtype),
        scratch_shapes=[pltpu.SemaphoreType.DMA((6,)), pltpu.SemaphoreType.DMA((6,)), pltpu.SemaphoreType.DMA],
        name=name)(packed)


def _allreduce_small(block, *, name):
    m, n = block.shape

    def body(x_ref, sum_ref, all_ref, send_sems, recv_sems):
        x, y, c, chips = _place()
        me, sibling = (x, y, c), (x, y, 1 - c)

        def rows(px, py, pc):
            return all_ref.at[pl.ds((4 * px + 2 * py + pc) * m, m), :]

        all_ref[pl.ds((4 * x + 2 * y + c) * m, m), :] = x_ref[...]
        first = [_remote(x_ref, rows(*me), send_sems.at[0], recv_sems.at[0], sibling)]
        first += [_remote(x_ref, rows(*me), send_sems.at[1 + j], recv_sems.at[1 + j], (*chip, c)) for j, chip in enumerate(chips)]
        for cp in first:
            cp.start()
        passed = [_remote(rows(*chip, c), rows(*chip, c), send_sems.at[4 + j], recv_sems.at[4 + j], sibling)
                  for j, chip in enumerate(chips)]
        for j, chip in enumerate(chips):
            _remote(x_ref, rows(*chip, c), send_sems.at[1 + j], recv_sems.at[1 + j], me).wait_recv()
            passed[j].start()
        _remote(x_ref, rows(*sibling), send_sems.at[0], recv_sems.at[0], me).wait_recv()
        for j, chip in enumerate(chips):
            _remote(x_ref, rows(*chip, 1 - c), send_sems.at[4 + j], recv_sems.at[4 + j], me).wait_recv()
        for cp in first + passed:
            cp.wait_send()
        acc = all_ref[pl.ds(0, m), :]
        for d in range(1, 8):
            acc = acc + all_ref[pl.ds(d * m, m), :]
        sum_ref[...] = acc

    vmem = pl.BlockSpec(memory_space=pltpu.VMEM)
    return pl.pallas_call(
        body, in_specs=[vmem], out_specs=vmem, out_shape=jax.ShapeDtypeStruct((m, n), F32),
        scratch_shapes=[pltpu.VMEM((8 * m, n), F32), pltpu.SemaphoreType.DMA((7,)), pltpu.SemaphoreType.DMA((7,))],
        name=name)(block)


def _pair_exchange(g, *, name):
    n, r, w = g.shape
    hr = r // 2

    def body(g_ref, out_ref, send_sem, recv_sem):
        x, y, c, _ = _place()
        cp = _remote(g_ref.at[:, pl.ds((1 - c) * hr, hr), :], out_ref, send_sem, recv_sem, (x, y, 1 - c))
        cp.start()
        cp.wait()

    return pl.pallas_call(
        body, in_specs=[ANY], out_specs=ANY, out_shape=jax.ShapeDtypeStruct((n, hr, w), g.dtype),
        scratch_shapes=[pltpu.SemaphoreType.DMA, pltpu.SemaphoreType.DMA], name=name)(g)


def _chip_scatter(part, *, name):
    n, h, w = part.shape

    def body(p_ref, out_ref, send_sems, recv_sems, local_sem):
        x, y, c, chips = _place()
        me = 2 * x + y
        mine = pltpu.make_async_copy(p_ref.at[me], out_ref.at[me], local_sem)
        mine.start()
        sends = [_remote(p_ref.at[2 * cx + cy], out_ref.at[me], send_sems.at[j], recv_sems.at[j], (cx, cy, c))
                 for j, (cx, cy) in enumerate(chips)]
        for cp in sends:
            cp.start()
        for j, (cx, cy) in enumerate(chips):
            _remote(p_ref.at[me], out_ref.at[2 * cx + cy], send_sems.at[j], recv_sems.at[j], (cx, cy, c)).wait_recv()
        for cp in sends:
            cp.wait_send()
        mine.wait()

    return pl.pallas_call(
        body, in_specs=[ANY], out_specs=ANY, out_shape=jax.ShapeDtypeStruct((n, h, w), part.dtype),
        scratch_shapes=[pltpu.SemaphoreType.DMA((3,)), pltpu.SemaphoreType.DMA((3,)), pltpu.SemaphoreType.DMA],
        name=name)(part)


def _pair_share(half, *, name):
    h, w = half.shape

    def body(h_ref, out_ref, send_sem, recv_sem, local_sem):
        x, y, c, _ = _place()
        dst = out_ref.at[pl.ds(c * h, h), :]
        mine = pltpu.make_async_copy(h_ref, dst, local_sem)
        mine.start()
        cp = _remote(h_ref, dst, send_sem, recv_sem, (x, y, 1 - c))
        cp.start()
        cp.wait_send()
        _remote(h_ref, out_ref.at[pl.ds((1 - c) * h, h), :], send_sem, recv_sem, (x, y, 1 - c)).wait_recv()
        mine.wait()

    return pl.pallas_call(
        body, in_specs=[ANY], out_specs=ANY, out_shape=jax.ShapeDtypeStruct((2 * h, w), half.dtype),
        scratch_shapes=[pltpu.SemaphoreType.DMA, pltpu.SemaphoreType.DMA, pltpu.SemaphoreType.DMA], name=name)(half)


def _add_own_half(g, recv, c, *, name):
    n, r, w = g.shape
    hr = r // 2
    tr = _pick_rows(hr, 1024)
    nr = hr // tr

    def body(c_ref, g_ref, r_ref, o_ref):
        o_ref[...] = (g_ref[...] + r_ref[...]).astype(BF16)

    return pl.pallas_call(
        body,
        grid_spec=pltpu.PrefetchScalarGridSpec(
            num_scalar_prefetch=1, grid=(n, nr),
            in_specs=[pl.BlockSpec((None, tr, w), lambda s, i, c_ref: (s, c_ref[0] * nr + i, 0)),
                      pl.BlockSpec((None, tr, w), lambda s, i, c_ref: (s, i, 0))],
            out_specs=pl.BlockSpec((None, tr, w), lambda s, i, c_ref: (s, i, 0))),
        out_shape=jax.ShapeDtypeStruct((n, hr, w), BF16),
        compiler_params=_cp("parallel", "parallel"), name=name)(c, g, recv)


def _sum_chips(q, *, name):
    n, h, w = q.shape
    tr = _pick_rows(h, 1024)

    def body(a, b, c, d, o_ref):
        o_ref[...] = ((a[...].astype(F32) + b[...].astype(F32)) + c[...].astype(F32)) + d[...].astype(F32)

    specs = [pl.BlockSpec((None, tr, w), functools.partial(lambda i, s: (s, i, 0), s=s)) for s in range(n)]
    return pl.pallas_call(
        body, grid=(h // tr,), in_specs=specs, out_specs=pl.BlockSpec((tr, w), lambda i: (i, 0)),
        out_shape=jax.ShapeDtypeStruct((h, w), F32), compiler_params=_cp("parallel"), name=name)(q, q, q, q)


PACK_W = 1024
BIG = (("att_w_qkv", 2), ("att_w_o", 1), ("hgrn_w_in", 2), ("hgrn_w_o", 1),
       ("ffn_w_in", 2), ("ffn_w_out", 1), ("ple_w_gate", 1), ("ple_w_proj", 2))
LB_ROWS = 32


def _pack_shards(shards):
    return jnp.concatenate([shards[n].reshape(-1, PACK_W) for n, _ in BIG], axis=0)


def _unpack_shards(buf, shapes):
    out, r0 = {}, 0
    for n, _ in BIG:
        nr = shapes[n][0] * shapes[n][1] * shapes[n][2] // PACK_W
        out[n] = buf[r0:r0 + nr].reshape(shapes[n])
        r0 += nr
    return out


def _gathered_to_full(buf, shapes):
    out, r0 = {}, 0
    for n, axis in BIG:
        l, r, c = shapes[n]
        nr = l * r * c // PACK_W
        sh = buf[:, r0:r0 + nr].reshape(N_CHIPS, l, r, c)
        out[n] = (sh.transpose(1, 0, 2, 3).reshape(l, N_CHIPS * r, c) if axis == 1
                  else sh.transpose(1, 2, 0, 3).reshape(l, r, N_CHIPS * c))
        r0 += nr
    return out, r0


def _full_to_slabs(full, shapes):
    parts = []
    for n, axis in BIG:
        l, r, c = shapes[n]
        g = full[n]
        g = (g.reshape(l, N_CHIPS, r, c).transpose(1, 0, 2, 3) if axis == 1
             else g.reshape(l, r, N_CHIPS, c).transpose(2, 0, 1, 3))
        parts.append(g.reshape(N_CHIPS, -1, PACK_W))
    return jnp.concatenate(parts, axis=1)


SMALL = ("ln_mix_g", "ln_mix_b", "ln_ffn_g", "ln_ffn_b")
SMALL_ROWS = 32


def _pack_small(vals, lb):
    rows = [vals[n] for n in SMALL] + [lb.reshape(-1, PACK_W)]
    for n in ("att_sink", "hgrn_norm_g"):
        flat = vals[n].reshape(1, -1)
        rows.append(jnp.pad(flat, ((0, 0), (0, PACK_W - flat.shape[1]))))
    buf = jnp.concatenate(rows, axis=0)
    return jnp.pad(buf, ((0, SMALL_ROWS - buf.shape[0]), (0, 0)))


def _unpack_small(buf, lb_shape):
    out, r0 = {}, 0
    for n in SMALL:
        out[n] = buf[r0:r0 + DEPTH]
        r0 += DEPTH
    nlb = lb_shape[0] * lb_shape[1] * lb_shape[2] // PACK_W
    out["hgrn_lb_logits"] = buf[r0:r0 + nlb].reshape(lb_shape)
    r0 += nlb
    out["att_sink"] = buf[r0, :2 * N_Q_HEADS].reshape(2, N_Q_HEADS)
    out["hgrn_norm_g"] = buf[r0 + 1, :2 * LANES].reshape(2, LANES)
    return out


WEIGHTS = ("att_w_qkv", "att_sink", "att_w_o", "hgrn_w_in", "hgrn_lb_logits", "hgrn_norm_g", "hgrn_w_o", "ln_mix_g",
           "ln_mix_b", "ffn_w_in", "ffn_w_out", "ln_ffn_g", "ln_ffn_b", "ple_w_gate", "ple_w_proj")


def kernel(x, p, att_w_qkv, att_sink, att_w_o, hgrn_w_in, hgrn_lb_logits, hgrn_norm_g, hgrn_w_o, ln_mix_g, ln_mix_b, ffn_w_in, ffn_w_out, ln_ffn_g, ln_ffn_b, ple_w_gate, ple_w_proj, loss_target, m_att_w_qkv, m_att_sink, m_att_w_o, m_hgrn_w_in, m_hgrn_lb_logits, m_hgrn_norm_g, m_hgrn_w_o, m_ln_mix_g, m_ln_mix_b, m_ffn_w_in, m_ffn_w_out, m_ln_ffn_g, m_ln_ffn_b, m_ple_w_gate, m_ple_w_proj, v_att_w_qkv, v_att_sink, v_att_w_o, v_hgrn_w_in, v_hgrn_lb_logits, v_hgrn_norm_g, v_hgrn_w_o, v_ln_mix_g, v_ln_mix_b, v_ffn_w_in, v_ffn_w_out, v_ln_ffn_g, v_ln_ffn_b, v_ple_w_gate, v_ple_w_proj):
    wts = dict(att_w_qkv=att_w_qkv, att_sink=att_sink, att_w_o=att_w_o, hgrn_w_in=hgrn_w_in, hgrn_lb_logits=hgrn_lb_logits,
               hgrn_norm_g=hgrn_norm_g, hgrn_w_o=hgrn_w_o, ln_mix_g=ln_mix_g, ln_mix_b=ln_mix_b, ffn_w_in=ffn_w_in,
               ffn_w_out=ffn_w_out, ln_ffn_g=ln_ffn_g, ln_ffn_b=ln_ffn_b, ple_w_gate=ple_w_gate, ple_w_proj=ple_w_proj)
    mom = dict(att_w_qkv=m_att_w_qkv, att_sink=m_att_sink, att_w_o=m_att_w_o, hgrn_w_in=m_hgrn_w_in, hgrn_lb_logits=m_hgrn_lb_logits,
               hgrn_norm_g=m_hgrn_norm_g, hgrn_w_o=m_hgrn_w_o, ln_mix_g=m_ln_mix_g, ln_mix_b=m_ln_mix_b, ffn_w_in=m_ffn_w_in,
               ffn_w_out=m_ffn_w_out, ln_ffn_g=m_ln_ffn_g, ln_ffn_b=m_ln_ffn_b, ple_w_gate=m_ple_w_gate, ple_w_proj=m_ple_w_proj)
    var = dict(att_w_qkv=v_att_w_qkv, att_sink=v_att_sink, att_w_o=v_att_w_o, hgrn_w_in=v_hgrn_w_in, hgrn_lb_logits=v_hgrn_lb_logits,
               hgrn_norm_g=v_hgrn_norm_g, hgrn_w_o=v_hgrn_w_o, ln_mix_g=v_ln_mix_g, ln_mix_b=v_ln_mix_b, ffn_w_in=v_ffn_w_in,
               ffn_w_out=v_ffn_w_out, ln_ffn_g=v_ln_ffn_g, ln_ffn_b=v_ln_ffn_b, ple_w_gate=v_ple_w_gate, ple_w_proj=v_ple_w_proj)
    shapes = {n: wts[n].shape for n, _ in BIG}
    chip = 2 * lax.axis_index("x") + lax.axis_index("y")
    core = lax.axis_index("c")

    lb_bits = lax.bitcast_convert_type(hgrn_lb_logits.reshape(-1), BF16).reshape(-1, PACK_W)
    packed = jnp.concatenate([_pack_shards({n: wts[n].astype(BF16) for n, _ in BIG}), lb_bits,
                              jnp.zeros((LB_ROWS - lb_bits.shape[0], PACK_W), BF16)], axis=0)
    gathered = _allgather_weights(packed, name="allgather_weights")
    w_full, r_big = _gathered_to_full(gathered, shapes)
    lb_sh = hgrn_lb_logits.shape
    lb_rows = gathered[:, r_big:r_big + lb_bits.shape[0]].reshape(N_CHIPS, -1, 2)
    lb_full = lax.bitcast_convert_type(lb_rows, F32).reshape(N_CHIPS, lb_sh[0], lb_sh[1], lb_sh[2])
    lb_full = lb_full.transpose(1, 2, 0, 3).reshape(lb_sh[0], lb_sh[1], N_CHIPS * lb_sh[2])
    sp = dict(att_sink=att_sink, hgrn_lb_logits=lb_full, hgrn_norm_g=hgrn_norm_g, ln_mix_g=ln_mix_g, ln_mix_b=ln_mix_b,
              ln_ffn_g=ln_ffn_g, ln_ffn_b=ln_ffn_b)

    loss, grad_x, gw, gs = _local_step(x[0], p, loss_target[0], w_full, sp)
    loss = lax.psum(loss[0, 0], ("x", "y", "c"))

    slabs = _full_to_slabs(gw, shapes)
    from_sibling = _pair_exchange(slabs, name="grad_pair_exchange")
    chip_part = _add_own_half(slabs, from_sibling, core.reshape(1).astype(jnp.int32), name="grad_pair_add")
    from_chips = _chip_scatter(chip_part, name="grad_chip_scatter")
    half = _sum_chips(from_chips, name="grad_chip_sum")
    g_big = _unpack_shards(_pair_share(half, name="grad_pair_share"), shapes)

    g_small_full = _unpack_small(_allreduce_small(_pack_small(gs, gs["hgrn_lb_logits"]), name="allreduce_small"),
                                 (lb_sh[0], lb_sh[1], N_CHIPS * lb_sh[2]))
    g_lb = lax.dynamic_slice_in_dim(g_small_full["hgrn_lb_logits"], chip * lb_sh[2], lb_sh[2], axis=2)
    grads = dict(g_big)
    grads.update({n: g_small_full[n] for n in SMALL + ("att_sink", "hgrn_norm_g")})
    grads["hgrn_lb_logits"] = g_lb

    delta, new_m, new_v = {}, {}, {}
    for n, _ in BIG:
        two_d = lambda a: a.reshape(-1, a.shape[-1])
        d, nm, nv = _adamw(two_d(wts[n]), two_d(grads[n]), two_d(mom[n]), two_d(var[n]), name=f"adamw_{n}")
        delta[n], new_m[n], new_v[n] = d.reshape(shapes[n]), nm.reshape(shapes[n]), nv.reshape(shapes[n])
    packs = [_pack_small(src, src["hgrn_lb_logits"]) for src in (wts, grads, mom, var)]
    outs = _adamw(*packs, name="adamw_small")
    for dst, buf in zip((delta, new_m, new_v), outs):
        dst.update(_unpack_small(buf, lb_sh))

    return (loss, grad_x[None], *[grads[n] for n in WEIGHTS], *[delta[n] for n in WEIGHTS],
            *[new_m[n] for n in WEIGHTS], *[new_v[n] for n in WEIGHTS])


def _gather_guest(packed):
    r, w = packed.shape
    hr = r // 2

    def copies(src_ref, out_ref, send_sems, recv_sems, local_sem):
        x, y, c, chips = _place()
        sibling = (x, y, 1 - c)

        def half(cx, cy, hc):
            return out_ref.at[2 * cx + cy, pl.ds(hc * hr, hr), :]

        my_half = src_ref.at[pl.ds(c * hr, hr), :]
        mine = pltpu.make_async_copy(src_ref, out_ref.at[2 * x + y], local_sem)
        first = [_remote(my_half, half(x, y, c), send_sems.at[j], recv_sems.at[j], (*chip, c)) for j, chip in enumerate(chips)]
        passed = [_remote(half(*chip, c), half(*chip, c), send_sems.at[3 + j], recv_sems.at[3 + j], sibling)
                  for j, chip in enumerate(chips)]
        from_chips = [_remote(my_half, half(*chip, c), send_sems.at[j], recv_sems.at[j], (*chip, c)) for j, chip in enumerate(chips)]
        from_sibling = [_remote(my_half, half(*chip, 1 - c), send_sems.at[3 + j], recv_sems.at[3 + j], sibling)
                        for j, chip in enumerate(chips)]
        return mine, first, passed, from_chips, from_sibling

    def start(gi, go, gs):
        mine, first, _, _, _ = copies(gi[0], go[0], *gs)
        mine.start()
        for cp in first:
            cp.start()

    def finish(gi, go, gs):
        mine, first, passed, from_chips, from_sibling = copies(gi[0], go[0], *gs)
        for arrival, onward in zip(from_chips, passed):
            arrival.wait_recv()
            onward.start()
        for arrival in from_sibling:
            arrival.wait_recv()
        for cp in first + passed:
            cp.wait_send()
        mine.wait()

    return _Guest((packed,), (jax.ShapeDtypeStruct((N_CHIPS, r, w), packed.dtype),),
                  (pltpu.SemaphoreType.DMA((6,)), pltpu.SemaphoreType.DMA((6,)), pltpu.SemaphoreType.DMA), start, finish)


def _pair_exchange_guest(g):
    n, r, w = g.shape
    hr = r // 2

    def copy(g_ref, out_ref, send_sem, recv_sem):
        x, y, c, _ = _place()
        return _remote(g_ref.at[:, pl.ds((1 - c) * hr, hr), :], out_ref, send_sem, recv_sem, (x, y, 1 - c))

    return _Guest((g,), (jax.ShapeDtypeStruct((n, hr, w), g.dtype),), (pltpu.SemaphoreType.DMA, pltpu.SemaphoreType.DMA),
                  lambda gi, go, gs: copy(gi[0], go[0], *gs).start(),
                  lambda gi, go, gs: copy(gi[0], go[0], *gs).wait())


def _chip_scatter_guest(part):
    n, h, w = part.shape

    def copies(p_ref, out_ref, send_sems, recv_sems, local_sem):
        x, y, c, chips = _place()
        me = 2 * x + y
        mine = pltpu.make_async_copy(p_ref.at[me], out_ref.at[me], local_sem)
        sends = [_remote(p_ref.at[2 * cx + cy], out_ref.at[me], send_sems.at[j], recv_sems.at[j], (cx, cy, c))
                 for j, (cx, cy) in enumerate(chips)]
        arrivals = [_remote(p_ref.at[me], out_ref.at[2 * cx + cy], send_sems.at[j], recv_sems.at[j], (cx, cy, c))
                    for j, (cx, cy) in enumerate(chips)]
        return mine, sends, arrivals

    def start(gi, go, gs):
        mine, sends, _ = copies(gi[0], go[0], *gs)
        mine.start()
        for cp in sends:
            cp.start()

    def finish(gi, go, gs):
        mine, sends, arrivals = copies(gi[0], go[0], *gs)
        for cp in arrivals:
            cp.wait_recv()
        for cp in sends:
            cp.wait_send()
        mine.wait()

    return _Guest((part,), (jax.ShapeDtypeStruct((n, h, w), part.dtype),),
                  (pltpu.SemaphoreType.DMA((3,)), pltpu.SemaphoreType.DMA((3,)), pltpu.SemaphoreType.DMA), start, finish)


def _pair_share_guest(half):
    h, w = half.shape

    def copies(h_ref, out_ref, send_sem, recv_sem, local_sem):
        x, y, c, _ = _place()
        dst = out_ref.at[pl.ds(c * h, h), :]
        mine = pltpu.make_async_copy(h_ref, dst, local_sem)
        send = _remote(h_ref, dst, send_sem, recv_sem, (x, y, 1 - c))
        arrival = _remote(h_ref, out_ref.at[pl.ds((1 - c) * h, h), :], send_sem, recv_sem, (x, y, 1 - c))
        return mine, send, arrival

    def start(gi, go, gs):
        mine, send, _ = copies(gi[0], go[0], *gs)
        mine.start()
        send.start()

    def finish(gi, go, gs):
        mine, send, arrival = copies(gi[0], go[0], *gs)
        send.wait_send()
        arrival.wait_recv()
        mine.wait()

    return _Guest((half,), (jax.ShapeDtypeStruct((2 * h, w), half.dtype),),
                  (pltpu.SemaphoreType.DMA, pltpu.SemaphoreType.DMA, pltpu.SemaphoreType.DMA), start, finish)


AXIS = dict(BIG)


def _layer_parts(i):
    j = i // 2
    mixer = (("att_w_qkv", j), ("att_w_o", j)) if i % 2 == 0 else (("hgrn_w_in", j), ("hgrn_w_o", j))
    return mixer + (("ffn_w_in", i), ("ffn_w_out", i), ("ple_w_gate", i), ("ple_w_proj", i))


def _pack_layer(shards, i):
    return jnp.concatenate([shards[n][l].reshape(-1, PACK_W) for n, l in _layer_parts(i)], axis=0)


def _unpack_layer(buf, i, shapes):
    out, r0 = {}, 0
    for n, _ in _layer_parts(i):
        r, c = shapes[n][1:]
        nr = r * c // PACK_W
        out[n] = buf[r0:r0 + nr].reshape(r, c)
        r0 += nr
    return out


def _gathered_layer(buf, i, shapes):
    out, r0 = {}, 0
    for n, _ in _layer_parts(i):
        r, c = shapes[n][1:]
        nr = r * c // PACK_W
        sh = buf[:, r0:r0 + nr].reshape(N_CHIPS, r, c)
        out[n] = sh.reshape(N_CHIPS * r, c) if AXIS[n] == 1 else sh.transpose(1, 0, 2).reshape(r, N_CHIPS * c)
        r0 += nr
    return out, r0


def _layer_slabs(full, i, shapes):
    parts = []
    for n, _ in _layer_parts(i):
        r, c = shapes[n][1:]
        g = full[n]
        g = g.reshape(N_CHIPS, -1, PACK_W) if AXIS[n] == 1 else g.reshape(r, N_CHIPS, c).transpose(1, 0, 2).reshape(N_CHIPS, -1, PACK_W)
        parts.append(g)
    return jnp.concatenate(parts, axis=1)


def kernel(x, p, att_w_qkv, att_sink, att_w_o, hgrn_w_in, hgrn_lb_logits, hgrn_norm_g, hgrn_w_o, ln_mix_g, ln_mix_b, ffn_w_in, ffn_w_out, ln_ffn_g, ln_ffn_b, ple_w_gate, ple_w_proj, loss_target, m_att_w_qkv, m_att_sink, m_att_w_o, m_hgrn_w_in, m_hgrn_lb_logits, m_hgrn_norm_g, m_hgrn_w_o, m_ln_mix_g, m_ln_mix_b, m_ffn_w_in, m_ffn_w_out, m_ln_ffn_g, m_ln_ffn_b, m_ple_w_gate, m_ple_w_proj, v_att_w_qkv, v_att_sink, v_att_w_o, v_hgrn_w_in, v_hgrn_lb_logits, v_hgrn_norm_g, v_hgrn_w_o, v_ln_mix_g, v_ln_mix_b, v_ffn_w_in, v_ffn_w_out, v_ln_ffn_g, v_ln_ffn_b, v_ple_w_gate, v_ple_w_proj):
    wts = dict(att_w_qkv=att_w_qkv, att_sink=att_sink, att_w_o=att_w_o, hgrn_w_in=hgrn_w_in, hgrn_lb_logits=hgrn_lb_logits,
               hgrn_norm_g=hgrn_norm_g, hgrn_w_o=hgrn_w_o, ln_mix_g=ln_mix_g, ln_mix_b=ln_mix_b, ffn_w_in=ffn_w_in,
               ffn_w_out=ffn_w_out, ln_ffn_g=ln_ffn_g, ln_ffn_b=ln_ffn_b, ple_w_gate=ple_w_gate, ple_w_proj=ple_w_proj)
    mom = dict(att_w_qkv=m_att_w_qkv, att_sink=m_att_sink, att_w_o=m_att_w_o, hgrn_w_in=m_hgrn_w_in, hgrn_lb_logits=m_hgrn_lb_logits,
               hgrn_norm_g=m_hgrn_norm_g, hgrn_w_o=m_hgrn_w_o, ln_mix_g=m_ln_mix_g, ln_mix_b=m_ln_mix_b, ffn_w_in=m_ffn_w_in,
               ffn_w_out=m_ffn_w_out, ln_ffn_g=m_ln_ffn_g, ln_ffn_b=m_ln_ffn_b, ple_w_gate=m_ple_w_gate, ple_w_proj=m_ple_w_proj)
    var = dict(att_w_qkv=v_att_w_qkv, att_sink=v_att_sink, att_w_o=v_att_w_o, hgrn_w_in=v_hgrn_w_in, hgrn_lb_logits=v_hgrn_lb_logits,
               hgrn_norm_g=v_hgrn_norm_g, hgrn_w_o=v_hgrn_w_o, ln_mix_g=v_ln_mix_g, ln_mix_b=v_ln_mix_b, ffn_w_in=v_ffn_w_in,
               ffn_w_out=v_ffn_w_out, ln_ffn_g=v_ln_ffn_g, ln_ffn_b=v_ln_ffn_b, ple_w_gate=v_ple_w_gate, ple_w_proj=v_ple_w_proj)
    shapes = {n: wts[n].shape for n, _ in BIG}
    chip = 2 * lax.axis_index("x") + lax.axis_index("y")
    core = lax.axis_index("c").reshape(1).astype(jnp.int32)
    xin, target = x[0], loss_target[0]
    t = xin.shape[0]
    row = lambda a, i: a[i][None]

    bf_shards = {n: wts[n].astype(BF16) for n, _ in BIG}
    lb_sh = hgrn_lb_logits.shape
    lb_bits = lax.bitcast_convert_type(hgrn_lb_logits.reshape(-1), BF16).reshape(-1, PACK_W)
    packed = [_pack_layer(bf_shards, i) for i in range(DEPTH)]
    packed[0] = jnp.concatenate([packed[0], lb_bits, jnp.zeros((LB_ROWS - lb_bits.shape[0], PACK_W), BF16)], axis=0)

    gathered = _run_guest(_gather_guest(packed[0]), name="gather_layer_0")[0]
    w, r_big = _gathered_layer(gathered, 0, shapes)
    lb_rows = gathered[:, r_big:r_big + lb_bits.shape[0]].reshape(N_CHIPS, -1, 2)
    lb_full = lax.bitcast_convert_type(lb_rows, F32).reshape(N_CHIPS, lb_sh[0], lb_sh[1], lb_sh[2])
    lb_full = lb_full.transpose(1, 2, 0, 3).reshape(lb_sh[0], lb_sh[1], N_CHIPS * lb_sh[2])
    logits2d = lb_full.reshape(DEPTH, 2 * D_MODEL)
    lb_all = _lower_bounds(logits2d, name="lb_fwd").reshape(DEPTH, 2, 1, D_MODEL)
    tables = _rope_tables(t)

    saved, weights = [], []
    xf, xb = xin, xin.astype(BF16)
    for i in range(DEPTH):
        j = i // 2
        nxt = _gather_guest(packed[i + 1]) if i + 1 < DEPTH else None
        s = dict(xin_bf=xb)
        if i % 2 == 0:
            q, k, v = _qkv_rope(xb, w["att_w_qkv"], tables, name=f"qkv_rope_{i}")
            (o, lse), got = _attn_fwd(q, k, v, att_sink[j], name=f"attn_fwd_{i}", guest=nxt)
            s.update(q=q, k=k, v=v, o=o, lse=lse)
            mix_in, w_o = o, w["att_w_o"]
        else:
            z = _mm_nn(xb, w["hgrn_w_in"], out_dtype=F32, name=f"hgrn_in_{i}")
            (o_f, s_f), got = _hgrn_fwd(z, lb_all[i, 0], False, name=f"hgrn_scan_f_{i}", guest=nxt)
            (o_b, s_b), _ = _hgrn_fwd(z, lb_all[i, 1], True, name=f"hgrn_scan_b_{i}")
            og = _hgrn_post_fwd(o_f, o_b, z, row(hgrn_norm_g, j), name=f"hgrn_post_{i}")
            s.update(z=z, o_f=o_f, o_b=o_b, s_f=s_f, s_b=s_b, og=og)
            mix_in, w_o = og, w["hgrn_w_o"]
        x1, x1b, xh1, rs1 = _mm_res_ln(mix_in, w_o, xf, row(ln_mix_g, i), row(ln_mix_b, i), name=f"mix_out_ln_{i}")
        g, u, h = _ffn_in(x1b, w["ffn_w_in"], name=f"ffn_in_{i}")
        x2, x2b, xh2, rs2 = _mm_res_ln(h, w["ffn_w_out"], x1, row(ln_ffn_g, i), row(ln_ffn_b, i), name=f"ffn_out_ln_{i}")
        xf, xb, gate, pp = _ple_fwd(x2, x2b, p, i, w["ple_w_gate"], w["ple_w_proj"], name=f"ple_fwd_{i}")
        s.update(x1b=x1b, xh1=xh1, rs1=rs1, g=g, u=u, h=h, x2b=x2b, xh2=xh2, rs2=rs2, gate=gate, pp=pp)
        saved.append(s)
        weights.append(w)
        if nxt is not None:
            w, _ = _gathered_layer(got[0], i + 1, shapes)

    loss, dx = _loss_head(xf, target, name="loss_head")
    loss = lax.psum(loss[0, 0], ("x", "y", "c"))

    gs = {n: [None] * DEPTH for n in SMALL}
    gs.update(att_sink=[None] * 2, hgrn_norm_g=[None] * 2)
    dlb = [jnp.zeros((2, D_MODEL), F32)] * DEPTH
    reduced = [None] * DEPTH
    slabs = None
    for i in reversed(range(DEPTH)):
        j = i // 2
        s, w = saved[i], weights[i]
        gw = {}
        res, got = _ple_bwd(dx, s["gate"], s["pp"], w["ple_w_gate"], s["xh2"], s["rs2"], row(ln_ffn_g, i), name=f"ple_bwd_{i}",
                            guest=None if slabs is None else _pair_exchange_guest(slabs))
        du2, du2b, dpre, dpp, gs["ln_ffn_g"][i], gs["ln_ffn_b"][i] = res
        part = None if slabs is None else _add_own_half(slabs, got[0], core, name=f"grad_pair_add_{i + 1}")
        gw["ple_w_gate"] = _mm_tn(s["x2b"], dpre, name=f"dw_ple_gate_{i}")
        gw["ple_w_proj"] = _mm_tn_p(p, i, dpp, name=f"dw_ple_proj_{i}")
        dgg, dgu = _ffn_out_bwd(du2b, w["ffn_w_out"], s["g"], s["u"], name=f"ffn_out_bwd_{i}")
        gw["ffn_w_out"] = _mm_tn(s["h"], du2b, name=f"dw_ffn_out_{i}")
        res = _mm_nt([dgg, dgu], w["ffn_w_in"], resid=du2, ln=(s["xh1"], s["rs1"], row(ln_mix_g, i)),
                     name=f"ffn_in_bwd_{i}", guest=None if part is None else _chip_scatter_guest(part))
        (du1, du1b, gs["ln_mix_g"][i], gs["ln_mix_b"][i]), got = res if part is not None else (res, None)
        half = None if part is None else _sum_chips(got[0], name=f"grad_chip_sum_{i + 1}")
        res = _mm_tn(s["x1b"], dgg, name=f"dw_ffn_gate_{i}", guest=None if half is None else _pair_share_guest(half))
        dw_gate, got = res if half is not None else (res, None)
        if half is not None:
            reduced[i + 1] = got[0]
        gw["ffn_w_in"] = jnp.concatenate([dw_gate, _mm_tn(s["x1b"], dgu, name=f"dw_ffn_up_{i}")], axis=1)
        if i % 2 == 0:
            gw["att_w_o"] = _mm_tn(s["o"], du1b, name=f"dw_att_o_{i}")
            do = _mm_nt([du1b], w["att_w_o"], out_dtype=BF16, name=f"att_o_bwd_{i}")
            dq, dkw, dvw, gs["att_sink"][j] = _attn_bwd(s["q"], s["k"], s["v"], s["o"], do, s["lse"], att_sink[j], name=f"attn_bwd_{i}")
            dqkv = _attn_post_bwd(dq, dkw, dvw, tables, name=f"attn_post_bwd_{i}")
            gw["att_w_qkv"] = _mm_tn(s["xin_bf"], dqkv, name=f"dw_att_qkv_{i}")
            dx = _mm_nt([dqkv], w["att_w_qkv"], resid=du1, name=f"qkv_bwd_{i}")
        else:
            gw["hgrn_w_o"] = _mm_tn(s["og"], du1b, name=f"dw_hgrn_o_{i}")
            dy = _mm_nt([du1b], w["hgrn_w_o"], out_dtype=BF16, name=f"hgrn_o_bwd_{i}")
            d_o, dgate, gs["hgrn_norm_g"][j] = _hgrn_post_bwd(dy, s["o_f"], s["o_b"], s["z"], row(hgrn_norm_g, j), name=f"hgrn_post_bwd_{i}")
            dq_f, dzf_f, dv_f, dlb_f = _hgrn_bwd(s["z"], lb_all[i, 0], d_o, s["s_f"], False, name=f"hgrn_scan_f_bwd_{i}")
            dq_b, dzf_b, dv_b, dlb_b = _hgrn_bwd(s["z"], lb_all[i, 1], d_o, s["s_b"], True, name=f"hgrn_scan_b_bwd_{i}")
            dlb[i] = jnp.stack([dlb_f.reshape(D_MODEL), dlb_b.reshape(D_MODEL)])
            dz = _hgrn_combine(dq_f, dq_b, dzf_f, dzf_b, dv_f, dv_b, dgate, name=f"hgrn_combine_{i}")
            gw["hgrn_w_in"] = _mm_tn(s["xin_bf"], dz, name=f"dw_hgrn_in_{i}")
            dx = _mm_nt([dz], w["hgrn_w_in"], resid=du1, name=f"hgrn_in_bwd_{i}")
        slabs = _layer_slabs(gw, i, shapes)

    from_sibling = _run_guest(_pair_exchange_guest(slabs), name="grad_pair_exchange_0")[0]
    part = _add_own_half(slabs, from_sibling, core, name="grad_pair_add_0")
    half = _sum_chips(_run_guest(_chip_scatter_guest(part), name="grad_chip_scatter_0")[0], name="grad_chip_sum_0")
    reduced[0] = _run_guest(_pair_share_guest(half), name="grad_pair_share_0")[0]

    g_layers = [_unpack_layer(reduced[i], i, shapes) for i in range(DEPTH)]
    grads = {}
    for n, _ in BIG:
        per_layer = [g_layers[i][n] for i in range(DEPTH) if n in g_layers[i]]
        grads[n] = jnp.stack(per_layer)

    gsmall = {n: jnp.concatenate(v, axis=0) for n, v in gs.items()}
    dlogits = _lower_bounds_bwd(logits2d, jnp.stack(dlb).reshape(DEPTH, 2 * D_MODEL), name="lb_bwd").reshape(DEPTH, 2, D_MODEL)
    g_small_full = _unpack_small(_allreduce_small(_pack_small(gsmall, dlogits), name="allreduce_small"),
                                 (lb_sh[0], lb_sh[1], N_CHIPS * lb_sh[2]))
    grads.update({n: g_small_full[n] for n in SMALL + ("att_sink", "hgrn_norm_g")})
    grads["hgrn_lb_logits"] = lax.dynamic_slice_in_dim(g_small_full["hgrn_lb_logits"], chip * lb_sh[2], lb_sh[2], axis=2)

    delta, new_m, new_v = {}, {}, {}
    for n, _ in BIG:
        two_d = lambda a: a.reshape(-1, a.shape[-1])
        d, nm, nv = _adamw(two_d(wts[n]), two_d(grads[n]), two_d(mom[n]), two_d(var[n]), name=f"adamw_{n}")
        delta[n], new_m[n], new_v[n] = d.reshape(shapes[n]), nm.reshape(shapes[n]), nv.reshape(shapes[n])
    packs = [_pack_small(src, src["hgrn_lb_logits"]) for src in (wts, grads, mom, var)]
    outs = _adamw(*packs, name="adamw_small")
    for dst, buf in zip((delta, new_m, new_v), outs):
        dst.update(_unpack_small(buf, lb_sh))

    return (loss, dx[None], *[grads[n] for n in WEIGHTS], *[delta[n] for n in WEIGHTS],
            *[new_m[n] for n in WEIGHTS], *[new_v[n] for n in WEIGHTS])
```

```python
import functools
from typing import Callable, NamedTuple

import jax
import jax.numpy as jnp
from jax import lax
from jax.experimental import pallas as pl
from jax.experimental.pallas import tpu as pltpu

F32, BF16 = jnp.float32, jnp.bfloat16

D_MODEL = 1024
DEPTH = 4
HEAD_DIM = 64
N_Q_HEADS = 16
N_KV_HEADS = 4
GROUP = 4
Q_DIM = 1024
KV_DIM = 256
WINDOW = 128
ROPE_DIM = 16
ROPE_THETA = 500000.0
HGRN_HEADS = 8
HGRN_KEY = 128
HGRN_CHUNK = 64
D_FF = 2816
PLE_DIM = 256
ALPHA = (2 * DEPTH) ** 0.25
LN_EPS = 1e-5
ADAM_LR, ADAM_B1, ADAM_B2, ADAM_EPS, ADAM_WD, ADAM_STEP = 0.001, 0.9, 0.999, 1e-08, 0.01, 10

LANES = 128
VMEM_LIMIT_BYTES = 56 * 2**20
FACTORED_DECAY_LIMIT = -80.0

NN = ((1,), (0,))
NT = ((1,), (1,))
TN = ((0,), (0,))

N_CHIPS = 4
MESH = pl.DeviceIdType.MESH


def _dot(a, b, dims):
    return lax.dot_general(a, b, (dims, ((), ())), preferred_element_type=F32)


def _cp(*sem):
    return pltpu.CompilerParams(dimension_semantics=sem, vmem_limit_bytes=VMEM_LIMIT_BYTES)


def _rows(t, cap=512):
    return min(cap, t)


def _pick(n, cap):
    best = None
    for d in range(LANES, min(n, cap) + 1, LANES):
        if n % d == 0:
            best = d
    assert best is not None, (n, cap)
    return best


def _sigmoid(x):
    return jax.nn.sigmoid(x)


def _ln_fwd(u, g, b):
    mu = jnp.mean(u, axis=-1, keepdims=True)
    xc = u - mu
    var = jnp.mean(xc * xc, axis=-1, keepdims=True)
    rstd = lax.rsqrt(var + LN_EPS)
    xhat = xc * rstd
    return xhat * g + b, xhat, rstd


def _ln_bwd(dy, xhat, rstd, g):
    dxh = dy * g
    m1 = jnp.mean(dxh, axis=-1, keepdims=True)
    m2 = jnp.mean(dxh * xhat, axis=-1, keepdims=True)
    du = rstd * (dxh - m1 - xhat * m2)
    return du, jnp.sum(dy * xhat, axis=0, keepdims=True), jnp.sum(dy, axis=0, keepdims=True)


def _full(shape):
    nd = len(shape)
    return pl.BlockSpec(shape, lambda *_: (0,) * nd)


ANY = pl.BlockSpec(memory_space=pl.ANY)


class _Guest(NamedTuple):
    args: tuple
    out_shape: tuple
    sems: tuple
    start: Callable
    finish: Callable


def _call(body, *, grid, in_specs, out_specs, out_shape, args, sem, name, scratch_shapes=(), guest=None):
    n_in, n_out, n_scr = len(args), len(out_shape), len(scratch_shapes)
    if guest is None:
        res = pl.pallas_call(body, grid=grid, in_specs=list(in_specs), out_specs=list(out_specs), out_shape=list(out_shape),
                             scratch_shapes=list(scratch_shapes), compiler_params=_cp(*sem), name=name)(*args)
        return list(res), []
    g_in, g_out = len(guest.args), len(guest.out_shape)

    def hosted(*refs):
        cuts = [n_in, g_in, n_out, g_out, n_scr]
        parts, pos = [], 0
        for n in cuts:
            parts.append(refs[pos:pos + n])
            pos += n
        h_in, gi, h_out, go, h_scr = parts
        gs = refs[pos:]
        ids = [pl.program_id(d) for d in range(len(grid))]
        first = functools.reduce(jnp.logical_and, [i == 0 for i in ids])
        last = functools.reduce(jnp.logical_and, [i == n - 1 for i, n in zip(ids, grid)])

        @pl.when(first)
        def _():
            guest.start(gi, go, gs)
        body(*h_in, *h_out, *h_scr)

        @pl.when(last)
        def _():
            guest.finish(gi, go, gs)

    res = pl.pallas_call(
        hosted, grid=grid, in_specs=list(in_specs) + [ANY] * g_in, out_specs=list(out_specs) + [ANY] * g_out,
        out_shape=list(out_shape) + list(guest.out_shape), scratch_shapes=list(scratch_shapes) + list(guest.sems),
        compiler_params=_cp(*(("arbitrary",) * len(grid))), name=name)(*args, *guest.args)
    return list(res[:n_out]), list(res[n_out:])


def _run_guest(guest, *, name):
    g_in = len(guest.args)

    def body(*refs):
        gi, go, gs = refs[:g_in], refs[g_in:g_in + len(guest.out_shape)], refs[g_in + len(guest.out_shape):]
        guest.start(gi, go, gs)
        guest.finish(gi, go, gs)

    return pl.pallas_call(body, in_specs=[ANY] * g_in, out_specs=[ANY] * len(guest.out_shape), out_shape=list(guest.out_shape),
                          scratch_shapes=list(guest.sems), name=name)(*guest.args)


def _mm_nn(a, w, *, out_dtype, name):
    t, k = a.shape
    n = w.shape[1]
    tm, tn = _rows(t), _pick(n, 1408)

    def body(a_ref, w_ref, o_ref):
        o_ref[...] = _dot(a_ref[...], w_ref[...], NN).astype(out_dtype)

    return pl.pallas_call(
        body, grid=(n // tn, t // tm),
        in_specs=[pl.BlockSpec((tm, k), lambda j, i: (i, 0)), pl.BlockSpec((k, tn), lambda j, i: (0, j))],
        out_specs=pl.BlockSpec((tm, tn), lambda j, i: (i, j)),
        out_shape=jax.ShapeDtypeStruct((t, n), out_dtype),
        compiler_params=_cp("parallel", "parallel"), name=name)(a, w)


def _mm_tn(a, b, *, name, guest=None):
    r, m = a.shape
    n = b.shape[1]
    tr, tm, tn = _rows(r), _pick(m, 1408), _pick(n, 1408)

    def body(a_ref, b_ref, o_ref):
        @pl.when(pl.program_id(2) == 0)
        def _():
            o_ref[...] = jnp.zeros_like(o_ref)
        o_ref[...] += _dot(a_ref[...].astype(BF16), b_ref[...].astype(BF16), TN)

    res, extra = _call(
        body, grid=(m // tm, n // tn, r // tr),
        in_specs=[pl.BlockSpec((tr, tm), lambda i, j, k: (k, i)), pl.BlockSpec((tr, tn), lambda i, j, k: (k, j))],
        out_specs=[pl.BlockSpec((tm, tn), lambda i, j, k: (i, j))],
        out_shape=[jax.ShapeDtypeStruct((m, n), F32)], args=(a, b),
        sem=("parallel", "parallel", "arbitrary"), name=name, guest=guest)
    return res[0] if guest is None else (res[0], extra)


def _mm_tn_p(p, layer, b, *, name):
    t = p.shape[2]
    n = b.shape[1]
    tr = _rows(t)

    def body(a_ref, b_ref, o_ref):
        @pl.when(pl.program_id(0) == 0)
        def _():
            o_ref[...] = jnp.zeros_like(o_ref)
        o_ref[...] += _dot(a_ref[...].astype(BF16), b_ref[...], TN)

    return pl.pallas_call(
        body, grid=(t // tr,),
        in_specs=[pl.BlockSpec((None, None, tr, PLE_DIM), lambda k: (layer, 0, k, 0)),
                  pl.BlockSpec((tr, n), lambda k: (k, 0))],
        out_specs=pl.BlockSpec((PLE_DIM, n), lambda k: (0, 0)),
        out_shape=jax.ShapeDtypeStruct((PLE_DIM, n), F32),
        compiler_params=_cp("arbitrary"), name=name)(p, b)


def _mm_res_ln(a, w, resid, g, b, *, name):
    t, k = a.shape
    tm = _rows(t)

    def body(a_ref, w_ref, r_ref, g_ref, b_ref, y_ref, ybf_ref, xh_ref, rs_ref):
        acc = _dot(a_ref[...], w_ref[...], NN)
        y, xh, rs = _ln_fwd(ALPHA * r_ref[...] + acc, g_ref[...], b_ref[...])
        y_ref[...] = y
        ybf_ref[...] = y.astype(BF16)
        xh_ref[...] = xh
        rs_ref[...] = rs

    row = pl.BlockSpec((tm, D_MODEL), lambda i: (i, 0))
    return pl.pallas_call(
        body, grid=(t // tm,),
        in_specs=[pl.BlockSpec((tm, k), lambda i: (i, 0)), _full((k, D_MODEL)), row, _full((1, D_MODEL)), _full((1, D_MODEL))],
        out_specs=[row, row, row, pl.BlockSpec((tm, 1), lambda i: (i, 0))],
        out_shape=[jax.ShapeDtypeStruct((t, D_MODEL), F32), jax.ShapeDtypeStruct((t, D_MODEL), BF16),
                   jax.ShapeDtypeStruct((t, D_MODEL), F32), jax.ShapeDtypeStruct((t, 1), F32)],
        compiler_params=_cp("parallel"), name=name)(a, w, resid, g, b)


def _ffn_in(x_bf, w, *, name):
    t = x_bf.shape[0]
    tm, tn = _rows(t), _pick(D_FF, 1408)
    nb = D_FF // tn

    def body(x_ref, wg_ref, wu_ref, dg_ref, du_ref, h_ref):
        x = x_ref[...]
        g = _dot(x, wg_ref[...], NN)
        u = _dot(x, wu_ref[...], NN)
        sig = _sigmoid(g)
        silu = g * sig
        dg_ref[...] = (u * sig * (1.0 + g * (1.0 - sig))).astype(BF16)
        du_ref[...] = silu.astype(BF16)
        h_ref[...] = (silu * u).astype(BF16)

    tile = pl.BlockSpec((tm, tn), lambda j, i: (i, j))
    shp = jax.ShapeDtypeStruct((t, D_FF), BF16)
    return pl.pallas_call(
        body, grid=(nb, t // tm),
        in_specs=[pl.BlockSpec((tm, D_MODEL), lambda j, i: (i, 0)),
                  pl.BlockSpec((D_MODEL, tn), lambda j, i: (0, j)),
                  pl.BlockSpec((D_MODEL, tn), lambda j, i: (0, j + nb))],
        out_specs=[tile, tile, tile], out_shape=[shp, shp, shp],
        compiler_params=_cp("parallel", "parallel"), name=name)(x_bf, w, w)


def _ple_fwd(x, x_bf, p, layer, wg, wp, *, name):
    t = x.shape[0]
    tm = _rows(t)

    def body(x_ref, xbf_ref, p_ref, wg_ref, wp_ref, y_ref, ybf_ref, gate_ref, pp_ref):
        gate = _sigmoid(_dot(xbf_ref[...], wg_ref[...], NN))
        pp = _dot(p_ref[...].astype(BF16), wp_ref[...], NN)
        y = x_ref[...] + gate * pp
        y_ref[...] = y
        ybf_ref[...] = y.astype(BF16)
        gate_ref[...] = gate.astype(BF16)
        pp_ref[...] = pp.astype(BF16)

    row = pl.BlockSpec((tm, D_MODEL), lambda i: (i, 0))
    f32, bf = jax.ShapeDtypeStruct((t, D_MODEL), F32), jax.ShapeDtypeStruct((t, D_MODEL), BF16)
    return pl.pallas_call(
        body, grid=(t // tm,),
        in_specs=[row, row, pl.BlockSpec((None, None, tm, PLE_DIM), lambda i: (layer, 0, i, 0)),
                  _full((D_MODEL, D_MODEL)), _full((PLE_DIM, D_MODEL))],
        out_specs=[row, row, row, row], out_shape=[f32, bf, bf, bf],
        compiler_params=_cp("parallel"), name=name)(x, x_bf, p, wg, wp)


def _loss_head(y, target, *, name):
    t = y.shape[0]
    tm = _rows(t)

    def body(y_ref, t_ref, loss_ref, dy_ref):
        e = y_ref[...] - t_ref[...]

        @pl.when(pl.program_id(0) == 0)
        def _():
            loss_ref[...] = jnp.zeros_like(loss_ref)
        sq = jnp.sum(jnp.sum(e * e, axis=1, keepdims=True), axis=0, keepdims=True)
        loss_ref[...] += sq * (0.5 / D_MODEL)
        dy_ref[...] = e * (1.0 / D_MODEL)

    row = pl.BlockSpec((tm, D_MODEL), lambda i: (i, 0))
    return pl.pallas_call(
        body, grid=(t // tm,), in_specs=[row, row],
        out_specs=[_full((1, 1)), row],
        out_shape=[jax.ShapeDtypeStruct((1, 1), F32), jax.ShapeDtypeStruct((t, D_MODEL), F32)],
        compiler_params=_cp("arbitrary"), name=name)(y, target)


def _ple_bwd(dx3, gate, pp, wg, xhat, rstd, g, *, name, guest=None):
    t = dx3.shape[0]
    tm = _rows(t)

    def body(dx_ref, gate_ref, pp_ref, wg_ref, xh_ref, rs_ref, g_ref,
             du_ref, dubf_ref, dpre_ref, dpp_ref, dg_ref, db_ref):
        dx = dx_ref[...]
        gate = gate_ref[...].astype(F32)
        dpre = (dx * pp_ref[...].astype(F32) * gate * (1.0 - gate)).astype(BF16)
        dpre_ref[...] = dpre
        dpp_ref[...] = (dx * gate).astype(BF16)
        dx2 = dx + _dot(dpre, wg_ref[...], NT)
        du, dg, db = _ln_bwd(dx2, xh_ref[...], rs_ref[...], g_ref[...])
        du_ref[...] = du
        dubf_ref[...] = du.astype(BF16)

        @pl.when(pl.program_id(0) == 0)
        def _():
            dg_ref[...] = jnp.zeros_like(dg_ref)
            db_ref[...] = jnp.zeros_like(db_ref)
        dg_ref[...] += dg
        db_ref[...] += db

    row = pl.BlockSpec((tm, D_MODEL), lambda i: (i, 0))
    vec = _full((1, D_MODEL))
    f32, bf = jax.ShapeDtypeStruct((t, D_MODEL), F32), jax.ShapeDtypeStruct((t, D_MODEL), BF16)
    v32 = jax.ShapeDtypeStruct((1, D_MODEL), F32)
    res, extra = _call(
        body, grid=(t // tm,),
        in_specs=[row, row, row, _full((D_MODEL, D_MODEL)), row, pl.BlockSpec((tm, 1), lambda i: (i, 0)), vec],
        out_specs=[row, row, row, row, vec, vec], out_shape=[f32, bf, bf, bf, v32, v32],
        args=(dx3, gate, pp, wg, xhat, rstd, g), sem=("arbitrary",), name=name, guest=guest)
    return res, extra


def _ffn_out_bwd(du_bf, w_out, g, u, *, name):
    t = du_bf.shape[0]
    tm, tn = _rows(t), _pick(D_FF, 1408)

    def body(du_ref, w_ref, hg_ref, hu_ref, dg_ref, dup_ref):
        dh = _dot(du_ref[...], w_ref[...], NT)
        dg_ref[...] = (dh * hg_ref[...].astype(F32)).astype(BF16)
        dup_ref[...] = (dh * hu_ref[...].astype(F32)).astype(BF16)

    tile = pl.BlockSpec((tm, tn), lambda j, i: (i, j))
    shp = jax.ShapeDtypeStruct((t, D_FF), BF16)
    return pl.pallas_call(
        body, grid=(D_FF // tn, t // tm),
        in_specs=[pl.BlockSpec((tm, D_MODEL), lambda j, i: (i, 0)), pl.BlockSpec((tn, D_MODEL), lambda j, i: (j, 0)),
                  tile, tile],
        out_specs=[tile, tile], out_shape=[shp, shp],
        compiler_params=_cp("parallel", "parallel"), name=name)(du_bf, w_out, g, u)


def _mm_nt(parts, w, *, resid=None, ln=None, out_dtype=F32, name, guest=None):
    t, kp = parts[0].shape
    npart = len(parts)
    tm = _rows(t)
    n_in = npart + 1 + (resid is not None) + (3 if ln is not None else 0)

    def body(*refs):
        a_refs, w_ref = refs[:npart], refs[npart]
        pos = npart + 1
        r_ref = None
        if resid is not None:
            r_ref = refs[pos]
            pos += 1
        if ln is not None:
            xh_ref, rs_ref, g_ref = refs[pos:pos + 3]
        outs = refs[n_in:]
        val = _dot(a_refs[0][...], w_ref[:, :kp], NT)
        for pi in range(1, npart):
            val = val + _dot(a_refs[pi][...], w_ref[:, pi * kp:(pi + 1) * kp], NT)
        if r_ref is not None:
            val = val + ALPHA * r_ref[...]
        if ln is None:
            outs[0][...] = val.astype(out_dtype)
        else:
            du, dg, db = _ln_bwd(val, xh_ref[...], rs_ref[...], g_ref[...])
            outs[0][...] = du
            outs[1][...] = du.astype(BF16)

            @pl.when(pl.program_id(0) == 0)
            def _():
                outs[2][...] = jnp.zeros_like(outs[2])
                outs[3][...] = jnp.zeros_like(outs[3])
            outs[2][...] += dg
            outs[3][...] += db

    row = pl.BlockSpec((tm, D_MODEL), lambda i: (i, 0))
    vec = pl.BlockSpec((1, D_MODEL), lambda i: (0, 0))
    in_specs = [pl.BlockSpec((tm, kp), lambda i: (i, 0)) for _ in range(npart)]
    in_specs.append(pl.BlockSpec((D_MODEL, npart * kp), lambda i: (0, 0), pipeline_mode=pl.Buffered(1)))
    args = list(parts) + [w]
    if resid is not None:
        in_specs.append(row)
        args.append(resid)
    if ln is not None:
        in_specs += [row, pl.BlockSpec((tm, 1), lambda i: (i, 0)), vec]
        args += list(ln)
        out_specs = [row, row, vec, vec]
        out_shape = [jax.ShapeDtypeStruct((t, D_MODEL), F32), jax.ShapeDtypeStruct((t, D_MODEL), BF16),
                     jax.ShapeDtypeStruct((1, D_MODEL), F32), jax.ShapeDtypeStruct((1, D_MODEL), F32)]
    else:
        out_specs = [row]
        out_shape = [jax.ShapeDtypeStruct((t, D_MODEL), out_dtype)]
    res, extra = _call(
        body, grid=(t // tm,), in_specs=in_specs, out_specs=out_specs, out_shape=out_shape, args=tuple(args),
        sem=("arbitrary" if ln is not None else "parallel",), name=name, guest=guest)
    res = res if ln is not None else res[0]
    return res if guest is None else (res, extra)


def _rope_tables(t):
    half = ROPE_DIM // 2
    inv = ROPE_THETA ** (-jnp.arange(0, ROPE_DIM, 2, dtype=F32) / ROPE_DIM)
    ang = jnp.arange(t, dtype=F32)[:, None] * inv[None, :]
    cos, sin = jnp.cos(ang), jnp.sin(ang)
    pad = HEAD_DIM - ROPE_DIM
    c = jnp.concatenate([cos, cos, jnp.ones((t, pad), F32)], axis=1)
    s_hi = jnp.concatenate([jnp.zeros((t, half), F32), sin, jnp.zeros((t, pad), F32)], axis=1)
    s_lo = jnp.concatenate([-sin, jnp.zeros((t, half + pad), F32)], axis=1)
    rep = LANES // HEAD_DIM
    return jnp.tile(c, (1, rep)), jnp.tile(s_hi, (1, rep)), jnp.tile(s_lo, (1, rep))


def _rope(x, c, s_hi, s_lo, sign):
    half = ROPE_DIM // 2
    parts = []
    for j in range(x.shape[1] // LANES):
        xg = x[:, j * LANES:(j + 1) * LANES]
        rot = pltpu.roll(xg, half, 1) * s_hi + pltpu.roll(xg, LANES - half, 1) * s_lo
        parts.append(xg * c + sign * rot)
    return jnp.concatenate(parts, axis=1)


def _qkv_rope(x_bf, w, tables, *, name):
    t = x_bf.shape[0]
    tm = _rows(t)
    n = Q_DIM + 2 * KV_DIM

    def body(x_ref, w_ref, c_ref, sh_ref, sl_ref, q_ref, k_ref, v_ref):
        acc = _dot(x_ref[...], w_ref[...], NN)
        c, sh, sl = c_ref[...], sh_ref[...], sl_ref[...]
        q_ref[...] = (_rope(acc[:, :Q_DIM], c, sh, sl, 1.0) * HEAD_DIM ** -0.5).astype(BF16)
        k_ref[...] = _rope(acc[:, Q_DIM:Q_DIM + KV_DIM], c, sh, sl, 1.0).astype(BF16)
        v_ref[...] = acc[:, Q_DIM + KV_DIM:].astype(BF16)

    tab = pl.BlockSpec((tm, LANES), lambda i: (i, 0))
    return pl.pallas_call(
        body, grid=(t // tm,),
        in_specs=[pl.BlockSpec((tm, D_MODEL), lambda i: (i, 0)), _full((D_MODEL, n)), tab, tab, tab],
        out_specs=[pl.BlockSpec((tm, Q_DIM), lambda i: (i, 0)), pl.BlockSpec((tm, KV_DIM), lambda i: (i, 0)),
                   pl.BlockSpec((tm, KV_DIM), lambda i: (i, 0))],
        out_shape=[jax.ShapeDtypeStruct((t, Q_DIM), BF16), jax.ShapeDtypeStruct((t, KV_DIM), BF16),
                   jax.ShapeDtypeStruct((t, KV_DIM), BF16)],
        compiler_params=_cp("parallel"), name=name)(x_bf, w, *tables)


ATT_ROWS = 256
ATT_STRIP = 64


def _window_specs(t, tq):
    r = tq // WINDOW
    last = t // WINDOW - 1
    return [pl.BlockSpec((WINDOW, KV_DIM), lambda i: (jnp.maximum(i * r - 1, 0), 0)),
            pl.BlockSpec((tq, KV_DIM), lambda i: (i, 0)),
            pl.BlockSpec((WINDOW, KV_DIM), lambda i: (jnp.minimum((i + 1) * r, last), 0))]


def _att_valid(base, t):
    rows = GROUP * WINDOW
    qpos = base + (lax.broadcasted_iota(jnp.int32, (rows, 3 * WINDOW), 0) & (WINDOW - 1))
    kpos = base - WINDOW + lax.broadcasted_iota(jnp.int32, (rows, 3 * WINDOW), 1)
    return (jnp.abs(qpos - kpos) <= WINDOW) & (kpos >= 0) & (kpos < t)


def _stack_heads(x, r0, g):
    return jnp.concatenate(
        [x[r0:r0 + WINDOW, (GROUP * g + h) * HEAD_DIM:(GROUP * g + h + 1) * HEAD_DIM] for h in range(GROUP)], axis=0)


def _sink_column(sink_ref, g):
    return jnp.concatenate([jnp.full((WINDOW, 1), sink_ref[GROUP * g + h], F32) for h in range(GROUP)], axis=0)


def _unstack_heads(pieces):
    return jnp.concatenate([pieces[g][h * WINDOW:(h + 1) * WINDOW] for g in range(N_KV_HEADS) for h in range(GROUP)], axis=1)


def _attn_fwd(q, k, v, sink, *, name, guest=None):
    t = q.shape[0]
    tq = min(ATT_ROWS, t)
    nsb = tq // WINDOW

    def body(sink_ref, q_ref, kp_ref, kc_ref, kn_ref, vp_ref, vc_ref, vn_ref, o_ref, l_ref):
        i = pl.program_id(0)
        qv = q_ref[...]
        kw = jnp.concatenate([kp_ref[...], kc_ref[...], kn_ref[...]], axis=0)
        vw = jnp.concatenate([vp_ref[...], vc_ref[...], vn_ref[...]], axis=0)
        ones = jnp.ones((3 * WINDOW, HEAD_DIM), BF16)
        for sb in range(nsb):
            r0 = sb * WINDOW
            bias =jnp.where(_att_valid(i * tq + r0, t)[:WINDOW], 0.0, -1e30)
            pieces = []
            for hh in range(N_Q_HEADS):
                g, h = hh // GROUP, hh % GROUP
                qh = qv[r0:r0 + WINDOW, hh * HEAD_DIM:(hh + 1) * HEAD_DIM]
                kg = kw[r0:r0 + 3 * WINDOW, g * HEAD_DIM:(g + 1) * HEAD_DIM]
                vg = jnp.concatenate([vw[r0:r0 + 3 * WINDOW, g * HEAD_DIM:(g + 1) * HEAD_DIM], ones], axis=1)
                s = _dot(qh, kg, NT) + bias
                snk = sink_ref[hh]
                m = jnp.maximum(jnp.max(s, axis=1, keepdims=True), snk)
                ov = _dot(jnp.exp(s - m).astype(BF16), vg, NN)
                den = ov[:, HEAD_DIM:HEAD_DIM + 1] + jnp.exp(snk - m)
                pieces.append(ov[:, :HEAD_DIM] * (1.0 / den))
                l_ref[g, sb, h * WINDOW:(h + 1) * WINDOW, :] = m + jnp.log(den)
            o_ref[r0:r0 + WINDOW, :] = jnp.concatenate(pieces, axis=1).astype(BF16)

    return _call(
        body, grid=(t // tq,),
        in_specs=[pl.BlockSpec(memory_space=pltpu.SMEM), pl.BlockSpec((tq, Q_DIM), lambda i: (i, 0))]
        + _window_specs(t, tq) + _window_specs(t, tq),
        out_specs=[pl.BlockSpec((tq, Q_DIM), lambda i: (i, 0)),
                   pl.BlockSpec((N_KV_HEADS, nsb, GROUP * WINDOW, 1), lambda i: (0, i, 0, 0))],
        out_shape=[jax.ShapeDtypeStruct((t, Q_DIM), BF16),
                   jax.ShapeDtypeStruct((N_KV_HEADS, t // WINDOW, GROUP * WINDOW, 1), F32)],
        args=(sink, q, k, k, k, v, v, v), sem=("parallel",), name=name, guest=guest)


def _attn_bwd(q, k, v, o, do, lse, sink, *, name):
    t = q.shape[0]
    tq = min(ATT_ROWS, t)
    nsb = tq // WINDOW

    def body(sink_ref, q_ref, o_ref, do_ref, l_ref, kp_ref, kc_ref, kn_ref, vp_ref, vc_ref, vn_ref,
             dq_ref, dkw_ref, dvw_ref, dsink_ref):
        i = pl.program_id(0)
        qv, ov, dov = q_ref[...], o_ref[...], do_ref[...]
        kw = jnp.concatenate([kp_ref[...], kc_ref[...], kn_ref[...]], axis=0)
        vw = jnp.concatenate([vp_ref[...], vc_ref[...], vn_ref[...]], axis=0)
        lane16 = lax.broadcasted_iota(jnp.int32, (1, N_Q_HEADS), 1)
        dsink = jnp.zeros((1, N_Q_HEADS), F32)
        for sb in range(nsb):
            r0 = sb * WINDOW
            valid = _att_valid(i * tq + r0, t)
            dq_p, dk_p, dv_p = [], [], []
            for g in range(N_KV_HEADS):
                qs, dos = _stack_heads(qv, r0, g), _stack_heads(dov, r0, g)
                delta = jnp.sum(dos.astype(F32) * _stack_heads(ov, r0, g).astype(F32), axis=1, keepdims=True)
                kg = kw[r0:r0 + 3 * WINDOW, g * HEAD_DIM:(g + 1) * HEAD_DIM]
                vg = vw[r0:r0 + 3 * WINDOW, g * HEAD_DIM:(g + 1) * HEAD_DIM]
                lse_col = l_ref[g, sb]
                pr = jnp.where(valid, jnp.exp(_dot(qs, kg, NT) - lse_col), 0.0)
                ds = (pr * (_dot(dos, vg, NT) - delta)).astype(BF16)
                dq_p.append(_dot(ds, kg, NN))
                dk_p.append(_dot(ds, qs, TN))
                dv_p.append(_dot(pr.astype(BF16), dos, TN))
                ps = jnp.exp(_sink_column(sink_ref, g) - lse_col) * delta
                for h in range(GROUP):
                    tot = jnp.sum(ps[h * WINDOW:(h + 1) * WINDOW], axis=0, keepdims=True)
                    dsink = dsink - jnp.where(lane16 == GROUP * g + h, tot, 0.0)
            dq_ref[r0:r0 + WINDOW, :] = _unstack_heads(dq_p)
            dkw_ref[sb] = jnp.concatenate(dk_p, axis=1)
            dvw_ref[sb] = jnp.concatenate(dv_p, axis=1)

        @pl.when(i == 0)
        def _():
            dsink_ref[...] = jnp.zeros_like(dsink_ref)
        dsink_ref[...] += dsink

    rowq = pl.BlockSpec((tq, Q_DIM), lambda i: (i, 0))
    win = pl.BlockSpec((nsb, 3 * WINDOW, KV_DIM), lambda i: (i, 0, 0))
    wshape = jax.ShapeDtypeStruct((t // WINDOW, 3 * WINDOW, KV_DIM), F32)
    return pl.pallas_call(
        body, grid=(t // tq,),
        in_specs=[pl.BlockSpec(memory_space=pltpu.SMEM), rowq, rowq, rowq,
                  pl.BlockSpec((N_KV_HEADS, nsb, GROUP * WINDOW, 1), lambda i: (0, i, 0, 0))]
        + _window_specs(t, tq) + _window_specs(t, tq),
        out_specs=[rowq, win, win, _full((1, N_Q_HEADS))],
        out_shape=[jax.ShapeDtypeStruct((t, Q_DIM), F32), wshape, wshape, jax.ShapeDtypeStruct((1, N_Q_HEADS), F32)],
        compiler_params=_cp("arbitrary"), name=name)(sink, q, o, do, lse, k, k, k, v, v, v)


def _attn_post_bwd(dq, dkw, dvw, tables, *, name):
    t = dq.shape[0]
    nb = t // WINDOW
    n = Q_DIM + 2 * KV_DIM

    def body(dq_ref, ka_ref, kb_ref, kc_ref, va_ref, vb_ref, vc_ref, c_ref, sh_ref, sl_ref, o_ref):
        j = pl.program_id(0)
        lo = (j > 0).astype(F32)
        hi = (j < nb - 1).astype(F32)
        c, sh, sl = c_ref[...], sh_ref[...], sl_ref[...]
        dk = ka_ref[...] * hi + kb_ref[...] + kc_ref[...] * lo
        dv = va_ref[...] * hi + vb_ref[...] + vc_ref[...] * lo
        o_ref[:, :Q_DIM] = (_rope(dq_ref[...], c, sh, sl, -1.0) * HEAD_DIM ** -0.5).astype(BF16)
        o_ref[:, Q_DIM:Q_DIM + KV_DIM] = _rope(dk, c, sh, sl, -1.0).astype(BF16)
        o_ref[:, Q_DIM + KV_DIM:] = dv.astype(BF16)

    wins = [pl.BlockSpec((None, WINDOW, KV_DIM), lambda j: (jnp.minimum(j + 1, nb - 1), 0, 0)),
            pl.BlockSpec((None, WINDOW, KV_DIM), lambda j: (j, 1, 0)),
            pl.BlockSpec((None, WINDOW, KV_DIM), lambda j: (jnp.maximum(j - 1, 0), 2, 0))]
    tab = pl.BlockSpec((WINDOW, LANES), lambda j: (j, 0))
    return pl.pallas_call(
        body, grid=(nb,),
        in_specs=[pl.BlockSpec((WINDOW, Q_DIM), lambda j: (j, 0))] + wins + wins + [tab, tab, tab],
        out_specs=pl.BlockSpec((WINDOW, n), lambda j: (j, 0)),
        out_shape=jax.ShapeDtypeStruct((t, n), BF16),
        compiler_params=_cp("parallel"), name=name)(dq, dkw, dkw, dkw, dvw, dvw, dvw, *tables)


HGRN_ROWS = 512
HGRN_UNROLL = 2


def _cum_mask(rev, transpose=False):
    row = lax.broadcasted_iota(jnp.int32, (HGRN_CHUNK, HGRN_CHUNK), 0)
    col = lax.broadcasted_iota(jnp.int32, (HGRN_CHUNK, HGRN_CHUNK), 1)
    return (row <= col) if rev != transpose else (row >= col)


def _gates(zq, zf, lb):
    q = zq * _sigmoid(zq)
    sig = _sigmoid(zf)
    f = lb + (1.0 - lb) * sig
    k = (1.0 - lb) * (1.0 - sig)
    return q, sig, f, k


def _intra_exact(q, k, b, mask):
    rows = lax.broadcasted_iota(jnp.int32, (HGRN_CHUNK, 1), 0)
    cols = lax.broadcasted_iota(jnp.int32, (HGRN_CHUNK, HGRN_CHUNK), 1)

    def it(s, a):
        pick = rows == s
        b_s = jnp.sum(jnp.where(pick, b, 0.0), axis=0, keepdims=True)
        k_s = jnp.sum(jnp.where(pick, k, 0.0), axis=0, keepdims=True)
        col = jnp.sum(q * jnp.exp(jnp.minimum(b - b_s, 0.0)) * k_s, axis=1, keepdims=True)
        return jnp.where(cols == s, col, a)

    a = lax.fori_loop(0, HGRN_CHUNK, it, jnp.zeros((HGRN_CHUNK, HGRN_CHUNK), F32))
    return jnp.where(mask, a, 0.0)


def _intra_exact_bwd(q, k, b, da):
    rows = lax.broadcasted_iota(jnp.int32, (HGRN_CHUNK, 1), 0)
    cols = lax.broadcasted_iota(jnp.int32, (HGRN_CHUNK, HGRN_CHUNK), 1)

    def it(s, carry):
        dq, dk = carry
        pick = rows == s
        b_s = jnp.sum(jnp.where(pick, b, 0.0), axis=0, keepdims=True)
        k_s = jnp.sum(jnp.where(pick, k, 0.0), axis=0, keepdims=True)
        w = jnp.sum(jnp.where(cols == s, da, 0.0), axis=1, keepdims=True) * jnp.exp(jnp.minimum(b - b_s, 0.0))
        dq = dq + w * k_s
        dk = jnp.where(pick, jnp.sum(w * q, axis=0, keepdims=True), dk)
        return dq, dk

    z = jnp.zeros((HGRN_CHUNK, HGRN_KEY), F32)
    return lax.fori_loop(0, HGRN_CHUNK, it, (z, z))


def _chunk_cumsum(g, from_end):
    n = g.shape[0]
    row = lax.broadcasted_iota(jnp.int32, g.shape, 0)
    x, s = g, 1
    while s < n:
        if from_end:
            x = x + jnp.where(row < n - s, pltpu.roll(x, n - s, 0), 0.0)
        else:
            x = x + jnp.where(row >= s, pltpu.roll(x, s, 0), 0.0)
        s *= 2
    return x


def _head(a, h):
    return a[:, h * LANES:(h + 1) * LANES]


def _block_is_mild(zf_ref, lb, nch):
    g = jnp.log(lb + (1.0 - lb) * _sigmoid(zf_ref[...]))
    lows = [jnp.min(jnp.sum(g[c * HGRN_CHUNK:(c + 1) * HGRN_CHUNK], axis=0, keepdims=True)) for c in range(nch)]
    return functools.reduce(jnp.minimum, lows) >= FACTORED_DECAY_LIMIT


def _hgrn_fwd(z, lb, rev, *, name, guest=None):
    t = z.shape[0]
    rb = min(HGRN_ROWS, t)
    nb, nch = t // rb, rb // HGRN_CHUNK
    heads = range(HGRN_HEADS)

    def blk(n):
        return nb - 1 - n if rev else n

    def body(zq_ref, zf_ref, zv_ref, lb_ref, o_ref, s_ref, st_ref):
        @pl.when(pl.program_id(0) == 0)
        def _():
            st_ref[...] = jnp.zeros_like(st_ref)
        lbv = lb_ref[...]
        cmask = _cum_mask(rev)

        def chunk(c, carry, mild):
            cc = nch - 1 - c if rev else c
            sl = pl.ds(pl.multiple_of(cc * HGRN_CHUNK, HGRN_CHUNK), HGRN_CHUNK)
            q, _, f, k = _gates(zq_ref[sl, :], zf_ref[sl, :], lbv)
            v = zv_ref[sl, :].astype(BF16)
            g = jnp.log(f)
            b = _chunk_cumsum(g, rev)
            tot = jnp.sum(g, axis=0, keepdims=True)
            qd = (q * jnp.exp(b)).astype(BF16)
            kd = (k * jnp.exp(tot - b)).astype(BF16)
            etot = jnp.exp(tot)

            def factored():
                kf = (k * jnp.exp(-b)).astype(BF16)
                return tuple(jnp.where(cmask, _dot(_head(qd, h), _head(kf, h), NT), 0.0) for h in heads)

            def exact():
                return tuple(_intra_exact(_head(q, h), _head(k, h), _head(b, h), cmask) for h in heads)

            a = factored() if mild else lax.cond(jnp.min(tot) >= FACTORED_DECAY_LIMIT, factored, exact)
            for h in heads:
                st = st_ref[h]
                st_bf = st.astype(BF16)
                s_ref[h, cc] = st_bf
                o_ref[sl, h * LANES:(h + 1) * LANES] = (
                    _dot(_head(qd, h), st_bf, NT) + _dot(a[h].astype(BF16), _head(v, h), NN))
                st_ref[h] = st * _head(etot, h) + _dot(_head(v, h), _head(kd, h), TN)
            return carry

        lax.cond(_block_is_mild(zf_ref, lbv, nch),
                 lambda: lax.fori_loop(0, nch, functools.partial(chunk, mild=True), 0, unroll=HGRN_UNROLL),
                 lambda: lax.fori_loop(0, nch, functools.partial(chunk, mild=False), 0))

    def col(j):
        return pl.BlockSpec((rb, D_MODEL), lambda n: (blk(n), j))

    return _call(
        body, grid=(nb,),
        in_specs=[col(0), col(2 if rev else 1), col(3), _full((1, D_MODEL))],
        out_specs=[col(0), pl.BlockSpec((HGRN_HEADS, nch, LANES, LANES), lambda n: (0, blk(n), 0, 0))],
        out_shape=[jax.ShapeDtypeStruct((t, D_MODEL), F32),
                   jax.ShapeDtypeStruct((HGRN_HEADS, t // HGRN_CHUNK, LANES, LANES), BF16)],
        scratch_shapes=[pltpu.VMEM((HGRN_HEADS, LANES, LANES), F32)],
        args=(z, z, z, lb), sem=("arbitrary",), name=name, guest=guest)


def _hgrn_bwd(z, lb, d_o, states, rev, *, name):
    t = z.shape[0]
    rb = min(HGRN_ROWS, t)
    nb, nch = t // rb, rb // HGRN_CHUNK
    heads = range(HGRN_HEADS)

    def blk(n):
        return n if rev else nb - 1 - n

    def body(zq_ref, zf_ref, zv_ref, lb_ref, do_ref, s_ref, dq_ref, dzf_ref, dv_ref, dlb_ref, dst_ref):
        @pl.when(pl.program_id(0) == 0)
        def _():
            dst_ref[...] = jnp.zeros_like(dst_ref)
            dlb_ref[...] = jnp.zeros_like(dlb_ref)
        lbv = lb_ref[...]
        cmask = _cum_mask(rev)

        def chunk(c, carry, mild):
            cc = c if rev else nch - 1 - c
            sl = pl.ds(pl.multiple_of(cc * HGRN_CHUNK, HGRN_CHUNK), HGRN_CHUNK)
            zq = zq_ref[sl, :]
            q, sig, f, k = _gates(zq, zf_ref[sl, :], lbv)
            v = zv_ref[sl, :].astype(BF16)
            g = jnp.log(f)
            b = _chunk_cumsum(g, rev)
            tot = jnp.sum(g, axis=0, keepdims=True)
            eb = jnp.exp(b)
            etb = jnp.exp(tot - b)
            etot = jnp.exp(tot)
            qd = (q * eb).astype(BF16)
            kd = (k * etb).astype(BF16)
            do = do_ref[sl, :].astype(BF16)
            da = [jnp.where(cmask, _dot(_head(do, h), _head(v, h), NT), 0.0) for h in heads]

            def factored():
                ke = jnp.exp(-b)
                kf = (k * ke).astype(BF16)
                res = []
                for h in heads:
                    qh, kh = _head(qd, h), _head(kf, h)
                    da_bf = da[h].astype(BF16)
                    dqf, dkf = _dot(da_bf, kh, NN), _dot(da_bf, qh, TN)
                    res.append((jnp.where(cmask, _dot(qh, kh, NT), 0.0), dqf * _head(eb, h), dkf * _head(ke, h),
                                qh.astype(F32) * dqf - kh.astype(F32) * dkf))
                return tuple(res)

            def exact():
                res = []
                for h in heads:
                    qh, kh, bh = _head(q, h), _head(k, h), _head(b, h)
                    dq_i, dk_i = _intra_exact_bwd(qh, kh, bh, da[h])
                    res.append((_intra_exact(qh, kh, bh, cmask), dq_i, dk_i, qh * dq_i - kh * dk_i))
                return tuple(res)

            intra = factored() if mild else lax.cond(jnp.min(tot) >= FACTORED_DECAY_LIMIT, factored, exact)
            dq_p, dk_p, db_p, dtot_p = [], [], [], []
            for h in heads:
                a, dq_i, dk_i, db_i = intra[h]
                doh, vh, qh, kh = _head(do, h), _head(v, h), _head(q, h), _head(k, h)
                st0 = s_ref[h, cc]
                dst = dst_ref[h]
                dst_bf = dst.astype(BF16)
                dv_ref[sl, h * LANES:(h + 1) * LANES] = _dot(a.astype(BF16), doh, TN) + _dot(_head(kd, h), dst_bf, NT)
                dq_x = _dot(doh, st0, NN) * _head(eb, h)
                dk_x = _dot(vh, dst_bf, NN) * _head(etb, h)
                dtot_p.append(jnp.sum(kh * dk_x, axis=0, keepdims=True)
                              + _head(etot, h) * jnp.sum(dst * st0.astype(F32), axis=0, keepdims=True))
                dst_ref[h] = dst * _head(etot, h) + _dot(doh, _head(qd, h), TN)
                dq_p.append(dq_i + dq_x)
                dk_p.append(dk_i + dk_x)
                db_p.append(db_i + qh * dq_x - kh * dk_x)
            dq, dk, db = (jnp.concatenate(x, axis=1) for x in (dq_p, dk_p, db_p))
            dg = _chunk_cumsum(db, not rev) + jnp.concatenate(dtot_p, axis=1)
            sq = _sigmoid(zq)
            dq_ref[sl, :] = dq * sq * (1.0 + zq * (1.0 - sq))
            dfk = dg / f - dk
            dzf_ref[sl, :] = (dfk * (1.0 - lbv) * sig * (1.0 - sig)).astype(BF16)
            dlb_ref[...] += jnp.sum(dfk * (1.0 - sig), axis=0, keepdims=True)
            return carry

        lax.cond(_block_is_mild(zf_ref, lbv, nch),
                 lambda: lax.fori_loop(0, nch, functools.partial(chunk, mild=True), 0, unroll=HGRN_UNROLL),
                 lambda: lax.fori_loop(0, nch, functools.partial(chunk, mild=False), 0))

    def col(j):
        return pl.BlockSpec((rb, D_MODEL), lambda n: (blk(n), j))

    return pl.pallas_call(
        body, grid=(nb,),
        in_specs=[col(0), col(2 if rev else 1), col(3), _full((1, D_MODEL)), col(0),
                  pl.BlockSpec((HGRN_HEADS, nch, LANES, LANES), lambda n: (0, blk(n), 0, 0))],
        out_specs=[col(0), col(0), col(0), _full((1, D_MODEL))],
        out_shape=[jax.ShapeDtypeStruct((t, D_MODEL), F32), jax.ShapeDtypeStruct((t, D_MODEL), BF16),
                   jax.ShapeDtypeStruct((t, D_MODEL), F32), jax.ShapeDtypeStruct((1, D_MODEL), F32)],
        scratch_shapes=[pltpu.VMEM((HGRN_HEADS, LANES, LANES), F32)],
        compiler_params=_cp("arbitrary"), name=name)(z, z, z, lb, d_o, states)


def _hgrn_post_fwd(o_f, o_b, z, norm_g, *, name):
    t = o_f.shape[0]
    tm = _rows(t)

    def body(of_ref, ob_ref, gate_ref, ng_ref, y_ref):
        ng = ng_ref[...]
        for h in range(HGRN_HEADS):
            sl = slice(h * LANES, (h + 1) * LANES)
            o = of_ref[:, sl] + ob_ref[:, sl]
            r = lax.rsqrt(jnp.mean(o * o, axis=1, keepdims=True) + LN_EPS)
            gt = gate_ref[:, sl]
            y_ref[:, sl] = (o * r * ng * (gt * _sigmoid(gt))).astype(BF16)

    row = pl.BlockSpec((tm, D_MODEL), lambda i: (i, 0))
    return pl.pallas_call(
        body, grid=(t // tm,),
        in_specs=[row, row, pl.BlockSpec((tm, D_MODEL), lambda i: (i, 4)), _full((1, LANES))],
        out_specs=row, out_shape=jax.ShapeDtypeStruct((t, D_MODEL), BF16),
        compiler_params=_cp("parallel"), name=name)(o_f, o_b, z, norm_g)


def _hgrn_post_bwd(dy, o_f, o_b, z, norm_g, *, name):
    t = o_f.shape[0]
    tm = _rows(t)

    def body(dy_ref, of_ref, ob_ref, gate_ref, ng_ref, do_ref, dgate_ref, dng_ref):
        ng = ng_ref[...]
        dng = jnp.zeros((1, LANES), F32)
        for h in range(HGRN_HEADS):
            sl = slice(h * LANES, (h + 1) * LANES)
            o = of_ref[:, sl] + ob_ref[:, sl]
            r = lax.rsqrt(jnp.mean(o * o, axis=1, keepdims=True) + LN_EPS)
            gt = gate_ref[:, sl]
            sg = _sigmoid(gt)
            dyv = dy_ref[:, sl].astype(F32)
            xn = o * r
            dgate_ref[:, sl] = (dyv * xn * ng * sg * (1.0 + gt * (1.0 - sg))).astype(BF16)
            dn = dyv * gt * sg
            dng = dng + jnp.sum(dn * xn, axis=0, keepdims=True)
            dxn = dn * ng
            do_ref[:, sl] = r * (dxn - xn * jnp.mean(dxn * xn, axis=1, keepdims=True))

        @pl.when(pl.program_id(0) == 0)
        def _():
            dng_ref[...] = jnp.zeros_like(dng_ref)
        dng_ref[...] += dng

    row = pl.BlockSpec((tm, D_MODEL), lambda i: (i, 0))
    return pl.pallas_call(
        body, grid=(t // tm,),
        in_specs=[row, row, row, pl.BlockSpec((tm, D_MODEL), lambda i: (i, 4)), _full((1, LANES))],
        out_specs=[row, row, _full((1, LANES))],
        out_shape=[jax.ShapeDtypeStruct((t, D_MODEL), F32), jax.ShapeDtypeStruct((t, D_MODEL), BF16),
                   jax.ShapeDtypeStruct((1, LANES), F32)],
        compiler_params=_cp("arbitrary"), name=name)(dy, o_f, o_b, z, norm_g)


def _hgrn_combine(dq_f, dq_b, dzf_f, dzf_b, dv_f, dv_b, dgate, *, name):
    t = dq_f.shape[0]
    tm = _rows(t)

    def body(qf, qb, ff, fb, vf, vb, gt, o_ref):
        o_ref[:, 0 * D_MODEL:1 * D_MODEL] = (qf[...] + qb[...]).astype(BF16)
        o_ref[:, 1 * D_MODEL:2 * D_MODEL] = ff[...]
        o_ref[:, 2 * D_MODEL:3 * D_MODEL] = fb[...]
        o_ref[:, 3 * D_MODEL:4 * D_MODEL] = (vf[...] + vb[...]).astype(BF16)
        o_ref[:, 4 * D_MODEL:5 * D_MODEL] = gt[...]

    row = pl.BlockSpec((tm, D_MODEL), lambda i: (i, 0))
    return pl.pallas_call(
        body, grid=(t // tm,), in_specs=[row] * 7,
        out_specs=pl.BlockSpec((tm, 5 * D_MODEL), lambda i: (i, 0)),
        out_shape=jax.ShapeDtypeStruct((t, 5 * D_MODEL), BF16),
        compiler_params=_cp("parallel"), name=name)(dq_f, dq_b, dzf_f, dzf_b, dv_f, dv_b, dgate)


def _lower_bounds(logits2d, *, name):
    def body(l_ref, lb_ref):
        l = l_ref[...]
        e = jnp.exp(l - jnp.max(l, axis=0, keepdims=True))
        sm = e / jnp.sum(e, axis=0, keepdims=True)
        s1, s2, s3 = sm[1:2], sm[2:3], sm[3:4]
        lb_ref[...] = jnp.concatenate([jnp.zeros_like(s1), s1, s1 + s2, s1 + s2 + s3], axis=0)

    return pl.pallas_call(body, out_shape=jax.ShapeDtypeStruct(logits2d.shape, F32), name=name)(logits2d)


def _lower_bounds_bwd(logits2d, dlb, *, name):
    def body(l_ref, d_ref, o_ref):
        l = l_ref[...]
        e = jnp.exp(l - jnp.max(l, axis=0, keepdims=True))
        sm = e / jnp.sum(e, axis=0, keepdims=True)
        d = d_ref[...]
        d1, d2, d3 = d[1:2], d[2:3], d[3:4]
        dsm = jnp.concatenate([jnp.zeros_like(d1), d1 + d2 + d3, d2 + d3, d3], axis=0)
        o_ref[...] = sm * (dsm - jnp.sum(sm * dsm, axis=0, keepdims=True))

    return pl.pallas_call(body, out_shape=jax.ShapeDtypeStruct(logits2d.shape, F32), name=name)(logits2d, dlb)


def _adamw(w, g, m, v, *, name):
    r, c = w.shape
    tr = r if r <= 512 else _pick_rows(r)

    def body(w_ref, g_ref, m_ref, v_ref, d_ref, nm_ref, nv_ref):
        gv = g_ref[...]
        nm = ADAM_B1 * m_ref[...] + (1.0 - ADAM_B1) * gv
        nv = ADAM_B2 * v_ref[...] + (1.0 - ADAM_B2) * (gv * gv)
        m_hat = nm / (1.0 - ADAM_B1 ** ADAM_STEP)
        v_hat = nv / (1.0 - ADAM_B2 ** ADAM_STEP)
        d_ref[...] = -ADAM_LR * (m_hat / (jnp.sqrt(v_hat) + ADAM_EPS) + ADAM_WD * w_ref[...])
        nm_ref[...] = nm
        nv_ref[...] = nv

    blk = pl.BlockSpec((tr, c), lambda i: (i, 0))
    shp = jax.ShapeDtypeStruct((r, c), F32)
    return pl.pallas_call(body, grid=(r // tr,), in_specs=[blk] * 4, out_specs=[blk] * 3, out_shape=[shp] * 3,
                          compiler_params=_cp("parallel"), name=name)(w, g, m, v)


def _pick_rows(r, cap=512):
    best = 8
    for d in range(8, cap + 1, 8):
        if r % d == 0:
            best = d
    assert r % best == 0, r
    return best


def _local_step(x, p, target, w, sp):
    t = x.shape[0]
    tables = _rope_tables(t)
    logits2d = sp["hgrn_lb_logits"].reshape(DEPTH, 2 * D_MODEL)
    lb_all = _lower_bounds(logits2d, name="lb_fwd").reshape(DEPTH, 2, 1, D_MODEL)
    row = lambda a, i: a[i][None]

    saved = []
    xf, xb = x, x.astype(BF16)
    for i in range(DEPTH):
        j = i // 2
        s = dict(xin_bf=xb)
        if i % 2 == 0:
            q, k, v = _qkv_rope(xb, w["att_w_qkv"][j], tables, name=f"qkv_rope_{i}")
            o, lse = _attn_fwd(q, k, v, sp["att_sink"][j], name=f"attn_fwd_{i}")
            s.update(q=q, k=k, v=v, o=o, lse=lse)
            mix_in, w_o = o, w["att_w_o"][j]
        else:
            z = _mm_nn(xb, w["hgrn_w_in"][j], out_dtype=F32, name=f"hgrn_in_{i}")
            o_f, s_f = _hgrn_fwd(z, lb_all[i, 0], False, name=f"hgrn_scan_f_{i}")
            o_b, s_b = _hgrn_fwd(z, lb_all[i, 1], True, name=f"hgrn_scan_b_{i}")
            og = _hgrn_post_fwd(o_f, o_b, z, row(sp["hgrn_norm_g"], j), name=f"hgrn_post_{i}")
            s.update(z=z, o_f=o_f, o_b=o_b, s_f=s_f, s_b=s_b, og=og)
            mix_in, w_o = og, w["hgrn_w_o"][j]
        x1, x1b, xh1, rs1 = _mm_res_ln(mix_in, w_o, xf, row(sp["ln_mix_g"], i), row(sp["ln_mix_b"], i), name=f"mix_out_ln_{i}")
        g, u, h = _ffn_in(x1b, w["ffn_w_in"][i], name=f"ffn_in_{i}")
        x2, x2b, xh2, rs2 = _mm_res_ln(h, w["ffn_w_out"][i], x1, row(sp["ln_ffn_g"], i), row(sp["ln_ffn_b"], i), name=f"ffn_out_ln_{i}")
        xf, xb, gate, pp = _ple_fwd(x2, x2b, p, i, w["ple_w_gate"][i], w["ple_w_proj"][i], name=f"ple_fwd_{i}")
        s.update(x1b=x1b, xh1=xh1, rs1=rs1, g=g, u=u, h=h, x2b=x2b, xh2=xh2, rs2=rs2, gate=gate, pp=pp)
        saved.append(s)

    loss, dx = _loss_head(xf, target, name="loss_head")

    gw = {n: [None] * w[n].shape[0] for n in w}
    gs = {n: [None] * DEPTH for n in ("ln_mix_g", "ln_mix_b", "ln_ffn_g", "ln_ffn_b")}
    gs.update(att_sink=[None] * 2, hgrn_norm_g=[None] * 2)
    dlb = [jnp.zeros((2, D_MODEL), F32)] * DEPTH
    for i in reversed(range(DEPTH)):
        j = i // 2
        s = saved[i]
        du2, du2b, dpre, dpp, gs["ln_ffn_g"][i], gs["ln_ffn_b"][i] = _ple_bwd(
            dx, s["gate"], s["pp"], w["ple_w_gate"][i], s["xh2"], s["rs2"], row(sp["ln_ffn_g"], i), name=f"ple_bwd_{i}")
        gw["ple_w_gate"][i] = _mm_tn(s["x2b"], dpre, name=f"dw_ple_gate_{i}")
        gw["ple_w_proj"][i] = _mm_tn_p(p, i, dpp, name=f"dw_ple_proj_{i}")
        dgg, dgu = _ffn_out_bwd(du2b, w["ffn_w_out"][i], s["g"], s["u"], name=f"ffn_out_bwd_{i}")
        gw["ffn_w_out"][i] = _mm_tn(s["h"], du2b, name=f"dw_ffn_out_{i}")
        du1, du1b, gs["ln_mix_g"][i], gs["ln_mix_b"][i] = _mm_nt(
            [dgg, dgu], w["ffn_w_in"][i], tk=_pick(D_FF, 1408), resid=du2,
            ln=(s["xh1"], s["rs1"], row(sp["ln_mix_g"], i)), name=f"ffn_in_bwd_{i}")
        gw["ffn_w_in"][i] = jnp.concatenate(
            [_mm_tn(s["x1b"], dgg, name=f"dw_ffn_gate_{i}"), _mm_tn(s["x1b"], dgu, name=f"dw_ffn_up_{i}")], axis=1)
        if i % 2 == 0:
            gw["att_w_o"][j] = _mm_tn(s["o"], du1b, name=f"dw_att_o_{i}")
            do = _mm_nt([du1b], w["att_w_o"][j], tk=Q_DIM, out_dtype=BF16, name=f"att_o_bwd_{i}")
            dq, dkw, dvw, gs["att_sink"][j] = _attn_bwd(
                s["q"], s["k"], s["v"], s["o"], do, s["lse"], sp["att_sink"][j], name=f"attn_bwd_{i}")
            dqkv = _attn_post_bwd(dq, dkw, dvw, tables, name=f"attn_post_bwd_{i}")
            gw["att_w_qkv"][j] = _mm_tn(s["xin_bf"], dqkv, name=f"dw_att_qkv_{i}")
            dx = _mm_nt([dqkv], w["att_w_qkv"][j], tk=Q_DIM + 2 * KV_DIM, resid=du1, name=f"qkv_bwd_{i}")
        else:
            gw["hgrn_w_o"][j] = _mm_tn(s["og"], du1b, name=f"dw_hgrn_o_{i}")
            dy = _mm_nt([du1b], w["hgrn_w_o"][j], tk=D_MODEL, out_dtype=BF16, name=f"hgrn_o_bwd_{i}")
            d_o, dgate, gs["hgrn_norm_g"][j] = _hgrn_post_bwd(
                dy, s["o_f"], s["o_b"], s["z"], row(sp["hgrn_norm_g"], j), name=f"hgrn_post_bwd_{i}")
            dq_f, dzf_f, dv_f, dlb_f = _hgrn_bwd(s["z"], lb_all[i, 0], d_o, s["s_f"], False, name=f"hgrn_scan_f_bwd_{i}")
            dq_b, dzf_b, dv_b, dlb_b = _hgrn_bwd(s["z"], lb_all[i, 1], d_o, s["s_b"], True, name=f"hgrn_scan_b_bwd_{i}")
            dlb[i] = jnp.stack([dlb_f.reshape(D_MODEL), dlb_b.reshape(D_MODEL)])
            dz = _hgrn_combine(dq_f, dq_b, dzf_f, dzf_b, dv_f, dv_b, dgate, name=f"hgrn_combine_{i}")
            gw["hgrn_w_in"][j] = _mm_tn(s["xin_bf"], dz, name=f"dw_hgrn_in_{i}")
            dx = _mm_nt([dz], w["hgrn_w_in"][j], tk=_pick(5 * D_MODEL, 1408), resid=du1, name=f"hgrn_in_bwd_{i}")

    gw = {n: jnp.stack(v) for n, v in gw.items()}
    gsmall = {n: jnp.concatenate(v, axis=0) for n, v in gs.items()}
    gsmall["hgrn_lb_logits"] = _lower_bounds_bwd(
        logits2d, jnp.stack(dlb).reshape(DEPTH, 2 * D_MODEL), name="lb_bwd").reshape(DEPTH, 2, D_MODEL)
    return loss, dx, gw, gsmall


ANY = pl.BlockSpec(memory_space=pl.ANY)


def _place():
    x, y, c = lax.axis_index("x"), lax.axis_index("y"), lax.axis_index("c")
    chips = [(1 - x, y), (x, 1 - y), (1 - x, 1 - y)]
    return x, y, c, chips


def _remote(src, dst, send_sem, recv_sem, to):
    return pltpu.make_async_remote_copy(src_ref=src, dst_ref=dst, send_sem=send_sem, recv_sem=recv_sem,
                                        device_id=to, device_id_type=MESH)


def _allgather_weights(packed, *, name):
    r = packed.shape[0]
    hr = r // 2

    def body(src_ref, out_ref, send_sems, recv_sems, local_sem):
        x, y, c, chips = _place()
        sibling = (x, y, 1 - c)

        def half(cx, cy, hc):
            return out_ref.at[2 * cx + cy, pl.ds(hc * hr, hr), :]

        mine = pltpu.make_async_copy(src_ref, out_ref.at[2 * x + y], local_sem)
        mine.start()
        my_half = src_ref.at[pl.ds(c * hr, hr), :]
        first = [_remote(my_half, half(x, y, c), send_sems.at[j], recv_sems.at[j], (*chip, c)) for j, chip in enumerate(chips)]
        for cp in first:
            cp.start()
        passed = [_remote(half(*chip, c), half(*chip, c), send_sems.at[3 + j], recv_sems.at[3 + j], sibling)
                  for j, chip in enumerate(chips)]
        for j, chip in enumerate(chips):
            _remote(my_half, half(*chip, c), send_sems.at[j], recv_sems.at[j], (*chip, c)).wait_recv()
            passed[j].start()
        for j, chip in enumerate(chips):
            _remote(my_half, half(*chip, 1 - c), send_sems.at[3 + j], recv_sems.at[3 + j], sibling).wait_recv()
        for cp in first + passed:
            cp.wait_send()
        mine.wait()

    return pl.pallas_call(
        body, in_specs=[ANY], out_specs=ANY, out_shape=jax.ShapeDtypeStruct((N_CHIPS, r, packed.shape[1]), packed.dtype),
        scratch_shapes=[pltpu.SemaphoreType.DMA((6,)), pltpu.SemaphoreType.DMA((6,)), pltpu.SemaphoreType.DMA],
        name=name)(packed)


def _allreduce_small(block, *, name):
    m, n = block.shape

    def body(x_ref, sum_ref, all_ref, send_sems, recv_sems):
        x, y, c, chips = _place()
        me, sibling = (x, y, c), (x, y, 1 - c)

        def rows(px, py, pc):
            return all_ref.at[pl.ds((4 * px + 2 * py + pc) * m, m), :]

        all_ref[pl.ds((4 * x + 2 * y + c) * m, m), :] = x_ref[...]
        first = [_remote(x_ref, rows(*me), send_sems.at[0], recv_sems.at[0], sibling)]
        first += [_remote(x_ref, rows(*me), send_sems.at[1 + j], recv_sems.at[1 + j], (*chip, c)) for j, chip in enumerate(chips)]
        for cp in first:
            cp.start()
        passed = [_remote(rows(*chip, c), rows(*chip, c), send_sems.at[4 + j], recv_sems.at[4 + j], sibling)
                  for j, chip in enumerate(chips)]
        for j, chip in enumerate(chips):
            _remote(x_ref, rows(*chip, c), send_sems.at[1 + j], recv_sems.at[1 + j], me).wait_recv()
            passed[j].start()
        _remote(x_ref, rows(*sibling), send_sems.at[0], recv_sems.at[0], me).wait_recv()
        for j, chip in enumerate(chips):
            _remote(x_ref, rows(*chip, 1 - c), send_sems.at[4 + j], recv_sems.at[4 + j], me).wait_recv()
        for cp in first + passed:
            cp.wait_send()
        acc = all_ref[pl.ds(0, m), :]
        for d in range(1, 8):
            acc = acc + all_ref[pl.ds(d * m, m), :]
        sum_ref[...] = acc

    vmem = pl.BlockSpec(memory_space=pltpu.VMEM)
    return pl.pallas_call(
        body, in_specs=[vmem], out_specs=vmem, out_shape=jax.ShapeDtypeStruct((m, n), F32),
        scratch_shapes=[pltpu.VMEM((8 * m, n), F32), pltpu.SemaphoreType.DMA((7,)), pltpu.SemaphoreType.DMA((7,))],
        name=name)(block)


def _pair_exchange(g, *, name):
    n, r, w = g.shape
    hr = r // 2

    def body(g_ref, out_ref, send_sem, recv_sem):
        x, y, c, _ = _place()
        cp = _remote(g_ref.at[:, pl.ds((1 - c) * hr, hr), :], out_ref, send_sem, recv_sem, (x, y, 1 - c))
        cp.start()
        cp.wait()

    return pl.pallas_call(
        body, in_specs=[ANY], out_specs=ANY, out_shape=jax.ShapeDtypeStruct((n, hr, w), g.dtype),
        scratch_shapes=[pltpu.SemaphoreType.DMA, pltpu.SemaphoreType.DMA], name=name)(g)


def _chip_scatter(part, *, name):
    n, h, w = part.shape

    def body(p_ref, out_ref, send_sems, recv_sems, local_sem):
        x, y, c, chips = _place()
        me = 2 * x + y
        mine = pltpu.make_async_copy(p_ref.at[me], out_ref.at[me], local_sem)
        mine.start()
        sends = [_remote(p_ref.at[2 * cx + cy], out_ref.at[me], send_sems.at[j], recv_sems.at[j], (cx, cy, c))
                 for j, (cx, cy) in enumerate(chips)]
        for cp in sends:
            cp.start()
        for j, (cx, cy) in enumerate(chips):
            _remote(p_ref.at[me], out_ref.at[2 * cx + cy], send_sems.at[j], recv_sems.at[j], (cx, cy, c)).wait_recv()
        for cp in sends:
            cp.wait_send()
        mine.wait()

    return pl.pallas_call(
        body, in_specs=[ANY], out_specs=ANY, out_shape=jax.ShapeDtypeStruct((n, h, w), part.dtype),
        scratch_shapes=[pltpu.SemaphoreType.DMA((3,)), pltpu.SemaphoreType.DMA((3,)), pltpu.SemaphoreType.DMA],
        name=name)(part)


def _pair_share(half, *, name):
    h, w = half.shape

    def body(h_ref, out_ref, send_sem, recv_sem, local_sem):
        x, y, c, _ = _place()
        dst = out_ref.at[pl.ds(c * h, h), :]
        mine = pltpu.make_async_copy(h_ref, dst, local_sem)
        mine.start()
        cp = _remote(h_ref, dst, send_sem, recv_sem, (x, y, 1 - c))
        cp.start()
        cp.wait_send()
        _remote(h_ref, out_ref.at[pl.ds((1 - c) * h, h), :], send_sem, recv_sem, (x, y, 1 - c)).wait_recv()
        mine.wait()

    return pl.pallas_call(
        body, in_specs=[ANY], out_specs=ANY, out_shape=jax.ShapeDtypeStruct((2 * h, w), half.dtype),
        scratch_shapes=[pltpu.SemaphoreType.DMA, pltpu.SemaphoreType.DMA, pltpu.SemaphoreType.DMA], name=name)(half)


def _add_own_half(g, recv, c, *, name):
    n, r, w = g.shape
    hr = r // 2
    tr = _pick_rows(hr, 1024)
    nr = hr // tr

    def body(c_ref, g_ref, r_ref, o_ref):
        o_ref[...] = (g_ref[...] + r_ref[...]).astype(BF16)

    return pl.pallas_call(
        body,
        grid_spec=pltpu.PrefetchScalarGridSpec(
            num_scalar_prefetch=1, grid=(n, nr),
            in_specs=[pl.BlockSpec((None, tr, w), lambda s, i, c_ref: (s, c_ref[0] * nr + i, 0)),
                      pl.BlockSpec((None, tr, w), lambda s, i, c_ref: (s, i, 0))],
            out_specs=pl.BlockSpec((None, tr, w), lambda s, i, c_ref: (s, i, 0))),
        out_shape=jax.ShapeDtypeStruct((n, hr, w), BF16),
        compiler_params=_cp("parallel", "parallel"), name=name)(c, g, recv)


def _sum_chips(q, *, name):
    n, h, w = q.shape
    tr = _pick_rows(h, 1024)

    def body(a, b, c, d, o_ref):
        o_ref[...] = ((a[...].astype(F32) + b[...].astype(F32)) + c[...].astype(F32)) + d[...].astype(F32)

    specs = [pl.BlockSpec((None, tr, w), functools.partial(lambda i, s: (s, i, 0), s=s)) for s in range(n)]
    return pl.pallas_call(
        body, grid=(h // tr,), in_specs=specs, out_specs=pl.BlockSpec((tr, w), lambda i: (i, 0)),
        out_shape=jax.ShapeDtypeStruct((h, w), F32), compiler_params=_cp("parallel"), name=name)(q, q, q, q)


PACK_W = 1024
BIG = (("att_w_qkv", 2), ("att_w_o", 1), ("hgrn_w_in", 2), ("hgrn_w_o", 1),
       ("ffn_w_in", 2), ("ffn_w_out", 1), ("ple_w_gate", 1), ("ple_w_proj", 2))
LB_ROWS = 32


def _pack_shards(shards):
    return jnp.concatenate([shards[n].reshape(-1, PACK_W) for n, _ in BIG], axis=0)


def _unpack_shards(buf, shapes):
    out, r0 = {}, 0
    for n, _ in BIG:
        nr = shapes[n][0] * shapes[n][1] * shapes[n][2] // PACK_W
        out[n] = buf[r0:r0 + nr].reshape(shapes[n])
        r0 += nr
    return out


def _gathered_to_full(buf, shapes):
    out, r0 = {}, 0
    for n, axis in BIG:
        l, r, c = shapes[n]
        nr = l * r * c // PACK_W
        sh = buf[:, r0:r0 + nr].reshape(N_CHIPS, l, r, c)
        out[n] = (sh.transpose(1, 0, 2, 3).reshape(l, N_CHIPS * r, c) if axis == 1
                  else sh.transpose(1, 2, 0, 3).reshape(l, r, N_CHIPS * c))
        r0 += nr
    return out, r0


def _full_to_slabs(full, shapes):
    parts = []
    for n, axis in BIG:
        l, r, c = shapes[n]
        g = full[n]
        g = (g.reshape(l, N_CHIPS, r, c).transpose(1, 0, 2, 3) if axis == 1
             else g.reshape(l, r, N_CHIPS, c).transpose(2, 0, 1, 3))
        parts.append(g.reshape(N_CHIPS, -1, PACK_W))
    return jnp.concatenate(parts, axis=1)


SMALL = ("ln_mix_g", "ln_mix_b", "ln_ffn_g", "ln_ffn_b")
SMALL_ROWS = 32


def _pack_small(vals, lb):
    rows = [vals[n] for n in SMALL] + [lb.reshape(-1, PACK_W)]
    for n in ("att_sink", "hgrn_norm_g"):
        flat = vals[n].reshape(1, -1)
        rows.append(jnp.pad(flat, ((0, 0), (0, PACK_W - flat.shape[1]))))
    buf = jnp.concatenate(rows, axis=0)
    return jnp.pad(buf, ((0, SMALL_ROWS - buf.shape[0]), (0, 0)))


def _unpack_small(buf, lb_shape):
    out, r0 = {}, 0
    for n in SMALL:
        out[n] = buf[r0:r0 + DEPTH]
        r0 += DEPTH
    nlb = lb_shape[0] * lb_shape[1] * lb_shape[2] // PACK_W
    out["hgrn_lb_logits"] = buf[r0:r0 + nlb].reshape(lb_shape)
    r0 += nlb
    out["att_sink"] = buf[r0, :2 * N_Q_HEADS].reshape(2, N_Q_HEADS)
    out["hgrn_norm_g"] = buf[r0 + 1, :2 * LANES].reshape(2, LANES)
    return out


WEIGHTS = ("att_w_qkv", "att_sink", "att_w_o", "hgrn_w_in", "hgrn_lb_logits", "hgrn_norm_g", "hgrn_w_o", "ln_mix_g",
           "ln_mix_b", "ffn_w_in", "ffn_w_out", "ln_ffn_g", "ln_ffn_b", "ple_w_gate", "ple_w_proj")


def kernel(x, p, att_w_qkv, att_sink, att_w_o, hgrn_w_in, hgrn_lb_logits, hgrn_norm_g, hgrn_w_o, ln_mix_g, ln_mix_b, ffn_w_in, ffn_w_out, ln_ffn_g, ln_ffn_b, ple_w_gate, ple_w_proj, loss_target, m_att_w_qkv, m_att_sink, m_att_w_o, m_hgrn_w_in, m_hgrn_lb_logits, m_hgrn_norm_g, m_hgrn_w_o, m_ln_mix_g, m_ln_mix_b, m_ffn_w_in, m_ffn_w_out, m_ln_ffn_g, m_ln_ffn_b, m_ple_w_gate, m_ple_w_proj, v_att_w_qkv, v_att_sink, v_att_w_o, v_hgrn_w_in, v_hgrn_lb_logits, v_hgrn_norm_g, v_hgrn_w_o, v_ln_mix_g, v_ln_mix_b, v_ffn_w_in, v_ffn_w_out, v_ln_ffn_g, v_ln_ffn_b, v_ple_w_gate, v_ple_w_proj):
    wts = dict(att_w_qkv=att_w_qkv, att_sink=att_sink, att_w_o=att_w_o, hgrn_w_in=hgrn_w_in, hgrn_lb_logits=hgrn_lb_logits,
               hgrn_norm_g=hgrn_norm_g, hgrn_w_o=hgrn_w_o, ln_mix_g=ln_mix_g, ln_mix_b=ln_mix_b, ffn_w_in=ffn_w_in,
               ffn_w_out=ffn_w_out, ln_ffn_g=ln_ffn_g, ln_ffn_b=ln_ffn_b, ple_w_gate=ple_w_gate, ple_w_proj=ple_w_proj)
    mom = dict(att_w_qkv=m_att_w_qkv, att_sink=m_att_sink, att_w_o=m_att_w_o, hgrn_w_in=m_hgrn_w_in, hgrn_lb_logits=m_hgrn_lb_logits,
               hgrn_norm_g=m_hgrn_norm_g, hgrn_w_o=m_hgrn_w_o, ln_mix_g=m_ln_mix_g, ln_mix_b=m_ln_mix_b, ffn_w_in=m_ffn_w_in,
               ffn_w_out=m_ffn_w_out, ln_ffn_g=m_ln_ffn_g, ln_ffn_b=m_ln_ffn_b, ple_w_gate=m_ple_w_gate, ple_w_proj=m_ple_w_proj)
    var = dict(att_w_qkv=v_att_w_qkv, att_sink=v_att_sink, att_w_o=v_att_w_o, hgrn_w_in=v_hgrn_w_in, hgrn_lb_logits=v_hgrn_lb_logits,
               hgrn_norm_g=v_hgrn_norm_g, hgrn_w_o=v_hgrn_w_o, ln_mix_g=v_ln_mix_g, ln_mix_b=v_ln_mix_b, ffn_w_in=v_ffn_w_in,
               ffn_w_out=v_ffn_w_out, ln_ffn_g=v_ln_ffn_g, ln_ffn_b=v_ln_ffn_b, ple_w_gate=v_ple_w_gate, ple_w_proj=v_ple_w_proj)
    shapes = {n: wts[n].shape for n, _ in BIG}
    chip = 2 * lax.axis_index("x") + lax.axis_index("y")
    core = lax.axis_index("c")

    lb_bits = lax.bitcast_convert_type(hgrn_lb_logits.reshape(-1), BF16).reshape(-1, PACK_W)
    packed = jnp.concatenate([_pack_shards({n: wts[n].astype(BF16) for n, _ in BIG}), lb_bits,
                              jnp.zeros((LB_ROWS - lb_bits.shape[0], PACK_W), BF16)], axis=0)
    gathered = _allgather_weights(packed, name="allgather_weights")
    w_full, r_big = _gathered_to_full(gathered, shapes)
    lb_sh = hgrn_lb_logits.shape
    lb_rows = gathered[:, r_big:r_big + lb_bits.shape[0]].reshape(N_CHIPS, -1, 2)
    lb_full = lax.bitcast_convert_type(lb_rows, F32).reshape(N_CHIPS, lb_sh[0], lb_sh[1], lb_sh[2])
    lb_full = lb_full.transpose(1, 2, 0, 3).reshape(lb_sh[0], lb_sh[1], N_CHIPS * lb_sh[2])
    sp = dict(att_sink=att_sink, hgrn_lb_logits=lb_full, hgrn_norm_g=hgrn_norm_g, ln_mix_g=ln_mix_g, ln_mix_b=ln_mix_b,
              ln_ffn_g=ln_ffn_g, ln_ffn_b=ln_ffn_b)

    loss, grad_x, gw, gs = _local_step(x[0], p, loss_target[0], w_full, sp)
    loss = lax.psum(loss[0, 0], ("x", "y", "c"))

    slabs = _full_to_slabs(gw, shapes)
    from_sibling = _pair_exchange(slabs, name="grad_pair_exchange")
    chip_part = _add_own_half(slabs, from_sibling, core.reshape(1).astype(jnp.int32), name="grad_pair_add")
    from_chips = _chip_scatter(chip_part, name="grad_chip_scatter")
    half = _sum_chips(from_chips, name="grad_chip_sum")
    g_big = _unpack_shards(_pair_share(half, name="grad_pair_share"), shapes)

    g_small_full = _unpack_small(_allreduce_small(_pack_small(gs, gs["hgrn_lb_logits"]), name="allreduce_small"),
                                 (lb_sh[0], lb_sh[1], N_CHIPS * lb_sh[2]))
    g_lb = lax.dynamic_slice_in_dim(g_small_full["hgrn_lb_logits"], chip * lb_sh[2], lb_sh[2], axis=2)
    grads = dict(g_big)
    grads.update({n: g_small_full[n] for n in SMALL + ("att_sink", "hgrn_norm_g")})
    grads["hgrn_lb_logits"] = g_lb

    delta, new_m, new_v = {}, {}, {}
    for n, _ in BIG:
        two_d = lambda a: a.reshape(-1, a.shape[-1])
        d, nm, nv = _adamw(two_d(wts[n]), two_d(grads[n]), two_d(mom[n]), two_d(var[n]), name=f"adamw_{n}")
        delta[n], new_m[n], new_v[n] = d.reshape(shapes[n]), nm.reshape(shapes[n]), nv.reshape(shapes[n])
    packs = [_pack_small(src, src["hgrn_lb_logits"]) for src in (wts, grads, mom, var)]
    outs = _adamw(*packs, name="adamw_small")
    for dst, buf in zip((delta, new_m, new_v), outs):
        dst.update(_unpack_small(buf, lb_sh))

    return (loss, grad_x[None], *[grads[n] for n in WEIGHTS], *[delta[n] for n in WEIGHTS],
            *[new_m[n] for n in WEIGHTS], *[new_v[n] for n in WEIGHTS])


def _gather_guest(packed):
    r, w = packed.shape
    hr = r // 2

    def copies(src_ref, out_ref, send_sems, recv_sems, local_sem):
        x, y, c, chips = _place()
        sibling = (x, y, 1 - c)

        def half(cx, cy, hc):
            return out_ref.at[2 * cx + cy, pl.ds(hc * hr, hr), :]

        my_half = src_ref.at[pl.ds(c * hr, hr), :]
        mine = pltpu.make_async_copy(src_ref, out_ref.at[2 * x + y], local_sem)
        first = [_remote(my_half, half(x, y, c), send_sems.at[j], recv_sems.at[j], (*chip, c)) for j, chip in enumerate(chips)]
        passed = [_remote(half(*chip, c), half(*chip, c), send_sems.at[3 + j], recv_sems.at[3 + j], sibling)
                  for j, chip in enumerate(chips)]
        from_chips = [_remote(my_half, half(*chip, c), send_sems.at[j], recv_sems.at[j], (*chip, c)) for j, chip in enumerate(chips)]
        from_sibling = [_remote(my_half, half(*chip, 1 - c), send_sems.at[3 + j], recv_sems.at[3 + j], sibling)
                        for j, chip in enumerate(chips)]
        return mine, first, passed, from_chips, from_sibling

    def start(gi, go, gs):
        mine, first, _, _, _ = copies(gi[0], go[0], *gs)
        mine.start()
        for cp in first:
            cp.start()

    def finish(gi, go, gs):
        mine, first, passed, from_chips, from_sibling = copies(gi[0], go[0], *gs)
        for arrival, onward in zip(from_chips, passed):
            arrival.wait_recv()
            onward.start()
        for arrival in from_sibling:
            arrival.wait_recv()
        for cp in first + passed:
            cp.wait_send()
        mine.wait()

    return _Guest((packed,), (jax.ShapeDtypeStruct((N_CHIPS, r, w), packed.dtype),),
                  (pltpu.SemaphoreType.DMA((6,)), pltpu.SemaphoreType.DMA((6,)), pltpu.SemaphoreType.DMA), start, finish)


def _pair_exchange_guest(g):
    n, r, w = g.shape
    hr = r // 2

    def copy(g_ref, out_ref, send_sem, recv_sem):
        x, y, c, _ = _place()
        return _remote(g_ref.at[:, pl.ds((1 - c) * hr, hr), :], out_ref, send_sem, recv_sem, (x, y, 1 - c))

    return _Guest((g,), (jax.ShapeDtypeStruct((n, hr, w), g.dtype),), (pltpu.SemaphoreType.DMA, pltpu.SemaphoreType.DMA),
                  lambda gi, go, gs: copy(gi[0], go[0], *gs).start(),
                  lambda gi, go, gs: copy(gi[0], go[0], *gs).wait())


def _chip_scatter_guest(part):
    n, h, w = part.shape

    def copies(p_ref, out_ref, send_sems, recv_sems, local_sem):
        x, y, c, chips = _place()
        me = 2 * x + y
        mine = pltpu.make_async_copy(p_ref.at[me], out_ref.at[me], local_sem)
        sends = [_remote(p_ref.at[2 * cx + cy], out_ref.at[me], send_sems.at[j], recv_sems.at[j], (cx, cy, c))
                 for j, (cx, cy) in enumerate(chips)]
        arrivals = [_remote(p_ref.at[me], out_ref.at[2 * cx + cy], send_sems.at[j], recv_sems.at[j], (cx, cy, c))
                    for j, (cx, cy) in enumerate(chips)]
        return mine, sends, arrivals

    def start(gi, go, gs):
        mine, sends, _ = copies(gi[0], go[0], *gs)
        mine.start()
        for cp in sends:
            cp.start()

    def finish(gi, go, gs):
        mine, sends, arrivals = copies(gi[0], go[0], *gs)
        for cp in arrivals:
            cp.wait_recv()
        for cp in sends:
            cp.wait_send()
        mine.wait()

    return _Guest((part,), (jax.ShapeDtypeStruct((n, h, w), part.dtype),),
                  (pltpu.SemaphoreType.DMA((3,)), pltpu.SemaphoreType.DMA((3,)), pltpu.SemaphoreType.DMA), start, finish)


def _pair_share_guest(half):
    h, w = half.shape

    def copies(h_ref, out_ref, send_sem, recv_sem, local_sem):
        x, y, c, _ = _place()
        dst = out_ref.at[pl.ds(c * h, h), :]
        mine = pltpu.make_async_copy(h_ref, dst, local_sem)
        send = _remote(h_ref, dst, send_sem, recv_sem, (x, y, 1 - c))
        arrival = _remote(h_ref, out_ref.at[pl.ds((1 - c) * h, h), :], send_sem, recv_sem, (x, y, 1 - c))
        return mine, send, arrival

    def start(gi, go, gs):
        mine, send, _ = copies(gi[0], go[0], *gs)
        mine.start()
        send.start()

    def finish(gi, go, gs):
        mine, send, arrival = copies(gi[0], go[0], *gs)
        send.wait_send()
        arrival.wait_recv()
        mine.wait()

    return _Guest((half,), (jax.ShapeDtypeStruct((2 * h, w), half.dtype),),
                  (pltpu.SemaphoreType.DMA, pltpu.SemaphoreType.DMA, pltpu.SemaphoreType.DMA), start, finish)


AXIS = dict(BIG)


def _layer_parts(i):
    j = i // 2
    mixer = (("att_w_qkv", j), ("att_w_o", j)) if i % 2 == 0 else (("hgrn_w_in", j), ("hgrn_w_o", j))
    return mixer + (("ffn_w_in", i), ("ffn_w_out", i), ("ple_w_gate", i), ("ple_w_proj", i))


def _pack_layer(shards, i):
    return jnp.concatenate([shards[n][l].reshape(-1, PACK_W) for n, l in _layer_parts(i)], axis=0)


def _unpack_layer(buf, i, shapes):
    out, r0 = {}, 0
    for n, _ in _layer_parts(i):
        r, c = shapes[n][1:]
        nr = r * c // PACK_W
        out[n] = buf[r0:r0 + nr].reshape(r, c)
        r0 += nr
    return out


def _gathered_layer(buf, i, shapes):
    out, r0 = {}, 0
    for n, _ in _layer_parts(i):
        r, c = shapes[n][1:]
        nr = r * c // PACK_W
        sh = buf[:, r0:r0 + nr].reshape(N_CHIPS, r, c)
        out[n] = sh.reshape(N_CHIPS * r, c) if AXIS[n] == 1 else sh.transpose(1, 0, 2).reshape(r, N_CHIPS * c)
        r0 += nr
    return out, r0


def _layer_slabs(full, i, shapes):
    parts = []
    for n, _ in _layer_parts(i):
        r, c = shapes[n][1:]
        g = full[n]
        g = g.reshape(N_CHIPS, -1, PACK_W) if AXIS[n] == 1 else g.reshape(r, N_CHIPS, c).transpose(1, 0, 2).reshape(N_CHIPS, -1, PACK_W)
        parts.append(g)
    return jnp.concatenate(parts, axis=1)


def kernel(x, p, att_w_qkv, att_sink, att_w_o, hgrn_w_in, hgrn_lb_logits, hgrn_norm_g, hgrn_w_o, ln_mix_g, ln_mix_b, ffn_w_in, ffn_w_out, ln_ffn_g, ln_ffn_b, ple_w_gate, ple_w_proj, loss_target, m_att_w_qkv, m_att_sink, m_att_w_o, m_hgrn_w_in, m_hgrn_lb_logits, m_hgrn_norm_g, m_hgrn_w_o, m_ln_mix_g, m_ln_mix_b, m_ffn_w_in, m_ffn_w_out, m_ln_ffn_g, m_ln_ffn_b, m_ple_w_gate, m_ple_w_proj, v_att_w_qkv, v_att_sink, v_att_w_o, v_hgrn_w_in, v_hgrn_lb_logits, v_hgrn_norm_g, v_hgrn_w_o, v_ln_mix_g, v_ln_mix_b, v_ffn_w_in, v_ffn_w_out, v_ln_ffn_g, v_ln_ffn_b, v_ple_w_gate, v_ple_w_proj):
    wts = dict(att_w_qkv=att_w_qkv, att_sink=att_sink, att_w_o=att_w_o, hgrn_w_in=hgrn_w_in, hgrn_lb_logits=hgrn_lb_logits,
               hgrn_norm_g=hgrn_norm_g, hgrn_w_o=hgrn_w_o, ln_mix_g=ln_mix_g, ln_mix_b=ln_mix_b, ffn_w_in=ffn_w_in,
               ffn_w_out=ffn_w_out, ln_ffn_g=ln_ffn_g, ln_ffn_b=ln_ffn_b, ple_w_gate=ple_w_gate, ple_w_proj=ple_w_proj)
    mom = dict(att_w_qkv=m_att_w_qkv, att_sink=m_att_sink, att_w_o=m_att_w_o, hgrn_w_in=m_hgrn_w_in, hgrn_lb_logits=m_hgrn_lb_logits,
               hgrn_norm_g=m_hgrn_norm_g, hgrn_w_o=m_hgrn_w_o, ln_mix_g=m_ln_mix_g, ln_mix_b=m_ln_mix_b, ffn_w_in=m_ffn_w_in,
               ffn_w_out=m_ffn_w_out, ln_ffn_g=m_ln_ffn_g, ln_ffn_b=m_ln_ffn_b, ple_w_gate=m_ple_w_gate, ple_w_proj=m_ple_w_proj)
    var = dict(att_w_qkv=v_att_w_qkv, att_sink=v_att_sink, att_w_o=v_att_w_o, hgrn_w_in=v_hgrn_w_in, hgrn_lb_logits=v_hgrn_lb_logits,
               hgrn_norm_g=v_hgrn_norm_g, hgrn_w_o=v_hgrn_w_o, ln_mix_g=v_ln_mix_g, ln_mix_b=v_ln_mix_b, ffn_w_in=v_ffn_w_in,
               ffn_w_out=v_ffn_w_out, ln_ffn_g=v_ln_ffn_g, ln_ffn_b=v_ln_ffn_b, ple_w_gate=v_ple_w_gate, ple_w_proj=v_ple_w_proj)
    shapes = {n: wts[n].shape for n, _ in BIG}
    chip = 2 * lax.axis_index("x") + lax.axis_index("y")
    core = lax.axis_index("c").reshape(1).astype(jnp.int32)
    xin, target = x[0], loss_target[0]
    t = xin.shape[0]
    row = lambda a, i: a[i][None]

    bf_shards = {n: wts[n].astype(BF16) for n, _ in BIG}
    lb_sh = hgrn_lb_logits.shape
    lb_bits = lax.bitcast_convert_type(hgrn_lb_logits.reshape(-1), BF16).reshape(-1, PACK_W)
    packed = [_pack_layer(bf_shards, i) for i in range(DEPTH)]
    packed[0] = jnp.concatenate([packed[0], lb_bits, jnp.zeros((LB_ROWS - lb_bits.shape[0], PACK_W), BF16)], axis=0)

    gathered = _run_guest(_gather_guest(packed[0]), name="gather_layer_0")[0]
    w, r_big = _gathered_layer(gathered, 0, shapes)
    lb_rows = gathered[:, r_big:r_big + lb_bits.shape[0]].reshape(N_CHIPS, -1, 2)
    lb_full = lax.bitcast_convert_type(lb_rows, F32).reshape(N_CHIPS, lb_sh[0], lb_sh[1], lb_sh[2])
    lb_full = lb_full.transpose(1, 2, 0, 3).reshape(lb_sh[0], lb_sh[1], N_CHIPS * lb_sh[2])
    logits2d = lb_full.reshape(DEPTH, 2 * D_MODEL)
    lb_all = _lower_bounds(logits2d, name="lb_fwd").reshape(DEPTH, 2, 1, D_MODEL)
    tables = _rope_tables(t)

    saved, weights = [], []
    xf, xb = xin, xin.astype(BF16)
    for i in range(DEPTH):
        j = i // 2
        nxt = _gather_guest(packed[i + 1]) if i + 1 < DEPTH else None
        s = dict(xin_bf=xb)
        if i % 2 == 0:
            q, k, v = _qkv_rope(xb, w["att_w_qkv"], tables, name=f"qkv_rope_{i}")
            (o, lse), got = _attn_fwd(q, k, v, att_sink[j], name=f"attn_fwd_{i}", guest=nxt)
            s.update(q=q, k=k, v=v, o=o, lse=lse)
            mix_in, w_o = o, w["att_w_o"]
        else:
            z = _mm_nn(xb, w["hgrn_w_in"], out_dtype=F32, name=f"hgrn_in_{i}")
            (o_f, s_f), got = _hgrn_fwd(z, lb_all[i, 0], False, name=f"hgrn_scan_f_{i}", guest=nxt)
            (o_b, s_b), _ = _hgrn_fwd(z, lb_all[i, 1], True, name=f"hgrn_scan_b_{i}")
            og = _hgrn_post_fwd(o_f, o_b, z, row(hgrn_norm_g, j), name=f"hgrn_post_{i}")
            s.update(z=z, o_f=o_f, o_b=o_b, s_f=s_f, s_b=s_b, og=og)
            mix_in, w_o = og, w["hgrn_w_o"]
        x1, x1b, xh1, rs1 = _mm_res_ln(mix_in, w_o, xf, row(ln_mix_g, i), row(ln_mix_b, i), name=f"mix_out_ln_{i}")
        g, u, h = _ffn_in(x1b, w["ffn_w_in"], name=f"ffn_in_{i}")
        x2, x2b, xh2, rs2 = _mm_res_ln(h, w["ffn_w_out"], x1, row(ln_ffn_g, i), row(ln_ffn_b, i), name=f"ffn_out_ln_{i}")
        xf, xb, gate, pp = _ple_fwd(x2, x2b, p, i, w["ple_w_gate"], w["ple_w_proj"], name=f"ple_fwd_{i}")
        s.update(x1b=x1b, xh1=xh1, rs1=rs1, g=g, u=u, h=h, x2b=x2b, xh2=xh2, rs2=rs2, gate=gate, pp=pp)
        saved.append(s)
        weights.append(w)
        if nxt is not None:
            w, _ = _gathered_layer(got[0], i + 1, shapes)

    loss, dx = _loss_head(xf, target, name="loss_head")
    loss = lax.psum(loss[0, 0], ("x", "y", "c"))

    gs = {n: [None] * DEPTH for n in SMALL}
    gs.update(att_sink=[None] * 2, hgrn_norm_g=[None] * 2)
    dlb = [jnp.zeros((2, D_MODEL), F32)] * DEPTH
    reduced = [None] * DEPTH
    slabs = None
    for i in reversed(range(DEPTH)):
        j = i // 2
        s, w = saved[i], weights[i]
        gw = {}
        res, got = _ple_bwd(dx, s["gate"], s["pp"], w["ple_w_gate"], s["xh2"], s["rs2"], row(ln_ffn_g, i), name=f"ple_bwd_{i}",
                            guest=None if slabs is None else _pair_exchange_guest(slabs))
        du2, du2b, dpre, dpp, gs["ln_ffn_g"][i], gs["ln_ffn_b"][i] = res
        part = None if slabs is None else _add_own_half(slabs, got[0], core, name=f"grad_pair_add_{i + 1}")
        gw["ple_w_gate"] = _mm_tn(s["x2b"], dpre, name=f"dw_ple_gate_{i}")
        gw["ple_w_proj"] = _mm_tn_p(p, i, dpp, name=f"dw_ple_proj_{i}")
        dgg, dgu = _ffn_out_bwd(du2b, w["ffn_w_out"], s["g"], s["u"], name=f"ffn_out_bwd_{i}")
        gw["ffn_w_out"] = _mm_tn(s["h"], du2b, name=f"dw_ffn_out_{i}")
        res = _mm_nt([dgg, dgu], w["ffn_w_in"], resid=du2, ln=(s["xh1"], s["rs1"], row(ln_mix_g, i)),
                     name=f"ffn_in_bwd_{i}", guest=None if part is None else _chip_scatter_guest(part))
        (du1, du1b, gs["ln_mix_g"][i], gs["ln_mix_b"][i]), got = res if part is not None else (res, None)
        half = None if part is None else _sum_chips(got[0], name=f"grad_chip_sum_{i + 1}")
        res = _mm_tn(s["x1b"], dgg, name=f"dw_ffn_gate_{i}", guest=None if half is None else _pair_share_guest(half))
        dw_gate, got = res if half is not None else (res, None)
        if half is not None:
            reduced[i + 1] = got[0]
        gw["ffn_w_in"] = jnp.concatenate([dw_gate, _mm_tn(s["x1b"], dgu, name=f"dw_ffn_up_{i}")], axis=1)
        if i % 2 == 0:
            gw["att_w_o"] = _mm_tn(s["o"], du1b, name=f"dw_att_o_{i}")
            do = _mm_nt([du1b], w["att_w_o"], out_dtype=BF16, name=f"att_o_bwd_{i}")
            dq, dkw, dvw, gs["att_sink"][j] = _attn_bwd(s["q"], s["k"], s["v"], s["o"], do, s["lse"], att_sink[j], name=f"attn_bwd_{i}")
            dqkv = _attn_post_bwd(dq, dkw, dvw, tables, name=f"attn_post_bwd_{i}")
            gw["att_w_qkv"] = _mm_tn(s["xin_bf"], dqkv, name=f"dw_att_qkv_{i}")
            dx = _mm_nt([dqkv], w["att_w_qkv"], resid=du1, name=f"qkv_bwd_{i}")
        else:
            gw["hgrn_w_o"] = _mm_tn(s["og"], du1b, name=f"dw_hgrn_o_{i}")
            dy = _mm_nt([du1b], w["hgrn_w_o"], out_dtype=BF16, name=f"hgrn_o_bwd_{i}")
            d_o, dgate, gs["hgrn_norm_g"][j] = _hgrn_post_bwd(dy, s["o_f"], s["o_b"], s["z"], row(hgrn_norm_g, j), name=f"hgrn_post_bwd_{i}")
            dq_f, dzf_f, dv_f, dlb_f = _hgrn_bwd(s["z"], lb_all[i, 0], d_o, s["s_f"], False, name=f"hgrn_scan_f_bwd_{i}")
            dq_b, dzf_b, dv_b, dlb_b = _hgrn_bwd(s["z"], lb_all[i, 1], d_o, s["s_b"], True, name=f"hgrn_scan_b_bwd_{i}")
            dlb[i] = jnp.stack([dlb_f.reshape(D_MODEL), dlb_b.reshape(D_MODEL)])
            dz = _hgrn_combine(dq_f, dq_b, dzf_f, dzf_b, dv_f, dv_b, dgate, name=f"hgrn_combine_{i}")
            gw["hgrn_w_in"] = _mm_tn(s["xin_bf"], dz, name=f"dw_hgrn_in_{i}")
            dx = _mm_nt([dz], w["hgrn_w_in"], resid=du1, name=f"hgrn_in_bwd_{i}")
        slabs = _layer_slabs(gw, i, shapes)

    from_sibling = _run_guest(_pair_exchange_guest(slabs), name="grad_pair_exchange_0")[0]
    part = _add_own_half(slabs, from_sibling, core, name="grad_pair_add_0")
    half = _sum_chips(_run_guest(_chip_scatter_guest(part), name="grad_chip_scatter_0")[0], name="grad_chip_sum_0")
    reduced[0] = _run_guest(_pair_share_guest(half), name="grad_pair_share_0")[0]

    g_layers = [_unpack_layer(reduced[i], i, shapes) for i in range(DEPTH)]
    grads = {}
    for n, _ in BIG:
        per_layer = [g_layers[i][n] for i in range(DEPTH) if n in g_layers[i]]
        grads[n] = jnp.stack(per_layer)

    gsmall = {n: jnp.concatenate(v, axis=0) for n, v in gs.items()}
    dlogits = _lower_bounds_bwd(logits2d, jnp.stack(dlb).reshape(DEPTH, 2 * D_MODEL), name="lb_bwd").reshape(DEPTH, 2, D_MODEL)
    g_small_full = _unpack_small(_allreduce_small(_pack_small(gsmall, dlogits), name="allreduce_small"),
                                 (lb_sh[0], lb_sh[1], N_CHIPS * lb_sh[2]))
    grads.update({n: g_small_full[n] for n in SMALL + ("att_sink", "hgrn_norm_g")})
    grads["hgrn_lb_logits"] = lax.dynamic_slice_in_dim(g_small_full["hgrn_lb_logits"], chip * lb_sh[2], lb_sh[2], axis=2)

    delta, new_m, new_v = {}, {}, {}
    for n, _ in BIG:
        two_d = lambda a: a.reshape(-1, a.shape[-1])
        d, nm, nv = _adamw(two_d(wts[n]), two_d(grads[n]), two_d(mom[n]), two_d(var[n]), name=f"adamw_{n}")
        delta[n], new_m[n], new_v[n] = d.reshape(shapes[n]), nm.reshape(shapes[n]), nv.reshape(shapes[n])
    packs = [_pack_small(src, src["hgrn_lb_logits"]) for src in (wts, grads, mom, var)]
    outs = _adamw(*packs, name="adamw_small")
    for dst, buf in zip((delta, new_m, new_v), outs):
        dst.update(_unpack_small(buf, lb_sh))

    return (loss, dx[None], *[grads[n] for n in WEIGHTS], *[delta[n] for n in WEIGHTS],
            *[new_m[n] for n in WEIGHTS], *[new_v[n] for n in WEIGHTS])
```

```python
import functools
from typing import Callable, NamedTuple

import jax
import jax.numpy as jnp
from jax import lax
from jax.experimental import pallas as pl
from jax.experimental.pallas import tpu as pltpu

F32, BF16 = jnp.float32, jnp.bfloat16

D_MODEL = 1024
DEPTH = 4
HEAD_DIM = 64
N_Q_HEADS = 16
N_KV_HEADS = 4
GROUP = 4
Q_DIM = 1024
KV_DIM = 256
WINDOW = 128
ROPE_DIM = 16
ROPE_THETA = 500000.0
HGRN_HEADS = 8
HGRN_KEY = 128
HGRN_CHUNK = 64
D_FF = 2816
PLE_DIM = 256
ALPHA = (2 * DEPTH) ** 0.25
LN_EPS = 1e-5
ADAM_LR, ADAM_B1, ADAM_B2, ADAM_EPS, ADAM_WD, ADAM_STEP = 0.001, 0.9, 0.999, 1e-08, 0.01, 10

LANES = 128
VMEM_LIMIT_BYTES = 56 * 2**20
FACTORED_DECAY_LIMIT = -80.0

NN = ((1,), (0,))
NT = ((1,), (1,))
TN = ((0,), (0,))

N_CHIPS = 4
MESH = pl.DeviceIdType.MESH


def _dot(a, b, dims):
    return lax.dot_general(a, b, (dims, ((), ())), preferred_element_type=F32)


def _cp(*sem):
    return pltpu.CompilerParams(dimension_semantics=sem, vmem_limit_bytes=VMEM_LIMIT_BYTES)


def _rows(t, cap=512):
    return min(cap, t)


def _pick(n, cap):
    best = None
    for d in range(LANES, min(n, cap) + 1, LANES):
        if n % d == 0:
            best = d
    assert best is not None, (n, cap)
    return best


def _sigmoid(x):
    return jax.nn.sigmoid(x)


def _ln_fwd(u, g, b):
    mu = jnp.mean(u, axis=-1, keepdims=True)
    xc = u - mu
    var = jnp.mean(xc * xc, axis=-1, keepdims=True)
    rstd = lax.rsqrt(var + LN_EPS)
    xhat = xc * rstd
    return xhat * g + b, xhat, rstd


def _ln_bwd(dy, xhat, rstd, g):
    dxh = dy * g
    m1 = jnp.mean(dxh, axis=-1, keepdims=True)
    m2 = jnp.mean(dxh * xhat, axis=-1, keepdims=True)
    du = rstd * (dxh - m1 - xhat * m2)
    return du, jnp.sum(dy * xhat, axis=0, keepdims=True), jnp.sum(dy, axis=0, keepdims=True)


def _full(shape):
    nd = len(shape)
    return pl.BlockSpec(shape, lambda *_: (0,) * nd)


ANY = pl.BlockSpec(memory_space=pl.ANY)


class _Guest(NamedTuple):
    args: tuple
    out_shape: tuple
    sems: tuple
    start: Callable
    finish: Callable


def _call(body, *, grid, in_specs, out_specs, out_shape, args, sem, name, scratch_shapes=(), guest=None):
    n_in, n_out, n_scr = len(args), len(out_shape), len(scratch_shapes)
    if guest is None:
        res = pl.pallas_call(body, grid=grid, in_specs=list(in_specs), out_specs=list(out_specs), out_shape=list(out_shape),
                             scratch_shapes=list(scratch_shapes), compiler_params=_cp(*sem), name=name)(*args)
        return list(res), []
    g_in, g_out = len(guest.args), len(guest.out_shape)

    def hosted(*refs):
        cuts = [n_in, g_in, n_out, g_out, n_scr]
        parts, pos = [], 0
        for n in cuts:
            parts.append(refs[pos:pos + n])
            pos += n
        h_in, gi, h_out, go, h_scr = parts
        gs = refs[pos:]
        ids = [pl.program_id(d) for d in range(len(grid))]
        first = functools.reduce(jnp.logical_and, [i == 0 for i in ids])
        last = functools.reduce(jnp.logical_and, [i == n - 1 for i, n in zip(ids, grid)])

        @pl.when(first)
        def _():
            guest.start(gi, go, gs)
        body(*h_in, *h_out, *h_scr)

        @pl.when(last)
        def _():
            guest.finish(gi, go, gs)

    res = pl.pallas_call(
        hosted, grid=grid, in_specs=list(in_specs) + [ANY] * g_in, out_specs=list(out_specs) + [ANY] * g_out,
        out_shape=list(out_shape) + list(guest.out_shape), scratch_shapes=list(scratch_shapes) + list(guest.sems),
        compiler_params=_cp(*(("arbitrary",) * len(grid))), name=name)(*args, *guest.args)
    return list(res[:n_out]), list(res[n_out:])


def _run_guest(guest, *, name):
    g_in = len(guest.args)

    def body(*refs):
        gi, go, gs = refs[:g_in], refs[g_in:g_in + len(guest.out_shape)], refs[g_in + len(guest.out_shape):]
        guest.start(gi, go, gs)
        guest.finish(gi, go, gs)

    return pl.pallas_call(body, in_specs=[ANY] * g_in, out_specs=[ANY] * len(guest.out_shape), out_shape=list(guest.out_shape),
                          scratch_shapes=list(guest.sems), name=name)(*guest.args)


def _mm_nn(a, w, *, out_dtype, name):
    t, k = a.shape
    n = w.shape[1]
    tm, tn = _rows(t), _pick(n, 1408)

    def body(a_ref, w_ref, o_ref):
        o_ref[...] = _dot(a_ref[...], w_ref[...], NN).astype(out_dtype)

    return pl.pallas_call(
        body, grid=(n // tn, t // tm),
        in_specs=[pl.BlockSpec((tm, k), lambda j, i: (i, 0)), pl.BlockSpec((k, tn), lambda j, i: (0, j))],
        out_specs=pl.BlockSpec((tm, tn), lambda j, i: (i, j)),
        out_shape=jax.ShapeDtypeStruct((t, n), out_dtype),
        compiler_params=_cp("parallel", "parallel"), name=name)(a, w)


def _mm_tn(a, b, *, name, guest=None):
    r, m = a.shape
    n = b.shape[1]
    tr, tm, tn = _rows(r), _pick(m, 1408), _pick(n, 2816)

    def body(a_ref, b_ref, o_ref):
        @pl.when(pl.program_id(2) == 0)
        def _():
            o_ref[...] = jnp.zeros_like(o_ref)
        o_ref[...] += _dot(a_ref[...].astype(BF16), b_ref[...].astype(BF16), TN)

    res, extra = _call(
        body, grid=(m // tm, n // tn, r // tr),
        in_specs=[pl.BlockSpec((tr, tm), lambda i, j, k: (k, i)), pl.BlockSpec((tr, tn), lambda i, j, k: (k, j))],
        out_specs=[pl.BlockSpec((tm, tn), lambda i, j, k: (i, j))],
        out_shape=[jax.ShapeDtypeStruct((m, n), F32)], args=(a, b),
        sem=("parallel", "parallel", "arbitrary"), name=name, guest=guest)
    return res[0] if guest is None else (res[0], extra)


def _mm_tn_p(p, layer, b, *, name):
    t = p.shape[2]
    n = b.shape[1]
    tr = _rows(t)

    def body(a_ref, b_ref, o_ref):
        @pl.when(pl.program_id(0) == 0)
        def _():
            o_ref[...] = jnp.zeros_like(o_ref)
        o_ref[...] += _dot(a_ref[...].astype(BF16), b_ref[...], TN)

    return pl.pallas_call(
        body, grid=(t // tr,),
        in_specs=[pl.BlockSpec((None, None, tr, PLE_DIM), lambda k: (layer, 0, k, 0)),
                  pl.BlockSpec((tr, n), lambda k: (k, 0))],
        out_specs=pl.BlockSpec((PLE_DIM, n), lambda k: (0, 0)),
        out_shape=jax.ShapeDtypeStruct((PLE_DIM, n), F32),
        compiler_params=_cp("arbitrary"), name=name)(p, b)


def _mm_res_ln(a, w, resid, g, b, *, name):
    t, k = a.shape
    tm = _rows(t)

    def body(a_ref, w_ref, r_ref, g_ref, b_ref, y_ref, ybf_ref, xh_ref, rs_ref):
        acc = _dot(a_ref[...], w_ref[...], NN)
        y, xh, rs = _ln_fwd(ALPHA * r_ref[...] + acc, g_ref[...], b_ref[...])
        y_ref[...] = y
        ybf_ref[...] = y.astype(BF16)
        xh_ref[...] = xh
        rs_ref[...] = rs

    row = pl.BlockSpec((tm, D_MODEL), lambda i: (i, 0))
    return pl.pallas_call(
        body, grid=(t // tm,),
        in_specs=[pl.BlockSpec((tm, k), lambda i: (i, 0)), _full((k, D_MODEL)), row, _full((1, D_MODEL)), _full((1, D_MODEL))],
        out_specs=[row, row, row, pl.BlockSpec((tm, 1), lambda i: (i, 0))],
        out_shape=[jax.ShapeDtypeStruct((t, D_MODEL), F32), jax.ShapeDtypeStruct((t, D_MODEL), BF16),
                   jax.ShapeDtypeStruct((t, D_MODEL), F32), jax.ShapeDtypeStruct((t, 1), F32)],
        compiler_params=_cp("parallel"), name=name)(a, w, resid, g, b)


def _ffn_in(x_bf, w, *, name):
    t = x_bf.shape[0]
    tm, tn = _rows(t), _pick(D_FF, 1408)
    nb = D_FF // tn

    def body(x_ref, wg_ref, wu_ref, dg_ref, du_ref, h_ref):
        x = x_ref[...]
        g = _dot(x, wg_ref[...], NN)
        u = _dot(x, wu_ref[...], NN)
        sig = _sigmoid(g)
        silu = g * sig
        dg_ref[...] = (u * sig * (1.0 + g * (1.0 - sig))).astype(BF16)
        du_ref[...] = silu.astype(BF16)
        h_ref[...] = (silu * u).astype(BF16)

    tile = pl.BlockSpec((tm, tn), lambda j, i: (i, j))
    shp = jax.ShapeDtypeStruct((t, D_FF), BF16)
    return pl.pallas_call(
        body, grid=(nb, t // tm),
        in_specs=[pl.BlockSpec((tm, D_MODEL), lambda j, i: (i, 0)),
                  pl.BlockSpec((D_MODEL, tn), lambda j, i: (0, j)),
                  pl.BlockSpec((D_MODEL, tn), lambda j, i: (0, j + nb))],
        out_specs=[tile, tile, tile], out_shape=[shp, shp, shp],
        compiler_params=_cp("parallel", "parallel"), name=name)(x_bf, w, w)


def _ple_fwd(x, x_bf, p, layer, wg, wp, *, name):
    t = x.shape[0]
    tm = _rows(t)

    def body(x_ref, xbf_ref, p_ref, wg_ref, wp_ref, y_ref, ybf_ref, gate_ref, pp_ref):
        gate = _sigmoid(_dot(xbf_ref[...], wg_ref[...], NN))
        pp = _dot(p_ref[...].astype(BF16), wp_ref[...], NN)
        y = x_ref[...] + gate * pp
        y_ref[...] = y
        ybf_ref[...] = y.astype(BF16)
        gate_ref[...] = gate.astype(BF16)
        pp_ref[...] = pp.astype(BF16)

    row = pl.BlockSpec((tm, D_MODEL), lambda i: (i, 0))
    f32, bf = jax.ShapeDtypeStruct((t, D_MODEL), F32), jax.ShapeDtypeStruct((t, D_MODEL), BF16)
    return pl.pallas_call(
        body, grid=(t // tm,),
        in_specs=[row, row, pl.BlockSpec((None, None, tm, PLE_DIM), lambda i: (layer, 0, i, 0)),
                  _full((D_MODEL, D_MODEL)), _full((PLE_DIM, D_MODEL))],
        out_specs=[row, row, row, row], out_shape=[f32, bf, bf, bf],
        compiler_params=_cp("parallel"), name=name)(x, x_bf, p, wg, wp)


def _loss_head(y, target, *, name):
    t = y.shape[0]
    tm = _rows(t)

    def body(y_ref, t_ref, loss_ref, dy_ref):
        e = y_ref[...] - t_ref[...]

        @pl.when(pl.program_id(0) == 0)
        def _():
            loss_ref[...] = jnp.zeros_like(loss_ref)
        sq = jnp.sum(jnp.sum(e * e, axis=1, keepdims=True), axis=0, keepdims=True)
        loss_ref[...] += sq * (0.5 / D_MODEL)
        dy_ref[...] = e * (1.0 / D_MODEL)

    row = pl.BlockSpec((tm, D_MODEL), lambda i: (i, 0))
    return pl.pallas_call(
        body, grid=(t // tm,), in_specs=[row, row],
        out_specs=[_full((1, 1)), row],
        out_shape=[jax.ShapeDtypeStruct((1, 1), F32), jax.ShapeDtypeStruct((t, D_MODEL), F32)],
        compiler_params=_cp("arbitrary"), name=name)(y, target)


def _ple_bwd(dx3, gate, pp, wg, xhat, rstd, g, *, name, guest=None):
    t = dx3.shape[0]
    tm = _rows(t)

    def body(dx_ref, gate_ref, pp_ref, wg_ref, xh_ref, rs_ref, g_ref,
             du_ref, dubf_ref, dpre_ref, dpp_ref, dg_ref, db_ref):
        dx = dx_ref[...]
        gate = gate_ref[...].astype(F32)
        dpre = (dx * pp_ref[...].astype(F32) * gate * (1.0 - gate)).astype(BF16)
        dpre_ref[...] = dpre
        dpp_ref[...] = (dx * gate).astype(BF16)
        dx2 = dx + _dot(dpre, wg_ref[...], NT)
        du, dg, db = _ln_bwd(dx2, xh_ref[...], rs_ref[...], g_ref[...])
        du_ref[...] = du
        dubf_ref[...] = du.astype(BF16)

        @pl.when(pl.program_id(0) == 0)
        def _():
            dg_ref[...] = jnp.zeros_like(dg_ref)
            db_ref[...] = jnp.zeros_like(db_ref)
        dg_ref[...] += dg
        db_ref[...] += db

    row = pl.BlockSpec((tm, D_MODEL), lambda i: (i, 0))
    vec = _full((1, D_MODEL))
    f32, bf = jax.ShapeDtypeStruct((t, D_MODEL), F32), jax.ShapeDtypeStruct((t, D_MODEL), BF16)
    v32 = jax.ShapeDtypeStruct((1, D_MODEL), F32)
    res, extra = _call(
        body, grid=(t // tm,),
        in_specs=[row, row, row, _full((D_MODEL, D_MODEL)), row, pl.BlockSpec((tm, 1), lambda i: (i, 0)), vec],
        out_specs=[row, row, row, row, vec, vec], out_shape=[f32, bf, bf, bf, v32, v32],
        args=(dx3, gate, pp, wg, xhat, rstd, g), sem=("arbitrary",), name=name, guest=guest)
    return res, extra


def _ffn_out_bwd(du_bf, w_out, g, u, *, name):
    t = du_bf.shape[0]
    tm, tn = _rows(t), _pick(D_FF, 1408)

    def body(du_ref, w_ref, hg_ref, hu_ref, dg_ref, dup_ref):
        dh = _dot(du_ref[...], w_ref[...], NT)
        dg_ref[...] = (dh * hg_ref[...].astype(F32)).astype(BF16)
        dup_ref[...] = (dh * hu_ref[...].astype(F32)).astype(BF16)

    tile = pl.BlockSpec((tm, tn), lambda j, i: (i, j))
    shp = jax.ShapeDtypeStruct((t, D_FF), BF16)
    return pl.pallas_call(
        body, grid=(D_FF // tn, t // tm),
        in_specs=[pl.BlockSpec((tm, D_MODEL), lambda j, i: (i, 0)), pl.BlockSpec((tn, D_MODEL), lambda j, i: (j, 0)),
                  tile, tile],
        out_specs=[tile, tile], out_shape=[shp, shp],
        compiler_params=_cp("parallel", "parallel"), name=name)(du_bf, w_out, g, u)


def _mm_nt(parts, w, *, resid=None, ln=None, out_dtype=F32, name, guest=None):
    t, kp = parts[0].shape
    npart = len(parts)
    tm = _rows(t)
    n_in = npart + 1 + (resid is not None) + (3 if ln is not None else 0)

    def body(*refs):
        a_refs, w_ref = refs[:npart], refs[npart]
        pos = npart + 1
        r_ref = None
        if resid is not None:
            r_ref = refs[pos]
            pos += 1
        if ln is not None:
            xh_ref, rs_ref, g_ref = refs[pos:pos + 3]
        outs = refs[n_in:]
        val = _dot(a_refs[0][...], w_ref[:, :kp], NT)
        for pi in range(1, npart):
            val = val + _dot(a_refs[pi][...], w_ref[:, pi * kp:(pi + 1) * kp], NT)
        if r_ref is not None:
            val = val + ALPHA * r_ref[...]
        if ln is None:
            outs[0][...] = val.astype(out_dtype)
        else:
            du, dg, db = _ln_bwd(val, xh_ref[...], rs_ref[...], g_ref[...])
            outs[0][...] = du
            outs[1][...] = du.astype(BF16)

            @pl.when(pl.program_id(0) == 0)
            def _():
                outs[2][...] = jnp.zeros_like(outs[2])
                outs[3][...] = jnp.zeros_like(outs[3])
            outs[2][...] += dg
            outs[3][...] += db

    row = pl.BlockSpec((tm, D_MODEL), lambda i: (i, 0))
    vec = pl.BlockSpec((1, D_MODEL), lambda i: (0, 0))
    in_specs = [pl.BlockSpec((tm, kp), lambda i: (i, 0)) for _ in range(npart)]
    in_specs.append(pl.BlockSpec((D_MODEL, npart * kp), lambda i: (0, 0), pipeline_mode=pl.Buffered(1)))
    args = list(parts) + [w]
    if resid is not None:
        in_specs.append(row)
        args.append(resid)
    if ln is not None:
        in_specs += [row, pl.BlockSpec((tm, 1), lambda i: (i, 0)), vec]
        args += list(ln)
        out_specs = [row, row, vec, vec]
        out_shape = [jax.ShapeDtypeStruct((t, D_MODEL), F32), jax.ShapeDtypeStruct((t, D_MODEL), BF16),
                     jax.ShapeDtypeStruct((1, D_MODEL), F32), jax.ShapeDtypeStruct((1, D_MODEL), F32)]
    else:
        out_specs = [row]
        out_shape = [jax.ShapeDtypeStruct((t, D_MODEL), out_dtype)]
    res, extra = _call(
        body, grid=(t // tm,), in_specs=in_specs, out_specs=out_specs, out_shape=out_shape, args=tuple(args),
        sem=("arbitrary" if ln is not None else "parallel",), name=name, guest=guest)
    res = res if ln is not None else res[0]
    return res if guest is None else (res, extra)


def _rope_tables(t):
    half = ROPE_DIM // 2
    inv = ROPE_THETA ** (-jnp.arange(0, ROPE_DIM, 2, dtype=F32) / ROPE_DIM)
    ang = jnp.arange(t, dtype=F32)[:, None] * inv[None, :]
    cos, sin = jnp.cos(ang), jnp.sin(ang)
    pad = HEAD_DIM - ROPE_DIM
    c = jnp.concatenate([cos, cos, jnp.ones((t, pad), F32)], axis=1)
    s_hi = jnp.concatenate([jnp.zeros((t, half), F32), sin, jnp.zeros((t, pad), F32)], axis=1)
    s_lo = jnp.concatenate([-sin, jnp.zeros((t, half + pad), F32)], axis=1)
    rep = LANES // HEAD_DIM
    return jnp.tile(c, (1, rep)), jnp.tile(s_hi, (1, rep)), jnp.tile(s_lo, (1, rep))


def _rope(x, c, s_hi, s_lo, sign):
    half = ROPE_DIM // 2
    parts = []
    for j in range(x.shape[1] // LANES):
        xg = x[:, j * LANES:(j + 1) * LANES]
        rot = pltpu.roll(xg, half, 1) * s_hi + pltpu.roll(xg, LANES - half, 1) * s_lo
        parts.append(xg * c + sign * rot)
    return jnp.concatenate(parts, axis=1)


def _qkv_rope(x_bf, w, tables, *, name):
    t = x_bf.shape[0]
    tm = _rows(t)
    n = Q_DIM + 2 * KV_DIM

    def body(x_ref, w_ref, c_ref, sh_ref, sl_ref, q_ref, k_ref, v_ref):
        acc = _dot(x_ref[...], w_ref[...], NN)
        c, sh, sl = c_ref[...], sh_ref[...], sl_ref[...]
        q_ref[...] = (_rope(acc[:, :Q_DIM], c, sh, sl, 1.0) * HEAD_DIM ** -0.5).astype(BF16)
        k_ref[...] = _rope(acc[:, Q_DIM:Q_DIM + KV_DIM], c, sh, sl, 1.0).astype(BF16)
        v_ref[...] = acc[:, Q_DIM + KV_DIM:].astype(BF16)

    tab = pl.BlockSpec((tm, LANES), lambda i: (i, 0))
    return pl.pallas_call(
        body, grid=(t // tm,),
        in_specs=[pl.BlockSpec((tm, D_MODEL), lambda i: (i, 0)), _full((D_MODEL, n)), tab, tab, tab],
        out_specs=[pl.BlockSpec((tm, Q_DIM), lambda i: (i, 0)), pl.BlockSpec((tm, KV_DIM), lambda i: (i, 0)),
                   pl.BlockSpec((tm, KV_DIM), lambda i: (i, 0))],
        out_shape=[jax.ShapeDtypeStruct((t, Q_DIM), BF16), jax.ShapeDtypeStruct((t, KV_DIM), BF16),
                   jax.ShapeDtypeStruct((t, KV_DIM), BF16)],
        compiler_params=_cp("parallel"), name=name)(x_bf, w, *tables)


ATT_ROWS = 256
ATT_STRIP = 64


def _window_specs(t, tq):
    r = tq // WINDOW
    last = t // WINDOW - 1
    return [pl.BlockSpec((WINDOW, KV_DIM), lambda i: (jnp.maximum(i * r - 1, 0), 0)),
            pl.BlockSpec((tq, KV_DIM), lambda i: (i, 0)),
            pl.BlockSpec((WINDOW, KV_DIM), lambda i: (jnp.minimum((i + 1) * r, last), 0))]


def _att_valid(base, t):
    rows = GROUP * WINDOW
    qpos = base + (lax.broadcasted_iota(jnp.int32, (rows, 3 * WINDOW), 0) & (WINDOW - 1))
    kpos = base - WINDOW + lax.broadcasted_iota(jnp.int32, (rows, 3 * WINDOW), 1)
    return (jnp.abs(qpos - kpos) <= WINDOW) & (kpos >= 0) & (kpos < t)


def _stack_heads(x, r0, g):
    return jnp.concatenate(
        [x[r0:r0 + WINDOW, (GROUP * g + h) * HEAD_DIM:(GROUP * g + h + 1) * HEAD_DIM] for h in range(GROUP)], axis=0)


def _sink_column(sink_ref, g):
    return jnp.concatenate([jnp.full((WINDOW, 1), sink_ref[GROUP * g + h], F32) for h in range(GROUP)], axis=0)


def _unstack_heads(pieces):
    return jnp.concatenate([pieces[g][h * WINDOW:(h + 1) * WINDOW] for g in range(N_KV_HEADS) for h in range(GROUP)], axis=1)


def _attn_fwd(q, k, v, sink, *, name, guest=None):
    t = q.shape[0]
    tq = min(ATT_ROWS, t)
    nsb = tq // WINDOW

    def body(sink_ref, q_ref, kp_ref, kc_ref, kn_ref, vp_ref, vc_ref, vn_ref, o_ref, l_ref):
        i = pl.program_id(0)
        qv = q_ref[...]
        kw = jnp.concatenate([kp_ref[...], kc_ref[...], kn_ref[...]], axis=0)
        vw = jnp.concatenate([vp_ref[...], vc_ref[...], vn_ref[...]], axis=0)
        ones = jnp.ones((3 * WINDOW, HEAD_DIM), BF16)
        for sb in range(nsb):
            r0 = sb * WINDOW
            bias =jnp.where(_att_valid(i * tq + r0, t)[:WINDOW], 0.0, -1e30)
            pieces = []
            for hh in range(N_Q_HEADS):
                g, h = hh // GROUP, hh % GROUP
                qh = qv[r0:r0 + WINDOW, hh * HEAD_DIM:(hh + 1) * HEAD_DIM]
                kg = kw[r0:r0 + 3 * WINDOW, g * HEAD_DIM:(g + 1) * HEAD_DIM]
                vg = jnp.concatenate([vw[r0:r0 + 3 * WINDOW, g * HEAD_DIM:(g + 1) * HEAD_DIM], ones], axis=1)
                s = _dot(qh, kg, NT) + bias
                snk = sink_ref[hh]
                m = jnp.maximum(jnp.max(s, axis=1, keepdims=True), snk)
                ov = _dot(jnp.exp(s - m).astype(BF16), vg, NN)
                den = ov[:, HEAD_DIM:HEAD_DIM + 1] + jnp.exp(snk - m)
                pieces.append(ov[:, :HEAD_DIM] * (1.0 / den))
                l_ref[g, sb, h * WINDOW:(h + 1) * WINDOW, :] = m + jnp.log(den)
            o_ref[r0:r0 + WINDOW, :] = jnp.concatenate(pieces, axis=1).astype(BF16)

    return _call(
        body, grid=(t // tq,),
        in_specs=[pl.BlockSpec(memory_space=pltpu.SMEM), pl.BlockSpec((tq, Q_DIM), lambda i: (i, 0))]
        + _window_specs(t, tq) + _window_specs(t, tq),
        out_specs=[pl.BlockSpec((tq, Q_DIM), lambda i: (i, 0)),
                   pl.BlockSpec((N_KV_HEADS, nsb, GROUP * WINDOW, 1), lambda i: (0, i, 0, 0))],
        out_shape=[jax.ShapeDtypeStruct((t, Q_DIM), BF16),
                   jax.ShapeDtypeStruct((N_KV_HEADS, t // WINDOW, GROUP * WINDOW, 1), F32)],
        args=(sink, q, k, k, k, v, v, v), sem=("parallel",), name=name, guest=guest)


def _attn_bwd(q, k, v, o, do, lse, sink, *, name):
    t = q.shape[0]
    tq = min(ATT_ROWS, t)
    nsb = tq // WINDOW

    def body(sink_ref, q_ref, o_ref, do_ref, l_ref, kp_ref, kc_ref, kn_ref, vp_ref, vc_ref, vn_ref,
             dq_ref, dkw_ref, dvw_ref, dsink_ref):
        i = pl.program_id(0)
        qv, ov, dov = q_ref[...], o_ref[...], do_ref[...]
        kw = jnp.concatenate([kp_ref[...], kc_ref[...], kn_ref[...]], axis=0)
        vw = jnp.concatenate([vp_ref[...], vc_ref[...], vn_ref[...]], axis=0)
        lane16 = lax.broadcasted_iota(jnp.int32, (1, N_Q_HEADS), 1)
        dsink = jnp.zeros((1, N_Q_HEADS), F32)
        for sb in range(nsb):
            r0 = sb * WINDOW
            valid = _att_valid(i * tq + r0, t)
            dq_p, dk_p, dv_p = [], [], []
            for g in range(N_KV_HEADS):
                qs, dos = _stack_heads(qv, r0, g), _stack_heads(dov, r0, g)
                delta = jnp.sum(dos.astype(F32) * _stack_heads(ov, r0, g).astype(F32), axis=1, keepdims=True)
                kg = kw[r0:r0 + 3 * WINDOW, g * HEAD_DIM:(g + 1) * HEAD_DIM]
                vg = vw[r0:r0 + 3 * WINDOW, g * HEAD_DIM:(g + 1) * HEAD_DIM]
                lse_col = l_ref[g, sb]
                pr = jnp.where(valid, jnp.exp(_dot(qs, kg, NT) - lse_col), 0.0)
                ds = (pr * (_dot(dos, vg, NT) - delta)).astype(BF16)
                dq_p.append(_dot(ds, kg, NN))
                dk_p.append(_dot(ds, qs, TN))
                dv_p.append(_dot(pr.astype(BF16), dos, TN))
                ps = jnp.exp(_sink_column(sink_ref, g) - lse_col) * delta
                for h in range(GROUP):
                    tot = jnp.sum(ps[h * WINDOW:(h + 1) * WINDOW], axis=0, keepdims=True)
                    dsink = dsink - jnp.where(lane16 == GROUP * g + h, tot, 0.0)
            dq_ref[r0:r0 + WINDOW, :] = _unstack_heads(dq_p)
            dkw_ref[sb] = jnp.concatenate(dk_p, axis=1)
            dvw_ref[sb] = jnp.concatenate(dv_p, axis=1)

        @pl.when(i == 0)
        def _():
            dsink_ref[...] = jnp.zeros_like(dsink_ref)
        dsink_ref[...] += dsink

    rowq = pl.BlockSpec((tq, Q_DIM), lambda i: (i, 0))
    win = pl.BlockSpec((nsb, 3 * WINDOW, KV_DIM), lambda i: (i, 0, 0))
    wshape = jax.ShapeDtypeStruct((t // WINDOW, 3 * WINDOW, KV_DIM), F32)
    return pl.pallas_call(
        body, grid=(t // tq,),
        in_specs=[pl.BlockSpec(memory_space=pltpu.SMEM), rowq, rowq, rowq,
                  pl.BlockSpec((N_KV_HEADS, nsb, GROUP * WINDOW, 1), lambda i: (0, i, 0, 0))]
        + _window_specs(t, tq) + _window_specs(t, tq),
        out_specs=[rowq, win, win, _full((1, N_Q_HEADS))],
        out_shape=[jax.ShapeDtypeStruct((t, Q_DIM), F32), wshape, wshape, jax.ShapeDtypeStruct((1, N_Q_HEADS), F32)],
        compiler_params=_cp("arbitrary"), name=name)(sink, q, o, do, lse, k, k, k, v, v, v)


def _attn_post_bwd(dq, dkw, dvw, tables, *, name):
    t = dq.shape[0]
    nb = t // WINDOW
    n = Q_DIM + 2 * KV_DIM

    def body(dq_ref, ka_ref, kb_ref, kc_ref, va_ref, vb_ref, vc_ref, c_ref, sh_ref, sl_ref, o_ref):
        j = pl.program_id(0)
        lo = (j > 0).astype(F32)
        hi = (j < nb - 1).astype(F32)
        c, sh, sl = c_ref[...], sh_ref[...], sl_ref[...]
        dk = ka_ref[...] * hi + kb_ref[...] + kc_ref[...] * lo
        dv = va_ref[...] * hi + vb_ref[...] + vc_ref[...] * lo
        o_ref[:, :Q_DIM] = (_rope(dq_ref[...], c, sh, sl, -1.0) * HEAD_DIM ** -0.5).astype(BF16)
        o_ref[:, Q_DIM:Q_DIM + KV_DIM] = _rope(dk, c, sh, sl, -1.0).astype(BF16)
        o_ref[:, Q_DIM + KV_DIM:] = dv.astype(BF16)

    wins = [pl.BlockSpec((None, WINDOW, KV_DIM), lambda j: (jnp.minimum(j + 1, nb - 1), 0, 0)),
            pl.BlockSpec((None, WINDOW, KV_DIM), lambda j: (j, 1, 0)),
            pl.BlockSpec((None, WINDOW, KV_DIM), lambda j: (jnp.maximum(j - 1, 0), 2, 0))]
    tab = pl.BlockSpec((WINDOW, LANES), lambda j: (j, 0))
    return pl.pallas_call(
        body, grid=(nb,),
        in_specs=[pl.BlockSpec((WINDOW, Q_DIM), lambda j: (j, 0))] + wins + wins + [tab, tab, tab],
        out_specs=pl.BlockSpec((WINDOW, n), lambda j: (j, 0)),
        out_shape=jax.ShapeDtypeStruct((t, n), BF16),
        compiler_params=_cp("parallel"), name=name)(dq, dkw, dkw, dkw, dvw, dvw, dvw, *tables)


HGRN_ROWS = 512
HGRN_UNROLL = 2


def _cum_mask(rev, transpose=False):
    row = lax.broadcasted_iota(jnp.int32, (HGRN_CHUNK, HGRN_CHUNK), 0)
    col = lax.broadcasted_iota(jnp.int32, (HGRN_CHUNK, HGRN_CHUNK), 1)
    return (row <= col) if rev != transpose else (row >= col)


def _gates(zq, zf, lb):
    q = zq * _sigmoid(zq)
    sig = _sigmoid(zf)
    f = lb + (1.0 - lb) * sig
    k = (1.0 - lb) * (1.0 - sig)
    return q, sig, f, k


def _intra_exact(q, k, b, mask):
    rows = lax.broadcasted_iota(jnp.int32, (HGRN_CHUNK, 1), 0)
    cols = lax.broadcasted_iota(jnp.int32, (HGRN_CHUNK, HGRN_CHUNK), 1)

    def it(s, a):
        pick = rows == s
        b_s = jnp.sum(jnp.where(pick, b, 0.0), axis=0, keepdims=True)
        k_s = jnp.sum(jnp.where(pick, k, 0.0), axis=0, keepdims=True)
        col = jnp.sum(q * jnp.exp(jnp.minimum(b - b_s, 0.0)) * k_s, axis=1, keepdims=True)
        return jnp.where(cols == s, col, a)

    a = lax.fori_loop(0, HGRN_CHUNK, it, jnp.zeros((HGRN_CHUNK, HGRN_CHUNK), F32))
    return jnp.where(mask, a, 0.0)


def _intra_exact_bwd(q, k, b, da):
    rows = lax.broadcasted_iota(jnp.int32, (HGRN_CHUNK, 1), 0)
    cols = lax.broadcasted_iota(jnp.int32, (HGRN_CHUNK, HGRN_CHUNK), 1)

    def it(s, carry):
        dq, dk = carry
        pick = rows == s
        b_s = jnp.sum(jnp.where(pick, b, 0.0), axis=0, keepdims=True)
        k_s = jnp.sum(jnp.where(pick, k, 0.0), axis=0, keepdims=True)
        w = jnp.sum(jnp.where(cols == s, da, 0.0), axis=1, keepdims=True) * jnp.exp(jnp.minimum(b - b_s, 0.0))
        dq = dq + w * k_s
        dk = jnp.where(pick, jnp.sum(w * q, axis=0, keepdims=True), dk)
        return dq, dk

    z = jnp.zeros((HGRN_CHUNK, HGRN_KEY), F32)
    return lax.fori_loop(0, HGRN_CHUNK, it, (z, z))


def _chunk_cumsum(g, from_end):
    n = g.shape[0]
    row = lax.broadcasted_iota(jnp.int32, g.shape, 0)
    x, s = g, 1
    while s < n:
        if from_end:
            x = x + jnp.where(row < n - s, pltpu.roll(x, n - s, 0), 0.0)
        else:
            x = x + jnp.where(row >= s, pltpu.roll(x, s, 0), 0.0)
        s *= 2
    return x


def _head(a, h):
    return a[:, h * LANES:(h + 1) * LANES]


def _block_is_mild(zf_ref, lb, nch):
    g = jnp.log(lb + (1.0 - lb) * _sigmoid(zf_ref[...]))
    lows = [jnp.min(jnp.sum(g[c * HGRN_CHUNK:(c + 1) * HGRN_CHUNK], axis=0, keepdims=True)) for c in range(nch)]
    return functools.reduce(jnp.minimum, lows) >= FACTORED_DECAY_LIMIT


def _hgrn_fwd(z, lb, rev, *, name, guest=None):
    t = z.shape[0]
    rb = min(HGRN_ROWS, t)
    nb, nch = t // rb, rb // HGRN_CHUNK
    heads = range(HGRN_HEADS)

    def blk(n):
        return nb - 1 - n if rev else n

    def body(zq_ref, zf_ref, zv_ref, lb_ref, o_ref, s_ref, st_ref):
        @pl.when(pl.program_id(0) == 0)
        def _():
            st_ref[...] = jnp.zeros_like(st_ref)
        lbv = lb_ref[...]
        cmask = _cum_mask(rev)

        def chunk(c, carry, mild):
            cc = nch - 1 - c if rev else c
            sl = pl.ds(pl.multiple_of(cc * HGRN_CHUNK, HGRN_CHUNK), HGRN_CHUNK)
            q, _, f, k = _gates(zq_ref[sl, :], zf_ref[sl, :], lbv)
            v = zv_ref[sl, :].astype(BF16)
            g = jnp.log(f)
            b = _chunk_cumsum(g, rev)
            tot = jnp.sum(g, axis=0, keepdims=True)
            qd = (q * jnp.exp(b)).astype(BF16)
            kd = (k * jnp.exp(tot - b)).astype(BF16)
            etot = jnp.exp(tot)

            def factored():
                kf = (k * jnp.exp(-b)).astype(BF16)
                return tuple(jnp.where(cmask, _dot(_head(qd, h), _head(kf, h), NT), 0.0) for h in heads)

            def exact():
                return tuple(_intra_exact(_head(q, h), _head(k, h), _head(b, h), cmask) for h in heads)

            a = factored() if mild else lax.cond(jnp.min(tot) >= FACTORED_DECAY_LIMIT, factored, exact)
            for h in heads:
                st = st_ref[h]
                st_bf = st.astype(BF16)
                s_ref[h, cc] = st_bf
                o_ref[sl, h * LANES:(h + 1) * LANES] = (
                    _dot(_head(qd, h), st_bf, NT) + _dot(a[h].astype(BF16), _head(v, h), NN))
                st_ref[h] = st * _head(etot, h) + _dot(_head(v, h), _head(kd, h), TN)
            return carry

        lax.cond(_block_is_mild(zf_ref, lbv, nch),
                 lambda: lax.fori_loop(0, nch, functools.partial(chunk, mild=True), 0, unroll=HGRN_UNROLL),
                 lambda: lax.fori_loop(0, nch, functools.partial(chunk, mild=False), 0))

    def col(j):
        return pl.BlockSpec((rb, D_MODEL), lambda n: (blk(n), j))

    return _call(
        body, grid=(nb,),
        in_specs=[col(0), col(2 if rev else 1), col(3), _full((1, D_MODEL))],
        out_specs=[col(0), pl.BlockSpec((HGRN_HEADS, nch, LANES, LANES), lambda n: (0, blk(n), 0, 0))],
        out_shape=[jax.ShapeDtypeStruct((t, D_MODEL), F32),
                   jax.ShapeDtypeStruct((HGRN_HEADS, t // HGRN_CHUNK, LANES, LANES), BF16)],
        scratch_shapes=[pltpu.VMEM((HGRN_HEADS, LANES, LANES), F32)],
        args=(z, z, z, lb), sem=("arbitrary",), name=name, guest=guest)


def _hgrn_bwd(z, lb, d_o, states, rev, *, name, other=None):
    t = z.shape[0]
    rb = min(HGRN_ROWS, t)
    nb, nch = t // rb, rb // HGRN_CHUNK
    heads = range(HGRN_HEADS)
    n_other = 0 if other is None else 2
    acc_dtype = F32 if other is None else BF16

    def blk(n):
        return n if rev else nb - 1 - n

    def body(zq_ref, zf_ref, zv_ref, lb_ref, do_ref, s_ref, *rest):
        oq_ref, ov_ref = rest[:n_other] if other is not None else (None, None)
        dq_ref, dzf_ref, dv_ref, dlb_ref, dst_ref = rest[n_other:]

        @pl.when(pl.program_id(0) == 0)
        def _():
            dst_ref[...] = jnp.zeros_like(dst_ref)
            dlb_ref[...] = jnp.zeros_like(dlb_ref)
        lbv = lb_ref[...]
        cmask = _cum_mask(rev)

        def chunk(c, carry, mild):
            cc = c if rev else nch - 1 - c
            sl = pl.ds(pl.multiple_of(cc * HGRN_CHUNK, HGRN_CHUNK), HGRN_CHUNK)
            zq = zq_ref[sl, :]
            q, sig, f, k = _gates(zq, zf_ref[sl, :], lbv)
            v = zv_ref[sl, :].astype(BF16)
            g = jnp.log(f)
            b = _chunk_cumsum(g, rev)
            tot = jnp.sum(g, axis=0, keepdims=True)
            eb = jnp.exp(b)
            etb = jnp.exp(tot - b)
            etot = jnp.exp(tot)
            qd = (q * eb).astype(BF16)
            kd = (k * etb).astype(BF16)
            do = do_ref[sl, :].astype(BF16)
            da = [jnp.where(cmask, _dot(_head(do, h), _head(v, h), NT), 0.0) for h in heads]

            def factored():
                ke = jnp.exp(-b)
                kf = (k * ke).astype(BF16)
                res = []
                for h in heads:
                    qh, kh = _head(qd, h), _head(kf, h)
                    da_bf = da[h].astype(BF16)
                    dqf, dkf = _dot(da_bf, kh, NN), _dot(da_bf, qh, TN)
                    res.append((jnp.where(cmask, _dot(qh, kh, NT), 0.0), dqf * _head(eb, h), dkf * _head(ke, h),
                                qh.astype(F32) * dqf - kh.astype(F32) * dkf))
                return tuple(res)

            def exact():
                res = []
                for h in heads:
                    qh, kh, bh = _head(q, h), _head(k, h), _head(b, h)
                    dq_i, dk_i = _intra_exact_bwd(qh, kh, bh, da[h])
                    res.append((_intra_exact(qh, kh, bh, cmask), dq_i, dk_i, qh * dq_i - kh * dk_i))
                return tuple(res)

            intra = factored() if mild else lax.cond(jnp.min(tot) >= FACTORED_DECAY_LIMIT, factored, exact)
            dq_p, dk_p, db_p, dtot_p = [], [], [], []
            for h in heads:
                a, dq_i, dk_i, db_i = intra[h]
                doh, vh, qh, kh = _head(do, h), _head(v, h), _head(q, h), _head(k, h)
                st0 = s_ref[h, cc]
                dst = dst_ref[h]
                dst_bf = dst.astype(BF16)
                dv = _dot(a.astype(BF16), doh, TN) + _dot(_head(kd, h), dst_bf, NT)
                if ov_ref is not None:
                    dv = dv + ov_ref[sl, h * LANES:(h + 1) * LANES]
                dv_ref[sl, h * LANES:(h + 1) * LANES] = dv.astype(acc_dtype)
                dq_x = _dot(doh, st0, NN) * _head(eb, h)
                dk_x = _dot(vh, dst_bf, NN) * _head(etb, h)
                dtot_p.append(jnp.sum(kh * dk_x, axis=0, keepdims=True)
                              + _head(etot, h) * jnp.sum(dst * st0.astype(F32), axis=0, keepdims=True))
                dst_ref[h] = dst * _head(etot, h) + _dot(doh, _head(qd, h), TN)
                dq_p.append(dq_i + dq_x)
                dk_p.append(dk_i + dk_x)
                db_p.append(db_i + qh * dq_x - kh * dk_x)
            dq, dk, db = (jnp.concatenate(x, axis=1) for x in (dq_p, dk_p, db_p))
            dg = _chunk_cumsum(db, not rev) + jnp.concatenate(dtot_p, axis=1)
            sq = _sigmoid(zq)
            dq_pre = dq * sq * (1.0 + zq * (1.0 - sq))
            if oq_ref is not None:
                dq_pre = dq_pre + oq_ref[sl, :]
            dq_ref[sl, :] = dq_pre.astype(acc_dtype)
            dfk = dg / f - dk
            dzf_ref[sl, :] = (dfk * (1.0 - lbv) * sig * (1.0 - sig)).astype(BF16)
            dlb_ref[...] += jnp.sum(dfk * (1.0 - sig), axis=0, keepdims=True)
            return carry

        lax.cond(_block_is_mild(zf_ref, lbv, nch),
                 lambda: lax.fori_loop(0, nch, functools.partial(chunk, mild=True), 0, unroll=HGRN_UNROLL),
                 lambda: lax.fori_loop(0, nch, functools.partial(chunk, mild=False), 0))

    def col(j):
        return pl.BlockSpec((rb, D_MODEL), lambda n: (blk(n), j))

    return pl.pallas_call(
        body, grid=(nb,),
        in_specs=[col(0), col(2 if rev else 1), col(3), _full((1, D_MODEL)), col(0),
                  pl.BlockSpec((HGRN_HEADS, nch, LANES, LANES), lambda n: (0, blk(n), 0, 0))] + [col(0)] * n_other,
        out_specs=[col(0), col(0), col(0), _full((1, D_MODEL))],
        out_shape=[jax.ShapeDtypeStruct((t, D_MODEL), acc_dtype), jax.ShapeDtypeStruct((t, D_MODEL), BF16),
                   jax.ShapeDtypeStruct((t, D_MODEL), acc_dtype), jax.ShapeDtypeStruct((1, D_MODEL), F32)],
        scratch_shapes=[pltpu.VMEM((HGRN_HEADS, LANES, LANES), F32)],
        compiler_params=_cp("arbitrary"), name=name)(z, z, z, lb, d_o, states, *(other or ()))


def _hgrn_post_fwd(o_f, o_b, z, norm_g, *, name):
    t = o_f.shape[0]
    tm = _rows(t)

    def body(of_ref, ob_ref, gate_ref, ng_ref, y_ref):
        ng = ng_ref[...]
        for h in range(HGRN_HEADS):
            sl = slice(h * LANES, (h + 1) * LANES)
            o = of_ref[:, sl] + ob_ref[:, sl]
            r = lax.rsqrt(jnp.mean(o * o, axis=1, keepdims=True) + LN_EPS)
            gt = gate_ref[:, sl]
            y_ref[:, sl] = (o * r * ng * (gt * _sigmoid(gt))).astype(BF16)

    row = pl.BlockSpec((tm, D_MODEL), lambda i: (i, 0))
    return pl.pallas_call(
        body, grid=(t // tm,),
        in_specs=[row, row, pl.BlockSpec((tm, D_MODEL), lambda i: (i, 4)), _full((1, LANES))],
        out_specs=row, out_shape=jax.ShapeDtypeStruct((t, D_MODEL), BF16),
        compiler_params=_cp("parallel"), name=name)(o_f, o_b, z, norm_g)


def _hgrn_post_bwd(dy, o_f, o_b, z, norm_g, *, name):
    t = o_f.shape[0]
    tm = _rows(t)

    def body(dy_ref, of_ref, ob_ref, gate_ref, ng_ref, do_ref, dgate_ref, dng_ref):
        ng = ng_ref[...]
        dng = jnp.zeros((1, LANES), F32)
        for h in range(HGRN_HEADS):
            sl = slice(h * LANES, (h + 1) * LANES)
            o = of_ref[:, sl] + ob_ref[:, sl]
            r = lax.rsqrt(jnp.mean(o * o, axis=1, keepdims=True) + LN_EPS)
            gt = gate_ref[:, sl]
            sg = _sigmoid(gt)
            dyv = dy_ref[:, sl].astype(F32)
            xn = o * r
            dgate_ref[:, sl] = (dyv * xn * ng * sg * (1.0 + gt * (1.0 - sg))).astype(BF16)
            dn = dyv * gt * sg
            dng = dng + jnp.sum(dn * xn, axis=0, keepdims=True)
            dxn = dn * ng
            do_ref[:, sl] = r * (dxn - xn * jnp.mean(dxn * xn, axis=1, keepdims=True))

        @pl.when(pl.program_id(0) == 0)
        def _():
            dng_ref[...] = jnp.zeros_like(dng_ref)
        dng_ref[...] += dng

    row = pl.BlockSpec((tm, D_MODEL), lambda i: (i, 0))
    return pl.pallas_call(
        body, grid=(t // tm,),
        in_specs=[row, row, row, pl.BlockSpec((tm, D_MODEL), lambda i: (i, 4)), _full((1, LANES))],
        out_specs=[row, row, _full((1, LANES))],
        out_shape=[jax.ShapeDtypeStruct((t, D_MODEL), F32), jax.ShapeDtypeStruct((t, D_MODEL), BF16),
                   jax.ShapeDtypeStruct((1, LANES), F32)],
        compiler_params=_cp("arbitrary"), name=name)(dy, o_f, o_b, z, norm_g)


def _hgrn_combine(dq_f, dq_b, dzf_f, dzf_b, dv_f, dv_b, dgate, *, name):
    t = dq_f.shape[0]
    tm = _rows(t)

    def body(qf, qb, ff, fb, vf, vb, gt, o_ref):
        o_ref[:, 0 * D_MODEL:1 * D_MODEL] = (qf[...] + qb[...]).astype(BF16)
        o_ref[:, 1 * D_MODEL:2 * D_MODEL] = ff[...]
        o_ref[:, 2 * D_MODEL:3 * D_MODEL] = fb[...]
        o_ref[:, 3 * D_MODEL:4 * D_MODEL] = (vf[...] + vb[...]).astype(BF16)
        o_ref[:, 4 * D_MODEL:5 * D_MODEL] = gt[...]

    row = pl.BlockSpec((tm, D_MODEL), lambda i: (i, 0))
    return pl.pallas_call(
        body, grid=(t // tm,), in_specs=[row] * 7,
        out_specs=pl.BlockSpec((tm, 5 * D_MODEL), lambda i: (i, 0)),
        out_shape=jax.ShapeDtypeStruct((t, 5 * D_MODEL), BF16),
        compiler_params=_cp("parallel"), name=name)(dq_f, dq_b, dzf_f, dzf_b, dv_f, dv_b, dgate)


def _lower_bounds(logits2d, *, name):
    def body(l_ref, lb_ref):
        l = l_ref[...]
        e = jnp.exp(l - jnp.max(l, axis=0, keepdims=True))
        sm = e / jnp.sum(e, axis=0, keepdims=True)
        s1, s2, s3 = sm[1:2], sm[2:3], sm[3:4]
        lb_ref[...] = jnp.concatenate([jnp.zeros_like(s1), s1, s1 + s2, s1 + s2 + s3], axis=0)

    return pl.pallas_call(body, out_shape=jax.ShapeDtypeStruct(logits2d.shape, F32), name=name)(logits2d)


def _lower_bounds_bwd(logits2d, dlb, *, name):
    def body(l_ref, d_ref, o_ref):
        l = l_ref[...]
        e = jnp.exp(l - jnp.max(l, axis=0, keepdims=True))
        sm = e / jnp.sum(e, axis=0, keepdims=True)
        d = d_ref[...]
        d1, d2, d3 = d[1:2], d[2:3], d[3:4]
        dsm = jnp.concatenate([jnp.zeros_like(d1), d1 + d2 + d3, d2 + d3, d3], axis=0)
        o_ref[...] = sm * (dsm - jnp.sum(sm * dsm, axis=0, keepdims=True))

    return pl.pallas_call(body, out_shape=jax.ShapeDtypeStruct(logits2d.shape, F32), name=name)(logits2d, dlb)


def _adamw(w, g, m, v, *, name):
    r, c = w.shape
    tr = r if r <= 512 else _pick_rows(r)

    def body(w_ref, g_ref, m_ref, v_ref, d_ref, nm_ref, nv_ref):
        gv = g_ref[...]
        nm = ADAM_B1 * m_ref[...] + (1.0 - ADAM_B1) * gv
        nv = ADAM_B2 * v_ref[...] + (1.0 - ADAM_B2) * (gv * gv)
        m_hat = nm / (1.0 - ADAM_B1 ** ADAM_STEP)
        v_hat = nv / (1.0 - ADAM_B2 ** ADAM_STEP)
        d_ref[...] = -ADAM_LR * (m_hat / (jnp.sqrt(v_hat) + ADAM_EPS) + ADAM_WD * w_ref[...])
        nm_ref[...] = nm
        nv_ref[...] = nv

    blk = pl.BlockSpec((tr, c), lambda i: (i, 0))
    shp = jax.ShapeDtypeStruct((r, c), F32)
    return pl.pallas_call(body, grid=(r // tr,), in_specs=[blk] * 4, out_specs=[blk] * 3, out_shape=[shp] * 3,
                          compiler_params=_cp("parallel"), name=name)(w, g, m, v)


def _pick_rows(r, cap=512):
    best = 8
    for d in range(8, cap + 1, 8):
        if r % d == 0:
            best = d
    assert r % best == 0, r
    return best


def _local_step(x, p, target, w, sp):
    t = x.shape[0]
    tables = _rope_tables(t)
    logits2d = sp["hgrn_lb_logits"].reshape(DEPTH, 2 * D_MODEL)
    lb_all = _lower_bounds(logits2d, name="lb_fwd").reshape(DEPTH, 2, 1, D_MODEL)
    row = lambda a, i: a[i][None]

    saved = []
    xf, xb = x, x.astype(BF16)
    for i in range(DEPTH):
        j = i // 2
        s = dict(xin_bf=xb)
        if i % 2 == 0:
            q, k, v = _qkv_rope(xb, w["att_w_qkv"][j], tables, name=f"qkv_rope_{i}")
            o, lse = _attn_fwd(q, k, v, sp["att_sink"][j], name=f"attn_fwd_{i}")
            s.update(q=q, k=k, v=v, o=o, lse=lse)
            mix_in, w_o = o, w["att_w_o"][j]
        else:
            z = _mm_nn(xb, w["hgrn_w_in"][j], out_dtype=F32, name=f"hgrn_in_{i}")
            o_f, s_f = _hgrn_fwd(z, lb_all[i, 0], False, name=f"hgrn_scan_f_{i}")
            o_b, s_b = _hgrn_fwd(z, lb_all[i, 1], True, name=f"hgrn_scan_b_{i}")
            og = _hgrn_post_fwd(o_f, o_b, z, row(sp["hgrn_norm_g"], j), name=f"hgrn_post_{i}")
            s.update(z=z, o_f=o_f, o_b=o_b, s_f=s_f, s_b=s_b, og=og)
            mix_in, w_o = og, w["hgrn_w_o"][j]
        x1, x1b, xh1, rs1 = _mm_res_ln(mix_in, w_o, xf, row(sp["ln_mix_g"], i), row(sp["ln_mix_b"], i), name=f"mix_out_ln_{i}")
        g, u, h = _ffn_in(x1b, w["ffn_w_in"][i], name=f"ffn_in_{i}")
        x2, x2b, xh2, rs2 = _mm_res_ln(h, w["ffn_w_out"][i], x1, row(sp["ln_ffn_g"], i), row(sp["ln_ffn_b"], i), name=f"ffn_out_ln_{i}")
        xf, xb, gate, pp = _ple_fwd(x2, x2b, p, i, w["ple_w_gate"][i], w["ple_w_proj"][i], name=f"ple_fwd_{i}")
        s.update(x1b=x1b, xh1=xh1, rs1=rs1, g=g, u=u, h=h, x2b=x2b, xh2=xh2, rs2=rs2, gate=gate, pp=pp)
        saved.append(s)

    loss, dx = _loss_head(xf, target, name="loss_head")

    gw = {n: [None] * w[n].shape[0] for n in w}
    gs = {n: [None] * DEPTH for n in ("ln_mix_g", "ln_mix_b", "ln_ffn_g", "ln_ffn_b")}
    gs.update(att_sink=[None] * 2, hgrn_norm_g=[None] * 2)
    dlb = [jnp.zeros((2, D_MODEL), F32)] * DEPTH
    for i in reversed(range(DEPTH)):
        j = i // 2
        s = saved[i]
        du2, du2b, dpre, dpp, gs["ln_ffn_g"][i], gs["ln_ffn_b"][i] = _ple_bwd(
            dx, s["gate"], s["pp"], w["ple_w_gate"][i], s["xh2"], s["rs2"], row(sp["ln_ffn_g"], i), name=f"ple_bwd_{i}")
        gw["ple_w_gate"][i] = _mm_tn(s["x2b"], dpre, name=f"dw_ple_gate_{i}")
        gw["ple_w_proj"][i] = _mm_tn_p(p, i, dpp, name=f"dw_ple_proj_{i}")
        dgg, dgu = _ffn_out_bwd(du2b, w["ffn_w_out"][i], s["g"], s["u"], name=f"ffn_out_bwd_{i}")
        gw["ffn_w_out"][i] = _mm_tn(s["h"], du2b, name=f"dw_ffn_out_{i}")
        du1, du1b, gs["ln_mix_g"][i], gs["ln_mix_b"][i] = _mm_nt(
            [dgg, dgu], w["ffn_w_in"][i], tk=_pick(D_FF, 1408), resid=du2,
            ln=(s["xh1"], s["rs1"], row(sp["ln_mix_g"], i)), name=f"ffn_in_bwd_{i}")
        gw["ffn_w_in"][i] = jnp.concatenate(
            [_mm_tn(s["x1b"], dgg, name=f"dw_ffn_gate_{i}"), _mm_tn(s["x1b"], dgu, name=f"dw_ffn_up_{i}")], axis=1)
        if i % 2 == 0:
            gw["att_w_o"][j] = _mm_tn(s["o"], du1b, name=f"dw_att_o_{i}")
            do = _mm_nt([du1b], w["att_w_o"][j], tk=Q_DIM, out_dtype=BF16, name=f"att_o_bwd_{i}")
            dq, dkw, dvw, gs["att_sink"][j] = _attn_bwd(
                s["q"], s["k"], s["v"], s["o"], do, s["lse"], sp["att_sink"][j], name=f"attn_bwd_{i}")
            dqkv = _attn_post_bwd(dq, dkw, dvw, tables, name=f"attn_post_bwd_{i}")
            gw["att_w_qkv"][j] = _mm_tn(s["xin_bf"], dqkv, name=f"dw_att_qkv_{i}")
            dx = _mm_nt([dqkv], w["att_w_qkv"][j], tk=Q_DIM + 2 * KV_DIM, resid=du1, name=f"qkv_bwd_{i}")
        else:
            gw["hgrn_w_o"][j] = _mm_tn(s["og"], du1b, name=f"dw_hgrn_o_{i}")
            dy = _mm_nt([du1b], w["hgrn_w_o"][j], tk=D_MODEL, out_dtype=BF16, name=f"hgrn_o_bwd_{i}")
            d_o, dgate, gs["hgrn_norm_g"][j] = _hgrn_post_bwd(
                dy, s["o_f"], s["o_b"], s["z"], row(sp["hgrn_norm_g"], j), name=f"hgrn_post_bwd_{i}")
            dq_f, dzf_f, dv_f, dlb_f = _hgrn_bwd(s["z"], lb_all[i, 0], d_o, s["s_f"], False, name=f"hgrn_scan_f_bwd_{i}")
            dq_b, dzf_b, dv_b, dlb_b = _hgrn_bwd(s["z"], lb_all[i, 1], d_o, s["s_b"], True, name=f"hgrn_scan_b_bwd_{i}")
            dlb[i] = jnp.stack([dlb_f.reshape(D_MODEL), dlb_b.reshape(D_MODEL)])
            dz = _hgrn_combine(dq_f, dq_b, dzf_f, dzf_b, dv_f, dv_b, dgate, name=f"hgrn_combine_{i}")
            gw["hgrn_w_in"][j] = _mm_tn(s["xin_bf"], dz, name=f"dw_hgrn_in_{i}")
            dx = _mm_nt([dz], w["hgrn_w_in"][j], tk=_pick(5 * D_MODEL, 1408), resid=du1, name=f"hgrn_in_bwd_{i}")

    gw = {n: jnp.stack(v) for n, v in gw.items()}
    gsmall = {n: jnp.concatenate(v, axis=0) for n, v in gs.items()}
    gsmall["hgrn_lb_logits"] = _lower_bounds_bwd(
        logits2d, jnp.stack(dlb).reshape(DEPTH, 2 * D_MODEL), name="lb_bwd").reshape(DEPTH, 2, D_MODEL)
    return loss, dx, gw, gsmall


ANY = pl.BlockSpec(memory_space=pl.ANY)


def _place():
    x, y, c = lax.axis_index("x"), lax.axis_index("y"), lax.axis_index("c")
    chips = [(1 - x, y), (x, 1 - y), (1 - x, 1 - y)]
    return x, y, c, chips


def _remote(src, dst, send_sem, recv_sem, to):
    return pltpu.make_async_remote_copy(src_ref=src, dst_ref=dst, send_sem=send_sem, recv_sem=recv_sem,
                                        device_id=to, device_id_type=MESH)


def _allgather_weights(packed, *, name):
    r = packed.shape[0]
    hr = r // 2

    def body(src_ref, out_ref, send_sems, recv_sems, local_sem):
        x, y, c, chips = _place()
        sibling = (x, y, 1 - c)

        def half(cx, cy, hc):
            return out_ref.at[2 * cx + cy, pl.ds(hc * hr, hr), :]

        mine = pltpu.make_async_copy(src_ref, out_ref.at[2 * x + y], local_sem)
        mine.start()
        my_half = src_ref.at[pl.ds(c * hr, hr), :]
        first = [_remote(my_half, half(x, y, c), send_sems.at[j], recv_sems.at[j], (*chip, c)) for j, chip in enumerate(chips)]
        for cp in first:
            cp.start()
        passed = [_remote(half(*chip, c), half(*chip, c), send_sems.at[3 + j], recv_sems.at[3 + j], sibling)
                  for j, chip in enumerate(chips)]
        for j, chip in enumerate(chips):
            _remote(my_half, half(*chip, c), send_sems.at[j], recv_sems.at[j], (*chip, c)).wait_recv()
            passed[j].start()
        for j, chip in enumerate(chips):
            _remote(my_half, half(*chip, 1 - c), send_sems.at[3 + j], recv_sems.at[3 + j], sibling).wait_recv()
        for cp in first + passed:
            cp.wait_send()
        mine.wait()

    return pl.pallas_call(
        body, in_specs=[ANY], out_specs=ANY, out_shape=jax.ShapeDtypeStruct((N_CHIPS, r, packed.shape[1]), packed.dtype),
        scratch_shapes=[pltpu.SemaphoreType.DMA((6,)), pltpu.SemaphoreType.DMA((6,)), pltpu.SemaphoreType.DMA],
        name=name)(packed)


def _allreduce_small(block, *, name):
    m, n = block.shape

    def body(x_ref, sum_ref, all_ref, send_sems, recv_sems):
        x, y, c, chips = _place()
        me, sibling = (x, y, c), (x, y, 1 - c)

        def rows(px, py, pc):
            return all_ref.at[pl.ds((4 * px + 2 * py + pc) * m, m), :]

        all_ref[pl.ds((4 * x + 2 * y + c) * m, m), :] = x_ref[...]
        first = [_remote(x_ref, rows(*me), send_sems.at[0], recv_sems.at[0], sibling)]
        first += [_remote(x_ref, rows(*me), send_sems.at[1 + j], recv_sems.at[1 + j], (*chip, c)) for j, chip in enumerate(chips)]
        for cp in first:
            cp.start()
        passed = [_remote(rows(*chip, c), rows(*chip, c), send_sems.at[4 + j], recv_sems.at[4 + j], sibling)
                  for j, chip in enumerate(chips)]
        for j, chip in enumerate(chips):
            _remote(x_ref, rows(*chip, c), send_sems.at[1 + j], recv_sems.at[1 + j], me).wait_recv()
            passed[j].start()
        _remote(x_ref, rows(*sibling), send_sems.at[0], recv_sems.at[0], me).wait_recv()
        for j, chip in enumerate(chips):
            _remote(x_ref, rows(*chip, 1 - c), send_sems.at[4 + j], recv_sems.at[4 + j], me).wait_recv()
        for cp in first + passed:
            cp.wait_send()
        acc = all_ref[pl.ds(0, m), :]
        for d in range(1, 8):
            acc = acc + all_ref[pl.ds(d * m, m), :]
        sum_ref[...] = acc

    vmem = pl.BlockSpec(memory_space=pltpu.VMEM)
    return pl.pallas_call(
        body, in_specs=[vmem], out_specs=vmem, out_shape=jax.ShapeDtypeStruct((m, n), F32),
        scratch_shapes=[pltpu.VMEM((8 * m, n), F32), pltpu.SemaphoreType.DMA((7,)), pltpu.SemaphoreType.DMA((7,))],
        name=name)(block)


def _pair_exchange(g, *, name):
    n, r, w = g.shape
    hr = r // 2

    def body(g_ref, out_ref, send_sem, recv_sem):
        x, y, c, _ = _place()
        cp = _remote(g_ref.at[:, pl.ds((1 - c) * hr, hr), :], out_ref, send_sem, recv_sem, (x, y, 1 - c))
        cp.start()
        cp.wait()

    return pl.pallas_call(
        body, in_specs=[ANY], out_specs=ANY, out_shape=jax.ShapeDtypeStruct((n, hr, w), g.dtype),
        scratch_shapes=[pltpu.SemaphoreType.DMA, pltpu.SemaphoreType.DMA], name=name)(g)


def _chip_scatter(part, *, name):
    n, h, w = part.shape

    def body(p_ref, out_ref, send_sems, recv_sems, local_sem):
        x, y, c, chips = _place()
        me = 2 * x + y
        mine = pltpu.make_async_copy(p_ref.at[me], out_ref.at[me], local_sem)
        mine.start()
        sends = [_remote(p_ref.at[2 * cx + cy], out_ref.at[me], send_sems.at[j], recv_sems.at[j], (cx, cy, c))
                 for j, (cx, cy) in enumerate(chips)]
        for cp in sends:
            cp.start()
        for j, (cx, cy) in enumerate(chips):
            _remote(p_ref.at[me], out_ref.at[2 * cx + cy], send_sems.at[j], recv_sems.at[j], (cx, cy, c)).wait_recv()
        for cp in sends:
            cp.wait_send()
        mine.wait()

    return pl.pallas_call(
        body, in_specs=[ANY], out_specs=ANY, out_shape=jax.ShapeDtypeStruct((n, h, w), part.dtype),
        scratch_shapes=[pltpu.SemaphoreType.DMA((3,)), pltpu.SemaphoreType.DMA((3,)), pltpu.SemaphoreType.DMA],
        name=name)(part)


def _pair_share(half, *, name):
    h, w = half.shape

    def body(h_ref, out_ref, send_sem, recv_sem, local_sem):
        x, y, c, _ = _place()
        dst = out_ref.at[pl.ds(c * h, h), :]
        mine = pltpu.make_async_copy(h_ref, dst, local_sem)
        mine.start()
        cp = _remote(h_ref, dst, send_sem, recv_sem, (x, y, 1 - c))
        cp.start()
        cp.wait_send()
        _remote(h_ref, out_ref.at[pl.ds((1 - c) * h, h), :], send_sem, recv_sem, (x, y, 1 - c)).wait_recv()
        mine.wait()

    return pl.pallas_call(
        body, in_specs=[ANY], out_specs=ANY, out_shape=jax.ShapeDtypeStruct((2 * h, w), half.dtype),
        scratch_shapes=[pltpu.SemaphoreType.DMA, pltpu.SemaphoreType.DMA, pltpu.SemaphoreType.DMA], name=name)(half)


def _add_own_half(g, recv, c, *, name):
    n, r, w = g.shape
    hr = r // 2
    tr = _pick_rows(hr, 1024)
    nr = hr // tr

    def body(c_ref, g_ref, r_ref, o_ref):
        o_ref[...] = (g_ref[...] + r_ref[...]).astype(BF16)

    return pl.pallas_call(
        body,
        grid_spec=pltpu.PrefetchScalarGridSpec(
            num_scalar_prefetch=1, grid=(n, nr),
            in_specs=[pl.BlockSpec((None, tr, w), lambda s, i, c_ref: (s, c_ref[0] * nr + i, 0)),
                      pl.BlockSpec((None, tr, w), lambda s, i, c_ref: (s, i, 0))],
            out_specs=pl.BlockSpec((None, tr, w), lambda s, i, c_ref: (s, i, 0))),
        out_shape=jax.ShapeDtypeStruct((n, hr, w), BF16),
        compiler_params=_cp("parallel", "parallel"), name=name)(c, g, recv)


def _sum_chips(q, *, name):
    n, h, w = q.shape
    tr = _pick_rows(h, 1024)

    def body(a, b, c, d, o_ref):
        o_ref[...] = ((a[...].astype(F32) + b[...].astype(F32)) + c[...].astype(F32)) + d[...].astype(F32)

    specs = [pl.BlockSpec((None, tr, w), functools.partial(lambda i, s: (s, i, 0), s=s)) for s in range(n)]
    return pl.pallas_call(
        body, grid=(h // tr,), in_specs=specs, out_specs=pl.BlockSpec((tr, w), lambda i: (i, 0)),
        out_shape=jax.ShapeDtypeStruct((h, w), F32), compiler_params=_cp("parallel"), name=name)(q, q, q, q)


PACK_W = 1024
BIG = (("att_w_qkv", 2), ("att_w_o", 1), ("hgrn_w_in", 2), ("hgrn_w_o", 1),
       ("ffn_w_in", 2), ("ffn_w_out", 1), ("ple_w_gate", 1), ("ple_w_proj", 2))
LB_ROWS = 32


def _pack_shards(shards):
    return jnp.concatenate([shards[n].reshape(-1, PACK_W) for n, _ in BIG], axis=0)


def _unpack_shards(buf, shapes):
    out, r0 = {}, 0
    for n, _ in BIG:
        nr = shapes[n][0] * shapes[n][1] * shapes[n][2] // PACK_W
        out[n] = buf[r0:r0 + nr].reshape(shapes[n])
        r0 += nr
    return out


def _gathered_to_full(buf, shapes):
    out, r0 = {}, 0
    for n, axis in BIG:
        l, r, c = shapes[n]
        nr = l * r * c // PACK_W
        sh = buf[:, r0:r0 + nr].reshape(N_CHIPS, l, r, c)
        out[n] = (sh.transpose(1, 0, 2, 3).reshape(l, N_CHIPS * r, c) if axis == 1
                  else sh.transpose(1, 2, 0, 3).reshape(l, r, N_CHIPS * c))
        r0 += nr
    return out, r0


def _full_to_slabs(full, shapes):
    parts = []
    for n, axis in BIG:
        l, r, c = shapes[n]
        g = full[n]
        g = (g.reshape(l, N_CHIPS, r, c).transpose(1, 0, 2, 3) if axis == 1
             else g.reshape(l, r, N_CHIPS, c).transpose(2, 0, 1, 3))
        parts.append(g.reshape(N_CHIPS, -1, PACK_W))
    return jnp.concatenate(parts, axis=1)


SMALL = ("ln_mix_g", "ln_mix_b", "ln_ffn_g", "ln_ffn_b")
SMALL_ROWS = 32


def _pack_small(vals, lb):
    rows = [vals[n] for n in SMALL] + [lb.reshape(-1, PACK_W)]
    for n in ("att_sink", "hgrn_norm_g"):
        flat = vals[n].reshape(1, -1)
        rows.append(jnp.pad(flat, ((0, 0), (0, PACK_W - flat.shape[1]))))
    buf = jnp.concatenate(rows, axis=0)
    return jnp.pad(buf, ((0, SMALL_ROWS - buf.shape[0]), (0, 0)))


def _unpack_small(buf, lb_shape):
    out, r0 = {}, 0
    for n in SMALL:
        out[n] = buf[r0:r0 + DEPTH]
        r0 += DEPTH
    nlb = lb_shape[0] * lb_shape[1] * lb_shape[2] // PACK_W
    out["hgrn_lb_logits"] = buf[r0:r0 + nlb].reshape(lb_shape)
    r0 += nlb
    out["att_sink"] = buf[r0, :2 * N_Q_HEADS].reshape(2, N_Q_HEADS)
    out["hgrn_norm_g"] = buf[r0 + 1, :2 * LANES].reshape(2, LANES)
    return out


WEIGHTS = ("att_w_qkv", "att_sink", "att_w_o", "hgrn_w_in", "hgrn_lb_logits", "hgrn_norm_g", "hgrn_w_o", "ln_mix_g",
           "ln_mix_b", "ffn_w_in", "ffn_w_out", "ln_ffn_g", "ln_ffn_b", "ple_w_gate", "ple_w_proj")


def kernel(x, p, att_w_qkv, att_sink, att_w_o, hgrn_w_in, hgrn_lb_logits, hgrn_norm_g, hgrn_w_o, ln_mix_g, ln_mix_b, ffn_w_in, ffn_w_out, ln_ffn_g, ln_ffn_b, ple_w_gate, ple_w_proj, loss_target, m_att_w_qkv, m_att_sink, m_att_w_o, m_hgrn_w_in, m_hgrn_lb_logits, m_hgrn_norm_g, m_hgrn_w_o, m_ln_mix_g, m_ln_mix_b, m_ffn_w_in, m_ffn_w_out, m_ln_ffn_g, m_ln_ffn_b, m_ple_w_gate, m_ple_w_proj, v_att_w_qkv, v_att_sink, v_att_w_o, v_hgrn_w_in, v_hgrn_lb_logits, v_hgrn_norm_g, v_hgrn_w_o, v_ln_mix_g, v_ln_mix_b, v_ffn_w_in, v_ffn_w_out, v_ln_ffn_g, v_ln_ffn_b, v_ple_w_gate, v_ple_w_proj):
    wts = dict(att_w_qkv=att_w_qkv, att_sink=att_sink, att_w_o=att_w_o, hgrn_w_in=hgrn_w_in, hgrn_lb_logits=hgrn_lb_logits,
               hgrn_norm_g=hgrn_norm_g, hgrn_w_o=hgrn_w_o, ln_mix_g=ln_mix_g, ln_mix_b=ln_mix_b, ffn_w_in=ffn_w_in,
               ffn_w_out=ffn_w_out, ln_ffn_g=ln_ffn_g, ln_ffn_b=ln_ffn_b, ple_w_gate=ple_w_gate, ple_w_proj=ple_w_proj)
    mom = dict(att_w_qkv=m_att_w_qkv, att_sink=m_att_sink, att_w_o=m_att_w_o, hgrn_w_in=m_hgrn_w_in, hgrn_lb_logits=m_hgrn_lb_logits,
               hgrn_norm_g=m_hgrn_norm_g, hgrn_w_o=m_hgrn_w_o, ln_mix_g=m_ln_mix_g, ln_mix_b=m_ln_mix_b, ffn_w_in=m_ffn_w_in,
               ffn_w_out=m_ffn_w_out, ln_ffn_g=m_ln_ffn_g, ln_ffn_b=m_ln_ffn_b, ple_w_gate=m_ple_w_gate, ple_w_proj=m_ple_w_proj)
    var = dict(att_w_qkv=v_att_w_qkv, att_sink=v_att_sink, att_w_o=v_att_w_o, hgrn_w_in=v_hgrn_w_in, hgrn_lb_logits=v_hgrn_lb_logits,
               hgrn_norm_g=v_hgrn_norm_g, hgrn_w_o=v_hgrn_w_o, ln_mix_g=v_ln_mix_g, ln_mix_b=v_ln_mix_b, ffn_w_in=v_ffn_w_in,
               ffn_w_out=v_ffn_w_out, ln_ffn_g=v_ln_ffn_g, ln_ffn_b=v_ln_ffn_b, ple_w_gate=v_ple_w_gate, ple_w_proj=v_ple_w_proj)
    shapes = {n: wts[n].shape for n, _ in BIG}
    chip = 2 * lax.axis_index("x") + lax.axis_index("y")
    core = lax.axis_index("c")

    lb_bits = lax.bitcast_convert_type(hgrn_lb_logits.reshape(-1), BF16).reshape(-1, PACK_W)
    packed = jnp.concatenate([_pack_shards({n: wts[n].astype(BF16) for n, _ in BIG}), lb_bits,
                              jnp.zeros((LB_ROWS - lb_bits.shape[0], PACK_W), BF16)], axis=0)
    gathered = _allgather_weights(packed, name="allgather_weights")
    w_full, r_big = _gathered_to_full(gathered, shapes)
    lb_sh = hgrn_lb_logits.shape
    lb_rows = gathered[:, r_big:r_big + lb_bits.shape[0]].reshape(N_CHIPS, -1, 2)
    lb_full = lax.bitcast_convert_type(lb_rows, F32).reshape(N_CHIPS, lb_sh[0], lb_sh[1], lb_sh[2])
    lb_full = lb_full.transpose(1, 2, 0, 3).reshape(lb_sh[0], lb_sh[1], N_CHIPS * lb_sh[2])
    sp = dict(att_sink=att_sink, hgrn_lb_logits=lb_full, hgrn_norm_g=hgrn_norm_g, ln_mix_g=ln_mix_g, ln_mix_b=ln_mix_b,
              ln_ffn_g=ln_ffn_g, ln_ffn_b=ln_ffn_b)

    loss, grad_x, gw, gs = _local_step(x[0], p, loss_target[0], w_full, sp)
    loss = lax.psum(loss[0, 0], ("x", "y", "c"))

    slabs = _full_to_slabs(gw, shapes)
    from_sibling = _pair_exchange(slabs, name="grad_pair_exchange")
    chip_part = _add_own_half(slabs, from_sibling, core.reshape(1).astype(jnp.int32), name="grad_pair_add")
    from_chips = _chip_scatter(chip_part, name="grad_chip_scatter")
    half = _sum_chips(from_chips, name="grad_chip_sum")
    g_big = _unpack_shards(_pair_share(half, name="grad_pair_share"), shapes)

    g_small_full = _unpack_small(_allreduce_small(_pack_small(gs, gs["hgrn_lb_logits"]), name="allreduce_small"),
                                 (lb_sh[0], lb_sh[1], N_CHIPS * lb_sh[2]))
    g_lb = lax.dynamic_slice_in_dim(g_small_full["hgrn_lb_logits"], chip * lb_sh[2], lb_sh[2], axis=2)
    grads = dict(g_big)
    grads.update({n: g_small_full[n] for n in SMALL + ("att_sink", "hgrn_norm_g")})
    grads["hgrn_lb_logits"] = g_lb

    delta, new_m, new_v = {}, {}, {}
    for n, _ in BIG:
        two_d = lambda a: a.reshape(-1, a.shape[-1])
        d, nm, nv = _adamw(two_d(wts[n]), two_d(grads[n]), two_d(mom[n]), two_d(var[n]), name=f"adamw_{n}")
        delta[n], new_m[n], new_v[n] = d.reshape(shapes[n]), nm.reshape(shapes[n]), nv.reshape(shapes[n])
    packs = [_pack_small(src, src["hgrn_lb_logits"]) for src in (wts, grads, mom, var)]
    outs = _adamw(*packs, name="adamw_small")
    for dst, buf in zip((delta, new_m, new_v), outs):
        dst.update(_unpack_small(buf, lb_sh))

    return (loss, grad_x[None], *[grads[n] for n in WEIGHTS], *[delta[n] for n in WEIGHTS],
            *[new_m[n] for n in WEIGHTS], *[new_v[n] for n in WEIGHTS])


def _gather_guest(packed):
    r, w = packed.shape
    hr = r // 2

    def copies(src_ref, out_ref, send_sems, recv_sems, local_sem):
        x, y, c, chips = _place()
        sibling = (x, y, 1 - c)

        def half(cx, cy, hc):
            return out_ref.at[2 * cx + cy, pl.ds(hc * hr, hr), :]

        my_half = src_ref.at[pl.ds(c * hr, hr), :]
        mine = pltpu.make_async_copy(src_ref, out_ref.at[2 * x + y], local_sem)
        first = [_remote(my_half, half(x, y, c), send_sems.at[j], recv_sems.at[j], (*chip, c)) for j, chip in enumerate(chips)]
        passed = [_remote(half(*chip, c), half(*chip, c), send_sems.at[3 + j], recv_sems.at[3 + j], sibling)
                  for j, chip in enumerate(chips)]
        from_chips = [_remote(my_half, half(*chip, c), send_sems.at[j], recv_sems.at[j], (*chip, c)) for j, chip in enumerate(chips)]
        from_sibling = [_remote(my_half, half(*chip, 1 - c), send_sems.at[3 + j], recv_sems.at[3 + j], sibling)
                        for j, chip in enumerate(chips)]
        return mine, first, passed, from_chips, from_sibling

    def start(gi, go, gs):
        mine, first, _, _, _ = copies(gi[0], go[0], *gs)
        mine.start()
        for cp in first:
            cp.start()

    def finish(gi, go, gs):
        mine, first, passed, from_chips, from_sibling = copies(gi[0], go[0], *gs)
        for arrival, onward in zip(from_chips, passed):
            arrival.wait_recv()
            onward.start()
        for arrival in from_sibling:
            arrival.wait_recv()
        for cp in first + passed:
            cp.wait_send()
        mine.wait()

    return _Guest((packed,), (jax.ShapeDtypeStruct((N_CHIPS, r, w), packed.dtype),),
                  (pltpu.SemaphoreType.DMA((6,)), pltpu.SemaphoreType.DMA((6,)), pltpu.SemaphoreType.DMA), start, finish)


def _pair_exchange_guest(g):
    n, r, w = g.shape
    hr = r // 2

    def copy(g_ref, out_ref, send_sem, recv_sem):
        x, y, c, _ = _place()
        return _remote(g_ref.at[:, pl.ds((1 - c) * hr, hr), :], out_ref, send_sem, recv_sem, (x, y, 1 - c))

    return _Guest((g,), (jax.ShapeDtypeStruct((n, hr, w), g.dtype),), (pltpu.SemaphoreType.DMA, pltpu.SemaphoreType.DMA),
                  lambda gi, go, gs: copy(gi[0], go[0], *gs).start(),
                  lambda gi, go, gs: copy(gi[0], go[0], *gs).wait())


def _chip_scatter_guest(part):
    n, h, w = part.shape

    def copies(p_ref, out_ref, send_sems, recv_sems, local_sem):
        x, y, c, chips = _place()
        me = 2 * x + y
        mine = pltpu.make_async_copy(p_ref.at[me], out_ref.at[me], local_sem)
        sends = [_remote(p_ref.at[2 * cx + cy], out_ref.at[me], send_sems.at[j], recv_sems.at[j], (cx, cy, c))
                 for j, (cx, cy) in enumerate(chips)]
        arrivals = [_remote(p_ref.at[me], out_ref.at[2 * cx + cy], send_sems.at[j], recv_sems.at[j], (cx, cy, c))
                    for j, (cx, cy) in enumerate(chips)]
        return mine, sends, arrivals

    def start(gi, go, gs):
        mine, sends, _ = copies(gi[0], go[0], *gs)
        mine.start()
        for cp in sends:
            cp.start()

    def finish(gi, go, gs):
        mine, sends, arrivals = copies(gi[0], go[0], *gs)
        for cp in arrivals:
            cp.wait_recv()
        for cp in sends:
            cp.wait_send()
        mine.wait()

    return _Guest((part,), (jax.ShapeDtypeStruct((n, h, w), part.dtype),),
                  (pltpu.SemaphoreType.DMA((3,)), pltpu.SemaphoreType.DMA((3,)), pltpu.SemaphoreType.DMA), start, finish)


def _pair_share_guest(half):
    h, w = half.shape

    def copies(h_ref, out_ref, send_sem, recv_sem, local_sem):
        x, y, c, _ = _place()
        dst = out_ref.at[pl.ds(c * h, h), :]
        mine = pltpu.make_async_copy(h_ref, dst, local_sem)
        send = _remote(h_ref, dst, send_sem, recv_sem, (x, y, 1 - c))
        arrival = _remote(h_ref, out_ref.at[pl.ds((1 - c) * h, h), :], send_sem, recv_sem, (x, y, 1 - c))
        return mine, send, arrival

    def start(gi, go, gs):
        mine, send, _ = copies(gi[0], go[0], *gs)
        mine.start()
        send.start()

    def finish(gi, go, gs):
        mine, send, arrival = copies(gi[0], go[0], *gs)
        send.wait_send()
        arrival.wait_recv()
        mine.wait()

    return _Guest((half,), (jax.ShapeDtypeStruct((2 * h, w), half.dtype),),
                  (pltpu.SemaphoreType.DMA, pltpu.SemaphoreType.DMA, pltpu.SemaphoreType.DMA), start, finish)


AXIS = dict(BIG)


def _layer_parts(i):
    j = i // 2
    mixer = (("att_w_qkv", j), ("att_w_o", j)) if i % 2 == 0 else (("hgrn_w_in", j), ("hgrn_w_o", j))
    return mixer + (("ffn_w_in", i), ("ffn_w_out", i), ("ple_w_gate", i), ("ple_w_proj", i))


def _pack_layer(shards, i):
    return jnp.concatenate([shards[n][l].reshape(-1, PACK_W) for n, l in _layer_parts(i)], axis=0)


def _unpack_layer(buf, i, shapes):
    out, r0 = {}, 0
    for n, _ in _layer_parts(i):
        r, c = shapes[n][1:]
        nr = r * c // PACK_W
        out[n] = buf[r0:r0 + nr].reshape(r, c)
        r0 += nr
    return out


def _gathered_layer(buf, i, shapes):
    out, r0 = {}, 0
    for n, _ in _layer_parts(i):
        r, c = shapes[n][1:]
        nr = r * c // PACK_W
        sh = buf[:, r0:r0 + nr].reshape(N_CHIPS, r, c)
        out[n] = sh.reshape(N_CHIPS * r, c) if AXIS[n] == 1 else sh.transpose(1, 0, 2).reshape(r, N_CHIPS * c)
        r0 += nr
    return out, r0


def _layer_slabs(full, i, shapes):
    parts = []
    for n, _ in _layer_parts(i):
        r, c = shapes[n][1:]
        g = full[n]
        g = g.reshape(N_CHIPS, -1, PACK_W) if AXIS[n] == 1 else g.reshape(r, N_CHIPS, c).transpose(1, 0, 2).reshape(N_CHIPS, -1, PACK_W)
        parts.append(g)
    return jnp.concatenate(parts, axis=1)


def kernel(x, p, att_w_qkv, att_sink, att_w_o, hgrn_w_in, hgrn_lb_logits, hgrn_norm_g, hgrn_w_o, ln_mix_g, ln_mix_b, ffn_w_in, ffn_w_out, ln_ffn_g, ln_ffn_b, ple_w_gate, ple_w_proj, loss_target, m_att_w_qkv, m_att_sink, m_att_w_o, m_hgrn_w_in, m_hgrn_lb_logits, m_hgrn_norm_g, m_hgrn_w_o, m_ln_mix_g, m_ln_mix_b, m_ffn_w_in, m_ffn_w_out, m_ln_ffn_g, m_ln_ffn_b, m_ple_w_gate, m_ple_w_proj, v_att_w_qkv, v_att_sink, v_att_w_o, v_hgrn_w_in, v_hgrn_lb_logits, v_hgrn_norm_g, v_hgrn_w_o, v_ln_mix_g, v_ln_mix_b, v_ffn_w_in, v_ffn_w_out, v_ln_ffn_g, v_ln_ffn_b, v_ple_w_gate, v_ple_w_proj):
    wts = dict(att_w_qkv=att_w_qkv, att_sink=att_sink, att_w_o=att_w_o, hgrn_w_in=hgrn_w_in, hgrn_lb_logits=hgrn_lb_logits,
               hgrn_norm_g=hgrn_norm_g, hgrn_w_o=hgrn_w_o, ln_mix_g=ln_mix_g, ln_mix_b=ln_mix_b, ffn_w_in=ffn_w_in,
               ffn_w_out=ffn_w_out, ln_ffn_g=ln_ffn_g, ln_ffn_b=ln_ffn_b, ple_w_gate=ple_w_gate, ple_w_proj=ple_w_proj)
    mom = dict(att_w_qkv=m_att_w_qkv, att_sink=m_att_sink, att_w_o=m_att_w_o, hgrn_w_in=m_hgrn_w_in, hgrn_lb_logits=m_hgrn_lb_logits,
               hgrn_norm_g=m_hgrn_norm_g, hgrn_w_o=m_hgrn_w_o, ln_mix_g=m_ln_mix_g, ln_mix_b=m_ln_mix_b, ffn_w_in=m_ffn_w_in,
               ffn_w_out=m_ffn_w_out, ln_ffn_g=m_ln_ffn_g, ln_ffn_b=m_ln_ffn_b, ple_w_gate=m_ple_w_gate, ple_w_proj=m_ple_w_proj)
    var = dict(att_w_qkv=v_att_w_qkv, att_sink=v_att_sink, att_w_o=v_att_w_o, hgrn_w_in=v_hgrn_w_in, hgrn_lb_logits=v_hgrn_lb_logits,
               hgrn_norm_g=v_hgrn_norm_g, hgrn_w_o=v_hgrn_w_o, ln_mix_g=v_ln_mix_g, ln_mix_b=v_ln_mix_b, ffn_w_in=v_ffn_w_in,
               ffn_w_out=v_ffn_w_out, ln_ffn_g=v_ln_ffn_g, ln_ffn_b=v_ln_ffn_b, ple_w_gate=v_ple_w_gate, ple_w_proj=v_ple_w_proj)
    shapes = {n: wts[n].shape for n, _ in BIG}
    chip = 2 * lax.axis_index("x") + lax.axis_index("y")
    core = lax.axis_index("c").reshape(1).astype(jnp.int32)
    xin, target = x[0], loss_target[0]
    t = xin.shape[0]
    row = lambda a, i: a[i][None]

    bf_shards = {n: wts[n].astype(BF16) for n, _ in BIG}
    lb_sh = hgrn_lb_logits.shape
    lb_bits = lax.bitcast_convert_type(hgrn_lb_logits.reshape(-1), BF16).reshape(-1, PACK_W)
    packed = [_pack_layer(bf_shards, i) for i in range(DEPTH)]
    packed[0] = jnp.concatenate([packed[0], lb_bits, jnp.zeros((LB_ROWS - lb_bits.shape[0], PACK_W), BF16)], axis=0)

    gathered = _run_guest(_gather_guest(packed[0]), name="gather_layer_0")[0]
    w, r_big = _gathered_layer(gathered, 0, shapes)
    lb_rows = gathered[:, r_big:r_big + lb_bits.shape[0]].reshape(N_CHIPS, -1, 2)
    lb_full = lax.bitcast_convert_type(lb_rows, F32).reshape(N_CHIPS, lb_sh[0], lb_sh[1], lb_sh[2])
    lb_full = lb_full.transpose(1, 2, 0, 3).reshape(lb_sh[0], lb_sh[1], N_CHIPS * lb_sh[2])
    logits2d = lb_full.reshape(DEPTH, 2 * D_MODEL)
    lb_all = _lower_bounds(logits2d, name="lb_fwd").reshape(DEPTH, 2, 1, D_MODEL)
    tables = _rope_tables(t)

    saved, weights = [], []
    xf, xb = xin, xin.astype(BF16)
    for i in range(DEPTH):
        j = i // 2
        nxt = _gather_guest(packed[i + 1]) if i + 1 < DEPTH else None
        s = dict(xin_bf=xb)
        if i % 2 == 0:
            q, k, v = _qkv_rope(xb, w["att_w_qkv"], tables, name=f"qkv_rope_{i}")
            (o, lse), got = _attn_fwd(q, k, v, att_sink[j], name=f"attn_fwd_{i}", guest=nxt)
            s.update(q=q, k=k, v=v, o=o, lse=lse)
            mix_in, w_o = o, w["att_w_o"]
        else:
            z = _mm_nn(xb, w["hgrn_w_in"], out_dtype=F32, name=f"hgrn_in_{i}")
            (o_f, s_f), got = _hgrn_fwd(z, lb_all[i, 0], False, name=f"hgrn_scan_f_{i}", guest=nxt)
            (o_b, s_b), _ = _hgrn_fwd(z, lb_all[i, 1], True, name=f"hgrn_scan_b_{i}")
            og = _hgrn_post_fwd(o_f, o_b, z, row(hgrn_norm_g, j), name=f"hgrn_post_{i}")
            s.update(z=z, o_f=o_f, o_b=o_b, s_f=s_f, s_b=s_b, og=og)
            mix_in, w_o = og, w["hgrn_w_o"]
        x1, x1b, xh1, rs1 = _mm_res_ln(mix_in, w_o, xf, row(ln_mix_g, i), row(ln_mix_b, i), name=f"mix_out_ln_{i}")
        g, u, h = _ffn_in(x1b, w["ffn_w_in"], name=f"ffn_in_{i}")
        x2, x2b, xh2, rs2 = _mm_res_ln(h, w["ffn_w_out"], x1, row(ln_ffn_g, i), row(ln_ffn_b, i), name=f"ffn_out_ln_{i}")
        xf, xb, gate, pp = _ple_fwd(x2, x2b, p, i, w["ple_w_gate"], w["ple_w_proj"], name=f"ple_fwd_{i}")
        s.update(x1b=x1b, xh1=xh1, rs1=rs1, g=g, u=u, h=h, x2b=x2b, xh2=xh2, rs2=rs2, gate=gate, pp=pp)
        saved.append(s)
        weights.append(w)
        if nxt is not None:
            w, _ = _gathered_layer(got[0], i + 1, shapes)

    loss, dx = _loss_head(xf, target, name="loss_head")
    loss = lax.psum(loss[0, 0], ("x", "y", "c"))

    gs = {n: [None] * DEPTH for n in SMALL}
    gs.update(att_sink=[None] * 2, hgrn_norm_g=[None] * 2)
    dlb = [jnp.zeros((2, D_MODEL), F32)] * DEPTH
    reduced = [None] * DEPTH
    slabs = None
    for i in reversed(range(DEPTH)):
        j = i // 2
        s, w = saved[i], weights[i]
        gw = {}
        res, got = _ple_bwd(dx, s["gate"], s["pp"], w["ple_w_gate"], s["xh2"], s["rs2"], row(ln_ffn_g, i), name=f"ple_bwd_{i}",
                            guest=None if slabs is None else _pair_exchange_guest(slabs))
        du2, du2b, dpre, dpp, gs["ln_ffn_g"][i], gs["ln_ffn_b"][i] = res
        part = None if slabs is None else _add_own_half(slabs, got[0], core, name=f"grad_pair_add_{i + 1}")
        gw["ple_w_gate"] = _mm_tn(s["x2b"], dpre, name=f"dw_ple_gate_{i}")
        gw["ple_w_proj"] = _mm_tn_p(p, i, dpp, name=f"dw_ple_proj_{i}")
        dgg, dgu = _ffn_out_bwd(du2b, w["ffn_w_out"], s["g"], s["u"], name=f"ffn_out_bwd_{i}")
        gw["ffn_w_out"] = _mm_tn(s["h"], du2b, name=f"dw_ffn_out_{i}")
        res = _mm_nt([dgg, dgu], w["ffn_w_in"], resid=du2, ln=(s["xh1"], s["rs1"], row(ln_mix_g, i)),
                     name=f"ffn_in_bwd_{i}", guest=None if part is None else _chip_scatter_guest(part))
        (du1, du1b, gs["ln_mix_g"][i], gs["ln_mix_b"][i]), got = res if part is not None else (res, None)
        half = None if part is None else _sum_chips(got[0], name=f"grad_chip_sum_{i + 1}")
        res = _mm_tn(s["x1b"], dgg, name=f"dw_ffn_gate_{i}", guest=None if half is None else _pair_share_guest(half))
        dw_gate, got = res if half is not None else (res, None)
        if half is not None:
            reduced[i + 1] = got[0]
        gw["ffn_w_in"] = jnp.concatenate([dw_gate, _mm_tn(s["x1b"], dgu, name=f"dw_ffn_up_{i}")], axis=1)
        if i % 2 == 0:
            gw["att_w_o"] = _mm_tn(s["o"], du1b, name=f"dw_att_o_{i}")
            do = _mm_nt([du1b], w["att_w_o"], out_dtype=BF16, name=f"att_o_bwd_{i}")
            dq, dkw, dvw, gs["att_sink"][j] = _attn_bwd(s["q"], s["k"], s["v"], s["o"], do, s["lse"], att_sink[j], name=f"attn_bwd_{i}")
            dqkv = _attn_post_bwd(dq, dkw, dvw, tables, name=f"attn_post_bwd_{i}")
            gw["att_w_qkv"] = _mm_tn(s["xin_bf"], dqkv, name=f"dw_att_qkv_{i}")
            dx = _mm_nt([dqkv], w["att_w_qkv"], resid=du1, name=f"qkv_bwd_{i}")
        else:
            gw["hgrn_w_o"] = _mm_tn(s["og"], du1b, name=f"dw_hgrn_o_{i}")
            dy = _mm_nt([du1b], w["hgrn_w_o"], out_dtype=BF16, name=f"hgrn_o_bwd_{i}")
            d_o, dgate, gs["hgrn_norm_g"][j] = _hgrn_post_bwd(dy, s["o_f"], s["o_b"], s["z"], row(hgrn_norm_g, j), name=f"hgrn_post_bwd_{i}")
            dq_f, dzf_f, dv_f, dlb_f = _hgrn_bwd(s["z"], lb_all[i, 0], d_o, s["s_f"], False, name=f"hgrn_scan_f_bwd_{i}")
            dq, dzf_b, dv, dlb_b = _hgrn_bwd(s["z"], lb_all[i, 1], d_o, s["s_b"], True, name=f"hgrn_scan_b_bwd_{i}",
                                             other=(dq_f, dv_f))
            dlb[i] = jnp.stack([dlb_f.reshape(D_MODEL), dlb_b.reshape(D_MODEL)])
            dz = [dq, dzf_f, dzf_b, dv, dgate]
            gw["hgrn_w_in"] = jnp.concatenate(
                [_mm_tn(s["xin_bf"], part, name=f"dw_hgrn_in_{i}_{n}") for n, part in enumerate(dz)], axis=1)
            dx = _mm_nt(dz, w["hgrn_w_in"], resid=du1, name=f"hgrn_in_bwd_{i}")
        slabs = _layer_slabs(gw, i, shapes)

    from_sibling = _run_guest(_pair_exchange_guest(slabs), name="grad_pair_exchange_0")[0]
    part = _add_own_half(slabs, from_sibling, core, name="grad_pair_add_0")
    half = _sum_chips(_run_guest(_chip_scatter_guest(part), name="grad_chip_scatter_0")[0], name="grad_chip_sum_0")
    reduced[0] = _run_guest(_pair_share_guest(half), name="grad_pair_share_0")[0]

    g_layers = [_unpack_layer(reduced[i], i, shapes) for i in range(DEPTH)]
    grads = {}
    for n, _ in BIG:
        per_layer = [g_layers[i][n] for i in range(DEPTH) if n in g_layers[i]]
        grads[n] = jnp.stack(per_layer)

    gsmall = {n: jnp.concatenate(v, axis=0) for n, v in gs.items()}
    dlogits = _lower_bounds_bwd(logits2d, jnp.stack(dlb).reshape(DEPTH, 2 * D_MODEL), name="lb_bwd").reshape(DEPTH, 2, D_MODEL)
    g_small_full = _unpack_small(_allreduce_small(_pack_small(gsmall, dlogits), name="allreduce_small"),
                                 (lb_sh[0], lb_sh[1], N_CHIPS * lb_sh[2]))
    grads.update({n: g_small_full[n] for n in SMALL + ("att_sink", "hgrn_norm_g")})
    grads["hgrn_lb_logits"] = lax.dynamic_slice_in_dim(g_small_full["hgrn_lb_logits"], chip * lb_sh[2], lb_sh[2], axis=2)

    delta, new_m, new_v = {}, {}, {}
    for n, _ in BIG:
        two_d = lambda a: a.reshape(-1, a.shape[-1])
        d, nm, nv = _adamw(two_d(wts[n]), two_d(grads[n]), two_d(mom[n]), two_d(var[n]), name=f"adamw_{n}")
        delta[n], new_m[n], new_v[n] = d.reshape(shapes[n]), nm.reshape(shapes[n]), nv.reshape(shapes[n])
    packs = [_pack_small(src, src["hgrn_lb_logits"]) for src in (wts, grads, mom, var)]
    outs = _adamw(*packs, name="adamw_small")
    for dst, buf in zip((delta, new_m, new_v), outs):
        dst.update(_unpack_small(buf, lb_sh))

    return (loss, dx[None], *[grads[n] for n in WEIGHTS], *[delta[n] for n in WEIGHTS],
            *[new_m[n] for n in WEIGHTS], *[new_v[n] for n in WEIGHTS])
```

```python
import functools
from typing import Callable, NamedTuple

import jax
import jax.numpy as jnp
from jax import lax
from jax.experimental import pallas as pl
from jax.experimental.pallas import tpu as pltpu

F32, BF16 = jnp.float32, jnp.bfloat16

D_MODEL = 1024
DEPTH = 4
HEAD_DIM = 64
N_Q_HEADS = 16
N_KV_HEADS = 4
GROUP = 4
Q_DIM = 1024
KV_DIM = 256
WINDOW = 128
ROPE_DIM = 16
ROPE_THETA = 500000.0
HGRN_HEADS = 8
HGRN_KEY = 128
HGRN_CHUNK = 64
D_FF = 2816
PLE_DIM = 256
ALPHA = (2 * DEPTH) ** 0.25
LN_EPS = 1e-5
ADAM_LR, ADAM_B1, ADAM_B2, ADAM_EPS, ADAM_WD, ADAM_STEP = 0.001, 0.9, 0.999, 1e-08, 0.01, 10

LANES = 128
VMEM_LIMIT_BYTES = 56 * 2**20
FACTORED_DECAY_LIMIT = -80.0

NN = ((1,), (0,))
NT = ((1,), (1,))
TN = ((0,), (0,))

N_CHIPS = 4
MESH = pl.DeviceIdType.MESH


def _dot(a, b, dims):
    return lax.dot_general(a, b, (dims, ((), ())), preferred_element_type=F32)


def _cp(*sem):
    return pltpu.CompilerParams(dimension_semantics=sem, vmem_limit_bytes=VMEM_LIMIT_BYTES)


def _rows(t, cap=512):
    return min(cap, t)


def _pick(n, cap):
    best = None
    for d in range(LANES, min(n, cap) + 1, LANES):
        if n % d == 0:
            best = d
    assert best is not None, (n, cap)
    return best


def _sigmoid(x):
    return jax.nn.sigmoid(x)


def _ln_fwd(u, g, b):
    mu = jnp.mean(u, axis=-1, keepdims=True)
    xc = u - mu
    var = jnp.mean(xc * xc, axis=-1, keepdims=True)
    rstd = lax.rsqrt(var + LN_EPS)
    xhat = xc * rstd
    return xhat * g + b, xhat, rstd


def _ln_bwd(dy, xhat, rstd, g):
    dxh = dy * g
    m1 = jnp.mean(dxh, axis=-1, keepdims=True)
    m2 = jnp.mean(dxh * xhat, axis=-1, keepdims=True)
    du = rstd * (dxh - m1 - xhat * m2)
    return du, jnp.sum(dy * xhat, axis=0, keepdims=True), jnp.sum(dy, axis=0, keepdims=True)


def _full(shape):
    nd = len(shape)
    return pl.BlockSpec(shape, lambda *_: (0,) * nd)


ANY = pl.BlockSpec(memory_space=pl.ANY)


class _Guest(NamedTuple):
    args: tuple
    out_shape: tuple
    sems: tuple
    start: Callable
    finish: Callable


def _call(body, *, grid, in_specs, out_specs, out_shape, args, sem, name, scratch_shapes=(), guest=None):
    n_in, n_out, n_scr = len(args), len(out_shape), len(scratch_shapes)
    if guest is None:
        res = pl.pallas_call(body, grid=grid, in_specs=list(in_specs), out_specs=list(out_specs), out_shape=list(out_shape),
                             scratch_shapes=list(scratch_shapes), compiler_params=_cp(*sem), name=name)(*args)
        return list(res), []
    g_in, g_out = len(guest.args), len(guest.out_shape)

    def hosted(*refs):
        cuts = [n_in, g_in, n_out, g_out, n_scr]
        parts, pos = [], 0
        for n in cuts:
            parts.append(refs[pos:pos + n])
            pos += n
        h_in, gi, h_out, go, h_scr = parts
        gs = refs[pos:]
        ids = [pl.program_id(d) for d in range(len(grid))]
        first = functools.reduce(jnp.logical_and, [i == 0 for i in ids])
        last = functools.reduce(jnp.logical_and, [i == n - 1 for i, n in zip(ids, grid)])

        @pl.when(first)
        def _():
            guest.start(gi, go, gs)
        body(*h_in, *h_out, *h_scr)

        @pl.when(last)
        def _():
            guest.finish(gi, go, gs)

    res = pl.pallas_call(
        hosted, grid=grid, in_specs=list(in_specs) + [ANY] * g_in, out_specs=list(out_specs) + [ANY] * g_out,
        out_shape=list(out_shape) + list(guest.out_shape), scratch_shapes=list(scratch_shapes) + list(guest.sems),
        compiler_params=_cp(*(("arbitrary",) * len(grid))), name=name)(*args, *guest.args)
    return list(res[:n_out]), list(res[n_out:])


def _run_guest(guest, *, name):
    g_in = len(guest.args)

    def body(*refs):
        gi, go, gs = refs[:g_in], refs[g_in:g_in + len(guest.out_shape)], refs[g_in + len(guest.out_shape):]
        guest.start(gi, go, gs)
        guest.finish(gi, go, gs)

    return pl.pallas_call(body, in_specs=[ANY] * g_in, out_specs=[ANY] * len(guest.out_shape), out_shape=list(guest.out_shape),
                          scratch_shapes=list(guest.sems), name=name)(*guest.args)


def _mm_nn(a, w, *, out_dtype, name):
    t, k = a.shape
    n = w.shape[1]
    tm, tn = _rows(t), _pick(n, 1408)

    def body(a_ref, w_ref, o_ref):
        o_ref[...] = _dot(a_ref[...], w_ref[...], NN).astype(out_dtype)

    return pl.pallas_call(
        body, grid=(n // tn, t // tm),
        in_specs=[pl.BlockSpec((tm, k), lambda j, i: (i, 0)), pl.BlockSpec((k, tn), lambda j, i: (0, j))],
        out_specs=pl.BlockSpec((tm, tn), lambda j, i: (i, j)),
        out_shape=jax.ShapeDtypeStruct((t, n), out_dtype),
        compiler_params=_cp("parallel", "parallel"), name=name)(a, w)


def _mm_tn(a, b, *, name, guest=None):
    r, m = a.shape
    n = b.shape[1]
    tr, tm, tn = _rows(r), _pick(m, 1408), _pick(n, 2816)

    def body(a_ref, b_ref, o_ref):
        @pl.when(pl.program_id(2) == 0)
        def _():
            o_ref[...] = jnp.zeros_like(o_ref)
        o_ref[...] += _dot(a_ref[...].astype(BF16), b_ref[...].astype(BF16), TN)

    res, extra = _call(
        body, grid=(m // tm, n // tn, r // tr),
        in_specs=[pl.BlockSpec((tr, tm), lambda i, j, k: (k, i)), pl.BlockSpec((tr, tn), lambda i, j, k: (k, j))],
        out_specs=[pl.BlockSpec((tm, tn), lambda i, j, k: (i, j))],
        out_shape=[jax.ShapeDtypeStruct((m, n), F32)], args=(a, b),
        sem=("parallel", "parallel", "arbitrary"), name=name, guest=guest)
    return res[0] if guest is None else (res[0], extra)


def _mm_tn_p(p, layer, b, *, name):
    t = p.shape[2]
    n = b.shape[1]
    tr = _rows(t)

    def body(a_ref, b_ref, o_ref):
        @pl.when(pl.program_id(0) == 0)
        def _():
            o_ref[...] = jnp.zeros_like(o_ref)
        o_ref[...] += _dot(a_ref[...].astype(BF16), b_ref[...], TN)

    return pl.pallas_call(
        body, grid=(t // tr,),
        in_specs=[pl.BlockSpec((None, None, tr, PLE_DIM), lambda k: (layer, 0, k, 0)),
                  pl.BlockSpec((tr, n), lambda k: (k, 0))],
        out_specs=pl.BlockSpec((PLE_DIM, n), lambda k: (0, 0)),
        out_shape=jax.ShapeDtypeStruct((PLE_DIM, n), F32),
        compiler_params=_cp("arbitrary"), name=name)(p, b)


def _mm_res_ln(a, w, resid, g, b, *, name):
    t, k = a.shape
    tm = _rows(t)

    def body(a_ref, w_ref, r_ref, g_ref, b_ref, y_ref, ybf_ref, xh_ref, rs_ref):
        acc = _dot(a_ref[...], w_ref[...], NN)
        y, xh, rs = _ln_fwd(ALPHA * r_ref[...] + acc, g_ref[...], b_ref[...])
        y_ref[...] = y
        ybf_ref[...] = y.astype(BF16)
        xh_ref[...] = xh
        rs_ref[...] = rs

    row = pl.BlockSpec((tm, D_MODEL), lambda i: (i, 0))
    return pl.pallas_call(
        body, grid=(t // tm,),
        in_specs=[pl.BlockSpec((tm, k), lambda i: (i, 0)), _full((k, D_MODEL)), row, _full((1, D_MODEL)), _full((1, D_MODEL))],
        out_specs=[row, row, row, pl.BlockSpec((tm, 1), lambda i: (i, 0))],
        out_shape=[jax.ShapeDtypeStruct((t, D_MODEL), F32), jax.ShapeDtypeStruct((t, D_MODEL), BF16),
                   jax.ShapeDtypeStruct((t, D_MODEL), F32), jax.ShapeDtypeStruct((t, 1), F32)],
        compiler_params=_cp("parallel"), name=name)(a, w, resid, g, b)


def _ffn_in(x_bf, w, *, name, guest=None):
    t = x_bf.shape[0]
    tm, tn = _rows(t), _pick(D_FF, 1408)
    nb = D_FF // tn

    def body(x_ref, wg_ref, wu_ref, dg_ref, du_ref, h_ref):
        x = x_ref[...]
        g = _dot(x, wg_ref[...], NN)
        u = _dot(x, wu_ref[...], NN)
        sig = _sigmoid(g)
        silu = g * sig
        dg_ref[...] = (u * sig * (1.0 + g * (1.0 - sig))).astype(BF16)
        du_ref[...] = silu.astype(BF16)
        h_ref[...] = (silu * u).astype(BF16)

    tile = pl.BlockSpec((tm, tn), lambda j, i: (i, j))
    shp = jax.ShapeDtypeStruct((t, D_FF), BF16)
    return _call(
        body, grid=(nb, t // tm),
        in_specs=[pl.BlockSpec((tm, D_MODEL), lambda j, i: (i, 0)),
                  pl.BlockSpec((D_MODEL, tn), lambda j, i: (0, j)),
                  pl.BlockSpec((D_MODEL, tn), lambda j, i: (0, j + nb))],
        out_specs=[tile, tile, tile], out_shape=[shp, shp, shp], args=(x_bf, w, w),
        sem=("parallel", "parallel"), name=name, guest=guest)


def _ple_fwd(x, x_bf, p, layer, wg, wp, *, name):
    t = x.shape[0]
    tm = _rows(t)

    def body(x_ref, xbf_ref, p_ref, wg_ref, wp_ref, y_ref, ybf_ref, gate_ref, pp_ref):
        gate = _sigmoid(_dot(xbf_ref[...], wg_ref[...], NN))
        pp = _dot(p_ref[...].astype(BF16), wp_ref[...], NN)
        y = x_ref[...] + gate * pp
        y_ref[...] = y
        ybf_ref[...] = y.astype(BF16)
        gate_ref[...] = gate.astype(BF16)
        pp_ref[...] = pp.astype(BF16)

    row = pl.BlockSpec((tm, D_MODEL), lambda i: (i, 0))
    f32, bf = jax.ShapeDtypeStruct((t, D_MODEL), F32), jax.ShapeDtypeStruct((t, D_MODEL), BF16)
    return pl.pallas_call(
        body, grid=(t // tm,),
        in_specs=[row, row, pl.BlockSpec((None, None, tm, PLE_DIM), lambda i: (layer, 0, i, 0)),
                  _full((D_MODEL, D_MODEL)), _full((PLE_DIM, D_MODEL))],
        out_specs=[row, row, row, row], out_shape=[f32, bf, bf, bf],
        compiler_params=_cp("parallel"), name=name)(x, x_bf, p, wg, wp)


def _loss_head(y, target, *, name):
    t = y.shape[0]
    tm = _rows(t)

    def body(y_ref, t_ref, loss_ref, dy_ref):
        e = y_ref[...] - t_ref[...]

        @pl.when(pl.program_id(0) == 0)
        def _():
            loss_ref[...] = jnp.zeros_like(loss_ref)
        sq = jnp.sum(jnp.sum(e * e, axis=1, keepdims=True), axis=0, keepdims=True)
        loss_ref[...] += sq * (0.5 / D_MODEL)
        dy_ref[...] = e * (1.0 / D_MODEL)

    row = pl.BlockSpec((tm, D_MODEL), lambda i: (i, 0))
    return pl.pallas_call(
        body, grid=(t // tm,), in_specs=[row, row],
        out_specs=[_full((1, 1)), row],
        out_shape=[jax.ShapeDtypeStruct((1, 1), F32), jax.ShapeDtypeStruct((t, D_MODEL), F32)],
        compiler_params=_cp("arbitrary"), name=name)(y, target)


def _ple_bwd(dx3, gate, pp, wg, xhat, rstd, g, *, name, guest=None):
    t = dx3.shape[0]
    tm = _rows(t)

    def body(dx_ref, gate_ref, pp_ref, wg_ref, xh_ref, rs_ref, g_ref,
             du_ref, dubf_ref, dpre_ref, dpp_ref, dg_ref, db_ref):
        dx = dx_ref[...]
        gate = gate_ref[...].astype(F32)
        dpre = (dx * pp_ref[...].astype(F32) * gate * (1.0 - gate)).astype(BF16)
        dpre_ref[...] = dpre
        dpp_ref[...] = (dx * gate).astype(BF16)
        dx2 = dx + _dot(dpre, wg_ref[...], NT)
        du, dg, db = _ln_bwd(dx2, xh_ref[...], rs_ref[...], g_ref[...])
        du_ref[...] = du
        dubf_ref[...] = du.astype(BF16)

        @pl.when(pl.program_id(0) == 0)
        def _():
            dg_ref[...] = jnp.zeros_like(dg_ref)
            db_ref[...] = jnp.zeros_like(db_ref)
        dg_ref[...] += dg
        db_ref[...] += db

    row = pl.BlockSpec((tm, D_MODEL), lambda i: (i, 0))
    vec = _full((1, D_MODEL))
    f32, bf = jax.ShapeDtypeStruct((t, D_MODEL), F32), jax.ShapeDtypeStruct((t, D_MODEL), BF16)
    v32 = jax.ShapeDtypeStruct((1, D_MODEL), F32)
    res, extra = _call(
        body, grid=(t // tm,),
        in_specs=[row, row, row, _full((D_MODEL, D_MODEL)), row, pl.BlockSpec((tm, 1), lambda i: (i, 0)), vec],
        out_specs=[row, row, row, row, vec, vec], out_shape=[f32, bf, bf, bf, v32, v32],
        args=(dx3, gate, pp, wg, xhat, rstd, g), sem=("arbitrary",), name=name, guest=guest)
    return res, extra


def _ffn_out_bwd(du_bf, w_out, g, u, *, name):
    t = du_bf.shape[0]
    tm, tn = _rows(t), _pick(D_FF, 1408)

    def body(du_ref, w_ref, hg_ref, hu_ref, dg_ref, dup_ref):
        dh = _dot(du_ref[...], w_ref[...], NT)
        dg_ref[...] = (dh * hg_ref[...].astype(F32)).astype(BF16)
        dup_ref[...] = (dh * hu_ref[...].astype(F32)).astype(BF16)

    tile = pl.BlockSpec((tm, tn), lambda j, i: (i, j))
    shp = jax.ShapeDtypeStruct((t, D_FF), BF16)
    return pl.pallas_call(
        body, grid=(D_FF // tn, t // tm),
        in_specs=[pl.BlockSpec((tm, D_MODEL), lambda j, i: (i, 0)), pl.BlockSpec((tn, D_MODEL), lambda j, i: (j, 0)),
                  tile, tile],
        out_specs=[tile, tile], out_shape=[shp, shp],
        compiler_params=_cp("parallel", "parallel"), name=name)(du_bf, w_out, g, u)


def _mm_nt(parts, w, *, resid=None, ln=None, out_dtype=F32, name, guest=None):
    t, kp = parts[0].shape
    npart = len(parts)
    tm = _rows(t)
    n_in = npart + 1 + (resid is not None) + (3 if ln is not None else 0)

    def body(*refs):
        a_refs, w_ref = refs[:npart], refs[npart]
        pos = npart + 1
        r_ref = None
        if resid is not None:
            r_ref = refs[pos]
            pos += 1
        if ln is not None:
            xh_ref, rs_ref, g_ref = refs[pos:pos + 3]
        outs = refs[n_in:]
        val = _dot(a_refs[0][...], w_ref[:, :kp], NT)
        for pi in range(1, npart):
            val = val + _dot(a_refs[pi][...], w_ref[:, pi * kp:(pi + 1) * kp], NT)
        if r_ref is not None:
            val = val + ALPHA * r_ref[...]
        if ln is None:
            outs[0][...] = val.astype(out_dtype)
        else:
            du, dg, db = _ln_bwd(val, xh_ref[...], rs_ref[...], g_ref[...])
            outs[0][...] = du
            outs[1][...] = du.astype(BF16)

            @pl.when(pl.program_id(0) == 0)
            def _():
                outs[2][...] = jnp.zeros_like(outs[2])
                outs[3][...] = jnp.zeros_like(outs[3])
            outs[2][...] += dg
            outs[3][...] += db

    row = pl.BlockSpec((tm, D_MODEL), lambda i: (i, 0))
    vec = pl.BlockSpec((1, D_MODEL), lambda i: (0, 0))
    in_specs = [pl.BlockSpec((tm, kp), lambda i: (i, 0)) for _ in range(npart)]
    in_specs.append(pl.BlockSpec((D_MODEL, npart * kp), lambda i: (0, 0), pipeline_mode=pl.Buffered(1)))
    args = list(parts) + [w]
    if resid is not None:
        in_specs.append(row)
        args.append(resid)
    if ln is not None:
        in_specs += [row, pl.BlockSpec((tm, 1), lambda i: (i, 0)), vec]
        args += list(ln)
        out_specs = [row, row, vec, vec]
        out_shape = [jax.ShapeDtypeStruct((t, D_MODEL), F32), jax.ShapeDtypeStruct((t, D_MODEL), BF16),
                     jax.ShapeDtypeStruct((1, D_MODEL), F32), jax.ShapeDtypeStruct((1, D_MODEL), F32)]
    else:
        out_specs = [row]
        out_shape = [jax.ShapeDtypeStruct((t, D_MODEL), out_dtype)]
    res, extra = _call(
        body, grid=(t // tm,), in_specs=in_specs, out_specs=out_specs, out_shape=out_shape, args=tuple(args),
        sem=("arbitrary" if ln is not None else "parallel",), name=name, guest=guest)
    res = res if ln is not None else res[0]
    return res if guest is None else (res, extra)


def _rope_tables(t):
    half = ROPE_DIM // 2
    inv = ROPE_THETA ** (-jnp.arange(0, ROPE_DIM, 2, dtype=F32) / ROPE_DIM)
    ang = jnp.arange(t, dtype=F32)[:, None] * inv[None, :]
    cos, sin = jnp.cos(ang), jnp.sin(ang)
    pad = HEAD_DIM - ROPE_DIM
    c = jnp.concatenate([cos, cos, jnp.ones((t, pad), F32)], axis=1)
    s_hi = jnp.concatenate([jnp.zeros((t, half), F32), sin, jnp.zeros((t, pad), F32)], axis=1)
    s_lo = jnp.concatenate([-sin, jnp.zeros((t, half + pad), F32)], axis=1)
    rep = LANES // HEAD_DIM
    return jnp.tile(c, (1, rep)), jnp.tile(s_hi, (1, rep)), jnp.tile(s_lo, (1, rep))


def _rope(x, c, s_hi, s_lo, sign):
    half = ROPE_DIM // 2
    parts = []
    for j in range(x.shape[1] // LANES):
        xg = x[:, j * LANES:(j + 1) * LANES]
        rot = pltpu.roll(xg, half, 1) * s_hi + pltpu.roll(xg, LANES - half, 1) * s_lo
        parts.append(xg * c + sign * rot)
    return jnp.concatenate(parts, axis=1)


def _qkv_rope(x_bf, w, tables, *, name):
    t = x_bf.shape[0]
    tm = _rows(t)
    n = Q_DIM + 2 * KV_DIM

    def body(x_ref, w_ref, c_ref, sh_ref, sl_ref, q_ref, k_ref, v_ref):
        acc = _dot(x_ref[...], w_ref[...], NN)
        c, sh, sl = c_ref[...], sh_ref[...], sl_ref[...]
        q_ref[...] = (_rope(acc[:, :Q_DIM], c, sh, sl, 1.0) * HEAD_DIM ** -0.5).astype(BF16)
        k_ref[...] = _rope(acc[:, Q_DIM:Q_DIM + KV_DIM], c, sh, sl, 1.0).astype(BF16)
        v_ref[...] = acc[:, Q_DIM + KV_DIM:].astype(BF16)

    tab = pl.BlockSpec((tm, LANES), lambda i: (i, 0))
    return pl.pallas_call(
        body, grid=(t // tm,),
        in_specs=[pl.BlockSpec((tm, D_MODEL), lambda i: (i, 0)), _full((D_MODEL, n)), tab, tab, tab],
        out_specs=[pl.BlockSpec((tm, Q_DIM), lambda i: (i, 0)), pl.BlockSpec((tm, KV_DIM), lambda i: (i, 0)),
                   pl.BlockSpec((tm, KV_DIM), lambda i: (i, 0))],
        out_shape=[jax.ShapeDtypeStruct((t, Q_DIM), BF16), jax.ShapeDtypeStruct((t, KV_DIM), BF16),
                   jax.ShapeDtypeStruct((t, KV_DIM), BF16)],
        compiler_params=_cp("parallel"), name=name)(x_bf, w, *tables)


ATT_ROWS = 256
ATT_STRIP = 64


def _window_specs(t, tq):
    r = tq // WINDOW
    last = t // WINDOW - 1
    return [pl.BlockSpec((WINDOW, KV_DIM), lambda i: (jnp.maximum(i * r - 1, 0), 0)),
            pl.BlockSpec((tq, KV_DIM), lambda i: (i, 0)),
            pl.BlockSpec((WINDOW, KV_DIM), lambda i: (jnp.minimum((i + 1) * r, last), 0))]


def _att_valid(base, t):
    rows = GROUP * WINDOW
    qpos = base + (lax.broadcasted_iota(jnp.int32, (rows, 3 * WINDOW), 0) & (WINDOW - 1))
    kpos = base - WINDOW + lax.broadcasted_iota(jnp.int32, (rows, 3 * WINDOW), 1)
    return (jnp.abs(qpos - kpos) <= WINDOW) & (kpos >= 0) & (kpos < t)


def _stack_heads(x, r0, g):
    return jnp.concatenate(
        [x[r0:r0 + WINDOW, (GROUP * g + h) * HEAD_DIM:(GROUP * g + h + 1) * HEAD_DIM] for h in range(GROUP)], axis=0)


def _sink_column(sink_ref, g):
    return jnp.concatenate([jnp.full((WINDOW, 1), sink_ref[GROUP * g + h], F32) for h in range(GROUP)], axis=0)


def _unstack_heads(pieces):
    return jnp.concatenate([pieces[g][h * WINDOW:(h + 1) * WINDOW] for g in range(N_KV_HEADS) for h in range(GROUP)], axis=1)


def _attn_fwd(q, k, v, sink, *, name, guest=None):
    t = q.shape[0]
    tq = min(ATT_ROWS, t)
    nsb = tq // WINDOW

    def body(sink_ref, q_ref, kp_ref, kc_ref, kn_ref, vp_ref, vc_ref, vn_ref, o_ref, l_ref):
        i = pl.program_id(0)
        qv = q_ref[...]
        kw = jnp.concatenate([kp_ref[...], kc_ref[...], kn_ref[...]], axis=0)
        vw = jnp.concatenate([vp_ref[...], vc_ref[...], vn_ref[...]], axis=0)
        ones = jnp.ones((3 * WINDOW, HEAD_DIM), BF16)
        for sb in range(nsb):
            r0 = sb * WINDOW
            bias =jnp.where(_att_valid(i * tq + r0, t)[:WINDOW], 0.0, -1e30)
            pieces = []
            for hh in range(N_Q_HEADS):
                g, h = hh // GROUP, hh % GROUP
                qh = qv[r0:r0 + WINDOW, hh * HEAD_DIM:(hh + 1) * HEAD_DIM]
                kg = kw[r0:r0 + 3 * WINDOW, g * HEAD_DIM:(g + 1) * HEAD_DIM]
                vg = jnp.concatenate([vw[r0:r0 + 3 * WINDOW, g * HEAD_DIM:(g + 1) * HEAD_DIM], ones], axis=1)
                s = _dot(qh, kg, NT) + bias
                snk = sink_ref[hh]
                m = jnp.maximum(jnp.max(s, axis=1, keepdims=True), snk)
                ov = _dot(jnp.exp(s - m).astype(BF16), vg, NN)
                den = ov[:, HEAD_DIM:HEAD_DIM + 1] + jnp.exp(snk - m)
                pieces.append(ov[:, :HEAD_DIM] * (1.0 / den))
                l_ref[g, sb, h * WINDOW:(h + 1) * WINDOW, :] = m + jnp.log(den)
            o_ref[r0:r0 + WINDOW, :] = jnp.concatenate(pieces, axis=1).astype(BF16)

    return _call(
        body, grid=(t // tq,),
        in_specs=[pl.BlockSpec(memory_space=pltpu.SMEM), pl.BlockSpec((tq, Q_DIM), lambda i: (i, 0))]
        + _window_specs(t, tq) + _window_specs(t, tq),
        out_specs=[pl.BlockSpec((tq, Q_DIM), lambda i: (i, 0)),
                   pl.BlockSpec((N_KV_HEADS, nsb, GROUP * WINDOW, 1), lambda i: (0, i, 0, 0))],
        out_shape=[jax.ShapeDtypeStruct((t, Q_DIM), BF16),
                   jax.ShapeDtypeStruct((N_KV_HEADS, t // WINDOW, GROUP * WINDOW, 1), F32)],
        args=(sink, q, k, k, k, v, v, v), sem=("parallel",), name=name, guest=guest)


def _attn_bwd(q, k, v, o, do, lse, sink, *, name):
    t = q.shape[0]
    tq = min(ATT_ROWS, t)
    nsb = tq // WINDOW

    def body(sink_ref, q_ref, o_ref, do_ref, l_ref, kp_ref, kc_ref, kn_ref, vp_ref, vc_ref, vn_ref,
             dq_ref, dkw_ref, dvw_ref, dsink_ref):
        i = pl.program_id(0)
        qv, ov, dov = q_ref[...], o_ref[...], do_ref[...]
        kw = jnp.concatenate([kp_ref[...], kc_ref[...], kn_ref[...]], axis=0)
        vw = jnp.concatenate([vp_ref[...], vc_ref[...], vn_ref[...]], axis=0)
        lane16 = lax.broadcasted_iota(jnp.int32, (1, N_Q_HEADS), 1)
        dsink = jnp.zeros((1, N_Q_HEADS), F32)
        for sb in range(nsb):
            r0 = sb * WINDOW
            valid = _att_valid(i * tq + r0, t)
            dq_p, dk_p, dv_p = [], [], []
            for g in range(N_KV_HEADS):
                qs, dos = _stack_heads(qv, r0, g), _stack_heads(dov, r0, g)
                delta = jnp.sum(dos.astype(F32) * _stack_heads(ov, r0, g).astype(F32), axis=1, keepdims=True)
                kg = kw[r0:r0 + 3 * WINDOW, g * HEAD_DIM:(g + 1) * HEAD_DIM]
                vg = vw[r0:r0 + 3 * WINDOW, g * HEAD_DIM:(g + 1) * HEAD_DIM]
                lse_col = l_ref[g, sb]
                pr = jnp.where(valid, jnp.exp(_dot(qs, kg, NT) - lse_col), 0.0)
                ds = (pr * (_dot(dos, vg, NT) - delta)).astype(BF16)
                dq_p.append(_dot(ds, kg, NN))
                dk_p.append(_dot(ds, qs, TN))
                dv_p.append(_dot(pr.astype(BF16), dos, TN))
                ps = jnp.exp(_sink_column(sink_ref, g) - lse_col) * delta
                for h in range(GROUP):
                    tot = jnp.sum(ps[h * WINDOW:(h + 1) * WINDOW], axis=0, keepdims=True)
                    dsink = dsink - jnp.where(lane16 == GROUP * g + h, tot, 0.0)
            dq_ref[r0:r0 + WINDOW, :] = _unstack_heads(dq_p)
            dkw_ref[sb] = jnp.concatenate(dk_p, axis=1)
            dvw_ref[sb] = jnp.concatenate(dv_p, axis=1)

        @pl.when(i == 0)
        def _():
            dsink_ref[...] = jnp.zeros_like(dsink_ref)
        dsink_ref[...] += dsink

    rowq = pl.BlockSpec((tq, Q_DIM), lambda i: (i, 0))
    win = pl.BlockSpec((nsb, 3 * WINDOW, KV_DIM), lambda i: (i, 0, 0))
    wshape = jax.ShapeDtypeStruct((t // WINDOW, 3 * WINDOW, KV_DIM), F32)
    return pl.pallas_call(
        body, grid=(t // tq,),
        in_specs=[pl.BlockSpec(memory_space=pltpu.SMEM), rowq, rowq, rowq,
                  pl.BlockSpec((N_KV_HEADS, nsb, GROUP * WINDOW, 1), lambda i: (0, i, 0, 0))]
        + _window_specs(t, tq) + _window_specs(t, tq),
        out_specs=[rowq, win, win, _full((1, N_Q_HEADS))],
        out_shape=[jax.ShapeDtypeStruct((t, Q_DIM), F32), wshape, wshape, jax.ShapeDtypeStruct((1, N_Q_HEADS), F32)],
        compiler_params=_cp("arbitrary"), name=name)(sink, q, o, do, lse, k, k, k, v, v, v)


def _attn_post_bwd(dq, dkw, dvw, tables, *, name):
    t = dq.shape[0]
    nb = t // WINDOW
    n = Q_DIM + 2 * KV_DIM

    def body(dq_ref, ka_ref, kb_ref, kc_ref, va_ref, vb_ref, vc_ref, c_ref, sh_ref, sl_ref, o_ref):
        j = pl.program_id(0)
        lo = (j > 0).astype(F32)
        hi = (j < nb - 1).astype(F32)
        c, sh, sl = c_ref[...], sh_ref[...], sl_ref[...]
        dk = ka_ref[...] * hi + kb_ref[...] + kc_ref[...] * lo
        dv = va_ref[...] * hi + vb_ref[...] + vc_ref[...] * lo
        o_ref[:, :Q_DIM] = (_rope(dq_ref[...], c, sh, sl, -1.0) * HEAD_DIM ** -0.5).astype(BF16)
        o_ref[:, Q_DIM:Q_DIM + KV_DIM] = _rope(dk, c, sh, sl, -1.0).astype(BF16)
        o_ref[:, Q_DIM + KV_DIM:] = dv.astype(BF16)

    wins = [pl.BlockSpec((None, WINDOW, KV_DIM), lambda j: (jnp.minimum(j + 1, nb - 1), 0, 0)),
            pl.BlockSpec((None, WINDOW, KV_DIM), lambda j: (j, 1, 0)),
            pl.BlockSpec((None, WINDOW, KV_DIM), lambda j: (jnp.maximum(j - 1, 0), 2, 0))]
    tab = pl.BlockSpec((WINDOW, LANES), lambda j: (j, 0))
    return pl.pallas_call(
        body, grid=(nb,),
        in_specs=[pl.BlockSpec((WINDOW, Q_DIM), lambda j: (j, 0))] + wins + wins + [tab, tab, tab],
        out_specs=pl.BlockSpec((WINDOW, n), lambda j: (j, 0)),
        out_shape=jax.ShapeDtypeStruct((t, n), BF16),
        compiler_params=_cp("parallel"), name=name)(dq, dkw, dkw, dkw, dvw, dvw, dvw, *tables)


HGRN_ROWS = 512
HGRN_UNROLL = 2


def _cum_mask(rev, transpose=False):
    row = lax.broadcasted_iota(jnp.int32, (HGRN_CHUNK, HGRN_CHUNK), 0)
    col = lax.broadcasted_iota(jnp.int32, (HGRN_CHUNK, HGRN_CHUNK), 1)
    return (row <= col) if rev != transpose else (row >= col)


def _gates(zq, zf, lb):
    q = zq * _sigmoid(zq)
    sig = _sigmoid(zf)
    f = lb + (1.0 - lb) * sig
    k = (1.0 - lb) * (1.0 - sig)
    return q, sig, f, k


def _intra_exact(q, k, b, mask):
    rows = lax.broadcasted_iota(jnp.int32, (HGRN_CHUNK, 1), 0)
    cols = lax.broadcasted_iota(jnp.int32, (HGRN_CHUNK, HGRN_CHUNK), 1)

    def it(s, a):
        pick = rows == s
        b_s = jnp.sum(jnp.where(pick, b, 0.0), axis=0, keepdims=True)
        k_s = jnp.sum(jnp.where(pick, k, 0.0), axis=0, keepdims=True)
        col = jnp.sum(q * jnp.exp(jnp.minimum(b - b_s, 0.0)) * k_s, axis=1, keepdims=True)
        return jnp.where(cols == s, col, a)

    a = lax.fori_loop(0, HGRN_CHUNK, it, jnp.zeros((HGRN_CHUNK, HGRN_CHUNK), F32))
    return jnp.where(mask, a, 0.0)


def _intra_exact_bwd(q, k, b, da):
    rows = lax.broadcasted_iota(jnp.int32, (HGRN_CHUNK, 1), 0)
    cols = lax.broadcasted_iota(jnp.int32, (HGRN_CHUNK, HGRN_CHUNK), 1)

    def it(s, carry):
        dq, dk = carry
        pick = rows == s
        b_s = jnp.sum(jnp.where(pick, b, 0.0), axis=0, keepdims=True)
        k_s = jnp.sum(jnp.where(pick, k, 0.0), axis=0, keepdims=True)
        w = jnp.sum(jnp.where(cols == s, da, 0.0), axis=1, keepdims=True) * jnp.exp(jnp.minimum(b - b_s, 0.0))
        dq = dq + w * k_s
        dk = jnp.where(pick, jnp.sum(w * q, axis=0, keepdims=True), dk)
        return dq, dk

    z = jnp.zeros((HGRN_CHUNK, HGRN_KEY), F32)
    return lax.fori_loop(0, HGRN_CHUNK, it, (z, z))


def _chunk_cumsum(g, from_end):
    n = g.shape[0]
    row = lax.broadcasted_iota(jnp.int32, g.shape, 0)
    x, s = g, 1
    while s < n:
        if from_end:
            x = x + jnp.where(row < n - s, pltpu.roll(x, n - s, 0), 0.0)
        else:
            x = x + jnp.where(row >= s, pltpu.roll(x, s, 0), 0.0)
        s *= 2
    return x


def _head(a, h):
    return a[:, h * LANES:(h + 1) * LANES]


def _block_is_mild(zf_ref, lb, nch):
    g = jnp.log(lb + (1.0 - lb) * _sigmoid(zf_ref[...]))
    lows = [jnp.min(jnp.sum(g[c * HGRN_CHUNK:(c + 1) * HGRN_CHUNK], axis=0, keepdims=True)) for c in range(nch)]
    return functools.reduce(jnp.minimum, lows) >= FACTORED_DECAY_LIMIT


def _hgrn_fwd(z, lb, rev, *, name, guest=None):
    t = z.shape[0]
    rb = min(HGRN_ROWS, t)
    nb, nch = t // rb, rb // HGRN_CHUNK
    heads = range(HGRN_HEADS)

    def blk(n):
        return nb - 1 - n if rev else n

    def body(zq_ref, zf_ref, zv_ref, lb_ref, o_ref, s_ref, st_ref):
        @pl.when(pl.program_id(0) == 0)
        def _():
            st_ref[...] = jnp.zeros_like(st_ref)
        lbv = lb_ref[...]
        cmask = _cum_mask(rev)

        def chunk(c, carry, mild):
            cc = nch - 1 - c if rev else c
            sl = pl.ds(pl.multiple_of(cc * HGRN_CHUNK, HGRN_CHUNK), HGRN_CHUNK)
            q, _, f, k = _gates(zq_ref[sl, :], zf_ref[sl, :], lbv)
            v = zv_ref[sl, :].astype(BF16)
            g = jnp.log(f)
            b = _chunk_cumsum(g, rev)
            tot = jnp.sum(g, axis=0, keepdims=True)
            qd = (q * jnp.exp(b)).astype(BF16)
            kd = (k * jnp.exp(tot - b)).astype(BF16)
            etot = jnp.exp(tot)

            def factored():
                kf = (k * jnp.exp(-b)).astype(BF16)
                return tuple(jnp.where(cmask, _dot(_head(qd, h), _head(kf, h), NT), 0.0) for h in heads)

            def exact():
                return tuple(_intra_exact(_head(q, h), _head(k, h), _head(b, h), cmask) for h in heads)

            a = factored() if mild else lax.cond(jnp.min(tot) >= FACTORED_DECAY_LIMIT, factored, exact)
            for h in heads:
                st = st_ref[h]
                st_bf = st.astype(BF16)
                s_ref[h, cc] = st_bf
                o_ref[sl, h * LANES:(h + 1) * LANES] = (
                    _dot(_head(qd, h), st_bf, NT) + _dot(a[h].astype(BF16), _head(v, h), NN))
                st_ref[h] = st * _head(etot, h) + _dot(_head(v, h), _head(kd, h), TN)
            return carry

        lax.cond(_block_is_mild(zf_ref, lbv, nch),
                 lambda: lax.fori_loop(0, nch, functools.partial(chunk, mild=True), 0, unroll=HGRN_UNROLL),
                 lambda: lax.fori_loop(0, nch, functools.partial(chunk, mild=False), 0))

    def col(j):
        return pl.BlockSpec((rb, D_MODEL), lambda n: (blk(n), j))

    return _call(
        body, grid=(nb,),
        in_specs=[col(0), col(2 if rev else 1), col(3), _full((1, D_MODEL))],
        out_specs=[col(0), pl.BlockSpec((HGRN_HEADS, nch, LANES, LANES), lambda n: (0, blk(n), 0, 0))],
        out_shape=[jax.ShapeDtypeStruct((t, D_MODEL), F32),
                   jax.ShapeDtypeStruct((HGRN_HEADS, t // HGRN_CHUNK, LANES, LANES), BF16)],
        scratch_shapes=[pltpu.VMEM((HGRN_HEADS, LANES, LANES), F32)],
        args=(z, z, z, lb), sem=("arbitrary",), name=name, guest=guest)


def _hgrn_bwd(z, lb, d_o, states, rev, *, name, other=None):
    t = z.shape[0]
    rb = min(HGRN_ROWS, t)
    nb, nch = t // rb, rb // HGRN_CHUNK
    heads = range(HGRN_HEADS)
    n_other = 0 if other is None else 2
    acc_dtype = F32 if other is None else BF16

    def blk(n):
        return n if rev else nb - 1 - n

    def body(zq_ref, zf_ref, zv_ref, lb_ref, do_ref, s_ref, *rest):
        oq_ref, ov_ref = rest[:n_other] if other is not None else (None, None)
        dq_ref, dzf_ref, dv_ref, dlb_ref, dst_ref = rest[n_other:]

        @pl.when(pl.program_id(0) == 0)
        def _():
            dst_ref[...] = jnp.zeros_like(dst_ref)
            dlb_ref[...] = jnp.zeros_like(dlb_ref)
        lbv = lb_ref[...]
        cmask = _cum_mask(rev)

        def chunk(c, carry, mild):
            cc = c if rev else nch - 1 - c
            sl = pl.ds(pl.multiple_of(cc * HGRN_CHUNK, HGRN_CHUNK), HGRN_CHUNK)
            zq = zq_ref[sl, :]
            q, sig, f, k = _gates(zq, zf_ref[sl, :], lbv)
            v = zv_ref[sl, :].astype(BF16)
            g = jnp.log(f)
            b = _chunk_cumsum(g, rev)
            tot = jnp.sum(g, axis=0, keepdims=True)
            eb = jnp.exp(b)
            etb = jnp.exp(tot - b)
            etot = jnp.exp(tot)
            qd = (q * eb).astype(BF16)
            kd = (k * etb).astype(BF16)
            do = do_ref[sl, :].astype(BF16)
            da = [jnp.where(cmask, _dot(_head(do, h), _head(v, h), NT), 0.0) for h in heads]

            def factored():
                ke = jnp.exp(-b)
                kf = (k * ke).astype(BF16)
                res = []
                for h in heads:
                    qh, kh = _head(qd, h), _head(kf, h)
                    da_bf = da[h].astype(BF16)
                    dqf, dkf = _dot(da_bf, kh, NN), _dot(da_bf, qh, TN)
                    res.append((jnp.where(cmask, _dot(qh, kh, NT), 0.0), dqf * _head(eb, h), dkf * _head(ke, h),
                                qh.astype(F32) * dqf - kh.astype(F32) * dkf))
                return tuple(res)

            def exact():
                res = []
                for h in heads:
                    qh, kh, bh = _head(q, h), _head(k, h), _head(b, h)
                    dq_i, dk_i = _intra_exact_bwd(qh, kh, bh, da[h])
                    res.append((_intra_exact(qh, kh, bh, cmask), dq_i, dk_i, qh * dq_i - kh * dk_i))
                return tuple(res)

            intra = factored() if mild else lax.cond(jnp.min(tot) >= FACTORED_DECAY_LIMIT, factored, exact)
            dq_p, dk_p, db_p, dtot_p = [], [], [], []
            for h in heads:
                a, dq_i, dk_i, db_i = intra[h]
                doh, vh, qh, kh = _head(do, h), _head(v, h), _head(q, h), _head(k, h)
                st0 = s_ref[h, cc]
                dst = dst_ref[h]
                dst_bf = dst.astype(BF16)
                dv = _dot(a.astype(BF16), doh, TN) + _dot(_head(kd, h), dst_bf, NT)
                if ov_ref is not None:
                    dv = dv + ov_ref[sl, h * LANES:(h + 1) * LANES]
                dv_ref[sl, h * LANES:(h + 1) * LANES] = dv.astype(acc_dtype)
                dq_x = _dot(doh, st0, NN) * _head(eb, h)
                dk_x = _dot(vh, dst_bf, NN) * _head(etb, h)
                dtot_p.append(jnp.sum(kh * dk_x, axis=0, keepdims=True)
                              + _head(etot, h) * jnp.sum(dst * st0.astype(F32), axis=0, keepdims=True))
                dst_ref[h] = dst * _head(etot, h) + _dot(doh, _head(qd, h), TN)
                dq_p.append(dq_i + dq_x)
                dk_p.append(dk_i + dk_x)
                db_p.append(db_i + qh * dq_x - kh * dk_x)
            dq, dk, db = (jnp.concatenate(x, axis=1) for x in (dq_p, dk_p, db_p))
            dg = _chunk_cumsum(db, not rev) + jnp.concatenate(dtot_p, axis=1)
            sq = _sigmoid(zq)
            dq_pre = dq * sq * (1.0 + zq * (1.0 - sq))
            if oq_ref is not None:
                dq_pre = dq_pre + oq_ref[sl, :]
            dq_ref[sl, :] = dq_pre.astype(acc_dtype)
            dfk = dg / f - dk
            dzf_ref[sl, :] = (dfk * (1.0 - lbv) * sig * (1.0 - sig)).astype(BF16)
            dlb_ref[...] += jnp.sum(dfk * (1.0 - sig), axis=0, keepdims=True)
            return carry

        lax.cond(_block_is_mild(zf_ref, lbv, nch),
                 lambda: lax.fori_loop(0, nch, functools.partial(chunk, mild=True), 0, unroll=HGRN_UNROLL),
                 lambda: lax.fori_loop(0, nch, functools.partial(chunk, mild=False), 0))

    def col(j):
        return pl.BlockSpec((rb, D_MODEL), lambda n: (blk(n), j))

    return pl.pallas_call(
        body, grid=(nb,),
        in_specs=[col(0), col(2 if rev else 1), col(3), _full((1, D_MODEL)), col(0),
                  pl.BlockSpec((HGRN_HEADS, nch, LANES, LANES), lambda n: (0, blk(n), 0, 0))] + [col(0)] * n_other,
        out_specs=[col(0), col(0), col(0), _full((1, D_MODEL))],
        out_shape=[jax.ShapeDtypeStruct((t, D_MODEL), acc_dtype), jax.ShapeDtypeStruct((t, D_MODEL), BF16),
                   jax.ShapeDtypeStruct((t, D_MODEL), acc_dtype), jax.ShapeDtypeStruct((1, D_MODEL), F32)],
        scratch_shapes=[pltpu.VMEM((HGRN_HEADS, LANES, LANES), F32)],
        compiler_params=_cp("arbitrary"), name=name)(z, z, z, lb, d_o, states, *(other or ()))


def _hgrn_post_fwd(o_f, o_b, z, norm_g, *, name):
    t = o_f.shape[0]
    tm = _rows(t)

    def body(of_ref, ob_ref, gate_ref, ng_ref, y_ref):
        ng = ng_ref[...]
        for h in range(HGRN_HEADS):
            sl = slice(h * LANES, (h + 1) * LANES)
            o = of_ref[:, sl] + ob_ref[:, sl]
            r = lax.rsqrt(jnp.mean(o * o, axis=1, keepdims=True) + LN_EPS)
            gt = gate_ref[:, sl]
            y_ref[:, sl] = (o * r * ng * (gt * _sigmoid(gt))).astype(BF16)

    row = pl.BlockSpec((tm, D_MODEL), lambda i: (i, 0))
    return pl.pallas_call(
        body, grid=(t // tm,),
        in_specs=[row, row, pl.BlockSpec((tm, D_MODEL), lambda i: (i, 4)), _full((1, LANES))],
        out_specs=row, out_shape=jax.ShapeDtypeStruct((t, D_MODEL), BF16),
        compiler_params=_cp("parallel"), name=name)(o_f, o_b, z, norm_g)


def _hgrn_post_bwd(dy, o_f, o_b, z, norm_g, *, name):
    t = o_f.shape[0]
    tm = _rows(t)

    def body(dy_ref, of_ref, ob_ref, gate_ref, ng_ref, do_ref, dgate_ref, dng_ref):
        ng = ng_ref[...]
        dng = jnp.zeros((1, LANES), F32)
        for h in range(HGRN_HEADS):
            sl = slice(h * LANES, (h + 1) * LANES)
            o = of_ref[:, sl] + ob_ref[:, sl]
            r = lax.rsqrt(jnp.mean(o * o, axis=1, keepdims=True) + LN_EPS)
            gt = gate_ref[:, sl]
            sg = _sigmoid(gt)
            dyv = dy_ref[:, sl].astype(F32)
            xn = o * r
            dgate_ref[:, sl] = (dyv * xn * ng * sg * (1.0 + gt * (1.0 - sg))).astype(BF16)
            dn = dyv * gt * sg
            dng = dng + jnp.sum(dn * xn, axis=0, keepdims=True)
            dxn = dn * ng
            do_ref[:, sl] = r * (dxn - xn * jnp.mean(dxn * xn, axis=1, keepdims=True))

        @pl.when(pl.program_id(0) == 0)
        def _():
            dng_ref[...] = jnp.zeros_like(dng_ref)
        dng_ref[...] += dng

    row = pl.BlockSpec((tm, D_MODEL), lambda i: (i, 0))
    return pl.pallas_call(
        body, grid=(t // tm,),
        in_specs=[row, row, row, pl.BlockSpec((tm, D_MODEL), lambda i: (i, 4)), _full((1, LANES))],
        out_specs=[row, row, _full((1, LANES))],
        out_shape=[jax.ShapeDtypeStruct((t, D_MODEL), F32), jax.ShapeDtypeStruct((t, D_MODEL), BF16),
                   jax.ShapeDtypeStruct((1, LANES), F32)],
        compiler_params=_cp("arbitrary"), name=name)(dy, o_f, o_b, z, norm_g)


def _hgrn_combine(dq_f, dq_b, dzf_f, dzf_b, dv_f, dv_b, dgate, *, name):
    t = dq_f.shape[0]
    tm = _rows(t)

    def body(qf, qb, ff, fb, vf, vb, gt, o_ref):
        o_ref[:, 0 * D_MODEL:1 * D_MODEL] = (qf[...] + qb[...]).astype(BF16)
        o_ref[:, 1 * D_MODEL:2 * D_MODEL] = ff[...]
        o_ref[:, 2 * D_MODEL:3 * D_MODEL] = fb[...]
        o_ref[:, 3 * D_MODEL:4 * D_MODEL] = (vf[...] + vb[...]).astype(BF16)
        o_ref[:, 4 * D_MODEL:5 * D_MODEL] = gt[...]

    row = pl.BlockSpec((tm, D_MODEL), lambda i: (i, 0))
    return pl.pallas_call(
        body, grid=(t // tm,), in_specs=[row] * 7,
        out_specs=pl.BlockSpec((tm, 5 * D_MODEL), lambda i: (i, 0)),
        out_shape=jax.ShapeDtypeStruct((t, 5 * D_MODEL), BF16),
        compiler_params=_cp("parallel"), name=name)(dq_f, dq_b, dzf_f, dzf_b, dv_f, dv_b, dgate)


def _lower_bounds(logits2d, *, name):
    def body(l_ref, lb_ref):
        l = l_ref[...]
        e = jnp.exp(l - jnp.max(l, axis=0, keepdims=True))
        sm = e / jnp.sum(e, axis=0, keepdims=True)
        s1, s2, s3 = sm[1:2], sm[2:3], sm[3:4]
        lb_ref[...] = jnp.concatenate([jnp.zeros_like(s1), s1, s1 + s2, s1 + s2 + s3], axis=0)

    return pl.pallas_call(body, out_shape=jax.ShapeDtypeStruct(logits2d.shape, F32), name=name)(logits2d)


def _lower_bounds_bwd(logits2d, dlb, *, name):
    def body(l_ref, d_ref, o_ref):
        l = l_ref[...]
        e = jnp.exp(l - jnp.max(l, axis=0, keepdims=True))
        sm = e / jnp.sum(e, axis=0, keepdims=True)
        d = d_ref[...]
        d1, d2, d3 = d[1:2], d[2:3], d[3:4]
        dsm = jnp.concatenate([jnp.zeros_like(d1), d1 + d2 + d3, d2 + d3, d3], axis=0)
        o_ref[...] = sm * (dsm - jnp.sum(sm * dsm, axis=0, keepdims=True))

    return pl.pallas_call(body, out_shape=jax.ShapeDtypeStruct(logits2d.shape, F32), name=name)(logits2d, dlb)


def _adamw(w, g, m, v, *, name):
    r, c = w.shape
    tr = r if r <= 512 else _pick_rows(r)

    def body(w_ref, g_ref, m_ref, v_ref, d_ref, nm_ref, nv_ref):
        gv = g_ref[...]
        nm = ADAM_B1 * m_ref[...] + (1.0 - ADAM_B1) * gv
        nv = ADAM_B2 * v_ref[...] + (1.0 - ADAM_B2) * (gv * gv)
        m_hat = nm / (1.0 - ADAM_B1 ** ADAM_STEP)
        v_hat = nv / (1.0 - ADAM_B2 ** ADAM_STEP)
        d_ref[...] = -ADAM_LR * (m_hat / (jnp.sqrt(v_hat) + ADAM_EPS) + ADAM_WD * w_ref[...])
        nm_ref[...] = nm
        nv_ref[...] = nv

    blk = pl.BlockSpec((tr, c), lambda i: (i, 0))
    shp = jax.ShapeDtypeStruct((r, c), F32)
    return pl.pallas_call(body, grid=(r // tr,), in_specs=[blk] * 4, out_specs=[blk] * 3, out_shape=[shp] * 3,
                          compiler_params=_cp("parallel"), name=name)(w, g, m, v)


def _pick_rows(r, cap=512):
    best = 8
    for d in range(8, cap + 1, 8):
        if r % d == 0:
            best = d
    assert r % best == 0, r
    return best


def _local_step(x, p, target, w, sp):
    t = x.shape[0]
    tables = _rope_tables(t)
    logits2d = sp["hgrn_lb_logits"].reshape(DEPTH, 2 * D_MODEL)
    lb_all = _lower_bounds(logits2d, name="lb_fwd").reshape(DEPTH, 2, 1, D_MODEL)
    row = lambda a, i: a[i][None]

    saved = []
    xf, xb = x, x.astype(BF16)
    for i in range(DEPTH):
        j = i // 2
        s = dict(xin_bf=xb)
        if i % 2 == 0:
            q, k, v = _qkv_rope(xb, w["att_w_qkv"][j], tables, name=f"qkv_rope_{i}")
            o, lse = _attn_fwd(q, k, v, sp["att_sink"][j], name=f"attn_fwd_{i}")
            s.update(q=q, k=k, v=v, o=o, lse=lse)
            mix_in, w_o = o, w["att_w_o"][j]
        else:
            z = _mm_nn(xb, w["hgrn_w_in"][j], out_dtype=F32, name=f"hgrn_in_{i}")
            o_f, s_f = _hgrn_fwd(z, lb_all[i, 0], False, name=f"hgrn_scan_f_{i}")
            o_b, s_b = _hgrn_fwd(z, lb_all[i, 1], True, name=f"hgrn_scan_b_{i}")
            og = _hgrn_post_fwd(o_f, o_b, z, row(sp["hgrn_norm_g"], j), name=f"hgrn_post_{i}")
            s.update(z=z, o_f=o_f, o_b=o_b, s_f=s_f, s_b=s_b, og=og)
            mix_in, w_o = og, w["hgrn_w_o"][j]
        x1, x1b, xh1, rs1 = _mm_res_ln(mix_in, w_o, xf, row(sp["ln_mix_g"], i), row(sp["ln_mix_b"], i), name=f"mix_out_ln_{i}")
        g, u, h = _ffn_in(x1b, w["ffn_w_in"][i], name=f"ffn_in_{i}")
        x2, x2b, xh2, rs2 = _mm_res_ln(h, w["ffn_w_out"][i], x1, row(sp["ln_ffn_g"], i), row(sp["ln_ffn_b"], i), name=f"ffn_out_ln_{i}")
        xf, xb, gate, pp = _ple_fwd(x2, x2b, p, i, w["ple_w_gate"][i], w["ple_w_proj"][i], name=f"ple_fwd_{i}")
        s.update(x1b=x1b, xh1=xh1, rs1=rs1, g=g, u=u, h=h, x2b=x2b, xh2=xh2, rs2=rs2, gate=gate, pp=pp)
        saved.append(s)

    loss, dx = _loss_head(xf, target, name="loss_head")

    gw = {n: [None] * w[n].shape[0] for n in w}
    gs = {n: [None] * DEPTH for n in ("ln_mix_g", "ln_mix_b", "ln_ffn_g", "ln_ffn_b")}
    gs.update(att_sink=[None] * 2, hgrn_norm_g=[None] * 2)
    dlb = [jnp.zeros((2, D_MODEL), F32)] * DEPTH
    for i in reversed(range(DEPTH)):
        j = i // 2
        s = saved[i]
        du2, du2b, dpre, dpp, gs["ln_ffn_g"][i], gs["ln_ffn_b"][i] = _ple_bwd(
            dx, s["gate"], s["pp"], w["ple_w_gate"][i], s["xh2"], s["rs2"], row(sp["ln_ffn_g"], i), name=f"ple_bwd_{i}")
        gw["ple_w_gate"][i] = _mm_tn(s["x2b"], dpre, name=f"dw_ple_gate_{i}")
        gw["ple_w_proj"][i] = _mm_tn_p(p, i, dpp, name=f"dw_ple_proj_{i}")
        dgg, dgu = _ffn_out_bwd(du2b, w["ffn_w_out"][i], s["g"], s["u"], name=f"ffn_out_bwd_{i}")
        gw["ffn_w_out"][i] = _mm_tn(s["h"], du2b, name=f"dw_ffn_out_{i}")
        du1, du1b, gs["ln_mix_g"][i], gs["ln_mix_b"][i] = _mm_nt(
            [dgg, dgu], w["ffn_w_in"][i], tk=_pick(D_FF, 1408), resid=du2,
            ln=(s["xh1"], s["rs1"], row(sp["ln_mix_g"], i)), name=f"ffn_in_bwd_{i}")
        gw["ffn_w_in"][i] = jnp.concatenate(
            [_mm_tn(s["x1b"], dgg, name=f"dw_ffn_gate_{i}"), _mm_tn(s["x1b"], dgu, name=f"dw_ffn_up_{i}")], axis=1)
        if i % 2 == 0:
            gw["att_w_o"][j] = _mm_tn(s["o"], du1b, name=f"dw_att_o_{i}")
            do = _mm_nt([du1b], w["att_w_o"][j], tk=Q_DIM, out_dtype=BF16, name=f"att_o_bwd_{i}")
            dq, dkw, dvw, gs["att_sink"][j] = _attn_bwd(
                s["q"], s["k"], s["v"], s["o"], do, s["lse"], sp["att_sink"][j], name=f"attn_bwd_{i}")
            dqkv = _attn_post_bwd(dq, dkw, dvw, tables, name=f"attn_post_bwd_{i}")
            gw["att_w_qkv"][j] = _mm_tn(s["xin_bf"], dqkv, name=f"dw_att_qkv_{i}")
            dx = _mm_nt([dqkv], w["att_w_qkv"][j], tk=Q_DIM + 2 * KV_DIM, resid=du1, name=f"qkv_bwd_{i}")
        else:
            gw["hgrn_w_o"][j] = _mm_tn(s["og"], du1b, name=f"dw_hgrn_o_{i}")
            dy = _mm_nt([du1b], w["hgrn_w_o"][j], tk=D_MODEL, out_dtype=BF16, name=f"hgrn_o_bwd_{i}")
            d_o, dgate, gs["hgrn_norm_g"][j] = _hgrn_post_bwd(
                dy, s["o_f"], s["o_b"], s["z"], row(sp["hgrn_norm_g"], j), name=f"hgrn_post_bwd_{i}")
            dq_f, dzf_f, dv_f, dlb_f = _hgrn_bwd(s["z"], lb_all[i, 0], d_o, s["s_f"], False, name=f"hgrn_scan_f_bwd_{i}")
            dq_b, dzf_b, dv_b, dlb_b = _hgrn_bwd(s["z"], lb_all[i, 1], d_o, s["s_b"], True, name=f"hgrn_scan_b_bwd_{i}")
            dlb[i] = jnp.stack([dlb_f.reshape(D_MODEL), dlb_b.reshape(D_MODEL)])
            dz = _hgrn_combine(dq_f, dq_b, dzf_f, dzf_b, dv_f, dv_b, dgate, name=f"hgrn_combine_{i}")
            gw["hgrn_w_in"][j] = _mm_tn(s["xin_bf"], dz, name=f"dw_hgrn_in_{i}")
            dx = _mm_nt([dz], w["hgrn_w_in"][j], tk=_pick(5 * D_MODEL, 1408), resid=du1, name=f"hgrn_in_bwd_{i}")

    gw = {n: jnp.stack(v) for n, v in gw.items()}
    gsmall = {n: jnp.concatenate(v, axis=0) for n, v in gs.items()}
    gsmall["hgrn_lb_logits"] = _lower_bounds_bwd(
        logits2d, jnp.stack(dlb).reshape(DEPTH, 2 * D_MODEL), name="lb_bwd").reshape(DEPTH, 2, D_MODEL)
    return loss, dx, gw, gsmall


ANY = pl.BlockSpec(memory_space=pl.ANY)


def _place():
    x, y, c = lax.axis_index("x"), lax.axis_index("y"), lax.axis_index("c")
    chips = [(1 - x, y), (x, 1 - y), (1 - x, 1 - y)]
    return x, y, c, chips


def _remote(src, dst, send_sem, recv_sem, to):
    return pltpu.make_async_remote_copy(src_ref=src, dst_ref=dst, send_sem=send_sem, recv_sem=recv_sem,
                                        device_id=to, device_id_type=MESH)


def _allgather_weights(packed, *, name):
    r = packed.shape[0]
    hr = r // 2

    def body(src_ref, out_ref, send_sems, recv_sems, local_sem):
        x, y, c, chips = _place()
        sibling = (x, y, 1 - c)

        def half(cx, cy, hc):
            return out_ref.at[2 * cx + cy, pl.ds(hc * hr, hr), :]

        mine = pltpu.make_async_copy(src_ref, out_ref.at[2 * x + y], local_sem)
        mine.start()
        my_half = src_ref.at[pl.ds(c * hr, hr), :]
        first = [_remote(my_half, half(x, y, c), send_sems.at[j], recv_sems.at[j], (*chip, c)) for j, chip in enumerate(chips)]
        for cp in first:
            cp.start()
        passed = [_remote(half(*chip, c), half(*chip, c), send_sems.at[3 + j], recv_sems.at[3 + j], sibling)
                  for j, chip in enumerate(chips)]
        for j, chip in enumerate(chips):
            _remote(my_half, half(*chip, c), send_sems.at[j], recv_sems.at[j], (*chip, c)).wait_recv()
            passed[j].start()
        for j, chip in enumerate(chips):
            _remote(my_half, half(*chip, 1 - c), send_sems.at[3 + j], recv_sems.at[3 + j], sibling).wait_recv()
        for cp in first + passed:
            cp.wait_send()
        mine.wait()

    return pl.pallas_call(
        body, in_specs=[ANY], out_specs=ANY, out_shape=jax.ShapeDtypeStruct((N_CHIPS, r, packed.shape[1]), packed.dtype),
        scratch_shapes=[pltpu.SemaphoreType.DMA((6,)), pltpu.SemaphoreType.DMA((6,)), pltpu.SemaphoreType.DMA],
        name=name)(packed)


def _allreduce_small(block, *, name):
    m, n = block.shape

    def body(x_ref, sum_ref, all_ref, send_sems, recv_sems):
        x, y, c, chips = _place()
        me, sibling = (x, y, c), (x, y, 1 - c)

        def rows(px, py, pc):
            return all_ref.at[pl.ds((4 * px + 2 * py + pc) * m, m), :]

        all_ref[pl.ds((4 * x + 2 * y + c) * m, m), :] = x_ref[...]
        first = [_remote(x_ref, rows(*me), send_sems.at[0], recv_sems.at[0], sibling)]
        first += [_remote(x_ref, rows(*me), send_sems.at[1 + j], recv_sems.at[1 + j], (*chip, c)) for j, chip in enumerate(chips)]
        for cp in first:
            cp.start()
        passed = [_remote(rows(*chip, c), rows(*chip, c), send_sems.at[4 + j], recv_sems.at[4 + j], sibling)
                  for j, chip in enumerate(chips)]
        for j, chip in enumerate(chips):
            _remote(x_ref, rows(*chip, c), send_sems.at[1 + j], recv_sems.at[1 + j], me).wait_recv()
            passed[j].start()
        _remote(x_ref, rows(*sibling), send_sems.at[0], recv_sems.at[0], me).wait_recv()
        for j, chip in enumerate(chips):
            _remote(x_ref, rows(*chip, 1 - c), send_sems.at[4 + j], recv_sems.at[4 + j], me).wait_recv()
        for cp in first + passed:
            cp.wait_send()
        acc = all_ref[pl.ds(0, m), :]
        for d in range(1, 8):
            acc = acc + all_ref[pl.ds(d * m, m), :]
        sum_ref[...] = acc

    vmem = pl.BlockSpec(memory_space=pltpu.VMEM)
    return pl.pallas_call(
        body, in_specs=[vmem], out_specs=vmem, out_shape=jax.ShapeDtypeStruct((m, n), F32),
        scratch_shapes=[pltpu.VMEM((8 * m, n), F32), pltpu.SemaphoreType.DMA((7,)), pltpu.SemaphoreType.DMA((7,))],
        name=name)(block)


def _pair_exchange(g, *, name):
    n, r, w = g.shape
    hr = r // 2

    def body(g_ref, out_ref, send_sem, recv_sem):
        x, y, c, _ = _place()
        cp = _remote(g_ref.at[:, pl.ds((1 - c) * hr, hr), :], out_ref, send_sem, recv_sem, (x, y, 1 - c))
        cp.start()
        cp.wait()

    return pl.pallas_call(
        body, in_specs=[ANY], out_specs=ANY, out_shape=jax.ShapeDtypeStruct((n, hr, w), g.dtype),
        scratch_shapes=[pltpu.SemaphoreType.DMA, pltpu.SemaphoreType.DMA], name=name)(g)


def _chip_scatter(part, *, name):
    n, h, w = part.shape

    def body(p_ref, out_ref, send_sems, recv_sems, local_sem):
        x, y, c, chips = _place()
        me = 2 * x + y
        mine = pltpu.make_async_copy(p_ref.at[me], out_ref.at[me], local_sem)
        mine.start()
        sends = [_remote(p_ref.at[2 * cx + cy], out_ref.at[me], send_sems.at[j], recv_sems.at[j], (cx, cy, c))
                 for j, (cx, cy) in enumerate(chips)]
        for cp in sends:
            cp.start()
        for j, (cx, cy) in enumerate(chips):
            _remote(p_ref.at[me], out_ref.at[2 * cx + cy], send_sems.at[j], recv_sems.at[j], (cx, cy, c)).wait_recv()
        for cp in sends:
            cp.wait_send()
        mine.wait()

    return pl.pallas_call(
        body, in_specs=[ANY], out_specs=ANY, out_shape=jax.ShapeDtypeStruct((n, h, w), part.dtype),
        scratch_shapes=[pltpu.SemaphoreType.DMA((3,)), pltpu.SemaphoreType.DMA((3,)), pltpu.SemaphoreType.DMA],
        name=name)(part)


def _pair_share(half, *, name):
    h, w = half.shape

    def body(h_ref, out_ref, send_sem, recv_sem, local_sem):
        x, y, c, _ = _place()
        dst = out_ref.at[pl.ds(c * h, h), :]
        mine = pltpu.make_async_copy(h_ref, dst, local_sem)
        mine.start()
        cp = _remote(h_ref, dst, send_sem, recv_sem, (x, y, 1 - c))
        cp.start()
        cp.wait_send()
        _remote(h_ref, out_ref.at[pl.ds((1 - c) * h, h), :], send_sem, recv_sem, (x, y, 1 - c)).wait_recv()
        mine.wait()

    return pl.pallas_call(
        body, in_specs=[ANY], out_specs=ANY, out_shape=jax.ShapeDtypeStruct((2 * h, w), half.dtype),
        scratch_shapes=[pltpu.SemaphoreType.DMA, pltpu.SemaphoreType.DMA, pltpu.SemaphoreType.DMA], name=name)(half)


def _add_own_half(g, recv, c, *, name):
    n, r, w = g.shape
    hr = r // 2
    tr = _pick_rows(hr, 1024)
    nr = hr // tr

    def body(c_ref, g_ref, r_ref, o_ref):
        o_ref[...] = (g_ref[...] + r_ref[...]).astype(BF16)

    return pl.pallas_call(
        body,
        grid_spec=pltpu.PrefetchScalarGridSpec(
            num_scalar_prefetch=1, grid=(n, nr),
            in_specs=[pl.BlockSpec((None, tr, w), lambda s, i, c_ref: (s, c_ref[0] * nr + i, 0)),
                      pl.BlockSpec((None, tr, w), lambda s, i, c_ref: (s, i, 0))],
            out_specs=pl.BlockSpec((None, tr, w), lambda s, i, c_ref: (s, i, 0))),
        out_shape=jax.ShapeDtypeStruct((n, hr, w), BF16),
        compiler_params=_cp("parallel", "parallel"), name=name)(c, g, recv)


def _sum_chips(q, *, name):
    n, h, w = q.shape
    tr = _pick_rows(h, 1024)

    def body(a, b, c, d, o_ref):
        o_ref[...] = ((a[...].astype(F32) + b[...].astype(F32)) + c[...].astype(F32)) + d[...].astype(F32)

    specs = [pl.BlockSpec((None, tr, w), functools.partial(lambda i, s: (s, i, 0), s=s)) for s in range(n)]
    return pl.pallas_call(
        body, grid=(h // tr,), in_specs=specs, out_specs=pl.BlockSpec((tr, w), lambda i: (i, 0)),
        out_shape=jax.ShapeDtypeStruct((h, w), F32), compiler_params=_cp("parallel"), name=name)(q, q, q, q)


PACK_W = 1024
BIG = (("att_w_qkv", 2), ("att_w_o", 1), ("hgrn_w_in", 2), ("hgrn_w_o", 1),
       ("ffn_w_in", 2), ("ffn_w_out", 1), ("ple_w_gate", 1), ("ple_w_proj", 2))
LB_ROWS = 32


def _pack_shards(shards):
    return jnp.concatenate([shards[n].reshape(-1, PACK_W) for n, _ in BIG], axis=0)


def _unpack_shards(buf, shapes):
    out, r0 = {}, 0
    for n, _ in BIG:
        nr = shapes[n][0] * shapes[n][1] * shapes[n][2] // PACK_W
        out[n] = buf[r0:r0 + nr].reshape(shapes[n])
        r0 += nr
    return out


def _gathered_to_full(buf, shapes):
    out, r0 = {}, 0
    for n, axis in BIG:
        l, r, c = shapes[n]
        nr = l * r * c // PACK_W
        sh = buf[:, r0:r0 + nr].reshape(N_CHIPS, l, r, c)
        out[n] = (sh.transpose(1, 0, 2, 3).reshape(l, N_CHIPS * r, c) if axis == 1
                  else sh.transpose(1, 2, 0, 3).reshape(l, r, N_CHIPS * c))
        r0 += nr
    return out, r0


def _full_to_slabs(full, shapes):
    parts = []
    for n, axis in BIG:
        l, r, c = shapes[n]
        g = full[n]
        g = (g.reshape(l, N_CHIPS, r, c).transpose(1, 0, 2, 3) if axis == 1
             else g.reshape(l, r, N_CHIPS, c).transpose(2, 0, 1, 3))
        parts.append(g.reshape(N_CHIPS, -1, PACK_W))
    return jnp.concatenate(parts, axis=1)


SMALL = ("ln_mix_g", "ln_mix_b", "ln_ffn_g", "ln_ffn_b")
SMALL_ROWS = 32


def _pack_small(vals, lb):
    rows = [vals[n] for n in SMALL] + [lb.reshape(-1, PACK_W)]
    for n in ("att_sink", "hgrn_norm_g"):
        flat = vals[n].reshape(1, -1)
        rows.append(jnp.pad(flat, ((0, 0), (0, PACK_W - flat.shape[1]))))
    buf = jnp.concatenate(rows, axis=0)
    return jnp.pad(buf, ((0, SMALL_ROWS - buf.shape[0]), (0, 0)))


def _unpack_small(buf, lb_shape):
    out, r0 = {}, 0
    for n in SMALL:
        out[n] = buf[r0:r0 + DEPTH]
        r0 += DEPTH
    nlb = lb_shape[0] * lb_shape[1] * lb_shape[2] // PACK_W
    out["hgrn_lb_logits"] = buf[r0:r0 + nlb].reshape(lb_shape)
    r0 += nlb
    out["att_sink"] = buf[r0, :2 * N_Q_HEADS].reshape(2, N_Q_HEADS)
    out["hgrn_norm_g"] = buf[r0 + 1, :2 * LANES].reshape(2, LANES)
    return out


WEIGHTS = ("att_w_qkv", "att_sink", "att_w_o", "hgrn_w_in", "hgrn_lb_logits", "hgrn_norm_g", "hgrn_w_o", "ln_mix_g",
           "ln_mix_b", "ffn_w_in", "ffn_w_out", "ln_ffn_g", "ln_ffn_b", "ple_w_gate", "ple_w_proj")


def kernel(x, p, att_w_qkv, att_sink, att_w_o, hgrn_w_in, hgrn_lb_logits, hgrn_norm_g, hgrn_w_o, ln_mix_g, ln_mix_b, ffn_w_in, ffn_w_out, ln_ffn_g, ln_ffn_b, ple_w_gate, ple_w_proj, loss_target, m_att_w_qkv, m_att_sink, m_att_w_o, m_hgrn_w_in, m_hgrn_lb_logits, m_hgrn_norm_g, m_hgrn_w_o, m_ln_mix_g, m_ln_mix_b, m_ffn_w_in, m_ffn_w_out, m_ln_ffn_g, m_ln_ffn_b, m_ple_w_gate, m_ple_w_proj, v_att_w_qkv, v_att_sink, v_att_w_o, v_hgrn_w_in, v_hgrn_lb_logits, v_hgrn_norm_g, v_hgrn_w_o, v_ln_mix_g, v_ln_mix_b, v_ffn_w_in, v_ffn_w_out, v_ln_ffn_g, v_ln_ffn_b, v_ple_w_gate, v_ple_w_proj):
    wts = dict(att_w_qkv=att_w_qkv, att_sink=att_sink, att_w_o=att_w_o, hgrn_w_in=hgrn_w_in, hgrn_lb_logits=hgrn_lb_logits,
               hgrn_norm_g=hgrn_norm_g, hgrn_w_o=hgrn_w_o, ln_mix_g=ln_mix_g, ln_mix_b=ln_mix_b, ffn_w_in=ffn_w_in,
               ffn_w_out=ffn_w_out, ln_ffn_g=ln_ffn_g, ln_ffn_b=ln_ffn_b, ple_w_gate=ple_w_gate, ple_w_proj=ple_w_proj)
    mom = dict(att_w_qkv=m_att_w_qkv, att_sink=m_att_sink, att_w_o=m_att_w_o, hgrn_w_in=m_hgrn_w_in, hgrn_lb_logits=m_hgrn_lb_logits,
               hgrn_norm_g=m_hgrn_norm_g, hgrn_w_o=m_hgrn_w_o, ln_mix_g=m_ln_mix_g, ln_mix_b=m_ln_mix_b, ffn_w_in=m_ffn_w_in,
               ffn_w_out=m_ffn_w_out, ln_ffn_g=m_ln_ffn_g, ln_ffn_b=m_ln_ffn_b, ple_w_gate=m_ple_w_gate, ple_w_proj=m_ple_w_proj)
    var = dict(att_w_qkv=v_att_w_qkv, att_sink=v_att_sink, att_w_o=v_att_w_o, hgrn_w_in=v_hgrn_w_in, hgrn_lb_logits=v_hgrn_lb_logits,
               hgrn_norm_g=v_hgrn_norm_g, hgrn_w_o=v_hgrn_w_o, ln_mix_g=v_ln_mix_g, ln_mix_b=v_ln_mix_b, ffn_w_in=v_ffn_w_in,
               ffn_w_out=v_ffn_w_out, ln_ffn_g=v_ln_ffn_g, ln_ffn_b=v_ln_ffn_b, ple_w_gate=v_ple_w_gate, ple_w_proj=v_ple_w_proj)
    shapes = {n: wts[n].shape for n, _ in BIG}
    chip = 2 * lax.axis_index("x") + lax.axis_index("y")
    core = lax.axis_index("c")

    lb_bits = lax.bitcast_convert_type(hgrn_lb_logits.reshape(-1), BF16).reshape(-1, PACK_W)
    packed = jnp.concatenate([_pack_shards({n: wts[n].astype(BF16) for n, _ in BIG}), lb_bits,
                              jnp.zeros((LB_ROWS - lb_bits.shape[0], PACK_W), BF16)], axis=0)
    gathered = _allgather_weights(packed, name="allgather_weights")
    w_full, r_big = _gathered_to_full(gathered, shapes)
    lb_sh = hgrn_lb_logits.shape
    lb_rows = gathered[:, r_big:r_big + lb_bits.shape[0]].reshape(N_CHIPS, -1, 2)
    lb_full = lax.bitcast_convert_type(lb_rows, F32).reshape(N_CHIPS, lb_sh[0], lb_sh[1], lb_sh[2])
    lb_full = lb_full.transpose(1, 2, 0, 3).reshape(lb_sh[0], lb_sh[1], N_CHIPS * lb_sh[2])
    sp = dict(att_sink=att_sink, hgrn_lb_logits=lb_full, hgrn_norm_g=hgrn_norm_g, ln_mix_g=ln_mix_g, ln_mix_b=ln_mix_b,
              ln_ffn_g=ln_ffn_g, ln_ffn_b=ln_ffn_b)

    loss, grad_x, gw, gs = _local_step(x[0], p, loss_target[0], w_full, sp)
    loss = lax.psum(loss[0, 0], ("x", "y", "c"))

    slabs = _full_to_slabs(gw, shapes)
    from_sibling = _pair_exchange(slabs, name="grad_pair_exchange")
    chip_part = _add_own_half(slabs, from_sibling, core.reshape(1).astype(jnp.int32), name="grad_pair_add")
    from_chips = _chip_scatter(chip_part, name="grad_chip_scatter")
    half = _sum_chips(from_chips, name="grad_chip_sum")
    g_big = _unpack_shards(_pair_share(half, name="grad_pair_share"), shapes)

    g_small_full = _unpack_small(_allreduce_small(_pack_small(gs, gs["hgrn_lb_logits"]), name="allreduce_small"),
                                 (lb_sh[0], lb_sh[1], N_CHIPS * lb_sh[2]))
    g_lb = lax.dynamic_slice_in_dim(g_small_full["hgrn_lb_logits"], chip * lb_sh[2], lb_sh[2], axis=2)
    grads = dict(g_big)
    grads.update({n: g_small_full[n] for n in SMALL + ("att_sink", "hgrn_norm_g")})
    grads["hgrn_lb_logits"] = g_lb

    delta, new_m, new_v = {}, {}, {}
    for n, _ in BIG:
        two_d = lambda a: a.reshape(-1, a.shape[-1])
        d, nm, nv = _adamw(two_d(wts[n]), two_d(grads[n]), two_d(mom[n]), two_d(var[n]), name=f"adamw_{n}")
        delta[n], new_m[n], new_v[n] = d.reshape(shapes[n]), nm.reshape(shapes[n]), nv.reshape(shapes[n])
    packs = [_pack_small(src, src["hgrn_lb_logits"]) for src in (wts, grads, mom, var)]
    outs = _adamw(*packs, name="adamw_small")
    for dst, buf in zip((delta, new_m, new_v), outs):
        dst.update(_unpack_small(buf, lb_sh))

    return (loss, grad_x[None], *[grads[n] for n in WEIGHTS], *[delta[n] for n in WEIGHTS],
            *[new_m[n] for n in WEIGHTS], *[new_v[n] for n in WEIGHTS])


def _gather_guest(packed):
    r, w = packed.shape
    hr = r // 2

    def copies(src_ref, out_ref, send_sems, recv_sems, local_sem):
        x, y, c, chips = _place()
        sibling = (x, y, 1 - c)

        def half(cx, cy, hc):
            return out_ref.at[2 * cx + cy, pl.ds(hc * hr, hr), :]

        my_half = src_ref.at[pl.ds(c * hr, hr), :]
        mine = pltpu.make_async_copy(src_ref, out_ref.at[2 * x + y], local_sem)
        first = [_remote(my_half, half(x, y, c), send_sems.at[j], recv_sems.at[j], (*chip, c)) for j, chip in enumerate(chips)]
        passed = [_remote(half(*chip, c), half(*chip, c), send_sems.at[3 + j], recv_sems.at[3 + j], sibling)
                  for j, chip in enumerate(chips)]
        from_chips = [_remote(my_half, half(*chip, c), send_sems.at[j], recv_sems.at[j], (*chip, c)) for j, chip in enumerate(chips)]
        from_sibling = [_remote(my_half, half(*chip, 1 - c), send_sems.at[3 + j], recv_sems.at[3 + j], sibling)
                        for j, chip in enumerate(chips)]
        return mine, first, passed, from_chips, from_sibling

    def start(gi, go, gs):
        mine, first, _, _, _ = copies(gi[0], go[0], *gs)
        mine.start()
        for cp in first:
            cp.start()

    def finish(gi, go, gs):
        mine, first, passed, from_chips, from_sibling = copies(gi[0], go[0], *gs)
        for arrival, onward in zip(from_chips, passed):
            arrival.wait_recv()
            onward.start()
        for arrival in from_sibling:
            arrival.wait_recv()
        for cp in first + passed:
            cp.wait_send()
        mine.wait()

    return _Guest((packed,), (jax.ShapeDtypeStruct((N_CHIPS, r, w), packed.dtype),),
                  (pltpu.SemaphoreType.DMA((6,)), pltpu.SemaphoreType.DMA((6,)), pltpu.SemaphoreType.DMA), start, finish)


def _pair_exchange_guest(g):
    n, r, w = g.shape
    hr = r // 2

    def copy(g_ref, out_ref, send_sem, recv_sem):
        x, y, c, _ = _place()
        return _remote(g_ref.at[:, pl.ds((1 - c) * hr, hr), :], out_ref, send_sem, recv_sem, (x, y, 1 - c))

    return _Guest((g,), (jax.ShapeDtypeStruct((n, hr, w), g.dtype),), (pltpu.SemaphoreType.DMA, pltpu.SemaphoreType.DMA),
                  lambda gi, go, gs: copy(gi[0], go[0], *gs).start(),
                  lambda gi, go, gs: copy(gi[0], go[0], *gs).wait())


def _chip_scatter_guest(part):
    n, h, w = part.shape

    def copies(p_ref, out_ref, send_sems, recv_sems, local_sem):
        x, y, c, chips = _place()
        me = 2 * x + y
        mine = pltpu.make_async_copy(p_ref.at[me], out_ref.at[me], local_sem)
        sends = [_remote(p_ref.at[2 * cx + cy], out_ref.at[me], send_sems.at[j], recv_sems.at[j], (cx, cy, c))
                 for j, (cx, cy) in enumerate(chips)]
        arrivals = [_remote(p_ref.at[me], out_ref.at[2 * cx + cy], send_sems.at[j], recv_sems.at[j], (cx, cy, c))
                    for j, (cx, cy) in enumerate(chips)]
        return mine, sends, arrivals

    def start(gi, go, gs):
        mine, sends, _ = copies(gi[0], go[0], *gs)
        mine.start()
        for cp in sends:
            cp.start()

    def finish(gi, go, gs):
        mine, sends, arrivals = copies(gi[0], go[0], *gs)
        for cp in arrivals:
            cp.wait_recv()
        for cp in sends:
            cp.wait_send()
        mine.wait()

    return _Guest((part,), (jax.ShapeDtypeStruct((n, h, w), part.dtype),),
                  (pltpu.SemaphoreType.DMA((3,)), pltpu.SemaphoreType.DMA((3,)), pltpu.SemaphoreType.DMA), start, finish)


def _pair_share_guest(halves):
    n = len(halves)

    def copies(k, h_ref, out_ref, send_sems, recv_sems, local_sems):
        h = h_ref.shape[0]
        x, y, c, _ = _place()
        dst = out_ref.at[pl.ds(c * h, h), :]
        mine = pltpu.make_async_copy(h_ref, dst, local_sems.at[k])
        send = _remote(h_ref, dst, send_sems.at[k], recv_sems.at[k], (x, y, 1 - c))
        arrival = _remote(h_ref, out_ref.at[pl.ds((1 - c) * h, h), :], send_sems.at[k], recv_sems.at[k], (x, y, 1 - c))
        return mine, send, arrival

    def start(gi, go, gs):
        for k in range(n):
            mine, send, _ = copies(k, gi[k], go[k], *gs)
            mine.start()
            send.start()

    def finish(gi, go, gs):
        for k in range(n):
            mine, send, arrival = copies(k, gi[k], go[k], *gs)
            send.wait_send()
            arrival.wait_recv()
            mine.wait()

    return _Guest(tuple(halves), tuple(jax.ShapeDtypeStruct((2 * a.shape[0], a.shape[1]), a.dtype) for a in halves),
                  (pltpu.SemaphoreType.DMA((n,)), pltpu.SemaphoreType.DMA((n,)), pltpu.SemaphoreType.DMA((n,))), start, finish)


AXIS = dict(BIG)


def _layer_parts(i):
    j = i // 2
    mixer = (("att_w_qkv", j), ("att_w_o", j)) if i % 2 == 0 else (("hgrn_w_in", j), ("hgrn_w_o", j))
    return mixer + (("ffn_w_in", i), ("ffn_w_out", i), ("ple_w_gate", i), ("ple_w_proj", i))


def _gather_groups():
    first = _layer_parts(0)
    return [first[:1], first[1:] + _layer_parts(1), _layer_parts(2), _layer_parts(3)]


def _pack_parts(shards, parts):
    return jnp.concatenate([shards[n][l].reshape(-1, PACK_W) for n, l in parts], axis=0)


def _gathered_parts(buf, parts, shapes):
    out, r0 = {}, 0
    for n, l in parts:
        r, c = shapes[n][1:]
        nr = r * c // PACK_W
        sh = buf[:, r0:r0 + nr].reshape(N_CHIPS, r, c)
        out[(n, l)] = sh.reshape(N_CHIPS * r, c) if AXIS[n] == 1 else sh.transpose(1, 0, 2).reshape(r, N_CHIPS * c)
        r0 += nr
    return out, r0


def _pack_layer(shards, i):
    return _pack_parts(shards, _layer_parts(i))


def _unpack_layer(buf, i, shapes):
    out, r0 = {}, 0
    for n, _ in _layer_parts(i):
        r, c = shapes[n][1:]
        nr = r * c // PACK_W
        out[n] = buf[r0:r0 + nr].reshape(r, c)
        r0 += nr
    return out


def _gathered_layer(buf, i, shapes):
    out, r0 = {}, 0
    for n, _ in _layer_parts(i):
        r, c = shapes[n][1:]
        nr = r * c // PACK_W
        sh = buf[:, r0:r0 + nr].reshape(N_CHIPS, r, c)
        out[n] = sh.reshape(N_CHIPS * r, c) if AXIS[n] == 1 else sh.transpose(1, 0, 2).reshape(r, N_CHIPS * c)
        r0 += nr
    return out, r0


def _layer_slabs(full, i, shapes):
    parts = []
    for n, _ in _layer_parts(i):
        r, c = shapes[n][1:]
        g = full[n]
        g = g.reshape(N_CHIPS, -1, PACK_W) if AXIS[n] == 1 else g.reshape(r, N_CHIPS, c).transpose(1, 0, 2).reshape(N_CHIPS, -1, PACK_W)
        parts.append(g)
    return jnp.concatenate(parts, axis=1)


def kernel(x, p, att_w_qkv, att_sink, att_w_o, hgrn_w_in, hgrn_lb_logits, hgrn_norm_g, hgrn_w_o, ln_mix_g, ln_mix_b, ffn_w_in, ffn_w_out, ln_ffn_g, ln_ffn_b, ple_w_gate, ple_w_proj, loss_target, m_att_w_qkv, m_att_sink, m_att_w_o, m_hgrn_w_in, m_hgrn_lb_logits, m_hgrn_norm_g, m_hgrn_w_o, m_ln_mix_g, m_ln_mix_b, m_ffn_w_in, m_ffn_w_out, m_ln_ffn_g, m_ln_ffn_b, m_ple_w_gate, m_ple_w_proj, v_att_w_qkv, v_att_sink, v_att_w_o, v_hgrn_w_in, v_hgrn_lb_logits, v_hgrn_norm_g, v_hgrn_w_o, v_ln_mix_g, v_ln_mix_b, v_ffn_w_in, v_ffn_w_out, v_ln_ffn_g, v_ln_ffn_b, v_ple_w_gate, v_ple_w_proj):
    wts = dict(att_w_qkv=att_w_qkv, att_sink=att_sink, att_w_o=att_w_o, hgrn_w_in=hgrn_w_in, hgrn_lb_logits=hgrn_lb_logits,
               hgrn_norm_g=hgrn_norm_g, hgrn_w_o=hgrn_w_o, ln_mix_g=ln_mix_g, ln_mix_b=ln_mix_b, ffn_w_in=ffn_w_in,
               ffn_w_out=ffn_w_out, ln_ffn_g=ln_ffn_g, ln_ffn_b=ln_ffn_b, ple_w_gate=ple_w_gate, ple_w_proj=ple_w_proj)
    mom = dict(att_w_qkv=m_att_w_qkv, att_sink=m_att_sink, att_w_o=m_att_w_o, hgrn_w_in=m_hgrn_w_in, hgrn_lb_logits=m_hgrn_lb_logits,
               hgrn_norm_g=m_hgrn_norm_g, hgrn_w_o=m_hgrn_w_o, ln_mix_g=m_ln_mix_g, ln_mix_b=m_ln_mix_b, ffn_w_in=m_ffn_w_in,
               ffn_w_out=m_ffn_w_out, ln_ffn_g=m_ln_ffn_g, ln_ffn_b=m_ln_ffn_b, ple_w_gate=m_ple_w_gate, ple_w_proj=m_ple_w_proj)
    var = dict(att_w_qkv=v_att_w_qkv, att_sink=v_att_sink, att_w_o=v_att_w_o, hgrn_w_in=v_hgrn_w_in, hgrn_lb_logits=v_hgrn_lb_logits,
               hgrn_norm_g=v_hgrn_norm_g, hgrn_w_o=v_hgrn_w_o, ln_mix_g=v_ln_mix_g, ln_mix_b=v_ln_mix_b, ffn_w_in=v_ffn_w_in,
               ffn_w_out=v_ffn_w_out, ln_ffn_g=v_ln_ffn_g, ln_ffn_b=v_ln_ffn_b, ple_w_gate=v_ple_w_gate, ple_w_proj=v_ple_w_proj)
    shapes = {n: wts[n].shape for n, _ in BIG}
    chip = 2 * lax.axis_index("x") + lax.axis_index("y")
    core = lax.axis_index("c").reshape(1).astype(jnp.int32)
    xin, target = x[0], loss_target[0]
    t = xin.shape[0]
    row = lambda a, i: a[i][None]

    bf_shards = {n: wts[n].astype(BF16) for n, _ in BIG}
    lb_sh = hgrn_lb_logits.shape
    lb_bits = lax.bitcast_convert_type(hgrn_lb_logits.reshape(-1), BF16).reshape(-1, PACK_W)
    groups = _gather_groups()
    packed = [_pack_parts(bf_shards, parts) for parts in groups]
    packed[0] = jnp.concatenate([packed[0], lb_bits, jnp.zeros((LB_ROWS - lb_bits.shape[0], PACK_W), BF16)], axis=0)

    gathered = _run_guest(_gather_guest(packed[0]), name="gather_first")[0]
    wfull, r_big = _gathered_parts(gathered, groups[0], shapes)
    lb_rows = gathered[:, r_big:r_big + lb_bits.shape[0]].reshape(N_CHIPS, -1, 2)
    lb_full = lax.bitcast_convert_type(lb_rows, F32).reshape(N_CHIPS, lb_sh[0], lb_sh[1], lb_sh[2])
    lb_full = lb_full.transpose(1, 2, 0, 3).reshape(lb_sh[0], lb_sh[1], N_CHIPS * lb_sh[2])
    logits2d = lb_full.reshape(DEPTH, 2 * D_MODEL)
    lb_all = _lower_bounds(logits2d, name="lb_fwd").reshape(DEPTH, 2, 1, D_MODEL)
    tables = _rope_tables(t)

    saved, weights = [], []
    xf, xb = xin, xin.astype(BF16)
    for i in range(DEPTH):
        j = i // 2
        nxt = _gather_guest(packed[i + 1]) if i + 1 < DEPTH else None
        s = dict(xin_bf=xb)
        if i % 2 == 0:
            q, k, v = _qkv_rope(xb, wfull[("att_w_qkv", j)], tables, name=f"qkv_rope_{i}")
            (o, lse), got = _attn_fwd(q, k, v, att_sink[j], name=f"attn_fwd_{i}", guest=nxt)
            s.update(q=q, k=k, v=v, o=o, lse=lse)
            mix_in = o
        else:
            z = _mm_nn(xb, wfull[("hgrn_w_in", j)], out_dtype=F32, name=f"hgrn_in_{i}")
            (o_f, s_f), _ = _hgrn_fwd(z, lb_all[i, 0], False, name=f"hgrn_scan_f_{i}")
            (o_b, s_b), _ = _hgrn_fwd(z, lb_all[i, 1], True, name=f"hgrn_scan_b_{i}")
            og = _hgrn_post_fwd(o_f, o_b, z, row(hgrn_norm_g, j), name=f"hgrn_post_{i}")
            s.update(z=z, o_f=o_f, o_b=o_b, s_f=s_f, s_b=s_b, og=og)
            mix_in = og
        in_mixer = nxt is not None and i % 2 == 0
        if in_mixer:
            wfull.update(_gathered_parts(got[0], groups[i + 1], shapes)[0])
        w = {n: wfull[(n, l)] for n, l in _layer_parts(i)}
        w_o = w["att_w_o" if i % 2 == 0 else "hgrn_w_o"]
        x1, x1b, xh1, rs1 = _mm_res_ln(mix_in, w_o, xf, row(ln_mix_g, i), row(ln_mix_b, i), name=f"mix_out_ln_{i}")
        (g, u, h), got = _ffn_in(x1b, w["ffn_w_in"], name=f"ffn_in_{i}", guest=None if in_mixer else nxt)
        if nxt is not None and not in_mixer:
            wfull.update(_gathered_parts(got[0], groups[i + 1], shapes)[0])
        x2, x2b, xh2, rs2 = _mm_res_ln(h, w["ffn_w_out"], x1, row(ln_ffn_g, i), row(ln_ffn_b, i), name=f"ffn_out_ln_{i}")
        xf, xb, gate, pp = _ple_fwd(x2, x2b, p, i, w["ple_w_gate"], w["ple_w_proj"], name=f"ple_fwd_{i}")
        s.update(x1b=x1b, xh1=xh1, rs1=rs1, g=g, u=u, h=h, x2b=x2b, xh2=xh2, rs2=rs2, gate=gate, pp=pp)
        saved.append(s)
        weights.append(w)

    loss, dx = _loss_head(xf, target, name="loss_head")
    loss = lax.psum(loss[0, 0], ("x", "y", "c"))

    gs = {n: [None] * DEPTH for n in SMALL}
    gs.update(att_sink=[None] * 2, hgrn_norm_g=[None] * 2)
    dlb = [jnp.zeros((2, D_MODEL), F32)] * DEPTH
    halves = [None] * DEPTH
    slabs = None
    for i in reversed(range(DEPTH)):
        j = i // 2
        s, w = saved[i], weights[i]
        gw = {}
        res, got = _ple_bwd(dx, s["gate"], s["pp"], w["ple_w_gate"], s["xh2"], s["rs2"], row(ln_ffn_g, i), name=f"ple_bwd_{i}",
                            guest=None if slabs is None else _pair_exchange_guest(slabs))
        du2, du2b, dpre, dpp, gs["ln_ffn_g"][i], gs["ln_ffn_b"][i] = res
        part = None if slabs is None else _add_own_half(slabs, got[0], core, name=f"grad_pair_add_{i + 1}")
        gw["ple_w_gate"] = _mm_tn(s["x2b"], dpre, name=f"dw_ple_gate_{i}")
        gw["ple_w_proj"] = _mm_tn_p(p, i, dpp, name=f"dw_ple_proj_{i}")
        dgg, dgu = _ffn_out_bwd(du2b, w["ffn_w_out"], s["g"], s["u"], name=f"ffn_out_bwd_{i}")
        gw["ffn_w_out"] = _mm_tn(s["h"], du2b, name=f"dw_ffn_out_{i}")
        res = _mm_nt([dgg, dgu], w["ffn_w_in"], resid=du2, ln=(s["xh1"], s["rs1"], row(ln_mix_g, i)),
                     name=f"ffn_in_bwd_{i}", guest=None if part is None else _chip_scatter_guest(part))
        (du1, du1b, gs["ln_mix_g"][i], gs["ln_mix_b"][i]), got = res if part is not None else (res, None)
        if part is not None:
            halves[i + 1] = _sum_chips(got[0], name=f"grad_chip_sum_{i + 1}")
        gw["ffn_w_in"] = jnp.concatenate(
            [_mm_tn(s["x1b"], dgg, name=f"dw_ffn_gate_{i}"), _mm_tn(s["x1b"], dgu, name=f"dw_ffn_up_{i}")], axis=1)
        if i % 2 == 0:
            gw["att_w_o"] = _mm_tn(s["o"], du1b, name=f"dw_att_o_{i}")
            do = _mm_nt([du1b], w["att_w_o"], out_dtype=BF16, name=f"att_o_bwd_{i}")
            dq, dkw, dvw, gs["att_sink"][j] = _attn_bwd(s["q"], s["k"], s["v"], s["o"], do, s["lse"], att_sink[j], name=f"attn_bwd_{i}")
            dqkv = _attn_post_bwd(dq, dkw, dvw, tables, name=f"attn_post_bwd_{i}")
            gw["att_w_qkv"] = _mm_tn(s["xin_bf"], dqkv, name=f"dw_att_qkv_{i}")
            dx = _mm_nt([dqkv], w["att_w_qkv"], resid=du1, name=f"qkv_bwd_{i}")
        else:
            gw["hgrn_w_o"] = _mm_tn(s["og"], du1b, name=f"dw_hgrn_o_{i}")
            dy = _mm_nt([du1b], w["hgrn_w_o"], out_dtype=BF16, name=f"hgrn_o_bwd_{i}")
            d_o, dgate, gs["hgrn_norm_g"][j] = _hgrn_post_bwd(dy, s["o_f"], s["o_b"], s["z"], row(hgrn_norm_g, j), name=f"hgrn_post_bwd_{i}")
            dq_f, dzf_f, dv_f, dlb_f = _hgrn_bwd(s["z"], lb_all[i, 0], d_o, s["s_f"], False, name=f"hgrn_scan_f_bwd_{i}")
            dq, dzf_b, dv, dlb_b = _hgrn_bwd(s["z"], lb_all[i, 1], d_o, s["s_b"], True, name=f"hgrn_scan_b_bwd_{i}",
                                             other=(dq_f, dv_f))
            dlb[i] = jnp.stack([dlb_f.reshape(D_MODEL), dlb_b.reshape(D_MODEL)])
            dz = [dq, dzf_f, dzf_b, dv, dgate]
            gw["hgrn_w_in"] = jnp.concatenate(
                [_mm_tn(s["xin_bf"], part, name=f"dw_hgrn_in_{i}_{n}") for n, part in enumerate(dz)], axis=1)
            dx = _mm_nt(dz, w["hgrn_w_in"], resid=du1, name=f"hgrn_in_bwd_{i}")
        slabs = _layer_slabs(gw, i, shapes)

    from_sibling = _run_guest(_pair_exchange_guest(slabs), name="grad_pair_exchange_0")[0]
    part = _add_own_half(slabs, from_sibling, core, name="grad_pair_add_0")
    halves[0] = _sum_chips(_run_guest(_chip_scatter_guest(part), name="grad_chip_scatter_0")[0], name="grad_chip_sum_0")
    reduced = _run_guest(_pair_share_guest(halves), name="grad_pair_share")

    g_layers = [_unpack_layer(reduced[i], i, shapes) for i in range(DEPTH)]
    grads = {}
    for n, _ in BIG:
        per_layer = [g_layers[i][n] for i in range(DEPTH) if n in g_layers[i]]
        grads[n] = jnp.stack(per_layer)

    gsmall = {n: jnp.concatenate(v, axis=0) for n, v in gs.items()}
    dlogits = _lower_bounds_bwd(logits2d, jnp.stack(dlb).reshape(DEPTH, 2 * D_MODEL), name="lb_bwd").reshape(DEPTH, 2, D_MODEL)
    g_small_full = _unpack_small(_allreduce_small(_pack_small(gsmall, dlogits), name="allreduce_small"),
                                 (lb_sh[0], lb_sh[1], N_CHIPS * lb_sh[2]))
    grads.update({n: g_small_full[n] for n in SMALL + ("att_sink", "hgrn_norm_g")})
    grads["hgrn_lb_logits"] = lax.dynamic_slice_in_dim(g_small_full["hgrn_lb_logits"], chip * lb_sh[2], lb_sh[2], axis=2)

    delta, new_m, new_v = {}, {}, {}
    for n, _ in BIG:
        two_d = lambda a: a.reshape(-1, a.shape[-1])
        d, nm, nv = _adamw(two_d(wts[n]), two_d(grads[n]), two_d(mom[n]), two_d(var[n]), name=f"adamw_{n}")
        delta[n], new_m[n], new_v[n] = d.reshape(shapes[n]), nm.reshape(shapes[n]), nv.reshape(shapes[n])
    packs = [_pack_small(src, src["hgrn_lb_logits"]) for src in (wts, grads, mom, var)]
    outs = _adamw(*packs, name="adamw_small")
    for dst, buf in zip((delta, new_m, new_v), outs):
        dst.update(_unpack_small(buf, lb_sh))

    return (loss, dx[None], *[grads[n] for n in WEIGHTS], *[delta[n] for n in WEIGHTS],
            *[new_m[n] for n in WEIGHTS], *[new_v[n] for n in WEIGHTS])
```

```python
import functools
from typing import Callable, NamedTuple

import jax
import jax.numpy as jnp
from jax import lax
from jax.experimental import pallas as pl
from jax.experimental.pallas import tpu as pltpu

F32, BF16 = jnp.float32, jnp.bfloat16

D_MODEL = 1024
DEPTH = 4
HEAD_DIM = 64
N_Q_HEADS = 16
N_KV_HEADS = 4
GROUP = 4
Q_DIM = 1024
KV_DIM = 256
WINDOW = 128
ROPE_DIM = 16
ROPE_THETA = 500000.0
HGRN_HEADS = 8
HGRN_KEY = 128
HGRN_CHUNK = 64
D_FF = 2816
PLE_DIM = 256
ALPHA = (2 * DEPTH) ** 0.25
LN_EPS = 1e-5
ADAM_LR, ADAM_B1, ADAM_B2, ADAM_EPS, ADAM_WD, ADAM_STEP = 0.001, 0.9, 0.999, 1e-08, 0.01, 10

LANES = 128
VMEM_LIMIT_BYTES = 56 * 2**20
FACTORED_DECAY_LIMIT = -80.0

NN = ((1,), (0,))
NT = ((1,), (1,))
TN = ((0,), (0,))

N_CHIPS = 4
MESH = pl.DeviceIdType.MESH


def _dot(a, b, dims):
    return lax.dot_general(a, b, (dims, ((), ())), preferred_element_type=F32)


def _cp(*sem):
    return pltpu.CompilerParams(dimension_semantics=sem, vmem_limit_bytes=VMEM_LIMIT_BYTES)


def _rows(t, cap=512):
    return min(cap, t)


def _pick(n, cap):
    best = None
    for d in range(LANES, min(n, cap) + 1, LANES):
        if n % d == 0:
            best = d
    assert best is not None, (n, cap)
    return best


def _sigmoid(x):
    return jax.nn.sigmoid(x)


def _ln_fwd(u, g, b):
    mu = jnp.mean(u, axis=-1, keepdims=True)
    xc = u - mu
    var = jnp.mean(xc * xc, axis=-1, keepdims=True)
    rstd = lax.rsqrt(var + LN_EPS)
    xhat = xc * rstd
    return xhat * g + b, xhat, rstd


def _ln_bwd(dy, xhat, rstd, g):
    dxh = dy * g
    m1 = jnp.mean(dxh, axis=-1, keepdims=True)
    m2 = jnp.mean(dxh * xhat, axis=-1, keepdims=True)
    du = rstd * (dxh - m1 - xhat * m2)
    return du, jnp.sum(dy * xhat, axis=0, keepdims=True), jnp.sum(dy, axis=0, keepdims=True)


def _full(shape):
    nd = len(shape)
    return pl.BlockSpec(shape, lambda *_: (0,) * nd)


ANY = pl.BlockSpec(memory_space=pl.ANY)


class _Guest(NamedTuple):
    args: tuple
    out_shape: tuple
    sems: tuple
    start: Callable
    finish: Callable


def _call(body, *, grid, in_specs, out_specs, out_shape, args, sem, name, scratch_shapes=(), guest=None):
    n_in, n_out, n_scr = len(args), len(out_shape), len(scratch_shapes)
    if guest is None:
        res = pl.pallas_call(body, grid=grid, in_specs=list(in_specs), out_specs=list(out_specs), out_shape=list(out_shape),
                             scratch_shapes=list(scratch_shapes), compiler_params=_cp(*sem), name=name)(*args)
        return list(res), []
    g_in, g_out = len(guest.args), len(guest.out_shape)

    def hosted(*refs):
        cuts = [n_in, g_in, n_out, g_out, n_scr]
        parts, pos = [], 0
        for n in cuts:
            parts.append(refs[pos:pos + n])
            pos += n
        h_in, gi, h_out, go, h_scr = parts
        gs = refs[pos:]
        ids = [pl.program_id(d) for d in range(len(grid))]
        first = functools.reduce(jnp.logical_and, [i == 0 for i in ids])
        last = functools.reduce(jnp.logical_and, [i == n - 1 for i, n in zip(ids, grid)])

        @pl.when(first)
        def _():
            guest.start(gi, go, gs)
        body(*h_in, *h_out, *h_scr)

        @pl.when(last)
        def _():
            guest.finish(gi, go, gs)

    res = pl.pallas_call(
        hosted, grid=grid, in_specs=list(in_specs) + [ANY] * g_in, out_specs=list(out_specs) + [ANY] * g_out,
        out_shape=list(out_shape) + list(guest.out_shape), scratch_shapes=list(scratch_shapes) + list(guest.sems),
        compiler_params=_cp(*(("arbitrary",) * len(grid))), name=name)(*args, *guest.args)
    return list(res[:n_out]), list(res[n_out:])


def _run_guest(guest, *, name):
    g_in = len(guest.args)

    def body(*refs):
        gi, go, gs = refs[:g_in], refs[g_in:g_in + len(guest.out_shape)], refs[g_in + len(guest.out_shape):]
        guest.start(gi, go, gs)
        guest.finish(gi, go, gs)

    return pl.pallas_call(body, in_specs=[ANY] * g_in, out_specs=[ANY] * len(guest.out_shape), out_shape=list(guest.out_shape),
                          scratch_shapes=list(guest.sems), name=name)(*guest.args)


def _mm_nn(a, w, *, out_dtype, name):
    t, k = a.shape
    n = w.shape[1]
    tm, tn = _rows(t), _pick(n, 1408)

    def body(a_ref, w_ref, o_ref):
        o_ref[...] = _dot(a_ref[...], w_ref[...], NN).astype(out_dtype)

    return pl.pallas_call(
        body, grid=(n // tn, t // tm),
        in_specs=[pl.BlockSpec((tm, k), lambda j, i: (i, 0)), pl.BlockSpec((k, tn), lambda j, i: (0, j))],
        out_specs=pl.BlockSpec((tm, tn), lambda j, i: (i, j)),
        out_shape=jax.ShapeDtypeStruct((t, n), out_dtype),
        compiler_params=_cp("parallel", "parallel"), name=name)(a, w)


def _mm_tn(a, b, *, name, guest=None):
    r, m = a.shape
    n = b.shape[1]
    tr, tm, tn = _rows(r), _pick(m, 1408), _pick(n, 2816)

    def body(a_ref, b_ref, o_ref):
        @pl.when(pl.program_id(2) == 0)
        def _():
            o_ref[...] = jnp.zeros_like(o_ref)
        o_ref[...] += _dot(a_ref[...].astype(BF16), b_ref[...].astype(BF16), TN)

    res, extra = _call(
        body, grid=(m // tm, n // tn, r // tr),
        in_specs=[pl.BlockSpec((tr, tm), lambda i, j, k: (k, i)), pl.BlockSpec((tr, tn), lambda i, j, k: (k, j))],
        out_specs=[pl.BlockSpec((tm, tn), lambda i, j, k: (i, j))],
        out_shape=[jax.ShapeDtypeStruct((m, n), F32)], args=(a, b),
        sem=("parallel", "parallel", "arbitrary"), name=name, guest=guest)
    return res[0] if guest is None else (res[0], extra)


def _mm_tn_p(p, layer, b, *, name):
    t = p.shape[2]
    n = b.shape[1]
    tr = _rows(t)

    def body(a_ref, b_ref, o_ref):
        @pl.when(pl.program_id(0) == 0)
        def _():
            o_ref[...] = jnp.zeros_like(o_ref)
        o_ref[...] += _dot(a_ref[...].astype(BF16), b_ref[...], TN)

    return pl.pallas_call(
        body, grid=(t // tr,),
        in_specs=[pl.BlockSpec((None, None, tr, PLE_DIM), lambda k: (layer, 0, k, 0)),
                  pl.BlockSpec((tr, n), lambda k: (k, 0))],
        out_specs=pl.BlockSpec((PLE_DIM, n), lambda k: (0, 0)),
        out_shape=jax.ShapeDtypeStruct((PLE_DIM, n), F32),
        compiler_params=_cp("arbitrary"), name=name)(p, b)


def _mm_res_ln(a, w, resid, g, b, *, name):
    t, k = a.shape
    tm = _rows(t)

    def body(a_ref, w_ref, r_ref, g_ref, b_ref, y_ref, ybf_ref, xh_ref, rs_ref):
        acc = _dot(a_ref[...], w_ref[...], NN)
        y, xh, rs = _ln_fwd(ALPHA * r_ref[...] + acc, g_ref[...], b_ref[...])
        y_ref[...] = y
        ybf_ref[...] = y.astype(BF16)
        xh_ref[...] = xh
        rs_ref[...] = rs

    row = pl.BlockSpec((tm, D_MODEL), lambda i: (i, 0))
    return pl.pallas_call(
        body, grid=(t // tm,),
        in_specs=[pl.BlockSpec((tm, k), lambda i: (i, 0)), _full((k, D_MODEL)), row, _full((1, D_MODEL)), _full((1, D_MODEL))],
        out_specs=[row, row, row, pl.BlockSpec((tm, 1), lambda i: (i, 0))],
        out_shape=[jax.ShapeDtypeStruct((t, D_MODEL), F32), jax.ShapeDtypeStruct((t, D_MODEL), BF16),
                   jax.ShapeDtypeStruct((t, D_MODEL), F32), jax.ShapeDtypeStruct((t, 1), F32)],
        compiler_params=_cp("parallel"), name=name)(a, w, resid, g, b)


def _ffn_in(x_bf, w, *, name, guest=None):
    t = x_bf.shape[0]
    tm, tn = _rows(t), _pick(D_FF, 1408)
    nb = D_FF // tn

    def body(x_ref, wg_ref, wu_ref, dg_ref, du_ref, h_ref):
        x = x_ref[...]
        g = _dot(x, wg_ref[...], NN)
        u = _dot(x, wu_ref[...], NN)
        sig = _sigmoid(g)
        silu = g * sig
        dg_ref[...] = (u * sig * (1.0 + g * (1.0 - sig))).astype(BF16)
        du_ref[...] = silu.astype(BF16)
        h_ref[...] = (silu * u).astype(BF16)

    tile = pl.BlockSpec((tm, tn), lambda j, i: (i, j))
    shp = jax.ShapeDtypeStruct((t, D_FF), BF16)
    return _call(
        body, grid=(nb, t // tm),
        in_specs=[pl.BlockSpec((tm, D_MODEL), lambda j, i: (i, 0)),
                  pl.BlockSpec((D_MODEL, tn), lambda j, i: (0, j)),
                  pl.BlockSpec((D_MODEL, tn), lambda j, i: (0, j + nb))],
        out_specs=[tile, tile, tile], out_shape=[shp, shp, shp], args=(x_bf, w, w),
        sem=("parallel", "parallel"), name=name, guest=guest)


def _ple_fwd(x, x_bf, p, layer, wg, wp, *, name):
    t = x.shape[0]
    tm = _rows(t)

    def body(x_ref, xbf_ref, p_ref, wg_ref, wp_ref, y_ref, ybf_ref, gate_ref, pp_ref):
        gate = _sigmoid(_dot(xbf_ref[...], wg_ref[...], NN))
        pp = _dot(p_ref[...].astype(BF16), wp_ref[...], NN)
        y = x_ref[...] + gate * pp
        y_ref[...] = y
        ybf_ref[...] = y.astype(BF16)
        gate_ref[...] = gate.astype(BF16)
        pp_ref[...] = pp.astype(BF16)

    row = pl.BlockSpec((tm, D_MODEL), lambda i: (i, 0))
    f32, bf = jax.ShapeDtypeStruct((t, D_MODEL), F32), jax.ShapeDtypeStruct((t, D_MODEL), BF16)
    return pl.pallas_call(
        body, grid=(t // tm,),
        in_specs=[row, row, pl.BlockSpec((None, None, tm, PLE_DIM), lambda i: (layer, 0, i, 0)),
                  _full((D_MODEL, D_MODEL)), _full((PLE_DIM, D_MODEL))],
        out_specs=[row, row, row, row], out_shape=[f32, bf, bf, bf],
        compiler_params=_cp("parallel"), name=name)(x, x_bf, p, wg, wp)


def _loss_head(y, target, *, name):
    t = y.shape[0]
    tm = _rows(t)

    def body(y_ref, t_ref, loss_ref, dy_ref):
        e = y_ref[...] - t_ref[...]

        @pl.when(pl.program_id(0) == 0)
        def _():
            loss_ref[...] = jnp.zeros_like(loss_ref)
        sq = jnp.sum(jnp.sum(e * e, axis=1, keepdims=True), axis=0, keepdims=True)
        loss_ref[...] += sq * (0.5 / D_MODEL)
        dy_ref[...] = e * (1.0 / D_MODEL)

    row = pl.BlockSpec((tm, D_MODEL), lambda i: (i, 0))
    return pl.pallas_call(
        body, grid=(t // tm,), in_specs=[row, row],
        out_specs=[_full((1, 1)), row],
        out_shape=[jax.ShapeDtypeStruct((1, 1), F32), jax.ShapeDtypeStruct((t, D_MODEL), F32)],
        compiler_params=_cp("arbitrary"), name=name)(y, target)


def _ple_bwd(dx3, gate, pp, wg, xhat, rstd, g, *, name, guest=None):
    t = dx3.shape[0]
    tm = _rows(t)

    def body(dx_ref, gate_ref, pp_ref, wg_ref, xh_ref, rs_ref, g_ref,
             du_ref, dubf_ref, dpre_ref, dpp_ref, dg_ref, db_ref):
        dx = dx_ref[...]
        gate = gate_ref[...].astype(F32)
        dpre = (dx * pp_ref[...].astype(F32) * gate * (1.0 - gate)).astype(BF16)
        dpre_ref[...] = dpre
        dpp_ref[...] = (dx * gate).astype(BF16)
        dx2 = dx + _dot(dpre, wg_ref[...], NT)
        du, dg, db = _ln_bwd(dx2, xh_ref[...], rs_ref[...], g_ref[...])
        du_ref[...] = du
        dubf_ref[...] = du.astype(BF16)

        @pl.when(pl.program_id(0) == 0)
        def _():
            dg_ref[...] = jnp.zeros_like(dg_ref)
            db_ref[...] = jnp.zeros_like(db_ref)
        dg_ref[...] += dg
        db_ref[...] += db

    row = pl.BlockSpec((tm, D_MODEL), lambda i: (i, 0))
    vec = _full((1, D_MODEL))
    f32, bf = jax.ShapeDtypeStruct((t, D_MODEL), F32), jax.ShapeDtypeStruct((t, D_MODEL), BF16)
    v32 = jax.ShapeDtypeStruct((1, D_MODEL), F32)
    res, extra = _call(
        body, grid=(t // tm,),
        in_specs=[row, row, row, _full((D_MODEL, D_MODEL)), row, pl.BlockSpec((tm, 1), lambda i: (i, 0)), vec],
        out_specs=[row, row, row, row, vec, vec], out_shape=[f32, bf, bf, bf, v32, v32],
        args=(dx3, gate, pp, wg, xhat, rstd, g), sem=("arbitrary",), name=name, guest=guest)
    return res, extra


def _ffn_out_bwd(du_bf, w_out, g, u, *, name):
    t = du_bf.shape[0]
    tm, tn = _rows(t), _pick(D_FF, 1408)

    def body(du_ref, w_ref, hg_ref, hu_ref, dg_ref, dup_ref):
        dh = _dot(du_ref[...], w_ref[...], NT)
        dg_ref[...] = (dh * hg_ref[...].astype(F32)).astype(BF16)
        dup_ref[...] = (dh * hu_ref[...].astype(F32)).astype(BF16)

    tile = pl.BlockSpec((tm, tn), lambda j, i: (i, j))
    shp = jax.ShapeDtypeStruct((t, D_FF), BF16)
    return pl.pallas_call(
        body, grid=(D_FF // tn, t // tm),
        in_specs=[pl.BlockSpec((tm, D_MODEL), lambda j, i: (i, 0)), pl.BlockSpec((tn, D_MODEL), lambda j, i: (j, 0)),
                  tile, tile],
        out_specs=[tile, tile], out_shape=[shp, shp],
        compiler_params=_cp("parallel", "parallel"), name=name)(du_bf, w_out, g, u)


def _mm_nt(parts, w, *, resid=None, ln=None, out_dtype=F32, name, guest=None):
    t, kp = parts[0].shape
    npart = len(parts)
    tm = _rows(t)
    n_in = npart + 1 + (resid is not None) + (3 if ln is not None else 0)

    def body(*refs):
        a_refs, w_ref = refs[:npart], refs[npart]
        pos = npart + 1
        r_ref = None
        if resid is not None:
            r_ref = refs[pos]
            pos += 1
        if ln is not None:
            xh_ref, rs_ref, g_ref = refs[pos:pos + 3]
        outs = refs[n_in:]
        val = _dot(a_refs[0][...], w_ref[:, :kp], NT)
        for pi in range(1, npart):
            val = val + _dot(a_refs[pi][...], w_ref[:, pi * kp:(pi + 1) * kp], NT)
        if r_ref is not None:
            val = val + ALPHA * r_ref[...]
        if ln is None:
            outs[0][...] = val.astype(out_dtype)
        else:
            du, dg, db = _ln_bwd(val, xh_ref[...], rs_ref[...], g_ref[...])
            outs[0][...] = du
            outs[1][...] = du.astype(BF16)

            @pl.when(pl.program_id(0) == 0)
            def _():
                outs[2][...] = jnp.zeros_like(outs[2])
                outs[3][...] = jnp.zeros_like(outs[3])
            outs[2][...] += dg
            outs[3][...] += db

    row = pl.BlockSpec((tm, D_MODEL), lambda i: (i, 0))
    vec = pl.BlockSpec((1, D_MODEL), lambda i: (0, 0))
    in_specs = [pl.BlockSpec((tm, kp), lambda i: (i, 0)) for _ in range(npart)]
    in_specs.append(pl.BlockSpec((D_MODEL, npart * kp), lambda i: (0, 0), pipeline_mode=pl.Buffered(1)))
    args = list(parts) + [w]
    if resid is not None:
        in_specs.append(row)
        args.append(resid)
    if ln is not None:
        in_specs += [row, pl.BlockSpec((tm, 1), lambda i: (i, 0)), vec]
        args += list(ln)
        out_specs = [row, row, vec, vec]
        out_shape = [jax.ShapeDtypeStruct((t, D_MODEL), F32), jax.ShapeDtypeStruct((t, D_MODEL), BF16),
                     jax.ShapeDtypeStruct((1, D_MODEL), F32), jax.ShapeDtypeStruct((1, D_MODEL), F32)]
    else:
        out_specs = [row]
        out_shape = [jax.ShapeDtypeStruct((t, D_MODEL), out_dtype)]
    res, extra = _call(
        body, grid=(t // tm,), in_specs=in_specs, out_specs=out_specs, out_shape=out_shape, args=tuple(args),
        sem=("arbitrary" if ln is not None else "parallel",), name=name, guest=guest)
    res = res if ln is not None else res[0]
    return res if guest is None else (res, extra)


def _rope_tables(t):
    half = ROPE_DIM // 2
    inv = ROPE_THETA ** (-jnp.arange(0, ROPE_DIM, 2, dtype=F32) / ROPE_DIM)
    ang = jnp.arange(t, dtype=F32)[:, None] * inv[None, :]
    cos, sin = jnp.cos(ang), jnp.sin(ang)
    pad = HEAD_DIM - ROPE_DIM
    c = jnp.concatenate([cos, cos, jnp.ones((t, pad), F32)], axis=1)
    s_hi = jnp.concatenate([jnp.zeros((t, half), F32), sin, jnp.zeros((t, pad), F32)], axis=1)
    s_lo = jnp.concatenate([-sin, jnp.zeros((t, half + pad), F32)], axis=1)
    rep = LANES // HEAD_DIM
    return jnp.tile(c, (1, rep)), jnp.tile(s_hi, (1, rep)), jnp.tile(s_lo, (1, rep))


def _rope(x, c, s_hi, s_lo, sign):
    half = ROPE_DIM // 2
    parts = []
    for j in range(x.shape[1] // LANES):
        xg = x[:, j * LANES:(j + 1) * LANES]
        rot = pltpu.roll(xg, half, 1) * s_hi + pltpu.roll(xg, LANES - half, 1) * s_lo
        parts.append(xg * c + sign * rot)
    return jnp.concatenate(parts, axis=1)


def _qkv_rope(x_bf, w, tables, *, name):
    t = x_bf.shape[0]
    tm = _rows(t)
    n = Q_DIM + 2 * KV_DIM

    def body(x_ref, w_ref, c_ref, sh_ref, sl_ref, q_ref, k_ref, v_ref):
        acc = _dot(x_ref[...], w_ref[...], NN)
        c, sh, sl = c_ref[...], sh_ref[...], sl_ref[...]
        q_ref[...] = (_rope(acc[:, :Q_DIM], c, sh, sl, 1.0) * HEAD_DIM ** -0.5).astype(BF16)
        k_ref[...] = _rope(acc[:, Q_DIM:Q_DIM + KV_DIM], c, sh, sl, 1.0).astype(BF16)
        v_ref[...] = acc[:, Q_DIM + KV_DIM:].astype(BF16)

    tab = pl.BlockSpec((tm, LANES), lambda i: (i, 0))
    return pl.pallas_call(
        body, grid=(t // tm,),
        in_specs=[pl.BlockSpec((tm, D_MODEL), lambda i: (i, 0)), _full((D_MODEL, n)), tab, tab, tab],
        out_specs=[pl.BlockSpec((tm, Q_DIM), lambda i: (i, 0)), pl.BlockSpec((tm, KV_DIM), lambda i: (i, 0)),
                   pl.BlockSpec((tm, KV_DIM), lambda i: (i, 0))],
        out_shape=[jax.ShapeDtypeStruct((t, Q_DIM), BF16), jax.ShapeDtypeStruct((t, KV_DIM), BF16),
                   jax.ShapeDtypeStruct((t, KV_DIM), BF16)],
        compiler_params=_cp("parallel"), name=name)(x_bf, w, *tables)


ATT_ROWS = 256
ATT_STRIP = 64


def _window_specs(t, tq):
    r = tq // WINDOW
    last = t // WINDOW - 1
    return [pl.BlockSpec((WINDOW, KV_DIM), lambda i: (jnp.maximum(i * r - 1, 0), 0)),
            pl.BlockSpec((tq, KV_DIM), lambda i: (i, 0)),
            pl.BlockSpec((WINDOW, KV_DIM), lambda i: (jnp.minimum((i + 1) * r, last), 0))]


def _att_valid(base, t):
    rows = GROUP * WINDOW
    qpos = base + (lax.broadcasted_iota(jnp.int32, (rows, 3 * WINDOW), 0) & (WINDOW - 1))
    kpos = base - WINDOW + lax.broadcasted_iota(jnp.int32, (rows, 3 * WINDOW), 1)
    return (jnp.abs(qpos - kpos) <= WINDOW) & (kpos >= 0) & (kpos < t)


def _stack_heads(x, r0, g):
    return jnp.concatenate(
        [x[r0:r0 + WINDOW, (GROUP * g + h) * HEAD_DIM:(GROUP * g + h + 1) * HEAD_DIM] for h in range(GROUP)], axis=0)


def _sink_column(sink_ref, g):
    return jnp.concatenate([jnp.full((WINDOW, 1), sink_ref[GROUP * g + h], F32) for h in range(GROUP)], axis=0)


def _unstack_heads(pieces):
    return jnp.concatenate([pieces[g][h * WINDOW:(h + 1) * WINDOW] for g in range(N_KV_HEADS) for h in range(GROUP)], axis=1)


def _attn_fwd(q, k, v, sink, *, name, guest=None):
    t = q.shape[0]
    tq = min(ATT_ROWS, t)
    nsb = tq // WINDOW

    def body(sink_ref, q_ref, kp_ref, kc_ref, kn_ref, vp_ref, vc_ref, vn_ref, o_ref, l_ref):
        i = pl.program_id(0)
        qv = q_ref[...]
        kw = jnp.concatenate([kp_ref[...], kc_ref[...], kn_ref[...]], axis=0)
        vw = jnp.concatenate([vp_ref[...], vc_ref[...], vn_ref[...]], axis=0)
        ones = jnp.ones((3 * WINDOW, HEAD_DIM), BF16)
        for sb in range(nsb):
            r0 = sb * WINDOW
            bias =jnp.where(_att_valid(i * tq + r0, t)[:WINDOW], 0.0, -1e30)
            pieces = []
            for hh in range(N_Q_HEADS):
                g, h = hh // GROUP, hh % GROUP
                qh = qv[r0:r0 + WINDOW, hh * HEAD_DIM:(hh + 1) * HEAD_DIM]
                kg = kw[r0:r0 + 3 * WINDOW, g * HEAD_DIM:(g + 1) * HEAD_DIM]
                vg = jnp.concatenate([vw[r0:r0 + 3 * WINDOW, g * HEAD_DIM:(g + 1) * HEAD_DIM], ones], axis=1)
                s = _dot(qh, kg, NT) + bias
                snk = sink_ref[hh]
                m = jnp.maximum(jnp.max(s, axis=1, keepdims=True), snk)
                ov = _dot(jnp.exp(s - m).astype(BF16), vg, NN)
                den = ov[:, HEAD_DIM:HEAD_DIM + 1] + jnp.exp(snk - m)
                pieces.append(ov[:, :HEAD_DIM] * (1.0 / den))
                l_ref[g, sb, h * WINDOW:(h + 1) * WINDOW, :] = m + jnp.log(den)
            o_ref[r0:r0 + WINDOW, :] = jnp.concatenate(pieces, axis=1).astype(BF16)

    return _call(
        body, grid=(t // tq,),
        in_specs=[pl.BlockSpec(memory_space=pltpu.SMEM), pl.BlockSpec((tq, Q_DIM), lambda i: (i, 0))]
        + _window_specs(t, tq) + _window_specs(t, tq),
        out_specs=[pl.BlockSpec((tq, Q_DIM), lambda i: (i, 0)),
                   pl.BlockSpec((N_KV_HEADS, nsb, GROUP * WINDOW, 1), lambda i: (0, i, 0, 0))],
        out_shape=[jax.ShapeDtypeStruct((t, Q_DIM), BF16),
                   jax.ShapeDtypeStruct((N_KV_HEADS, t // WINDOW, GROUP * WINDOW, 1), F32)],
        args=(sink, q, k, k, k, v, v, v), sem=("parallel",), name=name, guest=guest)


def _attn_bwd(q, k, v, o, do, lse, sink, *, name):
    t = q.shape[0]
    tq = min(ATT_ROWS, t)
    nsb = tq // WINDOW

    def body(sink_ref, q_ref, o_ref, do_ref, l_ref, kp_ref, kc_ref, kn_ref, vp_ref, vc_ref, vn_ref,
             dq_ref, dkw_ref, dvw_ref, dsink_ref):
        i = pl.program_id(0)
        qv, ov, dov = q_ref[...], o_ref[...], do_ref[...]
        kw = jnp.concatenate([kp_ref[...], kc_ref[...], kn_ref[...]], axis=0)
        vw = jnp.concatenate([vp_ref[...], vc_ref[...], vn_ref[...]], axis=0)
        lane16 = lax.broadcasted_iota(jnp.int32, (1, N_Q_HEADS), 1)
        dsink = jnp.zeros((1, N_Q_HEADS), F32)
        for sb in range(nsb):
            r0 = sb * WINDOW
            valid = _att_valid(i * tq + r0, t)
            dq_p, dk_p, dv_p = [], [], []
            for g in range(N_KV_HEADS):
                qs, dos = _stack_heads(qv, r0, g), _stack_heads(dov, r0, g)
                delta = jnp.sum(dos.astype(F32) * _stack_heads(ov, r0, g).astype(F32), axis=1, keepdims=True)
                kg = kw[r0:r0 + 3 * WINDOW, g * HEAD_DIM:(g + 1) * HEAD_DIM]
                vg = vw[r0:r0 + 3 * WINDOW, g * HEAD_DIM:(g + 1) * HEAD_DIM]
                lse_col = l_ref[g, sb]
                pr = jnp.where(valid, jnp.exp(_dot(qs, kg, NT) - lse_col), 0.0)
                ds = (pr * (_dot(dos, vg, NT) - delta)).astype(BF16)
                dq_p.append(_dot(ds, kg, NN))
                dk_p.append(_dot(ds, qs, TN))
                dv_p.append(_dot(pr.astype(BF16), dos, TN))
                ps = jnp.exp(_sink_column(sink_ref, g) - lse_col) * delta
                for h in range(GROUP):
                    tot = jnp.sum(ps[h * WINDOW:(h + 1) * WINDOW], axis=0, keepdims=True)
                    dsink = dsink - jnp.where(lane16 == GROUP * g + h, tot, 0.0)
            dq_ref[r0:r0 + WINDOW, :] = _unstack_heads(dq_p)
            dkw_ref[sb] = jnp.concatenate(dk_p, axis=1)
            dvw_ref[sb] = jnp.concatenate(dv_p, axis=1)

        @pl.when(i == 0)
        def _():
            dsink_ref[...] = jnp.zeros_like(dsink_ref)
        dsink_ref[...] += dsink

    rowq = pl.BlockSpec((tq, Q_DIM), lambda i: (i, 0))
    win = pl.BlockSpec((nsb, 3 * WINDOW, KV_DIM), lambda i: (i, 0, 0))
    wshape = jax.ShapeDtypeStruct((t // WINDOW, 3 * WINDOW, KV_DIM), F32)
    return pl.pallas_call(
        body, grid=(t // tq,),
        in_specs=[pl.BlockSpec(memory_space=pltpu.SMEM), rowq, rowq, rowq,
                  pl.BlockSpec((N_KV_HEADS, nsb, GROUP * WINDOW, 1), lambda i: (0, i, 0, 0))]
        + _window_specs(t, tq) + _window_specs(t, tq),
        out_specs=[rowq, win, win, _full((1, N_Q_HEADS))],
        out_shape=[jax.ShapeDtypeStruct((t, Q_DIM), F32), wshape, wshape, jax.ShapeDtypeStruct((1, N_Q_HEADS), F32)],
        compiler_params=_cp("arbitrary"), name=name)(sink, q, o, do, lse, k, k, k, v, v, v)


def _attn_post_bwd(dq, dkw, dvw, tables, *, name):
    t = dq.shape[0]
    nb = t // WINDOW
    n = Q_DIM + 2 * KV_DIM

    def body(dq_ref, ka_ref, kb_ref, kc_ref, va_ref, vb_ref, vc_ref, c_ref, sh_ref, sl_ref, o_ref):
        j = pl.program_id(0)
        lo = (j > 0).astype(F32)
        hi = (j < nb - 1).astype(F32)
        c, sh, sl = c_ref[...], sh_ref[...], sl_ref[...]
        dk = ka_ref[...] * hi + kb_ref[...] + kc_ref[...] * lo
        dv = va_ref[...] * hi + vb_ref[...] + vc_ref[...] * lo
        o_ref[:, :Q_DIM] = (_rope(dq_ref[...], c, sh, sl, -1.0) * HEAD_DIM ** -0.5).astype(BF16)
        o_ref[:, Q_DIM:Q_DIM + KV_DIM] = _rope(dk, c, sh, sl, -1.0).astype(BF16)
        o_ref[:, Q_DIM + KV_DIM:] = dv.astype(BF16)

    wins = [pl.BlockSpec((None, WINDOW, KV_DIM), lambda j: (jnp.minimum(j + 1, nb - 1), 0, 0)),
            pl.BlockSpec((None, WINDOW, KV_DIM), lambda j: (j, 1, 0)),
            pl.BlockSpec((None, WINDOW, KV_DIM), lambda j: (jnp.maximum(j - 1, 0), 2, 0))]
    tab = pl.BlockSpec((WINDOW, LANES), lambda j: (j, 0))
    return pl.pallas_call(
        body, grid=(nb,),
        in_specs=[pl.BlockSpec((WINDOW, Q_DIM), lambda j: (j, 0))] + wins + wins + [tab, tab, tab],
        out_specs=pl.BlockSpec((WINDOW, n), lambda j: (j, 0)),
        out_shape=jax.ShapeDtypeStruct((t, n), BF16),
        compiler_params=_cp("parallel"), name=name)(dq, dkw, dkw, dkw, dvw, dvw, dvw, *tables)


HGRN_ROWS = 512
HGRN_UNROLL = 2


def _cum_mask(rev, transpose=False):
    row = lax.broadcasted_iota(jnp.int32, (HGRN_CHUNK, HGRN_CHUNK), 0)
    col = lax.broadcasted_iota(jnp.int32, (HGRN_CHUNK, HGRN_CHUNK), 1)
    return (row <= col) if rev != transpose else (row >= col)


def _gates(zq, zf, lb):
    q = zq * _sigmoid(zq)
    sig = _sigmoid(zf)
    f = lb + (1.0 - lb) * sig
    k = (1.0 - lb) * (1.0 - sig)
    return q, sig, f, k


def _intra_exact(q, k, b, mask):
    rows = lax.broadcasted_iota(jnp.int32, (HGRN_CHUNK, 1), 0)
    cols = lax.broadcasted_iota(jnp.int32, (HGRN_CHUNK, HGRN_CHUNK), 1)

    def it(s, a):
        pick = rows == s
        b_s = jnp.sum(jnp.where(pick, b, 0.0), axis=0, keepdims=True)
        k_s = jnp.sum(jnp.where(pick, k, 0.0), axis=0, keepdims=True)
        col = jnp.sum(q * jnp.exp(jnp.minimum(b - b_s, 0.0)) * k_s, axis=1, keepdims=True)
        return jnp.where(cols == s, col, a)

    a = lax.fori_loop(0, HGRN_CHUNK, it, jnp.zeros((HGRN_CHUNK, HGRN_CHUNK), F32))
    return jnp.where(mask, a, 0.0)


def _intra_exact_bwd(q, k, b, da):
    rows = lax.broadcasted_iota(jnp.int32, (HGRN_CHUNK, 1), 0)
    cols = lax.broadcasted_iota(jnp.int32, (HGRN_CHUNK, HGRN_CHUNK), 1)

    def it(s, carry):
        dq, dk = carry
        pick = rows == s
        b_s = jnp.sum(jnp.where(pick, b, 0.0), axis=0, keepdims=True)
        k_s = jnp.sum(jnp.where(pick, k, 0.0), axis=0, keepdims=True)
        w = jnp.sum(jnp.where(cols == s, da, 0.0), axis=1, keepdims=True) * jnp.exp(jnp.minimum(b - b_s, 0.0))
        dq = dq + w * k_s
        dk = jnp.where(pick, jnp.sum(w * q, axis=0, keepdims=True), dk)
        return dq, dk

    z = jnp.zeros((HGRN_CHUNK, HGRN_KEY), F32)
    return lax.fori_loop(0, HGRN_CHUNK, it, (z, z))


def _chunk_cumsum(g, from_end):
    n = g.shape[0]
    row = lax.broadcasted_iota(jnp.int32, g.shape, 0)
    x, s = g, 1
    while s < n:
        if from_end:
            x = x + jnp.where(row < n - s, pltpu.roll(x, n - s, 0), 0.0)
        else:
            x = x + jnp.where(row >= s, pltpu.roll(x, s, 0), 0.0)
        s *= 2
    return x


def _head(a, h):
    return a[:, h * LANES:(h + 1) * LANES]


def _block_is_mild(zf_ref, lb, nch):
    g = jnp.log(lb + (1.0 - lb) * _sigmoid(zf_ref[...]))
    lows = [jnp.min(jnp.sum(g[c * HGRN_CHUNK:(c + 1) * HGRN_CHUNK], axis=0, keepdims=True)) for c in range(nch)]
    return functools.reduce(jnp.minimum, lows) >= FACTORED_DECAY_LIMIT


def _hgrn_fwd(z, lb, rev, *, name, guest=None):
    t = z.shape[0]
    rb = min(HGRN_ROWS, t)
    nb, nch = t // rb, rb // HGRN_CHUNK
    heads = range(HGRN_HEADS)

    def blk(n):
        return nb - 1 - n if rev else n

    def body(zq_ref, zf_ref, zv_ref, lb_ref, o_ref, s_ref, st_ref):
        @pl.when(pl.program_id(0) == 0)
        def _():
            st_ref[...] = jnp.zeros_like(st_ref)
        lbv = lb_ref[...]
        cmask = _cum_mask(rev)

        def chunk(c, carry, mild):
            cc = nch - 1 - c if rev else c
            sl = pl.ds(pl.multiple_of(cc * HGRN_CHUNK, HGRN_CHUNK), HGRN_CHUNK)
            q, _, f, k = _gates(zq_ref[sl, :], zf_ref[sl, :], lbv)
            v = zv_ref[sl, :].astype(BF16)
            g = jnp.log(f)
            b = _chunk_cumsum(g, rev)
            tot = jnp.sum(g, axis=0, keepdims=True)
            qd = (q * jnp.exp(b)).astype(BF16)
            kd = (k * jnp.exp(tot - b)).astype(BF16)
            etot = jnp.exp(tot)

            def factored():
                kf = (k * jnp.exp(-b)).astype(BF16)
                return tuple(jnp.where(cmask, _dot(_head(qd, h), _head(kf, h), NT), 0.0) for h in heads)

            def exact():
                return tuple(_intra_exact(_head(q, h), _head(k, h), _head(b, h), cmask) for h in heads)

            a = factored() if mild else lax.cond(jnp.min(tot) >= FACTORED_DECAY_LIMIT, factored, exact)
            for h in heads:
                st = st_ref[h]
                st_bf = st.astype(BF16)
                s_ref[h, cc] = st_bf
                o_ref[sl, h * LANES:(h + 1) * LANES] = (
                    _dot(_head(qd, h), st_bf, NT) + _dot(a[h].astype(BF16), _head(v, h), NN))
                st_ref[h] = st * _head(etot, h) + _dot(_head(v, h), _head(kd, h), TN)
            return carry

        lax.cond(_block_is_mild(zf_ref, lbv, nch),
                 lambda: lax.fori_loop(0, nch, functools.partial(chunk, mild=True), 0, unroll=HGRN_UNROLL),
                 lambda: lax.fori_loop(0, nch, functools.partial(chunk, mild=False), 0))

    def col(j):
        return pl.BlockSpec((rb, D_MODEL), lambda n: (blk(n), j))

    return _call(
        body, grid=(nb,),
        in_specs=[col(0), col(2 if rev else 1), col(3), _full((1, D_MODEL))],
        out_specs=[col(0), pl.BlockSpec((HGRN_HEADS, nch, LANES, LANES), lambda n: (0, blk(n), 0, 0))],
        out_shape=[jax.ShapeDtypeStruct((t, D_MODEL), F32),
                   jax.ShapeDtypeStruct((HGRN_HEADS, t // HGRN_CHUNK, LANES, LANES), BF16)],
        scratch_shapes=[pltpu.VMEM((HGRN_HEADS, LANES, LANES), F32)],
        args=(z, z, z, lb), sem=("arbitrary",), name=name, guest=guest)


def _hgrn_bwd(z, lb, d_o, states, rev, *, name, other=None):
    t = z.shape[0]
    rb = min(HGRN_ROWS, t)
    nb, nch = t // rb, rb // HGRN_CHUNK
    heads = range(HGRN_HEADS)
    n_other = 0 if other is None else 2
    acc_dtype = F32 if other is None else BF16

    def blk(n):
        return n if rev else nb - 1 - n

    def body(zq_ref, zf_ref, zv_ref, lb_ref, do_ref, s_ref, *rest):
        oq_ref, ov_ref = rest[:n_other] if other is not None else (None, None)
        dq_ref, dzf_ref, dv_ref, dlb_ref, dst_ref = rest[n_other:]

        @pl.when(pl.program_id(0) == 0)
        def _():
            dst_ref[...] = jnp.zeros_like(dst_ref)
            dlb_ref[...] = jnp.zeros_like(dlb_ref)
        lbv = lb_ref[...]
        cmask = _cum_mask(rev)

        def chunk(c, carry, mild):
            cc = c if rev else nch - 1 - c
            sl = pl.ds(pl.multiple_of(cc * HGRN_CHUNK, HGRN_CHUNK), HGRN_CHUNK)
            zq = zq_ref[sl, :]
            q, sig, f, k = _gates(zq, zf_ref[sl, :], lbv)
            v = zv_ref[sl, :].astype(BF16)
            g = jnp.log(f)
            b = _chunk_cumsum(g, rev)
            tot = jnp.sum(g, axis=0, keepdims=True)
            eb = jnp.exp(b)
            etb = jnp.exp(tot - b)
            etot = jnp.exp(tot)
            qd = (q * eb).astype(BF16)
            kd = (k * etb).astype(BF16)
            do = do_ref[sl, :].astype(BF16)
            da = [jnp.where(cmask, _dot(_head(do, h), _head(v, h), NT), 0.0) for h in heads]

            def factored():
                ke = jnp.exp(-b)
                kf = (k * ke).astype(BF16)
                res = []
                for h in heads:
                    qh, kh = _head(qd, h), _head(kf, h)
                    da_bf = da[h].astype(BF16)
                    dqf, dkf = _dot(da_bf, kh, NN), _dot(da_bf, qh, TN)
                    res.append((jnp.where(cmask, _dot(qh, kh, NT), 0.0), dqf * _head(eb, h), dkf * _head(ke, h),
                                qh.astype(F32) * dqf - kh.astype(F32) * dkf))
                return tuple(res)

            def exact():
                res = []
                for h in heads:
                    qh, kh, bh = _head(q, h), _head(k, h), _head(b, h)
                    dq_i, dk_i = _intra_exact_bwd(qh, kh, bh, da[h])
                    res.append((_intra_exact(qh, kh, bh, cmask), dq_i, dk_i, qh * dq_i - kh * dk_i))
                return tuple(res)

            intra = factored() if mild else lax.cond(jnp.min(tot) >= FACTORED_DECAY_LIMIT, factored, exact)
            dq_p, dk_p, db_p, dtot_p = [], [], [], []
            for h in heads:
                a, dq_i, dk_i, db_i = intra[h]
                doh, vh, qh, kh = _head(do, h), _head(v, h), _head(q, h), _head(k, h)
                st0 = s_ref[h, cc]
                dst = dst_ref[h]
                dst_bf = dst.astype(BF16)
                dv = _dot(a.astype(BF16), doh, TN) + _dot(_head(kd, h), dst_bf, NT)
                if ov_ref is not None:
                    dv = dv + ov_ref[sl, h * LANES:(h + 1) * LANES]
                dv_ref[sl, h * LANES:(h + 1) * LANES] = dv.astype(acc_dtype)
                dq_x = _dot(doh, st0, NN) * _head(eb, h)
                dk_x = _dot(vh, dst_bf, NN) * _head(etb, h)
                dtot_p.append(jnp.sum(kh * dk_x, axis=0, keepdims=True)
                              + _head(etot, h) * jnp.sum(dst * st0.astype(F32), axis=0, keepdims=True))
                dst_ref[h] = dst * _head(etot, h) + _dot(doh, _head(qd, h), TN)
                dq_p.append(dq_i + dq_x)
                dk_p.append(dk_i + dk_x)
                db_p.append(db_i + qh * dq_x - kh * dk_x)
            dq, dk, db = (jnp.concatenate(x, axis=1) for x in (dq_p, dk_p, db_p))
            dg = _chunk_cumsum(db, not rev) + jnp.concatenate(dtot_p, axis=1)
            sq = _sigmoid(zq)
            dq_pre = dq * sq * (1.0 + zq * (1.0 - sq))
            if oq_ref is not None:
                dq_pre = dq_pre + oq_ref[sl, :]
            dq_ref[sl, :] = dq_pre.astype(acc_dtype)
            dfk = dg / f - dk
            dzf_ref[sl, :] = (dfk * (1.0 - lbv) * sig * (1.0 - sig)).astype(BF16)
            dlb_ref[...] += jnp.sum(dfk * (1.0 - sig), axis=0, keepdims=True)
            return carry

        lax.cond(_block_is_mild(zf_ref, lbv, nch),
                 lambda: lax.fori_loop(0, nch, functools.partial(chunk, mild=True), 0, unroll=HGRN_UNROLL),
                 lambda: lax.fori_loop(0, nch, functools.partial(chunk, mild=False), 0))

    def col(j):
        return pl.BlockSpec((rb, D_MODEL), lambda n: (blk(n), j))

    return pl.pallas_call(
        body, grid=(nb,),
        in_specs=[col(0), col(2 if rev else 1), col(3), _full((1, D_MODEL)), col(0),
                  pl.BlockSpec((HGRN_HEADS, nch, LANES, LANES), lambda n: (0, blk(n), 0, 0))] + [col(0)] * n_other,
        out_specs=[col(0), col(0), col(0), _full((1, D_MODEL))],
        out_shape=[jax.ShapeDtypeStruct((t, D_MODEL), acc_dtype), jax.ShapeDtypeStruct((t, D_MODEL), BF16),
                   jax.ShapeDtypeStruct((t, D_MODEL), acc_dtype), jax.ShapeDtypeStruct((1, D_MODEL), F32)],
        scratch_shapes=[pltpu.VMEM((HGRN_HEADS, LANES, LANES), F32)],
        compiler_params=_cp("arbitrary"), name=name)(z, z, z, lb, d_o, states, *(other or ()))


def _hgrn_post_fwd(o_f, o_b, z, norm_g, *, name):
    t = o_f.shape[0]
    tm = _rows(t)

    def body(of_ref, ob_ref, gate_ref, ng_ref, y_ref):
        ng = ng_ref[...]
        for h in range(HGRN_HEADS):
            sl = slice(h * LANES, (h + 1) * LANES)
            o = of_ref[:, sl] + ob_ref[:, sl]
            r = lax.rsqrt(jnp.mean(o * o, axis=1, keepdims=True) + LN_EPS)
            gt = gate_ref[:, sl]
            y_ref[:, sl] = (o * r * ng * (gt * _sigmoid(gt))).astype(BF16)

    row = pl.BlockSpec((tm, D_MODEL), lambda i: (i, 0))
    return pl.pallas_call(
        body, grid=(t // tm,),
        in_specs=[row, row, pl.BlockSpec((tm, D_MODEL), lambda i: (i, 4)), _full((1, LANES))],
        out_specs=row, out_shape=jax.ShapeDtypeStruct((t, D_MODEL), BF16),
        compiler_params=_cp("parallel"), name=name)(o_f, o_b, z, norm_g)


def _hgrn_post_bwd(dy, o_f, o_b, z, norm_g, *, name):
    t = o_f.shape[0]
    tm = _rows(t)

    def body(dy_ref, of_ref, ob_ref, gate_ref, ng_ref, do_ref, dgate_ref, dng_ref):
        ng = ng_ref[...]
        dng = jnp.zeros((1, LANES), F32)
        for h in range(HGRN_HEADS):
            sl = slice(h * LANES, (h + 1) * LANES)
            o = of_ref[:, sl] + ob_ref[:, sl]
            r = lax.rsqrt(jnp.mean(o * o, axis=1, keepdims=True) + LN_EPS)
            gt = gate_ref[:, sl]
            sg = _sigmoid(gt)
            dyv = dy_ref[:, sl].astype(F32)
            xn = o * r
            dgate_ref[:, sl] = (dyv * xn * ng * sg * (1.0 + gt * (1.0 - sg))).astype(BF16)
            dn = dyv * gt * sg
            dng = dng + jnp.sum(dn * xn, axis=0, keepdims=True)
            dxn = dn * ng
            do_ref[:, sl] = r * (dxn - xn * jnp.mean(dxn * xn, axis=1, keepdims=True))

        @pl.when(pl.program_id(0) == 0)
        def _():
            dng_ref[...] = jnp.zeros_like(dng_ref)
        dng_ref[...] += dng

    row = pl.BlockSpec((tm, D_MODEL), lambda i: (i, 0))
    return pl.pallas_call(
        body, grid=(t // tm,),
        in_specs=[row, row, row, pl.BlockSpec((tm, D_MODEL), lambda i: (i, 4)), _full((1, LANES))],
        out_specs=[row, row, _full((1, LANES))],
        out_shape=[jax.ShapeDtypeStruct((t, D_MODEL), F32), jax.ShapeDtypeStruct((t, D_MODEL), BF16),
                   jax.ShapeDtypeStruct((1, LANES), F32)],
        compiler_params=_cp("arbitrary"), name=name)(dy, o_f, o_b, z, norm_g)


def _hgrn_combine(dq_f, dq_b, dzf_f, dzf_b, dv_f, dv_b, dgate, *, name):
    t = dq_f.shape[0]
    tm = _rows(t)

    def body(qf, qb, ff, fb, vf, vb, gt, o_ref):
        o_ref[:, 0 * D_MODEL:1 * D_MODEL] = (qf[...] + qb[...]).astype(BF16)
        o_ref[:, 1 * D_MODEL:2 * D_MODEL] = ff[...]
        o_ref[:, 2 * D_MODEL:3 * D_MODEL] = fb[...]
        o_ref[:, 3 * D_MODEL:4 * D_MODEL] = (vf[...] + vb[...]).astype(BF16)
        o_ref[:, 4 * D_MODEL:5 * D_MODEL] = gt[...]

    row = pl.BlockSpec((tm, D_MODEL), lambda i: (i, 0))
    return pl.pallas_call(
        body, grid=(t // tm,), in_specs=[row] * 7,
        out_specs=pl.BlockSpec((tm, 5 * D_MODEL), lambda i: (i, 0)),
        out_shape=jax.ShapeDtypeStruct((t, 5 * D_MODEL), BF16),
        compiler_params=_cp("parallel"), name=name)(dq_f, dq_b, dzf_f, dzf_b, dv_f, dv_b, dgate)


def _lower_bounds(logits2d, *, name):
    def body(l_ref, lb_ref):
        l = l_ref[...]
        e = jnp.exp(l - jnp.max(l, axis=0, keepdims=True))
        sm = e / jnp.sum(e, axis=0, keepdims=True)
        s1, s2, s3 = sm[1:2], sm[2:3], sm[3:4]
        lb_ref[...] = jnp.concatenate([jnp.zeros_like(s1), s1, s1 + s2, s1 + s2 + s3], axis=0)

    return pl.pallas_call(body, out_shape=jax.ShapeDtypeStruct(logits2d.shape, F32), name=name)(logits2d)


def _lower_bounds_bwd(logits2d, dlb, *, name):
    def body(l_ref, d_ref, o_ref):
        l = l_ref[...]
        e = jnp.exp(l - jnp.max(l, axis=0, keepdims=True))
        sm = e / jnp.sum(e, axis=0, keepdims=True)
        d = d_ref[...]
        d1, d2, d3 = d[1:2], d[2:3], d[3:4]
        dsm = jnp.concatenate([jnp.zeros_like(d1), d1 + d2 + d3, d2 + d3, d3], axis=0)
        o_ref[...] = sm * (dsm - jnp.sum(sm * dsm, axis=0, keepdims=True))

    return pl.pallas_call(body, out_shape=jax.ShapeDtypeStruct(logits2d.shape, F32), name=name)(logits2d, dlb)


def _adamw(w, g, m, v, *, name):
    r, c = w.shape
    tr = r if r <= 512 else _pick_rows(r)

    def body(w_ref, g_ref, m_ref, v_ref, d_ref, nm_ref, nv_ref):
        gv = g_ref[...]
        nm = ADAM_B1 * m_ref[...] + (1.0 - ADAM_B1) * gv
        nv = ADAM_B2 * v_ref[...] + (1.0 - ADAM_B2) * (gv * gv)
        m_hat = nm / (1.0 - ADAM_B1 ** ADAM_STEP)
        v_hat = nv / (1.0 - ADAM_B2 ** ADAM_STEP)
        d_ref[...] = -ADAM_LR * (m_hat / (jnp.sqrt(v_hat) + ADAM_EPS) + ADAM_WD * w_ref[...])
        nm_ref[...] = nm
        nv_ref[...] = nv

    blk = pl.BlockSpec((tr, c), lambda i: (i, 0))
    shp = jax.ShapeDtypeStruct((r, c), F32)
    return pl.pallas_call(body, grid=(r // tr,), in_specs=[blk] * 4, out_specs=[blk] * 3, out_shape=[shp] * 3,
                          compiler_params=_cp("parallel"), name=name)(w, g, m, v)


def _pick_rows(r, cap=512):
    best = 8
    for d in range(8, cap + 1, 8):
        if r % d == 0:
            best = d
    assert r % best == 0, r
    return best


def _local_step(x, p, target, w, sp):
    t = x.shape[0]
    tables = _rope_tables(t)
    logits2d = sp["hgrn_lb_logits"].reshape(DEPTH, 2 * D_MODEL)
    lb_all = _lower_bounds(logits2d, name="lb_fwd").reshape(DEPTH, 2, 1, D_MODEL)
    row = lambda a, i: a[i][None]

    saved = []
    xf, xb = x, x.astype(BF16)
    for i in range(DEPTH):
        j = i // 2
        s = dict(xin_bf=xb)
        if i % 2 == 0:
            q, k, v = _qkv_rope(xb, w["att_w_qkv"][j], tables, name=f"qkv_rope_{i}")
            o, lse = _attn_fwd(q, k, v, sp["att_sink"][j], name=f"attn_fwd_{i}")
            s.update(q=q, k=k, v=v, o=o, lse=lse)
            mix_in, w_o = o, w["att_w_o"][j]
        else:
            z = _mm_nn(xb, w["hgrn_w_in"][j], out_dtype=F32, name=f"hgrn_in_{i}")
            o_f, s_f = _hgrn_fwd(z, lb_all[i, 0], False, name=f"hgrn_scan_f_{i}")
            o_b, s_b = _hgrn_fwd(z, lb_all[i, 1], True, name=f"hgrn_scan_b_{i}")
            og = _hgrn_post_fwd(o_f, o_b, z, row(sp["hgrn_norm_g"], j), name=f"hgrn_post_{i}")
            s.update(z=z, o_f=o_f, o_b=o_b, s_f=s_f, s_b=s_b, og=og)
            mix_in, w_o = og, w["hgrn_w_o"][j]
        x1, x1b, xh1, rs1 = _mm_res_ln(mix_in, w_o, xf, row(sp["ln_mix_g"], i), row(sp["ln_mix_b"], i), name=f"mix_out_ln_{i}")
        g, u, h = _ffn_in(x1b, w["ffn_w_in"][i], name=f"ffn_in_{i}")
        x2, x2b, xh2, rs2 = _mm_res_ln(h, w["ffn_w_out"][i], x1, row(sp["ln_ffn_g"], i), row(sp["ln_ffn_b"], i), name=f"ffn_out_ln_{i}")
        xf, xb, gate, pp = _ple_fwd(x2, x2b, p, i, w["ple_w_gate"][i], w["ple_w_proj"][i], name=f"ple_fwd_{i}")
        s.update(x1b=x1b, xh1=xh1, rs1=rs1, g=g, u=u, h=h, x2b=x2b, xh2=xh2, rs2=rs2, gate=gate, pp=pp)
        saved.append(s)

    loss, dx = _loss_head(xf, target, name="loss_head")

    gw = {n: [None] * w[n].shape[0] for n in w}
    gs = {n: [None] * DEPTH for n in ("ln_mix_g", "ln_mix_b", "ln_ffn_g", "ln_ffn_b")}
    gs.update(att_sink=[None] * 2, hgrn_norm_g=[None] * 2)
    dlb = [jnp.zeros((2, D_MODEL), F32)] * DEPTH
    for i in reversed(range(DEPTH)):
        j = i // 2
        s = saved[i]
        du2, du2b, dpre, dpp, gs["ln_ffn_g"][i], gs["ln_ffn_b"][i] = _ple_bwd(
            dx, s["gate"], s["pp"], w["ple_w_gate"][i], s["xh2"], s["rs2"], row(sp["ln_ffn_g"], i), name=f"ple_bwd_{i}")
        gw["ple_w_gate"][i] = _mm_tn(s["x2b"], dpre, name=f"dw_ple_gate_{i}")
        gw["ple_w_proj"][i] = _mm_tn_p(p, i, dpp, name=f"dw_ple_proj_{i}")
        dgg, dgu = _ffn_out_bwd(du2b, w["ffn_w_out"][i], s["g"], s["u"], name=f"ffn_out_bwd_{i}")
        gw["ffn_w_out"][i] = _mm_tn(s["h"], du2b, name=f"dw_ffn_out_{i}")
        du1, du1b, gs["ln_mix_g"][i], gs["ln_mix_b"][i] = _mm_nt(
            [dgg, dgu], w["ffn_w_in"][i], tk=_pick(D_FF, 1408), resid=du2,
            ln=(s["xh1"], s["rs1"], row(sp["ln_mix_g"], i)), name=f"ffn_in_bwd_{i}")
        gw["ffn_w_in"][i] = jnp.concatenate(
            [_mm_tn(s["x1b"], dgg, name=f"dw_ffn_gate_{i}"), _mm_tn(s["x1b"], dgu, name=f"dw_ffn_up_{i}")], axis=1)
        if i % 2 == 0:
            gw["att_w_o"][j] = _mm_tn(s["o"], du1b, name=f"dw_att_o_{i}")
            do = _mm_nt([du1b], w["att_w_o"][j], tk=Q_DIM, out_dtype=BF16, name=f"att_o_bwd_{i}")
            dq, dkw, dvw, gs["att_sink"][j] = _attn_bwd(
                s["q"], s["k"], s["v"], s["o"], do, s["lse"], sp["att_sink"][j], name=f"attn_bwd_{i}")
            dqkv = _attn_post_bwd(dq, dkw, dvw, tables, name=f"attn_post_bwd_{i}")
            gw["att_w_qkv"][j] = _mm_tn(s["xin_bf"], dqkv, name=f"dw_att_qkv_{i}")
            dx = _mm_nt([dqkv], w["att_w_qkv"][j], tk=Q_DIM + 2 * KV_DIM, resid=du1, name=f"qkv_bwd_{i}")
        else:
            gw["hgrn_w_o"][j] = _mm_tn(s["og"], du1b, name=f"dw_hgrn_o_{i}")
            dy = _mm_nt([du1b], w["hgrn_w_o"][j], tk=D_MODEL, out_dtype=BF16, name=f"hgrn_o_bwd_{i}")
            d_o, dgate, gs["hgrn_norm_g"][j] = _hgrn_post_bwd(
                dy, s["o_f"], s["o_b"], s["z"], row(sp["hgrn_norm_g"], j), name=f"hgrn_post_bwd_{i}")
            dq_f, dzf_f, dv_f, dlb_f = _hgrn_bwd(s["z"], lb_all[i, 0], d_o, s["s_f"], False, name=f"hgrn_scan_f_bwd_{i}")
            dq_b, dzf_b, dv_b, dlb_b = _hgrn_bwd(s["z"], lb_all[i, 1], d_o, s["s_b"], True, name=f"hgrn_scan_b_bwd_{i}")
            dlb[i] = jnp.stack([dlb_f.reshape(D_MODEL), dlb_b.reshape(D_MODEL)])
            dz = _hgrn_combine(dq_f, dq_b, dzf_f, dzf_b, dv_f, dv_b, dgate, name=f"hgrn_combine_{i}")
            gw["hgrn_w_in"][j] = _mm_tn(s["xin_bf"], dz, name=f"dw_hgrn_in_{i}")
            dx = _mm_nt([dz], w["hgrn_w_in"][j], tk=_pick(5 * D_MODEL, 1408), resid=du1, name=f"hgrn_in_bwd_{i}")

    gw = {n: jnp.stack(v) for n, v in gw.items()}
    gsmall = {n: jnp.concatenate(v, axis=0) for n, v in gs.items()}
    gsmall["hgrn_lb_logits"] = _lower_bounds_bwd(
        logits2d, jnp.stack(dlb).reshape(DEPTH, 2 * D_MODEL), name="lb_bwd").reshape(DEPTH, 2, D_MODEL)
    return loss, dx, gw, gsmall


ANY = pl.BlockSpec(memory_space=pl.ANY)


def _place():
    x, y, c = lax.axis_index("x"), lax.axis_index("y"), lax.axis_index("c")
    chips = [(1 - x, y), (x, 1 - y), (1 - x, 1 - y)]
    return x, y, c, chips


def _remote(src, dst, send_sem, recv_sem, to):
    return pltpu.make_async_remote_copy(src_ref=src, dst_ref=dst, send_sem=send_sem, recv_sem=recv_sem,
                                        device_id=to, device_id_type=MESH)


def _allgather_weights(packed, *, name):
    r = packed.shape[0]
    hr = r // 2

    def body(src_ref, out_ref, send_sems, recv_sems, local_sem):
        x, y, c, chips = _place()
        sibling = (x, y, 1 - c)

        def half(cx, cy, hc):
            return out_ref.at[2 * cx + cy, pl.ds(hc * hr, hr), :]

        mine = pltpu.make_async_copy(src_ref, out_ref.at[2 * x + y], local_sem)
        mine.start()
        my_half = src_ref.at[pl.ds(c * hr, hr), :]
        first = [_remote(my_half, half(x, y, c), send_sems.at[j], recv_sems.at[j], (*chip, c)) for j, chip in enumerate(chips)]
        for cp in first:
            cp.start()
        passed = [_remote(half(*chip, c), half(*chip, c), send_sems.at[3 + j], recv_sems.at[3 + j], sibling)
                  for j, chip in enumerate(chips)]
        for j, chip in enumerate(chips):
            _remote(my_half, half(*chip, c), send_sems.at[j], recv_sems.at[j], (*chip, c)).wait_recv()
            passed[j].start()
        for j, chip in enumerate(chips):
            _remote(my_half, half(*chip, 1 - c), send_sems.at[3 + j], recv_sems.at[3 + j], sibling).wait_recv()
        for cp in first + passed:
            cp.wait_send()
        mine.wait()

    return pl.pallas_call(
        body, in_specs=[ANY], out_specs=ANY, out_shape=jax.ShapeDtypeStruct((N_CHIPS, r, packed.shape[1]), packed.dtype),
        scratch_shapes=[pltpu.SemaphoreType.DMA((6,)), pltpu.SemaphoreType.DMA((6,)), pltpu.SemaphoreType.DMA],
        name=name)(packed)


def _allreduce_small(block, *, name):
    m, n = block.shape

    def body(x_ref, sum_ref, all_ref, send_sems, recv_sems):
        x, y, c, chips = _place()
        me, sibling = (x, y, c), (x, y, 1 - c)

        def rows(px, py, pc):
            return all_ref.at[pl.ds((4 * px + 2 * py + pc) * m, m), :]

        all_ref[pl.ds((4 * x + 2 * y + c) * m, m), :] = x_ref[...]
        first = [_remote(x_ref, rows(*me), send_sems.at[0], recv_sems.at[0], sibling)]
        first += [_remote(x_ref, rows(*me), send_sems.at[1 + j], recv_sems.at[1 + j], (*chip, c)) for j, chip in enumerate(chips)]
        for cp in first:
            cp.start()
        passed = [_remote(rows(*chip, c), rows(*chip, c), send_sems.at[4 + j], recv_sems.at[4 + j], sibling)
                  for j, chip in enumerate(chips)]
        for j, chip in enumerate(chips):
            _remote(x_ref, rows(*chip, c), send_sems.at[1 + j], recv_sems.at[1 + j], me).wait_recv()
            passed[j].start()
        _remote(x_ref, rows(*sibling), send_sems.at[0], recv_sems.at[0], me).wait_recv()
        for j, chip in enumerate(chips):
            _remote(x_ref, rows(*chip, 1 - c), send_sems.at[4 + j], recv_sems.at[4 + j], me).wait_recv()
        for cp in first + passed:
            cp.wait_send()
        acc = all_ref[pl.ds(0, m), :]
        for d in range(1, 8):
            acc = acc + all_ref[pl.ds(d * m, m), :]
        sum_ref[...] = acc

    vmem = pl.BlockSpec(memory_space=pltpu.VMEM)
    return pl.pallas_call(
        body, in_specs=[vmem], out_specs=vmem, out_shape=jax.ShapeDtypeStruct((m, n), F32),
        scratch_shapes=[pltpu.VMEM((8 * m, n), F32), pltpu.SemaphoreType.DMA((7,)), pltpu.SemaphoreType.DMA((7,))],
        name=name)(block)


def _pair_exchange(g, *, name):
    n, r, w = g.shape
    hr = r // 2

    def body(g_ref, out_ref, send_sem, recv_sem):
        x, y, c, _ = _place()
        cp = _remote(g_ref.at[:, pl.ds((1 - c) * hr, hr), :], out_ref, send_sem, recv_sem, (x, y, 1 - c))
        cp.start()
        cp.wait()

    return pl.pallas_call(
        body, in_specs=[ANY], out_specs=ANY, out_shape=jax.ShapeDtypeStruct((n, hr, w), g.dtype),
        scratch_shapes=[pltpu.SemaphoreType.DMA, pltpu.SemaphoreType.DMA], name=name)(g)


def _chip_scatter(part, *, name):
    n, h, w = part.shape

    def body(p_ref, out_ref, send_sems, recv_sems, local_sem):
        x, y, c, chips = _place()
        me = 2 * x + y
        mine = pltpu.make_async_copy(p_ref.at[me], out_ref.at[me], local_sem)
        mine.start()
        sends = [_remote(p_ref.at[2 * cx + cy], out_ref.at[me], send_sems.at[j], recv_sems.at[j], (cx, cy, c))
                 for j, (cx, cy) in enumerate(chips)]
        for cp in sends:
            cp.start()
        for j, (cx, cy) in enumerate(chips):
            _remote(p_ref.at[me], out_ref.at[2 * cx + cy], send_sems.at[j], recv_sems.at[j], (cx, cy, c)).wait_recv()
        for cp in sends:
            cp.wait_send()
        mine.wait()

    return pl.pallas_call(
        body, in_specs=[ANY], out_specs=ANY, out_shape=jax.ShapeDtypeStruct((n, h, w), part.dtype),
        scratch_shapes=[pltpu.SemaphoreType.DMA((3,)), pltpu.SemaphoreType.DMA((3,)), pltpu.SemaphoreType.DMA],
        name=name)(part)


def _pair_share(half, *, name):
    h, w = half.shape

    def body(h_ref, out_ref, send_sem, recv_sem, local_sem):
        x, y, c, _ = _place()
        dst = out_ref.at[pl.ds(c * h, h), :]
        mine = pltpu.make_async_copy(h_ref, dst, local_sem)
        mine.start()
        cp = _remote(h_ref, dst, send_sem, recv_sem, (x, y, 1 - c))
        cp.start()
        cp.wait_send()
        _remote(h_ref, out_ref.at[pl.ds((1 - c) * h, h), :], send_sem, recv_sem, (x, y, 1 - c)).wait_recv()
        mine.wait()

    return pl.pallas_call(
        body, in_specs=[ANY], out_specs=ANY, out_shape=jax.ShapeDtypeStruct((2 * h, w), half.dtype),
        scratch_shapes=[pltpu.SemaphoreType.DMA, pltpu.SemaphoreType.DMA, pltpu.SemaphoreType.DMA], name=name)(half)


def _add_own_half(g, recv, c, *, name):
    n, r, w = g.shape
    hr = r // 2
    tr = _pick_rows(hr, 1024)
    nr = hr // tr

    def body(c_ref, g_ref, r_ref, o_ref):
        o_ref[...] = (g_ref[...] + r_ref[...]).astype(BF16)

    return pl.pallas_call(
        body,
        grid_spec=pltpu.PrefetchScalarGridSpec(
            num_scalar_prefetch=1, grid=(n, nr),
            in_specs=[pl.BlockSpec((None, tr, w), lambda s, i, c_ref: (s, c_ref[0] * nr + i, 0)),
                      pl.BlockSpec((None, tr, w), lambda s, i, c_ref: (s, i, 0))],
            out_specs=pl.BlockSpec((None, tr, w), lambda s, i, c_ref: (s, i, 0))),
        out_shape=jax.ShapeDtypeStruct((n, hr, w), BF16),
        compiler_params=_cp("parallel", "parallel"), name=name)(c, g, recv)


def _sum_chips(q, *, name):
    n, h, w = q.shape
    tr = _pick_rows(h, 1024)

    def body(a, b, c, d, o_ref):
        o_ref[...] = ((a[...].astype(F32) + b[...].astype(F32)) + c[...].astype(F32)) + d[...].astype(F32)

    specs = [pl.BlockSpec((None, tr, w), functools.partial(lambda i, s: (s, i, 0), s=s)) for s in range(n)]
    return pl.pallas_call(
        body, grid=(h // tr,), in_specs=specs, out_specs=pl.BlockSpec((tr, w), lambda i: (i, 0)),
        out_shape=jax.ShapeDtypeStruct((h, w), F32), compiler_params=_cp("parallel"), name=name)(q, q, q, q)


PACK_W = 1024
BIG = (("att_w_qkv", 2), ("att_w_o", 1), ("hgrn_w_in", 2), ("hgrn_w_o", 1),
       ("ffn_w_in", 2), ("ffn_w_out", 1), ("ple_w_gate", 1), ("ple_w_proj", 2))
LB_ROWS = 32


def _pack_shards(shards):
    return jnp.concatenate([shards[n].reshape(-1, PACK_W) for n, _ in BIG], axis=0)


def _unpack_shards(buf, shapes):
    out, r0 = {}, 0
    for n, _ in BIG:
        nr = shapes[n][0] * shapes[n][1] * shapes[n][2] // PACK_W
        out[n] = buf[r0:r0 + nr].reshape(shapes[n])
        r0 += nr
    return out


def _gathered_to_full(buf, shapes):
    out, r0 = {}, 0
    for n, axis in BIG:
        l, r, c = shapes[n]
        nr = l * r * c // PACK_W
        sh = buf[:, r0:r0 + nr].reshape(N_CHIPS, l, r, c)
        out[n] = (sh.transpose(1, 0, 2, 3).reshape(l, N_CHIPS * r, c) if axis == 1
                  else sh.transpose(1, 2, 0, 3).reshape(l, r, N_CHIPS * c))
        r0 += nr
    return out, r0


def _full_to_slabs(full, shapes):
    parts = []
    for n, axis in BIG:
        l, r, c = shapes[n]
        g = full[n]
        g = (g.reshape(l, N_CHIPS, r, c).transpose(1, 0, 2, 3) if axis == 1
             else g.reshape(l, r, N_CHIPS, c).transpose(2, 0, 1, 3))
        parts.append(g.reshape(N_CHIPS, -1, PACK_W))
    return jnp.concatenate(parts, axis=1)


SMALL = ("ln_mix_g", "ln_mix_b", "ln_ffn_g", "ln_ffn_b")
SMALL_ROWS = 32


def _pack_small(vals, lb):
    rows = [vals[n] for n in SMALL] + [lb.reshape(-1, PACK_W)]
    for n in ("att_sink", "hgrn_norm_g"):
        flat = vals[n].reshape(1, -1)
        rows.append(jnp.pad(flat, ((0, 0), (0, PACK_W - flat.shape[1]))))
    buf = jnp.concatenate(rows, axis=0)
    return jnp.pad(buf, ((0, SMALL_ROWS - buf.shape[0]), (0, 0)))


def _unpack_small(buf, lb_shape):
    out, r0 = {}, 0
    for n in SMALL:
        out[n] = buf[r0:r0 + DEPTH]
        r0 += DEPTH
    nlb = lb_shape[0] * lb_shape[1] * lb_shape[2] // PACK_W
    out["hgrn_lb_logits"] = buf[r0:r0 + nlb].reshape(lb_shape)
    r0 += nlb
    out["att_sink"] = buf[r0, :2 * N_Q_HEADS].reshape(2, N_Q_HEADS)
    out["hgrn_norm_g"] = buf[r0 + 1, :2 * LANES].reshape(2, LANES)
    return out


WEIGHTS = ("att_w_qkv", "att_sink", "att_w_o", "hgrn_w_in", "hgrn_lb_logits", "hgrn_norm_g", "hgrn_w_o", "ln_mix_g",
           "ln_mix_b", "ffn_w_in", "ffn_w_out", "ln_ffn_g", "ln_ffn_b", "ple_w_gate", "ple_w_proj")


def kernel(x, p, att_w_qkv, att_sink, att_w_o, hgrn_w_in, hgrn_lb_logits, hgrn_norm_g, hgrn_w_o, ln_mix_g, ln_mix_b, ffn_w_in, ffn_w_out, ln_ffn_g, ln_ffn_b, ple_w_gate, ple_w_proj, loss_target, m_att_w_qkv, m_att_sink, m_att_w_o, m_hgrn_w_in, m_hgrn_lb_logits, m_hgrn_norm_g, m_hgrn_w_o, m_ln_mix_g, m_ln_mix_b, m_ffn_w_in, m_ffn_w_out, m_ln_ffn_g, m_ln_ffn_b, m_ple_w_gate, m_ple_w_proj, v_att_w_qkv, v_att_sink, v_att_w_o, v_hgrn_w_in, v_hgrn_lb_logits, v_hgrn_norm_g, v_hgrn_w_o, v_ln_mix_g, v_ln_mix_b, v_ffn_w_in, v_ffn_w_out, v_ln_ffn_g, v_ln_ffn_b, v_ple_w_gate, v_ple_w_proj):
    wts = dict(att_w_qkv=att_w_qkv, att_sink=att_sink, att_w_o=att_w_o, hgrn_w_in=hgrn_w_in, hgrn_lb_logits=hgrn_lb_logits,
               hgrn_norm_g=hgrn_norm_g, hgrn_w_o=hgrn_w_o, ln_mix_g=ln_mix_g, ln_mix_b=ln_mix_b, ffn_w_in=ffn_w_in,
               ffn_w_out=ffn_w_out, ln_ffn_g=ln_ffn_g, ln_ffn_b=ln_ffn_b, ple_w_gate=ple_w_gate, ple_w_proj=ple_w_proj)
    mom = dict(att_w_qkv=m_att_w_qkv, att_sink=m_att_sink, att_w_o=m_att_w_o, hgrn_w_in=m_hgrn_w_in, hgrn_lb_logits=m_hgrn_lb_logits,
               hgrn_norm_g=m_hgrn_norm_g, hgrn_w_o=m_hgrn_w_o, ln_mix_g=m_ln_mix_g, ln_mix_b=m_ln_mix_b, ffn_w_in=m_ffn_w_in,
               ffn_w_out=m_ffn_w_out, ln_ffn_g=m_ln_ffn_g, ln_ffn_b=m_ln_ffn_b, ple_w_gate=m_ple_w_gate, ple_w_proj=m_ple_w_proj)
    var = dict(att_w_qkv=v_att_w_qkv, att_sink=v_att_sink, att_w_o=v_att_w_o, hgrn_w_in=v_hgrn_w_in, hgrn_lb_logits=v_hgrn_lb_logits,
               hgrn_norm_g=v_hgrn_norm_g, hgrn_w_o=v_hgrn_w_o, ln_mix_g=v_ln_mix_g, ln_mix_b=v_ln_mix_b, ffn_w_in=v_ffn_w_in,
               ffn_w_out=v_ffn_w_out, ln_ffn_g=v_ln_ffn_g, ln_ffn_b=v_ln_ffn_b, ple_w_gate=v_ple_w_gate, ple_w_proj=v_ple_w_proj)
    shapes = {n: wts[n].shape for n, _ in BIG}
    chip = 2 * lax.axis_index("x") + lax.axis_index("y")
    core = lax.axis_index("c")

    lb_bits = lax.bitcast_convert_type(hgrn_lb_logits.reshape(-1), BF16).reshape(-1, PACK_W)
    packed = jnp.concatenate([_pack_shards({n: wts[n].astype(BF16) for n, _ in BIG}), lb_bits,
                              jnp.zeros((LB_ROWS - lb_bits.shape[0], PACK_W), BF16)], axis=0)
    gathered = _allgather_weights(packed, name="allgather_weights")
    w_full, r_big = _gathered_to_full(gathered, shapes)
    lb_sh = hgrn_lb_logits.shape
    lb_rows = gathered[:, r_big:r_big + lb_bits.shape[0]].reshape(N_CHIPS, -1, 2)
    lb_full = lax.bitcast_convert_type(lb_rows, F32).reshape(N_CHIPS, lb_sh[0], lb_sh[1], lb_sh[2])
    lb_full = lb_full.transpose(1, 2, 0, 3).reshape(lb_sh[0], lb_sh[1], N_CHIPS * lb_sh[2])
    sp = dict(att_sink=att_sink, hgrn_lb_logits=lb_full, hgrn_norm_g=hgrn_norm_g, ln_mix_g=ln_mix_g, ln_mix_b=ln_mix_b,
              ln_ffn_g=ln_ffn_g, ln_ffn_b=ln_ffn_b)

    loss, grad_x, gw, gs = _local_step(x[0], p, loss_target[0], w_full, sp)
    loss = lax.psum(loss[0, 0], ("x", "y", "c"))

    slabs = _full_to_slabs(gw, shapes)
    from_sibling = _pair_exchange(slabs, name="grad_pair_exchange")
    chip_part = _add_own_half(slabs, from_sibling, core.reshape(1).astype(jnp.int32), name="grad_pair_add")
    from_chips = _chip_scatter(chip_part, name="grad_chip_scatter")
    half = _sum_chips(from_chips, name="grad_chip_sum")
    g_big = _unpack_shards(_pair_share(half, name="grad_pair_share"), shapes)

    g_small_full = _unpack_small(_allreduce_small(_pack_small(gs, gs["hgrn_lb_logits"]), name="allreduce_small"),
                                 (lb_sh[0], lb_sh[1], N_CHIPS * lb_sh[2]))
    g_lb = lax.dynamic_slice_in_dim(g_small_full["hgrn_lb_logits"], chip * lb_sh[2], lb_sh[2], axis=2)
    grads = dict(g_big)
    grads.update({n: g_small_full[n] for n in SMALL + ("att_sink", "hgrn_norm_g")})
    grads["hgrn_lb_logits"] = g_lb

    delta, new_m, new_v = {}, {}, {}
    for n, _ in BIG:
        two_d = lambda a: a.reshape(-1, a.shape[-1])
        d, nm, nv = _adamw(two_d(wts[n]), two_d(grads[n]), two_d(mom[n]), two_d(var[n]), name=f"adamw_{n}")
        delta[n], new_m[n], new_v[n] = d.reshape(shapes[n]), nm.reshape(shapes[n]), nv.reshape(shapes[n])
    packs = [_pack_small(src, src["hgrn_lb_logits"]) for src in (wts, grads, mom, var)]
    outs = _adamw(*packs, name="adamw_small")
    for dst, buf in zip((delta, new_m, new_v), outs):
        dst.update(_unpack_small(buf, lb_sh))

    return (loss, grad_x[None], *[grads[n] for n in WEIGHTS], *[delta[n] for n in WEIGHTS],
            *[new_m[n] for n in WEIGHTS], *[new_v[n] for n in WEIGHTS])


def _gather_guest(packed):
    r, w = packed.shape
    hr = r // 2

    def copies(src_ref, out_ref, send_sems, recv_sems, local_sem):
        x, y, c, chips = _place()
        sibling = (x, y, 1 - c)

        def half(cx, cy, hc):
            return out_ref.at[2 * cx + cy, pl.ds(hc * hr, hr), :]

        my_half = src_ref.at[pl.ds(c * hr, hr), :]
        mine = pltpu.make_async_copy(src_ref, out_ref.at[2 * x + y], local_sem)
        first = [_remote(my_half, half(x, y, c), send_sems.at[j], recv_sems.at[j], (*chip, c)) for j, chip in enumerate(chips)]
        passed = [_remote(half(*chip, c), half(*chip, c), send_sems.at[3 + j], recv_sems.at[3 + j], sibling)
                  for j, chip in enumerate(chips)]
        from_chips = [_remote(my_half, half(*chip, c), send_sems.at[j], recv_sems.at[j], (*chip, c)) for j, chip in enumerate(chips)]
        from_sibling = [_remote(my_half, half(*chip, 1 - c), send_sems.at[3 + j], recv_sems.at[3 + j], sibling)
                        for j, chip in enumerate(chips)]
        return mine, first, passed, from_chips, from_sibling

    def start(gi, go, gs):
        mine, first, _, _, _ = copies(gi[0], go[0], *gs)
        mine.start()
        for cp in first:
            cp.start()

    def finish(gi, go, gs):
        mine, first, passed, from_chips, from_sibling = copies(gi[0], go[0], *gs)
        for arrival, onward in zip(from_chips, passed):
            arrival.wait_recv()
            onward.start()
        for arrival in from_sibling:
            arrival.wait_recv()
        for cp in first + passed:
            cp.wait_send()
        mine.wait()

    return _Guest((packed,), (jax.ShapeDtypeStruct((N_CHIPS, r, w), packed.dtype),),
                  (pltpu.SemaphoreType.DMA((6,)), pltpu.SemaphoreType.DMA((6,)), pltpu.SemaphoreType.DMA), start, finish)


def _pair_exchange_guest(g):
    n, r, w = g.shape
    hr = r // 2

    def copy(g_ref, out_ref, send_sem, recv_sem):
        x, y, c, _ = _place()
        return _remote(g_ref.at[:, pl.ds((1 - c) * hr, hr), :], out_ref, send_sem, recv_sem, (x, y, 1 - c))

    return _Guest((g,), (jax.ShapeDtypeStruct((n, hr, w), g.dtype),), (pltpu.SemaphoreType.DMA, pltpu.SemaphoreType.DMA),
                  lambda gi, go, gs: copy(gi[0], go[0], *gs).start(),
                  lambda gi, go, gs: copy(gi[0], go[0], *gs).wait())


def _chip_scatter_guest(part):
    n, h, w = part.shape

    def copies(p_ref, out_ref, send_sems, recv_sems, local_sem):
        x, y, c, chips = _place()
        me = 2 * x + y
        mine = pltpu.make_async_copy(p_ref.at[me], out_ref.at[me], local_sem)
        sends = [_remote(p_ref.at[2 * cx + cy], out_ref.at[me], send_sems.at[j], recv_sems.at[j], (cx, cy, c))
                 for j, (cx, cy) in enumerate(chips)]
        arrivals = [_remote(p_ref.at[me], out_ref.at[2 * cx + cy], send_sems.at[j], recv_sems.at[j], (cx, cy, c))
                    for j, (cx, cy) in enumerate(chips)]
        return mine, sends, arrivals

    def start(gi, go, gs):
        mine, sends, _ = copies(gi[0], go[0], *gs)
        mine.start()
        for cp in sends:
            cp.start()

    def finish(gi, go, gs):
        mine, sends, arrivals = copies(gi[0], go[0], *gs)
        for cp in arrivals:
            cp.wait_recv()
        for cp in sends:
            cp.wait_send()
        mine.wait()

    return _Guest((part,), (jax.ShapeDtypeStruct((n, h, w), part.dtype),),
                  (pltpu.SemaphoreType.DMA((3,)), pltpu.SemaphoreType.DMA((3,)), pltpu.SemaphoreType.DMA), start, finish)


def _pair_share_guest(halves):
    n = len(halves)

    def copies(k, h_ref, out_ref, send_sems, recv_sems, local_sems):
        x, y, c, _ = _place()
        mine = pltpu.make_async_copy(h_ref, out_ref.at[c], local_sems.at[k])
        send = _remote(h_ref, out_ref.at[c], send_sems.at[k], recv_sems.at[k], (x, y, 1 - c))
        arrival = _remote(h_ref, out_ref.at[1 - c], send_sems.at[k], recv_sems.at[k], (x, y, 1 - c))
        return mine, send, arrival

    def start(gi, go, gs):
        for k in range(n):
            mine, send, _ = copies(k, gi[k], go[k], *gs)
            mine.start()
            send.start()

    def finish(gi, go, gs):
        for k in range(n):
            mine, send, arrival = copies(k, gi[k], go[k], *gs)
            send.wait_send()
            arrival.wait_recv()
            mine.wait()

    return _Guest(tuple(halves), tuple(jax.ShapeDtypeStruct((2,) + a.shape, a.dtype) for a in halves),
                  (pltpu.SemaphoreType.DMA((n,)), pltpu.SemaphoreType.DMA((n,)), pltpu.SemaphoreType.DMA((n,))), start, finish)


AXIS = dict(BIG)


def _layer_parts(i):
    j = i // 2
    mixer = (("att_w_qkv", j), ("att_w_o", j)) if i % 2 == 0 else (("hgrn_w_in", j), ("hgrn_w_o", j))
    return mixer + (("ffn_w_in", i), ("ffn_w_out", i), ("ple_w_gate", i), ("ple_w_proj", i))


def _gather_groups():
    first = _layer_parts(0)
    return [first[:1], first[1:] + _layer_parts(1), _layer_parts(2), _layer_parts(3)]


def _pack_parts(shards, parts):
    return jnp.concatenate([shards[n][l].reshape(-1, PACK_W) for n, l in parts], axis=0)


def _gathered_parts(buf, parts, shapes):
    out, r0 = {}, 0
    for n, l in parts:
        r, c = shapes[n][1:]
        nr = r * c // PACK_W
        sh = buf[:, r0:r0 + nr].reshape(N_CHIPS, r, c)
        out[(n, l)] = sh.reshape(N_CHIPS * r, c) if AXIS[n] == 1 else sh.transpose(1, 0, 2).reshape(r, N_CHIPS * c)
        r0 += nr
    return out, r0


def _pack_layer(shards, i):
    return _pack_parts(shards, _layer_parts(i))


def _unpack_layer(buf, i, shapes):
    out, r0 = {}, 0
    for n, _ in _layer_parts(i):
        r, c = shapes[n][1:]
        nr = r * c // PACK_W
        out[n] = buf[r0:r0 + nr].reshape(r, c)
        r0 += nr
    return out


def _gathered_layer(buf, i, shapes):
    out, r0 = {}, 0
    for n, _ in _layer_parts(i):
        r, c = shapes[n][1:]
        nr = r * c // PACK_W
        sh = buf[:, r0:r0 + nr].reshape(N_CHIPS, r, c)
        out[n] = sh.reshape(N_CHIPS * r, c) if AXIS[n] == 1 else sh.transpose(1, 0, 2).reshape(r, N_CHIPS * c)
        r0 += nr
    return out, r0


def _layer_slabs(full, i, shapes):
    parts = []
    for n, _ in _layer_parts(i):
        r, c = shapes[n][1:]
        g = full[n]
        g = g.reshape(N_CHIPS, -1, PACK_W) if AXIS[n] == 1 else g.reshape(r, N_CHIPS, c).transpose(1, 0, 2).reshape(N_CHIPS, -1, PACK_W)
        parts.append(g)
    return jnp.concatenate(parts, axis=1)


def kernel(x, p, att_w_qkv, att_sink, att_w_o, hgrn_w_in, hgrn_lb_logits, hgrn_norm_g, hgrn_w_o, ln_mix_g, ln_mix_b, ffn_w_in, ffn_w_out, ln_ffn_g, ln_ffn_b, ple_w_gate, ple_w_proj, loss_target, m_att_w_qkv, m_att_sink, m_att_w_o, m_hgrn_w_in, m_hgrn_lb_logits, m_hgrn_norm_g, m_hgrn_w_o, m_ln_mix_g, m_ln_mix_b, m_ffn_w_in, m_ffn_w_out, m_ln_ffn_g, m_ln_ffn_b, m_ple_w_gate, m_ple_w_proj, v_att_w_qkv, v_att_sink, v_att_w_o, v_hgrn_w_in, v_hgrn_lb_logits, v_hgrn_norm_g, v_hgrn_w_o, v_ln_mix_g, v_ln_mix_b, v_ffn_w_in, v_ffn_w_out, v_ln_ffn_g, v_ln_ffn_b, v_ple_w_gate, v_ple_w_proj):
    wts = dict(att_w_qkv=att_w_qkv, att_sink=att_sink, att_w_o=att_w_o, hgrn_w_in=hgrn_w_in, hgrn_lb_logits=hgrn_lb_logits,
               hgrn_norm_g=hgrn_norm_g, hgrn_w_o=hgrn_w_o, ln_mix_g=ln_mix_g, ln_mix_b=ln_mix_b, ffn_w_in=ffn_w_in,
               ffn_w_out=ffn_w_out, ln_ffn_g=ln_ffn_g, ln_ffn_b=ln_ffn_b, ple_w_gate=ple_w_gate, ple_w_proj=ple_w_proj)
    mom = dict(att_w_qkv=m_att_w_qkv, att_sink=m_att_sink, att_w_o=m_att_w_o, hgrn_w_in=m_hgrn_w_in, hgrn_lb_logits=m_hgrn_lb_logits,
               hgrn_norm_g=m_hgrn_norm_g, hgrn_w_o=m_hgrn_w_o, ln_mix_g=m_ln_mix_g, ln_mix_b=m_ln_mix_b, ffn_w_in=m_ffn_w_in,
               ffn_w_out=m_ffn_w_out, ln_ffn_g=m_ln_ffn_g, ln_ffn_b=m_ln_ffn_b, ple_w_gate=m_ple_w_gate, ple_w_proj=m_ple_w_proj)
    var = dict(att_w_qkv=v_att_w_qkv, att_sink=v_att_sink, att_w_o=v_att_w_o, hgrn_w_in=v_hgrn_w_in, hgrn_lb_logits=v_hgrn_lb_logits,
               hgrn_norm_g=v_hgrn_norm_g, hgrn_w_o=v_hgrn_w_o, ln_mix_g=v_ln_mix_g, ln_mix_b=v_ln_mix_b, ffn_w_in=v_ffn_w_in,
               ffn_w_out=v_ffn_w_out, ln_ffn_g=v_ln_ffn_g, ln_ffn_b=v_ln_ffn_b, ple_w_gate=v_ple_w_gate, ple_w_proj=v_ple_w_proj)
    shapes = {n: wts[n].shape for n, _ in BIG}
    chip = 2 * lax.axis_index("x") + lax.axis_index("y")
    core = lax.axis_index("c").reshape(1).astype(jnp.int32)
    xin, target = x[0], loss_target[0]
    t = xin.shape[0]
    row = lambda a, i: a[i][None]

    bf_shards = {n: wts[n].astype(BF16) for n, _ in BIG}
    lb_sh = hgrn_lb_logits.shape
    lb_bits = lax.bitcast_convert_type(hgrn_lb_logits.reshape(-1), BF16).reshape(-1, PACK_W)
    groups = _gather_groups()
    packed = [_pack_parts(bf_shards, parts) for parts in groups]
    packed[0] = jnp.concatenate([packed[0], lb_bits, jnp.zeros((LB_ROWS - lb_bits.shape[0], PACK_W), BF16)], axis=0)

    gathered = _run_guest(_gather_guest(packed[0]), name="gather_first")[0]
    wfull, r_big = _gathered_parts(gathered, groups[0], shapes)
    lb_rows = gathered[:, r_big:r_big + lb_bits.shape[0]].reshape(N_CHIPS, -1, 2)
    lb_full = lax.bitcast_convert_type(lb_rows, F32).reshape(N_CHIPS, lb_sh[0], lb_sh[1], lb_sh[2])
    lb_full = lb_full.transpose(1, 2, 0, 3).reshape(lb_sh[0], lb_sh[1], N_CHIPS * lb_sh[2])
    logits2d = lb_full.reshape(DEPTH, 2 * D_MODEL)
    lb_all = _lower_bounds(logits2d, name="lb_fwd").reshape(DEPTH, 2, 1, D_MODEL)
    tables = _rope_tables(t)

    saved, weights = [], []
    xf, xb = xin, xin.astype(BF16)
    for i in range(DEPTH):
        j = i // 2
        nxt = _gather_guest(packed[i + 1]) if i + 1 < DEPTH else None
        s = dict(xin_bf=xb)
        if i % 2 == 0:
            q, k, v = _qkv_rope(xb, wfull[("att_w_qkv", j)], tables, name=f"qkv_rope_{i}")
            (o, lse), got = _attn_fwd(q, k, v, att_sink[j], name=f"attn_fwd_{i}", guest=nxt)
            s.update(q=q, k=k, v=v, o=o, lse=lse)
            mix_in = o
        else:
            z = _mm_nn(xb, wfull[("hgrn_w_in", j)], out_dtype=F32, name=f"hgrn_in_{i}")
            (o_f, s_f), _ = _hgrn_fwd(z, lb_all[i, 0], False, name=f"hgrn_scan_f_{i}")
            (o_b, s_b), _ = _hgrn_fwd(z, lb_all[i, 1], True, name=f"hgrn_scan_b_{i}")
            og = _hgrn_post_fwd(o_f, o_b, z, row(hgrn_norm_g, j), name=f"hgrn_post_{i}")
            s.update(z=z, o_f=o_f, o_b=o_b, s_f=s_f, s_b=s_b, og=og)
            mix_in = og
        in_mixer = nxt is not None and i % 2 == 0
        if in_mixer:
            wfull.update(_gathered_parts(got[0], groups[i + 1], shapes)[0])
        w = {n: wfull[(n, l)] for n, l in _layer_parts(i)}
        w_o = w["att_w_o" if i % 2 == 0 else "hgrn_w_o"]
        x1, x1b, xh1, rs1 = _mm_res_ln(mix_in, w_o, xf, row(ln_mix_g, i), row(ln_mix_b, i), name=f"mix_out_ln_{i}")
        (g, u, h), got = _ffn_in(x1b, w["ffn_w_in"], name=f"ffn_in_{i}", guest=None if in_mixer else nxt)
        if nxt is not None and not in_mixer:
            wfull.update(_gathered_parts(got[0], groups[i + 1], shapes)[0])
        x2, x2b, xh2, rs2 = _mm_res_ln(h, w["ffn_w_out"], x1, row(ln_ffn_g, i), row(ln_ffn_b, i), name=f"ffn_out_ln_{i}")
        xf, xb, gate, pp = _ple_fwd(x2, x2b, p, i, w["ple_w_gate"], w["ple_w_proj"], name=f"ple_fwd_{i}")
        s.update(x1b=x1b, xh1=xh1, rs1=rs1, g=g, u=u, h=h, x2b=x2b, xh2=xh2, rs2=rs2, gate=gate, pp=pp)
        saved.append(s)
        weights.append(w)

    loss, dx = _loss_head(xf, target, name="loss_head")
    loss = lax.psum(loss[0, 0], ("x", "y", "c"))

    gs = {n: [None] * DEPTH for n in SMALL}
    gs.update(att_sink=[None] * 2, hgrn_norm_g=[None] * 2)
    dlb = [jnp.zeros((2, D_MODEL), F32)] * DEPTH
    halves = [None] * DEPTH
    slabs = None
    for i in reversed(range(DEPTH)):
        j = i // 2
        s, w = saved[i], weights[i]
        gw = {}
        res, got = _ple_bwd(dx, s["gate"], s["pp"], w["ple_w_gate"], s["xh2"], s["rs2"], row(ln_ffn_g, i), name=f"ple_bwd_{i}",
                            guest=None if slabs is None else _pair_exchange_guest(slabs))
        du2, du2b, dpre, dpp, gs["ln_ffn_g"][i], gs["ln_ffn_b"][i] = res
        part = None if slabs is None else _add_own_half(slabs, got[0], core, name=f"grad_pair_add_{i + 1}")
        gw["ple_w_gate"] = _mm_tn(s["x2b"], dpre, name=f"dw_ple_gate_{i}")
        gw["ple_w_proj"] = _mm_tn_p(p, i, dpp, name=f"dw_ple_proj_{i}")
        dgg, dgu = _ffn_out_bwd(du2b, w["ffn_w_out"], s["g"], s["u"], name=f"ffn_out_bwd_{i}")
        gw["ffn_w_out"] = _mm_tn(s["h"], du2b, name=f"dw_ffn_out_{i}")
        res = _mm_nt([dgg, dgu], w["ffn_w_in"], resid=du2, ln=(s["xh1"], s["rs1"], row(ln_mix_g, i)),
                     name=f"ffn_in_bwd_{i}", guest=None if part is None else _chip_scatter_guest(part))
        (du1, du1b, gs["ln_mix_g"][i], gs["ln_mix_b"][i]), got = res if part is not None else (res, None)
        if part is not None:
            halves[i + 1] = _sum_chips(got[0], name=f"grad_chip_sum_{i + 1}")
        gw["ffn_w_in"] = jnp.concatenate(
            [_mm_tn(s["x1b"], dgg, name=f"dw_ffn_gate_{i}"), _mm_tn(s["x1b"], dgu, name=f"dw_ffn_up_{i}")], axis=1)
        if i % 2 == 0:
            gw["att_w_o"] = _mm_tn(s["o"], du1b, name=f"dw_att_o_{i}")
            do = _mm_nt([du1b], w["att_w_o"], out_dtype=BF16, name=f"att_o_bwd_{i}")
            dq, dkw, dvw, gs["att_sink"][j] = _attn_bwd(s["q"], s["k"], s["v"], s["o"], do, s["lse"], att_sink[j], name=f"attn_bwd_{i}")
            dqkv = _attn_post_bwd(dq, dkw, dvw, tables, name=f"attn_post_bwd_{i}")
            gw["att_w_qkv"] = _mm_tn(s["xin_bf"], dqkv, name=f"dw_att_qkv_{i}")
            dx = _mm_nt([dqkv], w["att_w_qkv"], resid=du1, name=f"qkv_bwd_{i}")
        else:
            gw["hgrn_w_o"] = _mm_tn(s["og"], du1b, name=f"dw_hgrn_o_{i}")
            dy = _mm_nt([du1b], w["hgrn_w_o"], out_dtype=BF16, name=f"hgrn_o_bwd_{i}")
            d_o, dgate, gs["hgrn_norm_g"][j] = _hgrn_post_bwd(dy, s["o_f"], s["o_b"], s["z"], row(hgrn_norm_g, j), name=f"hgrn_post_bwd_{i}")
            dq_f, dzf_f, dv_f, dlb_f = _hgrn_bwd(s["z"], lb_all[i, 0], d_o, s["s_f"], False, name=f"hgrn_scan_f_bwd_{i}")
            dq, dzf_b, dv, dlb_b = _hgrn_bwd(s["z"], lb_all[i, 1], d_o, s["s_b"], True, name=f"hgrn_scan_b_bwd_{i}",
                                             other=(dq_f, dv_f))
            dlb[i] = jnp.stack([dlb_f.reshape(D_MODEL), dlb_b.reshape(D_MODEL)])
            dz = [dq, dzf_f, dzf_b, dv, dgate]
            gw["hgrn_w_in"] = jnp.concatenate(
                [_mm_tn(s["xin_bf"], part, name=f"dw_hgrn_in_{i}_{n}") for n, part in enumerate(dz)], axis=1)
            dx = _mm_nt(dz, w["hgrn_w_in"], resid=du1, name=f"hgrn_in_bwd_{i}")
        slabs = _layer_slabs(gw, i, shapes)

    from_sibling = _run_guest(_pair_exchange_guest(slabs), name="grad_pair_exchange_0")[0]
    part = _add_own_half(slabs, from_sibling, core, name="grad_pair_add_0")
    halves[0] = _sum_chips(_run_guest(_chip_scatter_guest(part), name="grad_chip_scatter_0")[0], name="grad_chip_sum_0")
    reduced = _run_guest(_pair_share_guest(halves), name="grad_pair_share")

    g_layers = [_unpack_layer(reduced[i].reshape(-1, PACK_W), i, shapes) for i in range(DEPTH)]
    grads = {}
    for n, _ in BIG:
        per_layer = [g_layers[i][n] for i in range(DEPTH) if n in g_layers[i]]
        grads[n] = jnp.stack(per_layer)

    gsmall = {n: jnp.concatenate(v, axis=0) for n, v in gs.items()}
    dlogits = _lower_bounds_bwd(logits2d, jnp.stack(dlb).reshape(DEPTH, 2 * D_MODEL), name="lb_bwd").reshape(DEPTH, 2, D_MODEL)
    g_small_full = _unpack_small(_allreduce_small(_pack_small(gsmall, dlogits), name="allreduce_small"),
                                 (lb_sh[0], lb_sh[1], N_CHIPS * lb_sh[2]))
    grads.update({n: g_small_full[n] for n in SMALL + ("att_sink", "hgrn_norm_g")})
    grads["hgrn_lb_logits"] = lax.dynamic_slice_in_dim(g_small_full["hgrn_lb_logits"], chip * lb_sh[2], lb_sh[2], axis=2)

    delta, new_m, new_v = {}, {}, {}
    for n, _ in BIG:
        two_d = lambda a: a.reshape(-1, a.shape[-1])
        d, nm, nv = _adamw(two_d(wts[n]), two_d(grads[n]), two_d(mom[n]), two_d(var[n]), name=f"adamw_{n}")
        delta[n], new_m[n], new_v[n] = d.reshape(shapes[n]), nm.reshape(shapes[n]), nv.reshape(shapes[n])
    packs = [_pack_small(src, src["hgrn_lb_logits"]) for src in (wts, grads, mom, var)]
    outs = _adamw(*packs, name="adamw_small")
    for dst, buf in zip((delta, new_m, new_v), outs):
        dst.update(_unpack_small(buf, lb_sh))

    return (loss, dx[None], *[grads[n] for n in WEIGHTS], *[delta[n] for n in WEIGHTS],
            *[new_m[n] for n in WEIGHTS], *[new_v[n] for n in WEIGHTS])
```

```python
import functools
from typing import Callable, NamedTuple

import jax
import jax.numpy as jnp
from jax import lax
from jax.experimental import pallas as pl
from jax.experimental.pallas import tpu as pltpu

F32, BF16 = jnp.float32, jnp.bfloat16

D_MODEL = 1024
DEPTH = 4
HEAD_DIM = 64
N_Q_HEADS = 16
N_KV_HEADS = 4
GROUP = 4
Q_DIM = 1024
KV_DIM = 256
WINDOW = 128
ROPE_DIM = 16
ROPE_THETA = 500000.0
HGRN_HEADS = 8
HGRN_KEY = 128
HGRN_CHUNK = 64
D_FF = 2816
PLE_DIM = 256
ALPHA = (2 * DEPTH) ** 0.25
LN_EPS = 1e-5
ADAM_LR, ADAM_B1, ADAM_B2, ADAM_EPS, ADAM_WD, ADAM_STEP = 0.001, 0.9, 0.999, 1e-08, 0.01, 10

LANES = 128
VMEM_LIMIT_BYTES = 56 * 2**20
FACTORED_DECAY_LIMIT = -80.0

NN = ((1,), (0,))
NT = ((1,), (1,))
TN = ((0,), (0,))

N_CHIPS = 4
MESH = pl.DeviceIdType.MESH


def _dot(a, b, dims):
    return lax.dot_general(a, b, (dims, ((), ())), preferred_element_type=F32)


def _cp(*sem):
    return pltpu.CompilerParams(dimension_semantics=sem, vmem_limit_bytes=VMEM_LIMIT_BYTES)


def _rows(t, cap=512):
    return min(cap, t)


def _pick(n, cap):
    best = None
    for d in range(LANES, min(n, cap) + 1, LANES):
        if n % d == 0:
            best = d
    assert best is not None, (n, cap)
    return best


def _sigmoid(x):
    return jax.nn.sigmoid(x)


def _ln_fwd(u, g, b):
    mu = jnp.mean(u, axis=-1, keepdims=True)
    xc = u - mu
    var = jnp.mean(xc * xc, axis=-1, keepdims=True)
    rstd = lax.rsqrt(var + LN_EPS)
    xhat = xc * rstd
    return xhat * g + b, xhat, rstd


def _ln_bwd(dy, xhat, rstd, g):
    dxh = dy * g
    m1 = jnp.mean(dxh, axis=-1, keepdims=True)
    m2 = jnp.mean(dxh * xhat, axis=-1, keepdims=True)
    du = rstd * (dxh - m1 - xhat * m2)
    return du, jnp.sum(dy * xhat, axis=0, keepdims=True), jnp.sum(dy, axis=0, keepdims=True)


def _full(shape):
    nd = len(shape)
    return pl.BlockSpec(shape, lambda *_: (0,) * nd)


ANY = pl.BlockSpec(memory_space=pl.ANY)


class _Guest(NamedTuple):
    args: tuple
    out_shape: tuple
    sems: tuple
    start: Callable
    finish: Callable


def _call(body, *, grid, in_specs, out_specs, out_shape, args, sem, name, scratch_shapes=(), guest=None):
    n_in, n_out, n_scr = len(args), len(out_shape), len(scratch_shapes)
    if guest is None:
        res = pl.pallas_call(body, grid=grid, in_specs=list(in_specs), out_specs=list(out_specs), out_shape=list(out_shape),
                             scratch_shapes=list(scratch_shapes), compiler_params=_cp(*sem), name=name)(*args)
        return list(res), []
    g_in, g_out = len(guest.args), len(guest.out_shape)

    def hosted(*refs):
        cuts = [n_in, g_in, n_out, g_out, n_scr]
        parts, pos = [], 0
        for n in cuts:
            parts.append(refs[pos:pos + n])
            pos += n
        h_in, gi, h_out, go, h_scr = parts
        gs = refs[pos:]
        ids = [pl.program_id(d) for d in range(len(grid))]
        first = functools.reduce(jnp.logical_and, [i == 0 for i in ids])
        last = functools.reduce(jnp.logical_and, [i == n - 1 for i, n in zip(ids, grid)])

        @pl.when(first)
        def _():
            guest.start(gi, go, gs)
        body(*h_in, *h_out, *h_scr)

        @pl.when(last)
        def _():
            guest.finish(gi, go, gs)

    res = pl.pallas_call(
        hosted, grid=grid, in_specs=list(in_specs) + [ANY] * g_in, out_specs=list(out_specs) + [ANY] * g_out,
        out_shape=list(out_shape) + list(guest.out_shape), scratch_shapes=list(scratch_shapes) + list(guest.sems),
        compiler_params=_cp(*(("arbitrary",) * len(grid))), name=name)(*args, *guest.args)
    return list(res[:n_out]), list(res[n_out:])


def _run_guest(guest, *, name):
    g_in = len(guest.args)

    def body(*refs):
        gi, go, gs = refs[:g_in], refs[g_in:g_in + len(guest.out_shape)], refs[g_in + len(guest.out_shape):]
        guest.start(gi, go, gs)
        guest.finish(gi, go, gs)

    return pl.pallas_call(body, in_specs=[ANY] * g_in, out_specs=[ANY] * len(guest.out_shape), out_shape=list(guest.out_shape),
                          scratch_shapes=list(guest.sems), name=name)(*guest.args)


def _mm_nn(a, w, *, out_dtype, name):
    t, k = a.shape
    n = w.shape[1]
    tm, tn = _rows(t), _pick(n, 1408)

    def body(a_ref, w_ref, o_ref):
        o_ref[...] = _dot(a_ref[...], w_ref[...], NN).astype(out_dtype)

    return pl.pallas_call(
        body, grid=(n // tn, t // tm),
        in_specs=[pl.BlockSpec((tm, k), lambda j, i: (i, 0)), pl.BlockSpec((k, tn), lambda j, i: (0, j))],
        out_specs=pl.BlockSpec((tm, tn), lambda j, i: (i, j)),
        out_shape=jax.ShapeDtypeStruct((t, n), out_dtype),
        compiler_params=_cp("parallel", "parallel"), name=name)(a, w)


def _mm_tn(a, b, *, name, guest=None):
    r, m = a.shape
    n = b.shape[1]
    tr, tm, tn = _rows(r), _pick(m, 1408), _pick(n, 2816)

    def body(a_ref, b_ref, o_ref):
        @pl.when(pl.program_id(2) == 0)
        def _():
            o_ref[...] = jnp.zeros_like(o_ref)
        o_ref[...] += _dot(a_ref[...].astype(BF16), b_ref[...].astype(BF16), TN)

    res, extra = _call(
        body, grid=(m // tm, n // tn, r // tr),
        in_specs=[pl.BlockSpec((tr, tm), lambda i, j, k: (k, i)), pl.BlockSpec((tr, tn), lambda i, j, k: (k, j))],
        out_specs=[pl.BlockSpec((tm, tn), lambda i, j, k: (i, j))],
        out_shape=[jax.ShapeDtypeStruct((m, n), F32)], args=(a, b),
        sem=("parallel", "parallel", "arbitrary"), name=name, guest=guest)
    return res[0] if guest is None else (res[0], extra)


def _mm_tn_p(p, layer, b, *, name):
    t = p.shape[2]
    n = b.shape[1]
    tr = _rows(t)

    def body(a_ref, b_ref, o_ref):
        @pl.when(pl.program_id(0) == 0)
        def _():
            o_ref[...] = jnp.zeros_like(o_ref)
        o_ref[...] += _dot(a_ref[...].astype(BF16), b_ref[...], TN)

    return pl.pallas_call(
        body, grid=(t // tr,),
        in_specs=[pl.BlockSpec((None, None, tr, PLE_DIM), lambda k: (layer, 0, k, 0)),
                  pl.BlockSpec((tr, n), lambda k: (k, 0))],
        out_specs=pl.BlockSpec((PLE_DIM, n), lambda k: (0, 0)),
        out_shape=jax.ShapeDtypeStruct((PLE_DIM, n), F32),
        compiler_params=_cp("arbitrary"), name=name)(p, b)


def _mm_res_ln(a, w, resid, g, b, *, name):
    t, k = a.shape
    tm = _rows(t)

    def body(a_ref, w_ref, r_ref, g_ref, b_ref, y_ref, ybf_ref, xh_ref, rs_ref):
        acc = _dot(a_ref[...], w_ref[...], NN)
        y, xh, rs = _ln_fwd(ALPHA * r_ref[...] + acc, g_ref[...], b_ref[...])
        y_ref[...] = y
        ybf_ref[...] = y.astype(BF16)
        xh_ref[...] = xh
        rs_ref[...] = rs

    row = pl.BlockSpec((tm, D_MODEL), lambda i: (i, 0))
    return pl.pallas_call(
        body, grid=(t // tm,),
        in_specs=[pl.BlockSpec((tm, k), lambda i: (i, 0)), _full((k, D_MODEL)), row, _full((1, D_MODEL)), _full((1, D_MODEL))],
        out_specs=[row, row, row, pl.BlockSpec((tm, 1), lambda i: (i, 0))],
        out_shape=[jax.ShapeDtypeStruct((t, D_MODEL), F32), jax.ShapeDtypeStruct((t, D_MODEL), BF16),
                   jax.ShapeDtypeStruct((t, D_MODEL), F32), jax.ShapeDtypeStruct((t, 1), F32)],
        compiler_params=_cp("parallel"), name=name)(a, w, resid, g, b)


def _ffn_in(x_bf, w, *, name, guest=None):
    t = x_bf.shape[0]
    tm, tn = _rows(t), _pick(D_FF, 1408)
    nb = D_FF // tn

    def body(x_ref, wg_ref, wu_ref, dg_ref, du_ref, h_ref):
        x = x_ref[...]
        g = _dot(x, wg_ref[...], NN)
        u = _dot(x, wu_ref[...], NN)
        sig = _sigmoid(g)
        silu = g * sig
        dg_ref[...] = (u * sig * (1.0 + g * (1.0 - sig))).astype(BF16)
        du_ref[...] = silu.astype(BF16)
        h_ref[...] = (silu * u).astype(BF16)

    tile = pl.BlockSpec((tm, tn), lambda j, i: (i, j))
    shp = jax.ShapeDtypeStruct((t, D_FF), BF16)
    return _call(
        body, grid=(nb, t // tm),
        in_specs=[pl.BlockSpec((tm, D_MODEL), lambda j, i: (i, 0)),
                  pl.BlockSpec((D_MODEL, tn), lambda j, i: (0, j)),
                  pl.BlockSpec((D_MODEL, tn), lambda j, i: (0, j + nb))],
        out_specs=[tile, tile, tile], out_shape=[shp, shp, shp], args=(x_bf, w, w),
        sem=("parallel", "parallel"), name=name, guest=guest)


def _ple_fwd(x, x_bf, p, layer, wg, wp, *, name):
    t = x.shape[0]
    tm = _rows(t)

    def body(x_ref, xbf_ref, p_ref, wg_ref, wp_ref, y_ref, ybf_ref, gate_ref, pp_ref):
        gate = _sigmoid(_dot(xbf_ref[...], wg_ref[...], NN))
        pp = _dot(p_ref[...].astype(BF16), wp_ref[...], NN)
        y = x_ref[...] + gate * pp
        y_ref[...] = y
        ybf_ref[...] = y.astype(BF16)
        gate_ref[...] = gate.astype(BF16)
        pp_ref[...] = pp.astype(BF16)

    row = pl.BlockSpec((tm, D_MODEL), lambda i: (i, 0))
    f32, bf = jax.ShapeDtypeStruct((t, D_MODEL), F32), jax.ShapeDtypeStruct((t, D_MODEL), BF16)
    return pl.pallas_call(
        body, grid=(t // tm,),
        in_specs=[row, row, pl.BlockSpec((None, None, tm, PLE_DIM), lambda i: (layer, 0, i, 0)),
                  _full((D_MODEL, D_MODEL)), _full((PLE_DIM, D_MODEL))],
        out_specs=[row, row, row, row], out_shape=[f32, bf, bf, bf],
        compiler_params=_cp("parallel"), name=name)(x, x_bf, p, wg, wp)


def _loss_head(y, target, *, name):
    t = y.shape[0]
    tm = _rows(t)

    def body(y_ref, t_ref, loss_ref, dy_ref):
        e = y_ref[...] - t_ref[...]

        @pl.when(pl.program_id(0) == 0)
        def _():
            loss_ref[...] = jnp.zeros_like(loss_ref)
        sq = jnp.sum(jnp.sum(e * e, axis=1, keepdims=True), axis=0, keepdims=True)
        loss_ref[...] += sq * (0.5 / D_MODEL)
        dy_ref[...] = e * (1.0 / D_MODEL)

    row = pl.BlockSpec((tm, D_MODEL), lambda i: (i, 0))
    return pl.pallas_call(
        body, grid=(t // tm,), in_specs=[row, row],
        out_specs=[_full((1, 1)), row],
        out_shape=[jax.ShapeDtypeStruct((1, 1), F32), jax.ShapeDtypeStruct((t, D_MODEL), F32)],
        compiler_params=_cp("arbitrary"), name=name)(y, target)


def _ple_bwd(dx3, gate, pp, wg, xhat, rstd, g, *, name, guest=None):
    t = dx3.shape[0]
    tm = _rows(t)

    def body(dx_ref, gate_ref, pp_ref, wg_ref, xh_ref, rs_ref, g_ref,
             du_ref, dubf_ref, dpre_ref, dpp_ref, dg_ref, db_ref):
        dx = dx_ref[...]
        gate = gate_ref[...].astype(F32)
        dpre = (dx * pp_ref[...].astype(F32) * gate * (1.0 - gate)).astype(BF16)
        dpre_ref[...] = dpre
        dpp_ref[...] = (dx * gate).astype(BF16)
        dx2 = dx + _dot(dpre, wg_ref[...], NT)
        du, dg, db = _ln_bwd(dx2, xh_ref[...], rs_ref[...], g_ref[...])
        du_ref[...] = du
        dubf_ref[...] = du.astype(BF16)

        @pl.when(pl.program_id(0) == 0)
        def _():
            dg_ref[...] = jnp.zeros_like(dg_ref)
            db_ref[...] = jnp.zeros_like(db_ref)
        dg_ref[...] += dg
        db_ref[...] += db

    row = pl.BlockSpec((tm, D_MODEL), lambda i: (i, 0))
    vec = _full((1, D_MODEL))
    f32, bf = jax.ShapeDtypeStruct((t, D_MODEL), F32), jax.ShapeDtypeStruct((t, D_MODEL), BF16)
    v32 = jax.ShapeDtypeStruct((1, D_MODEL), F32)
    res, extra = _call(
        body, grid=(t // tm,),
        in_specs=[row, row, row, _full((D_MODEL, D_MODEL)), row, pl.BlockSpec((tm, 1), lambda i: (i, 0)), vec],
        out_specs=[row, row, row, row, vec, vec], out_shape=[f32, bf, bf, bf, v32, v32],
        args=(dx3, gate, pp, wg, xhat, rstd, g), sem=("arbitrary",), name=name, guest=guest)
    return res, extra


def _ffn_out_bwd(du_bf, w_out, g, u, *, name):
    t = du_bf.shape[0]
    tm, tn = _rows(t), _pick(D_FF, 1408)

    def body(du_ref, w_ref, hg_ref, hu_ref, dg_ref, dup_ref):
        dh = _dot(du_ref[...], w_ref[...], NT)
        dg_ref[...] = (dh * hg_ref[...].astype(F32)).astype(BF16)
        dup_ref[...] = (dh * hu_ref[...].astype(F32)).astype(BF16)

    tile = pl.BlockSpec((tm, tn), lambda j, i: (i, j))
    shp = jax.ShapeDtypeStruct((t, D_FF), BF16)
    return pl.pallas_call(
        body, grid=(D_FF // tn, t // tm),
        in_specs=[pl.BlockSpec((tm, D_MODEL), lambda j, i: (i, 0)), pl.BlockSpec((tn, D_MODEL), lambda j, i: (j, 0)),
                  tile, tile],
        out_specs=[tile, tile], out_shape=[shp, shp],
        compiler_params=_cp("parallel", "parallel"), name=name)(du_bf, w_out, g, u)


def _mm_nt(parts, w, *, resid=None, ln=None, out_dtype=F32, name, guest=None):
    t, kp = parts[0].shape
    npart = len(parts)
    tm = _rows(t)
    n_in = npart + 1 + (resid is not None) + (3 if ln is not None else 0)

    def body(*refs):
        a_refs, w_ref = refs[:npart], refs[npart]
        pos = npart + 1
        r_ref = None
        if resid is not None:
            r_ref = refs[pos]
            pos += 1
        if ln is not None:
            xh_ref, rs_ref, g_ref = refs[pos:pos + 3]
        outs = refs[n_in:]
        val = _dot(a_refs[0][...], w_ref[:, :kp], NT)
        for pi in range(1, npart):
            val = val + _dot(a_refs[pi][...], w_ref[:, pi * kp:(pi + 1) * kp], NT)
        if r_ref is not None:
            val = val + ALPHA * r_ref[...]
        if ln is None:
            outs[0][...] = val.astype(out_dtype)
        else:
            du, dg, db = _ln_bwd(val, xh_ref[...], rs_ref[...], g_ref[...])
            outs[0][...] = du
            outs[1][...] = du.astype(BF16)

            @pl.when(pl.program_id(0) == 0)
            def _():
                outs[2][...] = jnp.zeros_like(outs[2])
                outs[3][...] = jnp.zeros_like(outs[3])
            outs[2][...] += dg
            outs[3][...] += db

    row = pl.BlockSpec((tm, D_MODEL), lambda i: (i, 0))
    vec = pl.BlockSpec((1, D_MODEL), lambda i: (0, 0))
    in_specs = [pl.BlockSpec((tm, kp), lambda i: (i, 0)) for _ in range(npart)]
    in_specs.append(pl.BlockSpec((D_MODEL, npart * kp), lambda i: (0, 0), pipeline_mode=pl.Buffered(1)))
    args = list(parts) + [w]
    if resid is not None:
        in_specs.append(row)
        args.append(resid)
    if ln is not None:
        in_specs += [row, pl.BlockSpec((tm, 1), lambda i: (i, 0)), vec]
        args += list(ln)
        out_specs = [row, row, vec, vec]
        out_shape = [jax.ShapeDtypeStruct((t, D_MODEL), F32), jax.ShapeDtypeStruct((t, D_MODEL), BF16),
                     jax.ShapeDtypeStruct((1, D_MODEL), F32), jax.ShapeDtypeStruct((1, D_MODEL), F32)]
    else:
        out_specs = [row]
        out_shape = [jax.ShapeDtypeStruct((t, D_MODEL), out_dtype)]
    res, extra = _call(
        body, grid=(t // tm,), in_specs=in_specs, out_specs=out_specs, out_shape=out_shape, args=tuple(args),
        sem=("arbitrary" if ln is not None else "parallel",), name=name, guest=guest)
    res = res if ln is not None else res[0]
    return res if guest is None else (res, extra)


def _rope_tables(t):
    half = ROPE_DIM // 2
    inv = ROPE_THETA ** (-jnp.arange(0, ROPE_DIM, 2, dtype=F32) / ROPE_DIM)
    ang = jnp.arange(t, dtype=F32)[:, None] * inv[None, :]
    cos, sin = jnp.cos(ang), jnp.sin(ang)
    pad = HEAD_DIM - ROPE_DIM
    c = jnp.concatenate([cos, cos, jnp.ones((t, pad), F32)], axis=1)
    s_hi = jnp.concatenate([jnp.zeros((t, half), F32), sin, jnp.zeros((t, pad), F32)], axis=1)
    s_lo = jnp.concatenate([-sin, jnp.zeros((t, half + pad), F32)], axis=1)
    rep = LANES // HEAD_DIM
    return jnp.tile(c, (1, rep)), jnp.tile(s_hi, (1, rep)), jnp.tile(s_lo, (1, rep))


def _rope(x, c, s_hi, s_lo, sign):
    half = ROPE_DIM // 2
    parts = []
    for j in range(x.shape[1] // LANES):
        xg = x[:, j * LANES:(j + 1) * LANES]
        rot = pltpu.roll(xg, half, 1) * s_hi + pltpu.roll(xg, LANES - half, 1) * s_lo
        parts.append(xg * c + sign * rot)
    return jnp.concatenate(parts, axis=1)


def _qkv_rope(x_bf, w, tables, *, name):
    t = x_bf.shape[0]
    tm = _rows(t)
    n = Q_DIM + 2 * KV_DIM

    def body(x_ref, w_ref, c_ref, sh_ref, sl_ref, q_ref, k_ref, v_ref):
        acc = _dot(x_ref[...], w_ref[...], NN)
        c, sh, sl = c_ref[...], sh_ref[...], sl_ref[...]
        q_ref[...] = (_rope(acc[:, :Q_DIM], c, sh, sl, 1.0) * HEAD_DIM ** -0.5).astype(BF16)
        k_ref[...] = _rope(acc[:, Q_DIM:Q_DIM + KV_DIM], c, sh, sl, 1.0).astype(BF16)
        v_ref[...] = acc[:, Q_DIM + KV_DIM:].astype(BF16)

    tab = pl.BlockSpec((tm, LANES), lambda i: (i, 0))
    return pl.pallas_call(
        body, grid=(t // tm,),
        in_specs=[pl.BlockSpec((tm, D_MODEL), lambda i: (i, 0)), _full((D_MODEL, n)), tab, tab, tab],
        out_specs=[pl.BlockSpec((tm, Q_DIM), lambda i: (i, 0)), pl.BlockSpec((tm, KV_DIM), lambda i: (i, 0)),
                   pl.BlockSpec((tm, KV_DIM), lambda i: (i, 0))],
        out_shape=[jax.ShapeDtypeStruct((t, Q_DIM), BF16), jax.ShapeDtypeStruct((t, KV_DIM), BF16),
                   jax.ShapeDtypeStruct((t, KV_DIM), BF16)],
        compiler_params=_cp("parallel"), name=name)(x_bf, w, *tables)


ATT_ROWS = 256
ATT_STRIP = 64


def _window_specs(t, tq):
    r = tq // WINDOW
    last = t // WINDOW - 1
    return [pl.BlockSpec((WINDOW, KV_DIM), lambda i: (jnp.maximum(i * r - 1, 0), 0)),
            pl.BlockSpec((tq, KV_DIM), lambda i: (i, 0)),
            pl.BlockSpec((WINDOW, KV_DIM), lambda i: (jnp.minimum((i + 1) * r, last), 0))]


def _att_valid(base, t):
    rows = GROUP * WINDOW
    qpos = base + (lax.broadcasted_iota(jnp.int32, (rows, 3 * WINDOW), 0) & (WINDOW - 1))
    kpos = base - WINDOW + lax.broadcasted_iota(jnp.int32, (rows, 3 * WINDOW), 1)
    return (jnp.abs(qpos - kpos) <= WINDOW) & (kpos >= 0) & (kpos < t)


def _stack_heads(x, r0, g):
    return jnp.concatenate(
        [x[r0:r0 + WINDOW, (GROUP * g + h) * HEAD_DIM:(GROUP * g + h + 1) * HEAD_DIM] for h in range(GROUP)], axis=0)


def _sink_column(sink_ref, g):
    return jnp.concatenate([jnp.full((WINDOW, 1), sink_ref[GROUP * g + h], F32) for h in range(GROUP)], axis=0)


def _unstack_heads(pieces):
    return jnp.concatenate([pieces[g][h * WINDOW:(h + 1) * WINDOW] for g in range(N_KV_HEADS) for h in range(GROUP)], axis=1)


def _attn_fwd(q, k, v, sink, *, name, guest=None):
    t = q.shape[0]
    tq = min(ATT_ROWS, t)
    nsb = tq // WINDOW

    def body(sink_ref, q_ref, kp_ref, kc_ref, kn_ref, vp_ref, vc_ref, vn_ref, o_ref, l_ref):
        i = pl.program_id(0)
        qv = q_ref[...]
        kw = jnp.concatenate([kp_ref[...], kc_ref[...], kn_ref[...]], axis=0)
        vw = jnp.concatenate([vp_ref[...], vc_ref[...], vn_ref[...]], axis=0)
        ones = jnp.ones((3 * WINDOW, HEAD_DIM), BF16)
        for sb in range(nsb):
            r0 = sb * WINDOW
            bias =jnp.where(_att_valid(i * tq + r0, t)[:WINDOW], 0.0, -1e30)
            pieces = []
            for hh in range(N_Q_HEADS):
                g, h = hh // GROUP, hh % GROUP
                qh = qv[r0:r0 + WINDOW, hh * HEAD_DIM:(hh + 1) * HEAD_DIM]
                kg = kw[r0:r0 + 3 * WINDOW, g * HEAD_DIM:(g + 1) * HEAD_DIM]
                vg = jnp.concatenate([vw[r0:r0 + 3 * WINDOW, g * HEAD_DIM:(g + 1) * HEAD_DIM], ones], axis=1)
                s = _dot(qh, kg, NT) + bias
                snk = sink_ref[hh]
                m = jnp.maximum(jnp.max(s, axis=1, keepdims=True), snk)
                ov = _dot(jnp.exp(s - m).astype(BF16), vg, NN)
                den = ov[:, HEAD_DIM:HEAD_DIM + 1] + jnp.exp(snk - m)
                pieces.append(ov[:, :HEAD_DIM] * (1.0 / den))
                l_ref[g, sb, h * WINDOW:(h + 1) * WINDOW, :] = m + jnp.log(den)
            o_ref[r0:r0 + WINDOW, :] = jnp.concatenate(pieces, axis=1).astype(BF16)

    return _call(
        body, grid=(t // tq,),
        in_specs=[pl.BlockSpec(memory_space=pltpu.SMEM), pl.BlockSpec((tq, Q_DIM), lambda i: (i, 0))]
        + _window_specs(t, tq) + _window_specs(t, tq),
        out_specs=[pl.BlockSpec((tq, Q_DIM), lambda i: (i, 0)),
                   pl.BlockSpec((N_KV_HEADS, nsb, GROUP * WINDOW, 1), lambda i: (0, i, 0, 0))],
        out_shape=[jax.ShapeDtypeStruct((t, Q_DIM), BF16),
                   jax.ShapeDtypeStruct((N_KV_HEADS, t // WINDOW, GROUP * WINDOW, 1), F32)],
        args=(sink, q, k, k, k, v, v, v), sem=("parallel",), name=name, guest=guest)


def _attn_bwd(q, k, v, o, do, lse, sink, *, name):
    t = q.shape[0]
    tq = min(ATT_ROWS, t)
    nsb = tq // WINDOW

    def body(sink_ref, q_ref, o_ref, do_ref, l_ref, kp_ref, kc_ref, kn_ref, vp_ref, vc_ref, vn_ref,
             dq_ref, dkw_ref, dvw_ref, dsink_ref):
        i = pl.program_id(0)
        qv, ov, dov = q_ref[...], o_ref[...], do_ref[...]
        kw = jnp.concatenate([kp_ref[...], kc_ref[...], kn_ref[...]], axis=0)
        vw = jnp.concatenate([vp_ref[...], vc_ref[...], vn_ref[...]], axis=0)
        lane16 = lax.broadcasted_iota(jnp.int32, (1, N_Q_HEADS), 1)
        dsink = jnp.zeros((1, N_Q_HEADS), F32)
        for sb in range(nsb):
            r0 = sb * WINDOW
            valid = _att_valid(i * tq + r0, t)
            dq_p, dk_p, dv_p = [], [], []
            for g in range(N_KV_HEADS):
                qs, dos = _stack_heads(qv, r0, g), _stack_heads(dov, r0, g)
                delta = jnp.sum(dos.astype(F32) * _stack_heads(ov, r0, g).astype(F32), axis=1, keepdims=True)
                kg = kw[r0:r0 + 3 * WINDOW, g * HEAD_DIM:(g + 1) * HEAD_DIM]
                vg = vw[r0:r0 + 3 * WINDOW, g * HEAD_DIM:(g + 1) * HEAD_DIM]
                lse_col = l_ref[g, sb]
                pr = jnp.where(valid, jnp.exp(_dot(qs, kg, NT) - lse_col), 0.0)
                ds = (pr * (_dot(dos, vg, NT) - delta)).astype(BF16)
                dq_p.append(_dot(ds, kg, NN))
                dk_p.append(_dot(ds, qs, TN))
                dv_p.append(_dot(pr.astype(BF16), dos, TN))
                ps = jnp.exp(_sink_column(sink_ref, g) - lse_col) * delta
                for h in range(GROUP):
                    tot = jnp.sum(ps[h * WINDOW:(h + 1) * WINDOW], axis=0, keepdims=True)
                    dsink = dsink - jnp.where(lane16 == GROUP * g + h, tot, 0.0)
            dq_ref[r0:r0 + WINDOW, :] = _unstack_heads(dq_p)
            dkw_ref[sb] = jnp.concatenate(dk_p, axis=1)
            dvw_ref[sb] = jnp.concatenate(dv_p, axis=1)

        @pl.when(i == 0)
        def _():
            dsink_ref[...] = jnp.zeros_like(dsink_ref)
        dsink_ref[...] += dsink

    rowq = pl.BlockSpec((tq, Q_DIM), lambda i: (i, 0))
    win = pl.BlockSpec((nsb, 3 * WINDOW, KV_DIM), lambda i: (i, 0, 0))
    wshape = jax.ShapeDtypeStruct((t // WINDOW, 3 * WINDOW, KV_DIM), F32)
    return pl.pallas_call(
        body, grid=(t // tq,),
        in_specs=[pl.BlockSpec(memory_space=pltpu.SMEM), rowq, rowq, rowq,
                  pl.BlockSpec((N_KV_HEADS, nsb, GROUP * WINDOW, 1), lambda i: (0, i, 0, 0))]
        + _window_specs(t, tq) + _window_specs(t, tq),
        out_specs=[rowq, win, win, _full((1, N_Q_HEADS))],
        out_shape=[jax.ShapeDtypeStruct((t, Q_DIM), F32), wshape, wshape, jax.ShapeDtypeStruct((1, N_Q_HEADS), F32)],
        compiler_params=_cp("arbitrary"), name=name)(sink, q, o, do, lse, k, k, k, v, v, v)


def _attn_post_bwd(dq, dkw, dvw, tables, *, name):
    t = dq.shape[0]
    nb = t // WINDOW
    n = Q_DIM + 2 * KV_DIM

    def body(dq_ref, ka_ref, kb_ref, kc_ref, va_ref, vb_ref, vc_ref, c_ref, sh_ref, sl_ref, o_ref):
        j = pl.program_id(0)
        lo = (j > 0).astype(F32)
        hi = (j < nb - 1).astype(F32)
        c, sh, sl = c_ref[...], sh_ref[...], sl_ref[...]
        dk = ka_ref[...] * hi + kb_ref[...] + kc_ref[...] * lo
        dv = va_ref[...] * hi + vb_ref[...] + vc_ref[...] * lo
        o_ref[:, :Q_DIM] = (_rope(dq_ref[...], c, sh, sl, -1.0) * HEAD_DIM ** -0.5).astype(BF16)
        o_ref[:, Q_DIM:Q_DIM + KV_DIM] = _rope(dk, c, sh, sl, -1.0).astype(BF16)
        o_ref[:, Q_DIM + KV_DIM:] = dv.astype(BF16)

    wins = [pl.BlockSpec((None, WINDOW, KV_DIM), lambda j: (jnp.minimum(j + 1, nb - 1), 0, 0)),
            pl.BlockSpec((None, WINDOW, KV_DIM), lambda j: (j, 1, 0)),
            pl.BlockSpec((None, WINDOW, KV_DIM), lambda j: (jnp.maximum(j - 1, 0), 2, 0))]
    tab = pl.BlockSpec((WINDOW, LANES), lambda j: (j, 0))
    return pl.pallas_call(
        body, grid=(nb,),
        in_specs=[pl.BlockSpec((WINDOW, Q_DIM), lambda j: (j, 0))] + wins + wins + [tab, tab, tab],
        out_specs=pl.BlockSpec((WINDOW, n), lambda j: (j, 0)),
        out_shape=jax.ShapeDtypeStruct((t, n), BF16),
        compiler_params=_cp("parallel"), name=name)(dq, dkw, dkw, dkw, dvw, dvw, dvw, *tables)


HGRN_ROWS = 512
HGRN_UNROLL = 2


def _cum_mask(rev, transpose=False):
    row = lax.broadcasted_iota(jnp.int32, (HGRN_CHUNK, HGRN_CHUNK), 0)
    col = lax.broadcasted_iota(jnp.int32, (HGRN_CHUNK, HGRN_CHUNK), 1)
    return (row <= col) if rev != transpose else (row >= col)


def _gates(zq, zf, lb):
    q = zq * _sigmoid(zq)
    sig = _sigmoid(zf)
    f = lb + (1.0 - lb) * sig
    k = (1.0 - lb) * (1.0 - sig)
    return q, sig, f, k


def _intra_exact(q, k, b, mask):
    rows = lax.broadcasted_iota(jnp.int32, (HGRN_CHUNK, 1), 0)
    cols = lax.broadcasted_iota(jnp.int32, (HGRN_CHUNK, HGRN_CHUNK), 1)

    def it(s, a):
        pick = rows == s
        b_s = jnp.sum(jnp.where(pick, b, 0.0), axis=0, keepdims=True)
        k_s = jnp.sum(jnp.where(pick, k, 0.0), axis=0, keepdims=True)
        col = jnp.sum(q * jnp.exp(jnp.minimum(b - b_s, 0.0)) * k_s, axis=1, keepdims=True)
        return jnp.where(cols == s, col, a)

    a = lax.fori_loop(0, HGRN_CHUNK, it, jnp.zeros((HGRN_CHUNK, HGRN_CHUNK), F32))
    return jnp.where(mask, a, 0.0)


def _intra_exact_bwd(q, k, b, da):
    rows = lax.broadcasted_iota(jnp.int32, (HGRN_CHUNK, 1), 0)
    cols = lax.broadcasted_iota(jnp.int32, (HGRN_CHUNK, HGRN_CHUNK), 1)

    def it(s, carry):
        dq, dk = carry
        pick = rows == s
        b_s = jnp.sum(jnp.where(pick, b, 0.0), axis=0, keepdims=True)
        k_s = jnp.sum(jnp.where(pick, k, 0.0), axis=0, keepdims=True)
        w = jnp.sum(jnp.where(cols == s, da, 0.0), axis=1, keepdims=True) * jnp.exp(jnp.minimum(b - b_s, 0.0))
        dq = dq + w * k_s
        dk = jnp.where(pick, jnp.sum(w * q, axis=0, keepdims=True), dk)
        return dq, dk

    z = jnp.zeros((HGRN_CHUNK, HGRN_KEY), F32)
    return lax.fori_loop(0, HGRN_CHUNK, it, (z, z))


def _chunk_cumsum(g, from_end):
    n = g.shape[0]
    row = lax.broadcasted_iota(jnp.int32, g.shape, 0)
    x, s = g, 1
    while s < n:
        if from_end:
            x = x + jnp.where(row < n - s, pltpu.roll(x, n - s, 0), 0.0)
        else:
            x = x + jnp.where(row >= s, pltpu.roll(x, s, 0), 0.0)
        s *= 2
    return x


def _head(a, h):
    return a[:, h * LANES:(h + 1) * LANES]


def _block_is_mild(zf_ref, lb, nch):
    g = jnp.log(lb + (1.0 - lb) * _sigmoid(zf_ref[...]))
    lows = [jnp.min(jnp.sum(g[c * HGRN_CHUNK:(c + 1) * HGRN_CHUNK], axis=0, keepdims=True)) for c in range(nch)]
    return functools.reduce(jnp.minimum, lows) >= FACTORED_DECAY_LIMIT


def _hgrn_fwd(z, lb, rev, *, name, guest=None):
    t = z.shape[0]
    rb = min(HGRN_ROWS, t)
    nb, nch = t // rb, rb // HGRN_CHUNK
    heads = range(HGRN_HEADS)

    def blk(n):
        return nb - 1 - n if rev else n

    def body(zq_ref, zf_ref, zv_ref, lb_ref, o_ref, s_ref, st_ref):
        @pl.when(pl.program_id(0) == 0)
        def _():
            st_ref[...] = jnp.zeros_like(st_ref)
        lbv = lb_ref[...]
        cmask = _cum_mask(rev)

        def chunk(c, carry, mild):
            cc = nch - 1 - c if rev else c
            sl = pl.ds(pl.multiple_of(cc * HGRN_CHUNK, HGRN_CHUNK), HGRN_CHUNK)
            q, _, f, k = _gates(zq_ref[sl, :], zf_ref[sl, :], lbv)
            v = zv_ref[sl, :].astype(BF16)
            g = jnp.log(f)
            b = _chunk_cumsum(g, rev)
            tot = jnp.sum(g, axis=0, keepdims=True)
            qd = (q * jnp.exp(b)).astype(BF16)
            kd = (k * jnp.exp(tot - b)).astype(BF16)
            etot = jnp.exp(tot)

            def factored():
                kf = (k * jnp.exp(-b)).astype(BF16)
                return tuple(jnp.where(cmask, _dot(_head(qd, h), _head(kf, h), NT), 0.0) for h in heads)

            def exact():
                return tuple(_intra_exact(_head(q, h), _head(k, h), _head(b, h), cmask) for h in heads)

            a = factored() if mild else lax.cond(jnp.min(tot) >= FACTORED_DECAY_LIMIT, factored, exact)
            for h in heads:
                st = st_ref[h]
                st_bf = st.astype(BF16)
                s_ref[h, cc] = st_bf
                o_ref[sl, h * LANES:(h + 1) * LANES] = (
                    _dot(_head(qd, h), st_bf, NT) + _dot(a[h].astype(BF16), _head(v, h), NN))
                st_ref[h] = st * _head(etot, h) + _dot(_head(v, h), _head(kd, h), TN)
            return carry

        lax.cond(_block_is_mild(zf_ref, lbv, nch),
                 lambda: lax.fori_loop(0, nch, functools.partial(chunk, mild=True), 0, unroll=HGRN_UNROLL),
                 lambda: lax.fori_loop(0, nch, functools.partial(chunk, mild=False), 0))

    def col(j):
        return pl.BlockSpec((rb, D_MODEL), lambda n: (blk(n), j))

    return _call(
        body, grid=(nb,),
        in_specs=[col(0), col(2 if rev else 1), col(3), _full((1, D_MODEL))],
        out_specs=[col(0), pl.BlockSpec((HGRN_HEADS, nch, LANES, LANES), lambda n: (0, blk(n), 0, 0))],
        out_shape=[jax.ShapeDtypeStruct((t, D_MODEL), F32),
                   jax.ShapeDtypeStruct((HGRN_HEADS, t // HGRN_CHUNK, LANES, LANES), BF16)],
        scratch_shapes=[pltpu.VMEM((HGRN_HEADS, LANES, LANES), F32)],
        args=(z, z, z, lb), sem=("arbitrary",), name=name, guest=guest)


def _hgrn_bwd(z, lb, d_o, states, rev, *, name, other=None):
    t = z.shape[0]
    rb = min(HGRN_ROWS, t)
    nb, nch = t // rb, rb // HGRN_CHUNK
    heads = range(HGRN_HEADS)
    n_other = 0 if other is None else 2
    acc_dtype = F32 if other is None else BF16

    def blk(n):
        return n if rev else nb - 1 - n

    def body(zq_ref, zf_ref, zv_ref, lb_ref, do_ref, s_ref, *rest):
        oq_ref, ov_ref = rest[:n_other] if other is not None else (None, None)
        dq_ref, dzf_ref, dv_ref, dlb_ref, dst_ref = rest[n_other:]

        @pl.when(pl.program_id(0) == 0)
        def _():
            dst_ref[...] = jnp.zeros_like(dst_ref)
            dlb_ref[...] = jnp.zeros_like(dlb_ref)
        lbv = lb_ref[...]
        cmask = _cum_mask(rev)

        def chunk(c, carry, mild):
            cc = c if rev else nch - 1 - c
            sl = pl.ds(pl.multiple_of(cc * HGRN_CHUNK, HGRN_CHUNK), HGRN_CHUNK)
            zq = zq_ref[sl, :]
            q, sig, f, k = _gates(zq, zf_ref[sl, :], lbv)
            v = zv_ref[sl, :].astype(BF16)
            g = jnp.log(f)
            b = _chunk_cumsum(g, rev)
            tot = jnp.sum(g, axis=0, keepdims=True)
            eb = jnp.exp(b)
            etb = jnp.exp(tot - b)
            etot = jnp.exp(tot)
            qd = (q * eb).astype(BF16)
            kd = (k * etb).astype(BF16)
            do = do_ref[sl, :].astype(BF16)
            da = [jnp.where(cmask, _dot(_head(do, h), _head(v, h), NT), 0.0) for h in heads]

            def factored():
                ke = jnp.exp(-b)
                kf = (k * ke).astype(BF16)
                res = []
                for h in heads:
                    qh, kh = _head(qd, h), _head(kf, h)
                    da_bf = da[h].astype(BF16)
                    dqf, dkf = _dot(da_bf, kh, NN), _dot(da_bf, qh, TN)
                    res.append((jnp.where(cmask, _dot(qh, kh, NT), 0.0), dqf * _head(eb, h), dkf * _head(ke, h),
                                qh.astype(F32) * dqf - kh.astype(F32) * dkf))
                return tuple(res)

            def exact():
                res = []
                for h in heads:
                    qh, kh, bh = _head(q, h), _head(k, h), _head(b, h)
                    dq_i, dk_i = _intra_exact_bwd(qh, kh, bh, da[h])
                    res.append((_intra_exact(qh, kh, bh, cmask), dq_i, dk_i, qh * dq_i - kh * dk_i))
                return tuple(res)

            intra = factored() if mild else lax.cond(jnp.min(tot) >= FACTORED_DECAY_LIMIT, factored, exact)
            dq_p, dk_p, db_p, dtot_p = [], [], [], []
            for h in heads:
                a, dq_i, dk_i, db_i = intra[h]
                doh, vh, qh, kh = _head(do, h), _head(v, h), _head(q, h), _head(k, h)
                st0 = s_ref[h, cc]
                dst = dst_ref[h]
                dst_bf = dst.astype(BF16)
                dv = _dot(a.astype(BF16), doh, TN) + _dot(_head(kd, h), dst_bf, NT)
                if ov_ref is not None:
                    dv = dv + ov_ref[sl, h * LANES:(h + 1) * LANES]
                dv_ref[sl, h * LANES:(h + 1) * LANES] = dv.astype(acc_dtype)
                dq_x = _dot(doh, st0, NN) * _head(eb, h)
                dk_x = _dot(vh, dst_bf, NN) * _head(etb, h)
                dtot_p.append(jnp.sum(kh * dk_x, axis=0, keepdims=True)
                              + _head(etot, h) * jnp.sum(dst * st0.astype(F32), axis=0, keepdims=True))
                dst_ref[h] = dst * _head(etot, h) + _dot(doh, _head(qd, h), TN)
                dq_p.append(dq_i + dq_x)
                dk_p.append(dk_i + dk_x)
                db_p.append(db_i + qh * dq_x - kh * dk_x)
            dq, dk, db = (jnp.concatenate(x, axis=1) for x in (dq_p, dk_p, db_p))
            dg = _chunk_cumsum(db, not rev) + jnp.concatenate(dtot_p, axis=1)
            sq = _sigmoid(zq)
            dq_pre = dq * sq * (1.0 + zq * (1.0 - sq))
            if oq_ref is not None:
                dq_pre = dq_pre + oq_ref[sl, :]
            dq_ref[sl, :] = dq_pre.astype(acc_dtype)
            dfk = dg / f - dk
            dzf_ref[sl, :] = (dfk * (1.0 - lbv) * sig * (1.0 - sig)).astype(BF16)
            dlb_ref[...] += jnp.sum(dfk * (1.0 - sig), axis=0, keepdims=True)
            return carry

        lax.cond(_block_is_mild(zf_ref, lbv, nch),
                 lambda: lax.fori_loop(0, nch, functools.partial(chunk, mild=True), 0, unroll=HGRN_UNROLL),
                 lambda: lax.fori_loop(0, nch, functools.partial(chunk, mild=False), 0))

    def col(j):
        return pl.BlockSpec((rb, D_MODEL), lambda n: (blk(n), j))

    return pl.pallas_call(
        body, grid=(nb,),
        in_specs=[col(0), col(2 if rev else 1), col(3), _full((1, D_MODEL)), col(0),
                  pl.BlockSpec((HGRN_HEADS, nch, LANES, LANES), lambda n: (0, blk(n), 0, 0))] + [col(0)] * n_other,
        out_specs=[col(0), col(0), col(0), _full((1, D_MODEL))],
        out_shape=[jax.ShapeDtypeStruct((t, D_MODEL), acc_dtype), jax.ShapeDtypeStruct((t, D_MODEL), BF16),
                   jax.ShapeDtypeStruct((t, D_MODEL), acc_dtype), jax.ShapeDtypeStruct((1, D_MODEL), F32)],
        scratch_shapes=[pltpu.VMEM((HGRN_HEADS, LANES, LANES), F32)],
        compiler_params=_cp("arbitrary"), name=name)(z, z, z, lb, d_o, states, *(other or ()))


def _hgrn_post_fwd(o_f, o_b, z, norm_g, *, name):
    t = o_f.shape[0]
    tm = _rows(t)

    def body(of_ref, ob_ref, gate_ref, ng_ref, y_ref):
        ng = ng_ref[...]
        for h in range(HGRN_HEADS):
            sl = slice(h * LANES, (h + 1) * LANES)
            o = of_ref[:, sl] + ob_ref[:, sl]
            r = lax.rsqrt(jnp.mean(o * o, axis=1, keepdims=True) + LN_EPS)
            gt = gate_ref[:, sl]
            y_ref[:, sl] = (o * r * ng * (gt * _sigmoid(gt))).astype(BF16)

    row = pl.BlockSpec((tm, D_MODEL), lambda i: (i, 0))
    return pl.pallas_call(
        body, grid=(t // tm,),
        in_specs=[row, row, pl.BlockSpec((tm, D_MODEL), lambda i: (i, 4)), _full((1, LANES))],
        out_specs=row, out_shape=jax.ShapeDtypeStruct((t, D_MODEL), BF16),
        compiler_params=_cp("parallel"), name=name)(o_f, o_b, z, norm_g)


def _hgrn_post_bwd(dy, o_f, o_b, z, norm_g, *, name):
    t = o_f.shape[0]
    tm = _rows(t)

    def body(dy_ref, of_ref, ob_ref, gate_ref, ng_ref, do_ref, dgate_ref, dng_ref):
        ng = ng_ref[...]
        dng = jnp.zeros((1, LANES), F32)
        for h in range(HGRN_HEADS):
            sl = slice(h * LANES, (h + 1) * LANES)
            o = of_ref[:, sl] + ob_ref[:, sl]
            r = lax.rsqrt(jnp.mean(o * o, axis=1, keepdims=True) + LN_EPS)
            gt = gate_ref[:, sl]
            sg = _sigmoid(gt)
            dyv = dy_ref[:, sl].astype(F32)
            xn = o * r
            dgate_ref[:, sl] = (dyv * xn * ng * sg * (1.0 + gt * (1.0 - sg))).astype(BF16)
            dn = dyv * gt * sg
            dng = dng + jnp.sum(dn * xn, axis=0, keepdims=True)
            dxn = dn * ng
            do_ref[:, sl] = r * (dxn - xn * jnp.mean(dxn * xn, axis=1, keepdims=True))

        @pl.when(pl.program_id(0) == 0)
        def _():
            dng_ref[...] = jnp.zeros_like(dng_ref)
        dng_ref[...] += dng

    row = pl.BlockSpec((tm, D_MODEL), lambda i: (i, 0))
    return pl.pallas_call(
        body, grid=(t // tm,),
        in_specs=[row, row, row, pl.BlockSpec((tm, D_MODEL), lambda i: (i, 4)), _full((1, LANES))],
        out_specs=[row, row, _full((1, LANES))],
        out_shape=[jax.ShapeDtypeStruct((t, D_MODEL), F32), jax.ShapeDtypeStruct((t, D_MODEL), BF16),
                   jax.ShapeDtypeStruct((1, LANES), F32)],
        compiler_params=_cp("arbitrary"), name=name)(dy, o_f, o_b, z, norm_g)


def _hgrn_combine(dq_f, dq_b, dzf_f, dzf_b, dv_f, dv_b, dgate, *, name):
    t = dq_f.shape[0]
    tm = _rows(t)

    def body(qf, qb, ff, fb, vf, vb, gt, o_ref):
        o_ref[:, 0 * D_MODEL:1 * D_MODEL] = (qf[...] + qb[...]).astype(BF16)
        o_ref[:, 1 * D_MODEL:2 * D_MODEL] = ff[...]
        o_ref[:, 2 * D_MODEL:3 * D_MODEL] = fb[...]
        o_ref[:, 3 * D_MODEL:4 * D_MODEL] = (vf[...] + vb[...]).astype(BF16)
        o_ref[:, 4 * D_MODEL:5 * D_MODEL] = gt[...]

    row = pl.BlockSpec((tm, D_MODEL), lambda i: (i, 0))
    return pl.pallas_call(
        body, grid=(t // tm,), in_specs=[row] * 7,
        out_specs=pl.BlockSpec((tm, 5 * D_MODEL), lambda i: (i, 0)),
        out_shape=jax.ShapeDtypeStruct((t, 5 * D_MODEL), BF16),
        compiler_params=_cp("parallel"), name=name)(dq_f, dq_b, dzf_f, dzf_b, dv_f, dv_b, dgate)


def _lower_bounds(logits2d, *, name):
    def body(l_ref, lb_ref):
        l = l_ref[...]
        e = jnp.exp(l - jnp.max(l, axis=0, keepdims=True))
        sm = e / jnp.sum(e, axis=0, keepdims=True)
        s1, s2, s3 = sm[1:2], sm[2:3], sm[3:4]
        lb_ref[...] = jnp.concatenate([jnp.zeros_like(s1), s1, s1 + s2, s1 + s2 + s3], axis=0)

    return pl.pallas_call(body, out_shape=jax.ShapeDtypeStruct(logits2d.shape, F32), name=name)(logits2d)


def _lower_bounds_bwd(logits2d, dlb, *, name):
    def body(l_ref, d_ref, o_ref):
        l = l_ref[...]
        e = jnp.exp(l - jnp.max(l, axis=0, keepdims=True))
        sm = e / jnp.sum(e, axis=0, keepdims=True)
        d = d_ref[...]
        d1, d2, d3 = d[1:2], d[2:3], d[3:4]
        dsm = jnp.concatenate([jnp.zeros_like(d1), d1 + d2 + d3, d2 + d3, d3], axis=0)
        o_ref[...] = sm * (dsm - jnp.sum(sm * dsm, axis=0, keepdims=True))

    return pl.pallas_call(body, out_shape=jax.ShapeDtypeStruct(logits2d.shape, F32), name=name)(logits2d, dlb)


def _adamw(w, g, m, v, *, name):
    r, c = w.shape
    tr = r if r <= 512 else _pick_rows(r)

    def body(w_ref, g_ref, m_ref, v_ref, d_ref, nm_ref, nv_ref):
        gv = g_ref[...]
        nm = ADAM_B1 * m_ref[...] + (1.0 - ADAM_B1) * gv
        nv = ADAM_B2 * v_ref[...] + (1.0 - ADAM_B2) * (gv * gv)
        m_hat = nm / (1.0 - ADAM_B1 ** ADAM_STEP)
        v_hat = nv / (1.0 - ADAM_B2 ** ADAM_STEP)
        d_ref[...] = -ADAM_LR * (m_hat / (jnp.sqrt(v_hat) + ADAM_EPS) + ADAM_WD * w_ref[...])
        nm_ref[...] = nm
        nv_ref[...] = nv

    blk = pl.BlockSpec((tr, c), lambda i: (i, 0))
    shp = jax.ShapeDtypeStruct((r, c), F32)
    return pl.pallas_call(body, grid=(r // tr,), in_specs=[blk] * 4, out_specs=[blk] * 3, out_shape=[shp] * 3,
                          compiler_params=_cp("parallel"), name=name)(w, g, m, v)


def _pick_rows(r, cap=512):
    best = 8
    for d in range(8, cap + 1, 8):
        if r % d == 0:
            best = d
    assert r % best == 0, r
    return best


def _local_step(x, p, target, w, sp):
    t = x.shape[0]
    tables = _rope_tables(t)
    logits2d = sp["hgrn_lb_logits"].reshape(DEPTH, 2 * D_MODEL)
    lb_all = _lower_bounds(logits2d, name="lb_fwd").reshape(DEPTH, 2, 1, D_MODEL)
    row = lambda a, i: a[i][None]

    saved = []
    xf, xb = x, x.astype(BF16)
    for i in range(DEPTH):
        j = i // 2
        s = dict(xin_bf=xb)
        if i % 2 == 0:
            q, k, v = _qkv_rope(xb, w["att_w_qkv"][j], tables, name=f"qkv_rope_{i}")
            o, lse = _attn_fwd(q, k, v, sp["att_sink"][j], name=f"attn_fwd_{i}")
            s.update(q=q, k=k, v=v, o=o, lse=lse)
            mix_in, w_o = o, w["att_w_o"][j]
        else:
            z = _mm_nn(xb, w["hgrn_w_in"][j], out_dtype=F32, name=f"hgrn_in_{i}")
            o_f, s_f = _hgrn_fwd(z, lb_all[i, 0], False, name=f"hgrn_scan_f_{i}")
            o_b, s_b = _hgrn_fwd(z, lb_all[i, 1], True, name=f"hgrn_scan_b_{i}")
            og = _hgrn_post_fwd(o_f, o_b, z, row(sp["hgrn_norm_g"], j), name=f"hgrn_post_{i}")
            s.update(z=z, o_f=o_f, o_b=o_b, s_f=s_f, s_b=s_b, og=og)
            mix_in, w_o = og, w["hgrn_w_o"][j]
        x1, x1b, xh1, rs1 = _mm_res_ln(mix_in, w_o, xf, row(sp["ln_mix_g"], i), row(sp["ln_mix_b"], i), name=f"mix_out_ln_{i}")
        g, u, h = _ffn_in(x1b, w["ffn_w_in"][i], name=f"ffn_in_{i}")
        x2, x2b, xh2, rs2 = _mm_res_ln(h, w["ffn_w_out"][i], x1, row(sp["ln_ffn_g"], i), row(sp["ln_ffn_b"], i), name=f"ffn_out_ln_{i}")
        xf, xb, gate, pp = _ple_fwd(x2, x2b, p, i, w["ple_w_gate"][i], w["ple_w_proj"][i], name=f"ple_fwd_{i}")
        s.update(x1b=x1b, xh1=xh1, rs1=rs1, g=g, u=u, h=h, x2b=x2b, xh2=xh2, rs2=rs2, gate=gate, pp=pp)
        saved.append(s)

    loss, dx = _loss_head(xf, target, name="loss_head")

    gw = {n: [None] * w[n].shape[0] for n in w}
    gs = {n: [None] * DEPTH for n in ("ln_mix_g", "ln_mix_b", "ln_ffn_g", "ln_ffn_b")}
    gs.update(att_sink=[None] * 2, hgrn_norm_g=[None] * 2)
    dlb = [jnp.zeros((2, D_MODEL), F32)] * DEPTH
    for i in reversed(range(DEPTH)):
        j = i // 2
        s = saved[i]
        du2, du2b, dpre, dpp, gs["ln_ffn_g"][i], gs["ln_ffn_b"][i] = _ple_bwd(
            dx, s["gate"], s["pp"], w["ple_w_gate"][i], s["xh2"], s["rs2"], row(sp["ln_ffn_g"], i), name=f"ple_bwd_{i}")
        gw["ple_w_gate"][i] = _mm_tn(s["x2b"], dpre, name=f"dw_ple_gate_{i}")
        gw["ple_w_proj"][i] = _mm_tn_p(p, i, dpp, name=f"dw_ple_proj_{i}")
        dgg, dgu = _ffn_out_bwd(du2b, w["ffn_w_out"][i], s["g"], s["u"], name=f"ffn_out_bwd_{i}")
        gw["ffn_w_out"][i] = _mm_tn(s["h"], du2b, name=f"dw_ffn_out_{i}")
        du1, du1b, gs["ln_mix_g"][i], gs["ln_mix_b"][i] = _mm_nt(
            [dgg, dgu], w["ffn_w_in"][i], tk=_pick(D_FF, 1408), resid=du2,
            ln=(s["xh1"], s["rs1"], row(sp["ln_mix_g"], i)), name=f"ffn_in_bwd_{i}")
        gw["ffn_w_in"][i] = jnp.concatenate(
            [_mm_tn(s["x1b"], dgg, name=f"dw_ffn_gate_{i}"), _mm_tn(s["x1b"], dgu, name=f"dw_ffn_up_{i}")], axis=1)
        if i % 2 == 0:
            gw["att_w_o"][j] = _mm_tn(s["o"], du1b, name=f"dw_att_o_{i}")
            do = _mm_nt([du1b], w["att_w_o"][j], tk=Q_DIM, out_dtype=BF16, name=f"att_o_bwd_{i}")
            dq, dkw, dvw, gs["att_sink"][j] = _attn_bwd(
                s["q"], s["k"], s["v"], s["o"], do, s["lse"], sp["att_sink"][j], name=f"attn_bwd_{i}")
            dqkv = _attn_post_bwd(dq, dkw, dvw, tables, name=f"attn_post_bwd_{i}")
            gw["att_w_qkv"][j] = _mm_tn(s["xin_bf"], dqkv, name=f"dw_att_qkv_{i}")
            dx = _mm_nt([dqkv], w["att_w_qkv"][j], tk=Q_DIM + 2 * KV_DIM, resid=du1, name=f"qkv_bwd_{i}")
        else:
            gw["hgrn_w_o"][j] = _mm_tn(s["og"], du1b, name=f"dw_hgrn_o_{i}")
            dy = _mm_nt([du1b], w["hgrn_w_o"][j], tk=D_MODEL, out_dtype=BF16, name=f"hgrn_o_bwd_{i}")
            d_o, dgate, gs["hgrn_norm_g"][j] = _hgrn_post_bwd(
                dy, s["o_f"], s["o_b"], s["z"], row(sp["hgrn_norm_g"], j), name=f"hgrn_post_bwd_{i}")
            dq_f, dzf_f, dv_f, dlb_f = _hgrn_bwd(s["z"], lb_all[i, 0], d_o, s["s_f"], False, name=f"hgrn_scan_f_bwd_{i}")
            dq_b, dzf_b, dv_b, dlb_b = _hgrn_bwd(s["z"], lb_all[i, 1], d_o, s["s_b"], True, name=f"hgrn_scan_b_bwd_{i}")
            dlb[i] = jnp.stack([dlb_f.reshape(D_MODEL), dlb_b.reshape(D_MODEL)])
            dz = _hgrn_combine(dq_f, dq_b, dzf_f, dzf_b, dv_f, dv_b, dgate, name=f"hgrn_combine_{i}")
            gw["hgrn_w_in"][j] = _mm_tn(s["xin_bf"], dz, name=f"dw_hgrn_in_{i}")
            dx = _mm_nt([dz], w["hgrn_w_in"][j], tk=_pick(5 * D_MODEL, 1408), resid=du1, name=f"hgrn_in_bwd_{i}")

    gw = {n: jnp.stack(v) for n, v in gw.items()}
    gsmall = {n: jnp.concatenate(v, axis=0) for n, v in gs.items()}
    gsmall["hgrn_lb_logits"] = _lower_bounds_bwd(
        logits2d, jnp.stack(dlb).reshape(DEPTH, 2 * D_MODEL), name="lb_bwd").reshape(DEPTH, 2, D_MODEL)
    return loss, dx, gw, gsmall


ANY = pl.BlockSpec(memory_space=pl.ANY)


def _place():
    x, y, c = lax.axis_index("x"), lax.axis_index("y"), lax.axis_index("c")
    chips = [(1 - x, y), (x, 1 - y), (1 - x, 1 - y)]
    return x, y, c, chips


def _remote(src, dst, send_sem, recv_sem, to):
    return pltpu.make_async_remote_copy(src_ref=src, dst_ref=dst, send_sem=send_sem, recv_sem=recv_sem,
                                        device_id=to, device_id_type=MESH)


def _allgather_weights(packed, *, name):
    r = packed.shape[0]
    hr = r // 2

    def body(src_ref, out_ref, send_sems, recv_sems, local_sem):
        x, y, c, chips = _place()
        sibling = (x, y, 1 - c)

        def half(cx, cy, hc):
            return out_ref.at[2 * cx + cy, pl.ds(hc * hr, hr), :]

        mine = pltpu.make_async_copy(src_ref, out_ref.at[2 * x + y], local_sem)
        mine.start()
        my_half = src_ref.at[pl.ds(c * hr, hr), :]
        first = [_remote(my_half, half(x, y, c), send_sems.at[j], recv_sems.at[j], (*chip, c)) for j, chip in enumerate(chips)]
        for cp in first:
            cp.start()
        passed = [_remote(half(*chip, c), half(*chip, c), send_sems.at[3 + j], recv_sems.at[3 + j], sibling)
                  for j, chip in enumerate(chips)]
        for j, chip in enumerate(chips):
            _remote(my_half, half(*chip, c), send_sems.at[j], recv_sems.at[j], (*chip, c)).wait_recv()
            passed[j].start()
        for j, chip in enumerate(chips):
            _remote(my_half, half(*chip, 1 - c), send_sems.at[3 + j], recv_sems.at[3 + j], sibling).wait_recv()
        for cp in first + passed:
            cp.wait_send()
        mine.wait()

    return pl.pallas_call(
        body, in_specs=[ANY], out_specs=ANY, out_shape=jax.ShapeDtypeStruct((N_CHIPS, r, packed.shape[1]), packed.dtype),
        scratch_shapes=[pltpu.SemaphoreType.DMA((6,)), pltpu.SemaphoreType.DMA((6,)), pltpu.SemaphoreType.DMA],
        name=name)(packed)


def _allreduce_small(block, *, name):
    m, n = block.shape

    def body(x_ref, sum_ref, all_ref, send_sems, recv_sems):
        x, y, c, chips = _place()
        me, sibling = (x, y, c), (x, y, 1 - c)

        def rows(px, py, pc):
            return all_ref.at[pl.ds((4 * px + 2 * py + pc) * m, m), :]

        all_ref[pl.ds((4 * x + 2 * y + c) * m, m), :] = x_ref[...]
        first = [_remote(x_ref, rows(*me), send_sems.at[0], recv_sems.at[0], sibling)]
        first += [_remote(x_ref, rows(*me), send_sems.at[1 + j], recv_sems.at[1 + j], (*chip, c)) for j, chip in enumerate(chips)]
        for cp in first:
            cp.start()
        passed = [_remote(rows(*chip, c), rows(*chip, c), send_sems.at[4 + j], recv_sems.at[4 + j], sibling)
                  for j, chip in enumerate(chips)]
        for j, chip in enumerate(chips):
            _remote(x_ref, rows(*chip, c), send_sems.at[1 + j], recv_sems.at[1 + j], me).wait_recv()
            passed[j].start()
        _remote(x_ref, rows(*sibling), send_sems.at[0], recv_sems.at[0], me).wait_recv()
        for j, chip in enumerate(chips):
            _remote(x_ref, rows(*chip, 1 - c), send_sems.at[4 + j], recv_sems.at[4 + j], me).wait_recv()
        for cp in first + passed:
            cp.wait_send()
        acc = all_ref[pl.ds(0, m), :]
        for d in range(1, 8):
            acc = acc + all_ref[pl.ds(d * m, m), :]
        sum_ref[...] = acc

    vmem = pl.BlockSpec(memory_space=pltpu.VMEM)
    return pl.pallas_call(
        body, in_specs=[vmem], out_specs=vmem, out_shape=jax.ShapeDtypeStruct((m, n), F32),
        scratch_shapes=[pltpu.VMEM((8 * m, n), F32), pltpu.SemaphoreType.DMA((7,)), pltpu.SemaphoreType.DMA((7,))],
        name=name)(block)


def _pair_exchange(g, *, name):
    n, r, w = g.shape
    hr = r // 2

    def body(g_ref, out_ref, send_sem, recv_sem):
        x, y, c, _ = _place()
        cp = _remote(g_ref.at[:, pl.ds((1 - c) * hr, hr), :], out_ref, send_sem, recv_sem, (x, y, 1 - c))
        cp.start()
        cp.wait()

    return pl.pallas_call(
        body, in_specs=[ANY], out_specs=ANY, out_shape=jax.ShapeDtypeStruct((n, hr, w), g.dtype),
        scratch_shapes=[pltpu.SemaphoreType.DMA, pltpu.SemaphoreType.DMA], name=name)(g)


def _chip_scatter(part, *, name):
    n, h, w = part.shape

    def body(p_ref, out_ref, send_sems, recv_sems, local_sem):
        x, y, c, chips = _place()
        me = 2 * x + y
        mine = pltpu.make_async_copy(p_ref.at[me], out_ref.at[me], local_sem)
        mine.start()
        sends = [_remote(p_ref.at[2 * cx + cy], out_ref.at[me], send_sems.at[j], recv_sems.at[j], (cx, cy, c))
                 for j, (cx, cy) in enumerate(chips)]
        for cp in sends:
            cp.start()
        for j, (cx, cy) in enumerate(chips):
            _remote(p_ref.at[me], out_ref.at[2 * cx + cy], send_sems.at[j], recv_sems.at[j], (cx, cy, c)).wait_recv()
        for cp in sends:
            cp.wait_send()
        mine.wait()

    return pl.pallas_call(
        body, in_specs=[ANY], out_specs=ANY, out_shape=jax.ShapeDtypeStruct((n, h, w), part.dtype),
        scratch_shapes=[pltpu.SemaphoreType.DMA((3,)), pltpu.SemaphoreType.DMA((3,)), pltpu.SemaphoreType.DMA],
        name=name)(part)


def _pair_share(half, *, name):
    h, w = half.shape

    def body(h_ref, out_ref, send_sem, recv_sem, local_sem):
        x, y, c, _ = _place()
        dst = out_ref.at[pl.ds(c * h, h), :]
        mine = pltpu.make_async_copy(h_ref, dst, local_sem)
        mine.start()
        cp = _remote(h_ref, dst, send_sem, recv_sem, (x, y, 1 - c))
        cp.start()
        cp.wait_send()
        _remote(h_ref, out_ref.at[pl.ds((1 - c) * h, h), :], send_sem, recv_sem, (x, y, 1 - c)).wait_recv()
        mine.wait()

    return pl.pallas_call(
        body, in_specs=[ANY], out_specs=ANY, out_shape=jax.ShapeDtypeStruct((2 * h, w), half.dtype),
        scratch_shapes=[pltpu.SemaphoreType.DMA, pltpu.SemaphoreType.DMA, pltpu.SemaphoreType.DMA], name=name)(half)


def _add_own_half(g, recv, c, *, name):
    n, r, w = g.shape
    hr = r // 2
    tr = _pick_rows(hr, 1024)
    nr = hr // tr

    def body(c_ref, g_ref, r_ref, o_ref):
        o_ref[...] = (g_ref[...] + r_ref[...]).astype(BF16)

    return pl.pallas_call(
        body,
        grid_spec=pltpu.PrefetchScalarGridSpec(
            num_scalar_prefetch=1, grid=(n, nr),
            in_specs=[pl.BlockSpec((None, tr, w), lambda s, i, c_ref: (s, c_ref[0] * nr + i, 0)),
                      pl.BlockSpec((None, tr, w), lambda s, i, c_ref: (s, i, 0))],
            out_specs=pl.BlockSpec((None, tr, w), lambda s, i, c_ref: (s, i, 0))),
        out_shape=jax.ShapeDtypeStruct((n, hr, w), BF16),
        compiler_params=_cp("parallel", "parallel"), name=name)(c, g, recv)


def _sum_chips(q, *, name):
    n, h, w = q.shape
    tr = _pick_rows(h, 1024)

    def body(a, b, c, d, o_ref):
        o_ref[...] = ((a[...].astype(F32) + b[...].astype(F32)) + c[...].astype(F32)) + d[...].astype(F32)

    specs = [pl.BlockSpec((None, tr, w), functools.partial(lambda i, s: (s, i, 0), s=s)) for s in range(n)]
    return pl.pallas_call(
        body, grid=(h // tr,), in_specs=specs, out_specs=pl.BlockSpec((tr, w), lambda i: (i, 0)),
        out_shape=jax.ShapeDtypeStruct((h, w), F32), compiler_params=_cp("parallel"), name=name)(q, q, q, q)


PACK_W = 1024
BIG = (("att_w_qkv", 2), ("att_w_o", 1), ("hgrn_w_in", 2), ("hgrn_w_o", 1),
       ("ffn_w_in", 2), ("ffn_w_out", 1), ("ple_w_gate", 1), ("ple_w_proj", 2))
LB_ROWS = 32


def _pack_shards(shards):
    return jnp.concatenate([shards[n].reshape(-1, PACK_W) for n, _ in BIG], axis=0)


def _unpack_shards(buf, shapes):
    out, r0 = {}, 0
    for n, _ in BIG:
        nr = shapes[n][0] * shapes[n][1] * shapes[n][2] // PACK_W
        out[n] = buf[r0:r0 + nr].reshape(shapes[n])
        r0 += nr
    return out


def _gathered_to_full(buf, shapes):
    out, r0 = {}, 0
    for n, axis in BIG:
        l, r, c = shapes[n]
        nr = l * r * c // PACK_W
        sh = buf[:, r0:r0 + nr].reshape(N_CHIPS, l, r, c)
        out[n] = (sh.transpose(1, 0, 2, 3).reshape(l, N_CHIPS * r, c) if axis == 1
                  else sh.transpose(1, 2, 0, 3).reshape(l, r, N_CHIPS * c))
        r0 += nr
    return out, r0


def _full_to_slabs(full, shapes):
    parts = []
    for n, axis in BIG:
        l, r, c = shapes[n]
        g = full[n]
        g = (g.reshape(l, N_CHIPS, r, c).transpose(1, 0, 2, 3) if axis == 1
             else g.reshape(l, r, N_CHIPS, c).transpose(2, 0, 1, 3))
        parts.append(g.reshape(N_CHIPS, -1, PACK_W))
    return jnp.concatenate(parts, axis=1)


SMALL = ("ln_mix_g", "ln_mix_b", "ln_ffn_g", "ln_ffn_b")
SMALL_ROWS = 32


def _pack_small(vals, lb):
    rows = [vals[n] for n in SMALL] + [lb.reshape(-1, PACK_W)]
    for n in ("att_sink", "hgrn_norm_g"):
        flat = vals[n].reshape(1, -1)
        rows.append(jnp.pad(flat, ((0, 0), (0, PACK_W - flat.shape[1]))))
    buf = jnp.concatenate(rows, axis=0)
    return jnp.pad(buf, ((0, SMALL_ROWS - buf.shape[0]), (0, 0)))


def _unpack_small(buf, lb_shape):
    out, r0 = {}, 0
    for n in SMALL:
        out[n] = buf[r0:r0 + DEPTH]
        r0 += DEPTH
    nlb = lb_shape[0] * lb_shape[1] * lb_shape[2] // PACK_W
    out["hgrn_lb_logits"] = buf[r0:r0 + nlb].reshape(lb_shape)
    r0 += nlb
    out["att_sink"] = buf[r0, :2 * N_Q_HEADS].reshape(2, N_Q_HEADS)
    out["hgrn_norm_g"] = buf[r0 + 1, :2 * LANES].reshape(2, LANES)
    return out


WEIGHTS = ("att_w_qkv", "att_sink", "att_w_o", "hgrn_w_in", "hgrn_lb_logits", "hgrn_norm_g", "hgrn_w_o", "ln_mix_g",
           "ln_mix_b", "ffn_w_in", "ffn_w_out", "ln_ffn_g", "ln_ffn_b", "ple_w_gate", "ple_w_proj")


def kernel(x, p, att_w_qkv, att_sink, att_w_o, hgrn_w_in, hgrn_lb_logits, hgrn_norm_g, hgrn_w_o, ln_mix_g, ln_mix_b, ffn_w_in, ffn_w_out, ln_ffn_g, ln_ffn_b, ple_w_gate, ple_w_proj, loss_target, m_att_w_qkv, m_att_sink, m_att_w_o, m_hgrn_w_in, m_hgrn_lb_logits, m_hgrn_norm_g, m_hgrn_w_o, m_ln_mix_g, m_ln_mix_b, m_ffn_w_in, m_ffn_w_out, m_ln_ffn_g, m_ln_ffn_b, m_ple_w_gate, m_ple_w_proj, v_att_w_qkv, v_att_sink, v_att_w_o, v_hgrn_w_in, v_hgrn_lb_logits, v_hgrn_norm_g, v_hgrn_w_o, v_ln_mix_g, v_ln_mix_b, v_ffn_w_in, v_ffn_w_out, v_ln_ffn_g, v_ln_ffn_b, v_ple_w_gate, v_ple_w_proj):
    wts = dict(att_w_qkv=att_w_qkv, att_sink=att_sink, att_w_o=att_w_o, hgrn_w_in=hgrn_w_in, hgrn_lb_logits=hgrn_lb_logits,
               hgrn_norm_g=hgrn_norm_g, hgrn_w_o=hgrn_w_o, ln_mix_g=ln_mix_g, ln_mix_b=ln_mix_b, ffn_w_in=ffn_w_in,
               ffn_w_out=ffn_w_out, ln_ffn_g=ln_ffn_g, ln_ffn_b=ln_ffn_b, ple_w_gate=ple_w_gate, ple_w_proj=ple_w_proj)
    mom = dict(att_w_qkv=m_att_w_qkv, att_sink=m_att_sink, att_w_o=m_att_w_o, hgrn_w_in=m_hgrn_w_in, hgrn_lb_logits=m_hgrn_lb_logits,
               hgrn_norm_g=m_hgrn_norm_g, hgrn_w_o=m_hgrn_w_o, ln_mix_g=m_ln_mix_g, ln_mix_b=m_ln_mix_b, ffn_w_in=m_ffn_w_in,
               ffn_w_out=m_ffn_w_out, ln_ffn_g=m_ln_ffn_g, ln_ffn_b=m_ln_ffn_b, ple_w_gate=m_ple_w_gate, ple_w_proj=m_ple_w_proj)
    var = dict(att_w_qkv=v_att_w_qkv, att_sink=v_att_sink, att_w_o=v_att_w_o, hgrn_w_in=v_hgrn_w_in, hgrn_lb_logits=v_hgrn_lb_logits,
               hgrn_norm_g=v_hgrn_norm_g, hgrn_w_o=v_hgrn_w_o, ln_mix_g=v_ln_mix_g, ln_mix_b=v_ln_mix_b, ffn_w_in=v_ffn_w_in,
               ffn_w_out=v_ffn_w_out, ln_ffn_g=v_ln_ffn_g, ln_ffn_b=v_ln_ffn_b, ple_w_gate=v_ple_w_gate, ple_w_proj=v_ple_w_proj)
    shapes = {n: wts[n].shape for n, _ in BIG}
    chip = 2 * lax.axis_index("x") + lax.axis_index("y")
    core = lax.axis_index("c")

    lb_bits = lax.bitcast_convert_type(hgrn_lb_logits.reshape(-1), BF16).reshape(-1, PACK_W)
    packed = jnp.concatenate([_pack_shards({n: wts[n].astype(BF16) for n, _ in BIG}), lb_bits,
                              jnp.zeros((LB_ROWS - lb_bits.shape[0], PACK_W), BF16)], axis=0)
    gathered = _allgather_weights(packed, name="allgather_weights")
    w_full, r_big = _gathered_to_full(gathered, shapes)
    lb_sh = hgrn_lb_logits.shape
    lb_rows = gathered[:, r_big:r_big + lb_bits.shape[0]].reshape(N_CHIPS, -1, 2)
    lb_full = lax.bitcast_convert_type(lb_rows, F32).reshape(N_CHIPS, lb_sh[0], lb_sh[1], lb_sh[2])
    lb_full = lb_full.transpose(1, 2, 0, 3).reshape(lb_sh[0], lb_sh[1], N_CHIPS * lb_sh[2])
    sp = dict(att_sink=att_sink, hgrn_lb_logits=lb_full, hgrn_norm_g=hgrn_norm_g, ln_mix_g=ln_mix_g, ln_mix_b=ln_mix_b,
              ln_ffn_g=ln_ffn_g, ln_ffn_b=ln_ffn_b)

    loss, grad_x, gw, gs = _local_step(x[0], p, loss_target[0], w_full, sp)
    loss = lax.psum(loss[0, 0], ("x", "y", "c"))

    slabs = _full_to_slabs(gw, shapes)
    from_sibling = _pair_exchange(slabs, name="grad_pair_exchange")
    chip_part = _add_own_half(slabs, from_sibling, core.reshape(1).astype(jnp.int32), name="grad_pair_add")
    from_chips = _chip_scatter(chip_part, name="grad_chip_scatter")
    half = _sum_chips(from_chips, name="grad_chip_sum")
    g_big = _unpack_shards(_pair_share(half, name="grad_pair_share"), shapes)

    g_small_full = _unpack_small(_allreduce_small(_pack_small(gs, gs["hgrn_lb_logits"]), name="allreduce_small"),
                                 (lb_sh[0], lb_sh[1], N_CHIPS * lb_sh[2]))
    g_lb = lax.dynamic_slice_in_dim(g_small_full["hgrn_lb_logits"], chip * lb_sh[2], lb_sh[2], axis=2)
    grads = dict(g_big)
    grads.update({n: g_small_full[n] for n in SMALL + ("att_sink", "hgrn_norm_g")})
    grads["hgrn_lb_logits"] = g_lb

    delta, new_m, new_v = {}, {}, {}
    for n, _ in BIG:
        two_d = lambda a: a.reshape(-1, a.shape[-1])
        d, nm, nv = _adamw(two_d(wts[n]), two_d(grads[n]), two_d(mom[n]), two_d(var[n]), name=f"adamw_{n}")
        delta[n], new_m[n], new_v[n] = d.reshape(shapes[n]), nm.reshape(shapes[n]), nv.reshape(shapes[n])
    packs = [_pack_small(src, src["hgrn_lb_logits"]) for src in (wts, grads, mom, var)]
    outs = _adamw(*packs, name="adamw_small")
    for dst, buf in zip((delta, new_m, new_v), outs):
        dst.update(_unpack_small(buf, lb_sh))

    return (loss, grad_x[None], *[grads[n] for n in WEIGHTS], *[delta[n] for n in WEIGHTS],
            *[new_m[n] for n in WEIGHTS], *[new_v[n] for n in WEIGHTS])


def _gather_guest(packed):
    r, w = packed.shape
    hr = r // 2

    def copies(src_ref, out_ref, send_sems, recv_sems):
        x, y, c, chips = _place()
        sibling = (x, y, 1 - c)

        def half(cx, cy, hc):
            return out_ref.at[2 * cx + cy, pl.ds(hc * hr, hr), :]

        my_half = src_ref.at[pl.ds(c * hr, hr), :]
        first = [_remote(my_half, half(x, y, c), send_sems.at[j], recv_sems.at[j], (*chip, c)) for j, chip in enumerate(chips)]
        passed = [_remote(half(*chip, c), half(*chip, c), send_sems.at[3 + j], recv_sems.at[3 + j], sibling)
                  for j, chip in enumerate(chips)]
        from_chips = [_remote(my_half, half(*chip, c), send_sems.at[j], recv_sems.at[j], (*chip, c)) for j, chip in enumerate(chips)]
        from_sibling = [_remote(my_half, half(*chip, 1 - c), send_sems.at[3 + j], recv_sems.at[3 + j], sibling)
                        for j, chip in enumerate(chips)]
        return first, passed, from_chips, from_sibling

    def start(gi, go, gs):
        for cp in copies(gi[0], go[0], *gs)[0]:
            cp.start()

    def finish(gi, go, gs):
        first, passed, from_chips, from_sibling = copies(gi[0], go[0], *gs)
        for arrival, onward in zip(from_chips, passed):
            arrival.wait_recv()
            onward.start()
        for arrival in from_sibling:
            arrival.wait_recv()
        for cp in first + passed:
            cp.wait_send()

    return _Guest((packed,), (jax.ShapeDtypeStruct((N_CHIPS, r, w), packed.dtype),),
                  (pltpu.SemaphoreType.DMA((6,)), pltpu.SemaphoreType.DMA((6,))), start, finish)


def _pair_exchange_guest(g):
    n, r, w = g.shape
    hr = r // 2

    def copy(g_ref, out_ref, send_sem, recv_sem):
        x, y, c, _ = _place()
        return _remote(g_ref.at[:, pl.ds((1 - c) * hr, hr), :], out_ref, send_sem, recv_sem, (x, y, 1 - c))

    return _Guest((g,), (jax.ShapeDtypeStruct((n, hr, w), g.dtype),), (pltpu.SemaphoreType.DMA, pltpu.SemaphoreType.DMA),
                  lambda gi, go, gs: copy(gi[0], go[0], *gs).start(),
                  lambda gi, go, gs: copy(gi[0], go[0], *gs).wait())


def _chip_scatter_guest(part):
    n, h, w = part.shape

    def copies(p_ref, out_ref, send_sems, recv_sems):
        x, y, c, chips = _place()
        me = 2 * x + y
        sends = [_remote(p_ref.at[2 * cx + cy], out_ref.at[me], send_sems.at[j], recv_sems.at[j], (cx, cy, c))
                 for j, (cx, cy) in enumerate(chips)]
        arrivals = [_remote(p_ref.at[me], out_ref.at[2 * cx + cy], send_sems.at[j], recv_sems.at[j], (cx, cy, c))
                    for j, (cx, cy) in enumerate(chips)]
        return sends, arrivals

    def start(gi, go, gs):
        for cp in copies(gi[0], go[0], *gs)[0]:
            cp.start()

    def finish(gi, go, gs):
        sends, arrivals = copies(gi[0], go[0], *gs)
        for cp in arrivals:
            cp.wait_recv()
        for cp in sends:
            cp.wait_send()

    return _Guest((part,), (jax.ShapeDtypeStruct((n, h, w), part.dtype),),
                  (pltpu.SemaphoreType.DMA((3,)), pltpu.SemaphoreType.DMA((3,))), start, finish)


def _pair_share_guest(halves):
    n = len(halves)

    def copies(k, h_ref, out_ref, send_sems, recv_sems):
        x, y, c, _ = _place()
        send = _remote(h_ref, out_ref.at[c], send_sems.at[k], recv_sems.at[k], (x, y, 1 - c))
        arrival = _remote(h_ref, out_ref.at[1 - c], send_sems.at[k], recv_sems.at[k], (x, y, 1 - c))
        return send, arrival

    def start(gi, go, gs):
        for k in range(n):
            copies(k, gi[k], go[k], *gs)[0].start()

    def finish(gi, go, gs):
        for k in range(n):
            send, arrival = copies(k, gi[k], go[k], *gs)
            send.wait_send()
            arrival.wait_recv()

    return _Guest(tuple(halves), tuple(jax.ShapeDtypeStruct((2,) + a.shape, a.dtype) for a in halves),
                  (pltpu.SemaphoreType.DMA((n,)), pltpu.SemaphoreType.DMA((n,))), start, finish)


def _own(stack, idx):
    return lax.dynamic_index_in_dim(stack, idx, axis=0, keepdims=False)


def _put(buf, block, idx):
    return lax.dynamic_update_slice_in_dim(buf, block[None], idx, axis=0)


AXIS = dict(BIG)


def _layer_parts(i):
    j = i // 2
    mixer = (("att_w_qkv", j), ("att_w_o", j)) if i % 2 == 0 else (("hgrn_w_in", j), ("hgrn_w_o", j))
    return mixer + (("ffn_w_in", i), ("ffn_w_out", i), ("ple_w_gate", i), ("ple_w_proj", i))


def _gather_groups():
    first = _layer_parts(0)
    return [first[:1], first[1:] + _layer_parts(1), _layer_parts(2), _layer_parts(3)]


def _pack_parts(shards, parts):
    return jnp.concatenate([shards[n][l].reshape(-1, PACK_W) for n, l in parts], axis=0)


def _gathered_parts(buf, parts, shapes):
    out, r0 = {}, 0
    for n, l in parts:
        r, c = shapes[n][1:]
        nr = r * c // PACK_W
        sh = buf[:, r0:r0 + nr].reshape(N_CHIPS, r, c)
        out[(n, l)] = sh.reshape(N_CHIPS * r, c) if AXIS[n] == 1 else sh.transpose(1, 0, 2).reshape(r, N_CHIPS * c)
        r0 += nr
    return out, r0


def _pack_layer(shards, i):
    return _pack_parts(shards, _layer_parts(i))


def _unpack_layer(buf, i, shapes):
    out, r0 = {}, 0
    for n, _ in _layer_parts(i):
        r, c = shapes[n][1:]
        nr = r * c // PACK_W
        out[n] = buf[r0:r0 + nr].reshape(r, c)
        r0 += nr
    return out


def _gathered_layer(buf, i, shapes):
    out, r0 = {}, 0
    for n, _ in _layer_parts(i):
        r, c = shapes[n][1:]
        nr = r * c // PACK_W
        sh = buf[:, r0:r0 + nr].reshape(N_CHIPS, r, c)
        out[n] = sh.reshape(N_CHIPS * r, c) if AXIS[n] == 1 else sh.transpose(1, 0, 2).reshape(r, N_CHIPS * c)
        r0 += nr
    return out, r0


def _layer_slabs(full, i, shapes):
    parts = []
    for n, _ in _layer_parts(i):
        r, c = shapes[n][1:]
        g = full[n]
        g = g.reshape(N_CHIPS, -1, PACK_W) if AXIS[n] == 1 else g.reshape(r, N_CHIPS, c).transpose(1, 0, 2).reshape(N_CHIPS, -1, PACK_W)
        parts.append(g)
    return jnp.concatenate(parts, axis=1)


def kernel(x, p, att_w_qkv, att_sink, att_w_o, hgrn_w_in, hgrn_lb_logits, hgrn_norm_g, hgrn_w_o, ln_mix_g, ln_mix_b, ffn_w_in, ffn_w_out, ln_ffn_g, ln_ffn_b, ple_w_gate, ple_w_proj, loss_target, m_att_w_qkv, m_att_sink, m_att_w_o, m_hgrn_w_in, m_hgrn_lb_logits, m_hgrn_norm_g, m_hgrn_w_o, m_ln_mix_g, m_ln_mix_b, m_ffn_w_in, m_ffn_w_out, m_ln_ffn_g, m_ln_ffn_b, m_ple_w_gate, m_ple_w_proj, v_att_w_qkv, v_att_sink, v_att_w_o, v_hgrn_w_in, v_hgrn_lb_logits, v_hgrn_norm_g, v_hgrn_w_o, v_ln_mix_g, v_ln_mix_b, v_ffn_w_in, v_ffn_w_out, v_ln_ffn_g, v_ln_ffn_b, v_ple_w_gate, v_ple_w_proj):
    wts = dict(att_w_qkv=att_w_qkv, att_sink=att_sink, att_w_o=att_w_o, hgrn_w_in=hgrn_w_in, hgrn_lb_logits=hgrn_lb_logits,
               hgrn_norm_g=hgrn_norm_g, hgrn_w_o=hgrn_w_o, ln_mix_g=ln_mix_g, ln_mix_b=ln_mix_b, ffn_w_in=ffn_w_in,
               ffn_w_out=ffn_w_out, ln_ffn_g=ln_ffn_g, ln_ffn_b=ln_ffn_b, ple_w_gate=ple_w_gate, ple_w_proj=ple_w_proj)
    mom = dict(att_w_qkv=m_att_w_qkv, att_sink=m_att_sink, att_w_o=m_att_w_o, hgrn_w_in=m_hgrn_w_in, hgrn_lb_logits=m_hgrn_lb_logits,
               hgrn_norm_g=m_hgrn_norm_g, hgrn_w_o=m_hgrn_w_o, ln_mix_g=m_ln_mix_g, ln_mix_b=m_ln_mix_b, ffn_w_in=m_ffn_w_in,
               ffn_w_out=m_ffn_w_out, ln_ffn_g=m_ln_ffn_g, ln_ffn_b=m_ln_ffn_b, ple_w_gate=m_ple_w_gate, ple_w_proj=m_ple_w_proj)
    var = dict(att_w_qkv=v_att_w_qkv, att_sink=v_att_sink, att_w_o=v_att_w_o, hgrn_w_in=v_hgrn_w_in, hgrn_lb_logits=v_hgrn_lb_logits,
               hgrn_norm_g=v_hgrn_norm_g, hgrn_w_o=v_hgrn_w_o, ln_mix_g=v_ln_mix_g, ln_mix_b=v_ln_mix_b, ffn_w_in=v_ffn_w_in,
               ffn_w_out=v_ffn_w_out, ln_ffn_g=v_ln_ffn_g, ln_ffn_b=v_ln_ffn_b, ple_w_gate=v_ple_w_gate, ple_w_proj=v_ple_w_proj)
    shapes = {n: wts[n].shape for n, _ in BIG}
    chip = 2 * lax.axis_index("x") + lax.axis_index("y")
    core = lax.axis_index("c").reshape(1).astype(jnp.int32)
    xin, target = x[0], loss_target[0]
    t = xin.shape[0]
    row = lambda a, i: a[i][None]

    bf_shards = {n: wts[n].astype(BF16) for n, _ in BIG}
    lb_sh = hgrn_lb_logits.shape
    lb_bits = lax.bitcast_convert_type(hgrn_lb_logits.reshape(-1), BF16).reshape(-1, PACK_W)
    groups = _gather_groups()
    packed = [_pack_parts(bf_shards, parts) for parts in groups]
    packed[0] = jnp.concatenate([packed[0], lb_bits, jnp.zeros((LB_ROWS - lb_bits.shape[0], PACK_W), BF16)], axis=0)

    gathered = _put(_run_guest(_gather_guest(packed[0]), name="gather_first")[0], packed[0], chip)
    wfull, r_big = _gathered_parts(gathered, groups[0], shapes)
    lb_rows = gathered[:, r_big:r_big + lb_bits.shape[0]].reshape(N_CHIPS, -1, 2)
    lb_full = lax.bitcast_convert_type(lb_rows, F32).reshape(N_CHIPS, lb_sh[0], lb_sh[1], lb_sh[2])
    lb_full = lb_full.transpose(1, 2, 0, 3).reshape(lb_sh[0], lb_sh[1], N_CHIPS * lb_sh[2])
    logits2d = lb_full.reshape(DEPTH, 2 * D_MODEL)
    lb_all = _lower_bounds(logits2d, name="lb_fwd").reshape(DEPTH, 2, 1, D_MODEL)
    tables = _rope_tables(t)

    saved, weights = [], []
    xf, xb = xin, xin.astype(BF16)
    for i in range(DEPTH):
        j = i // 2
        nxt = _gather_guest(packed[i + 1]) if i + 1 < DEPTH else None
        s = dict(xin_bf=xb)
        if i % 2 == 0:
            q, k, v = _qkv_rope(xb, wfull[("att_w_qkv", j)], tables, name=f"qkv_rope_{i}")
            (o, lse), got = _attn_fwd(q, k, v, att_sink[j], name=f"attn_fwd_{i}", guest=nxt)
            s.update(q=q, k=k, v=v, o=o, lse=lse)
            mix_in = o
        else:
            z = _mm_nn(xb, wfull[("hgrn_w_in", j)], out_dtype=F32, name=f"hgrn_in_{i}")
            (o_f, s_f), _ = _hgrn_fwd(z, lb_all[i, 0], False, name=f"hgrn_scan_f_{i}")
            (o_b, s_b), _ = _hgrn_fwd(z, lb_all[i, 1], True, name=f"hgrn_scan_b_{i}")
            og = _hgrn_post_fwd(o_f, o_b, z, row(hgrn_norm_g, j), name=f"hgrn_post_{i}")
            s.update(z=z, o_f=o_f, o_b=o_b, s_f=s_f, s_b=s_b, og=og)
            mix_in = og
        in_mixer = nxt is not None and i % 2 == 0
        if in_mixer:
            wfull.update(_gathered_parts(_put(got[0], packed[i + 1], chip), groups[i + 1], shapes)[0])
        w = {n: wfull[(n, l)] for n, l in _layer_parts(i)}
        w_o = w["att_w_o" if i % 2 == 0 else "hgrn_w_o"]
        x1, x1b, xh1, rs1 = _mm_res_ln(mix_in, w_o, xf, row(ln_mix_g, i), row(ln_mix_b, i), name=f"mix_out_ln_{i}")
        (g, u, h), got = _ffn_in(x1b, w["ffn_w_in"], name=f"ffn_in_{i}", guest=None if in_mixer else nxt)
        if nxt is not None and not in_mixer:
            wfull.update(_gathered_parts(_put(got[0], packed[i + 1], chip), groups[i + 1], shapes)[0])
        x2, x2b, xh2, rs2 = _mm_res_ln(h, w["ffn_w_out"], x1, row(ln_ffn_g, i), row(ln_ffn_b, i), name=f"ffn_out_ln_{i}")
        xf, xb, gate, pp = _ple_fwd(x2, x2b, p, i, w["ple_w_gate"], w["ple_w_proj"], name=f"ple_fwd_{i}")
        s.update(x1b=x1b, xh1=xh1, rs1=rs1, g=g, u=u, h=h, x2b=x2b, xh2=xh2, rs2=rs2, gate=gate, pp=pp)
        saved.append(s)
        weights.append(w)

    loss, dx = _loss_head(xf, target, name="loss_head")
    loss = lax.psum(loss[0, 0], ("x", "y", "c"))

    gs = {n: [None] * DEPTH for n in SMALL}
    gs.update(att_sink=[None] * 2, hgrn_norm_g=[None] * 2)
    dlb = [jnp.zeros((2, D_MODEL), F32)] * DEPTH
    halves = [None] * DEPTH
    slabs = None
    for i in reversed(range(DEPTH)):
        j = i // 2
        s, w = saved[i], weights[i]
        gw = {}
        res, got = _ple_bwd(dx, s["gate"], s["pp"], w["ple_w_gate"], s["xh2"], s["rs2"], row(ln_ffn_g, i), name=f"ple_bwd_{i}",
                            guest=None if slabs is None else _pair_exchange_guest(slabs))
        du2, du2b, dpre, dpp, gs["ln_ffn_g"][i], gs["ln_ffn_b"][i] = res
        part = None if slabs is None else _add_own_half(slabs, got[0], core, name=f"grad_pair_add_{i + 1}")
        gw["ple_w_gate"] = _mm_tn(s["x2b"], dpre, name=f"dw_ple_gate_{i}")
        gw["ple_w_proj"] = _mm_tn_p(p, i, dpp, name=f"dw_ple_proj_{i}")
        dgg, dgu = _ffn_out_bwd(du2b, w["ffn_w_out"], s["g"], s["u"], name=f"ffn_out_bwd_{i}")
        gw["ffn_w_out"] = _mm_tn(s["h"], du2b, name=f"dw_ffn_out_{i}")
        res = _mm_nt([dgg, dgu], w["ffn_w_in"], resid=du2, ln=(s["xh1"], s["rs1"], row(ln_mix_g, i)),
                     name=f"ffn_in_bwd_{i}", guest=None if part is None else _chip_scatter_guest(part))
        (du1, du1b, gs["ln_mix_g"][i], gs["ln_mix_b"][i]), got = res if part is not None else (res, None)
        if part is not None:
            halves[i + 1] = _sum_chips(_put(got[0], _own(part, chip), chip), name=f"grad_chip_sum_{i + 1}")
        gw["ffn_w_in"] = jnp.concatenate(
            [_mm_tn(s["x1b"], dgg, name=f"dw_ffn_gate_{i}"), _mm_tn(s["x1b"], dgu, name=f"dw_ffn_up_{i}")], axis=1)
        if i % 2 == 0:
            gw["att_w_o"] = _mm_tn(s["o"], du1b, name=f"dw_att_o_{i}")
            do = _mm_nt([du1b], w["att_w_o"], out_dtype=BF16, name=f"att_o_bwd_{i}")
            dq, dkw, dvw, gs["att_sink"][j] = _attn_bwd(s["q"], s["k"], s["v"], s["o"], do, s["lse"], att_sink[j], name=f"attn_bwd_{i}")
            dqkv = _attn_post_bwd(dq, dkw, dvw, tables, name=f"attn_post_bwd_{i}")
            gw["att_w_qkv"] = _mm_tn(s["xin_bf"], dqkv, name=f"dw_att_qkv_{i}")
            dx = _mm_nt([dqkv], w["att_w_qkv"], resid=du1, name=f"qkv_bwd_{i}")
        else:
            gw["hgrn_w_o"] = _mm_tn(s["og"], du1b, name=f"dw_hgrn_o_{i}")
            dy = _mm_nt([du1b], w["hgrn_w_o"], out_dtype=BF16, name=f"hgrn_o_bwd_{i}")
            d_o, dgate, gs["hgrn_norm_g"][j] = _hgrn_post_bwd(dy, s["o_f"], s["o_b"], s["z"], row(hgrn_norm_g, j), name=f"hgrn_post_bwd_{i}")
            dq_f, dzf_f, dv_f, dlb_f = _hgrn_bwd(s["z"], lb_all[i, 0], d_o, s["s_f"], False, name=f"hgrn_scan_f_bwd_{i}")
            dq, dzf_b, dv, dlb_b = _hgrn_bwd(s["z"], lb_all[i, 1], d_o, s["s_b"], True, name=f"hgrn_scan_b_bwd_{i}",
                                             other=(dq_f, dv_f))
            dlb[i] = jnp.stack([dlb_f.reshape(D_MODEL), dlb_b.reshape(D_MODEL)])
            dz = [dq, dzf_f, dzf_b, dv, dgate]
            gw["hgrn_w_in"] = jnp.concatenate(
                [_mm_tn(s["xin_bf"], part, name=f"dw_hgrn_in_{i}_{n}") for n, part in enumerate(dz)], axis=1)
            dx = _mm_nt(dz, w["hgrn_w_in"], resid=du1, name=f"hgrn_in_bwd_{i}")
        slabs = _layer_slabs(gw, i, shapes)

    from_sibling = _run_guest(_pair_exchange_guest(slabs), name="grad_pair_exchange_0")[0]
    part = _add_own_half(slabs, from_sibling, core, name="grad_pair_add_0")
    from_chips = _run_guest(_chip_scatter_guest(part), name="grad_chip_scatter_0")[0]
    halves[0] = _sum_chips(_put(from_chips, _own(part, chip), chip), name="grad_chip_sum_0")
    shared = _run_guest(_pair_share_guest(halves), name="grad_pair_share")
    reduced = [_put(buf, half, lax.axis_index("c")) for buf, half in zip(shared, halves)]

    g_layers = [_unpack_layer(reduced[i].reshape(-1, PACK_W), i, shapes) for i in range(DEPTH)]
    grads = {}
    for n, _ in BIG:
        per_layer = [g_layers[i][n] for i in range(DEPTH) if n in g_layers[i]]
        grads[n] = jnp.stack(per_layer)

    gsmall = {n: jnp.concatenate(v, axis=0) for n, v in gs.items()}
    dlogits = _lower_bounds_bwd(logits2d, jnp.stack(dlb).reshape(DEPTH, 2 * D_MODEL), name="lb_bwd").reshape(DEPTH, 2, D_MODEL)
    g_small_full = _unpack_small(_allreduce_small(_pack_small(gsmall, dlogits), name="allreduce_small"),
                                 (lb_sh[0], lb_sh[1], N_CHIPS * lb_sh[2]))
    grads.update({n: g_small_full[n] for n in SMALL + ("att_sink", "hgrn_norm_g")})
    grads["hgrn_lb_logits"] = lax.dynamic_slice_in_dim(g_small_full["hgrn_lb_logits"], chip * lb_sh[2], lb_sh[2], axis=2)

    delta, new_m, new_v = {}, {}, {}
    for n, _ in BIG:
        two_d = lambda a: a.reshape(-1, a.shape[-1])
        d, nm, nv = _adamw(two_d(wts[n]), two_d(grads[n]), two_d(mom[n]), two_d(var[n]), name=f"adamw_{n}")
        delta[n], new_m[n], new_v[n] = d.reshape(shapes[n]), nm.reshape(shapes[n]), nv.reshape(shapes[n])
    packs = [_pack_small(src, src["hgrn_lb_logits"]) for src in (wts, grads, mom, var)]
    outs = _adamw(*packs, name="adamw_small")
    for dst, buf in zip((delta, new_m, new_v), outs):
        dst.update(_unpack_small(buf, lb_sh))

    return (loss, dx[None], *[grads[n] for n in WEIGHTS], *[delta[n] for n in WEIGHTS],
            *[new_m[n] for n in WEIGHTS], *[new_v[n] for n in WEIGHTS])
```

```python
import functools
from typing import Callable, NamedTuple

import jax
import jax.numpy as jnp
from jax import lax
from jax.experimental import pallas as pl
from jax.experimental.pallas import tpu as pltpu

F32, BF16 = jnp.float32, jnp.bfloat16

D_MODEL = 1024
DEPTH = 4
HEAD_DIM = 64
N_Q_HEADS = 16
N_KV_HEADS = 4
GROUP = 4
Q_DIM = 1024
KV_DIM = 256
WINDOW = 128
ROPE_DIM = 16
ROPE_THETA = 500000.0
HGRN_HEADS = 8
HGRN_KEY = 128
HGRN_CHUNK = 64
D_FF = 2816
PLE_DIM = 256
ALPHA = (2 * DEPTH) ** 0.25
LN_EPS = 1e-5
ADAM_LR, ADAM_B1, ADAM_B2, ADAM_EPS, ADAM_WD, ADAM_STEP = 0.001, 0.9, 0.999, 1e-08, 0.01, 10

LANES = 128
VMEM_LIMIT_BYTES = 56 * 2**20
FACTORED_DECAY_LIMIT = -80.0

NN = ((1,), (0,))
NT = ((1,), (1,))
TN = ((0,), (0,))

N_CHIPS = 4
MESH = pl.DeviceIdType.MESH


def _dot(a, b, dims):
    return lax.dot_general(a, b, (dims, ((), ())), preferred_element_type=F32)


def _cp(*sem):
    return pltpu.CompilerParams(dimension_semantics=sem, vmem_limit_bytes=VMEM_LIMIT_BYTES)


def _rows(t, cap=512):
    return min(cap, t)


def _pick(n, cap):
    best = None
    for d in range(LANES, min(n, cap) + 1, LANES):
        if n % d == 0:
            best = d
    assert best is not None, (n, cap)
    return best


def _sigmoid(x):
    return jax.nn.sigmoid(x)


def _ln_fwd(u, g, b):
    mu = jnp.mean(u, axis=-1, keepdims=True)
    xc = u - mu
    var = jnp.mean(xc * xc, axis=-1, keepdims=True)
    rstd = lax.rsqrt(var + LN_EPS)
    xhat = xc * rstd
    return xhat * g + b, xhat, rstd


def _ln_bwd(dy, xhat, rstd, g):
    dxh = dy * g
    m1 = jnp.mean(dxh, axis=-1, keepdims=True)
    m2 = jnp.mean(dxh * xhat, axis=-1, keepdims=True)
    du = rstd * (dxh - m1 - xhat * m2)
    return du, jnp.sum(dy * xhat, axis=0, keepdims=True), jnp.sum(dy, axis=0, keepdims=True)


def _full(shape):
    nd = len(shape)
    return pl.BlockSpec(shape, lambda *_: (0,) * nd)


ANY = pl.BlockSpec(memory_space=pl.ANY)


class _Guest(NamedTuple):
    args: tuple
    out_shape: tuple
    sems: tuple
    start: Callable
    finish: Callable


def _call(body, *, grid, in_specs, out_specs, out_shape, args, sem, name, scratch_shapes=(), guest=None):
    n_in, n_out, n_scr = len(args), len(out_shape), len(scratch_shapes)
    if guest is None:
        res = pl.pallas_call(body, grid=grid, in_specs=list(in_specs), out_specs=list(out_specs), out_shape=list(out_shape),
                             scratch_shapes=list(scratch_shapes), compiler_params=_cp(*sem), name=name)(*args)
        return list(res), []
    g_in, g_out = len(guest.args), len(guest.out_shape)

    def hosted(*refs):
        cuts = [n_in, g_in, n_out, g_out, n_scr]
        parts, pos = [], 0
        for n in cuts:
            parts.append(refs[pos:pos + n])
            pos += n
        h_in, gi, h_out, go, h_scr = parts
        gs = refs[pos:]
        ids = [pl.program_id(d) for d in range(len(grid))]
        first = functools.reduce(jnp.logical_and, [i == 0 for i in ids])
        last = functools.reduce(jnp.logical_and, [i == n - 1 for i, n in zip(ids, grid)])

        @pl.when(first)
        def _():
            guest.start(gi, go, gs)
        body(*h_in, *h_out, *h_scr)

        @pl.when(last)
        def _():
            guest.finish(gi, go, gs)

    res = pl.pallas_call(
        hosted, grid=grid, in_specs=list(in_specs) + [ANY] * g_in, out_specs=list(out_specs) + [ANY] * g_out,
        out_shape=list(out_shape) + list(guest.out_shape), scratch_shapes=list(scratch_shapes) + list(guest.sems),
        compiler_params=_cp(*(("arbitrary",) * len(grid))), name=name)(*args, *guest.args)
    return list(res[:n_out]), list(res[n_out:])


def _run_guest(guest, *, name):
    g_in = len(guest.args)

    def body(*refs):
        gi, go, gs = refs[:g_in], refs[g_in:g_in + len(guest.out_shape)], refs[g_in + len(guest.out_shape):]
        guest.start(gi, go, gs)
        guest.finish(gi, go, gs)

    return pl.pallas_call(body, in_specs=[ANY] * g_in, out_specs=[ANY] * len(guest.out_shape), out_shape=list(guest.out_shape),
                          scratch_shapes=list(guest.sems), name=name)(*guest.args)


def _mm_nn(a, w, *, out_dtype, name):
    t, k = a.shape
    n = w.shape[1]
    tm, tn = _rows(t), _pick(n, 1408)

    def body(a_ref, w_ref, o_ref):
        o_ref[...] = _dot(a_ref[...], w_ref[...], NN).astype(out_dtype)

    return pl.pallas_call(
        body, grid=(n // tn, t // tm),
        in_specs=[pl.BlockSpec((tm, k), lambda j, i: (i, 0)), pl.BlockSpec((k, tn), lambda j, i: (0, j))],
        out_specs=pl.BlockSpec((tm, tn), lambda j, i: (i, j)),
        out_shape=jax.ShapeDtypeStruct((t, n), out_dtype),
        compiler_params=_cp("parallel", "parallel"), name=name)(a, w)


def _mm_tn(a, b, *, name, guest=None):
    r, m = a.shape
    n = b.shape[1]
    tr, tm, tn = _rows(r), _pick(m, 1408), _pick(n, 2816)

    def body(a_ref, b_ref, o_ref):
        @pl.when(pl.program_id(2) == 0)
        def _():
            o_ref[...] = jnp.zeros_like(o_ref)
        o_ref[...] += _dot(a_ref[...].astype(BF16), b_ref[...].astype(BF16), TN)

    res, extra = _call(
        body, grid=(m // tm, n // tn, r // tr),
        in_specs=[pl.BlockSpec((tr, tm), lambda i, j, k: (k, i)), pl.BlockSpec((tr, tn), lambda i, j, k: (k, j))],
        out_specs=[pl.BlockSpec((tm, tn), lambda i, j, k: (i, j))],
        out_shape=[jax.ShapeDtypeStruct((m, n), F32)], args=(a, b),
        sem=("parallel", "parallel", "arbitrary"), name=name, guest=guest)
    return res[0] if guest is None else (res[0], extra)


def _mm_tn_p(p, layer, b, *, name):
    t = p.shape[2]
    n = b.shape[1]
    tr = _rows(t)

    def body(a_ref, b_ref, o_ref):
        @pl.when(pl.program_id(0) == 0)
        def _():
            o_ref[...] = jnp.zeros_like(o_ref)
        o_ref[...] += _dot(a_ref[...].astype(BF16), b_ref[...], TN)

    return pl.pallas_call(
        body, grid=(t // tr,),
        in_specs=[pl.BlockSpec((None, None, tr, PLE_DIM), lambda k: (layer, 0, k, 0)),
                  pl.BlockSpec((tr, n), lambda k: (k, 0))],
        out_specs=pl.BlockSpec((PLE_DIM, n), lambda k: (0, 0)),
        out_shape=jax.ShapeDtypeStruct((PLE_DIM, n), F32),
        compiler_params=_cp("arbitrary"), name=name)(p, b)


def _mm_res_ln(a, w, resid, g, b, *, name):
    t, k = a.shape
    tm = _rows(t)

    def body(a_ref, w_ref, r_ref, g_ref, b_ref, y_ref, ybf_ref, xh_ref, rs_ref):
        acc = _dot(a_ref[...], w_ref[...], NN)
        y, xh, rs = _ln_fwd(ALPHA * r_ref[...] + acc, g_ref[...], b_ref[...])
        y_ref[...] = y
        ybf_ref[...] = y.astype(BF16)
        xh_ref[...] = xh
        rs_ref[...] = rs

    row = pl.BlockSpec((tm, D_MODEL), lambda i: (i, 0))
    return pl.pallas_call(
        body, grid=(t // tm,),
        in_specs=[pl.BlockSpec((tm, k), lambda i: (i, 0)), _full((k, D_MODEL)), row, _full((1, D_MODEL)), _full((1, D_MODEL))],
        out_specs=[row, row, row, pl.BlockSpec((tm, 1), lambda i: (i, 0))],
        out_shape=[jax.ShapeDtypeStruct((t, D_MODEL), F32), jax.ShapeDtypeStruct((t, D_MODEL), BF16),
                   jax.ShapeDtypeStruct((t, D_MODEL), F32), jax.ShapeDtypeStruct((t, 1), F32)],
        compiler_params=_cp("parallel"), name=name)(a, w, resid, g, b)


def _ffn_in(x_bf, w, *, name, guest=None):
    t = x_bf.shape[0]
    tm, tn = _rows(t), _pick(D_FF, 1408)
    nb = D_FF // tn

    def body(x_ref, wg_ref, wu_ref, dg_ref, du_ref, h_ref):
        x = x_ref[...]
        g = _dot(x, wg_ref[...], NN)
        u = _dot(x, wu_ref[...], NN)
        sig = _sigmoid(g)
        silu = g * sig
        dg_ref[...] = (u * sig * (1.0 + g * (1.0 - sig))).astype(BF16)
        du_ref[...] = silu.astype(BF16)
        h_ref[...] = (silu * u).astype(BF16)

    tile = pl.BlockSpec((tm, tn), lambda j, i: (i, j))
    shp = jax.ShapeDtypeStruct((t, D_FF), BF16)
    return _call(
        body, grid=(nb, t // tm),
        in_specs=[pl.BlockSpec((tm, D_MODEL), lambda j, i: (i, 0)),
                  pl.BlockSpec((D_MODEL, tn), lambda j, i: (0, j)),
                  pl.BlockSpec((D_MODEL, tn), lambda j, i: (0, j + nb))],
        out_specs=[tile, tile, tile], out_shape=[shp, shp, shp], args=(x_bf, w, w),
        sem=("parallel", "parallel"), name=name, guest=guest)


def _ple_fwd(x, x_bf, p, layer, wg, wp, *, name):
    t = x.shape[0]
    tm = _rows(t)

    def body(x_ref, xbf_ref, p_ref, wg_ref, wp_ref, y_ref, ybf_ref, gate_ref, pp_ref):
        gate = _sigmoid(_dot(xbf_ref[...], wg_ref[...], NN))
        pp = _dot(p_ref[...].astype(BF16), wp_ref[...], NN)
        y = x_ref[...] + gate * pp
        y_ref[...] = y
        ybf_ref[...] = y.astype(BF16)
        gate_ref[...] = gate.astype(BF16)
        pp_ref[...] = pp.astype(BF16)

    row = pl.BlockSpec((tm, D_MODEL), lambda i: (i, 0))
    f32, bf = jax.ShapeDtypeStruct((t, D_MODEL), F32), jax.ShapeDtypeStruct((t, D_MODEL), BF16)
    return pl.pallas_call(
        body, grid=(t // tm,),
        in_specs=[row, row, pl.BlockSpec((None, None, tm, PLE_DIM), lambda i: (layer, 0, i, 0)),
                  _full((D_MODEL, D_MODEL)), _full((PLE_DIM, D_MODEL))],
        out_specs=[row, row, row, row], out_shape=[f32, bf, bf, bf],
        compiler_params=_cp("parallel"), name=name)(x, x_bf, p, wg, wp)


def _loss_head(y, target, *, name):
    t = y.shape[0]
    tm = _rows(t)

    def body(y_ref, t_ref, loss_ref, dy_ref):
        e = y_ref[...] - t_ref[...]

        @pl.when(pl.program_id(0) == 0)
        def _():
            loss_ref[...] = jnp.zeros_like(loss_ref)
        sq = jnp.sum(jnp.sum(e * e, axis=1, keepdims=True), axis=0, keepdims=True)
        loss_ref[...] += sq * (0.5 / D_MODEL)
        dy_ref[...] = e * (1.0 / D_MODEL)

    row = pl.BlockSpec((tm, D_MODEL), lambda i: (i, 0))
    return pl.pallas_call(
        body, grid=(t // tm,), in_specs=[row, row],
        out_specs=[_full((1, 1)), row],
        out_shape=[jax.ShapeDtypeStruct((1, 1), F32), jax.ShapeDtypeStruct((t, D_MODEL), F32)],
        compiler_params=_cp("arbitrary"), name=name)(y, target)


def _ple_bwd(dx3, gate, pp, wg, xhat, rstd, g, *, name, guest=None):
    t = dx3.shape[0]
    tm = _rows(t)

    def body(dx_ref, gate_ref, pp_ref, wg_ref, xh_ref, rs_ref, g_ref,
             du_ref, dubf_ref, dpre_ref, dpp_ref, dg_ref, db_ref):
        dx = dx_ref[...]
        gate = gate_ref[...].astype(F32)
        dpre = (dx * pp_ref[...].astype(F32) * gate * (1.0 - gate)).astype(BF16)
        dpre_ref[...] = dpre
        dpp_ref[...] = (dx * gate).astype(BF16)
        dx2 = dx + _dot(dpre, wg_ref[...], NT)
        du, dg, db = _ln_bwd(dx2, xh_ref[...], rs_ref[...], g_ref[...])
        du_ref[...] = du
        dubf_ref[...] = du.astype(BF16)

        @pl.when(pl.program_id(0) == 0)
        def _():
            dg_ref[...] = jnp.zeros_like(dg_ref)
            db_ref[...] = jnp.zeros_like(db_ref)
        dg_ref[...] += dg
        db_ref[...] += db

    row = pl.BlockSpec((tm, D_MODEL), lambda i: (i, 0))
    vec = _full((1, D_MODEL))
    f32, bf = jax.ShapeDtypeStruct((t, D_MODEL), F32), jax.ShapeDtypeStruct((t, D_MODEL), BF16)
    v32 = jax.ShapeDtypeStruct((1, D_MODEL), F32)
    res, extra = _call(
        body, grid=(t // tm,),
        in_specs=[row, row, row, _full((D_MODEL, D_MODEL)), row, pl.BlockSpec((tm, 1), lambda i: (i, 0)), vec],
        out_specs=[row, row, row, row, vec, vec], out_shape=[f32, bf, bf, bf, v32, v32],
        args=(dx3, gate, pp, wg, xhat, rstd, g), sem=("arbitrary",), name=name, guest=guest)
    return res, extra


def _ffn_out_bwd(du_bf, w_out, g, u, *, name):
    t = du_bf.shape[0]
    tm, tn = _rows(t), _pick(D_FF, 1408)

    def body(du_ref, w_ref, hg_ref, hu_ref, dg_ref, dup_ref):
        dh = _dot(du_ref[...], w_ref[...], NT)
        dg_ref[...] = (dh * hg_ref[...].astype(F32)).astype(BF16)
        dup_ref[...] = (dh * hu_ref[...].astype(F32)).astype(BF16)

    tile = pl.BlockSpec((tm, tn), lambda j, i: (i, j))
    shp = jax.ShapeDtypeStruct((t, D_FF), BF16)
    return pl.pallas_call(
        body, grid=(D_FF // tn, t // tm),
        in_specs=[pl.BlockSpec((tm, D_MODEL), lambda j, i: (i, 0)), pl.BlockSpec((tn, D_MODEL), lambda j, i: (j, 0)),
                  tile, tile],
        out_specs=[tile, tile], out_shape=[shp, shp],
        compiler_params=_cp("parallel", "parallel"), name=name)(du_bf, w_out, g, u)


def _mm_nt(parts, w, *, resid=None, ln=None, out_dtype=F32, name, guest=None):
    t, kp = parts[0].shape
    npart = len(parts)
    tm = _rows(t)
    n_in = npart + 1 + (resid is not None) + (3 if ln is not None else 0)

    def body(*refs):
        a_refs, w_ref = refs[:npart], refs[npart]
        pos = npart + 1
        r_ref = None
        if resid is not None:
            r_ref = refs[pos]
            pos += 1
        if ln is not None:
            xh_ref, rs_ref, g_ref = refs[pos:pos + 3]
        outs = refs[n_in:]
        val = _dot(a_refs[0][...], w_ref[:, :kp], NT)
        for pi in range(1, npart):
            val = val + _dot(a_refs[pi][...], w_ref[:, pi * kp:(pi + 1) * kp], NT)
        if r_ref is not None:
            val = val + ALPHA * r_ref[...]
        if ln is None:
            outs[0][...] = val.astype(out_dtype)
        else:
            du, dg, db = _ln_bwd(val, xh_ref[...], rs_ref[...], g_ref[...])
            outs[0][...] = du
            outs[1][...] = du.astype(BF16)

            @pl.when(pl.program_id(0) == 0)
            def _():
                outs[2][...] = jnp.zeros_like(outs[2])
                outs[3][...] = jnp.zeros_like(outs[3])
            outs[2][...] += dg
            outs[3][...] += db

    row = pl.BlockSpec((tm, D_MODEL), lambda i: (i, 0))
    vec = pl.BlockSpec((1, D_MODEL), lambda i: (0, 0))
    in_specs = [pl.BlockSpec((tm, kp), lambda i: (i, 0)) for _ in range(npart)]
    in_specs.append(pl.BlockSpec((D_MODEL, npart * kp), lambda i: (0, 0), pipeline_mode=pl.Buffered(1)))
    args = list(parts) + [w]
    if resid is not None:
        in_specs.append(row)
        args.append(resid)
    if ln is not None:
        in_specs += [row, pl.BlockSpec((tm, 1), lambda i: (i, 0)), vec]
        args += list(ln)
        out_specs = [row, row, vec, vec]
        out_shape = [jax.ShapeDtypeStruct((t, D_MODEL), F32), jax.ShapeDtypeStruct((t, D_MODEL), BF16),
                     jax.ShapeDtypeStruct((1, D_MODEL), F32), jax.ShapeDtypeStruct((1, D_MODEL), F32)]
    else:
        out_specs = [row]
        out_shape = [jax.ShapeDtypeStruct((t, D_MODEL), out_dtype)]
    res, extra = _call(
        body, grid=(t // tm,), in_specs=in_specs, out_specs=out_specs, out_shape=out_shape, args=tuple(args),
        sem=("arbitrary" if ln is not None else "parallel",), name=name, guest=guest)
    res = res if ln is not None else res[0]
    return res if guest is None else (res, extra)


def _rope_tables(t):
    half = ROPE_DIM // 2
    inv = ROPE_THETA ** (-jnp.arange(0, ROPE_DIM, 2, dtype=F32) / ROPE_DIM)
    ang = jnp.arange(t, dtype=F32)[:, None] * inv[None, :]
    cos, sin = jnp.cos(ang), jnp.sin(ang)
    pad = HEAD_DIM - ROPE_DIM
    c = jnp.concatenate([cos, cos, jnp.ones((t, pad), F32)], axis=1)
    s_hi = jnp.concatenate([jnp.zeros((t, half), F32), sin, jnp.zeros((t, pad), F32)], axis=1)
    s_lo = jnp.concatenate([-sin, jnp.zeros((t, half + pad), F32)], axis=1)
    rep = LANES // HEAD_DIM
    return jnp.tile(c, (1, rep)), jnp.tile(s_hi, (1, rep)), jnp.tile(s_lo, (1, rep))


def _rope(x, c, s_hi, s_lo, sign):
    half = ROPE_DIM // 2
    parts = []
    for j in range(x.shape[1] // LANES):
        xg = x[:, j * LANES:(j + 1) * LANES]
        rot = pltpu.roll(xg, half, 1) * s_hi + pltpu.roll(xg, LANES - half, 1) * s_lo
        parts.append(xg * c + sign * rot)
    return jnp.concatenate(parts, axis=1)


def _qkv_rope(x_bf, w, tables, *, name):
    t = x_bf.shape[0]
    tm = _rows(t)
    n = Q_DIM + 2 * KV_DIM

    def body(x_ref, w_ref, c_ref, sh_ref, sl_ref, q_ref, k_ref, v_ref):
        acc = _dot(x_ref[...], w_ref[...], NN)
        c, sh, sl = c_ref[...], sh_ref[...], sl_ref[...]
        q_ref[...] = (_rope(acc[:, :Q_DIM], c, sh, sl, 1.0) * HEAD_DIM ** -0.5).astype(BF16)
        k_ref[...] = _rope(acc[:, Q_DIM:Q_DIM + KV_DIM], c, sh, sl, 1.0).astype(BF16)
        v_ref[...] = acc[:, Q_DIM + KV_DIM:].astype(BF16)

    tab = pl.BlockSpec((tm, LANES), lambda i: (i, 0))
    return pl.pallas_call(
        body, grid=(t // tm,),
        in_specs=[pl.BlockSpec((tm, D_MODEL), lambda i: (i, 0)), _full((D_MODEL, n)), tab, tab, tab],
        out_specs=[pl.BlockSpec((tm, Q_DIM), lambda i: (i, 0)), pl.BlockSpec((tm, KV_DIM), lambda i: (i, 0)),
                   pl.BlockSpec((tm, KV_DIM), lambda i: (i, 0))],
        out_shape=[jax.ShapeDtypeStruct((t, Q_DIM), BF16), jax.ShapeDtypeStruct((t, KV_DIM), BF16),
                   jax.ShapeDtypeStruct((t, KV_DIM), BF16)],
        compiler_params=_cp("parallel"), name=name)(x_bf, w, *tables)


ATT_ROWS = 256


def _window_specs(t, tq):
    r = tq // WINDOW
    last = t // WINDOW - 1
    return [pl.BlockSpec((WINDOW, KV_DIM), lambda i: (jnp.maximum(i * r - 1, 0), 0)),
            pl.BlockSpec((tq, KV_DIM), lambda i: (i, 0)),
            pl.BlockSpec((WINDOW, KV_DIM), lambda i: (jnp.minimum((i + 1) * r, last), 0))]


def _att_valid(base, t):
    rows = GROUP * WINDOW
    qpos = base + (lax.broadcasted_iota(jnp.int32, (rows, 3 * WINDOW), 0) & (WINDOW - 1))
    kpos = base - WINDOW + lax.broadcasted_iota(jnp.int32, (rows, 3 * WINDOW), 1)
    return (jnp.abs(qpos - kpos) <= WINDOW) & (kpos >= 0) & (kpos < t)


def _stack_heads(x, r0, g):
    return jnp.concatenate(
        [x[r0:r0 + WINDOW, (GROUP * g + h) * HEAD_DIM:(GROUP * g + h + 1) * HEAD_DIM] for h in range(GROUP)], axis=0)


def _sink_column(sink_ref, g):
    return jnp.concatenate([jnp.full((WINDOW, 1), sink_ref[GROUP * g + h], F32) for h in range(GROUP)], axis=0)


def _unstack_heads(pieces):
    return jnp.concatenate([pieces[g][h * WINDOW:(h + 1) * WINDOW] for g in range(N_KV_HEADS) for h in range(GROUP)], axis=1)


def _attn_fwd(q, k, v, sink, *, name, guest=None):
    t = q.shape[0]
    tq = min(ATT_ROWS, t)
    nsb = tq // WINDOW

    def body(sink_ref, q_ref, kp_ref, kc_ref, kn_ref, vp_ref, vc_ref, vn_ref, o_ref, l_ref):
        i = pl.program_id(0)
        qv = q_ref[...]
        kw = jnp.concatenate([kp_ref[...], kc_ref[...], kn_ref[...]], axis=0)
        vw = jnp.concatenate([vp_ref[...], vc_ref[...], vn_ref[...]], axis=0)
        ones = jnp.ones((3 * WINDOW, HEAD_DIM), BF16)
        for sb in range(nsb):
            r0 = sb * WINDOW
            bias =jnp.where(_att_valid(i * tq + r0, t)[:WINDOW], 0.0, -1e30)
            pieces = []
            for hh in range(N_Q_HEADS):
                g, h = hh // GROUP, hh % GROUP
                qh = qv[r0:r0 + WINDOW, hh * HEAD_DIM:(hh + 1) * HEAD_DIM]
                kg = kw[r0:r0 + 3 * WINDOW, g * HEAD_DIM:(g + 1) * HEAD_DIM]
                vg = jnp.concatenate([vw[r0:r0 + 3 * WINDOW, g * HEAD_DIM:(g + 1) * HEAD_DIM], ones], axis=1)
                s = _dot(qh, kg, NT) + bias
                snk = sink_ref[hh]
                m = jnp.maximum(jnp.max(s, axis=1, keepdims=True), snk)
                ov = _dot(jnp.exp(s - m).astype(BF16), vg, NN)
                den = ov[:, HEAD_DIM:HEAD_DIM + 1] + jnp.exp(snk - m)
                pieces.append(ov[:, :HEAD_DIM] * (1.0 / den))
                l_ref[g, sb, h * WINDOW:(h + 1) * WINDOW, :] = m + jnp.log(den)
            o_ref[r0:r0 + WINDOW, :] = jnp.concatenate(pieces, axis=1).astype(BF16)

    return _call(
        body, grid=(t // tq,),
        in_specs=[pl.BlockSpec(memory_space=pltpu.SMEM), pl.BlockSpec((tq, Q_DIM), lambda i: (i, 0))]
        + _window_specs(t, tq) + _window_specs(t, tq),
        out_specs=[pl.BlockSpec((tq, Q_DIM), lambda i: (i, 0)),
                   pl.BlockSpec((N_KV_HEADS, nsb, GROUP * WINDOW, 1), lambda i: (0, i, 0, 0))],
        out_shape=[jax.ShapeDtypeStruct((t, Q_DIM), BF16),
                   jax.ShapeDtypeStruct((N_KV_HEADS, t // WINDOW, GROUP * WINDOW, 1), F32)],
        args=(sink, q, k, k, k, v, v, v), sem=("parallel",), name=name, guest=guest)


def _attn_bwd(q, k, v, o, do, lse, sink, *, name):
    t = q.shape[0]
    tq = min(ATT_ROWS, t)
    nsb = tq // WINDOW

    def body(sink_ref, q_ref, o_ref, do_ref, l_ref, kp_ref, kc_ref, kn_ref, vp_ref, vc_ref, vn_ref,
             dq_ref, dkw_ref, dvw_ref, dsink_ref):
        i = pl.program_id(0)
        qv, ov, dov = q_ref[...], o_ref[...], do_ref[...]
        kw = jnp.concatenate([kp_ref[...], kc_ref[...], kn_ref[...]], axis=0)
        vw = jnp.concatenate([vp_ref[...], vc_ref[...], vn_ref[...]], axis=0)
        lane16 = lax.broadcasted_iota(jnp.int32, (1, N_Q_HEADS), 1)
        dsink = jnp.zeros((1, N_Q_HEADS), F32)
        for sb in range(nsb):
            r0 = sb * WINDOW
            valid = _att_valid(i * tq + r0, t)
            dq_p, dk_p, dv_p = [], [], []
            for g in range(N_KV_HEADS):
                qs, dos = _stack_heads(qv, r0, g), _stack_heads(dov, r0, g)
                delta = jnp.sum(dos.astype(F32) * _stack_heads(ov, r0, g).astype(F32), axis=1, keepdims=True)
                kg = kw[r0:r0 + 3 * WINDOW, g * HEAD_DIM:(g + 1) * HEAD_DIM]
                vg = vw[r0:r0 + 3 * WINDOW, g * HEAD_DIM:(g + 1) * HEAD_DIM]
                lse_col = l_ref[g, sb]
                pr = jnp.where(valid, jnp.exp(_dot(qs, kg, NT) - lse_col), 0.0)
                ds = (pr * (_dot(dos, vg, NT) - delta)).astype(BF16)
                dq_p.append(_dot(ds, kg, NN))
                dk_p.append(_dot(ds, qs, TN))
                dv_p.append(_dot(pr.astype(BF16), dos, TN))
                ps = jnp.exp(_sink_column(sink_ref, g) - lse_col) * delta
                for h in range(GROUP):
                    tot = jnp.sum(ps[h * WINDOW:(h + 1) * WINDOW], axis=0, keepdims=True)
                    dsink = dsink - jnp.where(lane16 == GROUP * g + h, tot, 0.0)
            dq_ref[r0:r0 + WINDOW, :] = _unstack_heads(dq_p)
            dkw_ref[sb] = jnp.concatenate(dk_p, axis=1)
            dvw_ref[sb] = jnp.concatenate(dv_p, axis=1)

        @pl.when(i == 0)
        def _():
            dsink_ref[...] = jnp.zeros_like(dsink_ref)
        dsink_ref[...] += dsink

    rowq = pl.BlockSpec((tq, Q_DIM), lambda i: (i, 0))
    win = pl.BlockSpec((nsb, 3 * WINDOW, KV_DIM), lambda i: (i, 0, 0))
    wshape = jax.ShapeDtypeStruct((t // WINDOW, 3 * WINDOW, KV_DIM), F32)
    return pl.pallas_call(
        body, grid=(t // tq,),
        in_specs=[pl.BlockSpec(memory_space=pltpu.SMEM), rowq, rowq, rowq,
                  pl.BlockSpec((N_KV_HEADS, nsb, GROUP * WINDOW, 1), lambda i: (0, i, 0, 0))]
        + _window_specs(t, tq) + _window_specs(t, tq),
        out_specs=[rowq, win, win, _full((1, N_Q_HEADS))],
        out_shape=[jax.ShapeDtypeStruct((t, Q_DIM), F32), wshape, wshape, jax.ShapeDtypeStruct((1, N_Q_HEADS), F32)],
        compiler_params=_cp("arbitrary"), name=name)(sink, q, o, do, lse, k, k, k, v, v, v)


def _attn_post_bwd(dq, dkw, dvw, tables, *, name):
    t = dq.shape[0]
    nb = t // WINDOW
    n = Q_DIM + 2 * KV_DIM

    def body(dq_ref, ka_ref, kb_ref, kc_ref, va_ref, vb_ref, vc_ref, c_ref, sh_ref, sl_ref, o_ref):
        j = pl.program_id(0)
        lo = (j > 0).astype(F32)
        hi = (j < nb - 1).astype(F32)
        c, sh, sl = c_ref[...], sh_ref[...], sl_ref[...]
        dk = ka_ref[...] * hi + kb_ref[...] + kc_ref[...] * lo
        dv = va_ref[...] * hi + vb_ref[...] + vc_ref[...] * lo
        o_ref[:, :Q_DIM] = (_rope(dq_ref[...], c, sh, sl, -1.0) * HEAD_DIM ** -0.5).astype(BF16)
        o_ref[:, Q_DIM:Q_DIM + KV_DIM] = _rope(dk, c, sh, sl, -1.0).astype(BF16)
        o_ref[:, Q_DIM + KV_DIM:] = dv.astype(BF16)

    wins = [pl.BlockSpec((None, WINDOW, KV_DIM), lambda j: (jnp.minimum(j + 1, nb - 1), 0, 0)),
            pl.BlockSpec((None, WINDOW, KV_DIM), lambda j: (j, 1, 0)),
            pl.BlockSpec((None, WINDOW, KV_DIM), lambda j: (jnp.maximum(j - 1, 0), 2, 0))]
    tab = pl.BlockSpec((WINDOW, LANES), lambda j: (j, 0))
    return pl.pallas_call(
        body, grid=(nb,),
        in_specs=[pl.BlockSpec((WINDOW, Q_DIM), lambda j: (j, 0))] + wins + wins + [tab, tab, tab],
        out_specs=pl.BlockSpec((WINDOW, n), lambda j: (j, 0)),
        out_shape=jax.ShapeDtypeStruct((t, n), BF16),
        compiler_params=_cp("parallel"), name=name)(dq, dkw, dkw, dkw, dvw, dvw, dvw, *tables)


HGRN_ROWS = 512
HGRN_UNROLL = 2
HGRN_UNROLL_FWD = 4


def _cum_mask(rev):
    row = lax.broadcasted_iota(jnp.int32, (HGRN_CHUNK, HGRN_CHUNK), 0)
    col = lax.broadcasted_iota(jnp.int32, (HGRN_CHUNK, HGRN_CHUNK), 1)
    return (row <= col) if rev else (row >= col)


def _gates(zq, zf, lb):
    q = zq * _sigmoid(zq)
    sig = _sigmoid(zf)
    f = lb + (1.0 - lb) * sig
    k = (1.0 - lb) * (1.0 - sig)
    return q, sig, f, k


def _intra_exact(q, k, b, mask):
    rows = lax.broadcasted_iota(jnp.int32, (HGRN_CHUNK, 1), 0)
    cols = lax.broadcasted_iota(jnp.int32, (HGRN_CHUNK, HGRN_CHUNK), 1)

    def it(s, a):
        pick = rows == s
        b_s = jnp.sum(jnp.where(pick, b, 0.0), axis=0, keepdims=True)
        k_s = jnp.sum(jnp.where(pick, k, 0.0), axis=0, keepdims=True)
        col = jnp.sum(q * jnp.exp(jnp.minimum(b - b_s, 0.0)) * k_s, axis=1, keepdims=True)
        return jnp.where(cols == s, col, a)

    a = lax.fori_loop(0, HGRN_CHUNK, it, jnp.zeros((HGRN_CHUNK, HGRN_CHUNK), F32))
    return jnp.where(mask, a, 0.0)


def _intra_exact_bwd(q, k, b, da):
    rows = lax.broadcasted_iota(jnp.int32, (HGRN_CHUNK, 1), 0)
    cols = lax.broadcasted_iota(jnp.int32, (HGRN_CHUNK, HGRN_CHUNK), 1)

    def it(s, carry):
        dq, dk = carry
        pick = rows == s
        b_s = jnp.sum(jnp.where(pick, b, 0.0), axis=0, keepdims=True)
        k_s = jnp.sum(jnp.where(pick, k, 0.0), axis=0, keepdims=True)
        w = jnp.sum(jnp.where(cols == s, da, 0.0), axis=1, keepdims=True) * jnp.exp(jnp.minimum(b - b_s, 0.0))
        dq = dq + w * k_s
        dk = jnp.where(pick, jnp.sum(w * q, axis=0, keepdims=True), dk)
        return dq, dk

    z = jnp.zeros((HGRN_CHUNK, HGRN_KEY), F32)
    return lax.fori_loop(0, HGRN_CHUNK, it, (z, z))


def _chunk_cumsum(g, from_end):
    n = g.shape[0]
    row = lax.broadcasted_iota(jnp.int32, g.shape, 0)
    x, s = g, 1
    while s < n:
        if from_end:
            x = x + jnp.where(row < n - s, pltpu.roll(x, n - s, 0), 0.0)
        else:
            x = x + jnp.where(row >= s, pltpu.roll(x, s, 0), 0.0)
        s *= 2
    return x


def _head(a, h):
    return a[:, h * LANES:(h + 1) * LANES]


def _block_is_mild(zf_ref, lb, nch):
    g = jnp.log(lb + (1.0 - lb) * _sigmoid(zf_ref[...]))
    lows = [jnp.min(jnp.sum(g[c * HGRN_CHUNK:(c + 1) * HGRN_CHUNK], axis=0, keepdims=True)) for c in range(nch)]
    return functools.reduce(jnp.minimum, lows) >= FACTORED_DECAY_LIMIT


def _hgrn_fwd(z, lb, rev, *, name, guest=None):
    t = z.shape[0]
    rb = min(HGRN_ROWS, t)
    nb, nch = t // rb, rb // HGRN_CHUNK
    heads = range(HGRN_HEADS)

    def blk(n):
        return nb - 1 - n if rev else n

    def body(zq_ref, zf_ref, zv_ref, lb_ref, o_ref, s_ref, st_ref):
        @pl.when(pl.program_id(0) == 0)
        def _():
            st_ref[...] = jnp.zeros_like(st_ref)
        lbv = lb_ref[...]
        cmask = _cum_mask(rev)

        def chunk(c, carry, mild):
            cc = nch - 1 - c if rev else c
            sl = pl.ds(pl.multiple_of(cc * HGRN_CHUNK, HGRN_CHUNK), HGRN_CHUNK)
            q, _, f, k = _gates(zq_ref[sl, :], zf_ref[sl, :], lbv)
            v = zv_ref[sl, :].astype(BF16)
            g = jnp.log(f)
            b = _chunk_cumsum(g, rev)
            tot = jnp.sum(g, axis=0, keepdims=True)
            qd = (q * jnp.exp(b)).astype(BF16)
            kd = (k * jnp.exp(tot - b)).astype(BF16)
            etot = jnp.exp(tot)

            def factored():
                kf = (k * jnp.exp(-b)).astype(BF16)
                return tuple(jnp.where(cmask, _dot(_head(qd, h), _head(kf, h), NT), 0.0) for h in heads)

            def exact():
                return tuple(_intra_exact(_head(q, h), _head(k, h), _head(b, h), cmask) for h in heads)

            a = factored() if mild else lax.cond(jnp.min(tot) >= FACTORED_DECAY_LIMIT, factored, exact)
            for h in heads:
                st = st_ref[h]
                st_bf = st.astype(BF16)
                s_ref[h, cc] = st_bf
                o_ref[sl, h * LANES:(h + 1) * LANES] = (
                    _dot(_head(qd, h), st_bf, NT) + _dot(a[h].astype(BF16), _head(v, h), NN))
                st_ref[h] = st * _head(etot, h) + _dot(_head(v, h), _head(kd, h), TN)
            return carry

        lax.cond(_block_is_mild(zf_ref, lbv, nch),
                 lambda: lax.fori_loop(0, nch, functools.partial(chunk, mild=True), 0, unroll=HGRN_UNROLL_FWD),
                 lambda: lax.fori_loop(0, nch, functools.partial(chunk, mild=False), 0))

    def col(j):
        return pl.BlockSpec((rb, D_MODEL), lambda n: (blk(n), j))

    return _call(
        body, grid=(nb,),
        in_specs=[col(0), col(2 if rev else 1), col(3), _full((1, D_MODEL))],
        out_specs=[col(0), pl.BlockSpec((HGRN_HEADS, nch, LANES, LANES), lambda n: (0, blk(n), 0, 0))],
        out_shape=[jax.ShapeDtypeStruct((t, D_MODEL), F32),
                   jax.ShapeDtypeStruct((HGRN_HEADS, t // HGRN_CHUNK, LANES, LANES), BF16)],
        scratch_shapes=[pltpu.VMEM((HGRN_HEADS, LANES, LANES), F32)],
        args=(z, z, z, lb), sem=("arbitrary",), name=name, guest=guest)


def _hgrn_bwd(z, lb, d_o, states, rev, *, name, other=None):
    t = z.shape[0]
    rb = min(HGRN_ROWS, t)
    nb, nch = t // rb, rb // HGRN_CHUNK
    heads = range(HGRN_HEADS)
    n_other = 0 if other is None else 2
    acc_dtype = F32 if other is None else BF16

    def blk(n):
        return n if rev else nb - 1 - n

    def body(zq_ref, zf_ref, zv_ref, lb_ref, do_ref, s_ref, *rest):
        oq_ref, ov_ref = rest[:n_other] if other is not None else (None, None)
        dq_ref, dzf_ref, dv_ref, dlb_ref, dst_ref = rest[n_other:]

        @pl.when(pl.program_id(0) == 0)
        def _():
            dst_ref[...] = jnp.zeros_like(dst_ref)
            dlb_ref[...] = jnp.zeros_like(dlb_ref)
        lbv = lb_ref[...]
        cmask = _cum_mask(rev)

        def chunk(c, carry, mild):
            cc = c if rev else nch - 1 - c
            sl = pl.ds(pl.multiple_of(cc * HGRN_CHUNK, HGRN_CHUNK), HGRN_CHUNK)
            zq = zq_ref[sl, :]
            q, sig, f, k = _gates(zq, zf_ref[sl, :], lbv)
            v = zv_ref[sl, :].astype(BF16)
            g = jnp.log(f)
            b = _chunk_cumsum(g, rev)
            tot = jnp.sum(g, axis=0, keepdims=True)
            eb = jnp.exp(b)
            etb = jnp.exp(tot - b)
            etot = jnp.exp(tot)
            qd = (q * eb).astype(BF16)
            kd = (k * etb).astype(BF16)
            do = do_ref[sl, :].astype(BF16)
            da = [jnp.where(cmask, _dot(_head(do, h), _head(v, h), NT), 0.0) for h in heads]

            def factored():
                ke = jnp.exp(-b)
                kf = (k * ke).astype(BF16)
                res = []
                for h in heads:
                    qh, kh = _head(qd, h), _head(kf, h)
                    da_bf = da[h].astype(BF16)
                    dqf, dkf = _dot(da_bf, kh, NN), _dot(da_bf, qh, TN)
                    res.append((jnp.where(cmask, _dot(qh, kh, NT), 0.0), dqf * _head(eb, h), dkf * _head(ke, h),
                                qh.astype(F32) * dqf - kh.astype(F32) * dkf))
                return tuple(res)

            def exact():
                res = []
                for h in heads:
                    qh, kh, bh = _head(q, h), _head(k, h), _head(b, h)
                    dq_i, dk_i = _intra_exact_bwd(qh, kh, bh, da[h])
                    res.append((_intra_exact(qh, kh, bh, cmask), dq_i, dk_i, qh * dq_i - kh * dk_i))
                return tuple(res)

            intra = factored() if mild else lax.cond(jnp.min(tot) >= FACTORED_DECAY_LIMIT, factored, exact)
            dq_p, dk_p, db_p, dtot_p = [], [], [], []
            for h in heads:
                a, dq_i, dk_i, db_i = intra[h]
                doh, vh, qh, kh = _head(do, h), _head(v, h), _head(q, h), _head(k, h)
                st0 = s_ref[h, cc]
                dst = dst_ref[h]
                dst_bf = dst.astype(BF16)
                dv = _dot(a.astype(BF16), doh, TN) + _dot(_head(kd, h), dst_bf, NT)
                if ov_ref is not None:
                    dv = dv + ov_ref[sl, h * LANES:(h + 1) * LANES]
                dv_ref[sl, h * LANES:(h + 1) * LANES] = dv.astype(acc_dtype)
                dq_x = _dot(doh, st0, NN) * _head(eb, h)
                dk_x = _dot(vh, dst_bf, NN) * _head(etb, h)
                dtot_p.append(jnp.sum(kh * dk_x, axis=0, keepdims=True)
                              + _head(etot, h) * jnp.sum(dst * st0.astype(F32), axis=0, keepdims=True))
                dst_ref[h] = dst * _head(etot, h) + _dot(doh, _head(qd, h), TN)
                dq_p.append(dq_i + dq_x)
                dk_p.append(dk_i + dk_x)
                db_p.append(db_i + qh * dq_x - kh * dk_x)
            dq, dk, db = (jnp.concatenate(x, axis=1) for x in (dq_p, dk_p, db_p))
            dg = _chunk_cumsum(db, not rev) + jnp.concatenate(dtot_p, axis=1)
            sq = _sigmoid(zq)
            dq_pre = dq * sq * (1.0 + zq * (1.0 - sq))
            if oq_ref is not None:
                dq_pre = dq_pre + oq_ref[sl, :]
            dq_ref[sl, :] = dq_pre.astype(acc_dtype)
            dfk = dg / f - dk
            dzf_ref[sl, :] = (dfk * (1.0 - lbv) * sig * (1.0 - sig)).astype(BF16)
            dlb_ref[...] += jnp.sum(dfk * (1.0 - sig), axis=0, keepdims=True)
            return carry

        lax.cond(_block_is_mild(zf_ref, lbv, nch),
                 lambda: lax.fori_loop(0, nch, functools.partial(chunk, mild=True), 0, unroll=HGRN_UNROLL),
                 lambda: lax.fori_loop(0, nch, functools.partial(chunk, mild=False), 0))

    def col(j):
        return pl.BlockSpec((rb, D_MODEL), lambda n: (blk(n), j))

    return pl.pallas_call(
        body, grid=(nb,),
        in_specs=[col(0), col(2 if rev else 1), col(3), _full((1, D_MODEL)), col(0),
                  pl.BlockSpec((HGRN_HEADS, nch, LANES, LANES), lambda n: (0, blk(n), 0, 0))] + [col(0)] * n_other,
        out_specs=[col(0), col(0), col(0), _full((1, D_MODEL))],
        out_shape=[jax.ShapeDtypeStruct((t, D_MODEL), acc_dtype), jax.ShapeDtypeStruct((t, D_MODEL), BF16),
                   jax.ShapeDtypeStruct((t, D_MODEL), acc_dtype), jax.ShapeDtypeStruct((1, D_MODEL), F32)],
        scratch_shapes=[pltpu.VMEM((HGRN_HEADS, LANES, LANES), F32)],
        compiler_params=_cp("arbitrary"), name=name)(z, z, z, lb, d_o, states, *(other or ()))


def _hgrn_post_fwd(o_f, o_b, z, norm_g, *, name):
    t = o_f.shape[0]
    tm = _rows(t)

    def body(of_ref, ob_ref, gate_ref, ng_ref, y_ref):
        ng = ng_ref[...]
        for h in range(HGRN_HEADS):
            sl = slice(h * LANES, (h + 1) * LANES)
            o = of_ref[:, sl] + ob_ref[:, sl]
            r = lax.rsqrt(jnp.mean(o * o, axis=1, keepdims=True) + LN_EPS)
            gt = gate_ref[:, sl]
            y_ref[:, sl] = (o * r * ng * (gt * _sigmoid(gt))).astype(BF16)

    row = pl.BlockSpec((tm, D_MODEL), lambda i: (i, 0))
    return pl.pallas_call(
        body, grid=(t // tm,),
        in_specs=[row, row, pl.BlockSpec((tm, D_MODEL), lambda i: (i, 4)), _full((1, LANES))],
        out_specs=row, out_shape=jax.ShapeDtypeStruct((t, D_MODEL), BF16),
        compiler_params=_cp("parallel"), name=name)(o_f, o_b, z, norm_g)


def _hgrn_post_bwd(dy, o_f, o_b, z, norm_g, *, name):
    t = o_f.shape[0]
    tm = _rows(t)

    def body(dy_ref, of_ref, ob_ref, gate_ref, ng_ref, do_ref, dgate_ref, dng_ref):
        ng = ng_ref[...]
        dng = jnp.zeros((1, LANES), F32)
        for h in range(HGRN_HEADS):
            sl = slice(h * LANES, (h + 1) * LANES)
            o = of_ref[:, sl] + ob_ref[:, sl]
            r = lax.rsqrt(jnp.mean(o * o, axis=1, keepdims=True) + LN_EPS)
            gt = gate_ref[:, sl]
            sg = _sigmoid(gt)
            dyv = dy_ref[:, sl].astype(F32)
            xn = o * r
            dgate_ref[:, sl] = (dyv * xn * ng * sg * (1.0 + gt * (1.0 - sg))).astype(BF16)
            dn = dyv * gt * sg
            dng = dng + jnp.sum(dn * xn, axis=0, keepdims=True)
            dxn = dn * ng
            do_ref[:, sl] = r * (dxn - xn * jnp.mean(dxn * xn, axis=1, keepdims=True))

        @pl.when(pl.program_id(0) == 0)
        def _():
            dng_ref[...] = jnp.zeros_like(dng_ref)
        dng_ref[...] += dng

    row = pl.BlockSpec((tm, D_MODEL), lambda i: (i, 0))
    return pl.pallas_call(
        body, grid=(t // tm,),
        in_specs=[row, row, row, pl.BlockSpec((tm, D_MODEL), lambda i: (i, 4)), _full((1, LANES))],
        out_specs=[row, row, _full((1, LANES))],
        out_shape=[jax.ShapeDtypeStruct((t, D_MODEL), F32), jax.ShapeDtypeStruct((t, D_MODEL), BF16),
                   jax.ShapeDtypeStruct((1, LANES), F32)],
        compiler_params=_cp("arbitrary"), name=name)(dy, o_f, o_b, z, norm_g)


def _lower_bounds(logits2d, *, name):
    def body(l_ref, lb_ref):
        l = l_ref[...]
        e = jnp.exp(l - jnp.max(l, axis=0, keepdims=True))
        sm = e / jnp.sum(e, axis=0, keepdims=True)
        s1, s2, s3 = sm[1:2], sm[2:3], sm[3:4]
        lb_ref[...] = jnp.concatenate([jnp.zeros_like(s1), s1, s1 + s2, s1 + s2 + s3], axis=0)

    return pl.pallas_call(body, out_shape=jax.ShapeDtypeStruct(logits2d.shape, F32), name=name)(logits2d)


def _lower_bounds_bwd(logits2d, dlb, *, name):
    def body(l_ref, d_ref, o_ref):
        l = l_ref[...]
        e = jnp.exp(l - jnp.max(l, axis=0, keepdims=True))
        sm = e / jnp.sum(e, axis=0, keepdims=True)
        d = d_ref[...]
        d1, d2, d3 = d[1:2], d[2:3], d[3:4]
        dsm = jnp.concatenate([jnp.zeros_like(d1), d1 + d2 + d3, d2 + d3, d3], axis=0)
        o_ref[...] = sm * (dsm - jnp.sum(sm * dsm, axis=0, keepdims=True))

    return pl.pallas_call(body, out_shape=jax.ShapeDtypeStruct(logits2d.shape, F32), name=name)(logits2d, dlb)


def _adamw(w, g, m, v, *, name):
    r, c = w.shape
    tr = r if r <= 512 else _pick_rows(r)

    def body(w_ref, g_ref, m_ref, v_ref, d_ref, nm_ref, nv_ref):
        gv = g_ref[...]
        nm = ADAM_B1 * m_ref[...] + (1.0 - ADAM_B1) * gv
        nv = ADAM_B2 * v_ref[...] + (1.0 - ADAM_B2) * (gv * gv)
        m_hat = nm / (1.0 - ADAM_B1 ** ADAM_STEP)
        v_hat = nv / (1.0 - ADAM_B2 ** ADAM_STEP)
        d_ref[...] = -ADAM_LR * (m_hat / (jnp.sqrt(v_hat) + ADAM_EPS) + ADAM_WD * w_ref[...])
        nm_ref[...] = nm
        nv_ref[...] = nv

    blk = pl.BlockSpec((tr, c), lambda i: (i, 0))
    shp = jax.ShapeDtypeStruct((r, c), F32)
    return pl.pallas_call(body, grid=(r // tr,), in_specs=[blk] * 4, out_specs=[blk] * 3, out_shape=[shp] * 3,
                          compiler_params=_cp("parallel"), name=name)(w, g, m, v)


def _pick_rows(r, cap=512):
    best = 8
    for d in range(8, cap + 1, 8):
        if r % d == 0:
            best = d
    assert r % best == 0, r
    return best


def _place():
    x, y, c = lax.axis_index("x"), lax.axis_index("y"), lax.axis_index("c")
    chips = [(1 - x, y), (x, 1 - y), (1 - x, 1 - y)]
    return x, y, c, chips


def _remote(src, dst, send_sem, recv_sem, to):
    return pltpu.make_async_remote_copy(src_ref=src, dst_ref=dst, send_sem=send_sem, recv_sem=recv_sem,
                                        device_id=to, device_id_type=MESH)


def _allreduce_small(block, *, name):
    m, n = block.shape

    def body(x_ref, sum_ref, all_ref, send_sems, recv_sems):
        x, y, c, chips = _place()
        me, sibling = (x, y, c), (x, y, 1 - c)

        def rows(px, py, pc):
            return all_ref.at[pl.ds((4 * px + 2 * py + pc) * m, m), :]

        all_ref[pl.ds((4 * x + 2 * y + c) * m, m), :] = x_ref[...]
        first = [_remote(x_ref, rows(*me), send_sems.at[0], recv_sems.at[0], sibling)]
        first += [_remote(x_ref, rows(*me), send_sems.at[1 + j], recv_sems.at[1 + j], (*chip, c)) for j, chip in enumerate(chips)]
        for cp in first:
            cp.start()
        passed = [_remote(rows(*chip, c), rows(*chip, c), send_sems.at[4 + j], recv_sems.at[4 + j], sibling)
                  for j, chip in enumerate(chips)]
        for j, chip in enumerate(chips):
            _remote(x_ref, rows(*chip, c), send_sems.at[1 + j], recv_sems.at[1 + j], me).wait_recv()
            passed[j].start()
        _remote(x_ref, rows(*sibling), send_sems.at[0], recv_sems.at[0], me).wait_recv()
        for j, chip in enumerate(chips):
            _remote(x_ref, rows(*chip, 1 - c), send_sems.at[4 + j], recv_sems.at[4 + j], me).wait_recv()
        for cp in first + passed:
            cp.wait_send()
        acc = all_ref[pl.ds(0, m), :]
        for d in range(1, 8):
            acc = acc + all_ref[pl.ds(d * m, m), :]
        sum_ref[...] = acc

    vmem = pl.BlockSpec(memory_space=pltpu.VMEM)
    return pl.pallas_call(
        body, in_specs=[vmem], out_specs=vmem, out_shape=jax.ShapeDtypeStruct((m, n), F32),
        scratch_shapes=[pltpu.VMEM((8 * m, n), F32), pltpu.SemaphoreType.DMA((7,)), pltpu.SemaphoreType.DMA((7,))],
        name=name)(block)


def _add_own_half(g, recv, c, *, name):
    n, r, w = g.shape
    hr = r // 2
    tr = _pick_rows(hr, 1024)
    nr = hr // tr

    def body(c_ref, g_ref, r_ref, o_ref):
        o_ref[...] = (g_ref[...] + r_ref[...]).astype(BF16)

    return pl.pallas_call(
        body,
        grid_spec=pltpu.PrefetchScalarGridSpec(
            num_scalar_prefetch=1, grid=(n, nr),
            in_specs=[pl.BlockSpec((None, tr, w), lambda s, i, c_ref: (s, c_ref[0] * nr + i, 0)),
                      pl.BlockSpec((None, tr, w), lambda s, i, c_ref: (s, i, 0))],
            out_specs=pl.BlockSpec((None, tr, w), lambda s, i, c_ref: (s, i, 0))),
        out_shape=jax.ShapeDtypeStruct((n, hr, w), BF16),
        compiler_params=_cp("parallel", "parallel"), name=name)(c, g, recv)


def _sum_chips(q, *, name):
    n, h, w = q.shape
    tr = _pick_rows(h, 1024)

    def body(a, b, c, d, o_ref):
        o_ref[...] = ((a[...].astype(F32) + b[...].astype(F32)) + c[...].astype(F32)) + d[...].astype(F32)

    specs = [pl.BlockSpec((None, tr, w), functools.partial(lambda i, s: (s, i, 0), s=s)) for s in range(n)]
    return pl.pallas_call(
        body, grid=(h // tr,), in_specs=specs, out_specs=pl.BlockSpec((tr, w), lambda i: (i, 0)),
        out_shape=jax.ShapeDtypeStruct((h, w), F32), compiler_params=_cp("parallel"), name=name)(q, q, q, q)


PACK_W = 1024
BIG = (("att_w_qkv", 2), ("att_w_o", 1), ("hgrn_w_in", 2), ("hgrn_w_o", 1),
       ("ffn_w_in", 2), ("ffn_w_out", 1), ("ple_w_gate", 1), ("ple_w_proj", 2))
LB_ROWS = 32


SMALL =("ln_mix_g", "ln_mix_b", "ln_ffn_g", "ln_ffn_b")
SMALL_ROWS = 32


def _pack_small(vals, lb):
    rows = [vals[n] for n in SMALL] + [lb.reshape(-1, PACK_W)]
    for n in ("att_sink", "hgrn_norm_g"):
        flat = vals[n].reshape(1, -1)
        rows.append(jnp.pad(flat, ((0, 0), (0, PACK_W - flat.shape[1]))))
    buf = jnp.concatenate(rows, axis=0)
    return jnp.pad(buf, ((0, SMALL_ROWS - buf.shape[0]), (0, 0)))


def _unpack_small(buf, lb_shape):
    out, r0 = {}, 0
    for n in SMALL:
        out[n] = buf[r0:r0 + DEPTH]
        r0 += DEPTH
    nlb = lb_shape[0] * lb_shape[1] * lb_shape[2] // PACK_W
    out["hgrn_lb_logits"] = buf[r0:r0 + nlb].reshape(lb_shape)
    r0 += nlb
    out["att_sink"] = buf[r0, :2 * N_Q_HEADS].reshape(2, N_Q_HEADS)
    out["hgrn_norm_g"] = buf[r0 + 1, :2 * LANES].reshape(2, LANES)
    return out


WEIGHTS = ("att_w_qkv", "att_sink", "att_w_o", "hgrn_w_in", "hgrn_lb_logits", "hgrn_norm_g", "hgrn_w_o", "ln_mix_g",
           "ln_mix_b", "ffn_w_in", "ffn_w_out", "ln_ffn_g", "ln_ffn_b", "ple_w_gate", "ple_w_proj")


def _gather_guest(packed):
    r, w = packed.shape
    hr = r // 2

    def copies(src_ref, out_ref, send_sems, recv_sems):
        x, y, c, chips = _place()
        sibling = (x, y, 1 - c)

        def half(cx, cy, hc):
            return out_ref.at[2 * cx + cy, pl.ds(hc * hr, hr), :]

        my_half = src_ref.at[pl.ds(c * hr, hr), :]
        first = [_remote(my_half, half(x, y, c), send_sems.at[j], recv_sems.at[j], (*chip, c)) for j, chip in enumerate(chips)]
        passed = [_remote(half(*chip, c), half(*chip, c), send_sems.at[3 + j], recv_sems.at[3 + j], sibling)
                  for j, chip in enumerate(chips)]
        from_chips = [_remote(my_half, half(*chip, c), send_sems.at[j], recv_sems.at[j], (*chip, c)) for j, chip in enumerate(chips)]
        from_sibling = [_remote(my_half, half(*chip, 1 - c), send_sems.at[3 + j], recv_sems.at[3 + j], sibling)
                        for j, chip in enumerate(chips)]
        return first, passed, from_chips, from_sibling

    def start(gi, go, gs):
        for cp in copies(gi[0], go[0], *gs)[0]:
            cp.start()

    def finish(gi, go, gs):
        first, passed, from_chips, from_sibling = copies(gi[0], go[0], *gs)
        for arrival, onward in zip(from_chips, passed):
            arrival.wait_recv()
            onward.start()
        for arrival in from_sibling:
            arrival.wait_recv()
        for cp in first + passed:
            cp.wait_send()

    return _Guest((packed,), (jax.ShapeDtypeStruct((N_CHIPS, r, w), packed.dtype),),
                  (pltpu.SemaphoreType.DMA((6,)), pltpu.SemaphoreType.DMA((6,))), start, finish)


def _pair_exchange_guest(g):
    n, r, w = g.shape
    hr = r // 2

    def copy(g_ref, out_ref, send_sem, recv_sem):
        x, y, c, _ = _place()
        return _remote(g_ref.at[:, pl.ds((1 - c) * hr, hr), :], out_ref, send_sem, recv_sem, (x, y, 1 - c))

    return _Guest((g,), (jax.ShapeDtypeStruct((n, hr, w), g.dtype),), (pltpu.SemaphoreType.DMA, pltpu.SemaphoreType.DMA),
                  lambda gi, go, gs: copy(gi[0], go[0], *gs).start(),
                  lambda gi, go, gs: copy(gi[0], go[0], *gs).wait())


def _chip_scatter_guest(part):
    n, h, w = part.shape

    def copies(p_ref, out_ref, send_sems, recv_sems):
        x, y, c, chips = _place()
        me = 2 * x + y
        sends = [_remote(p_ref.at[2 * cx + cy], out_ref.at[me], send_sems.at[j], recv_sems.at[j], (cx, cy, c))
                 for j, (cx, cy) in enumerate(chips)]
        arrivals = [_remote(p_ref.at[me], out_ref.at[2 * cx + cy], send_sems.at[j], recv_sems.at[j], (cx, cy, c))
                    for j, (cx, cy) in enumerate(chips)]
        return sends, arrivals

    def start(gi, go, gs):
        for cp in copies(gi[0], go[0], *gs)[0]:
            cp.start()

    def finish(gi, go, gs):
        sends, arrivals = copies(gi[0], go[0], *gs)
        for cp in arrivals:
            cp.wait_recv()
        for cp in sends:
            cp.wait_send()

    return _Guest((part,), (jax.ShapeDtypeStruct((n, h, w), part.dtype),),
                  (pltpu.SemaphoreType.DMA((3,)), pltpu.SemaphoreType.DMA((3,))), start, finish)


def _pair_share_guest(halves):
    n = len(halves)

    def copies(k, h_ref, out_ref, send_sems, recv_sems):
        x, y, c, _ = _place()
        send = _remote(h_ref, out_ref.at[c], send_sems.at[k], recv_sems.at[k], (x, y, 1 - c))
        arrival = _remote(h_ref, out_ref.at[1 - c], send_sems.at[k], recv_sems.at[k], (x, y, 1 - c))
        return send, arrival

    def start(gi, go, gs):
        for k in range(n):
            copies(k, gi[k], go[k], *gs)[0].start()

    def finish(gi, go, gs):
        for k in range(n):
            send, arrival = copies(k, gi[k], go[k], *gs)
            send.wait_send()
            arrival.wait_recv()

    return _Guest(tuple(halves), tuple(jax.ShapeDtypeStruct((2,) + a.shape, a.dtype) for a in halves),
                  (pltpu.SemaphoreType.DMA((n,)), pltpu.SemaphoreType.DMA((n,))), start, finish)


def _own(stack, idx):
    return lax.dynamic_index_in_dim(stack, idx, axis=0, keepdims=False)


def _put(buf, block, idx):
    return lax.dynamic_update_slice_in_dim(buf, block[None], idx, axis=0)


AXIS = dict(BIG)


def _layer_parts(i):
    j = i // 2
    mixer = (("att_w_qkv", j), ("att_w_o", j)) if i % 2 == 0 else (("hgrn_w_in", j), ("hgrn_w_o", j))
    return mixer + (("ffn_w_in", i), ("ffn_w_out", i), ("ple_w_gate", i), ("ple_w_proj", i))


def _gather_groups():
    first = _layer_parts(0)
    return [first[:1], first[1:] + _layer_parts(1), _layer_parts(2), _layer_parts(3)]


def _pack_parts(shards, parts):
    return jnp.concatenate([shards[n][l].reshape(-1, PACK_W) for n, l in parts], axis=0)


def _gathered_parts(buf, parts, shapes):
    out, r0 = {}, 0
    for n, l in parts:
        r, c = shapes[n][1:]
        nr = r * c // PACK_W
        sh = buf[:, r0:r0 + nr].reshape(N_CHIPS, r, c)
        out[(n, l)] = sh.reshape(N_CHIPS * r, c) if AXIS[n] == 1 else sh.transpose(1, 0, 2).reshape(r, N_CHIPS * c)
        r0 += nr
    return out, r0


def _unpack_layer(buf, i, shapes):
    out, r0 = {}, 0
    for n, _ in _layer_parts(i):
        r, c = shapes[n][1:]
        nr = r * c // PACK_W
        out[n] = buf[r0:r0 + nr].reshape(r, c)
        r0 += nr
    return out


def _layer_slabs(full, i, shapes):
    parts = []
    for n, _ in _layer_parts(i):
        r, c = shapes[n][1:]
        g = full[n]
        g = g.reshape(N_CHIPS, -1, PACK_W) if AXIS[n] == 1 else g.reshape(r, N_CHIPS, c).transpose(1, 0, 2).reshape(N_CHIPS, -1, PACK_W)
        parts.append(g)
    return jnp.concatenate(parts, axis=1)


def kernel(x, p, att_w_qkv, att_sink, att_w_o, hgrn_w_in, hgrn_lb_logits, hgrn_norm_g, hgrn_w_o, ln_mix_g, ln_mix_b, ffn_w_in, ffn_w_out, ln_ffn_g, ln_ffn_b, ple_w_gate, ple_w_proj, loss_target, m_att_w_qkv, m_att_sink, m_att_w_o, m_hgrn_w_in, m_hgrn_lb_logits, m_hgrn_norm_g, m_hgrn_w_o, m_ln_mix_g, m_ln_mix_b, m_ffn_w_in, m_ffn_w_out, m_ln_ffn_g, m_ln_ffn_b, m_ple_w_gate, m_ple_w_proj, v_att_w_qkv, v_att_sink, v_att_w_o, v_hgrn_w_in, v_hgrn_lb_logits, v_hgrn_norm_g, v_hgrn_w_o, v_ln_mix_g, v_ln_mix_b, v_ffn_w_in, v_ffn_w_out, v_ln_ffn_g, v_ln_ffn_b, v_ple_w_gate, v_ple_w_proj):
    wts = dict(att_w_qkv=att_w_qkv, att_sink=att_sink, att_w_o=att_w_o, hgrn_w_in=hgrn_w_in, hgrn_lb_logits=hgrn_lb_logits,
               hgrn_norm_g=hgrn_norm_g, hgrn_w_o=hgrn_w_o, ln_mix_g=ln_mix_g, ln_mix_b=ln_mix_b, ffn_w_in=ffn_w_in,
               ffn_w_out=ffn_w_out, ln_ffn_g=ln_ffn_g, ln_ffn_b=ln_ffn_b, ple_w_gate=ple_w_gate, ple_w_proj=ple_w_proj)
    mom = dict(att_w_qkv=m_att_w_qkv, att_sink=m_att_sink, att_w_o=m_att_w_o, hgrn_w_in=m_hgrn_w_in, hgrn_lb_logits=m_hgrn_lb_logits,
               hgrn_norm_g=m_hgrn_norm_g, hgrn_w_o=m_hgrn_w_o, ln_mix_g=m_ln_mix_g, ln_mix_b=m_ln_mix_b, ffn_w_in=m_ffn_w_in,
               ffn_w_out=m_ffn_w_out, ln_ffn_g=m_ln_ffn_g, ln_ffn_b=m_ln_ffn_b, ple_w_gate=m_ple_w_gate, ple_w_proj=m_ple_w_proj)
    var = dict(att_w_qkv=v_att_w_qkv, att_sink=v_att_sink, att_w_o=v_att_w_o, hgrn_w_in=v_hgrn_w_in, hgrn_lb_logits=v_hgrn_lb_logits,
               hgrn_norm_g=v_hgrn_norm_g, hgrn_w_o=v_hgrn_w_o, ln_mix_g=v_ln_mix_g, ln_mix_b=v_ln_mix_b, ffn_w_in=v_ffn_w_in,
               ffn_w_out=v_ffn_w_out, ln_ffn_g=v_ln_ffn_g, ln_ffn_b=v_ln_ffn_b, ple_w_gate=v_ple_w_gate, ple_w_proj=v_ple_w_proj)
    shapes = {n: wts[n].shape for n, _ in BIG}
    chip = 2 * lax.axis_index("x") + lax.axis_index("y")
    core = lax.axis_index("c").reshape(1).astype(jnp.int32)
    xin, target = x[0], loss_target[0]
    t = xin.shape[0]
    row = lambda a, i: a[i][None]

    bf_shards = {n: wts[n].astype(BF16) for n, _ in BIG}
    lb_sh = hgrn_lb_logits.shape
    lb_bits = lax.bitcast_convert_type(hgrn_lb_logits.reshape(-1), BF16).reshape(-1, PACK_W)
    groups = _gather_groups()
    packed = [_pack_parts(bf_shards, parts) for parts in groups]
    packed[0] = jnp.concatenate([packed[0], lb_bits, jnp.zeros((LB_ROWS - lb_bits.shape[0], PACK_W), BF16)], axis=0)

    gathered = _put(_run_guest(_gather_guest(packed[0]), name="gather_first")[0], packed[0], chip)
    wfull, r_big = _gathered_parts(gathered, groups[0], shapes)
    lb_rows = gathered[:, r_big:r_big + lb_bits.shape[0]].reshape(N_CHIPS, -1, 2)
    lb_full = lax.bitcast_convert_type(lb_rows, F32).reshape(N_CHIPS, lb_sh[0], lb_sh[1], lb_sh[2])
    lb_full = lb_full.transpose(1, 2, 0, 3).reshape(lb_sh[0], lb_sh[1], N_CHIPS * lb_sh[2])
    logits2d = lb_full.reshape(DEPTH, 2 * D_MODEL)
    lb_all = _lower_bounds(logits2d, name="lb_fwd").reshape(DEPTH, 2, 1, D_MODEL)
    tables = _rope_tables(t)

    saved, weights = [], []
    xf, xb = xin, xin.astype(BF16)
    for i in range(DEPTH):
        j = i // 2
        nxt = _gather_guest(packed[i + 1]) if i + 1 < DEPTH else None
        s = dict(xin_bf=xb)
        if i % 2 == 0:
            q, k, v = _qkv_rope(xb, wfull[("att_w_qkv", j)], tables, name=f"qkv_rope_{i}")
            (o, lse), got = _attn_fwd(q, k, v, att_sink[j], name=f"attn_fwd_{i}", guest=nxt)
            s.update(q=q, k=k, v=v, o=o, lse=lse)
            mix_in = o
        else:
            z = _mm_nn(xb, wfull[("hgrn_w_in", j)], out_dtype=F32, name=f"hgrn_in_{i}")
            (o_f, s_f), _ = _hgrn_fwd(z, lb_all[i, 0], False, name=f"hgrn_scan_f_{i}")
            (o_b, s_b), _ = _hgrn_fwd(z, lb_all[i, 1], True, name=f"hgrn_scan_b_{i}")
            og = _hgrn_post_fwd(o_f, o_b, z, row(hgrn_norm_g, j), name=f"hgrn_post_{i}")
            s.update(z=z, o_f=o_f, o_b=o_b, s_f=s_f, s_b=s_b, og=og)
            mix_in = og
        in_mixer = nxt is not None and i % 2 == 0
        if in_mixer:
            wfull.update(_gathered_parts(_put(got[0], packed[i + 1], chip), groups[i + 1], shapes)[0])
        w = {n: wfull[(n, l)] for n, l in _layer_parts(i)}
        w_o = w["att_w_o" if i % 2 == 0 else "hgrn_w_o"]
        x1, x1b, xh1, rs1 = _mm_res_ln(mix_in, w_o, xf, row(ln_mix_g, i), row(ln_mix_b, i), name=f"mix_out_ln_{i}")
        (g, u, h), got = _ffn_in(x1b, w["ffn_w_in"], name=f"ffn_in_{i}", guest=None if in_mixer else nxt)
        if nxt is not None and not in_mixer:
            wfull.update(_gathered_parts(_put(got[0], packed[i + 1], chip), groups[i + 1], shapes)[0])
        x2, x2b, xh2, rs2 = _mm_res_ln(h, w["ffn_w_out"], x1, row(ln_ffn_g, i), row(ln_ffn_b, i), name=f"ffn_out_ln_{i}")
        xf, xb, gate, pp = _ple_fwd(x2, x2b, p, i, w["ple_w_gate"], w["ple_w_proj"], name=f"ple_fwd_{i}")
        s.update(x1b=x1b, xh1=xh1, rs1=rs1, g=g, u=u, h=h, x2b=x2b, xh2=xh2, rs2=rs2, gate=gate, pp=pp)
        saved.append(s)
        weights.append(w)

    loss, dx = _loss_head(xf, target, name="loss_head")
    loss = lax.psum(loss[0, 0], ("x", "y", "c"))

    gs = {n: [None] * DEPTH for n in SMALL}
    gs.update(att_sink=[None] * 2, hgrn_norm_g=[None] * 2)
    dlb = [jnp.zeros((2, D_MODEL), F32)] * DEPTH
    halves = [None] * DEPTH
    slabs = None
    for i in reversed(range(DEPTH)):
        j = i // 2
        s, w = saved[i], weights[i]
        gw = {}
        res, got = _ple_bwd(dx, s["gate"], s["pp"], w["ple_w_gate"], s["xh2"], s["rs2"], row(ln_ffn_g, i), name=f"ple_bwd_{i}",
                            guest=None if slabs is None else _pair_exchange_guest(slabs))
        du2, du2b, dpre, dpp, gs["ln_ffn_g"][i], gs["ln_ffn_b"][i] = res
        part = None if slabs is None else _add_own_half(slabs, got[0], core, name=f"grad_pair_add_{i + 1}")
        gw["ple_w_gate"] = _mm_tn(s["x2b"], dpre, name=f"dw_ple_gate_{i}")
        gw["ple_w_proj"] = _mm_tn_p(p, i, dpp, name=f"dw_ple_proj_{i}")
        dgg, dgu = _ffn_out_bwd(du2b, w["ffn_w_out"], s["g"], s["u"], name=f"ffn_out_bwd_{i}")
        gw["ffn_w_out"] = _mm_tn(s["h"], du2b, name=f"dw_ffn_out_{i}")
        res = _mm_nt([dgg, dgu], w["ffn_w_in"], resid=du2, ln=(s["xh1"], s["rs1"], row(ln_mix_g, i)),
                     name=f"ffn_in_bwd_{i}", guest=None if part is None else _chip_scatter_guest(part))
        (du1, du1b, gs["ln_mix_g"][i], gs["ln_mix_b"][i]), got = res if part is not None else (res, None)
        if part is not None:
            halves[i + 1] = _sum_chips(_put(got[0], _own(part, chip), chip), name=f"grad_chip_sum_{i + 1}")
        gw["ffn_w_in"] = jnp.concatenate(
            [_mm_tn(s["x1b"], dgg, name=f"dw_ffn_gate_{i}"), _mm_tn(s["x1b"], dgu, name=f"dw_ffn_up_{i}")], axis=1)
        if i % 2 == 0:
            gw["att_w_o"] = _mm_tn(s["o"], du1b, name=f"dw_att_o_{i}")
            do = _mm_nt([du1b], w["att_w_o"], out_dtype=BF16, name=f"att_o_bwd_{i}")
            dq, dkw, dvw, gs["att_sink"][j] = _attn_bwd(s["q"], s["k"], s["v"], s["o"], do, s["lse"], att_sink[j], name=f"attn_bwd_{i}")
            dqkv = _attn_post_bwd(dq, dkw, dvw, tables, name=f"attn_post_bwd_{i}")
            gw["att_w_qkv"] = _mm_tn(s["xin_bf"], dqkv, name=f"dw_att_qkv_{i}")
            dx = _mm_nt([dqkv], w["att_w_qkv"], resid=du1, name=f"qkv_bwd_{i}")
        else:
            gw["hgrn_w_o"] = _mm_tn(s["og"], du1b, name=f"dw_hgrn_o_{i}")
            dy = _mm_nt([du1b], w["hgrn_w_o"], out_dtype=BF16, name=f"hgrn_o_bwd_{i}")
            d_o, dgate, gs["hgrn_norm_g"][j] = _hgrn_post_bwd(dy, s["o_f"], s["o_b"], s["z"], row(hgrn_norm_g, j), name=f"hgrn_post_bwd_{i}")
            dq_f, dzf_f, dv_f, dlb_f = _hgrn_bwd(s["z"], lb_all[i, 0], d_o, s["s_f"], False, name=f"hgrn_scan_f_bwd_{i}")
            dq, dzf_b, dv, dlb_b = _hgrn_bwd(s["z"], lb_all[i, 1], d_o, s["s_b"], True, name=f"hgrn_scan_b_bwd_{i}",
                                             other=(dq_f, dv_f))
            dlb[i] = jnp.stack([dlb_f.reshape(D_MODEL), dlb_b.reshape(D_MODEL)])
            dz = [dq, dzf_f, dzf_b, dv, dgate]
            gw["hgrn_w_in"] = jnp.concatenate(
                [_mm_tn(s["xin_bf"], part, name=f"dw_hgrn_in_{i}_{n}") for n, part in enumerate(dz)], axis=1)
            dx = _mm_nt(dz, w["hgrn_w_in"], resid=du1, name=f"hgrn_in_bwd_{i}")
        slabs = _layer_slabs(gw, i, shapes)

    from_sibling = _run_guest(_pair_exchange_guest(slabs), name="grad_pair_exchange_0")[0]
    part = _add_own_half(slabs, from_sibling, core, name="grad_pair_add_0")
    from_chips = _run_guest(_chip_scatter_guest(part), name="grad_chip_scatter_0")[0]
    halves[0] = _sum_chips(_put(from_chips, _own(part, chip), chip), name="grad_chip_sum_0")
    shared = _run_guest(_pair_share_guest(halves), name="grad_pair_share")
    reduced = [_put(buf, half, lax.axis_index("c")) for buf, half in zip(shared, halves)]

    g_layers = [_unpack_layer(reduced[i].reshape(-1, PACK_W), i, shapes) for i in range(DEPTH)]
    grads = {}
    for n, _ in BIG:
        per_layer = [g_layers[i][n] for i in range(DEPTH) if n in g_layers[i]]
        grads[n] = jnp.stack(per_layer)

    gsmall = {n: jnp.concatenate(v, axis=0) for n, v in gs.items()}
    dlogits = _lower_bounds_bwd(logits2d, jnp.stack(dlb).reshape(DEPTH, 2 * D_MODEL), name="lb_bwd").reshape(DEPTH, 2, D_MODEL)
    g_small_full = _unpack_small(_allreduce_small(_pack_small(gsmall, dlogits), name="allreduce_small"),
                                 (lb_sh[0], lb_sh[1], N_CHIPS * lb_sh[2]))
    grads.update({n: g_small_full[n] for n in SMALL + ("att_sink", "hgrn_norm_g")})
    grads["hgrn_lb_logits"] = lax.dynamic_slice_in_dim(g_small_full["hgrn_lb_logits"], chip * lb_sh[2], lb_sh[2], axis=2)

    delta, new_m, new_v = {}, {}, {}
    for n, _ in BIG:
        two_d = lambda a: a.reshape(-1, a.shape[-1])
        d, nm, nv = _adamw(two_d(wts[n]), two_d(grads[n]), two_d(mom[n]), two_d(var[n]), name=f"adamw_{n}")
        delta[n], new_m[n], new_v[n] = d.reshape(shapes[n]), nm.reshape(shapes[n]), nv.reshape(shapes[n])
    packs = [_pack_small(src, src["hgrn_lb_logits"]) for src in (wts, grads, mom, var)]
    outs = _adamw(*packs, name="adamw_small")
    for dst, buf in zip((delta, new_m, new_v), outs):
        dst.update(_unpack_small(buf, lb_sh))

    return (loss, dx[None], *[grads[n] for n in WEIGHTS], *[delta[n] for n in WEIGHTS],
            *[new_m[n] for n in WEIGHTS], *[new_v[n] for n in WEIGHTS])
```

```python
import functools
from typing import Callable, NamedTuple

import jax
import jax.numpy as jnp
from jax import lax
from jax.experimental import pallas as pl
from jax.experimental.pallas import tpu as pltpu

F32, BF16 = jnp.float32, jnp.bfloat16

D_MODEL = 1024
DEPTH = 4
HEAD_DIM = 64
N_Q_HEADS = 16
N_KV_HEADS = 4
GROUP = 4
Q_DIM = 1024
KV_DIM = 256
WINDOW = 128
ROPE_DIM = 16
ROPE_THETA = 500000.0
HGRN_HEADS = 8
HGRN_KEY = 128
HGRN_CHUNK = 64
D_FF = 2816
PLE_DIM = 256
ALPHA = (2 * DEPTH) ** 0.25
LN_EPS = 1e-5
ADAM_LR, ADAM_B1, ADAM_B2, ADAM_EPS, ADAM_WD, ADAM_STEP = 0.001, 0.9, 0.999, 1e-08, 0.01, 10

LANES = 128
VMEM_LIMIT_BYTES = 56 * 2**20
FACTORED_DECAY_LIMIT = -80.0

NN = ((1,), (0,))
NT = ((1,), (1,))
TN = ((0,), (0,))

N_CHIPS = 4
MESH = pl.DeviceIdType.MESH


def _dot(a, b, dims):
    return lax.dot_general(a, b, (dims, ((), ())), preferred_element_type=F32)


def _cp(*sem):
    return pltpu.CompilerParams(dimension_semantics=sem, vmem_limit_bytes=VMEM_LIMIT_BYTES)


def _rows(t, cap=512):
    return min(cap, t)


def _pick(n, cap):
    best = None
    for d in range(LANES, min(n, cap) + 1, LANES):
        if n % d == 0:
            best = d
    assert best is not None, (n, cap)
    return best


def _sigmoid(x):
    return jax.nn.sigmoid(x)


def _ln_fwd(u, g, b):
    mu = jnp.mean(u, axis=-1, keepdims=True)
    xc = u - mu
    var = jnp.mean(xc * xc, axis=-1, keepdims=True)
    rstd = lax.rsqrt(var + LN_EPS)
    xhat = xc * rstd
    return xhat * g + b, xhat, rstd


def _ln_bwd(dy, xhat, rstd, g):
    dxh = dy * g
    m1 = jnp.mean(dxh, axis=-1, keepdims=True)
    m2 = jnp.mean(dxh * xhat, axis=-1, keepdims=True)
    du = rstd * (dxh - m1 - xhat * m2)
    return du, jnp.sum(dy * xhat, axis=0, keepdims=True), jnp.sum(dy, axis=0, keepdims=True)


def _full(shape):
    nd = len(shape)
    return pl.BlockSpec(shape, lambda *_: (0,) * nd)


ANY = pl.BlockSpec(memory_space=pl.ANY)


class _Guest(NamedTuple):
    args: tuple
    out_shape: tuple
    sems: tuple
    start: Callable
    finish: Callable


def _call(body, *, grid, in_specs, out_specs, out_shape, args, sem, name, scratch_shapes=(), guest=None):
    n_in, n_out, n_scr = len(args), len(out_shape), len(scratch_shapes)
    if guest is None:
        res = pl.pallas_call(body, grid=grid, in_specs=list(in_specs), out_specs=list(out_specs), out_shape=list(out_shape),
                             scratch_shapes=list(scratch_shapes), compiler_params=_cp(*sem), name=name)(*args)
        return list(res), []
    g_in, g_out = len(guest.args), len(guest.out_shape)

    def hosted(*refs):
        cuts = [n_in, g_in, n_out, g_out, n_scr]
        parts, pos = [], 0
        for n in cuts:
            parts.append(refs[pos:pos + n])
            pos += n
        h_in, gi, h_out, go, h_scr = parts
        gs = refs[pos:]
        ids = [pl.program_id(d) for d in range(len(grid))]
        first = functools.reduce(jnp.logical_and, [i == 0 for i in ids])
        last = functools.reduce(jnp.logical_and, [i == n - 1 for i, n in zip(ids, grid)])

        @pl.when(first)
        def _():
            guest.start(gi, go, gs)
        body(*h_in, *h_out, *h_scr)

        @pl.when(last)
        def _():
            guest.finish(gi, go, gs)

    res = pl.pallas_call(
        hosted, grid=grid, in_specs=list(in_specs) + [ANY] * g_in, out_specs=list(out_specs) + [ANY] * g_out,
        out_shape=list(out_shape) + list(guest.out_shape), scratch_shapes=list(scratch_shapes) + list(guest.sems),
        compiler_params=_cp(*(("arbitrary",) * len(grid))), name=name)(*args, *guest.args)
    return list(res[:n_out]), list(res[n_out:])


def _run_guest(guest, *, name):
    g_in = len(guest.args)

    def body(*refs):
        gi, go, gs = refs[:g_in], refs[g_in:g_in + len(guest.out_shape)], refs[g_in + len(guest.out_shape):]
        guest.start(gi, go, gs)
        guest.finish(gi, go, gs)

    return pl.pallas_call(body, in_specs=[ANY] * g_in, out_specs=[ANY] * len(guest.out_shape), out_shape=list(guest.out_shape),
                          scratch_shapes=list(guest.sems), name=name)(*guest.args)


def _mm_nn(a, w, *, out_dtype, name):
    t, k = a.shape
    n = w.shape[1]
    tm, tn = _rows(t), _pick(n, 1408)

    def body(a_ref, w_ref, o_ref):
        o_ref[...] = _dot(a_ref[...], w_ref[...], NN).astype(out_dtype)

    return pl.pallas_call(
        body, grid=(n // tn, t // tm),
        in_specs=[pl.BlockSpec((tm, k), lambda j, i: (i, 0)), pl.BlockSpec((k, tn), lambda j, i: (0, j))],
        out_specs=pl.BlockSpec((tm, tn), lambda j, i: (i, j)),
        out_shape=jax.ShapeDtypeStruct((t, n), out_dtype),
        compiler_params=_cp("parallel", "parallel"), name=name)(a, w)


def _mm_tn(a, b, *, name, guest=None):
    r, m = a.shape
    n = b.shape[1]
    tr, tm, tn = _rows(r), _pick(m, 1408), _pick(n, 2816)

    def body(a_ref, b_ref, o_ref):
        @pl.when(pl.program_id(2) == 0)
        def _():
            o_ref[...] = jnp.zeros_like(o_ref)
        o_ref[...] += _dot(a_ref[...].astype(BF16), b_ref[...].astype(BF16), TN)

    res, extra = _call(
        body, grid=(m // tm, n // tn, r // tr),
        in_specs=[pl.BlockSpec((tr, tm), lambda i, j, k: (k, i)), pl.BlockSpec((tr, tn), lambda i, j, k: (k, j))],
        out_specs=[pl.BlockSpec((tm, tn), lambda i, j, k: (i, j))],
        out_shape=[jax.ShapeDtypeStruct((m, n), F32)], args=(a, b),
        sem=("parallel", "parallel", "arbitrary"), name=name, guest=guest)
    return res[0] if guest is None else (res[0], extra)


def _mm_tn_p(p, layer, b, *, name):
    t = p.shape[2]
    n = b.shape[1]
    tr = _rows(t)

    def body(a_ref, b_ref, o_ref):
        @pl.when(pl.program_id(0) == 0)
        def _():
            o_ref[...] = jnp.zeros_like(o_ref)
        o_ref[...] += _dot(a_ref[...].astype(BF16), b_ref[...], TN)

    return pl.pallas_call(
        body, grid=(t // tr,),
        in_specs=[pl.BlockSpec((None, None, tr, PLE_DIM), lambda k: (layer, 0, k, 0)),
                  pl.BlockSpec((tr, n), lambda k: (k, 0))],
        out_specs=pl.BlockSpec((PLE_DIM, n), lambda k: (0, 0)),
        out_shape=jax.ShapeDtypeStruct((PLE_DIM, n), F32),
        compiler_params=_cp("arbitrary"), name=name)(p, b)


def _mm_res_ln(a, w, resid, g, b, *, name):
    t, k = a.shape
    tm = _rows(t)

    def body(a_ref, w_ref, r_ref, g_ref, b_ref, y_ref, ybf_ref, xh_ref, rs_ref):
        acc = _dot(a_ref[...], w_ref[...], NN)
        y, xh, rs = _ln_fwd(ALPHA * r_ref[...] + acc, g_ref[...], b_ref[...])
        y_ref[...] = y
        ybf_ref[...] = y.astype(BF16)
        xh_ref[...] = xh
        rs_ref[...] = rs

    row = pl.BlockSpec((tm, D_MODEL), lambda i: (i, 0))
    return pl.pallas_call(
        body, grid=(t // tm,),
        in_specs=[pl.BlockSpec((tm, k), lambda i: (i, 0)), _full((k, D_MODEL)), row, _full((1, D_MODEL)), _full((1, D_MODEL))],
        out_specs=[row, row, row, pl.BlockSpec((tm, 1), lambda i: (i, 0))],
        out_shape=[jax.ShapeDtypeStruct((t, D_MODEL), F32), jax.ShapeDtypeStruct((t, D_MODEL), BF16),
                   jax.ShapeDtypeStruct((t, D_MODEL), F32), jax.ShapeDtypeStruct((t, 1), F32)],
        compiler_params=_cp("parallel"), name=name)(a, w, resid, g, b)


def _ffn_in(x_bf, w, *, name, guest=None):
    t = x_bf.shape[0]
    tm, tn = _rows(t), _pick(D_FF, 1408)
    nb = D_FF // tn

    def body(x_ref, wg_ref, wu_ref, dg_ref, du_ref, h_ref):
        x = x_ref[...]
        g = _dot(x, wg_ref[...], NN)
        u = _dot(x, wu_ref[...], NN)
        sig = _sigmoid(g)
        silu = g * sig
        dg_ref[...] = (u * sig * (1.0 + g * (1.0 - sig))).astype(BF16)
        du_ref[...] = silu.astype(BF16)
        h_ref[...] = (silu * u).astype(BF16)

    tile = pl.BlockSpec((tm, tn), lambda j, i: (i, j))
    shp = jax.ShapeDtypeStruct((t, D_FF), BF16)
    return _call(
        body, grid=(nb, t // tm),
        in_specs=[pl.BlockSpec((tm, D_MODEL), lambda j, i: (i, 0)),
                  pl.BlockSpec((D_MODEL, tn), lambda j, i: (0, j)),
                  pl.BlockSpec((D_MODEL, tn), lambda j, i: (0, j + nb))],
        out_specs=[tile, tile, tile], out_shape=[shp, shp, shp], args=(x_bf, w, w),
        sem=("parallel", "parallel"), name=name, guest=guest)


def _ple_fwd(x, x_bf, p, layer, wg, wp, *, name):
    t = x.shape[0]
    tm = _rows(t)

    def body(x_ref, xbf_ref, p_ref, wg_ref, wp_ref, y_ref, ybf_ref, gate_ref, pp_ref):
        gate = _sigmoid(_dot(xbf_ref[...], wg_ref[...], NN))
        pp = _dot(p_ref[...].astype(BF16), wp_ref[...], NN)
        y = x_ref[...] + gate * pp
        y_ref[...] = y
        ybf_ref[...] = y.astype(BF16)
        gate_ref[...] = gate.astype(BF16)
        pp_ref[...] = pp.astype(BF16)

    row = pl.BlockSpec((tm, D_MODEL), lambda i: (i, 0))
    f32, bf = jax.ShapeDtypeStruct((t, D_MODEL), F32), jax.ShapeDtypeStruct((t, D_MODEL), BF16)
    return pl.pallas_call(
        body, grid=(t // tm,),
        in_specs=[row, row, pl.BlockSpec((None, None, tm, PLE_DIM), lambda i: (layer, 0, i, 0)),
                  _full((D_MODEL, D_MODEL)), _full((PLE_DIM, D_MODEL))],
        out_specs=[row, row, row, row], out_shape=[f32, bf, bf, bf],
        compiler_params=_cp("parallel"), name=name)(x, x_bf, p, wg, wp)


def _loss_head(y, target, *, name):
    t = y.shape[0]
    tm = _rows(t)

    def body(y_ref, t_ref, loss_ref, dy_ref):
        e = y_ref[...] - t_ref[...]

        @pl.when(pl.program_id(0) == 0)
        def _():
            loss_ref[...] = jnp.zeros_like(loss_ref)
        sq = jnp.sum(jnp.sum(e * e, axis=1, keepdims=True), axis=0, keepdims=True)
        loss_ref[...] += sq * (0.5 / D_MODEL)
        dy_ref[...] = e * (1.0 / D_MODEL)

    row = pl.BlockSpec((tm, D_MODEL), lambda i: (i, 0))
    return pl.pallas_call(
        body, grid=(t // tm,), in_specs=[row, row],
        out_specs=[_full((1, 1)), row],
        out_shape=[jax.ShapeDtypeStruct((1, 1), F32), jax.ShapeDtypeStruct((t, D_MODEL), F32)],
        compiler_params=_cp("arbitrary"), name=name)(y, target)


def _ple_bwd(dx3, gate, pp, wg, xhat, rstd, g, *, name, guest=None):
    t = dx3.shape[0]
    tm = _rows(t)

    def body(dx_ref, gate_ref, pp_ref, wg_ref, xh_ref, rs_ref, g_ref,
             du_ref, dubf_ref, dpre_ref, dpp_ref, dg_ref, db_ref):
        dx = dx_ref[...]
        gate = gate_ref[...].astype(F32)
        dpre = (dx * pp_ref[...].astype(F32) * gate * (1.0 - gate)).astype(BF16)
        dpre_ref[...] = dpre
        dpp_ref[...] = (dx * gate).astype(BF16)
        dx2 = dx + _dot(dpre, wg_ref[...], NT)
        du, dg, db = _ln_bwd(dx2, xh_ref[...], rs_ref[...], g_ref[...])
        du_ref[...] = du
        dubf_ref[...] = du.astype(BF16)

        @pl.when(pl.program_id(0) == 0)
        def _():
            dg_ref[...] = jnp.zeros_like(dg_ref)
            db_ref[...] = jnp.zeros_like(db_ref)
        dg_ref[...] += dg
        db_ref[...] += db

    row = pl.BlockSpec((tm, D_MODEL), lambda i: (i, 0))
    vec = _full((1, D_MODEL))
    f32, bf = jax.ShapeDtypeStruct((t, D_MODEL), F32), jax.ShapeDtypeStruct((t, D_MODEL), BF16)
    v32 = jax.ShapeDtypeStruct((1, D_MODEL), F32)
    res, extra = _call(
        body, grid=(t // tm,),
        in_specs=[row, row, row, _full((D_MODEL, D_MODEL)), row, pl.BlockSpec((tm, 1), lambda i: (i, 0)), vec],
        out_specs=[row, row, row, row, vec, vec], out_shape=[f32, bf, bf, bf, v32, v32],
        args=(dx3, gate, pp, wg, xhat, rstd, g), sem=("arbitrary",), name=name, guest=guest)
    return res, extra


def _ffn_out_bwd(du_bf, w_out, g, u, *, name):
    t = du_bf.shape[0]
    tm, tn = _rows(t), _pick(D_FF, 1408)

    def body(du_ref, w_ref, hg_ref, hu_ref, dg_ref, dup_ref):
        dh = _dot(du_ref[...], w_ref[...], NT)
        dg_ref[...] = (dh * hg_ref[...].astype(F32)).astype(BF16)
        dup_ref[...] = (dh * hu_ref[...].astype(F32)).astype(BF16)

    tile = pl.BlockSpec((tm, tn), lambda j, i: (i, j))
    shp = jax.ShapeDtypeStruct((t, D_FF), BF16)
    return pl.pallas_call(
        body, grid=(D_FF // tn, t // tm),
        in_specs=[pl.BlockSpec((tm, D_MODEL), lambda j, i: (i, 0)), pl.BlockSpec((tn, D_MODEL), lambda j, i: (j, 0)),
                  tile, tile],
        out_specs=[tile, tile], out_shape=[shp, shp],
        compiler_params=_cp("parallel", "parallel"), name=name)(du_bf, w_out, g, u)


def _mm_nt(parts, w, *, resid=None, ln=None, out_dtype=F32, name, guest=None):
    t, kp = parts[0].shape
    npart = len(parts)
    tm = _rows(t)
    n_in = npart + 1 + (resid is not None) + (3 if ln is not None else 0)

    def body(*refs):
        a_refs, w_ref = refs[:npart], refs[npart]
        pos = npart + 1
        r_ref = None
        if resid is not None:
            r_ref = refs[pos]
            pos += 1
        if ln is not None:
            xh_ref, rs_ref, g_ref = refs[pos:pos + 3]
        outs = refs[n_in:]
        val = _dot(a_refs[0][...], w_ref[:, :kp], NT)
        for pi in range(1, npart):
            val = val + _dot(a_refs[pi][...], w_ref[:, pi * kp:(pi + 1) * kp], NT)
        if r_ref is not None:
            val = val + ALPHA * r_ref[...]
        if ln is None:
            outs[0][...] = val.astype(out_dtype)
        else:
            du, dg, db = _ln_bwd(val, xh_ref[...], rs_ref[...], g_ref[...])
            outs[0][...] = du
            outs[1][...] = du.astype(BF16)

            @pl.when(pl.program_id(0) == 0)
            def _():
                outs[2][...] = jnp.zeros_like(outs[2])
                outs[3][...] = jnp.zeros_like(outs[3])
            outs[2][...] += dg
            outs[3][...] += db

    row = pl.BlockSpec((tm, D_MODEL), lambda i: (i, 0))
    vec = pl.BlockSpec((1, D_MODEL), lambda i: (0, 0))
    in_specs = [pl.BlockSpec((tm, kp), lambda i: (i, 0)) for _ in range(npart)]
    in_specs.append(pl.BlockSpec((D_MODEL, npart * kp), lambda i: (0, 0), pipeline_mode=pl.Buffered(1)))
    args = list(parts) + [w]
    if resid is not None:
        in_specs.append(row)
        args.append(resid)
    if ln is not None:
        in_specs += [row, pl.BlockSpec((tm, 1), lambda i: (i, 0)), vec]
        args += list(ln)
        out_specs = [row, row, vec, vec]
        out_shape = [jax.ShapeDtypeStruct((t, D_MODEL), F32), jax.ShapeDtypeStruct((t, D_MODEL), BF16),
                     jax.ShapeDtypeStruct((1, D_MODEL), F32), jax.ShapeDtypeStruct((1, D_MODEL), F32)]
    else:
        out_specs = [row]
        out_shape = [jax.ShapeDtypeStruct((t, D_MODEL), out_dtype)]
    res, extra = _call(
        body, grid=(t // tm,), in_specs=in_specs, out_specs=out_specs, out_shape=out_shape, args=tuple(args),
        sem=("arbitrary" if ln is not None else "parallel",), name=name, guest=guest)
    res = res if ln is not None else res[0]
    return res if guest is None else (res, extra)


def _rope_tables(t):
    half = ROPE_DIM // 2
    inv = ROPE_THETA ** (-jnp.arange(0, ROPE_DIM, 2, dtype=F32) / ROPE_DIM)
    ang = jnp.arange(t, dtype=F32)[:, None] * inv[None, :]
    cos, sin = jnp.cos(ang), jnp.sin(ang)
    pad = HEAD_DIM - ROPE_DIM
    c = jnp.concatenate([cos, cos, jnp.ones((t, pad), F32)], axis=1)
    s_hi = jnp.concatenate([jnp.zeros((t, half), F32), sin, jnp.zeros((t, pad), F32)], axis=1)
    s_lo = jnp.concatenate([-sin, jnp.zeros((t, half + pad), F32)], axis=1)
    rep = LANES // HEAD_DIM
    return jnp.tile(c, (1, rep)), jnp.tile(s_hi, (1, rep)), jnp.tile(s_lo, (1, rep))


def _rope(x, c, s_hi, s_lo, sign):
    half = ROPE_DIM // 2
    parts = []
    for j in range(x.shape[1] // LANES):
        xg = x[:, j * LANES:(j + 1) * LANES]
        rot = pltpu.roll(xg, half, 1) * s_hi + pltpu.roll(xg, LANES - half, 1) * s_lo
        parts.append(xg * c + sign * rot)
    return jnp.concatenate(parts, axis=1)


def _qkv_rope(x_bf, w, tables, *, name):
    t = x_bf.shape[0]
    tm = _rows(t)
    n = Q_DIM + 2 * KV_DIM

    def body(x_ref, w_ref, c_ref, sh_ref, sl_ref, q_ref, k_ref, v_ref):
        acc = _dot(x_ref[...], w_ref[...], NN)
        c, sh, sl = c_ref[...], sh_ref[...], sl_ref[...]
        q_ref[...] = (_rope(acc[:, :Q_DIM], c, sh, sl, 1.0) * HEAD_DIM ** -0.5).astype(BF16)
        k_ref[...] = _rope(acc[:, Q_DIM:Q_DIM + KV_DIM], c, sh, sl, 1.0).astype(BF16)
        v_ref[...] = acc[:, Q_DIM + KV_DIM:].astype(BF16)

    tab = pl.BlockSpec((tm, LANES), lambda i: (i, 0))
    return pl.pallas_call(
        body, grid=(t // tm,),
        in_specs=[pl.BlockSpec((tm, D_MODEL), lambda i: (i, 0)), _full((D_MODEL, n)), tab, tab, tab],
        out_specs=[pl.BlockSpec((tm, Q_DIM), lambda i: (i, 0)), pl.BlockSpec((tm, KV_DIM), lambda i: (i, 0)),
                   pl.BlockSpec((tm, KV_DIM), lambda i: (i, 0))],
        out_shape=[jax.ShapeDtypeStruct((t, Q_DIM), BF16), jax.ShapeDtypeStruct((t, KV_DIM), BF16),
                   jax.ShapeDtypeStruct((t, KV_DIM), BF16)],
        compiler_params=_cp("parallel"), name=name)(x_bf, w, *tables)


ATT_ROWS = 256


def _window_specs(t, tq):
    r = tq // WINDOW
    last = t // WINDOW - 1
    return [pl.BlockSpec((WINDOW, KV_DIM), lambda i: (jnp.maximum(i * r - 1, 0), 0)),
            pl.BlockSpec((tq, KV_DIM), lambda i: (i, 0)),
            pl.BlockSpec((WINDOW, KV_DIM), lambda i: (jnp.minimum((i + 1) * r, last), 0))]


def _att_valid(base, t):
    rows = GROUP * WINDOW
    qpos = base + (lax.broadcasted_iota(jnp.int32, (rows, 3 * WINDOW), 0) & (WINDOW - 1))
    kpos = base - WINDOW + lax.broadcasted_iota(jnp.int32, (rows, 3 * WINDOW), 1)
    return (jnp.abs(qpos - kpos) <= WINDOW) & (kpos >= 0) & (kpos < t)


def _stack_heads(x, r0, g):
    return jnp.concatenate(
        [x[r0:r0 + WINDOW, (GROUP * g + h) * HEAD_DIM:(GROUP * g + h + 1) * HEAD_DIM] for h in range(GROUP)], axis=0)


def _sink_column(sink_ref, g):
    return jnp.concatenate([jnp.full((WINDOW, 1), sink_ref[GROUP * g + h], F32) for h in range(GROUP)], axis=0)


def _unstack_heads(pieces):
    return jnp.concatenate([pieces[g][h * WINDOW:(h + 1) * WINDOW] for g in range(N_KV_HEADS) for h in range(GROUP)], axis=1)


def _attn_fwd(q, k, v, sink, *, name, guest=None):
    t = q.shape[0]
    tq = min(ATT_ROWS, t)
    nsb = tq // WINDOW

    def body(sink_ref, q_ref, kp_ref, kc_ref, kn_ref, vp_ref, vc_ref, vn_ref, o_ref, l_ref):
        i = pl.program_id(0)
        qv = q_ref[...]
        kw = jnp.concatenate([kp_ref[...], kc_ref[...], kn_ref[...]], axis=0)
        vw = jnp.concatenate([vp_ref[...], vc_ref[...], vn_ref[...]], axis=0)
        ones = jnp.ones((3 * WINDOW, HEAD_DIM), BF16)
        for sb in range(nsb):
            r0 = sb * WINDOW
            bias =jnp.where(_att_valid(i * tq + r0, t)[:WINDOW], 0.0, -1e30)
            pieces = []
            for hh in range(N_Q_HEADS):
                g, h = hh // GROUP, hh % GROUP
                qh = qv[r0:r0 + WINDOW, hh * HEAD_DIM:(hh + 1) * HEAD_DIM]
                kg = kw[r0:r0 + 3 * WINDOW, g * HEAD_DIM:(g + 1) * HEAD_DIM]
                vg = jnp.concatenate([vw[r0:r0 + 3 * WINDOW, g * HEAD_DIM:(g + 1) * HEAD_DIM], ones], axis=1)
                s = _dot(qh, kg, NT) + bias
                snk = sink_ref[hh]
                m = jnp.maximum(jnp.max(s, axis=1, keepdims=True), snk)
                ov = _dot(jnp.exp(s - m).astype(BF16), vg, NN)
                den = ov[:, HEAD_DIM:HEAD_DIM + 1] + jnp.exp(snk - m)
                pieces.append(ov[:, :HEAD_DIM] * (1.0 / den))
                l_ref[g, sb, h * WINDOW:(h + 1) * WINDOW, :] = m + jnp.log(den)
            o_ref[r0:r0 + WINDOW, :] = jnp.concatenate(pieces, axis=1).astype(BF16)

    return _call(
        body, grid=(t // tq,),
        in_specs=[pl.BlockSpec(memory_space=pltpu.SMEM), pl.BlockSpec((tq, Q_DIM), lambda i: (i, 0))]
        + _window_specs(t, tq) + _window_specs(t, tq),
        out_specs=[pl.BlockSpec((tq, Q_DIM), lambda i: (i, 0)),
                   pl.BlockSpec((N_KV_HEADS, nsb, GROUP * WINDOW, 1), lambda i: (0, i, 0, 0))],
        out_shape=[jax.ShapeDtypeStruct((t, Q_DIM), BF16),
                   jax.ShapeDtypeStruct((N_KV_HEADS, t // WINDOW, GROUP * WINDOW, 1), F32)],
        args=(sink, q, k, k, k, v, v, v), sem=("parallel",), name=name, guest=guest)


def _attn_bwd(q, k, v, o, do, lse, sink, *, name):
    t = q.shape[0]
    tq = min(ATT_ROWS, t)
    nsb = tq // WINDOW

    def body(sink_ref, q_ref, o_ref, do_ref, l_ref, kp_ref, kc_ref, kn_ref, vp_ref, vc_ref, vn_ref,
             dq_ref, dkw_ref, dvw_ref, dsink_ref):
        i = pl.program_id(0)
        qv, ov, dov = q_ref[...], o_ref[...], do_ref[...]
        kw = jnp.concatenate([kp_ref[...], kc_ref[...], kn_ref[...]], axis=0)
        vw = jnp.concatenate([vp_ref[...], vc_ref[...], vn_ref[...]], axis=0)
        lane16 = lax.broadcasted_iota(jnp.int32, (1, N_Q_HEADS), 1)
        dsink = jnp.zeros((1, N_Q_HEADS), F32)
        for sb in range(nsb):
            r0 = sb * WINDOW
            valid = _att_valid(i * tq + r0, t)
            dq_p, dk_p, dv_p = [], [], []
            for g in range(N_KV_HEADS):
                qs, dos = _stack_heads(qv, r0, g), _stack_heads(dov, r0, g)
                delta = jnp.sum(dos.astype(F32) * _stack_heads(ov, r0, g).astype(F32), axis=1, keepdims=True)
                kg = kw[r0:r0 + 3 * WINDOW, g * HEAD_DIM:(g + 1) * HEAD_DIM]
                vg = vw[r0:r0 + 3 * WINDOW, g * HEAD_DIM:(g + 1) * HEAD_DIM]
                lse_col = l_ref[g, sb]
                pr = jnp.where(valid, jnp.exp(_dot(qs, kg, NT) - lse_col), 0.0)
                ds = (pr * (_dot(dos, vg, NT) - delta)).astype(BF16)
                dq_p.append(_dot(ds, kg, NN))
                dk_p.append(_dot(ds, qs, TN))
                dv_p.append(_dot(pr.astype(BF16), dos, TN))
                ps = jnp.exp(_sink_column(sink_ref, g) - lse_col) * delta
                for h in range(GROUP):
                    tot = jnp.sum(ps[h * WINDOW:(h + 1) * WINDOW], axis=0, keepdims=True)
                    dsink = dsink - jnp.where(lane16 == GROUP * g + h, tot, 0.0)
            dq_ref[r0:r0 + WINDOW, :] = _unstack_heads(dq_p)
            dkw_ref[sb] = jnp.concatenate(dk_p, axis=1)
            dvw_ref[sb] = jnp.concatenate(dv_p, axis=1)

        @pl.when(i == 0)
        def _():
            dsink_ref[...] = jnp.zeros_like(dsink_ref)
        dsink_ref[...] += dsink

    rowq = pl.BlockSpec((tq, Q_DIM), lambda i: (i, 0))
    win = pl.BlockSpec((nsb, 3 * WINDOW, KV_DIM), lambda i: (i, 0, 0))
    wshape = jax.ShapeDtypeStruct((t // WINDOW, 3 * WINDOW, KV_DIM), F32)
    return pl.pallas_call(
        body, grid=(t // tq,),
        in_specs=[pl.BlockSpec(memory_space=pltpu.SMEM), rowq, rowq, rowq,
                  pl.BlockSpec((N_KV_HEADS, nsb, GROUP * WINDOW, 1), lambda i: (0, i, 0, 0))]
        + _window_specs(t, tq) + _window_specs(t, tq),
        out_specs=[rowq, win, win, _full((1, N_Q_HEADS))],
        out_shape=[jax.ShapeDtypeStruct((t, Q_DIM), F32), wshape, wshape, jax.ShapeDtypeStruct((1, N_Q_HEADS), F32)],
        compiler_params=_cp("arbitrary"), name=name)(sink, q, o, do, lse, k, k, k, v, v, v)


def _attn_post_bwd(dq, dkw, dvw, tables, *, name):
    t = dq.shape[0]
    nb = t // WINDOW
    n = Q_DIM + 2 * KV_DIM

    def body(dq_ref, ka_ref, kb_ref, kc_ref, va_ref, vb_ref, vc_ref, c_ref, sh_ref, sl_ref, o_ref):
        j = pl.program_id(0)
        lo = (j > 0).astype(F32)
        hi = (j < nb - 1).astype(F32)
        c, sh, sl = c_ref[...], sh_ref[...], sl_ref[...]
        dk = ka_ref[...] * hi + kb_ref[...] + kc_ref[...] * lo
        dv = va_ref[...] * hi + vb_ref[...] + vc_ref[...] * lo
        o_ref[:, :Q_DIM] = (_rope(dq_ref[...], c, sh, sl, -1.0) * HEAD_DIM ** -0.5).astype(BF16)
        o_ref[:, Q_DIM:Q_DIM + KV_DIM] = _rope(dk, c, sh, sl, -1.0).astype(BF16)
        o_ref[:, Q_DIM + KV_DIM:] = dv.astype(BF16)

    wins = [pl.BlockSpec((None, WINDOW, KV_DIM), lambda j: (jnp.minimum(j + 1, nb - 1), 0, 0)),
            pl.BlockSpec((None, WINDOW, KV_DIM), lambda j: (j, 1, 0)),
            pl.BlockSpec((None, WINDOW, KV_DIM), lambda j: (jnp.maximum(j - 1, 0), 2, 0))]
    tab = pl.BlockSpec((WINDOW, LANES), lambda j: (j, 0))
    return pl.pallas_call(
        body, grid=(nb,),
        in_specs=[pl.BlockSpec((WINDOW, Q_DIM), lambda j: (j, 0))] + wins + wins + [tab, tab, tab],
        out_specs=pl.BlockSpec((WINDOW, n), lambda j: (j, 0)),
        out_shape=jax.ShapeDtypeStruct((t, n), BF16),
        compiler_params=_cp("parallel"), name=name)(dq, dkw, dkw, dkw, dvw, dvw, dvw, *tables)


HGRN_ROWS = 512
HGRN_UNROLL = 2
HGRN_UNROLL_FWD = 4


def _cum_mask(rev):
    row = lax.broadcasted_iota(jnp.int32, (HGRN_CHUNK, HGRN_CHUNK), 0)
    col = lax.broadcasted_iota(jnp.int32, (HGRN_CHUNK, HGRN_CHUNK), 1)
    return (row <= col) if rev else (row >= col)


def _gates(zq, zf, lb):
    q = zq * _sigmoid(zq)
    sig = _sigmoid(zf)
    f = lb + (1.0 - lb) * sig
    k = (1.0 - lb) * (1.0 - sig)
    return q, sig, f, k


def _intra_exact(q, k, b, mask):
    rows = lax.broadcasted_iota(jnp.int32, (HGRN_CHUNK, 1), 0)
    cols = lax.broadcasted_iota(jnp.int32, (HGRN_CHUNK, HGRN_CHUNK), 1)

    def it(s, a):
        pick = rows == s
        b_s = jnp.sum(jnp.where(pick, b, 0.0), axis=0, keepdims=True)
        k_s = jnp.sum(jnp.where(pick, k, 0.0), axis=0, keepdims=True)
        col = jnp.sum(q * jnp.exp(jnp.minimum(b - b_s, 0.0)) * k_s, axis=1, keepdims=True)
        return jnp.where(cols == s, col, a)

    a = lax.fori_loop(0, HGRN_CHUNK, it, jnp.zeros((HGRN_CHUNK, HGRN_CHUNK), F32))
    return jnp.where(mask, a, 0.0)


def _intra_exact_bwd(q, k, b, da):
    rows = lax.broadcasted_iota(jnp.int32, (HGRN_CHUNK, 1), 0)
    cols = lax.broadcasted_iota(jnp.int32, (HGRN_CHUNK, HGRN_CHUNK), 1)

    def it(s, carry):
        dq, dk = carry
        pick = rows == s
        b_s = jnp.sum(jnp.where(pick, b, 0.0), axis=0, keepdims=True)
        k_s = jnp.sum(jnp.where(pick, k, 0.0), axis=0, keepdims=True)
        w = jnp.sum(jnp.where(cols == s, da, 0.0), axis=1, keepdims=True) * jnp.exp(jnp.minimum(b - b_s, 0.0))
        dq = dq + w * k_s
        dk = jnp.where(pick, jnp.sum(w * q, axis=0, keepdims=True), dk)
        return dq, dk

    z = jnp.zeros((HGRN_CHUNK, HGRN_KEY), F32)
    return lax.fori_loop(0, HGRN_CHUNK, it, (z, z))


def _chunk_cumsum(g, from_end):
    n = g.shape[0]
    row = lax.broadcasted_iota(jnp.int32, g.shape, 0)
    x, s = g, 1
    while s < n:
        if from_end:
            x = x + jnp.where(row < n - s, pltpu.roll(x, n - s, 0), 0.0)
        else:
            x = x + jnp.where(row >= s, pltpu.roll(x, s, 0), 0.0)
        s *= 2
    return x


def _head(a, h):
    return a[:, h * LANES:(h + 1) * LANES]


def _block_is_mild(zf_ref, lb, nch):
    def lowest_total():
        g = jnp.log(lb + (1.0 - lb) * _sigmoid(zf_ref[...]))
        lows = [jnp.min(jnp.sum(g[c * HGRN_CHUNK:(c + 1) * HGRN_CHUNK], axis=0, keepdims=True)) for c in range(nch)]
        return functools.reduce(jnp.minimum, lows)

    floor = jnp.min(HGRN_CHUNK * jnp.log(lb))
    return lax.cond(floor >= FACTORED_DECAY_LIMIT, lambda: floor, lowest_total) >= FACTORED_DECAY_LIMIT


def _hgrn_fwd(z, lb, rev, *, name, guest=None):
    t = z.shape[0]
    rb = min(HGRN_ROWS, t)
    nb, nch = t // rb, rb // HGRN_CHUNK
    heads = range(HGRN_HEADS)

    def blk(n):
        return nb - 1 - n if rev else n

    def body(zq_ref, zf_ref, zv_ref, lb_ref, o_ref, s_ref, st_ref):
        @pl.when(pl.program_id(0) == 0)
        def _():
            st_ref[...] = jnp.zeros_like(st_ref)
        lbv = lb_ref[...]
        cmask = _cum_mask(rev)

        def chunk(c, carry, mild):
            cc = nch - 1 - c if rev else c
            sl = pl.ds(pl.multiple_of(cc * HGRN_CHUNK, HGRN_CHUNK), HGRN_CHUNK)
            q, _, f, k = _gates(zq_ref[sl, :], zf_ref[sl, :], lbv)
            v = zv_ref[sl, :].astype(BF16)
            g = jnp.log(f)
            b = _chunk_cumsum(g, rev)
            tot = jnp.sum(g, axis=0, keepdims=True)
            qd = (q * jnp.exp(b)).astype(BF16)
            kd = (k * jnp.exp(tot - b)).astype(BF16)
            etot = jnp.exp(tot)

            def factored():
                kf = (k * jnp.exp(-b)).astype(BF16)
                return tuple(jnp.where(cmask, _dot(_head(qd, h), _head(kf, h), NT), 0.0) for h in heads)

            def exact():
                return tuple(_intra_exact(_head(q, h), _head(k, h), _head(b, h), cmask) for h in heads)

            a = factored() if mild else lax.cond(jnp.min(tot) >= FACTORED_DECAY_LIMIT, factored, exact)
            for h in heads:
                st = st_ref[h]
                st_bf = st.astype(BF16)
                s_ref[h, cc] = st_bf
                o_ref[sl, h * LANES:(h + 1) * LANES] = (
                    _dot(_head(qd, h), st_bf, NT) + _dot(a[h].astype(BF16), _head(v, h), NN))
                st_ref[h] = st * _head(etot, h) + _dot(_head(v, h), _head(kd, h), TN)
            return carry

        lax.cond(_block_is_mild(zf_ref, lbv, nch),
                 lambda: lax.fori_loop(0, nch, functools.partial(chunk, mild=True), 0, unroll=HGRN_UNROLL_FWD),
                 lambda: lax.fori_loop(0, nch, functools.partial(chunk, mild=False), 0))

    def col(j):
        return pl.BlockSpec((rb, D_MODEL), lambda n: (blk(n), j))

    return _call(
        body, grid=(nb,),
        in_specs=[col(0), col(2 if rev else 1), col(3), _full((1, D_MODEL))],
        out_specs=[col(0), pl.BlockSpec((HGRN_HEADS, nch, LANES, LANES), lambda n: (0, blk(n), 0, 0))],
        out_shape=[jax.ShapeDtypeStruct((t, D_MODEL), F32),
                   jax.ShapeDtypeStruct((HGRN_HEADS, t // HGRN_CHUNK, LANES, LANES), BF16)],
        scratch_shapes=[pltpu.VMEM((HGRN_HEADS, LANES, LANES), F32)],
        args=(z, z, z, lb), sem=("arbitrary",), name=name, guest=guest)


def _hgrn_bwd(z, lb, d_o, states, rev, *, name, other=None):
    t = z.shape[0]
    rb = min(HGRN_ROWS, t)
    nb, nch = t // rb, rb // HGRN_CHUNK
    heads = range(HGRN_HEADS)
    n_other = 0 if other is None else 2
    acc_dtype = F32 if other is None else BF16

    def blk(n):
        return n if rev else nb - 1 - n

    def body(zq_ref, zf_ref, zv_ref, lb_ref, do_ref, s_ref, *rest):
        oq_ref, ov_ref = rest[:n_other] if other is not None else (None, None)
        dq_ref, dzf_ref, dv_ref, dlb_ref, dst_ref = rest[n_other:]

        @pl.when(pl.program_id(0) == 0)
        def _():
            dst_ref[...] = jnp.zeros_like(dst_ref)
            dlb_ref[...] = jnp.zeros_like(dlb_ref)
        lbv = lb_ref[...]
        cmask = _cum_mask(rev)

        def chunk(c, carry, mild):
            cc = c if rev else nch - 1 - c
            sl = pl.ds(pl.multiple_of(cc * HGRN_CHUNK, HGRN_CHUNK), HGRN_CHUNK)
            zq = zq_ref[sl, :]
            q, sig, f, k = _gates(zq, zf_ref[sl, :], lbv)
            v = zv_ref[sl, :].astype(BF16)
            g = jnp.log(f)
            b = _chunk_cumsum(g, rev)
            tot = jnp.sum(g, axis=0, keepdims=True)
            eb = jnp.exp(b)
            etb = jnp.exp(tot - b)
            etot = jnp.exp(tot)
            qd = (q * eb).astype(BF16)
            kd = (k * etb).astype(BF16)
            do = do_ref[sl, :].astype(BF16)
            da = [jnp.where(cmask, _dot(_head(do, h), _head(v, h), NT), 0.0) for h in heads]

            def factored():
                ke = jnp.exp(-b)
                kf = (k * ke).astype(BF16)
                res = []
                for h in heads:
                    qh, kh = _head(qd, h), _head(kf, h)
                    da_bf = da[h].astype(BF16)
                    dqf, dkf = _dot(da_bf, kh, NN), _dot(da_bf, qh, TN)
                    res.append((jnp.where(cmask, _dot(qh, kh, NT), 0.0), dqf * _head(eb, h), dkf * _head(ke, h),
                                qh.astype(F32) * dqf - kh.astype(F32) * dkf))
                return tuple(res)

            def exact():
                res = []
                for h in heads:
                    qh, kh, bh = _head(q, h), _head(k, h), _head(b, h)
                    dq_i, dk_i = _intra_exact_bwd(qh, kh, bh, da[h])
                    res.append((_intra_exact(qh, kh, bh, cmask), dq_i, dk_i, qh * dq_i - kh * dk_i))
                return tuple(res)

            intra = factored() if mild else lax.cond(jnp.min(tot) >= FACTORED_DECAY_LIMIT, factored, exact)
            dq_p, dk_p, db_p, dtot_p = [], [], [], []
            for h in heads:
                a, dq_i, dk_i, db_i = intra[h]
                doh, vh, qh, kh = _head(do, h), _head(v, h), _head(q, h), _head(k, h)
                st0 = s_ref[h, cc]
                dst = dst_ref[h]
                dst_bf = dst.astype(BF16)
                dv = _dot(a.astype(BF16), doh, TN) + _dot(_head(kd, h), dst_bf, NT)
                if ov_ref is not None:
                    dv = dv + ov_ref[sl, h * LANES:(h + 1) * LANES]
                dv_ref[sl, h * LANES:(h + 1) * LANES] = dv.astype(acc_dtype)
                dq_x = _dot(doh, st0, NN) * _head(eb, h)
                dk_x = _dot(vh, dst_bf, NN) * _head(etb, h)
                dtot_p.append(jnp.sum(kh * dk_x, axis=0, keepdims=True)
                              + _head(etot, h) * jnp.sum(dst * st0.astype(F32), axis=0, keepdims=True))
                dst_ref[h] = dst * _head(etot, h) + _dot(doh, _head(qd, h), TN)
                dq_p.append(dq_i + dq_x)
                dk_p.append(dk_i + dk_x)
                db_p.append(db_i + qh * dq_x - kh * dk_x)
            dq, dk, db = (jnp.concatenate(x, axis=1) for x in (dq_p, dk_p, db_p))
            dg = _chunk_cumsum(db, not rev) + jnp.concatenate(dtot_p, axis=1)
            sq = _sigmoid(zq)
            dq_pre = dq * sq * (1.0 + zq * (1.0 - sq))
            if oq_ref is not None:
                dq_pre = dq_pre + oq_ref[sl, :]
            dq_ref[sl, :] = dq_pre.astype(acc_dtype)
            dfk = dg / f - dk
            dzf_ref[sl, :] = (dfk * (1.0 - lbv) * sig * (1.0 - sig)).astype(BF16)
            dlb_ref[...] += jnp.sum(dfk * (1.0 - sig), axis=0, keepdims=True)
            return carry

        lax.cond(_block_is_mild(zf_ref, lbv, nch),
                 lambda: lax.fori_loop(0, nch, functools.partial(chunk, mild=True), 0, unroll=HGRN_UNROLL),
                 lambda: lax.fori_loop(0, nch, functools.partial(chunk, mild=False), 0))

    def col(j):
        return pl.BlockSpec((rb, D_MODEL), lambda n: (blk(n), j))

    return pl.pallas_call(
        body, grid=(nb,),
        in_specs=[col(0), col(2 if rev else 1), col(3), _full((1, D_MODEL)), col(0),
                  pl.BlockSpec((HGRN_HEADS, nch, LANES, LANES), lambda n: (0, blk(n), 0, 0))] + [col(0)] * n_other,
        out_specs=[col(0), col(0), col(0), _full((1, D_MODEL))],
        out_shape=[jax.ShapeDtypeStruct((t, D_MODEL), acc_dtype), jax.ShapeDtypeStruct((t, D_MODEL), BF16),
                   jax.ShapeDtypeStruct((t, D_MODEL), acc_dtype), jax.ShapeDtypeStruct((1, D_MODEL), F32)],
        scratch_shapes=[pltpu.VMEM((HGRN_HEADS, LANES, LANES), F32)],
        compiler_params=_cp("arbitrary"), name=name)(z, z, z, lb, d_o, states, *(other or ()))


def _hgrn_post_fwd(o_f, o_b, z, norm_g, *, name):
    t = o_f.shape[0]
    tm = _rows(t)

    def body(of_ref, ob_ref, gate_ref, ng_ref, y_ref):
        ng = ng_ref[...]
        for h in range(HGRN_HEADS):
            sl = slice(h * LANES, (h + 1) * LANES)
            o = of_ref[:, sl] + ob_ref[:, sl]
            r = lax.rsqrt(jnp.mean(o * o, axis=1, keepdims=True) + LN_EPS)
            gt = gate_ref[:, sl]
            y_ref[:, sl] = (o * r * ng * (gt * _sigmoid(gt))).astype(BF16)

    row = pl.BlockSpec((tm, D_MODEL), lambda i: (i, 0))
    return pl.pallas_call(
        body, grid=(t // tm,),
        in_specs=[row, row, pl.BlockSpec((tm, D_MODEL), lambda i: (i, 4)), _full((1, LANES))],
        out_specs=row, out_shape=jax.ShapeDtypeStruct((t, D_MODEL), BF16),
        compiler_params=_cp("parallel"), name=name)(o_f, o_b, z, norm_g)


def _hgrn_post_bwd(dy, o_f, o_b, z, norm_g, *, name):
    t = o_f.shape[0]
    tm = _rows(t)

    def body(dy_ref, of_ref, ob_ref, gate_ref, ng_ref, do_ref, dgate_ref, dng_ref):
        ng = ng_ref[...]
        dng = jnp.zeros((1, LANES), F32)
        for h in range(HGRN_HEADS):
            sl = slice(h * LANES, (h + 1) * LANES)
            o = of_ref[:, sl] + ob_ref[:, sl]
            r = lax.rsqrt(jnp.mean(o * o, axis=1, keepdims=True) + LN_EPS)
            gt = gate_ref[:, sl]
            sg = _sigmoid(gt)
            dyv = dy_ref[:, sl].astype(F32)
            xn = o * r
            dgate_ref[:, sl] = (dyv * xn * ng * sg * (1.0 + gt * (1.0 - sg))).astype(BF16)
            dn = dyv * gt * sg
            dng = dng + jnp.sum(dn * xn, axis=0, keepdims=True)
            dxn = dn * ng
            do_ref[:, sl] = r * (dxn - xn * jnp.mean(dxn * xn, axis=1, keepdims=True))

        @pl.when(pl.program_id(0) == 0)
        def _():
            dng_ref[...] = jnp.zeros_like(dng_ref)
        dng_ref[...] += dng

    row = pl.BlockSpec((tm, D_MODEL), lambda i: (i, 0))
    return pl.pallas_call(
        body, grid=(t // tm,),
        in_specs=[row, row, row, pl.BlockSpec((tm, D_MODEL), lambda i: (i, 4)), _full((1, LANES))],
        out_specs=[row, row, _full((1, LANES))],
        out_shape=[jax.ShapeDtypeStruct((t, D_MODEL), F32), jax.ShapeDtypeStruct((t, D_MODEL), BF16),
                   jax.ShapeDtypeStruct((1, LANES), F32)],
        compiler_params=_cp("arbitrary"), name=name)(dy, o_f, o_b, z, norm_g)


def _lower_bounds(logits2d, *, name):
    def body(l_ref, lb_ref):
        l = l_ref[...]
        e = jnp.exp(l - jnp.max(l, axis=0, keepdims=True))
        sm = e / jnp.sum(e, axis=0, keepdims=True)
        s1, s2, s3 = sm[1:2], sm[2:3], sm[3:4]
        lb_ref[...] = jnp.concatenate([jnp.zeros_like(s1), s1, s1 + s2, s1 + s2 + s3], axis=0)

    return pl.pallas_call(body, out_shape=jax.ShapeDtypeStruct(logits2d.shape, F32), name=name)(logits2d)


def _lower_bounds_bwd(logits2d, dlb, *, name):
    def body(l_ref, d_ref, o_ref):
        l = l_ref[...]
        e = jnp.exp(l - jnp.max(l, axis=0, keepdims=True))
        sm = e / jnp.sum(e, axis=0, keepdims=True)
        d = d_ref[...]
        d1, d2, d3 = d[1:2], d[2:3], d[3:4]
        dsm = jnp.concatenate([jnp.zeros_like(d1), d1 + d2 + d3, d2 + d3, d3], axis=0)
        o_ref[...] = sm * (dsm - jnp.sum(sm * dsm, axis=0, keepdims=True))

    return pl.pallas_call(body, out_shape=jax.ShapeDtypeStruct(logits2d.shape, F32), name=name)(logits2d, dlb)


def _adamw(w, g, m, v, *, name):
    r, c = w.shape
    tr = r if r <= 512 else _pick_rows(r)

    def body(w_ref, g_ref, m_ref, v_ref, d_ref, nm_ref, nv_ref):
        gv = g_ref[...]
        nm = ADAM_B1 * m_ref[...] + (1.0 - ADAM_B1) * gv
        nv = ADAM_B2 * v_ref[...] + (1.0 - ADAM_B2) * (gv * gv)
        m_hat = nm / (1.0 - ADAM_B1 ** ADAM_STEP)
        v_hat = nv / (1.0 - ADAM_B2 ** ADAM_STEP)
        d_ref[...] = -ADAM_LR * (m_hat / (jnp.sqrt(v_hat) + ADAM_EPS) + ADAM_WD * w_ref[...])
        nm_ref[...] = nm
        nv_ref[...] = nv

    blk = pl.BlockSpec((tr, c), lambda i: (i, 0))
    shp = jax.ShapeDtypeStruct((r, c), F32)
    return pl.pallas_call(body, grid=(r // tr,), in_specs=[blk] * 4, out_specs=[blk] * 3, out_shape=[shp] * 3,
                          compiler_params=_cp("parallel"), name=name)(w, g, m, v)


def _pick_rows(r, cap=512):
    best = 8
    for d in range(8, cap + 1, 8):
        if r % d == 0:
            best = d
    assert r % best == 0, r
    return best


def _place():
    x, y, c = lax.axis_index("x"), lax.axis_index("y"), lax.axis_index("c")
    chips = [(1 - x, y), (x, 1 - y), (1 - x, 1 - y)]
    return x, y, c, chips


def _remote(src, dst, send_sem, recv_sem, to):
    return pltpu.make_async_remote_copy(src_ref=src, dst_ref=dst, send_sem=send_sem, recv_sem=recv_sem,
                                        device_id=to, device_id_type=MESH)


def _allreduce_small(block, *, name):
    m, n = block.shape

    def body(x_ref, sum_ref, all_ref, send_sems, recv_sems):
        x, y, c, chips = _place()
        me, sibling = (x, y, c), (x, y, 1 - c)

        def rows(px, py, pc):
            return all_ref.at[pl.ds((4 * px + 2 * py + pc) * m, m), :]

        all_ref[pl.ds((4 * x + 2 * y + c) * m, m), :] = x_ref[...]
        first = [_remote(x_ref, rows(*me), send_sems.at[0], recv_sems.at[0], sibling)]
        first += [_remote(x_ref, rows(*me), send_sems.at[1 + j], recv_sems.at[1 + j], (*chip, c)) for j, chip in enumerate(chips)]
        for cp in first:
            cp.start()
        passed = [_remote(rows(*chip, c), rows(*chip, c), send_sems.at[4 + j], recv_sems.at[4 + j], sibling)
                  for j, chip in enumerate(chips)]
        for j, chip in enumerate(chips):
            _remote(x_ref, rows(*chip, c), send_sems.at[1 + j], recv_sems.at[1 + j], me).wait_recv()
            passed[j].start()
        _remote(x_ref, rows(*sibling), send_sems.at[0], recv_sems.at[0], me).wait_recv()
        for j, chip in enumerate(chips):
            _remote(x_ref, rows(*chip, 1 - c), send_sems.at[4 + j], recv_sems.at[4 + j], me).wait_recv()
        for cp in first + passed:
            cp.wait_send()
        acc = all_ref[pl.ds(0, m), :]
        for d in range(1, 8):
            acc = acc + all_ref[pl.ds(d * m, m), :]
        sum_ref[...] = acc

    vmem = pl.BlockSpec(memory_space=pltpu.VMEM)
    return pl.pallas_call(
        body, in_specs=[vmem], out_specs=vmem, out_shape=jax.ShapeDtypeStruct((m, n), F32),
        scratch_shapes=[pltpu.VMEM((8 * m, n), F32), pltpu.SemaphoreType.DMA((7,)), pltpu.SemaphoreType.DMA((7,))],
        name=name)(block)


def _add_own_half(g, recv, c, *, name):
    n, r, w = g.shape
    hr = r // 2
    tr = _pick_rows(hr, 1024)
    nr = hr // tr

    def body(c_ref, g_ref, r_ref, o_ref):
        o_ref[...] = (g_ref[...] + r_ref[...]).astype(BF16)

    return pl.pallas_call(
        body,
        grid_spec=pltpu.PrefetchScalarGridSpec(
            num_scalar_prefetch=1, grid=(n, nr),
            in_specs=[pl.BlockSpec((None, tr, w), lambda s, i, c_ref: (s, c_ref[0] * nr + i, 0)),
                      pl.BlockSpec((None, tr, w), lambda s, i, c_ref: (s, i, 0))],
            out_specs=pl.BlockSpec((None, tr, w), lambda s, i, c_ref: (s, i, 0))),
        out_shape=jax.ShapeDtypeStruct((n, hr, w), BF16),
        compiler_params=_cp("parallel", "parallel"), name=name)(c, g, recv)


def _sum_chips(q, *, name):
    n, h, w = q.shape
    tr = _pick_rows(h, 1024)

    def body(a, b, c, d, o_ref):
        o_ref[...] = ((a[...].astype(F32) + b[...].astype(F32)) + c[...].astype(F32)) + d[...].astype(F32)

    specs = [pl.BlockSpec((None, tr, w), functools.partial(lambda i, s: (s, i, 0), s=s)) for s in range(n)]
    return pl.pallas_call(
        body, grid=(h // tr,), in_specs=specs, out_specs=pl.BlockSpec((tr, w), lambda i: (i, 0)),
        out_shape=jax.ShapeDtypeStruct((h, w), F32), compiler_params=_cp("parallel"), name=name)(q, q, q, q)


PACK_W = 1024
BIG = (("att_w_qkv", 2), ("att_w_o", 1), ("hgrn_w_in", 2), ("hgrn_w_o", 1),
       ("ffn_w_in", 2), ("ffn_w_out", 1), ("ple_w_gate", 1), ("ple_w_proj", 2))
LB_ROWS = 32


SMALL =("ln_mix_g", "ln_mix_b", "ln_ffn_g", "ln_ffn_b")
SMALL_ROWS = 32


def _pack_small(vals, lb):
    rows = [vals[n] for n in SMALL] + [lb.reshape(-1, PACK_W)]
    for n in ("att_sink", "hgrn_norm_g"):
        flat = vals[n].reshape(1, -1)
        rows.append(jnp.pad(flat, ((0, 0), (0, PACK_W - flat.shape[1]))))
    buf = jnp.concatenate(rows, axis=0)
    return jnp.pad(buf, ((0, SMALL_ROWS - buf.shape[0]), (0, 0)))


def _unpack_small(buf, lb_shape):
    out, r0 = {}, 0
    for n in SMALL:
        out[n] = buf[r0:r0 + DEPTH]
        r0 += DEPTH
    nlb = lb_shape[0] * lb_shape[1] * lb_shape[2] // PACK_W
    out["hgrn_lb_logits"] = buf[r0:r0 + nlb].reshape(lb_shape)
    r0 += nlb
    out["att_sink"] = buf[r0, :2 * N_Q_HEADS].reshape(2, N_Q_HEADS)
    out["hgrn_norm_g"] = buf[r0 + 1, :2 * LANES].reshape(2, LANES)
    return out


WEIGHTS = ("att_w_qkv", "att_sink", "att_w_o", "hgrn_w_in", "hgrn_lb_logits", "hgrn_norm_g", "hgrn_w_o", "ln_mix_g",
           "ln_mix_b", "ffn_w_in", "ffn_w_out", "ln_ffn_g", "ln_ffn_b", "ple_w_gate", "ple_w_proj")


def _gather_guest(packed):
    r, w = packed.shape
    hr = r // 2

    def copies(src_ref, out_ref, send_sems, recv_sems):
        x, y, c, chips = _place()
        sibling = (x, y, 1 - c)

        def half(cx, cy, hc):
            return out_ref.at[2 * cx + cy, pl.ds(hc * hr, hr), :]

        my_half = src_ref.at[pl.ds(c * hr, hr), :]
        first = [_remote(my_half, half(x, y, c), send_sems.at[j], recv_sems.at[j], (*chip, c)) for j, chip in enumerate(chips)]
        passed = [_remote(half(*chip, c), half(*chip, c), send_sems.at[3 + j], recv_sems.at[3 + j], sibling)
                  for j, chip in enumerate(chips)]
        from_chips = [_remote(my_half, half(*chip, c), send_sems.at[j], recv_sems.at[j], (*chip, c)) for j, chip in enumerate(chips)]
        from_sibling = [_remote(my_half, half(*chip, 1 - c), send_sems.at[3 + j], recv_sems.at[3 + j], sibling)
                        for j, chip in enumerate(chips)]
        return first, passed, from_chips, from_sibling

    def start(gi, go, gs):
        for cp in copies(gi[0], go[0], *gs)[0]:
            cp.start()

    def finish(gi, go, gs):
        first, passed, from_chips, from_sibling = copies(gi[0], go[0], *gs)
        for arrival, onward in zip(from_chips, passed):
            arrival.wait_recv()
            onward.start()
        for arrival in from_sibling:
            arrival.wait_recv()
        for cp in first + passed:
            cp.wait_send()

    return _Guest((packed,), (jax.ShapeDtypeStruct((N_CHIPS, r, w), packed.dtype),),
                  (pltpu.SemaphoreType.DMA((6,)), pltpu.SemaphoreType.DMA((6,))), start, finish)


def _pair_exchange_guest(g):
    n, r, w = g.shape
    hr = r // 2

    def copy(g_ref, out_ref, send_sem, recv_sem):
        x, y, c, _ = _place()
        return _remote(g_ref.at[:, pl.ds((1 - c) * hr, hr), :], out_ref, send_sem, recv_sem, (x, y, 1 - c))

    return _Guest((g,), (jax.ShapeDtypeStruct((n, hr, w), g.dtype),), (pltpu.SemaphoreType.DMA, pltpu.SemaphoreType.DMA),
                  lambda gi, go, gs: copy(gi[0], go[0], *gs).start(),
                  lambda gi, go, gs: copy(gi[0], go[0], *gs).wait())


def _chip_scatter_guest(part):
    n, h, w = part.shape

    def copies(p_ref, out_ref, send_sems, recv_sems):
        x, y, c, chips = _place()
        me = 2 * x + y
        sends = [_remote(p_ref.at[2 * cx + cy], out_ref.at[me], send_sems.at[j], recv_sems.at[j], (cx, cy, c))
                 for j, (cx, cy) in enumerate(chips)]
        arrivals = [_remote(p_ref.at[me], out_ref.at[2 * cx + cy], send_sems.at[j], recv_sems.at[j], (cx, cy, c))
                    for j, (cx, cy) in enumerate(chips)]
        return sends, arrivals

    def start(gi, go, gs):
        for cp in copies(gi[0], go[0], *gs)[0]:
            cp.start()

    def finish(gi, go, gs):
        sends, arrivals = copies(gi[0], go[0], *gs)
        for cp in arrivals:
            cp.wait_recv()
        for cp in sends:
            cp.wait_send()

    return _Guest((part,), (jax.ShapeDtypeStruct((n, h, w), part.dtype),),
                  (pltpu.SemaphoreType.DMA((3,)), pltpu.SemaphoreType.DMA((3,))), start, finish)


def _pair_share_guest(halves):
    n = len(halves)

    def copies(k, h_ref, out_ref, send_sems, recv_sems):
        x, y, c, _ = _place()
        send = _remote(h_ref, out_ref.at[c], send_sems.at[k], recv_sems.at[k], (x, y, 1 - c))
        arrival = _remote(h_ref, out_ref.at[1 - c], send_sems.at[k], recv_sems.at[k], (x, y, 1 - c))
        return send, arrival

    def start(gi, go, gs):
        for k in range(n):
            copies(k, gi[k], go[k], *gs)[0].start()

    def finish(gi, go, gs):
        for k in range(n):
            send, arrival = copies(k, gi[k], go[k], *gs)
            send.wait_send()
            arrival.wait_recv()

    return _Guest(tuple(halves), tuple(jax.ShapeDtypeStruct((2,) + a.shape, a.dtype) for a in halves),
                  (pltpu.SemaphoreType.DMA((n,)), pltpu.SemaphoreType.DMA((n,))), start, finish)


def _own(stack, idx):
    return lax.dynamic_index_in_dim(stack, idx, axis=0, keepdims=False)


def _put(buf, block, idx):
    return lax.dynamic_update_slice_in_dim(buf, block[None], idx, axis=0)


AXIS = dict(BIG)


def _layer_parts(i):
    j = i // 2
    mixer = (("att_w_qkv", j), ("att_w_o", j)) if i % 2 == 0 else (("hgrn_w_in", j), ("hgrn_w_o", j))
    return mixer + (("ffn_w_in", i), ("ffn_w_out", i), ("ple_w_gate", i), ("ple_w_proj", i))


def _gather_groups():
    first = _layer_parts(0)
    return [first[:1], first[1:] + _layer_parts(1), _layer_parts(2), _layer_parts(3)]


def _pack_parts(shards, parts):
    return jnp.concatenate([shards[n][l].reshape(-1, PACK_W) for n, l in parts], axis=0)


def _gathered_parts(buf, parts, shapes):
    out, r0 = {}, 0
    for n, l in parts:
        r, c = shapes[n][1:]
        nr = r * c // PACK_W
        sh = buf[:, r0:r0 + nr].reshape(N_CHIPS, r, c)
        out[(n, l)] = sh.reshape(N_CHIPS * r, c) if AXIS[n] == 1 else sh.transpose(1, 0, 2).reshape(r, N_CHIPS * c)
        r0 += nr
    return out, r0


def _unpack_layer(buf, i, shapes):
    out, r0 = {}, 0
    for n, _ in _layer_parts(i):
        r, c = shapes[n][1:]
        nr = r * c // PACK_W
        out[n] = buf[r0:r0 + nr].reshape(r, c)
        r0 += nr
    return out


def _layer_slabs(full, i, shapes):
    parts = []
    for n, _ in _layer_parts(i):
        r, c = shapes[n][1:]
        g = full[n]
        g = g.reshape(N_CHIPS, -1, PACK_W) if AXIS[n] == 1 else g.reshape(r, N_CHIPS, c).transpose(1, 0, 2).reshape(N_CHIPS, -1, PACK_W)
        parts.append(g)
    return jnp.concatenate(parts, axis=1)


def kernel(x, p, att_w_qkv, att_sink, att_w_o, hgrn_w_in, hgrn_lb_logits, hgrn_norm_g, hgrn_w_o, ln_mix_g, ln_mix_b, ffn_w_in, ffn_w_out, ln_ffn_g, ln_ffn_b, ple_w_gate, ple_w_proj, loss_target, m_att_w_qkv, m_att_sink, m_att_w_o, m_hgrn_w_in, m_hgrn_lb_logits, m_hgrn_norm_g, m_hgrn_w_o, m_ln_mix_g, m_ln_mix_b, m_ffn_w_in, m_ffn_w_out, m_ln_ffn_g, m_ln_ffn_b, m_ple_w_gate, m_ple_w_proj, v_att_w_qkv, v_att_sink, v_att_w_o, v_hgrn_w_in, v_hgrn_lb_logits, v_hgrn_norm_g, v_hgrn_w_o, v_ln_mix_g, v_ln_mix_b, v_ffn_w_in, v_ffn_w_out, v_ln_ffn_g, v_ln_ffn_b, v_ple_w_gate, v_ple_w_proj):
    wts = dict(att_w_qkv=att_w_qkv, att_sink=att_sink, att_w_o=att_w_o, hgrn_w_in=hgrn_w_in, hgrn_lb_logits=hgrn_lb_logits,
               hgrn_norm_g=hgrn_norm_g, hgrn_w_o=hgrn_w_o, ln_mix_g=ln_mix_g, ln_mix_b=ln_mix_b, ffn_w_in=ffn_w_in,
               ffn_w_out=ffn_w_out, ln_ffn_g=ln_ffn_g, ln_ffn_b=ln_ffn_b, ple_w_gate=ple_w_gate, ple_w_proj=ple_w_proj)
    mom = dict(att_w_qkv=m_att_w_qkv, att_sink=m_att_sink, att_w_o=m_att_w_o, hgrn_w_in=m_hgrn_w_in, hgrn_lb_logits=m_hgrn_lb_logits,
               hgrn_norm_g=m_hgrn_norm_g, hgrn_w_o=m_hgrn_w_o, ln_mix_g=m_ln_mix_g, ln_mix_b=m_ln_mix_b, ffn_w_in=m_ffn_w_in,
               ffn_w_out=m_ffn_w_out, ln_ffn_g=m_ln_ffn_g, ln_ffn_b=m_ln_ffn_b, ple_w_gate=m_ple_w_gate, ple_w_proj=m_ple_w_proj)
    var = dict(att_w_qkv=v_att_w_qkv, att_sink=v_att_sink, att_w_o=v_att_w_o, hgrn_w_in=v_hgrn_w_in, hgrn_lb_logits=v_hgrn_lb_logits,
               hgrn_norm_g=v_hgrn_norm_g, hgrn_w_o=v_hgrn_w_o, ln_mix_g=v_ln_mix_g, ln_mix_b=v_ln_mix_b, ffn_w_in=v_ffn_w_in,
               ffn_w_out=v_ffn_w_out, ln_ffn_g=v_ln_ffn_g, ln_ffn_b=v_ln_ffn_b, ple_w_gate=v_ple_w_gate, ple_w_proj=v_ple_w_proj)
    shapes = {n: wts[n].shape for n, _ in BIG}
    chip = 2 * lax.axis_index("x") + lax.axis_index("y")
    core = lax.axis_index("c").reshape(1).astype(jnp.int32)
    xin, target = x[0], loss_target[0]
    t = xin.shape[0]
    row = lambda a, i: a[i][None]

    bf_shards = {n: wts[n].astype(BF16) for n, _ in BIG}
    lb_sh = hgrn_lb_logits.shape
    lb_bits = lax.bitcast_convert_type(hgrn_lb_logits.reshape(-1), BF16).reshape(-1, PACK_W)
    groups = _gather_groups()
    packed = [_pack_parts(bf_shards, parts) for parts in groups]
    packed[0] = jnp.concatenate([packed[0], lb_bits, jnp.zeros((LB_ROWS - lb_bits.shape[0], PACK_W), BF16)], axis=0)

    gathered = _put(_run_guest(_gather_guest(packed[0]), name="gather_first")[0], packed[0], chip)
    wfull, r_big = _gathered_parts(gathered, groups[0], shapes)
    lb_rows = gathered[:, r_big:r_big + lb_bits.shape[0]].reshape(N_CHIPS, -1, 2)
    lb_full = lax.bitcast_convert_type(lb_rows, F32).reshape(N_CHIPS, lb_sh[0], lb_sh[1], lb_sh[2])
    lb_full = lb_full.transpose(1, 2, 0, 3).reshape(lb_sh[0], lb_sh[1], N_CHIPS * lb_sh[2])
    logits2d = lb_full.reshape(DEPTH, 2 * D_MODEL)
    lb_all = _lower_bounds(logits2d, name="lb_fwd").reshape(DEPTH, 2, 1, D_MODEL)
    tables = _rope_tables(t)

    saved, weights = [], []
    xf, xb = xin, xin.astype(BF16)
    for i in range(DEPTH):
        j = i // 2
        nxt = _gather_guest(packed[i + 1]) if i + 1 < DEPTH else None
        s = dict(xin_bf=xb)
        if i % 2 == 0:
            q, k, v = _qkv_rope(xb, wfull[("att_w_qkv", j)], tables, name=f"qkv_rope_{i}")
            (o, lse), got = _attn_fwd(q, k, v, att_sink[j], name=f"attn_fwd_{i}", guest=nxt)
            s.update(q=q, k=k, v=v, o=o, lse=lse)
            mix_in = o
        else:
            z = _mm_nn(xb, wfull[("hgrn_w_in", j)], out_dtype=F32, name=f"hgrn_in_{i}")
            (o_f, s_f), _ = _hgrn_fwd(z, lb_all[i, 0], False, name=f"hgrn_scan_f_{i}")
            (o_b, s_b), _ = _hgrn_fwd(z, lb_all[i, 1], True, name=f"hgrn_scan_b_{i}")
            og = _hgrn_post_fwd(o_f, o_b, z, row(hgrn_norm_g, j), name=f"hgrn_post_{i}")
            s.update(z=z, o_f=o_f, o_b=o_b, s_f=s_f, s_b=s_b, og=og)
            mix_in = og
        in_mixer = nxt is not None and i % 2 == 0
        if in_mixer:
            wfull.update(_gathered_parts(_put(got[0], packed[i + 1], chip), groups[i + 1], shapes)[0])
        w = {n: wfull[(n, l)] for n, l in _layer_parts(i)}
        w_o = w["att_w_o" if i % 2 == 0 else "hgrn_w_o"]
        x1, x1b, xh1, rs1 = _mm_res_ln(mix_in, w_o, xf, row(ln_mix_g, i), row(ln_mix_b, i), name=f"mix_out_ln_{i}")
        (g, u, h), got = _ffn_in(x1b, w["ffn_w_in"], name=f"ffn_in_{i}", guest=None if in_mixer else nxt)
        if nxt is not None and not in_mixer:
            wfull.update(_gathered_parts(_put(got[0], packed[i + 1], chip), groups[i + 1], shapes)[0])
        x2, x2b, xh2, rs2 = _mm_res_ln(h, w["ffn_w_out"], x1, row(ln_ffn_g, i), row(ln_ffn_b, i), name=f"ffn_out_ln_{i}")
        xf, xb, gate, pp = _ple_fwd(x2, x2b, p, i, w["ple_w_gate"], w["ple_w_proj"], name=f"ple_fwd_{i}")
        s.update(x1b=x1b, xh1=xh1, rs1=rs1, g=g, u=u, h=h, x2b=x2b, xh2=xh2, rs2=rs2, gate=gate, pp=pp)
        saved.append(s)
        weights.append(w)

    loss, dx = _loss_head(xf, target, name="loss_head")
    loss = lax.psum(loss[0, 0], ("x", "y", "c"))

    gs = {n: [None] * DEPTH for n in SMALL}
    gs.update(att_sink=[None] * 2, hgrn_norm_g=[None] * 2)
    dlb = [jnp.zeros((2, D_MODEL), F32)] * DEPTH
    halves = [None] * DEPTH
    slabs = None
    for i in reversed(range(DEPTH)):
        j = i // 2
        s, w = saved[i], weights[i]
        gw = {}
        res, got = _ple_bwd(dx, s["gate"], s["pp"], w["ple_w_gate"], s["xh2"], s["rs2"], row(ln_ffn_g, i), name=f"ple_bwd_{i}",
                            guest=None if slabs is None else _pair_exchange_guest(slabs))
        du2, du2b, dpre, dpp, gs["ln_ffn_g"][i], gs["ln_ffn_b"][i] = res
        part = None if slabs is None else _add_own_half(slabs, got[0], core, name=f"grad_pair_add_{i + 1}")
        gw["ple_w_gate"] = _mm_tn(s["x2b"], dpre, name=f"dw_ple_gate_{i}")
        gw["ple_w_proj"] = _mm_tn_p(p, i, dpp, name=f"dw_ple_proj_{i}")
        dgg, dgu = _ffn_out_bwd(du2b, w["ffn_w_out"], s["g"], s["u"], name=f"ffn_out_bwd_{i}")
        gw["ffn_w_out"] = _mm_tn(s["h"], du2b, name=f"dw_ffn_out_{i}")
        res = _mm_nt([dgg, dgu], w["ffn_w_in"], resid=du2, ln=(s["xh1"], s["rs1"], row(ln_mix_g, i)),
                     name=f"ffn_in_bwd_{i}", guest=None if part is None else _chip_scatter_guest(part))
        (du1, du1b, gs["ln_mix_g"][i], gs["ln_mix_b"][i]), got = res if part is not None else (res, None)
        if part is not None:
            halves[i + 1] = _sum_chips(_put(got[0], _own(part, chip), chip), name=f"grad_chip_sum_{i + 1}")
        gw["ffn_w_in"] = jnp.concatenate(
            [_mm_tn(s["x1b"], dgg, name=f"dw_ffn_gate_{i}"), _mm_tn(s["x1b"], dgu, name=f"dw_ffn_up_{i}")], axis=1)
        if i % 2 == 0:
            gw["att_w_o"] = _mm_tn(s["o"], du1b, name=f"dw_att_o_{i}")
            do = _mm_nt([du1b], w["att_w_o"], out_dtype=BF16, name=f"att_o_bwd_{i}")
            dq, dkw, dvw, gs["att_sink"][j] = _attn_bwd(s["q"], s["k"], s["v"], s["o"], do, s["lse"], att_sink[j], name=f"attn_bwd_{i}")
            dqkv = _attn_post_bwd(dq, dkw, dvw, tables, name=f"attn_post_bwd_{i}")
            gw["att_w_qkv"] = _mm_tn(s["xin_bf"], dqkv, name=f"dw_att_qkv_{i}")
            dx = _mm_nt([dqkv], w["att_w_qkv"], resid=du1, name=f"qkv_bwd_{i}")
        else:
            gw["hgrn_w_o"] = _mm_tn(s["og"], du1b, name=f"dw_hgrn_o_{i}")
            dy = _mm_nt([du1b], w["hgrn_w_o"], out_dtype=BF16, name=f"hgrn_o_bwd_{i}")
            d_o, dgate, gs["hgrn_norm_g"][j] = _hgrn_post_bwd(dy, s["o_f"], s["o_b"], s["z"], row(hgrn_norm_g, j), name=f"hgrn_post_bwd_{i}")
            dq_f, dzf_f, dv_f, dlb_f = _hgrn_bwd(s["z"], lb_all[i, 0], d_o, s["s_f"], False, name=f"hgrn_scan_f_bwd_{i}")
            dq, dzf_b, dv, dlb_b = _hgrn_bwd(s["z"], lb_all[i, 1], d_o, s["s_b"], True, name=f"hgrn_scan_b_bwd_{i}",
                                             other=(dq_f, dv_f))
            dlb[i] = jnp.stack([dlb_f.reshape(D_MODEL), dlb_b.reshape(D_MODEL)])
            dz = [dq, dzf_f, dzf_b, dv, dgate]
            gw["hgrn_w_in"] = jnp.concatenate(
                [_mm_tn(s["xin_bf"], part, name=f"dw_hgrn_in_{i}_{n}") for n, part in enumerate(dz)], axis=1)
            dx = _mm_nt(dz, w["hgrn_w_in"], resid=du1, name=f"hgrn_in_bwd_{i}")
        slabs = _layer_slabs(gw, i, shapes)

    from_sibling = _run_guest(_pair_exchange_guest(slabs), name="grad_pair_exchange_0")[0]
    part = _add_own_half(slabs, from_sibling, core, name="grad_pair_add_0")
    from_chips = _run_guest(_chip_scatter_guest(part), name="grad_chip_scatter_0")[0]
    halves[0] = _sum_chips(_put(from_chips, _own(part, chip), chip), name="grad_chip_sum_0")
    shared = _run_guest(_pair_share_guest(halves), name="grad_pair_share")
    reduced = [_put(buf, half, lax.axis_index("c")) for buf, half in zip(shared, halves)]

    g_layers = [_unpack_layer(reduced[i].reshape(-1, PACK_W), i, shapes) for i in range(DEPTH)]
    grads = {}
    for n, _ in BIG:
        per_layer = [g_layers[i][n] for i in range(DEPTH) if n in g_layers[i]]
        grads[n] = jnp.stack(per_layer)

    gsmall = {n: jnp.concatenate(v, axis=0) for n, v in gs.items()}
    dlogits = _lower_bounds_bwd(logits2d, jnp.stack(dlb).reshape(DEPTH, 2 * D_MODEL), name="lb_bwd").reshape(DEPTH, 2, D_MODEL)
    g_small_full = _unpack_small(_allreduce_small(_pack_small(gsmall, dlogits), name="allreduce_small"),
                                 (lb_sh[0], lb_sh[1], N_CHIPS * lb_sh[2]))
    grads.update({n: g_small_full[n] for n in SMALL + ("att_sink", "hgrn_norm_g")})
    grads["hgrn_lb_logits"] = lax.dynamic_slice_in_dim(g_small_full["hgrn_lb_logits"], chip * lb_sh[2], lb_sh[2], axis=2)

    delta, new_m, new_v = {}, {}, {}
    for n, _ in BIG:
        two_d = lambda a: a.reshape(-1, a.shape[-1])
        d, nm, nv = _adamw(two_d(wts[n]), two_d(grads[n]), two_d(mom[n]), two_d(var[n]), name=f"adamw_{n}")
        delta[n], new_m[n], new_v[n] = d.reshape(shapes[n]), nm.reshape(shapes[n]), nv.reshape(shapes[n])
    packs = [_pack_small(src, src["hgrn_lb_logits"]) for src in (wts, grads, mom, var)]
    outs = _adamw(*packs, name="adamw_small")
    for dst, buf in zip((delta, new_m, new_v), outs):
        dst.update(_unpack_small(buf, lb_sh))

    return (loss, dx[None], *[grads[n] for n in WEIGHTS], *[delta[n] for n in WEIGHTS],
            *[new_m[n] for n in WEIGHTS], *[new_v[n] for n in WEIGHTS])
```

```python
import functools
from typing import Callable, NamedTuple

import jax
import jax.numpy as jnp
from jax import lax
from jax.experimental import pallas as pl
from jax.experimental.pallas import tpu as pltpu

F32, BF16 = jnp.float32, jnp.bfloat16

D_MODEL = 1024
DEPTH = 4
HEAD_DIM = 64
N_Q_HEADS = 16
N_KV_HEADS = 4
GROUP = 4
Q_DIM = 1024
KV_DIM = 256
WINDOW = 128
ROPE_DIM = 16
ROPE_THETA = 500000.0
HGRN_HEADS = 8
HGRN_KEY = 128
HGRN_CHUNK = 64
D_FF = 2816
PLE_DIM = 256
ALPHA = (2 * DEPTH) ** 0.25
LN_EPS = 1e-5
ADAM_LR, ADAM_B1, ADAM_B2, ADAM_EPS, ADAM_WD, ADAM_STEP = 0.001, 0.9, 0.999, 1e-08, 0.01, 10

LANES = 128
VMEM_LIMIT_BYTES = 56 * 2**20
FACTORED_DECAY_LIMIT = -80.0

NN = ((1,), (0,))
NT = ((1,), (1,))
TN = ((0,), (0,))

N_CHIPS = 4
MESH = pl.DeviceIdType.MESH


def _dot(a, b, dims):
    return lax.dot_general(a, b, (dims, ((), ())), preferred_element_type=F32)


def _cp(*sem):
    return pltpu.CompilerParams(dimension_semantics=sem, vmem_limit_bytes=VMEM_LIMIT_BYTES)


def _rows(t, cap=512):
    return min(cap, t)


def _pick(n, cap):
    best = None
    for d in range(LANES, min(n, cap) + 1, LANES):
        if n % d == 0:
            best = d
    assert best is not None, (n, cap)
    return best


def _sigmoid(x):
    return jax.nn.sigmoid(x)


def _ln_fwd(u, g, b):
    mu = jnp.mean(u, axis=-1, keepdims=True)
    xc = u - mu
    var = jnp.mean(xc * xc, axis=-1, keepdims=True)
    rstd = lax.rsqrt(var + LN_EPS)
    xhat = xc * rstd
    return xhat * g + b, xhat, rstd


def _ln_bwd(dy, xhat, rstd, g):
    dxh = dy * g
    m1 = jnp.mean(dxh, axis=-1, keepdims=True)
    m2 = jnp.mean(dxh * xhat, axis=-1, keepdims=True)
    du = rstd * (dxh - m1 - xhat * m2)
    return du, jnp.sum(dy * xhat, axis=0, keepdims=True), jnp.sum(dy, axis=0, keepdims=True)


def _full(shape):
    nd = len(shape)
    return pl.BlockSpec(shape, lambda *_: (0,) * nd)


ANY = pl.BlockSpec(memory_space=pl.ANY)


class _Guest(NamedTuple):
    args: tuple
    out_shape: tuple
    sems: tuple
    start: Callable
    finish: Callable


def _call(body, *, grid, in_specs, out_specs, out_shape, args, sem, name, scratch_shapes=(), guest=None):
    n_in, n_out, n_scr = len(args), len(out_shape), len(scratch_shapes)
    if guest is None:
        res = pl.pallas_call(body, grid=grid, in_specs=list(in_specs), out_specs=list(out_specs), out_shape=list(out_shape),
                             scratch_shapes=list(scratch_shapes), compiler_params=_cp(*sem), name=name)(*args)
        return list(res), []
    g_in, g_out = len(guest.args), len(guest.out_shape)

    def hosted(*refs):
        cuts = [n_in, g_in, n_out, g_out, n_scr]
        parts, pos = [], 0
        for n in cuts:
            parts.append(refs[pos:pos + n])
            pos += n
        h_in, gi, h_out, go, h_scr = parts
        gs = refs[pos:]
        ids = [pl.program_id(d) for d in range(len(grid))]
        first = functools.reduce(jnp.logical_and, [i == 0 for i in ids])
        last = functools.reduce(jnp.logical_and, [i == n - 1 for i, n in zip(ids, grid)])

        @pl.when(first)
        def _():
            guest.start(gi, go, gs)
        body(*h_in, *h_out, *h_scr)

        @pl.when(last)
        def _():
            guest.finish(gi, go, gs)

    res = pl.pallas_call(
        hosted, grid=grid, in_specs=list(in_specs) + [ANY] * g_in, out_specs=list(out_specs) + [ANY] * g_out,
        out_shape=list(out_shape) + list(guest.out_shape), scratch_shapes=list(scratch_shapes) + list(guest.sems),
        compiler_params=_cp(*(("arbitrary",) * len(grid))), name=name)(*args, *guest.args)
    return list(res[:n_out]), list(res[n_out:])


def _run_guest(guest, *, name):
    g_in = len(guest.args)

    def body(*refs):
        gi, go, gs = refs[:g_in], refs[g_in:g_in + len(guest.out_shape)], refs[g_in + len(guest.out_shape):]
        guest.start(gi, go, gs)
        guest.finish(gi, go, gs)

    return pl.pallas_call(body, in_specs=[ANY] * g_in, out_specs=[ANY] * len(guest.out_shape), out_shape=list(guest.out_shape),
                          scratch_shapes=list(guest.sems), name=name)(*guest.args)


def _mm_nn(a, w, *, out_dtype, name):
    t, k = a.shape
    n = w.shape[1]
    tm, tn = _rows(t), _pick(n, 1408)

    def body(a_ref, w_ref, o_ref):
        o_ref[...] = _dot(a_ref[...], w_ref[...], NN).astype(out_dtype)

    return pl.pallas_call(
        body, grid=(n // tn, t // tm),
        in_specs=[pl.BlockSpec((tm, k), lambda j, i: (i, 0)), pl.BlockSpec((k, tn), lambda j, i: (0, j))],
        out_specs=pl.BlockSpec((tm, tn), lambda j, i: (i, j)),
        out_shape=jax.ShapeDtypeStruct((t, n), out_dtype),
        compiler_params=_cp("parallel", "parallel"), name=name)(a, w)


def _mm_tn(a, b, *, name, guest=None, shard_cols=None):
    r, m = a.shape
    n = b.shape[1]
    tr, tm, tn = _rows(r), _pick(m, 1408), _pick(n, 2816)
    c = tn if shard_cols is None else shard_cols
    per = tn // c

    def body(a_ref, b_ref, o_ref):
        @pl.when(pl.program_id(2) == 0)
        def _():
            o_ref[...] = jnp.zeros_like(o_ref)
        res = _dot(a_ref[...].astype(BF16), b_ref[...].astype(BF16), TN)
        if shard_cols is None:
            o_ref[...] += res
        else:
            for s in range(per):
                o_ref[s] += res[:, s * c:(s + 1) * c]

    if shard_cols is None:
        out_spec, out_shape = pl.BlockSpec((tm, tn), lambda i, j, k: (i, j)), jax.ShapeDtypeStruct((m, n), F32)
    else:
        out_spec, out_shape = pl.BlockSpec((per, tm, c), lambda i, j, k: (j, i, 0)), jax.ShapeDtypeStruct((n // c, m, c), F32)
    res, extra = _call(
        body, grid=(m // tm, n // tn, r // tr),
        in_specs=[pl.BlockSpec((tr, tm), lambda i, j, k: (k, i)), pl.BlockSpec((tr, tn), lambda i, j, k: (k, j))],
        out_specs=[out_spec], out_shape=[out_shape], args=(a, b),
        sem=("parallel", "parallel", "arbitrary"), name=name, guest=guest)
    return res[0] if guest is None else (res[0], extra)


def _mm_tn_p(p, layer, b, *, name):
    t = p.shape[2]
    n = b.shape[1]
    tr = _rows(t)

    def body(a_ref, b_ref, o_ref):
        @pl.when(pl.program_id(0) == 0)
        def _():
            o_ref[...] = jnp.zeros_like(o_ref)
        o_ref[...] += _dot(a_ref[...].astype(BF16), b_ref[...], TN)

    return pl.pallas_call(
        body, grid=(t // tr,),
        in_specs=[pl.BlockSpec((None, None, tr, PLE_DIM), lambda k: (layer, 0, k, 0)),
                  pl.BlockSpec((tr, n), lambda k: (k, 0))],
        out_specs=pl.BlockSpec((PLE_DIM, n), lambda k: (0, 0)),
        out_shape=jax.ShapeDtypeStruct((PLE_DIM, n), F32),
        compiler_params=_cp("arbitrary"), name=name)(p, b)


def _mm_res_ln(a, w, resid, g, b, *, name):
    t, k = a.shape
    tm = _rows(t)

    def body(a_ref, w_ref, r_ref, g_ref, b_ref, y_ref, ybf_ref, xh_ref, rs_ref):
        acc = _dot(a_ref[...], w_ref[...], NN)
        y, xh, rs = _ln_fwd(ALPHA * r_ref[...] + acc, g_ref[...], b_ref[...])
        y_ref[...] = y
        ybf_ref[...] = y.astype(BF16)
        xh_ref[...] = xh
        rs_ref[...] = rs

    row = pl.BlockSpec((tm, D_MODEL), lambda i: (i, 0))
    return pl.pallas_call(
        body, grid=(t // tm,),
        in_specs=[pl.BlockSpec((tm, k), lambda i: (i, 0)), _full((k, D_MODEL)), row, _full((1, D_MODEL)), _full((1, D_MODEL))],
        out_specs=[row, row, row, pl.BlockSpec((tm, 1), lambda i: (i, 0))],
        out_shape=[jax.ShapeDtypeStruct((t, D_MODEL), F32), jax.ShapeDtypeStruct((t, D_MODEL), BF16),
                   jax.ShapeDtypeStruct((t, D_MODEL), F32), jax.ShapeDtypeStruct((t, 1), F32)],
        compiler_params=_cp("parallel"), name=name)(a, w, resid, g, b)


def _ffn_in(x_bf, w, *, name, guest=None):
    t = x_bf.shape[0]
    tm, tn = _rows(t), _pick(D_FF, 1408)
    nb = D_FF // tn
    assert w.shape == (2 * nb, D_MODEL, tn), w.shape

    def body(x_ref, wg_ref, wu_ref, dg_ref, du_ref, h_ref):
        x = x_ref[...]
        g = _dot(x, wg_ref[...], NN)
        u = _dot(x, wu_ref[...], NN)
        sig = _sigmoid(g)
        silu = g * sig
        dg_ref[...] = (u * sig * (1.0 + g * (1.0 - sig))).astype(BF16)
        du_ref[...] = silu.astype(BF16)
        h_ref[...] = (silu * u).astype(BF16)

    tile = pl.BlockSpec((tm, tn), lambda j, i: (i, j))
    shp = jax.ShapeDtypeStruct((t, D_FF), BF16)
    return _call(
        body, grid=(nb, t // tm),
        in_specs=[pl.BlockSpec((tm, D_MODEL), lambda j, i: (i, 0)),
                  pl.BlockSpec((None, D_MODEL, tn), lambda j, i: (j, 0, 0)),
                  pl.BlockSpec((None, D_MODEL, tn), lambda j, i: (j + nb, 0, 0))],
        out_specs=[tile, tile, tile], out_shape=[shp, shp, shp], args=(x_bf, w, w),
        sem=("parallel", "parallel"), name=name, guest=guest)


def _ple_fwd(x, x_bf, p, layer, wg, wp, *, name):
    t = x.shape[0]
    tm = _rows(t)

    def body(x_ref, xbf_ref, p_ref, wg_ref, wp_ref, y_ref, ybf_ref, gate_ref, pp_ref):
        gate = _sigmoid(_dot(xbf_ref[...], wg_ref[...], NN))
        pp = _dot(p_ref[...].astype(BF16), wp_ref[...], NN)
        y = x_ref[...] + gate * pp
        y_ref[...] = y
        ybf_ref[...] = y.astype(BF16)
        gate_ref[...] = gate.astype(BF16)
        pp_ref[...] = pp.astype(BF16)

    row = pl.BlockSpec((tm, D_MODEL), lambda i: (i, 0))
    f32, bf = jax.ShapeDtypeStruct((t, D_MODEL), F32), jax.ShapeDtypeStruct((t, D_MODEL), BF16)
    return pl.pallas_call(
        body, grid=(t // tm,),
        in_specs=[row, row, pl.BlockSpec((None, None, tm, PLE_DIM), lambda i: (layer, 0, i, 0)),
                  _full((D_MODEL, D_MODEL)), _full((PLE_DIM, D_MODEL))],
        out_specs=[row, row, row, row], out_shape=[f32, bf, bf, bf],
        compiler_params=_cp("parallel"), name=name)(x, x_bf, p, wg, wp)


def _loss_head(y, target, *, name):
    t = y.shape[0]
    tm = _rows(t)

    def body(y_ref, t_ref, loss_ref, dy_ref):
        e = y_ref[...] - t_ref[...]

        @pl.when(pl.program_id(0) == 0)
        def _():
            loss_ref[...] = jnp.zeros_like(loss_ref)
        sq = jnp.sum(jnp.sum(e * e, axis=1, keepdims=True), axis=0, keepdims=True)
        loss_ref[...] += sq * (0.5 / D_MODEL)
        dy_ref[...] = e * (1.0 / D_MODEL)

    row = pl.BlockSpec((tm, D_MODEL), lambda i: (i, 0))
    return pl.pallas_call(
        body, grid=(t // tm,), in_specs=[row, row],
        out_specs=[_full((1, 1)), row],
        out_shape=[jax.ShapeDtypeStruct((1, 1), F32), jax.ShapeDtypeStruct((t, D_MODEL), F32)],
        compiler_params=_cp("arbitrary"), name=name)(y, target)


def _ple_bwd(dx3, gate, pp, wg, xhat, rstd, g, *, name, guest=None):
    t = dx3.shape[0]
    tm = _rows(t)

    def body(dx_ref, gate_ref, pp_ref, wg_ref, xh_ref, rs_ref, g_ref,
             du_ref, dubf_ref, dpre_ref, dpp_ref, dg_ref, db_ref):
        dx = dx_ref[...]
        gate = gate_ref[...].astype(F32)
        dpre = (dx * pp_ref[...].astype(F32) * gate * (1.0 - gate)).astype(BF16)
        dpre_ref[...] = dpre
        dpp_ref[...] = (dx * gate).astype(BF16)
        dx2 = dx + _dot(dpre, wg_ref[...], NT)
        du, dg, db = _ln_bwd(dx2, xh_ref[...], rs_ref[...], g_ref[...])
        du_ref[...] = du
        dubf_ref[...] = du.astype(BF16)

        @pl.when(pl.program_id(0) == 0)
        def _():
            dg_ref[...] = jnp.zeros_like(dg_ref)
            db_ref[...] = jnp.zeros_like(db_ref)
        dg_ref[...] += dg
        db_ref[...] += db

    row = pl.BlockSpec((tm, D_MODEL), lambda i: (i, 0))
    vec = _full((1, D_MODEL))
    f32, bf = jax.ShapeDtypeStruct((t, D_MODEL), F32), jax.ShapeDtypeStruct((t, D_MODEL), BF16)
    v32 = jax.ShapeDtypeStruct((1, D_MODEL), F32)
    res, extra = _call(
        body, grid=(t // tm,),
        in_specs=[row, row, row, _full((D_MODEL, D_MODEL)), row, pl.BlockSpec((tm, 1), lambda i: (i, 0)), vec],
        out_specs=[row, row, row, row, vec, vec], out_shape=[f32, bf, bf, bf, v32, v32],
        args=(dx3, gate, pp, wg, xhat, rstd, g), sem=("arbitrary",), name=name, guest=guest)
    return res, extra


def _ffn_out_bwd(du_bf, w_out, g, u, *, name):
    t = du_bf.shape[0]
    tm, tn = _rows(t), _pick(D_FF, 1408)

    def body(du_ref, w_ref, hg_ref, hu_ref, dg_ref, dup_ref):
        dh = _dot(du_ref[...], w_ref[...], NT)
        dg_ref[...] = (dh * hg_ref[...].astype(F32)).astype(BF16)
        dup_ref[...] = (dh * hu_ref[...].astype(F32)).astype(BF16)

    tile = pl.BlockSpec((tm, tn), lambda j, i: (i, j))
    shp = jax.ShapeDtypeStruct((t, D_FF), BF16)
    return pl.pallas_call(
        body, grid=(D_FF // tn, t // tm),
        in_specs=[pl.BlockSpec((tm, D_MODEL), lambda j, i: (i, 0)), pl.BlockSpec((tn, D_MODEL), lambda j, i: (j, 0)),
                  tile, tile],
        out_specs=[tile, tile], out_shape=[shp, shp],
        compiler_params=_cp("parallel", "parallel"), name=name)(du_bf, w_out, g, u)


def _mm_nt(parts, w, *, resid=None, ln=None, out_dtype=F32, name, guest=None):
    t, kp = parts[0].shape
    npart = len(parts)
    tm = _rows(t)
    n_in = npart + 1 + (resid is not None) + (3 if ln is not None else 0)

    def body(*refs):
        a_refs, w_ref = refs[:npart], refs[npart]
        pos = npart + 1
        r_ref = None
        if resid is not None:
            r_ref = refs[pos]
            pos += 1
        if ln is not None:
            xh_ref, rs_ref, g_ref = refs[pos:pos + 3]
        outs = refs[n_in:]
        if w.ndim == 2:
            terms = [_dot(a_refs[pi][...], w_ref[:, pi * kp:(pi + 1) * kp], NT) for pi in range(npart)]
        else:
            c, per = w.shape[2], w.shape[0] // npart
            terms = [_dot(a_refs[pi][:, s * c:(s + 1) * c], w_ref[pi * per + s], NT) for pi in range(npart) for s in range(per)]
        val = functools.reduce(lambda u, v: u + v, terms)
        if r_ref is not None:
            val = val + ALPHA * r_ref[...]
        if ln is None:
            outs[0][...] = val.astype(out_dtype)
        else:
            du, dg, db = _ln_bwd(val, xh_ref[...], rs_ref[...], g_ref[...])
            outs[0][...] = du
            outs[1][...] = du.astype(BF16)

            @pl.when(pl.program_id(0) == 0)
            def _():
                outs[2][...] = jnp.zeros_like(outs[2])
                outs[3][...] = jnp.zeros_like(outs[3])
            outs[2][...] += dg
            outs[3][...] += db

    row = pl.BlockSpec((tm, D_MODEL), lambda i: (i, 0))
    vec = pl.BlockSpec((1, D_MODEL), lambda i: (0, 0))
    in_specs = [pl.BlockSpec((tm, kp), lambda i: (i, 0)) for _ in range(npart)]
    in_specs.append(pl.BlockSpec(w.shape, lambda i: (0,) * w.ndim, pipeline_mode=pl.Buffered(1)))
    args = list(parts) + [w]
    if resid is not None:
        in_specs.append(row)
        args.append(resid)
    if ln is not None:
        in_specs += [row, pl.BlockSpec((tm, 1), lambda i: (i, 0)), vec]
        args += list(ln)
        out_specs = [row, row, vec, vec]
        out_shape = [jax.ShapeDtypeStruct((t, D_MODEL), F32), jax.ShapeDtypeStruct((t, D_MODEL), BF16),
                     jax.ShapeDtypeStruct((1, D_MODEL), F32), jax.ShapeDtypeStruct((1, D_MODEL), F32)]
    else:
        out_specs = [row]
        out_shape = [jax.ShapeDtypeStruct((t, D_MODEL), out_dtype)]
    res, extra = _call(
        body, grid=(t // tm,), in_specs=in_specs, out_specs=out_specs, out_shape=out_shape, args=tuple(args),
        sem=("arbitrary" if ln is not None else "parallel",), name=name, guest=guest)
    res = res if ln is not None else res[0]
    return res if guest is None else (res, extra)


def _rope_tables(t):
    half = ROPE_DIM // 2
    inv = ROPE_THETA ** (-jnp.arange(0, ROPE_DIM, 2, dtype=F32) / ROPE_DIM)
    ang = jnp.arange(t, dtype=F32)[:, None] * inv[None, :]
    cos, sin = jnp.cos(ang), jnp.sin(ang)
    pad = HEAD_DIM - ROPE_DIM
    c = jnp.concatenate([cos, cos, jnp.ones((t, pad), F32)], axis=1)
    s_hi = jnp.concatenate([jnp.zeros((t, half), F32), sin, jnp.zeros((t, pad), F32)], axis=1)
    s_lo = jnp.concatenate([-sin, jnp.zeros((t, half + pad), F32)], axis=1)
    rep = LANES // HEAD_DIM
    return jnp.tile(c, (1, rep)), jnp.tile(s_hi, (1, rep)), jnp.tile(s_lo, (1, rep))


def _rope(x, c, s_hi, s_lo, sign):
    half = ROPE_DIM // 2
    parts = []
    for j in range(x.shape[1] // LANES):
        xg = x[:, j * LANES:(j + 1) * LANES]
        rot = pltpu.roll(xg, half, 1) * s_hi + pltpu.roll(xg, LANES - half, 1) * s_lo
        parts.append(xg * c + sign * rot)
    return jnp.concatenate(parts, axis=1)


def _qkv_rope(x_bf, w, tables, *, name):
    t = x_bf.shape[0]
    tm = _rows(t)
    n = Q_DIM + 2 * KV_DIM

    def body(x_ref, w_ref, c_ref, sh_ref, sl_ref, q_ref, k_ref, v_ref):
        acc = _dot(x_ref[...], w_ref[...], NN)
        c, sh, sl = c_ref[...], sh_ref[...], sl_ref[...]
        q_ref[...] = (_rope(acc[:, :Q_DIM], c, sh, sl, 1.0) * HEAD_DIM ** -0.5).astype(BF16)
        k_ref[...] = _rope(acc[:, Q_DIM:Q_DIM + KV_DIM], c, sh, sl, 1.0).astype(BF16)
        v_ref[...] = acc[:, Q_DIM + KV_DIM:].astype(BF16)

    tab = pl.BlockSpec((tm, LANES), lambda i: (i, 0))
    return pl.pallas_call(
        body, grid=(t // tm,),
        in_specs=[pl.BlockSpec((tm, D_MODEL), lambda i: (i, 0)), _full((D_MODEL, n)), tab, tab, tab],
        out_specs=[pl.BlockSpec((tm, Q_DIM), lambda i: (i, 0)), pl.BlockSpec((tm, KV_DIM), lambda i: (i, 0)),
                   pl.BlockSpec((tm, KV_DIM), lambda i: (i, 0))],
        out_shape=[jax.ShapeDtypeStruct((t, Q_DIM), BF16), jax.ShapeDtypeStruct((t, KV_DIM), BF16),
                   jax.ShapeDtypeStruct((t, KV_DIM), BF16)],
        compiler_params=_cp("parallel"), name=name)(x_bf, w, *tables)


ATT_ROWS = 256


def _window_specs(t, tq):
    r = tq // WINDOW
    last = t // WINDOW - 1
    return [pl.BlockSpec((WINDOW, KV_DIM), lambda i: (jnp.maximum(i * r - 1, 0), 0)),
            pl.BlockSpec((tq, KV_DIM), lambda i: (i, 0)),
            pl.BlockSpec((WINDOW, KV_DIM), lambda i: (jnp.minimum((i + 1) * r, last), 0))]


def _att_valid(base, t):
    rows = GROUP * WINDOW
    qpos = base + (lax.broadcasted_iota(jnp.int32, (rows, 3 * WINDOW), 0) & (WINDOW - 1))
    kpos = base - WINDOW + lax.broadcasted_iota(jnp.int32, (rows, 3 * WINDOW), 1)
    return (jnp.abs(qpos - kpos) <= WINDOW) & (kpos >= 0) & (kpos < t)


def _stack_heads(x, r0, g):
    return jnp.concatenate(
        [x[r0:r0 + WINDOW, (GROUP * g + h) * HEAD_DIM:(GROUP * g + h + 1) * HEAD_DIM] for h in range(GROUP)], axis=0)


def _sink_column(sink_ref, g):
    return jnp.concatenate([jnp.full((WINDOW, 1), sink_ref[GROUP * g + h], F32) for h in range(GROUP)], axis=0)


def _unstack_heads(pieces):
    return jnp.concatenate([pieces[g][h * WINDOW:(h + 1) * WINDOW] for g in range(N_KV_HEADS) for h in range(GROUP)], axis=1)


def _attn_fwd(q, k, v, sink, *, name, guest=None):
    t = q.shape[0]
    tq = min(ATT_ROWS, t)
    nsb = tq // WINDOW

    def body(sink_ref, q_ref, kp_ref, kc_ref, kn_ref, vp_ref, vc_ref, vn_ref, o_ref, l_ref):
        i = pl.program_id(0)
        qv = q_ref[...]
        kw = jnp.concatenate([kp_ref[...], kc_ref[...], kn_ref[...]], axis=0)
        vw = jnp.concatenate([vp_ref[...], vc_ref[...], vn_ref[...]], axis=0)
        ones = jnp.ones((3 * WINDOW, HEAD_DIM), BF16)
        for sb in range(nsb):
            r0 = sb * WINDOW
            bias =jnp.where(_att_valid(i * tq + r0, t)[:WINDOW], 0.0, -1e30)
            pieces = []
            for hh in range(N_Q_HEADS):
                g, h = hh // GROUP, hh % GROUP
                qh = qv[r0:r0 + WINDOW, hh * HEAD_DIM:(hh + 1) * HEAD_DIM]
                kg = kw[r0:r0 + 3 * WINDOW, g * HEAD_DIM:(g + 1) * HEAD_DIM]
                vg = jnp.concatenate([vw[r0:r0 + 3 * WINDOW, g * HEAD_DIM:(g + 1) * HEAD_DIM], ones], axis=1)
                s = _dot(qh, kg, NT) + bias
                snk = sink_ref[hh]
                m = jnp.maximum(jnp.max(s, axis=1, keepdims=True), snk)
                ov = _dot(jnp.exp(s - m).astype(BF16), vg, NN)
                den = ov[:, HEAD_DIM:HEAD_DIM + 1] + jnp.exp(snk - m)
                pieces.append(ov[:, :HEAD_DIM] * (1.0 / den))
                l_ref[g, sb, h * WINDOW:(h + 1) * WINDOW, :] = m + jnp.log(den)
            o_ref[r0:r0 + WINDOW, :] = jnp.concatenate(pieces, axis=1).astype(BF16)

    return _call(
        body, grid=(t // tq,),
        in_specs=[pl.BlockSpec(memory_space=pltpu.SMEM), pl.BlockSpec((tq, Q_DIM), lambda i: (i, 0))]
        + _window_specs(t, tq) + _window_specs(t, tq),
        out_specs=[pl.BlockSpec((tq, Q_DIM), lambda i: (i, 0)),
                   pl.BlockSpec((N_KV_HEADS, nsb, GROUP * WINDOW, 1), lambda i: (0, i, 0, 0))],
        out_shape=[jax.ShapeDtypeStruct((t, Q_DIM), BF16),
                   jax.ShapeDtypeStruct((N_KV_HEADS, t // WINDOW, GROUP * WINDOW, 1), F32)],
        args=(sink, q, k, k, k, v, v, v), sem=("parallel",), name=name, guest=guest)


def _attn_bwd(q, k, v, o, do, lse, sink, *, name):
    t = q.shape[0]
    tq = min(ATT_ROWS, t)
    nsb = tq // WINDOW

    def body(sink_ref, q_ref, o_ref, do_ref, l_ref, kp_ref, kc_ref, kn_ref, vp_ref, vc_ref, vn_ref,
             dq_ref, dkw_ref, dvw_ref, dsink_ref):
        i = pl.program_id(0)
        qv, ov, dov = q_ref[...], o_ref[...], do_ref[...]
        kw = jnp.concatenate([kp_ref[...], kc_ref[...], kn_ref[...]], axis=0)
        vw = jnp.concatenate([vp_ref[...], vc_ref[...], vn_ref[...]], axis=0)
        lane16 = lax.broadcasted_iota(jnp.int32, (1, N_Q_HEADS), 1)
        dsink = jnp.zeros((1, N_Q_HEADS), F32)
        for sb in range(nsb):
            r0 = sb * WINDOW
            valid = _att_valid(i * tq + r0, t)
            dq_p, dk_p, dv_p = [], [], []
            for g in range(N_KV_HEADS):
                qs, dos = _stack_heads(qv, r0, g), _stack_heads(dov, r0, g)
                delta = jnp.sum(dos.astype(F32) * _stack_heads(ov, r0, g).astype(F32), axis=1, keepdims=True)
                kg = kw[r0:r0 + 3 * WINDOW, g * HEAD_DIM:(g + 1) * HEAD_DIM]
                vg = vw[r0:r0 + 3 * WINDOW, g * HEAD_DIM:(g + 1) * HEAD_DIM]
                lse_col = l_ref[g, sb]
                pr = jnp.where(valid, jnp.exp(_dot(qs, kg, NT) - lse_col), 0.0)
                ds = (pr * (_dot(dos, vg, NT) - delta)).astype(BF16)
                dq_p.append(_dot(ds, kg, NN))
                dk_p.append(_dot(ds, qs, TN))
                dv_p.append(_dot(pr.astype(BF16), dos, TN))
                ps = jnp.exp(_sink_column(sink_ref, g) - lse_col) * delta
                for h in range(GROUP):
                    tot = jnp.sum(ps[h * WINDOW:(h + 1) * WINDOW], axis=0, keepdims=True)
                    dsink = dsink - jnp.where(lane16 == GROUP * g + h, tot, 0.0)
            dq_ref[r0:r0 + WINDOW, :] = _unstack_heads(dq_p)
            dkw_ref[sb] = jnp.concatenate(dk_p, axis=1)
            dvw_ref[sb] = jnp.concatenate(dv_p, axis=1)

        @pl.when(i == 0)
        def _():
            dsink_ref[...] = jnp.zeros_like(dsink_ref)
        dsink_ref[...] += dsink

    rowq = pl.BlockSpec((tq, Q_DIM), lambda i: (i, 0))
    win = pl.BlockSpec((nsb, 3 * WINDOW, KV_DIM), lambda i: (i, 0, 0))
    wshape = jax.ShapeDtypeStruct((t // WINDOW, 3 * WINDOW, KV_DIM), F32)
    return pl.pallas_call(
        body, grid=(t // tq,),
        in_specs=[pl.BlockSpec(memory_space=pltpu.SMEM), rowq, rowq, rowq,
                  pl.BlockSpec((N_KV_HEADS, nsb, GROUP * WINDOW, 1), lambda i: (0, i, 0, 0))]
        + _window_specs(t, tq) + _window_specs(t, tq),
        out_specs=[rowq, win, win, _full((1, N_Q_HEADS))],
        out_shape=[jax.ShapeDtypeStruct((t, Q_DIM), F32), wshape, wshape, jax.ShapeDtypeStruct((1, N_Q_HEADS), F32)],
        compiler_params=_cp("arbitrary"), name=name)(sink, q, o, do, lse, k, k, k, v, v, v)


def _attn_post_bwd(dq, dkw, dvw, tables, *, name):
    t = dq.shape[0]
    nb = t // WINDOW
    n = Q_DIM + 2 * KV_DIM

    def body(dq_ref, ka_ref, kb_ref, kc_ref, va_ref, vb_ref, vc_ref, c_ref, sh_ref, sl_ref, o_ref):
        j = pl.program_id(0)
        lo = (j > 0).astype(F32)
        hi = (j < nb - 1).astype(F32)
        c, sh, sl = c_ref[...], sh_ref[...], sl_ref[...]
        dk = ka_ref[...] * hi + kb_ref[...] + kc_ref[...] * lo
        dv = va_ref[...] * hi + vb_ref[...] + vc_ref[...] * lo
        o_ref[:, :Q_DIM] = (_rope(dq_ref[...], c, sh, sl, -1.0) * HEAD_DIM ** -0.5).astype(BF16)
        o_ref[:, Q_DIM:Q_DIM + KV_DIM] = _rope(dk, c, sh, sl, -1.0).astype(BF16)
        o_ref[:, Q_DIM + KV_DIM:] = dv.astype(BF16)

    wins = [pl.BlockSpec((None, WINDOW, KV_DIM), lambda j: (jnp.minimum(j + 1, nb - 1), 0, 0)),
            pl.BlockSpec((None, WINDOW, KV_DIM), lambda j: (j, 1, 0)),
            pl.BlockSpec((None, WINDOW, KV_DIM), lambda j: (jnp.maximum(j - 1, 0), 2, 0))]
    tab = pl.BlockSpec((WINDOW, LANES), lambda j: (j, 0))
    return pl.pallas_call(
        body, grid=(nb,),
        in_specs=[pl.BlockSpec((WINDOW, Q_DIM), lambda j: (j, 0))] + wins + wins + [tab, tab, tab],
        out_specs=pl.BlockSpec((WINDOW, n), lambda j: (j, 0)),
        out_shape=jax.ShapeDtypeStruct((t, n), BF16),
        compiler_params=_cp("parallel"), name=name)(dq, dkw, dkw, dkw, dvw, dvw, dvw, *tables)


HGRN_ROWS = 512
HGRN_UNROLL = 2
HGRN_UNROLL_FWD = 4


def _cum_mask(rev):
    row = lax.broadcasted_iota(jnp.int32, (HGRN_CHUNK, HGRN_CHUNK), 0)
    col = lax.broadcasted_iota(jnp.int32, (HGRN_CHUNK, HGRN_CHUNK), 1)
    return (row <= col) if rev else (row >= col)


def _gates(zq, zf, lb):
    q = zq * _sigmoid(zq)
    sig = _sigmoid(zf)
    f = lb + (1.0 - lb) * sig
    k = (1.0 - lb) * (1.0 - sig)
    return q, sig, f, k


def _intra_exact(q, k, b, mask):
    rows = lax.broadcasted_iota(jnp.int32, (HGRN_CHUNK, 1), 0)
    cols = lax.broadcasted_iota(jnp.int32, (HGRN_CHUNK, HGRN_CHUNK), 1)

    def it(s, a):
        pick = rows == s
        b_s = jnp.sum(jnp.where(pick, b, 0.0), axis=0, keepdims=True)
        k_s = jnp.sum(jnp.where(pick, k, 0.0), axis=0, keepdims=True)
        col = jnp.sum(q * jnp.exp(jnp.minimum(b - b_s, 0.0)) * k_s, axis=1, keepdims=True)
        return jnp.where(cols == s, col, a)

    a = lax.fori_loop(0, HGRN_CHUNK, it, jnp.zeros((HGRN_CHUNK, HGRN_CHUNK), F32))
    return jnp.where(mask, a, 0.0)


def _intra_exact_bwd(q, k, b, da):
    rows = lax.broadcasted_iota(jnp.int32, (HGRN_CHUNK, 1), 0)
    cols = lax.broadcasted_iota(jnp.int32, (HGRN_CHUNK, HGRN_CHUNK), 1)

    def it(s, carry):
        dq, dk = carry
        pick = rows == s
        b_s = jnp.sum(jnp.where(pick, b, 0.0), axis=0, keepdims=True)
        k_s = jnp.sum(jnp.where(pick, k, 0.0), axis=0, keepdims=True)
        w = jnp.sum(jnp.where(cols == s, da, 0.0), axis=1, keepdims=True) * jnp.exp(jnp.minimum(b - b_s, 0.0))
        dq = dq + w * k_s
        dk = jnp.where(pick, jnp.sum(w * q, axis=0, keepdims=True), dk)
        return dq, dk

    z = jnp.zeros((HGRN_CHUNK, HGRN_KEY), F32)
    return lax.fori_loop(0, HGRN_CHUNK, it, (z, z))


def _chunk_cumsum(g, from_end):
    n = g.shape[0]
    row = lax.broadcasted_iota(jnp.int32, g.shape, 0)
    x, s = g, 1
    while s < n:
        if from_end:
            x = x + jnp.where(row < n - s, pltpu.roll(x, n - s, 0), 0.0)
        else:
            x = x + jnp.where(row >= s, pltpu.roll(x, s, 0), 0.0)
        s *= 2
    return x


def _head(a, h):
    return a[:, h * LANES:(h + 1) * LANES]


def _block_is_mild(zf_ref, lb, nch):
    def lowest_total():
        g = jnp.log(lb + (1.0 - lb) * _sigmoid(zf_ref[...]))
        lows = [jnp.min(jnp.sum(g[c * HGRN_CHUNK:(c + 1) * HGRN_CHUNK], axis=0, keepdims=True)) for c in range(nch)]
        return functools.reduce(jnp.minimum, lows)

    floor = jnp.min(HGRN_CHUNK * jnp.log(lb))
    return lax.cond(floor >= FACTORED_DECAY_LIMIT, lambda: floor, lowest_total) >= FACTORED_DECAY_LIMIT


def _hgrn_fwd(z, lb, rev, *, name, guest=None):
    t = z.shape[0]
    rb = min(HGRN_ROWS, t)
    nb, nch = t // rb, rb // HGRN_CHUNK
    heads = range(HGRN_HEADS)

    def blk(n):
        return nb - 1 - n if rev else n

    def body(zq_ref, zf_ref, zv_ref, lb_ref, o_ref, s_ref, st_ref):
        @pl.when(pl.program_id(0) == 0)
        def _():
            st_ref[...] = jnp.zeros_like(st_ref)
        lbv = lb_ref[...]
        cmask = _cum_mask(rev)

        def chunk(c, carry, mild):
            cc = nch - 1 - c if rev else c
            sl = pl.ds(pl.multiple_of(cc * HGRN_CHUNK, HGRN_CHUNK), HGRN_CHUNK)
            q, _, f, k = _gates(zq_ref[sl, :], zf_ref[sl, :], lbv)
            v = zv_ref[sl, :].astype(BF16)
            g = jnp.log(f)
            b = _chunk_cumsum(g, rev)
            tot = jnp.sum(g, axis=0, keepdims=True)
            qd = (q * jnp.exp(b)).astype(BF16)
            kd = (k * jnp.exp(tot - b)).astype(BF16)
            etot = jnp.exp(tot)

            def factored():
                kf = (k * jnp.exp(-b)).astype(BF16)
                return tuple(jnp.where(cmask, _dot(_head(qd, h), _head(kf, h), NT), 0.0) for h in heads)

            def exact():
                return tuple(_intra_exact(_head(q, h), _head(k, h), _head(b, h), cmask) for h in heads)

            a = factored() if mild else lax.cond(jnp.min(tot) >= FACTORED_DECAY_LIMIT, factored, exact)
            for h in heads:
                st = st_ref[h]
                st_bf = st.astype(BF16)
                s_ref[h, cc] = st_bf
                o_ref[sl, h * LANES:(h + 1) * LANES] = (
                    _dot(_head(qd, h), st_bf, NT) + _dot(a[h].astype(BF16), _head(v, h), NN))
                st_ref[h] = st * _head(etot, h) + _dot(_head(v, h), _head(kd, h), TN)
            return carry

        lax.cond(_block_is_mild(zf_ref, lbv, nch),
                 lambda: lax.fori_loop(0, nch, functools.partial(chunk, mild=True), 0, unroll=HGRN_UNROLL_FWD),
                 lambda: lax.fori_loop(0, nch, functools.partial(chunk, mild=False), 0))

    def col(j):
        return pl.BlockSpec((rb, D_MODEL), lambda n: (blk(n), j))

    return _call(
        body, grid=(nb,),
        in_specs=[col(0), col(2 if rev else 1), col(3), _full((1, D_MODEL))],
        out_specs=[col(0), pl.BlockSpec((HGRN_HEADS, nch, LANES, LANES), lambda n: (0, blk(n), 0, 0))],
        out_shape=[jax.ShapeDtypeStruct((t, D_MODEL), F32),
                   jax.ShapeDtypeStruct((HGRN_HEADS, t // HGRN_CHUNK, LANES, LANES), BF16)],
        scratch_shapes=[pltpu.VMEM((HGRN_HEADS, LANES, LANES), F32)],
        args=(z, z, z, lb), sem=("arbitrary",), name=name, guest=guest)


def _hgrn_bwd(z, lb, d_o, states, rev, *, name, other=None):
    t = z.shape[0]
    rb = min(HGRN_ROWS, t)
    nb, nch = t // rb, rb // HGRN_CHUNK
    heads = range(HGRN_HEADS)
    n_other = 0 if other is None else 2
    acc_dtype = F32 if other is None else BF16

    def blk(n):
        return n if rev else nb - 1 - n

    def body(zq_ref, zf_ref, zv_ref, lb_ref, do_ref, s_ref, *rest):
        oq_ref, ov_ref = rest[:n_other] if other is not None else (None, None)
        dq_ref, dzf_ref, dv_ref, dlb_ref, dst_ref = rest[n_other:]

        @pl.when(pl.program_id(0) == 0)
        def _():
            dst_ref[...] = jnp.zeros_like(dst_ref)
            dlb_ref[...] = jnp.zeros_like(dlb_ref)
        lbv = lb_ref[...]
        cmask = _cum_mask(rev)

        def chunk(c, carry, mild):
            cc = c if rev else nch - 1 - c
            sl = pl.ds(pl.multiple_of(cc * HGRN_CHUNK, HGRN_CHUNK), HGRN_CHUNK)
            zq = zq_ref[sl, :]
            q, sig, f, k = _gates(zq, zf_ref[sl, :], lbv)
            v = zv_ref[sl, :].astype(BF16)
            g = jnp.log(f)
            b = _chunk_cumsum(g, rev)
            tot = jnp.sum(g, axis=0, keepdims=True)
            eb = jnp.exp(b)
            etb = jnp.exp(tot - b)
            etot = jnp.exp(tot)
            qd = (q * eb).astype(BF16)
            kd = (k * etb).astype(BF16)
            do = do_ref[sl, :].astype(BF16)
            da = [jnp.where(cmask, _dot(_head(do, h), _head(v, h), NT), 0.0) for h in heads]

            def factored():
                ke = jnp.exp(-b)
                kf = (k * ke).astype(BF16)
                res = []
                for h in heads:
                    qh, kh = _head(qd, h), _head(kf, h)
                    da_bf = da[h].astype(BF16)
                    dqf, dkf = _dot(da_bf, kh, NN), _dot(da_bf, qh, TN)
                    res.append((jnp.where(cmask, _dot(qh, kh, NT), 0.0), dqf * _head(eb, h), dkf * _head(ke, h),
                                qh.astype(F32) * dqf - kh.astype(F32) * dkf))
                return tuple(res)

            def exact():
                res = []
                for h in heads:
                    qh, kh, bh = _head(q, h), _head(k, h), _head(b, h)
                    dq_i, dk_i = _intra_exact_bwd(qh, kh, bh, da[h])
                    res.append((_intra_exact(qh, kh, bh, cmask), dq_i, dk_i, qh * dq_i - kh * dk_i))
                return tuple(res)

            intra = factored() if mild else lax.cond(jnp.min(tot) >= FACTORED_DECAY_LIMIT, factored, exact)
            dq_p, dk_p, db_p, dtot_p = [], [], [], []
            for h in heads:
                a, dq_i, dk_i, db_i = intra[h]
                doh, vh, qh, kh = _head(do, h), _head(v, h), _head(q, h), _head(k, h)
                st0 = s_ref[h, cc]
                dst = dst_ref[h]
                dst_bf = dst.astype(BF16)
                dv = _dot(a.astype(BF16), doh, TN) + _dot(_head(kd, h), dst_bf, NT)
                if ov_ref is not None:
                    dv = dv + ov_ref[sl, h * LANES:(h + 1) * LANES]
                dv_ref[sl, h * LANES:(h + 1) * LANES] = dv.astype(acc_dtype)
                dq_x = _dot(doh, st0, NN) * _head(eb, h)
                dk_x = _dot(vh, dst_bf, NN) * _head(etb, h)
                dtot_p.append(jnp.sum(kh * dk_x, axis=0, keepdims=True)
                              + _head(etot, h) * jnp.sum(dst * st0.astype(F32), axis=0, keepdims=True))
                dst_ref[h] = dst * _head(etot, h) + _dot(doh, _head(qd, h), TN)
                dq_p.append(dq_i + dq_x)
                dk_p.append(dk_i + dk_x)
                db_p.append(db_i + qh * dq_x - kh * dk_x)
            dq, dk, db = (jnp.concatenate(x, axis=1) for x in (dq_p, dk_p, db_p))
            dg = _chunk_cumsum(db, not rev) + jnp.concatenate(dtot_p, axis=1)
            sq = _sigmoid(zq)
            dq_pre = dq * sq * (1.0 + zq * (1.0 - sq))
            if oq_ref is not None:
                dq_pre = dq_pre + oq_ref[sl, :]
            dq_ref[sl, :] = dq_pre.astype(acc_dtype)
            dfk = dg / f - dk
            dzf_ref[sl, :] = (dfk * (1.0 - lbv) * sig * (1.0 - sig)).astype(BF16)
            dlb_ref[...] += jnp.sum(dfk * (1.0 - sig), axis=0, keepdims=True)
            return carry

        lax.cond(_block_is_mild(zf_ref, lbv, nch),
                 lambda: lax.fori_loop(0, nch, functools.partial(chunk, mild=True), 0, unroll=HGRN_UNROLL),
                 lambda: lax.fori_loop(0, nch, functools.partial(chunk, mild=False), 0))

    def col(j):
        return pl.BlockSpec((rb, D_MODEL), lambda n: (blk(n), j))

    return pl.pallas_call(
        body, grid=(nb,),
        in_specs=[col(0), col(2 if rev else 1), col(3), _full((1, D_MODEL)), col(0),
                  pl.BlockSpec((HGRN_HEADS, nch, LANES, LANES), lambda n: (0, blk(n), 0, 0))] + [col(0)] * n_other,
        out_specs=[col(0), col(0), col(0), _full((1, D_MODEL))],
        out_shape=[jax.ShapeDtypeStruct((t, D_MODEL), acc_dtype), jax.ShapeDtypeStruct((t, D_MODEL), BF16),
                   jax.ShapeDtypeStruct((t, D_MODEL), acc_dtype), jax.ShapeDtypeStruct((1, D_MODEL), F32)],
        scratch_shapes=[pltpu.VMEM((HGRN_HEADS, LANES, LANES), F32)],
        compiler_params=_cp("arbitrary"), name=name)(z, z, z, lb, d_o, states, *(other or ()))


def _hgrn_post_fwd(o_f, o_b, z, norm_g, *, name):
    t = o_f.shape[0]
    tm = _rows(t)

    def body(of_ref, ob_ref, gate_ref, ng_ref, y_ref):
        ng = ng_ref[...]
        for h in range(HGRN_HEADS):
            sl = slice(h * LANES, (h + 1) * LANES)
            o = of_ref[:, sl] + ob_ref[:, sl]
            r = lax.rsqrt(jnp.mean(o * o, axis=1, keepdims=True) + LN_EPS)
            gt = gate_ref[:, sl]
            y_ref[:, sl] = (o * r * ng * (gt * _sigmoid(gt))).astype(BF16)

    row = pl.BlockSpec((tm, D_MODEL), lambda i: (i, 0))
    return pl.pallas_call(
        body, grid=(t // tm,),
        in_specs=[row, row, pl.BlockSpec((tm, D_MODEL), lambda i: (i, 4)), _full((1, LANES))],
        out_specs=row, out_shape=jax.ShapeDtypeStruct((t, D_MODEL), BF16),
        compiler_params=_cp("parallel"), name=name)(o_f, o_b, z, norm_g)


def _hgrn_post_bwd(dy, o_f, o_b, z, norm_g, *, name):
    t = o_f.shape[0]
    tm = _rows(t)

    def body(dy_ref, of_ref, ob_ref, gate_ref, ng_ref, do_ref, dgate_ref, dng_ref):
        ng = ng_ref[...]
        dng = jnp.zeros((1, LANES), F32)
        for h in range(HGRN_HEADS):
            sl = slice(h * LANES, (h + 1) * LANES)
            o = of_ref[:, sl] + ob_ref[:, sl]
            r = lax.rsqrt(jnp.mean(o * o, axis=1, keepdims=True) + LN_EPS)
            gt = gate_ref[:, sl]
            sg = _sigmoid(gt)
            dyv = dy_ref[:, sl].astype(F32)
            xn = o * r
            dgate_ref[:, sl] = (dyv * xn * ng * sg * (1.0 + gt * (1.0 - sg))).astype(BF16)
            dn = dyv * gt * sg
            dng = dng + jnp.sum(dn * xn, axis=0, keepdims=True)
            dxn = dn * ng
            do_ref[:, sl] = r * (dxn - xn * jnp.mean(dxn * xn, axis=1, keepdims=True))

        @pl.when(pl.program_id(0) == 0)
        def _():
            dng_ref[...] = jnp.zeros_like(dng_ref)
        dng_ref[...] += dng

    row = pl.BlockSpec((tm, D_MODEL), lambda i: (i, 0))
    return pl.pallas_call(
        body, grid=(t // tm,),
        in_specs=[row, row, row, pl.BlockSpec((tm, D_MODEL), lambda i: (i, 4)), _full((1, LANES))],
        out_specs=[row, row, _full((1, LANES))],
        out_shape=[jax.ShapeDtypeStruct((t, D_MODEL), F32), jax.ShapeDtypeStruct((t, D_MODEL), BF16),
                   jax.ShapeDtypeStruct((1, LANES), F32)],
        compiler_params=_cp("arbitrary"), name=name)(dy, o_f, o_b, z, norm_g)


def _lower_bounds(logits2d, *, name):
    def body(l_ref, lb_ref):
        l = l_ref[...]
        e = jnp.exp(l - jnp.max(l, axis=0, keepdims=True))
        sm = e / jnp.sum(e, axis=0, keepdims=True)
        s1, s2, s3 = sm[1:2], sm[2:3], sm[3:4]
        lb_ref[...] = jnp.concatenate([jnp.zeros_like(s1), s1, s1 + s2, s1 + s2 + s3], axis=0)

    return pl.pallas_call(body, out_shape=jax.ShapeDtypeStruct(logits2d.shape, F32), name=name)(logits2d)


def _lower_bounds_bwd(logits2d, dlb, *, name):
    def body(l_ref, d_ref, o_ref):
        l = l_ref[...]
        e = jnp.exp(l - jnp.max(l, axis=0, keepdims=True))
        sm = e / jnp.sum(e, axis=0, keepdims=True)
        d = d_ref[...]
        d1, d2, d3 = d[1:2], d[2:3], d[3:4]
        dsm = jnp.concatenate([jnp.zeros_like(d1), d1 + d2 + d3, d2 + d3, d3], axis=0)
        o_ref[...] = sm * (dsm - jnp.sum(sm * dsm, axis=0, keepdims=True))

    return pl.pallas_call(body, out_shape=jax.ShapeDtypeStruct(logits2d.shape, F32), name=name)(logits2d, dlb)


def _adamw(w, g, m, v, *, name):
    r, c = w.shape
    tr = r if r <= 512 else _pick_rows(r)

    def body(w_ref, g_ref, m_ref, v_ref, d_ref, nm_ref, nv_ref):
        gv = g_ref[...]
        nm = ADAM_B1 * m_ref[...] + (1.0 - ADAM_B1) * gv
        nv = ADAM_B2 * v_ref[...] + (1.0 - ADAM_B2) * (gv * gv)
        m_hat = nm / (1.0 - ADAM_B1 ** ADAM_STEP)
        v_hat = nv / (1.0 - ADAM_B2 ** ADAM_STEP)
        d_ref[...] = -ADAM_LR * (m_hat / (jnp.sqrt(v_hat) + ADAM_EPS) + ADAM_WD * w_ref[...])
        nm_ref[...] = nm
        nv_ref[...] = nv

    blk = pl.BlockSpec((tr, c), lambda i: (i, 0))
    shp = jax.ShapeDtypeStruct((r, c), F32)
    return pl.pallas_call(body, grid=(r // tr,), in_specs=[blk] * 4, out_specs=[blk] * 3, out_shape=[shp] * 3,
                          compiler_params=_cp("parallel"), name=name)(w, g, m, v)


def _pick_rows(r, cap=512):
    best = 8
    for d in range(8, cap + 1, 8):
        if r % d == 0:
            best = d
    assert r % best == 0, r
    return best


def _place():
    x, y, c = lax.axis_index("x"), lax.axis_index("y"), lax.axis_index("c")
    chips = [(1 - x, y), (x, 1 - y), (1 - x, 1 - y)]
    return x, y, c, chips


def _remote(src, dst, send_sem, recv_sem, to):
    return pltpu.make_async_remote_copy(src_ref=src, dst_ref=dst, send_sem=send_sem, recv_sem=recv_sem,
                                        device_id=to, device_id_type=MESH)


def _allreduce_small(block, *, name):
    m, n = block.shape

    def body(x_ref, sum_ref, all_ref, send_sems, recv_sems):
        x, y, c, chips = _place()
        me, sibling = (x, y, c), (x, y, 1 - c)

        def rows(px, py, pc):
            return all_ref.at[pl.ds((4 * px + 2 * py + pc) * m, m), :]

        all_ref[pl.ds((4 * x + 2 * y + c) * m, m), :] = x_ref[...]
        first = [_remote(x_ref, rows(*me), send_sems.at[0], recv_sems.at[0], sibling)]
        first += [_remote(x_ref, rows(*me), send_sems.at[1 + j], recv_sems.at[1 + j], (*chip, c)) for j, chip in enumerate(chips)]
        for cp in first:
            cp.start()
        passed = [_remote(rows(*chip, c), rows(*chip, c), send_sems.at[4 + j], recv_sems.at[4 + j], sibling)
                  for j, chip in enumerate(chips)]
        for j, chip in enumerate(chips):
            _remote(x_ref, rows(*chip, c), send_sems.at[1 + j], recv_sems.at[1 + j], me).wait_recv()
            passed[j].start()
        _remote(x_ref, rows(*sibling), send_sems.at[0], recv_sems.at[0], me).wait_recv()
        for j, chip in enumerate(chips):
            _remote(x_ref, rows(*chip, 1 - c), send_sems.at[4 + j], recv_sems.at[4 + j], me).wait_recv()
        for cp in first + passed:
            cp.wait_send()
        acc = all_ref[pl.ds(0, m), :]
        for d in range(1, 8):
            acc = acc + all_ref[pl.ds(d * m, m), :]
        sum_ref[...] = acc

    vmem = pl.BlockSpec(memory_space=pltpu.VMEM)
    return pl.pallas_call(
        body, in_specs=[vmem], out_specs=vmem, out_shape=jax.ShapeDtypeStruct((m, n), F32),
        scratch_shapes=[pltpu.VMEM((8 * m, n), F32), pltpu.SemaphoreType.DMA((7,)), pltpu.SemaphoreType.DMA((7,))],
        name=name)(block)


def _add_own_half(g, recv, c, *, name):
    n, r, w = g.shape
    hr = r // 2
    tr = _pick_rows(hr, 1024)
    nr = hr // tr

    def body(c_ref, g_ref, r_ref, o_ref):
        o_ref[...] = (g_ref[...] + r_ref[...]).astype(BF16)

    return pl.pallas_call(
        body,
        grid_spec=pltpu.PrefetchScalarGridSpec(
            num_scalar_prefetch=1, grid=(n, nr),
            in_specs=[pl.BlockSpec((None, tr, w), lambda s, i, c_ref: (s, c_ref[0] * nr + i, 0)),
                      pl.BlockSpec((None, tr, w), lambda s, i, c_ref: (s, i, 0))],
            out_specs=pl.BlockSpec((None, tr, w), lambda s, i, c_ref: (s, i, 0))),
        out_shape=jax.ShapeDtypeStruct((n, hr, w), BF16),
        compiler_params=_cp("parallel", "parallel"), name=name)(c, g, recv)


def _sum_chips(q, *, name):
    n, h, w = q.shape
    tr = _pick_rows(h, 1024)

    def body(a, b, c, d, o_ref):
        o_ref[...] = ((a[...].astype(F32) + b[...].astype(F32)) + c[...].astype(F32)) + d[...].astype(F32)

    specs = [pl.BlockSpec((None, tr, w), functools.partial(lambda i, s: (s, i, 0), s=s)) for s in range(n)]
    return pl.pallas_call(
        body, grid=(h // tr,), in_specs=specs, out_specs=pl.BlockSpec((tr, w), lambda i: (i, 0)),
        out_shape=jax.ShapeDtypeStruct((h, w), F32), compiler_params=_cp("parallel"), name=name)(q, q, q, q)


PACK_W = 1024
BIG = (("att_w_qkv", 2), ("att_w_o", 1), ("hgrn_w_in", 2), ("hgrn_w_o", 1),
       ("ffn_w_in", 2), ("ffn_w_out", 1), ("ple_w_gate", 1), ("ple_w_proj", 2))
LB_ROWS = 32


SMALL =("ln_mix_g", "ln_mix_b", "ln_ffn_g", "ln_ffn_b")
SMALL_ROWS = 32


def _pack_small(vals, lb):
    rows = [vals[n] for n in SMALL] + [lb.reshape(-1, PACK_W)]
    for n in ("att_sink", "hgrn_norm_g"):
        flat = vals[n].reshape(1, -1)
        rows.append(jnp.pad(flat, ((0, 0), (0, PACK_W - flat.shape[1]))))
    buf = jnp.concatenate(rows, axis=0)
    return jnp.pad(buf, ((0, SMALL_ROWS - buf.shape[0]), (0, 0)))


def _unpack_small(buf, lb_shape):
    out, r0 = {}, 0
    for n in SMALL:
        out[n] = buf[r0:r0 + DEPTH]
        r0 += DEPTH
    nlb = lb_shape[0] * lb_shape[1] * lb_shape[2] // PACK_W
    out["hgrn_lb_logits"] = buf[r0:r0 + nlb].reshape(lb_shape)
    r0 += nlb
    out["att_sink"] = buf[r0, :2 * N_Q_HEADS].reshape(2, N_Q_HEADS)
    out["hgrn_norm_g"] = buf[r0 + 1, :2 * LANES].reshape(2, LANES)
    return out


WEIGHTS = ("att_w_qkv", "att_sink", "att_w_o", "hgrn_w_in", "hgrn_lb_logits", "hgrn_norm_g", "hgrn_w_o", "ln_mix_g",
           "ln_mix_b", "ffn_w_in", "ffn_w_out", "ln_ffn_g", "ln_ffn_b", "ple_w_gate", "ple_w_proj")


def _gather_guest(packed):
    r, w = packed.shape
    hr = r // 2

    def copies(src_ref, out_ref, send_sems, recv_sems):
        x, y, c, chips = _place()
        sibling = (x, y, 1 - c)

        def half(cx, cy, hc):
            return out_ref.at[2 * cx + cy, pl.ds(hc * hr, hr), :]

        my_half = src_ref.at[pl.ds(c * hr, hr), :]
        first = [_remote(my_half, half(x, y, c), send_sems.at[j], recv_sems.at[j], (*chip, c)) for j, chip in enumerate(chips)]
        passed = [_remote(half(*chip, c), half(*chip, c), send_sems.at[3 + j], recv_sems.at[3 + j], sibling)
                  for j, chip in enumerate(chips)]
        from_chips = [_remote(my_half, half(*chip, c), send_sems.at[j], recv_sems.at[j], (*chip, c)) for j, chip in enumerate(chips)]
        from_sibling = [_remote(my_half, half(*chip, 1 - c), send_sems.at[3 + j], recv_sems.at[3 + j], sibling)
                        for j, chip in enumerate(chips)]
        return first, passed, from_chips, from_sibling

    def start(gi, go, gs):
        for cp in copies(gi[0], go[0], *gs)[0]:
            cp.start()

    def finish(gi, go, gs):
        first, passed, from_chips, from_sibling = copies(gi[0], go[0], *gs)
        for arrival, onward in zip(from_chips, passed):
            arrival.wait_recv()
            onward.start()
        for arrival in from_sibling:
            arrival.wait_recv()
        for cp in first + passed:
            cp.wait_send()

    return _Guest((packed,), (jax.ShapeDtypeStruct((N_CHIPS, r, w), packed.dtype),),
                  (pltpu.SemaphoreType.DMA((6,)), pltpu.SemaphoreType.DMA((6,))), start, finish)


def _pair_exchange_guest(g):
    n, r, w = g.shape
    hr = r // 2

    def copy(g_ref, out_ref, send_sem, recv_sem):
        x, y, c, _ = _place()
        return _remote(g_ref.at[:, pl.ds((1 - c) * hr, hr), :], out_ref, send_sem, recv_sem, (x, y, 1 - c))

    return _Guest((g,), (jax.ShapeDtypeStruct((n, hr, w), g.dtype),), (pltpu.SemaphoreType.DMA, pltpu.SemaphoreType.DMA),
                  lambda gi, go, gs: copy(gi[0], go[0], *gs).start(),
                  lambda gi, go, gs: copy(gi[0], go[0], *gs).wait())


def _chip_scatter_guest(part):
    n, h, w = part.shape

    def copies(p_ref, out_ref, send_sems, recv_sems):
        x, y, c, chips = _place()
        me = 2 * x + y
        sends = [_remote(p_ref.at[2 * cx + cy], out_ref.at[me], send_sems.at[j], recv_sems.at[j], (cx, cy, c))
                 for j, (cx, cy) in enumerate(chips)]
        arrivals = [_remote(p_ref.at[me], out_ref.at[2 * cx + cy], send_sems.at[j], recv_sems.at[j], (cx, cy, c))
                    for j, (cx, cy) in enumerate(chips)]
        return sends, arrivals

    def start(gi, go, gs):
        for cp in copies(gi[0], go[0], *gs)[0]:
            cp.start()

    def finish(gi, go, gs):
        sends, arrivals = copies(gi[0], go[0], *gs)
        for cp in arrivals:
            cp.wait_recv()
        for cp in sends:
            cp.wait_send()

    return _Guest((part,), (jax.ShapeDtypeStruct((n, h, w), part.dtype),),
                  (pltpu.SemaphoreType.DMA((3,)), pltpu.SemaphoreType.DMA((3,))), start, finish)


def _pair_share_guest(halves):
    n = len(halves)

    def copies(k, h_ref, out_ref, send_sems, recv_sems):
        x, y, c, _ = _place()
        send = _remote(h_ref, out_ref.at[c], send_sems.at[k], recv_sems.at[k], (x, y, 1 - c))
        arrival = _remote(h_ref, out_ref.at[1 - c], send_sems.at[k], recv_sems.at[k], (x, y, 1 - c))
        return send, arrival

    def start(gi, go, gs):
        for k in range(n):
            copies(k, gi[k], go[k], *gs)[0].start()

    def finish(gi, go, gs):
        for k in range(n):
            send, arrival = copies(k, gi[k], go[k], *gs)
            send.wait_send()
            arrival.wait_recv()

    return _Guest(tuple(halves), tuple(jax.ShapeDtypeStruct((2,) + a.shape, a.dtype) for a in halves),
                  (pltpu.SemaphoreType.DMA((n,)), pltpu.SemaphoreType.DMA((n,))), start, finish)


def _own(stack, idx):
    return lax.dynamic_index_in_dim(stack, idx, axis=0, keepdims=False)


def _put(buf, block, idx):
    return lax.dynamic_update_slice_in_dim(buf, block[None], idx, axis=0)


AXIS = dict(BIG)
SHARD_MAJOR = ("ffn_w_in",)


def _layer_parts(i):
    j = i // 2
    mixer = (("att_w_qkv", j), ("att_w_o", j)) if i % 2 == 0 else (("hgrn_w_in", j), ("hgrn_w_o", j))
    return mixer + (("ffn_w_in", i), ("ffn_w_out", i), ("ple_w_gate", i), ("ple_w_proj", i))


def _gather_groups():
    first = _layer_parts(0)
    return [first[:1], first[1:] + _layer_parts(1), _layer_parts(2), _layer_parts(3)]


def _pack_parts(shards, parts):
    return jnp.concatenate([shards[n][l].reshape(-1, PACK_W) for n, l in parts], axis=0)


def _gathered_parts(buf, parts, shapes):
    out, r0 = {}, 0
    for n, l in parts:
        r, c = shapes[n][1:]
        nr = r * c // PACK_W
        sh = buf[:, r0:r0 + nr].reshape(N_CHIPS, r, c)
        if n in SHARD_MAJOR:
            out[(n, l)] = sh
        else:
            out[(n, l)] = sh.reshape(N_CHIPS * r, c) if AXIS[n] == 1 else sh.transpose(1, 0, 2).reshape(r, N_CHIPS * c)
        r0 += nr
    return out, r0


def _unpack_layer(buf, i, shapes):
    out, r0 = {}, 0
    for n, _ in _layer_parts(i):
        r, c = shapes[n][1:]
        nr = r * c // PACK_W
        out[n] = buf[r0:r0 + nr].reshape(r, c)
        r0 += nr
    return out


def _layer_slabs(full, i, shapes):
    parts = []
    for n, _ in _layer_parts(i):
        r, c = shapes[n][1:]
        g = full[n]
        if AXIS[n] == 1 or n in SHARD_MAJOR:
            g = g.reshape(N_CHIPS, -1, PACK_W)
        else:
            g = g.reshape(r, N_CHIPS, c).transpose(1, 0, 2).reshape(N_CHIPS, -1, PACK_W)
        parts.append(g)
    return jnp.concatenate(parts, axis=1)


def kernel(x, p, att_w_qkv, att_sink, att_w_o, hgrn_w_in, hgrn_lb_logits, hgrn_norm_g, hgrn_w_o, ln_mix_g, ln_mix_b, ffn_w_in, ffn_w_out, ln_ffn_g, ln_ffn_b, ple_w_gate, ple_w_proj, loss_target, m_att_w_qkv, m_att_sink, m_att_w_o, m_hgrn_w_in, m_hgrn_lb_logits, m_hgrn_norm_g, m_hgrn_w_o, m_ln_mix_g, m_ln_mix_b, m_ffn_w_in, m_ffn_w_out, m_ln_ffn_g, m_ln_ffn_b, m_ple_w_gate, m_ple_w_proj, v_att_w_qkv, v_att_sink, v_att_w_o, v_hgrn_w_in, v_hgrn_lb_logits, v_hgrn_norm_g, v_hgrn_w_o, v_ln_mix_g, v_ln_mix_b, v_ffn_w_in, v_ffn_w_out, v_ln_ffn_g, v_ln_ffn_b, v_ple_w_gate, v_ple_w_proj):
    wts = dict(att_w_qkv=att_w_qkv, att_sink=att_sink, att_w_o=att_w_o, hgrn_w_in=hgrn_w_in, hgrn_lb_logits=hgrn_lb_logits,
               hgrn_norm_g=hgrn_norm_g, hgrn_w_o=hgrn_w_o, ln_mix_g=ln_mix_g, ln_mix_b=ln_mix_b, ffn_w_in=ffn_w_in,
               ffn_w_out=ffn_w_out, ln_ffn_g=ln_ffn_g, ln_ffn_b=ln_ffn_b, ple_w_gate=ple_w_gate, ple_w_proj=ple_w_proj)
    mom = dict(att_w_qkv=m_att_w_qkv, att_sink=m_att_sink, att_w_o=m_att_w_o, hgrn_w_in=m_hgrn_w_in, hgrn_lb_logits=m_hgrn_lb_logits,
               hgrn_norm_g=m_hgrn_norm_g, hgrn_w_o=m_hgrn_w_o, ln_mix_g=m_ln_mix_g, ln_mix_b=m_ln_mix_b, ffn_w_in=m_ffn_w_in,
               ffn_w_out=m_ffn_w_out, ln_ffn_g=m_ln_ffn_g, ln_ffn_b=m_ln_ffn_b, ple_w_gate=m_ple_w_gate, ple_w_proj=m_ple_w_proj)
    var = dict(att_w_qkv=v_att_w_qkv, att_sink=v_att_sink, att_w_o=v_att_w_o, hgrn_w_in=v_hgrn_w_in, hgrn_lb_logits=v_hgrn_lb_logits,
               hgrn_norm_g=v_hgrn_norm_g, hgrn_w_o=v_hgrn_w_o, ln_mix_g=v_ln_mix_g, ln_mix_b=v_ln_mix_b, ffn_w_in=v_ffn_w_in,
               ffn_w_out=v_ffn_w_out, ln_ffn_g=v_ln_ffn_g, ln_ffn_b=v_ln_ffn_b, ple_w_gate=v_ple_w_gate, ple_w_proj=v_ple_w_proj)
    shapes = {n: wts[n].shape for n, _ in BIG}
    chip = 2 * lax.axis_index("x") + lax.axis_index("y")
    core = lax.axis_index("c").reshape(1).astype(jnp.int32)
    xin, target = x[0], loss_target[0]
    t = xin.shape[0]
    row = lambda a, i: a[i][None]

    bf_shards = {n: wts[n].astype(BF16) for n, _ in BIG}
    lb_sh = hgrn_lb_logits.shape
    lb_bits = lax.bitcast_convert_type(hgrn_lb_logits.reshape(-1), BF16).reshape(-1, PACK_W)
    groups = _gather_groups()
    packed = [_pack_parts(bf_shards, parts) for parts in groups]
    packed[0] = jnp.concatenate([packed[0], lb_bits, jnp.zeros((LB_ROWS - lb_bits.shape[0], PACK_W), BF16)], axis=0)

    gathered = _put(_run_guest(_gather_guest(packed[0]), name="gather_first")[0], packed[0], chip)
    wfull, r_big = _gathered_parts(gathered, groups[0], shapes)
    lb_rows = gathered[:, r_big:r_big + lb_bits.shape[0]].reshape(N_CHIPS, -1, 2)
    lb_full = lax.bitcast_convert_type(lb_rows, F32).reshape(N_CHIPS, lb_sh[0], lb_sh[1], lb_sh[2])
    lb_full = lb_full.transpose(1, 2, 0, 3).reshape(lb_sh[0], lb_sh[1], N_CHIPS * lb_sh[2])
    logits2d = lb_full.reshape(DEPTH, 2 * D_MODEL)
    lb_all = _lower_bounds(logits2d, name="lb_fwd").reshape(DEPTH, 2, 1, D_MODEL)
    tables = _rope_tables(t)

    saved, weights = [], []
    xf, xb = xin, xin.astype(BF16)
    for i in range(DEPTH):
        j = i // 2
        nxt = _gather_guest(packed[i + 1]) if i + 1 < DEPTH else None
        s = dict(xin_bf=xb)
        if i % 2 == 0:
            q, k, v = _qkv_rope(xb, wfull[("att_w_qkv", j)], tables, name=f"qkv_rope_{i}")
            (o, lse), got = _attn_fwd(q, k, v, att_sink[j], name=f"attn_fwd_{i}", guest=nxt)
            s.update(q=q, k=k, v=v, o=o, lse=lse)
            mix_in = o
        else:
            z = _mm_nn(xb, wfull[("hgrn_w_in", j)], out_dtype=F32, name=f"hgrn_in_{i}")
            (o_f, s_f), _ = _hgrn_fwd(z, lb_all[i, 0], False, name=f"hgrn_scan_f_{i}")
            (o_b, s_b), _ = _hgrn_fwd(z, lb_all[i, 1], True, name=f"hgrn_scan_b_{i}")
            og = _hgrn_post_fwd(o_f, o_b, z, row(hgrn_norm_g, j), name=f"hgrn_post_{i}")
            s.update(z=z, o_f=o_f, o_b=o_b, s_f=s_f, s_b=s_b, og=og)
            mix_in = og
        in_mixer = nxt is not None and i % 2 == 0
        if in_mixer:
            wfull.update(_gathered_parts(_put(got[0], packed[i + 1], chip), groups[i + 1], shapes)[0])
        w = {n: wfull[(n, l)] for n, l in _layer_parts(i)}
        w_o = w["att_w_o" if i % 2 == 0 else "hgrn_w_o"]
        x1, x1b, xh1, rs1 = _mm_res_ln(mix_in, w_o, xf, row(ln_mix_g, i), row(ln_mix_b, i), name=f"mix_out_ln_{i}")
        (g, u, h), got = _ffn_in(x1b, w["ffn_w_in"], name=f"ffn_in_{i}", guest=None if in_mixer else nxt)
        if nxt is not None and not in_mixer:
            wfull.update(_gathered_parts(_put(got[0], packed[i + 1], chip), groups[i + 1], shapes)[0])
        x2, x2b, xh2, rs2 = _mm_res_ln(h, w["ffn_w_out"], x1, row(ln_ffn_g, i), row(ln_ffn_b, i), name=f"ffn_out_ln_{i}")
        xf, xb, gate, pp = _ple_fwd(x2, x2b, p, i, w["ple_w_gate"], w["ple_w_proj"], name=f"ple_fwd_{i}")
        s.update(x1b=x1b, xh1=xh1, rs1=rs1, g=g, u=u, h=h, x2b=x2b, xh2=xh2, rs2=rs2, gate=gate, pp=pp)
        saved.append(s)
        weights.append(w)

    loss, dx = _loss_head(xf, target, name="loss_head")
    loss = lax.psum(loss[0, 0], ("x", "y", "c"))

    gs = {n: [None] * DEPTH for n in SMALL}
    gs.update(att_sink=[None] * 2, hgrn_norm_g=[None] * 2)
    dlb = [jnp.zeros((2, D_MODEL), F32)] * DEPTH
    halves = [None] * DEPTH
    slabs = None
    for i in reversed(range(DEPTH)):
        j = i // 2
        s, w = saved[i], weights[i]
        gw = {}
        res, got = _ple_bwd(dx, s["gate"], s["pp"], w["ple_w_gate"], s["xh2"], s["rs2"], row(ln_ffn_g, i), name=f"ple_bwd_{i}",
                            guest=None if slabs is None else _pair_exchange_guest(slabs))
        du2, du2b, dpre, dpp, gs["ln_ffn_g"][i], gs["ln_ffn_b"][i] = res
        part = None if slabs is None else _add_own_half(slabs, got[0], core, name=f"grad_pair_add_{i + 1}")
        gw["ple_w_gate"] = _mm_tn(s["x2b"], dpre, name=f"dw_ple_gate_{i}")
        gw["ple_w_proj"] = _mm_tn_p(p, i, dpp, name=f"dw_ple_proj_{i}")
        dgg, dgu = _ffn_out_bwd(du2b, w["ffn_w_out"], s["g"], s["u"], name=f"ffn_out_bwd_{i}")
        gw["ffn_w_out"] = _mm_tn(s["h"], du2b, name=f"dw_ffn_out_{i}")
        res = _mm_nt([dgg, dgu], w["ffn_w_in"], resid=du2, ln=(s["xh1"], s["rs1"], row(ln_mix_g, i)),
                     name=f"ffn_in_bwd_{i}", guest=None if part is None else _chip_scatter_guest(part))
        (du1, du1b, gs["ln_mix_g"][i], gs["ln_mix_b"][i]), got = res if part is not None else (res, None)
        if part is not None:
            halves[i + 1] = _sum_chips(_put(got[0], _own(part, chip), chip), name=f"grad_chip_sum_{i + 1}")
        shard_w = shapes["ffn_w_in"][2]
        gw["ffn_w_in"] = jnp.concatenate(
            [_mm_tn(s["x1b"], dgg, name=f"dw_ffn_gate_{i}", shard_cols=shard_w),
             _mm_tn(s["x1b"], dgu, name=f"dw_ffn_up_{i}", shard_cols=shard_w)], axis=0)
        if i % 2 == 0:
            gw["att_w_o"] = _mm_tn(s["o"], du1b, name=f"dw_att_o_{i}")
            do = _mm_nt([du1b], w["att_w_o"], out_dtype=BF16, name=f"att_o_bwd_{i}")
            dq, dkw, dvw, gs["att_sink"][j] = _attn_bwd(s["q"], s["k"], s["v"], s["o"], do, s["lse"], att_sink[j], name=f"attn_bwd_{i}")
            dqkv = _attn_post_bwd(dq, dkw, dvw, tables, name=f"attn_post_bwd_{i}")
            gw["att_w_qkv"] = _mm_tn(s["xin_bf"], dqkv, name=f"dw_att_qkv_{i}")
            dx = _mm_nt([dqkv], w["att_w_qkv"], resid=du1, name=f"qkv_bwd_{i}")
        else:
            gw["hgrn_w_o"] = _mm_tn(s["og"], du1b, name=f"dw_hgrn_o_{i}")
            dy = _mm_nt([du1b], w["hgrn_w_o"], out_dtype=BF16, name=f"hgrn_o_bwd_{i}")
            d_o, dgate, gs["hgrn_norm_g"][j] = _hgrn_post_bwd(dy, s["o_f"], s["o_b"], s["z"], row(hgrn_norm_g, j), name=f"hgrn_post_bwd_{i}")
            dq_f, dzf_f, dv_f, dlb_f = _hgrn_bwd(s["z"], lb_all[i, 0], d_o, s["s_f"], False, name=f"hgrn_scan_f_bwd_{i}")
            dq, dzf_b, dv, dlb_b = _hgrn_bwd(s["z"], lb_all[i, 1], d_o, s["s_b"], True, name=f"hgrn_scan_b_bwd_{i}",
                                             other=(dq_f, dv_f))
            dlb[i] = jnp.stack([dlb_f.reshape(D_MODEL), dlb_b.reshape(D_MODEL)])
            dz = [dq, dzf_f, dzf_b, dv, dgate]
            gw["hgrn_w_in"] = jnp.concatenate(
                [_mm_tn(s["xin_bf"], part, name=f"dw_hgrn_in_{i}_{n}") for n, part in enumerate(dz)], axis=1)
            dx = _mm_nt(dz, w["hgrn_w_in"], resid=du1, name=f"hgrn_in_bwd_{i}")
        slabs = _layer_slabs(gw, i, shapes)

    from_sibling = _run_guest(_pair_exchange_guest(slabs), name="grad_pair_exchange_0")[0]
    part = _add_own_half(slabs, from_sibling, core, name="grad_pair_add_0")
    from_chips = _run_guest(_chip_scatter_guest(part), name="grad_chip_scatter_0")[0]
    halves[0] = _sum_chips(_put(from_chips, _own(part, chip), chip), name="grad_chip_sum_0")
    shared = _run_guest(_pair_share_guest(halves), name="grad_pair_share")
    reduced = [_put(buf, half, lax.axis_index("c")) for buf, half in zip(shared, halves)]

    g_layers = [_unpack_layer(reduced[i].reshape(-1, PACK_W), i, shapes) for i in range(DEPTH)]
    grads = {}
    for n, _ in BIG:
        per_layer = [g_layers[i][n] for i in range(DEPTH) if n in g_layers[i]]
        grads[n] = jnp.stack(per_layer)

    gsmall = {n: jnp.concatenate(v, axis=0) for n, v in gs.items()}
    dlogits = _lower_bounds_bwd(logits2d, jnp.stack(dlb).reshape(DEPTH, 2 * D_MODEL), name="lb_bwd").reshape(DEPTH, 2, D_MODEL)
    g_small_full = _unpack_small(_allreduce_small(_pack_small(gsmall, dlogits), name="allreduce_small"),
                                 (lb_sh[0], lb_sh[1], N_CHIPS * lb_sh[2]))
    grads.update({n: g_small_full[n] for n in SMALL + ("att_sink", "hgrn_norm_g")})
    grads["hgrn_lb_logits"] = lax.dynamic_slice_in_dim(g_small_full["hgrn_lb_logits"], chip * lb_sh[2], lb_sh[2], axis=2)

    delta, new_m, new_v = {}, {}, {}
    for n, _ in BIG:
        two_d = lambda a: a.reshape(-1, a.shape[-1])
        d, nm, nv = _adamw(two_d(wts[n]), two_d(grads[n]), two_d(mom[n]), two_d(var[n]), name=f"adamw_{n}")
        delta[n], new_m[n], new_v[n] = d.reshape(shapes[n]), nm.reshape(shapes[n]), nv.reshape(shapes[n])
    packs = [_pack_small(src, src["hgrn_lb_logits"]) for src in (wts, grads, mom, var)]
    outs = _adamw(*packs, name="adamw_small")
    for dst, buf in zip((delta, new_m, new_v), outs):
        dst.update(_unpack_small(buf, lb_sh))

    return (loss, dx[None], *[grads[n] for n in WEIGHTS], *[delta[n] for n in WEIGHTS],
            *[new_m[n] for n in WEIGHTS], *[new_v[n] for n in WEIGHTS])
```

```python
import functools
from typing import Callable, NamedTuple

import jax
import jax.numpy as jnp
from jax import lax
from jax.experimental import pallas as pl
from jax.experimental.pallas import tpu as pltpu

F32, BF16 = jnp.float32, jnp.bfloat16

D_MODEL = 1024
DEPTH = 4
HEAD_DIM = 64
N_Q_HEADS = 16
N_KV_HEADS = 4
GROUP = 4
Q_DIM = 1024
KV_DIM = 256
WINDOW = 128
ROPE_DIM = 16
ROPE_THETA = 500000.0
HGRN_HEADS = 8
HGRN_KEY = 128
HGRN_CHUNK = 64
D_FF = 2816
PLE_DIM = 256
ALPHA = (2 * DEPTH) ** 0.25
LN_EPS = 1e-5
ADAM_LR, ADAM_B1, ADAM_B2, ADAM_EPS, ADAM_WD, ADAM_STEP = 0.001, 0.9, 0.999, 1e-08, 0.01, 10

LANES = 128
VMEM_LIMIT_BYTES = 56 * 2**20
FACTORED_DECAY_LIMIT = -80.0

NN = ((1,), (0,))
NT = ((1,), (1,))
TN = ((0,), (0,))

N_CHIPS = 4
MESH = pl.DeviceIdType.MESH


def _dot(a, b, dims):
    return lax.dot_general(a, b, (dims, ((), ())), preferred_element_type=F32)


def _cp(*sem):
    return pltpu.CompilerParams(dimension_semantics=sem, vmem_limit_bytes=VMEM_LIMIT_BYTES)


def _rows(t, cap=512):
    return min(cap, t)


def _pick(n, cap):
    best = None
    for d in range(LANES, min(n, cap) + 1, LANES):
        if n % d == 0:
            best = d
    assert best is not None, (n, cap)
    return best


def _sigmoid(x):
    return jax.nn.sigmoid(x)


def _ln_fwd(u, g, b):
    mu = jnp.mean(u, axis=-1, keepdims=True)
    xc = u - mu
    var = jnp.mean(xc * xc, axis=-1, keepdims=True)
    rstd = lax.rsqrt(var + LN_EPS)
    xhat = xc * rstd
    return xhat * g + b, xhat, rstd


def _ln_bwd(dy, xhat, rstd, g):
    dxh = dy * g
    m1 = jnp.mean(dxh, axis=-1, keepdims=True)
    m2 = jnp.mean(dxh * xhat, axis=-1, keepdims=True)
    du = rstd * (dxh - m1 - xhat * m2)
    return du, jnp.sum(dy * xhat, axis=0, keepdims=True), jnp.sum(dy, axis=0, keepdims=True)


def _full(shape):
    nd = len(shape)
    return pl.BlockSpec(shape, lambda *_: (0,) * nd)


ANY = pl.BlockSpec(memory_space=pl.ANY)


class _Guest(NamedTuple):
    args: tuple
    out_shape: tuple
    sems: tuple
    start: Callable
    finish: Callable


def _call(body, *, grid, in_specs, out_specs, out_shape, args, sem, name, scratch_shapes=(), guest=None):
    n_in, n_out, n_scr = len(args), len(out_shape), len(scratch_shapes)
    if guest is None:
        res = pl.pallas_call(body, grid=grid, in_specs=list(in_specs), out_specs=list(out_specs), out_shape=list(out_shape),
                             scratch_shapes=list(scratch_shapes), compiler_params=_cp(*sem), name=name)(*args)
        return list(res), []
    g_in, g_out = len(guest.args), len(guest.out_shape)

    def hosted(*refs):
        cuts = [n_in, g_in, n_out, g_out, n_scr]
        parts, pos = [], 0
        for n in cuts:
            parts.append(refs[pos:pos + n])
            pos += n
        h_in, gi, h_out, go, h_scr = parts
        gs = refs[pos:]
        ids = [pl.program_id(d) for d in range(len(grid))]
        first = functools.reduce(jnp.logical_and, [i == 0 for i in ids])
        last = functools.reduce(jnp.logical_and, [i == n - 1 for i, n in zip(ids, grid)])

        @pl.when(first)
        def _():
            guest.start(gi, go, gs)
        body(*h_in, *h_out, *h_scr)

        @pl.when(last)
        def _():
            guest.finish(gi, go, gs)

    res = pl.pallas_call(
        hosted, grid=grid, in_specs=list(in_specs) + [ANY] * g_in, out_specs=list(out_specs) + [ANY] * g_out,
        out_shape=list(out_shape) + list(guest.out_shape), scratch_shapes=list(scratch_shapes) + list(guest.sems),
        compiler_params=_cp(*(("arbitrary",) * len(grid))), name=name)(*args, *guest.args)
    return list(res[:n_out]), list(res[n_out:])


def _run_guest(guest, *, name):
    g_in = len(guest.args)

    def body(*refs):
        gi, go, gs = refs[:g_in], refs[g_in:g_in + len(guest.out_shape)], refs[g_in + len(guest.out_shape):]
        guest.start(gi, go, gs)
        guest.finish(gi, go, gs)

    return pl.pallas_call(body, in_specs=[ANY] * g_in, out_specs=[ANY] * len(guest.out_shape), out_shape=list(guest.out_shape),
                          scratch_shapes=list(guest.sems), name=name)(*guest.args)


def _mm_nn(a, w, *, out_dtype, name):
    t, k = a.shape
    n = w.shape[1]
    tm, tn = _rows(t), _pick(n, 1408)

    def body(a_ref, w_ref, o_ref):
        o_ref[...] = _dot(a_ref[...], w_ref[...], NN).astype(out_dtype)

    return pl.pallas_call(
        body, grid=(n // tn, t // tm),
        in_specs=[pl.BlockSpec((tm, k), lambda j, i: (i, 0)), pl.BlockSpec((k, tn), lambda j, i: (0, j))],
        out_specs=pl.BlockSpec((tm, tn), lambda j, i: (i, j)),
        out_shape=jax.ShapeDtypeStruct((t, n), out_dtype),
        compiler_params=_cp("parallel", "parallel"), name=name)(a, w)


def _mm_tn(a, b, *, name, guest=None, shard_cols=None):
    r, m = a.shape
    n = b.shape[1]
    tr, tm, tn = _rows(r), _pick(m, 1408), _pick(n, 2816)
    c = tn if shard_cols is None else shard_cols
    per = tn // c

    def body(a_ref, b_ref, o_ref):
        @pl.when(pl.program_id(2) == 0)
        def _():
            o_ref[...] = jnp.zeros_like(o_ref)
        res = _dot(a_ref[...].astype(BF16), b_ref[...].astype(BF16), TN)
        if shard_cols is None:
            o_ref[...] += res
        else:
            for s in range(per):
                o_ref[s] += res[:, s * c:(s + 1) * c]

    if shard_cols is None:
        out_spec, out_shape = pl.BlockSpec((tm, tn), lambda i, j, k: (i, j)), jax.ShapeDtypeStruct((m, n), F32)
    else:
        out_spec, out_shape = pl.BlockSpec((per, tm, c), lambda i, j, k: (j, i, 0)), jax.ShapeDtypeStruct((n // c, m, c), F32)
    res, extra = _call(
        body, grid=(m // tm, n // tn, r // tr),
        in_specs=[pl.BlockSpec((tr, tm), lambda i, j, k: (k, i)), pl.BlockSpec((tr, tn), lambda i, j, k: (k, j))],
        out_specs=[out_spec], out_shape=[out_shape], args=(a, b),
        sem=("parallel", "parallel", "arbitrary"), name=name, guest=guest)
    return res[0] if guest is None else (res[0], extra)


def _mm_tn_p(p, layer, b, *, name):
    t = p.shape[2]
    n = b.shape[1]
    tr = _rows(t)

    def body(a_ref, b_ref, o_ref):
        @pl.when(pl.program_id(0) == 0)
        def _():
            o_ref[...] = jnp.zeros_like(o_ref)
        o_ref[...] += _dot(a_ref[...].astype(BF16), b_ref[...], TN)

    return pl.pallas_call(
        body, grid=(t // tr,),
        in_specs=[pl.BlockSpec((None, None, tr, PLE_DIM), lambda k: (layer, 0, k, 0)),
                  pl.BlockSpec((tr, n), lambda k: (k, 0))],
        out_specs=pl.BlockSpec((PLE_DIM, n), lambda k: (0, 0)),
        out_shape=jax.ShapeDtypeStruct((PLE_DIM, n), F32),
        compiler_params=_cp("arbitrary"), name=name)(p, b)


def _mm_res_ln(a, w, resid, g, b, *, name):
    t, k = a.shape
    tm = _rows(t)

    def body(a_ref, w_ref, r_ref, g_ref, b_ref, y_ref, ybf_ref, xh_ref, rs_ref):
        acc = _dot(a_ref[...], w_ref[...], NN)
        y, xh, rs = _ln_fwd(ALPHA * r_ref[...] + acc, g_ref[...], b_ref[...])
        y_ref[...] = y
        ybf_ref[...] = y.astype(BF16)
        xh_ref[...] = xh.astype(BF16)
        rs_ref[...] = rs

    row = pl.BlockSpec((tm, D_MODEL), lambda i: (i, 0))
    return pl.pallas_call(
        body, grid=(t // tm,),
        in_specs=[pl.BlockSpec((tm, k), lambda i: (i, 0)), _full((k, D_MODEL)), row, _full((1, D_MODEL)), _full((1, D_MODEL))],
        out_specs=[row, row, row, pl.BlockSpec((tm, 1), lambda i: (i, 0))],
        out_shape=[jax.ShapeDtypeStruct((t, D_MODEL), F32), jax.ShapeDtypeStruct((t, D_MODEL), BF16),
                   jax.ShapeDtypeStruct((t, D_MODEL), BF16), jax.ShapeDtypeStruct((t, 1), F32)],
        compiler_params=_cp("parallel"), name=name)(a, w, resid, g, b)


def _ffn_in(x_bf, w, *, name, guest=None):
    t = x_bf.shape[0]
    tm, tn = _rows(t), _pick(D_FF, 1408)
    nb = D_FF // tn
    assert w.shape == (2 * nb, D_MODEL, tn), w.shape

    def body(x_ref, wg_ref, wu_ref, dg_ref, du_ref, h_ref):
        x = x_ref[...]
        g = _dot(x, wg_ref[...], NN)
        u = _dot(x, wu_ref[...], NN)
        sig = _sigmoid(g)
        silu = g * sig
        dg_ref[...] = (u * sig * (1.0 + g * (1.0 - sig))).astype(BF16)
        du_ref[...] = silu.astype(BF16)
        h_ref[...] = (silu * u).astype(BF16)

    tile = pl.BlockSpec((tm, tn), lambda j, i: (i, j))
    shp = jax.ShapeDtypeStruct((t, D_FF), BF16)
    return _call(
        body, grid=(nb, t // tm),
        in_specs=[pl.BlockSpec((tm, D_MODEL), lambda j, i: (i, 0)),
                  pl.BlockSpec((None, D_MODEL, tn), lambda j, i: (j, 0, 0)),
                  pl.BlockSpec((None, D_MODEL, tn), lambda j, i: (j + nb, 0, 0))],
        out_specs=[tile, tile, tile], out_shape=[shp, shp, shp], args=(x_bf, w, w),
        sem=("parallel", "parallel"), name=name, guest=guest)


def _ple_fwd(x, x_bf, p, layer, wg, wp, *, name):
    t = x.shape[0]
    tm = _rows(t)

    def body(x_ref, xbf_ref, p_ref, wg_ref, wp_ref, y_ref, ybf_ref, gate_ref, pp_ref):
        gate = _sigmoid(_dot(xbf_ref[...], wg_ref[...], NN))
        pp = _dot(p_ref[...].astype(BF16), wp_ref[...], NN)
        y = x_ref[...] + gate * pp
        y_ref[...] = y
        ybf_ref[...] = y.astype(BF16)
        gate_ref[...] = gate.astype(BF16)
        pp_ref[...] = pp.astype(BF16)

    row = pl.BlockSpec((tm, D_MODEL), lambda i: (i, 0))
    f32, bf = jax.ShapeDtypeStruct((t, D_MODEL), F32), jax.ShapeDtypeStruct((t, D_MODEL), BF16)
    return pl.pallas_call(
        body, grid=(t // tm,),
        in_specs=[row, row, pl.BlockSpec((None, None, tm, PLE_DIM), lambda i: (layer, 0, i, 0)),
                  _full((D_MODEL, D_MODEL)), _full((PLE_DIM, D_MODEL))],
        out_specs=[row, row, row, row], out_shape=[f32, bf, bf, bf],
        compiler_params=_cp("parallel"), name=name)(x, x_bf, p, wg, wp)


def _loss_head(y, target, *, name):
    t = y.shape[0]
    tm = _rows(t)

    def body(y_ref, t_ref, loss_ref, dy_ref):
        e = y_ref[...] - t_ref[...]

        @pl.when(pl.program_id(0) == 0)
        def _():
            loss_ref[...] = jnp.zeros_like(loss_ref)
        sq = jnp.sum(jnp.sum(e * e, axis=1, keepdims=True), axis=0, keepdims=True)
        loss_ref[...] += sq * (0.5 / D_MODEL)
        dy_ref[...] = e * (1.0 / D_MODEL)

    row = pl.BlockSpec((tm, D_MODEL), lambda i: (i, 0))
    return pl.pallas_call(
        body, grid=(t // tm,), in_specs=[row, row],
        out_specs=[_full((1, 1)), row],
        out_shape=[jax.ShapeDtypeStruct((1, 1), F32), jax.ShapeDtypeStruct((t, D_MODEL), F32)],
        compiler_params=_cp("arbitrary"), name=name)(y, target)


def _ple_bwd(dx3, gate, pp, wg, xhat, rstd, g, *, name, guest=None):
    t = dx3.shape[0]
    tm = _rows(t)

    def body(dx_ref, gate_ref, pp_ref, wg_ref, xh_ref, rs_ref, g_ref,
             du_ref, dubf_ref, dpre_ref, dpp_ref, dg_ref, db_ref):
        dx = dx_ref[...]
        gate = gate_ref[...].astype(F32)
        dpre = (dx * pp_ref[...].astype(F32) * gate * (1.0 - gate)).astype(BF16)
        dpre_ref[...] = dpre
        dpp_ref[...] = (dx * gate).astype(BF16)
        dx2 = dx + _dot(dpre, wg_ref[...], NT)
        du, dg, db = _ln_bwd(dx2, xh_ref[...].astype(F32), rs_ref[...], g_ref[...])
        du_ref[...] = du
        dubf_ref[...] = du.astype(BF16)

        @pl.when(pl.program_id(0) == 0)
        def _():
            dg_ref[...] = jnp.zeros_like(dg_ref)
            db_ref[...] = jnp.zeros_like(db_ref)
        dg_ref[...] += dg
        db_ref[...] += db

    row = pl.BlockSpec((tm, D_MODEL), lambda i: (i, 0))
    vec = _full((1, D_MODEL))
    f32, bf = jax.ShapeDtypeStruct((t, D_MODEL), F32), jax.ShapeDtypeStruct((t, D_MODEL), BF16)
    v32 = jax.ShapeDtypeStruct((1, D_MODEL), F32)
    res, extra = _call(
        body, grid=(t // tm,),
        in_specs=[row, row, row, _full((D_MODEL, D_MODEL)), row, pl.BlockSpec((tm, 1), lambda i: (i, 0)), vec],
        out_specs=[row, row, row, row, vec, vec], out_shape=[f32, bf, bf, bf, v32, v32],
        args=(dx3, gate, pp, wg, xhat, rstd, g), sem=("arbitrary",), name=name, guest=guest)
    return res, extra


def _ffn_out_bwd(du_bf, w_out, g, u, *, name):
    t = du_bf.shape[0]
    tm, tn = _rows(t), _pick(D_FF, 1408)

    def body(du_ref, w_ref, hg_ref, hu_ref, dg_ref, dup_ref):
        dh = _dot(du_ref[...], w_ref[...], NT)
        dg_ref[...] = (dh * hg_ref[...].astype(F32)).astype(BF16)
        dup_ref[...] = (dh * hu_ref[...].astype(F32)).astype(BF16)

    tile = pl.BlockSpec((tm, tn), lambda j, i: (i, j))
    shp = jax.ShapeDtypeStruct((t, D_FF), BF16)
    return pl.pallas_call(
        body, grid=(D_FF // tn, t // tm),
        in_specs=[pl.BlockSpec((tm, D_MODEL), lambda j, i: (i, 0)), pl.BlockSpec((tn, D_MODEL), lambda j, i: (j, 0)),
                  tile, tile],
        out_specs=[tile, tile], out_shape=[shp, shp],
        compiler_params=_cp("parallel", "parallel"), name=name)(du_bf, w_out, g, u)


def _mm_nt(parts, w, *, resid=None, ln=None, out_dtype=F32, name, guest=None):
    t, kp = parts[0].shape
    npart = len(parts)
    tm = _rows(t)
    n_in = npart + 1 + (resid is not None) + (3 if ln is not None else 0)

    def body(*refs):
        a_refs, w_ref = refs[:npart], refs[npart]
        pos = npart + 1
        r_ref = None
        if resid is not None:
            r_ref = refs[pos]
            pos += 1
        if ln is not None:
            xh_ref, rs_ref, g_ref = refs[pos:pos + 3]
        outs = refs[n_in:]
        if w.ndim == 2:
            terms = [_dot(a_refs[pi][...], w_ref[:, pi * kp:(pi + 1) * kp], NT) for pi in range(npart)]
        else:
            c, per = w.shape[2], w.shape[0] // npart
            terms = [_dot(a_refs[pi][:, s * c:(s + 1) * c], w_ref[pi * per + s], NT) for pi in range(npart) for s in range(per)]
        val = functools.reduce(lambda u, v: u + v, terms)
        if r_ref is not None:
            val = val + ALPHA * r_ref[...]
        if ln is None:
            outs[0][...] = val.astype(out_dtype)
        else:
            du, dg, db = _ln_bwd(val, xh_ref[...].astype(F32), rs_ref[...], g_ref[...])
            outs[0][...] = du
            outs[1][...] = du.astype(BF16)

            @pl.when(pl.program_id(0) == 0)
            def _():
                outs[2][...] = jnp.zeros_like(outs[2])
                outs[3][...] = jnp.zeros_like(outs[3])
            outs[2][...] += dg
            outs[3][...] += db

    row = pl.BlockSpec((tm, D_MODEL), lambda i: (i, 0))
    vec = pl.BlockSpec((1, D_MODEL), lambda i: (0, 0))
    in_specs = [pl.BlockSpec((tm, kp), lambda i: (i, 0)) for _ in range(npart)]
    in_specs.append(pl.BlockSpec(w.shape, lambda i: (0,) * w.ndim, pipeline_mode=pl.Buffered(1)))
    args = list(parts) + [w]
    if resid is not None:
        in_specs.append(row)
        args.append(resid)
    if ln is not None:
        in_specs += [row, pl.BlockSpec((tm, 1), lambda i: (i, 0)), vec]
        args += list(ln)
        out_specs = [row, row, vec, vec]
        out_shape = [jax.ShapeDtypeStruct((t, D_MODEL), F32), jax.ShapeDtypeStruct((t, D_MODEL), BF16),
                     jax.ShapeDtypeStruct((1, D_MODEL), F32), jax.ShapeDtypeStruct((1, D_MODEL), F32)]
    else:
        out_specs = [row]
        out_shape = [jax.ShapeDtypeStruct((t, D_MODEL), out_dtype)]
    res, extra = _call(
        body, grid=(t // tm,), in_specs=in_specs, out_specs=out_specs, out_shape=out_shape, args=tuple(args),
        sem=("arbitrary" if ln is not None else "parallel",), name=name, guest=guest)
    res = res if ln is not None else res[0]
    return res if guest is None else (res, extra)


def _rope_tables(t):
    half = ROPE_DIM // 2
    inv = ROPE_THETA ** (-jnp.arange(0, ROPE_DIM, 2, dtype=F32) / ROPE_DIM)
    ang = jnp.arange(t, dtype=F32)[:, None] * inv[None, :]
    cos, sin = jnp.cos(ang), jnp.sin(ang)
    pad = HEAD_DIM - ROPE_DIM
    c = jnp.concatenate([cos, cos, jnp.ones((t, pad), F32)], axis=1)
    s_hi = jnp.concatenate([jnp.zeros((t, half), F32), sin, jnp.zeros((t, pad), F32)], axis=1)
    s_lo = jnp.concatenate([-sin, jnp.zeros((t, half + pad), F32)], axis=1)
    rep = LANES // HEAD_DIM
    return jnp.tile(c, (1, rep)), jnp.tile(s_hi, (1, rep)), jnp.tile(s_lo, (1, rep))


def _rope(x, c, s_hi, s_lo, sign):
    half = ROPE_DIM // 2
    parts = []
    for j in range(x.shape[1] // LANES):
        xg = x[:, j * LANES:(j + 1) * LANES]
        rot = pltpu.roll(xg, half, 1) * s_hi + pltpu.roll(xg, LANES - half, 1) * s_lo
        parts.append(xg * c + sign * rot)
    return jnp.concatenate(parts, axis=1)


def _qkv_rope(x_bf, w, tables, *, name):
    t = x_bf.shape[0]
    tm = _rows(t)
    n = Q_DIM + 2 * KV_DIM

    def body(x_ref, w_ref, c_ref, sh_ref, sl_ref, q_ref, k_ref, v_ref):
        acc = _dot(x_ref[...], w_ref[...], NN)
        c, sh, sl = c_ref[...], sh_ref[...], sl_ref[...]
        q_ref[...] = (_rope(acc[:, :Q_DIM], c, sh, sl, 1.0) * HEAD_DIM ** -0.5).astype(BF16)
        k_ref[...] = _rope(acc[:, Q_DIM:Q_DIM + KV_DIM], c, sh, sl, 1.0).astype(BF16)
        v_ref[...] = acc[:, Q_DIM + KV_DIM:].astype(BF16)

    tab = pl.BlockSpec((tm, LANES), lambda i: (i, 0))
    return pl.pallas_call(
        body, grid=(t // tm,),
        in_specs=[pl.BlockSpec((tm, D_MODEL), lambda i: (i, 0)), _full((D_MODEL, n)), tab, tab, tab],
        out_specs=[pl.BlockSpec((tm, Q_DIM), lambda i: (i, 0)), pl.BlockSpec((tm, KV_DIM), lambda i: (i, 0)),
                   pl.BlockSpec((tm, KV_DIM), lambda i: (i, 0))],
        out_shape=[jax.ShapeDtypeStruct((t, Q_DIM), BF16), jax.ShapeDtypeStruct((t, KV_DIM), BF16),
                   jax.ShapeDtypeStruct((t, KV_DIM), BF16)],
        compiler_params=_cp("parallel"), name=name)(x_bf, w, *tables)


ATT_ROWS = 256


def _window_specs(t, tq):
    r = tq // WINDOW
    last = t // WINDOW - 1
    return [pl.BlockSpec((WINDOW, KV_DIM), lambda i: (jnp.maximum(i * r - 1, 0), 0)),
            pl.BlockSpec((tq, KV_DIM), lambda i: (i, 0)),
            pl.BlockSpec((WINDOW, KV_DIM), lambda i: (jnp.minimum((i + 1) * r, last), 0))]


def _att_valid(base, t):
    rows = GROUP * WINDOW
    qpos = base + (lax.broadcasted_iota(jnp.int32, (rows, 3 * WINDOW), 0) & (WINDOW - 1))
    kpos = base - WINDOW + lax.broadcasted_iota(jnp.int32, (rows, 3 * WINDOW), 1)
    return (jnp.abs(qpos - kpos) <= WINDOW) & (kpos >= 0) & (kpos < t)


def _stack_heads(x, r0, g):
    return jnp.concatenate(
        [x[r0:r0 + WINDOW, (GROUP * g + h) * HEAD_DIM:(GROUP * g + h + 1) * HEAD_DIM] for h in range(GROUP)], axis=0)


def _sink_column(sink_ref, g):
    return jnp.concatenate([jnp.full((WINDOW, 1), sink_ref[GROUP * g + h], F32) for h in range(GROUP)], axis=0)


def _unstack_heads(pieces):
    return jnp.concatenate([pieces[g][h * WINDOW:(h + 1) * WINDOW] for g in range(N_KV_HEADS) for h in range(GROUP)], axis=1)


def _attn_fwd(q, k, v, sink, *, name, guest=None):
    t = q.shape[0]
    tq = min(ATT_ROWS, t)
    nsb = tq // WINDOW

    def body(sink_ref, q_ref, kp_ref, kc_ref, kn_ref, vp_ref, vc_ref, vn_ref, o_ref, l_ref):
        i = pl.program_id(0)
        qv = q_ref[...]
        kw = jnp.concatenate([kp_ref[...], kc_ref[...], kn_ref[...]], axis=0)
        vw = jnp.concatenate([vp_ref[...], vc_ref[...], vn_ref[...]], axis=0)
        ones = jnp.ones((3 * WINDOW, HEAD_DIM), BF16)
        for sb in range(nsb):
            r0 = sb * WINDOW
            bias =jnp.where(_att_valid(i * tq + r0, t)[:WINDOW], 0.0, -1e30)
            pieces = []
            for hh in range(N_Q_HEADS):
                g, h = hh // GROUP, hh % GROUP
                qh = qv[r0:r0 + WINDOW, hh * HEAD_DIM:(hh + 1) * HEAD_DIM]
                kg = kw[r0:r0 + 3 * WINDOW, g * HEAD_DIM:(g + 1) * HEAD_DIM]
                vg = jnp.concatenate([vw[r0:r0 + 3 * WINDOW, g * HEAD_DIM:(g + 1) * HEAD_DIM], ones], axis=1)
                s = _dot(qh, kg, NT) + bias
                snk = sink_ref[hh]
                m = jnp.maximum(jnp.max(s, axis=1, keepdims=True), snk)
                ov = _dot(jnp.exp(s - m).astype(BF16), vg, NN)
                den = ov[:, HEAD_DIM:HEAD_DIM + 1] + jnp.exp(snk - m)
                pieces.append(ov[:, :HEAD_DIM] * (1.0 / den))
                l_ref[g, sb, h * WINDOW:(h + 1) * WINDOW, :] = m + jnp.log(den)
            o_ref[r0:r0 + WINDOW, :] = jnp.concatenate(pieces, axis=1).astype(BF16)

    return _call(
        body, grid=(t // tq,),
        in_specs=[pl.BlockSpec(memory_space=pltpu.SMEM), pl.BlockSpec((tq, Q_DIM), lambda i: (i, 0))]
        + _window_specs(t, tq) + _window_specs(t, tq),
        out_specs=[pl.BlockSpec((tq, Q_DIM), lambda i: (i, 0)),
                   pl.BlockSpec((N_KV_HEADS, nsb, GROUP * WINDOW, 1), lambda i: (0, i, 0, 0))],
        out_shape=[jax.ShapeDtypeStruct((t, Q_DIM), BF16),
                   jax.ShapeDtypeStruct((N_KV_HEADS, t // WINDOW, GROUP * WINDOW, 1), F32)],
        args=(sink, q, k, k, k, v, v, v), sem=("parallel",), name=name, guest=guest)


def _attn_bwd(q, k, v, o, do, lse, sink, *, name):
    t = q.shape[0]
    tq = min(ATT_ROWS, t)
    nsb = tq // WINDOW

    def body(sink_ref, q_ref, o_ref, do_ref, l_ref, kp_ref, kc_ref, kn_ref, vp_ref, vc_ref, vn_ref,
             dq_ref, dkw_ref, dvw_ref, dsink_ref):
        i = pl.program_id(0)
        qv, ov, dov = q_ref[...], o_ref[...], do_ref[...]
        kw = jnp.concatenate([kp_ref[...], kc_ref[...], kn_ref[...]], axis=0)
        vw = jnp.concatenate([vp_ref[...], vc_ref[...], vn_ref[...]], axis=0)
        lane16 = lax.broadcasted_iota(jnp.int32, (1, N_Q_HEADS), 1)
        dsink = jnp.zeros((1, N_Q_HEADS), F32)
        for sb in range(nsb):
            r0 = sb * WINDOW
            valid = _att_valid(i * tq + r0, t)
            dq_p, dk_p, dv_p = [], [], []
            for g in range(N_KV_HEADS):
                qs, dos = _stack_heads(qv, r0, g), _stack_heads(dov, r0, g)
                delta = jnp.sum(dos.astype(F32) * _stack_heads(ov, r0, g).astype(F32), axis=1, keepdims=True)
                kg = kw[r0:r0 + 3 * WINDOW, g * HEAD_DIM:(g + 1) * HEAD_DIM]
                vg = vw[r0:r0 + 3 * WINDOW, g * HEAD_DIM:(g + 1) * HEAD_DIM]
                lse_col = l_ref[g, sb]
                pr = jnp.where(valid, jnp.exp(_dot(qs, kg, NT) - lse_col), 0.0)
                ds = (pr * (_dot(dos, vg, NT) - delta)).astype(BF16)
                dq_p.append(_dot(ds, kg, NN))
                dk_p.append(_dot(ds, qs, TN))
                dv_p.append(_dot(pr.astype(BF16), dos, TN))
                ps = jnp.exp(_sink_column(sink_ref, g) - lse_col) * delta
                for h in range(GROUP):
                    tot = jnp.sum(ps[h * WINDOW:(h + 1) * WINDOW], axis=0, keepdims=True)
                    dsink = dsink - jnp.where(lane16 == GROUP * g + h, tot, 0.0)
            dq_ref[r0:r0 + WINDOW, :] = _unstack_heads(dq_p)
            dkw_ref[sb] = jnp.concatenate(dk_p, axis=1)
            dvw_ref[sb] = jnp.concatenate(dv_p, axis=1)

        @pl.when(i == 0)
        def _():
            dsink_ref[...] = jnp.zeros_like(dsink_ref)
        dsink_ref[...] += dsink

    rowq = pl.BlockSpec((tq, Q_DIM), lambda i: (i, 0))
    win = pl.BlockSpec((nsb, 3 * WINDOW, KV_DIM), lambda i: (i, 0, 0))
    wshape = jax.ShapeDtypeStruct((t // WINDOW, 3 * WINDOW, KV_DIM), F32)
    return pl.pallas_call(
        body, grid=(t // tq,),
        in_specs=[pl.BlockSpec(memory_space=pltpu.SMEM), rowq, rowq, rowq,
                  pl.BlockSpec((N_KV_HEADS, nsb, GROUP * WINDOW, 1), lambda i: (0, i, 0, 0))]
        + _window_specs(t, tq) + _window_specs(t, tq),
        out_specs=[rowq, win, win, _full((1, N_Q_HEADS))],
        out_shape=[jax.ShapeDtypeStruct((t, Q_DIM), F32), wshape, wshape, jax.ShapeDtypeStruct((1, N_Q_HEADS), F32)],
        compiler_params=_cp("arbitrary"), name=name)(sink, q, o, do, lse, k, k, k, v, v, v)


def _attn_post_bwd(dq, dkw, dvw, tables, *, name):
    t = dq.shape[0]
    nb = t // WINDOW
    n = Q_DIM + 2 * KV_DIM

    def body(dq_ref, ka_ref, kb_ref, kc_ref, va_ref, vb_ref, vc_ref, c_ref, sh_ref, sl_ref, o_ref):
        j = pl.program_id(0)
        lo = (j > 0).astype(F32)
        hi = (j < nb - 1).astype(F32)
        c, sh, sl = c_ref[...], sh_ref[...], sl_ref[...]
        dk = ka_ref[...] * hi + kb_ref[...] + kc_ref[...] * lo
        dv = va_ref[...] * hi + vb_ref[...] + vc_ref[...] * lo
        o_ref[:, :Q_DIM] = (_rope(dq_ref[...], c, sh, sl, -1.0) * HEAD_DIM ** -0.5).astype(BF16)
        o_ref[:, Q_DIM:Q_DIM + KV_DIM] = _rope(dk, c, sh, sl, -1.0).astype(BF16)
        o_ref[:, Q_DIM + KV_DIM:] = dv.astype(BF16)

    wins = [pl.BlockSpec((None, WINDOW, KV_DIM), lambda j: (jnp.minimum(j + 1, nb - 1), 0, 0)),
            pl.BlockSpec((None, WINDOW, KV_DIM), lambda j: (j, 1, 0)),
            pl.BlockSpec((None, WINDOW, KV_DIM), lambda j: (jnp.maximum(j - 1, 0), 2, 0))]
    tab = pl.BlockSpec((WINDOW, LANES), lambda j: (j, 0))
    return pl.pallas_call(
        body, grid=(nb,),
        in_specs=[pl.BlockSpec((WINDOW, Q_DIM), lambda j: (j, 0))] + wins + wins + [tab, tab, tab],
        out_specs=pl.BlockSpec((WINDOW, n), lambda j: (j, 0)),
        out_shape=jax.ShapeDtypeStruct((t, n), BF16),
        compiler_params=_cp("parallel"), name=name)(dq, dkw, dkw, dkw, dvw, dvw, dvw, *tables)


HGRN_ROWS = 512
HGRN_UNROLL = 2
HGRN_UNROLL_FWD = 4


def _cum_mask(rev):
    row = lax.broadcasted_iota(jnp.int32, (HGRN_CHUNK, HGRN_CHUNK), 0)
    col = lax.broadcasted_iota(jnp.int32, (HGRN_CHUNK, HGRN_CHUNK), 1)
    return (row <= col) if rev else (row >= col)


def _gates(zq, zf, lb):
    q = zq * _sigmoid(zq)
    sig = _sigmoid(zf)
    f = lb + (1.0 - lb) * sig
    k = (1.0 - lb) * (1.0 - sig)
    return q, sig, f, k


def _intra_exact(q, k, b, mask):
    rows = lax.broadcasted_iota(jnp.int32, (HGRN_CHUNK, 1), 0)
    cols = lax.broadcasted_iota(jnp.int32, (HGRN_CHUNK, HGRN_CHUNK), 1)

    def it(s, a):
        pick = rows == s
        b_s = jnp.sum(jnp.where(pick, b, 0.0), axis=0, keepdims=True)
        k_s = jnp.sum(jnp.where(pick, k, 0.0), axis=0, keepdims=True)
        col = jnp.sum(q * jnp.exp(jnp.minimum(b - b_s, 0.0)) * k_s, axis=1, keepdims=True)
        return jnp.where(cols == s, col, a)

    a = lax.fori_loop(0, HGRN_CHUNK, it, jnp.zeros((HGRN_CHUNK, HGRN_CHUNK), F32))
    return jnp.where(mask, a, 0.0)


def _intra_exact_bwd(q, k, b, da):
    rows = lax.broadcasted_iota(jnp.int32, (HGRN_CHUNK, 1), 0)
    cols = lax.broadcasted_iota(jnp.int32, (HGRN_CHUNK, HGRN_CHUNK), 1)

    def it(s, carry):
        dq, dk = carry
        pick = rows == s
        b_s = jnp.sum(jnp.where(pick, b, 0.0), axis=0, keepdims=True)
        k_s = jnp.sum(jnp.where(pick, k, 0.0), axis=0, keepdims=True)
        w = jnp.sum(jnp.where(cols == s, da, 0.0), axis=1, keepdims=True) * jnp.exp(jnp.minimum(b - b_s, 0.0))
        dq = dq + w * k_s
        dk = jnp.where(pick, jnp.sum(w * q, axis=0, keepdims=True), dk)
        return dq, dk

    z = jnp.zeros((HGRN_CHUNK, HGRN_KEY), F32)
    return lax.fori_loop(0, HGRN_CHUNK, it, (z, z))


def _chunk_cumsum(g, from_end):
    n = g.shape[0]
    row = lax.broadcasted_iota(jnp.int32, g.shape, 0)
    x, s = g, 1
    while s < n:
        if from_end:
            x = x + jnp.where(row < n - s, pltpu.roll(x, n - s, 0), 0.0)
        else:
            x = x + jnp.where(row >= s, pltpu.roll(x, s, 0), 0.0)
        s *= 2
    return x


def _head(a, h):
    return a[:, h * LANES:(h + 1) * LANES]


def _block_is_mild(zf_ref, lb, nch):
    def lowest_total():
        g = jnp.log(lb + (1.0 - lb) * _sigmoid(zf_ref[...]))
        lows = [jnp.min(jnp.sum(g[c * HGRN_CHUNK:(c + 1) * HGRN_CHUNK], axis=0, keepdims=True)) for c in range(nch)]
        return functools.reduce(jnp.minimum, lows)

    floor = jnp.min(HGRN_CHUNK * jnp.log(lb))
    return lax.cond(floor >= FACTORED_DECAY_LIMIT, lambda: floor, lowest_total) >= FACTORED_DECAY_LIMIT


def _hgrn_fwd(z, lb, rev, *, name, guest=None):
    t = z.shape[0]
    rb = min(HGRN_ROWS, t)
    nb, nch = t // rb, rb // HGRN_CHUNK
    heads = range(HGRN_HEADS)

    def blk(n):
        return nb - 1 - n if rev else n

    def body(zq_ref, zf_ref, zv_ref, lb_ref, o_ref, s_ref, st_ref):
        @pl.when(pl.program_id(0) == 0)
        def _():
            st_ref[...] = jnp.zeros_like(st_ref)
        lbv = lb_ref[...]
        cmask = _cum_mask(rev)

        def chunk(c, carry, mild):
            cc = nch - 1 - c if rev else c
            sl = pl.ds(pl.multiple_of(cc * HGRN_CHUNK, HGRN_CHUNK), HGRN_CHUNK)
            q, _, f, k = _gates(zq_ref[sl, :], zf_ref[sl, :], lbv)
            v = zv_ref[sl, :].astype(BF16)
            g = jnp.log(f)
            b = _chunk_cumsum(g, rev)
            tot = jnp.sum(g, axis=0, keepdims=True)
            qd = (q * jnp.exp(b)).astype(BF16)
            kd = (k * jnp.exp(tot - b)).astype(BF16)
            etot = jnp.exp(tot)

            def factored():
                kf = (k * jnp.exp(-b)).astype(BF16)
                return tuple(jnp.where(cmask, _dot(_head(qd, h), _head(kf, h), NT), 0.0) for h in heads)

            def exact():
                return tuple(_intra_exact(_head(q, h), _head(k, h), _head(b, h), cmask) for h in heads)

            a = factored() if mild else lax.cond(jnp.min(tot) >= FACTORED_DECAY_LIMIT, factored, exact)
            for h in heads:
                st = st_ref[h]
                st_bf = st.astype(BF16)
                s_ref[h, cc] = st_bf
                o_ref[sl, h * LANES:(h + 1) * LANES] = (
                    _dot(_head(qd, h), st_bf, NT) + _dot(a[h].astype(BF16), _head(v, h), NN))
                st_ref[h] = st * _head(etot, h) + _dot(_head(v, h), _head(kd, h), TN)
            return carry

        lax.cond(_block_is_mild(zf_ref, lbv, nch),
                 lambda: lax.fori_loop(0, nch, functools.partial(chunk, mild=True), 0, unroll=HGRN_UNROLL_FWD),
                 lambda: lax.fori_loop(0, nch, functools.partial(chunk, mild=False), 0))

    def col(j):
        return pl.BlockSpec((rb, D_MODEL), lambda n: (blk(n), j))

    return _call(
        body, grid=(nb,),
        in_specs=[col(0), col(2 if rev else 1), col(3), _full((1, D_MODEL))],
        out_specs=[col(0), pl.BlockSpec((HGRN_HEADS, nch, LANES, LANES), lambda n: (0, blk(n), 0, 0))],
        out_shape=[jax.ShapeDtypeStruct((t, D_MODEL), F32),
                   jax.ShapeDtypeStruct((HGRN_HEADS, t // HGRN_CHUNK, LANES, LANES), BF16)],
        scratch_shapes=[pltpu.VMEM((HGRN_HEADS, LANES, LANES), F32)],
        args=(z, z, z, lb), sem=("arbitrary",), name=name, guest=guest)


def _hgrn_bwd(z, lb, d_o, states, rev, *, name, other=None):
    t = z.shape[0]
    rb = min(HGRN_ROWS, t)
    nb, nch = t // rb, rb // HGRN_CHUNK
    heads = range(HGRN_HEADS)
    n_other = 0 if other is None else 2
    acc_dtype = F32 if other is None else BF16

    def blk(n):
        return n if rev else nb - 1 - n

    def body(zq_ref, zf_ref, zv_ref, lb_ref, do_ref, s_ref, *rest):
        oq_ref, ov_ref = rest[:n_other] if other is not None else (None, None)
        dq_ref, dzf_ref, dv_ref, dlb_ref, dst_ref = rest[n_other:]

        @pl.when(pl.program_id(0) == 0)
        def _():
            dst_ref[...] = jnp.zeros_like(dst_ref)
            dlb_ref[...] = jnp.zeros_like(dlb_ref)
        lbv = lb_ref[...]
        cmask = _cum_mask(rev)

        def chunk(c, carry, mild):
            cc = c if rev else nch - 1 - c
            sl = pl.ds(pl.multiple_of(cc * HGRN_CHUNK, HGRN_CHUNK), HGRN_CHUNK)
            zq = zq_ref[sl, :]
            q, sig, f, k = _gates(zq, zf_ref[sl, :], lbv)
            v = zv_ref[sl, :].astype(BF16)
            g = jnp.log(f)
            b = _chunk_cumsum(g, rev)
            tot = jnp.sum(g, axis=0, keepdims=True)
            eb = jnp.exp(b)
            etb = jnp.exp(tot - b)
            etot = jnp.exp(tot)
            qd = (q * eb).astype(BF16)
            kd = (k * etb).astype(BF16)
            do = do_ref[sl, :].astype(BF16)
            da = [jnp.where(cmask, _dot(_head(do, h), _head(v, h), NT), 0.0) for h in heads]

            def factored():
                ke = jnp.exp(-b)
                kf = (k * ke).astype(BF16)
                res = []
                for h in heads:
                    qh, kh = _head(qd, h), _head(kf, h)
                    da_bf = da[h].astype(BF16)
                    dqf, dkf = _dot(da_bf, kh, NN), _dot(da_bf, qh, TN)
                    res.append((jnp.where(cmask, _dot(qh, kh, NT), 0.0), dqf * _head(eb, h), dkf * _head(ke, h),
                                qh.astype(F32) * dqf - kh.astype(F32) * dkf))
                return tuple(res)

            def exact():
                res = []
                for h in heads:
                    qh, kh, bh = _head(q, h), _head(k, h), _head(b, h)
                    dq_i, dk_i = _intra_exact_bwd(qh, kh, bh, da[h])
                    res.append((_intra_exact(qh, kh, bh, cmask), dq_i, dk_i, qh * dq_i - kh * dk_i))
                return tuple(res)

            intra = factored() if mild else lax.cond(jnp.min(tot) >= FACTORED_DECAY_LIMIT, factored, exact)
            dq_p, dk_p, db_p, dtot_p = [], [], [], []
            for h in heads:
                a, dq_i, dk_i, db_i = intra[h]
                doh, vh, qh, kh = _head(do, h), _head(v, h), _head(q, h), _head(k, h)
                st0 = s_ref[h, cc]
                dst = dst_ref[h]
                dst_bf = dst.astype(BF16)
                dv = _dot(a.astype(BF16), doh, TN) + _dot(_head(kd, h), dst_bf, NT)
                if ov_ref is not None:
                    dv = dv + ov_ref[sl, h * LANES:(h + 1) * LANES]
                dv_ref[sl, h * LANES:(h + 1) * LANES] = dv.astype(acc_dtype)
                dq_x = _dot(doh, st0, NN) * _head(eb, h)
                dk_x = _dot(vh, dst_bf, NN) * _head(etb, h)
                dtot_p.append(jnp.sum(kh * dk_x, axis=0, keepdims=True)
                              + _head(etot, h) * jnp.sum(dst * st0.astype(F32), axis=0, keepdims=True))
                dst_ref[h] = dst * _head(etot, h) + _dot(doh, _head(qd, h), TN)
                dq_p.append(dq_i + dq_x)
                dk_p.append(dk_i + dk_x)
                db_p.append(db_i + qh * dq_x - kh * dk_x)
            dq, dk, db = (jnp.concatenate(x, axis=1) for x in (dq_p, dk_p, db_p))
            dg = _chunk_cumsum(db, not rev) + jnp.concatenate(dtot_p, axis=1)
            sq = _sigmoid(zq)
            dq_pre = dq * sq * (1.0 + zq * (1.0 - sq))
            if oq_ref is not None:
                dq_pre = dq_pre + oq_ref[sl, :]
            dq_ref[sl, :] = dq_pre.astype(acc_dtype)
            dfk = dg / f - dk
            dzf_ref[sl, :] = (dfk * (1.0 - lbv) * sig * (1.0 - sig)).astype(BF16)
            dlb_ref[...] += jnp.sum(dfk * (1.0 - sig), axis=0, keepdims=True)
            return carry

        lax.cond(_block_is_mild(zf_ref, lbv, nch),
                 lambda: lax.fori_loop(0, nch, functools.partial(chunk, mild=True), 0, unroll=HGRN_UNROLL),
                 lambda: lax.fori_loop(0, nch, functools.partial(chunk, mild=False), 0))

    def col(j):
        return pl.BlockSpec((rb, D_MODEL), lambda n: (blk(n), j))

    return pl.pallas_call(
        body, grid=(nb,),
        in_specs=[col(0), col(2 if rev else 1), col(3), _full((1, D_MODEL)), col(0),
                  pl.BlockSpec((HGRN_HEADS, nch, LANES, LANES), lambda n: (0, blk(n), 0, 0))] + [col(0)] * n_other,
        out_specs=[col(0), col(0), col(0), _full((1, D_MODEL))],
        out_shape=[jax.ShapeDtypeStruct((t, D_MODEL), acc_dtype), jax.ShapeDtypeStruct((t, D_MODEL), BF16),
                   jax.ShapeDtypeStruct((t, D_MODEL), acc_dtype), jax.ShapeDtypeStruct((1, D_MODEL), F32)],
        scratch_shapes=[pltpu.VMEM((HGRN_HEADS, LANES, LANES), F32)],
        compiler_params=_cp("arbitrary"), name=name)(z, z, z, lb, d_o, states, *(other or ()))


def _hgrn_post_fwd(o_f, o_b, z, norm_g, *, name):
    t = o_f.shape[0]
    tm = _rows(t)

    def body(of_ref, ob_ref, gate_ref, ng_ref, y_ref):
        ng = ng_ref[...]
        for h in range(HGRN_HEADS):
            sl = slice(h * LANES, (h + 1) * LANES)
            o = of_ref[:, sl] + ob_ref[:, sl]
            r = lax.rsqrt(jnp.mean(o * o, axis=1, keepdims=True) + LN_EPS)
            gt = gate_ref[:, sl]
            y_ref[:, sl] = (o * r * ng * (gt * _sigmoid(gt))).astype(BF16)

    row = pl.BlockSpec((tm, D_MODEL), lambda i: (i, 0))
    return pl.pallas_call(
        body, grid=(t // tm,),
        in_specs=[row, row, pl.BlockSpec((tm, D_MODEL), lambda i: (i, 4)), _full((1, LANES))],
        out_specs=row, out_shape=jax.ShapeDtypeStruct((t, D_MODEL), BF16),
        compiler_params=_cp("parallel"), name=name)(o_f, o_b, z, norm_g)


def _hgrn_post_bwd(dy, o_f, o_b, z, norm_g, *, name):
    t = o_f.shape[0]
    tm = _rows(t)

    def body(dy_ref, of_ref, ob_ref, gate_ref, ng_ref, do_ref, dgate_ref, dng_ref):
        ng = ng_ref[...]
        dng = jnp.zeros((1, LANES), F32)
        for h in range(HGRN_HEADS):
            sl = slice(h * LANES, (h + 1) * LANES)
            o = of_ref[:, sl] + ob_ref[:, sl]
            r = lax.rsqrt(jnp.mean(o * o, axis=1, keepdims=True) + LN_EPS)
            gt = gate_ref[:, sl]
            sg = _sigmoid(gt)
            dyv = dy_ref[:, sl].astype(F32)
            xn = o * r
            dgate_ref[:, sl] = (dyv * xn * ng * sg * (1.0 + gt * (1.0 - sg))).astype(BF16)
            dn = dyv * gt * sg
            dng = dng + jnp.sum(dn * xn, axis=0, keepdims=True)
            dxn = dn * ng
            do_ref[:, sl] = r * (dxn - xn * jnp.mean(dxn * xn, axis=1, keepdims=True))

        @pl.when(pl.program_id(0) == 0)
        def _():
            dng_ref[...] = jnp.zeros_like(dng_ref)
        dng_ref[...] += dng

    row = pl.BlockSpec((tm, D_MODEL), lambda i: (i, 0))
    return pl.pallas_call(
        body, grid=(t // tm,),
        in_specs=[row, row, row, pl.BlockSpec((tm, D_MODEL), lambda i: (i, 4)), _full((1, LANES))],
        out_specs=[row, row, _full((1, LANES))],
        out_shape=[jax.ShapeDtypeStruct((t, D_MODEL), F32), jax.ShapeDtypeStruct((t, D_MODEL), BF16),
                   jax.ShapeDtypeStruct((1, LANES), F32)],
        compiler_params=_cp("arbitrary"), name=name)(dy, o_f, o_b, z, norm_g)


def _lower_bounds(logits2d, *, name):
    def body(l_ref, lb_ref):
        l = l_ref[...]
        e = jnp.exp(l - jnp.max(l, axis=0, keepdims=True))
        sm = e / jnp.sum(e, axis=0, keepdims=True)
        s1, s2, s3 = sm[1:2], sm[2:3], sm[3:4]
        lb_ref[...] = jnp.concatenate([jnp.zeros_like(s1), s1, s1 + s2, s1 + s2 + s3], axis=0)

    return pl.pallas_call(body, out_shape=jax.ShapeDtypeStruct(logits2d.shape, F32), name=name)(logits2d)


def _lower_bounds_bwd(logits2d, dlb, *, name):
    def body(l_ref, d_ref, o_ref):
        l = l_ref[...]
        e = jnp.exp(l - jnp.max(l, axis=0, keepdims=True))
        sm = e / jnp.sum(e, axis=0, keepdims=True)
        d = d_ref[...]
        d1, d2, d3 = d[1:2], d[2:3], d[3:4]
        dsm = jnp.concatenate([jnp.zeros_like(d1), d1 + d2 + d3, d2 + d3, d3], axis=0)
        o_ref[...] = sm * (dsm - jnp.sum(sm * dsm, axis=0, keepdims=True))

    return pl.pallas_call(body, out_shape=jax.ShapeDtypeStruct(logits2d.shape, F32), name=name)(logits2d, dlb)


def _adamw(w, g, m, v, *, name):
    r, c = w.shape
    tr = r if r <= 512 else _pick_rows(r)

    def body(w_ref, g_ref, m_ref, v_ref, d_ref, nm_ref, nv_ref):
        gv = g_ref[...]
        nm = ADAM_B1 * m_ref[...] + (1.0 - ADAM_B1) * gv
        nv = ADAM_B2 * v_ref[...] + (1.0 - ADAM_B2) * (gv * gv)
        m_hat = nm / (1.0 - ADAM_B1 ** ADAM_STEP)
        v_hat = nv / (1.0 - ADAM_B2 ** ADAM_STEP)
        d_ref[...] = -ADAM_LR * (m_hat / (jnp.sqrt(v_hat) + ADAM_EPS) + ADAM_WD * w_ref[...])
        nm_ref[...] = nm
        nv_ref[...] = nv

    blk = pl.BlockSpec((tr, c), lambda i: (i, 0))
    shp = jax.ShapeDtypeStruct((r, c), F32)
    return pl.pallas_call(body, grid=(r // tr,), in_specs=[blk] * 4, out_specs=[blk] * 3, out_shape=[shp] * 3,
                          compiler_params=_cp("parallel"), name=name)(w, g, m, v)


def _pick_rows(r, cap=512):
    best = 8
    for d in range(8, cap + 1, 8):
        if r % d == 0:
            best = d
    assert r % best == 0, r
    return best


def _place():
    x, y, c = lax.axis_index("x"), lax.axis_index("y"), lax.axis_index("c")
    chips = [(1 - x, y), (x, 1 - y), (1 - x, 1 - y)]
    return x, y, c, chips


def _remote(src, dst, send_sem, recv_sem, to):
    return pltpu.make_async_remote_copy(src_ref=src, dst_ref=dst, send_sem=send_sem, recv_sem=recv_sem,
                                        device_id=to, device_id_type=MESH)


def _allreduce_small(block, *, name):
    m, n = block.shape

    def body(x_ref, sum_ref, all_ref, send_sems, recv_sems):
        x, y, c, chips = _place()
        me, sibling = (x, y, c), (x, y, 1 - c)

        def rows(px, py, pc):
            return all_ref.at[pl.ds((4 * px + 2 * py + pc) * m, m), :]

        all_ref[pl.ds((4 * x + 2 * y + c) * m, m), :] = x_ref[...]
        first = [_remote(x_ref, rows(*me), send_sems.at[0], recv_sems.at[0], sibling)]
        first += [_remote(x_ref, rows(*me), send_sems.at[1 + j], recv_sems.at[1 + j], (*chip, c)) for j, chip in enumerate(chips)]
        for cp in first:
            cp.start()
        passed = [_remote(rows(*chip, c), rows(*chip, c), send_sems.at[4 + j], recv_sems.at[4 + j], sibling)
                  for j, chip in enumerate(chips)]
        for j, chip in enumerate(chips):
            _remote(x_ref, rows(*chip, c), send_sems.at[1 + j], recv_sems.at[1 + j], me).wait_recv()
            passed[j].start()
        _remote(x_ref, rows(*sibling), send_sems.at[0], recv_sems.at[0], me).wait_recv()
        for j, chip in enumerate(chips):
            _remote(x_ref, rows(*chip, 1 - c), send_sems.at[4 + j], recv_sems.at[4 + j], me).wait_recv()
        for cp in first + passed:
            cp.wait_send()
        acc = all_ref[pl.ds(0, m), :]
        for d in range(1, 8):
            acc = acc + all_ref[pl.ds(d * m, m), :]
        sum_ref[...] = acc

    vmem = pl.BlockSpec(memory_space=pltpu.VMEM)
    return pl.pallas_call(
        body, in_specs=[vmem], out_specs=vmem, out_shape=jax.ShapeDtypeStruct((m, n), F32),
        scratch_shapes=[pltpu.VMEM((8 * m, n), F32), pltpu.SemaphoreType.DMA((7,)), pltpu.SemaphoreType.DMA((7,))],
        name=name)(block)


def _add_own_half(g, recv, c, *, name):
    n, r, w = g.shape
    hr = r // 2
    tr = _pick_rows(hr, 1024)
    nr = hr // tr

    def body(c_ref, g_ref, r_ref, o_ref):
        o_ref[...] = (g_ref[...] + r_ref[...]).astype(BF16)

    return pl.pallas_call(
        body,
        grid_spec=pltpu.PrefetchScalarGridSpec(
            num_scalar_prefetch=1, grid=(n, nr),
            in_specs=[pl.BlockSpec((None, tr, w), lambda s, i, c_ref: (s, c_ref[0] * nr + i, 0)),
                      pl.BlockSpec((None, tr, w), lambda s, i, c_ref: (s, i, 0))],
            out_specs=pl.BlockSpec((None, tr, w), lambda s, i, c_ref: (s, i, 0))),
        out_shape=jax.ShapeDtypeStruct((n, hr, w), BF16),
        compiler_params=_cp("parallel", "parallel"), name=name)(c, g, recv)


def _sum_chips(q, *, name):
    n, h, w = q.shape
    tr = _pick_rows(h, 1024)

    def body(a, b, c, d, o_ref):
        o_ref[...] = ((a[...].astype(F32) + b[...].astype(F32)) + c[...].astype(F32)) + d[...].astype(F32)

    specs = [pl.BlockSpec((None, tr, w), functools.partial(lambda i, s: (s, i, 0), s=s)) for s in range(n)]
    return pl.pallas_call(
        body, grid=(h // tr,), in_specs=specs, out_specs=pl.BlockSpec((tr, w), lambda i: (i, 0)),
        out_shape=jax.ShapeDtypeStruct((h, w), F32), compiler_params=_cp("parallel"), name=name)(q, q, q, q)


PACK_W = 1024
BIG = (("att_w_qkv", 2), ("att_w_o", 1), ("hgrn_w_in", 2), ("hgrn_w_o", 1),
       ("ffn_w_in", 2), ("ffn_w_out", 1), ("ple_w_gate", 1), ("ple_w_proj", 2))
LB_ROWS = 32


SMALL =("ln_mix_g", "ln_mix_b", "ln_ffn_g", "ln_ffn_b")
SMALL_ROWS = 32


def _pack_small(vals, lb):
    rows = [vals[n] for n in SMALL] + [lb.reshape(-1, PACK_W)]
    for n in ("att_sink", "hgrn_norm_g"):
        flat = vals[n].reshape(1, -1)
        rows.append(jnp.pad(flat, ((0, 0), (0, PACK_W - flat.shape[1]))))
    buf = jnp.concatenate(rows, axis=0)
    return jnp.pad(buf, ((0, SMALL_ROWS - buf.shape[0]), (0, 0)))


def _unpack_small(buf, lb_shape):
    out, r0 = {}, 0
    for n in SMALL:
        out[n] = buf[r0:r0 + DEPTH]
        r0 += DEPTH
    nlb = lb_shape[0] * lb_shape[1] * lb_shape[2] // PACK_W
    out["hgrn_lb_logits"] = buf[r0:r0 + nlb].reshape(lb_shape)
    r0 += nlb
    out["att_sink"] = buf[r0, :2 * N_Q_HEADS].reshape(2, N_Q_HEADS)
    out["hgrn_norm_g"] = buf[r0 + 1, :2 * LANES].reshape(2, LANES)
    return out


WEIGHTS = ("att_w_qkv", "att_sink", "att_w_o", "hgrn_w_in", "hgrn_lb_logits", "hgrn_norm_g", "hgrn_w_o", "ln_mix_g",
           "ln_mix_b", "ffn_w_in", "ffn_w_out", "ln_ffn_g", "ln_ffn_b", "ple_w_gate", "ple_w_proj")


def _gather_guest(packed):
    r, w = packed.shape
    hr = r // 2

    def copies(src_ref, out_ref, send_sems, recv_sems):
        x, y, c, chips = _place()
        sibling = (x, y, 1 - c)

        def half(cx, cy, hc):
            return out_ref.at[2 * cx + cy, pl.ds(hc * hr, hr), :]

        my_half = src_ref.at[pl.ds(c * hr, hr), :]
        first = [_remote(my_half, half(x, y, c), send_sems.at[j], recv_sems.at[j], (*chip, c)) for j, chip in enumerate(chips)]
        passed = [_remote(half(*chip, c), half(*chip, c), send_sems.at[3 + j], recv_sems.at[3 + j], sibling)
                  for j, chip in enumerate(chips)]
        from_chips = [_remote(my_half, half(*chip, c), send_sems.at[j], recv_sems.at[j], (*chip, c)) for j, chip in enumerate(chips)]
        from_sibling = [_remote(my_half, half(*chip, 1 - c), send_sems.at[3 + j], recv_sems.at[3 + j], sibling)
                        for j, chip in enumerate(chips)]
        return first, passed, from_chips, from_sibling

    def start(gi, go, gs):
        for cp in copies(gi[0], go[0], *gs)[0]:
            cp.start()

    def finish(gi, go, gs):
        first, passed, from_chips, from_sibling = copies(gi[0], go[0], *gs)
        for arrival, onward in zip(from_chips, passed):
            arrival.wait_recv()
            onward.start()
        for arrival in from_sibling:
            arrival.wait_recv()
        for cp in first + passed:
            cp.wait_send()

    return _Guest((packed,), (jax.ShapeDtypeStruct((N_CHIPS, r, w), packed.dtype),),
                  (pltpu.SemaphoreType.DMA((6,)), pltpu.SemaphoreType.DMA((6,))), start, finish)


def _pair_exchange_guest(g):
    n, r, w = g.shape
    hr = r // 2

    def copy(g_ref, out_ref, send_sem, recv_sem):
        x, y, c, _ = _place()
        return _remote(g_ref.at[:, pl.ds((1 - c) * hr, hr), :], out_ref, send_sem, recv_sem, (x, y, 1 - c))

    return _Guest((g,), (jax.ShapeDtypeStruct((n, hr, w), g.dtype),), (pltpu.SemaphoreType.DMA, pltpu.SemaphoreType.DMA),
                  lambda gi, go, gs: copy(gi[0], go[0], *gs).start(),
                  lambda gi, go, gs: copy(gi[0], go[0], *gs).wait())


def _chip_scatter_guest(part):
    n, h, w = part.shape

    def copies(p_ref, out_ref, send_sems, recv_sems):
        x, y, c, chips = _place()
        me = 2 * x + y
        sends = [_remote(p_ref.at[2 * cx + cy], out_ref.at[me], send_sems.at[j], recv_sems.at[j], (cx, cy, c))
                 for j, (cx, cy) in enumerate(chips)]
        arrivals = [_remote(p_ref.at[me], out_ref.at[2 * cx + cy], send_sems.at[j], recv_sems.at[j], (cx, cy, c))
                    for j, (cx, cy) in enumerate(chips)]
        return sends, arrivals

    def start(gi, go, gs):
        for cp in copies(gi[0], go[0], *gs)[0]:
            cp.start()

    def finish(gi, go, gs):
        sends, arrivals = copies(gi[0], go[0], *gs)
        for cp in arrivals:
            cp.wait_recv()
        for cp in sends:
            cp.wait_send()

    return _Guest((part,), (jax.ShapeDtypeStruct((n, h, w), part.dtype),),
                  (pltpu.SemaphoreType.DMA((3,)), pltpu.SemaphoreType.DMA((3,))), start, finish)


def _pair_share_guest(halves):
    n = len(halves)

    def copies(k, h_ref, out_ref, send_sems, recv_sems):
        x, y, c, _ = _place()
        send = _remote(h_ref, out_ref.at[c], send_sems.at[k], recv_sems.at[k], (x, y, 1 - c))
        arrival = _remote(h_ref, out_ref.at[1 - c], send_sems.at[k], recv_sems.at[k], (x, y, 1 - c))
        return send, arrival

    def start(gi, go, gs):
        for k in range(n):
            copies(k, gi[k], go[k], *gs)[0].start()

    def finish(gi, go, gs):
        for k in range(n):
            send, arrival = copies(k, gi[k], go[k], *gs)
            send.wait_send()
            arrival.wait_recv()

    return _Guest(tuple(halves), tuple(jax.ShapeDtypeStruct((2,) + a.shape, a.dtype) for a in halves),
                  (pltpu.SemaphoreType.DMA((n,)), pltpu.SemaphoreType.DMA((n,))), start, finish)


def _own(stack, idx):
    return lax.dynamic_index_in_dim(stack, idx, axis=0, keepdims=False)


def _put(buf, block, idx):
    return lax.dynamic_update_slice_in_dim(buf, block[None], idx, axis=0)


AXIS = dict(BIG)
SHARD_MAJOR = ("ffn_w_in",)


def _layer_parts(i):
    j = i // 2
    mixer = (("att_w_qkv", j), ("att_w_o", j)) if i % 2 == 0 else (("hgrn_w_in", j), ("hgrn_w_o", j))
    return mixer + (("ffn_w_in", i), ("ffn_w_out", i), ("ple_w_gate", i), ("ple_w_proj", i))


def _gather_groups():
    first = _layer_parts(0)
    return [first[:1], first[1:] + _layer_parts(1), _layer_parts(2), _layer_parts(3)]


def _pack_parts(shards, parts):
    return jnp.concatenate([shards[n][l].reshape(-1, PACK_W) for n, l in parts], axis=0)


def _gathered_parts(buf, parts, shapes):
    out, r0 = {}, 0
    for n, l in parts:
        r, c = shapes[n][1:]
        nr = r * c // PACK_W
        sh = buf[:, r0:r0 + nr].reshape(N_CHIPS, r, c)
        if n in SHARD_MAJOR:
            out[(n, l)] = sh
        else:
            out[(n, l)] = sh.reshape(N_CHIPS * r, c) if AXIS[n] == 1 else sh.transpose(1, 0, 2).reshape(r, N_CHIPS * c)
        r0 += nr
    return out, r0


def _unpack_layer(buf, i, shapes):
    out, r0 = {}, 0
    for n, _ in _layer_parts(i):
        r, c = shapes[n][1:]
        nr = r * c // PACK_W
        out[n] = buf[r0:r0 + nr].reshape(r, c)
        r0 += nr
    return out


def _layer_slabs(full, i, shapes):
    parts = []
    for n, _ in _layer_parts(i):
        r, c = shapes[n][1:]
        g = full[n]
        if AXIS[n] == 1 or n in SHARD_MAJOR:
            g = g.reshape(N_CHIPS, -1, PACK_W)
        else:
            g = g.reshape(r, N_CHIPS, c).transpose(1, 0, 2).reshape(N_CHIPS, -1, PACK_W)
        parts.append(g)
    return jnp.concatenate(parts, axis=1)


def kernel(x, p, att_w_qkv, att_sink, att_w_o, hgrn_w_in, hgrn_lb_logits, hgrn_norm_g, hgrn_w_o, ln_mix_g, ln_mix_b, ffn_w_in, ffn_w_out, ln_ffn_g, ln_ffn_b, ple_w_gate, ple_w_proj, loss_target, m_att_w_qkv, m_att_sink, m_att_w_o, m_hgrn_w_in, m_hgrn_lb_logits, m_hgrn_norm_g, m_hgrn_w_o, m_ln_mix_g, m_ln_mix_b, m_ffn_w_in, m_ffn_w_out, m_ln_ffn_g, m_ln_ffn_b, m_ple_w_gate, m_ple_w_proj, v_att_w_qkv, v_att_sink, v_att_w_o, v_hgrn_w_in, v_hgrn_lb_logits, v_hgrn_norm_g, v_hgrn_w_o, v_ln_mix_g, v_ln_mix_b, v_ffn_w_in, v_ffn_w_out, v_ln_ffn_g, v_ln_ffn_b, v_ple_w_gate, v_ple_w_proj):
    wts = dict(att_w_qkv=att_w_qkv, att_sink=att_sink, att_w_o=att_w_o, hgrn_w_in=hgrn_w_in, hgrn_lb_logits=hgrn_lb_logits,
               hgrn_norm_g=hgrn_norm_g, hgrn_w_o=hgrn_w_o, ln_mix_g=ln_mix_g, ln_mix_b=ln_mix_b, ffn_w_in=ffn_w_in,
               ffn_w_out=ffn_w_out, ln_ffn_g=ln_ffn_g, ln_ffn_b=ln_ffn_b, ple_w_gate=ple_w_gate, ple_w_proj=ple_w_proj)
    mom = dict(att_w_qkv=m_att_w_qkv, att_sink=m_att_sink, att_w_o=m_att_w_o, hgrn_w_in=m_hgrn_w_in, hgrn_lb_logits=m_hgrn_lb_logits,
               hgrn_norm_g=m_hgrn_norm_g, hgrn_w_o=m_hgrn_w_o, ln_mix_g=m_ln_mix_g, ln_mix_b=m_ln_mix_b, ffn_w_in=m_ffn_w_in,
               ffn_w_out=m_ffn_w_out, ln_ffn_g=m_ln_ffn_g, ln_ffn_b=m_ln_ffn_b, ple_w_gate=m_ple_w_gate, ple_w_proj=m_ple_w_proj)
    var = dict(att_w_qkv=v_att_w_qkv, att_sink=v_att_sink, att_w_o=v_att_w_o, hgrn_w_in=v_hgrn_w_in, hgrn_lb_logits=v_hgrn_lb_logits,
               hgrn_norm_g=v_hgrn_norm_g, hgrn_w_o=v_hgrn_w_o, ln_mix_g=v_ln_mix_g, ln_mix_b=v_ln_mix_b, ffn_w_in=v_ffn_w_in,
               ffn_w_out=v_ffn_w_out, ln_ffn_g=v_ln_ffn_g, ln_ffn_b=v_ln_ffn_b, ple_w_gate=v_ple_w_gate, ple_w_proj=v_ple_w_proj)
    shapes = {n: wts[n].shape for n, _ in BIG}
    chip = 2 * lax.axis_index("x") + lax.axis_index("y")
    core = lax.axis_index("c").reshape(1).astype(jnp.int32)
    xin, target = x[0], loss_target[0]
    t = xin.shape[0]
    row = lambda a, i: a[i][None]

    bf_shards = {n: wts[n].astype(BF16) for n, _ in BIG}
    lb_sh = hgrn_lb_logits.shape
    lb_bits = lax.bitcast_convert_type(hgrn_lb_logits.reshape(-1), BF16).reshape(-1, PACK_W)
    groups = _gather_groups()
    packed = [_pack_parts(bf_shards, parts) for parts in groups]
    packed[0] = jnp.concatenate([packed[0], lb_bits, jnp.zeros((LB_ROWS - lb_bits.shape[0], PACK_W), BF16)], axis=0)

    gathered = _put(_run_guest(_gather_guest(packed[0]), name="gather_first")[0], packed[0], chip)
    wfull, r_big = _gathered_parts(gathered, groups[0], shapes)
    lb_rows = gathered[:, r_big:r_big + lb_bits.shape[0]].reshape(N_CHIPS, -1, 2)
    lb_full = lax.bitcast_convert_type(lb_rows, F32).reshape(N_CHIPS, lb_sh[0], lb_sh[1], lb_sh[2])
    lb_full = lb_full.transpose(1, 2, 0, 3).reshape(lb_sh[0], lb_sh[1], N_CHIPS * lb_sh[2])
    logits2d = lb_full.reshape(DEPTH, 2 * D_MODEL)
    lb_all = _lower_bounds(logits2d, name="lb_fwd").reshape(DEPTH, 2, 1, D_MODEL)
    tables = _rope_tables(t)

    saved, weights = [], []
    xf, xb = xin, xin.astype(BF16)
    for i in range(DEPTH):
        j = i // 2
        nxt = _gather_guest(packed[i + 1]) if i + 1 < DEPTH else None
        s = dict(xin_bf=xb)
        if i % 2 == 0:
            q, k, v = _qkv_rope(xb, wfull[("att_w_qkv", j)], tables, name=f"qkv_rope_{i}")
            (o, lse), got = _attn_fwd(q, k, v, att_sink[j], name=f"attn_fwd_{i}", guest=nxt)
            s.update(q=q, k=k, v=v, o=o, lse=lse)
            mix_in = o
        else:
            z = _mm_nn(xb, wfull[("hgrn_w_in", j)], out_dtype=F32, name=f"hgrn_in_{i}")
            (o_f, s_f), _ = _hgrn_fwd(z, lb_all[i, 0], False, name=f"hgrn_scan_f_{i}")
            (o_b, s_b), _ = _hgrn_fwd(z, lb_all[i, 1], True, name=f"hgrn_scan_b_{i}")
            og = _hgrn_post_fwd(o_f, o_b, z, row(hgrn_norm_g, j), name=f"hgrn_post_{i}")
            s.update(z=z, o_f=o_f, o_b=o_b, s_f=s_f, s_b=s_b, og=og)
            mix_in = og
        in_mixer = nxt is not None and i % 2 == 0
        if in_mixer:
            wfull.update(_gathered_parts(_put(got[0], packed[i + 1], chip), groups[i + 1], shapes)[0])
        w = {n: wfull[(n, l)] for n, l in _layer_parts(i)}
        w_o = w["att_w_o" if i % 2 == 0 else "hgrn_w_o"]
        x1, x1b, xh1, rs1 = _mm_res_ln(mix_in, w_o, xf, row(ln_mix_g, i), row(ln_mix_b, i), name=f"mix_out_ln_{i}")
        (g, u, h), got = _ffn_in(x1b, w["ffn_w_in"], name=f"ffn_in_{i}", guest=None if in_mixer else nxt)
        if nxt is not None and not in_mixer:
            wfull.update(_gathered_parts(_put(got[0], packed[i + 1], chip), groups[i + 1], shapes)[0])
        x2, x2b, xh2, rs2 = _mm_res_ln(h, w["ffn_w_out"], x1, row(ln_ffn_g, i), row(ln_ffn_b, i), name=f"ffn_out_ln_{i}")
        xf, xb, gate, pp = _ple_fwd(x2, x2b, p, i, w["ple_w_gate"], w["ple_w_proj"], name=f"ple_fwd_{i}")
        s.update(x1b=x1b, xh1=xh1, rs1=rs1, g=g, u=u, h=h, x2b=x2b, xh2=xh2, rs2=rs2, gate=gate, pp=pp)
        saved.append(s)
        weights.append(w)

    loss, dx = _loss_head(xf, target, name="loss_head")
    loss = lax.psum(loss[0, 0], ("x", "y", "c"))

    gs = {n: [None] * DEPTH for n in SMALL}
    gs.update(att_sink=[None] * 2, hgrn_norm_g=[None] * 2)
    dlb = [jnp.zeros((2, D_MODEL), F32)] * DEPTH
    halves = [None] * DEPTH
    slabs = None
    for i in reversed(range(DEPTH)):
        j = i // 2
        s, w = saved[i], weights[i]
        gw = {}
        res, got = _ple_bwd(dx, s["gate"], s["pp"], w["ple_w_gate"], s["xh2"], s["rs2"], row(ln_ffn_g, i), name=f"ple_bwd_{i}",
                            guest=None if slabs is None else _pair_exchange_guest(slabs))
        du2, du2b, dpre, dpp, gs["ln_ffn_g"][i], gs["ln_ffn_b"][i] = res
        part = None if slabs is None else _add_own_half(slabs, got[0], core, name=f"grad_pair_add_{i + 1}")
        gw["ple_w_gate"] = _mm_tn(s["x2b"], dpre, name=f"dw_ple_gate_{i}")
        gw["ple_w_proj"] = _mm_tn_p(p, i, dpp, name=f"dw_ple_proj_{i}")
        dgg, dgu = _ffn_out_bwd(du2b, w["ffn_w_out"], s["g"], s["u"], name=f"ffn_out_bwd_{i}")
        gw["ffn_w_out"] = _mm_tn(s["h"], du2b, name=f"dw_ffn_out_{i}")
        res = _mm_nt([dgg, dgu], w["ffn_w_in"], resid=du2, ln=(s["xh1"], s["rs1"], row(ln_mix_g, i)),
                     name=f"ffn_in_bwd_{i}", guest=None if part is None else _chip_scatter_guest(part))
        (du1, du1b, gs["ln_mix_g"][i], gs["ln_mix_b"][i]), got = res if part is not None else (res, None)
        if part is not None:
            halves[i + 1] = _sum_chips(_put(got[0], _own(part, chip), chip), name=f"grad_chip_sum_{i + 1}")
        shard_w = shapes["ffn_w_in"][2]
        gw["ffn_w_in"] = jnp.concatenate(
            [_mm_tn(s["x1b"], dgg, name=f"dw_ffn_gate_{i}", shard_cols=shard_w),
             _mm_tn(s["x1b"], dgu, name=f"dw_ffn_up_{i}", shard_cols=shard_w)], axis=0)
        if i % 2 == 0:
            gw["att_w_o"] = _mm_tn(s["o"], du1b, name=f"dw_att_o_{i}")
            do = _mm_nt([du1b], w["att_w_o"], out_dtype=BF16, name=f"att_o_bwd_{i}")
            dq, dkw, dvw, gs["att_sink"][j] = _attn_bwd(s["q"], s["k"], s["v"], s["o"], do, s["lse"], att_sink[j], name=f"attn_bwd_{i}")
            dqkv = _attn_post_bwd(dq, dkw, dvw, tables, name=f"attn_post_bwd_{i}")
            gw["att_w_qkv"] = _mm_tn(s["xin_bf"], dqkv, name=f"dw_att_qkv_{i}")
            dx = _mm_nt([dqkv], w["att_w_qkv"], resid=du1, name=f"qkv_bwd_{i}")
        else:
            gw["hgrn_w_o"] = _mm_tn(s["og"], du1b, name=f"dw_hgrn_o_{i}")
            dy = _mm_nt([du1b], w["hgrn_w_o"], out_dtype=BF16, name=f"hgrn_o_bwd_{i}")
            d_o, dgate, gs["hgrn_norm_g"][j] = _hgrn_post_bwd(dy, s["o_f"], s["o_b"], s["z"], row(hgrn_norm_g, j), name=f"hgrn_post_bwd_{i}")
            dq_f, dzf_f, dv_f, dlb_f = _hgrn_bwd(s["z"], lb_all[i, 0], d_o, s["s_f"], False, name=f"hgrn_scan_f_bwd_{i}")
            dq, dzf_b, dv, dlb_b = _hgrn_bwd(s["z"], lb_all[i, 1], d_o, s["s_b"], True, name=f"hgrn_scan_b_bwd_{i}",
                                             other=(dq_f, dv_f))
            dlb[i] = jnp.stack([dlb_f.reshape(D_MODEL), dlb_b.reshape(D_MODEL)])
            dz = [dq, dzf_f, dzf_b, dv, dgate]
            gw["hgrn_w_in"] = jnp.concatenate(
                [_mm_tn(s["xin_bf"], part, name=f"dw_hgrn_in_{i}_{n}") for n, part in enumerate(dz)], axis=1)
            dx = _mm_nt(dz, w["hgrn_w_in"], resid=du1, name=f"hgrn_in_bwd_{i}")
        slabs = _layer_slabs(gw, i, shapes)

    from_sibling = _run_guest(_pair_exchange_guest(slabs), name="grad_pair_exchange_0")[0]
    part = _add_own_half(slabs, from_sibling, core, name="grad_pair_add_0")
    from_chips = _run_guest(_chip_scatter_guest(part), name="grad_chip_scatter_0")[0]
    halves[0] = _sum_chips(_put(from_chips, _own(part, chip), chip), name="grad_chip_sum_0")
    shared = _run_guest(_pair_share_guest(halves), name="grad_pair_share")
    reduced = [_put(buf, half, lax.axis_index("c")) for buf, half in zip(shared, halves)]

    g_layers = [_unpack_layer(reduced[i].reshape(-1, PACK_W), i, shapes) for i in range(DEPTH)]
    grads = {}
    for n, _ in BIG:
        per_layer = [g_layers[i][n] for i in range(DEPTH) if n in g_layers[i]]
        grads[n] = jnp.stack(per_layer)

    gsmall = {n: jnp.concatenate(v, axis=0) for n, v in gs.items()}
    dlogits = _lower_bounds_bwd(logits2d, jnp.stack(dlb).reshape(DEPTH, 2 * D_MODEL), name="lb_bwd").reshape(DEPTH, 2, D_MODEL)
    g_small_full = _unpack_small(_allreduce_small(_pack_small(gsmall, dlogits), name="allreduce_small"),
                                 (lb_sh[0], lb_sh[1], N_CHIPS * lb_sh[2]))
    grads.update({n: g_small_full[n] for n in SMALL + ("att_sink", "hgrn_norm_g")})
    grads["hgrn_lb_logits"] = lax.dynamic_slice_in_dim(g_small_full["hgrn_lb_logits"], chip * lb_sh[2], lb_sh[2], axis=2)

    delta, new_m, new_v = {}, {}, {}
    for n, _ in BIG:
        two_d = lambda a: a.reshape(-1, a.shape[-1])
        d, nm, nv = _adamw(two_d(wts[n]), two_d(grads[n]), two_d(mom[n]), two_d(var[n]), name=f"adamw_{n}")
        delta[n], new_m[n], new_v[n] = d.reshape(shapes[n]), nm.reshape(shapes[n]), nv.reshape(shapes[n])
    packs = [_pack_small(src, src["hgrn_lb_logits"]) for src in (wts, grads, mom, var)]
    outs = _adamw(*packs, name="adamw_small")
    for dst, buf in zip((delta, new_m, new_v), outs):
        dst.update(_unpack_small(buf, lb_sh))

    return (loss, dx[None], *[grads[n] for n in WEIGHTS], *[delta[n] for n in WEIGHTS],
            *[new_m[n] for n in WEIGHTS], *[new_v[n] for n in WEIGHTS])
```

```python
import functools
from typing import Callable, NamedTuple

import jax
import jax.numpy as jnp
from jax import lax
from jax.experimental import pallas as pl
from jax.experimental.pallas import tpu as pltpu

F32, BF16 = jnp.float32, jnp.bfloat16

D_MODEL = 1024
DEPTH = 4
HEAD_DIM = 64
N_Q_HEADS = 16
N_KV_HEADS = 4
GROUP = 4
Q_DIM = 1024
KV_DIM = 256
WINDOW = 128
ROPE_DIM = 16
ROPE_THETA = 500000.0
HGRN_HEADS = 8
HGRN_KEY = 128
HGRN_CHUNK = 64
D_FF = 2816
PLE_DIM = 256
ALPHA = (2 * DEPTH) ** 0.25
LN_EPS = 1e-5
ADAM_LR, ADAM_B1, ADAM_B2, ADAM_EPS, ADAM_WD, ADAM_STEP = 0.001, 0.9, 0.999, 1e-08, 0.01, 10

LANES = 128
VMEM_LIMIT_BYTES = 56 * 2**20
FACTORED_DECAY_LIMIT = -80.0

NN = ((1,), (0,))
NT = ((1,), (1,))
TN = ((0,), (0,))

N_CHIPS = 4
MESH = pl.DeviceIdType.MESH


def _dot(a, b, dims):
    return lax.dot_general(a, b, (dims, ((), ())), preferred_element_type=F32)


def _cp(*sem):
    return pltpu.CompilerParams(dimension_semantics=sem, vmem_limit_bytes=VMEM_LIMIT_BYTES)


def _rows(t, cap=512):
    return min(cap, t)


def _pick(n, cap):
    best = None
    for d in range(LANES, min(n, cap) + 1, LANES):
        if n % d == 0:
            best = d
    assert best is not None, (n, cap)
    return best


def _sigmoid(x):
    return jax.nn.sigmoid(x)


def _ln_fwd(u, g, b):
    mu = jnp.mean(u, axis=-1, keepdims=True)
    xc = u - mu
    var = jnp.mean(xc * xc, axis=-1, keepdims=True)
    rstd = lax.rsqrt(var + LN_EPS)
    xhat = xc * rstd
    return xhat * g + b, xhat, rstd


def _ln_bwd(dy, xhat, rstd, g):
    dxh = dy * g
    m1 = jnp.mean(dxh, axis=-1, keepdims=True)
    m2 = jnp.mean(dxh * xhat, axis=-1, keepdims=True)
    du = rstd * (dxh - m1 - xhat * m2)
    return du, jnp.sum(dy * xhat, axis=0, keepdims=True), jnp.sum(dy, axis=0, keepdims=True)


def _full(shape):
    nd = len(shape)
    return pl.BlockSpec(shape, lambda *_: (0,) * nd)


ANY = pl.BlockSpec(memory_space=pl.ANY)


class _Guest(NamedTuple):
    args: tuple
    out_shape: tuple
    sems: tuple
    start: Callable
    finish: Callable


def _call(body, *, grid, in_specs, out_specs, out_shape, args, sem, name, scratch_shapes=(), guest=None):
    n_in, n_out, n_scr = len(args), len(out_shape), len(scratch_shapes)
    if guest is None:
        res = pl.pallas_call(body, grid=grid, in_specs=list(in_specs), out_specs=list(out_specs), out_shape=list(out_shape),
                             scratch_shapes=list(scratch_shapes), compiler_params=_cp(*sem), name=name)(*args)
        return list(res), []
    g_in, g_out = len(guest.args), len(guest.out_shape)

    def hosted(*refs):
        cuts = [n_in, g_in, n_out, g_out, n_scr]
        parts, pos = [], 0
        for n in cuts:
            parts.append(refs[pos:pos + n])
            pos += n
        h_in, gi, h_out, go, h_scr = parts
        gs = refs[pos:]
        ids = [pl.program_id(d) for d in range(len(grid))]
        first = functools.reduce(jnp.logical_and, [i == 0 for i in ids])
        last = functools.reduce(jnp.logical_and, [i == n - 1 for i, n in zip(ids, grid)])

        @pl.when(first)
        def _():
            guest.start(gi, go, gs)
        body(*h_in, *h_out, *h_scr)

        @pl.when(last)
        def _():
            guest.finish(gi, go, gs)

    res = pl.pallas_call(
        hosted, grid=grid, in_specs=list(in_specs) + [ANY] * g_in, out_specs=list(out_specs) + [ANY] * g_out,
        out_shape=list(out_shape) + list(guest.out_shape), scratch_shapes=list(scratch_shapes) + list(guest.sems),
        compiler_params=_cp(*(("arbitrary",) * len(grid))), name=name)(*args, *guest.args)
    return list(res[:n_out]), list(res[n_out:])


def _run_guest(guest, *, name):
    g_in = len(guest.args)

    def body(*refs):
        gi, go, gs = refs[:g_in], refs[g_in:g_in + len(guest.out_shape)], refs[g_in + len(guest.out_shape):]
        guest.start(gi, go, gs)
        guest.finish(gi, go, gs)

    return pl.pallas_call(body, in_specs=[ANY] * g_in, out_specs=[ANY] * len(guest.out_shape), out_shape=list(guest.out_shape),
                          scratch_shapes=list(guest.sems), name=name)(*guest.args)


def _mm_nn(a, w, *, out_dtype, name):
    t, k = a.shape
    n = w.shape[1]
    tm, tn = _rows(t), _pick(n, 1408)

    def body(a_ref, w_ref, o_ref):
        o_ref[...] = _dot(a_ref[...], w_ref[...], NN).astype(out_dtype)

    return pl.pallas_call(
        body, grid=(n // tn, t // tm),
        in_specs=[pl.BlockSpec((tm, k), lambda j, i: (i, 0)), pl.BlockSpec((k, tn), lambda j, i: (0, j))],
        out_specs=pl.BlockSpec((tm, tn), lambda j, i: (i, j)),
        out_shape=jax.ShapeDtypeStruct((t, n), out_dtype),
        compiler_params=_cp("parallel", "parallel"), name=name)(a, w)


def _mm_tn(a, b, *, name, guest=None, shard_cols=None):
    r, m = a.shape
    n = b.shape[1]
    tr, tm, tn = _rows(r), _pick(m, 1408), _pick(n, 2816)
    c = tn if shard_cols is None else shard_cols
    per = tn // c

    def body(a_ref, b_ref, o_ref):
        @pl.when(pl.program_id(2) == 0)
        def _():
            o_ref[...] = jnp.zeros_like(o_ref)
        res = _dot(a_ref[...].astype(BF16), b_ref[...].astype(BF16), TN)
        if shard_cols is None:
            o_ref[...] += res
        else:
            for s in range(per):
                o_ref[s] += res[:, s * c:(s + 1) * c]

    if shard_cols is None:
        out_spec, out_shape = pl.BlockSpec((tm, tn), lambda i, j, k: (i, j)), jax.ShapeDtypeStruct((m, n), F32)
    else:
        out_spec, out_shape = pl.BlockSpec((per, tm, c), lambda i, j, k: (j, i, 0)), jax.ShapeDtypeStruct((n // c, m, c), F32)
    res, extra = _call(
        body, grid=(m // tm, n // tn, r // tr),
        in_specs=[pl.BlockSpec((tr, tm), lambda i, j, k: (k, i)), pl.BlockSpec((tr, tn), lambda i, j, k: (k, j))],
        out_specs=[out_spec], out_shape=[out_shape], args=(a, b),
        sem=("parallel", "parallel", "arbitrary"), name=name, guest=guest)
    return res[0] if guest is None else (res[0], extra)


def _mm_tn_p(p, layer, b, *, name):
    t = p.shape[2]
    n = b.shape[1]
    tr = _rows(t)

    def body(a_ref, b_ref, o_ref):
        @pl.when(pl.program_id(0) == 0)
        def _():
            o_ref[...] = jnp.zeros_like(o_ref)
        o_ref[...] += _dot(a_ref[...].astype(BF16), b_ref[...], TN)

    return pl.pallas_call(
        body, grid=(t // tr,),
        in_specs=[pl.BlockSpec((None, None, tr, PLE_DIM), lambda k: (layer, 0, k, 0)),
                  pl.BlockSpec((tr, n), lambda k: (k, 0))],
        out_specs=pl.BlockSpec((PLE_DIM, n), lambda k: (0, 0)),
        out_shape=jax.ShapeDtypeStruct((PLE_DIM, n), F32),
        compiler_params=_cp("arbitrary"), name=name)(p, b)


def _mm_res_ln(a, w, resid, g, b, *, name):
    t, k = a.shape
    tm = _rows(t)

    def body(a_ref, w_ref, r_ref, g_ref, b_ref, y_ref, ybf_ref, xh_ref, rs_ref):
        acc = _dot(a_ref[...], w_ref[...], NN)
        y, xh, rs = _ln_fwd(ALPHA * r_ref[...] + acc, g_ref[...], b_ref[...])
        y_ref[...] = y
        ybf_ref[...] = y.astype(BF16)
        xh_ref[...] = xh.astype(BF16)
        rs_ref[...] = rs

    row = pl.BlockSpec((tm, D_MODEL), lambda i: (i, 0))
    return pl.pallas_call(
        body, grid=(t // tm,),
        in_specs=[pl.BlockSpec((tm, k), lambda i: (i, 0)), _full((k, D_MODEL)), row, _full((1, D_MODEL)), _full((1, D_MODEL))],
        out_specs=[row, row, row, pl.BlockSpec((tm, 1), lambda i: (i, 0))],
        out_shape=[jax.ShapeDtypeStruct((t, D_MODEL), F32), jax.ShapeDtypeStruct((t, D_MODEL), BF16),
                   jax.ShapeDtypeStruct((t, D_MODEL), BF16), jax.ShapeDtypeStruct((t, 1), F32)],
        compiler_params=_cp("parallel"), name=name)(a, w, resid, g, b)


def _ffn_in(x_bf, w, *, name, guest=None):
    t = x_bf.shape[0]
    tm, tn = _rows(t), _pick(D_FF, 1408)
    nb = D_FF // tn
    assert w.shape == (2 * nb, D_MODEL, tn), w.shape

    def body(x_ref, wg_ref, wu_ref, dg_ref, du_ref, h_ref):
        x = x_ref[...]
        g = _dot(x, wg_ref[...], NN)
        u = _dot(x, wu_ref[...], NN)
        sig = _sigmoid(g)
        silu = g * sig
        dg_ref[...] = (u * sig * (1.0 + g * (1.0 - sig))).astype(BF16)
        du_ref[...] = silu.astype(BF16)
        h_ref[...] = (silu * u).astype(BF16)

    tile = pl.BlockSpec((tm, tn), lambda j, i: (i, j))
    shp = jax.ShapeDtypeStruct((t, D_FF), BF16)
    return _call(
        body, grid=(nb, t // tm),
        in_specs=[pl.BlockSpec((tm, D_MODEL), lambda j, i: (i, 0)),
                  pl.BlockSpec((None, D_MODEL, tn), lambda j, i: (j, 0, 0)),
                  pl.BlockSpec((None, D_MODEL, tn), lambda j, i: (j + nb, 0, 0))],
        out_specs=[tile, tile, tile], out_shape=[shp, shp, shp], args=(x_bf, w, w),
        sem=("parallel", "parallel"), name=name, guest=guest)


def _ple_fwd(x, x_bf, p, layer, wg, wp, *, name):
    t = x.shape[0]
    tm = _rows(t)

    def body(x_ref, xbf_ref, p_ref, wg_ref, wp_ref, y_ref, ybf_ref, gate_ref, pp_ref):
        gate = _sigmoid(_dot(xbf_ref[...], wg_ref[...], NN))
        pp = _dot(p_ref[...].astype(BF16), wp_ref[...], NN)
        y = x_ref[...] + gate * pp
        y_ref[...] = y
        ybf_ref[...] = y.astype(BF16)
        gate_ref[...] = gate.astype(BF16)
        pp_ref[...] = pp.astype(BF16)

    row = pl.BlockSpec((tm, D_MODEL), lambda i: (i, 0))
    f32, bf = jax.ShapeDtypeStruct((t, D_MODEL), F32), jax.ShapeDtypeStruct((t, D_MODEL), BF16)
    return pl.pallas_call(
        body, grid=(t // tm,),
        in_specs=[row, row, pl.BlockSpec((None, None, tm, PLE_DIM), lambda i: (layer, 0, i, 0)),
                  _full((D_MODEL, D_MODEL)), _full((PLE_DIM, D_MODEL))],
        out_specs=[row, row, row, row], out_shape=[f32, bf, bf, bf],
        compiler_params=_cp("parallel"), name=name)(x, x_bf, p, wg, wp)


def _loss_head(y, target, *, name):
    t = y.shape[0]
    tm = _rows(t)

    def body(y_ref, t_ref, loss_ref, dy_ref):
        e = y_ref[...] - t_ref[...]

        @pl.when(pl.program_id(0) == 0)
        def _():
            loss_ref[...] = jnp.zeros_like(loss_ref)
        sq = jnp.sum(jnp.sum(e * e, axis=1, keepdims=True), axis=0, keepdims=True)
        loss_ref[...] += sq * (0.5 / D_MODEL)
        dy_ref[...] = e * (1.0 / D_MODEL)

    row = pl.BlockSpec((tm, D_MODEL), lambda i: (i, 0))
    return pl.pallas_call(
        body, grid=(t // tm,), in_specs=[row, row],
        out_specs=[_full((1, 1)), row],
        out_shape=[jax.ShapeDtypeStruct((1, 1), F32), jax.ShapeDtypeStruct((t, D_MODEL), F32)],
        compiler_params=_cp("arbitrary"), name=name)(y, target)


def _ple_bwd(dx3, gate, pp, wg, xhat, rstd, g, *, name, guest=None):
    t = dx3.shape[0]
    tm = _rows(t)

    def body(dx_ref, gate_ref, pp_ref, wg_ref, xh_ref, rs_ref, g_ref,
             du_ref, dubf_ref, dpre_ref, dpp_ref, dg_ref, db_ref):
        dx = dx_ref[...]
        gate = gate_ref[...].astype(F32)
        dpre = (dx * pp_ref[...].astype(F32) * gate * (1.0 - gate)).astype(BF16)
        dpre_ref[...] = dpre
        dpp_ref[...] = (dx * gate).astype(BF16)
        dx2 = dx + _dot(dpre, wg_ref[...], NT)
        du, dg, db = _ln_bwd(dx2, xh_ref[...].astype(F32), rs_ref[...], g_ref[...])
        du_ref[...] = du
        dubf_ref[...] = du.astype(BF16)

        @pl.when(pl.program_id(0) == 0)
        def _():
            dg_ref[...] = jnp.zeros_like(dg_ref)
            db_ref[...] = jnp.zeros_like(db_ref)
        dg_ref[...] += dg
        db_ref[...] += db

    row = pl.BlockSpec((tm, D_MODEL), lambda i: (i, 0))
    vec = _full((1, D_MODEL))
    f32, bf = jax.ShapeDtypeStruct((t, D_MODEL), F32), jax.ShapeDtypeStruct((t, D_MODEL), BF16)
    v32 = jax.ShapeDtypeStruct((1, D_MODEL), F32)
    res, extra = _call(
        body, grid=(t // tm,),
        in_specs=[row, row, row, _full((D_MODEL, D_MODEL)), row, pl.BlockSpec((tm, 1), lambda i: (i, 0)), vec],
        out_specs=[row, row, row, row, vec, vec], out_shape=[f32, bf, bf, bf, v32, v32],
        args=(dx3, gate, pp, wg, xhat, rstd, g), sem=("arbitrary",), name=name, guest=guest)
    return res, extra


def _ffn_out_bwd(du_bf, w_out, g, u, *, name):
    t = du_bf.shape[0]
    tm, tn = _rows(t), _pick(D_FF, 1408)

    def body(du_ref, w_ref, hg_ref, hu_ref, dg_ref, dup_ref):
        dh = _dot(du_ref[...], w_ref[...], NT)
        dg_ref[...] = (dh * hg_ref[...].astype(F32)).astype(BF16)
        dup_ref[...] = (dh * hu_ref[...].astype(F32)).astype(BF16)

    tile = pl.BlockSpec((tm, tn), lambda j, i: (i, j))
    shp = jax.ShapeDtypeStruct((t, D_FF), BF16)
    return pl.pallas_call(
        body, grid=(D_FF // tn, t // tm),
        in_specs=[pl.BlockSpec((tm, D_MODEL), lambda j, i: (i, 0)), pl.BlockSpec((tn, D_MODEL), lambda j, i: (j, 0)),
                  tile, tile],
        out_specs=[tile, tile], out_shape=[shp, shp],
        compiler_params=_cp("parallel", "parallel"), name=name)(du_bf, w_out, g, u)


def _mm_nt(parts, w, *, resid=None, ln=None, out_dtype=F32, name, guest=None):
    t, kp = parts[0].shape
    npart = len(parts)
    tm = _rows(t)
    n_in = npart + 1 + (resid is not None) + (3 if ln is not None else 0)

    def body(*refs):
        a_refs, w_ref = refs[:npart], refs[npart]
        pos = npart + 1
        r_ref = None
        if resid is not None:
            r_ref = refs[pos]
            pos += 1
        if ln is not None:
            xh_ref, rs_ref, g_ref = refs[pos:pos + 3]
        outs = refs[n_in:]
        if w.ndim == 2:
            terms = [_dot(a_refs[pi][...], w_ref[:, pi * kp:(pi + 1) * kp], NT) for pi in range(npart)]
        else:
            c, per = w.shape[2], w.shape[0] // npart
            terms = [_dot(a_refs[pi][:, s * c:(s + 1) * c], w_ref[pi * per + s], NT) for pi in range(npart) for s in range(per)]
        val = functools.reduce(lambda u, v: u + v, terms)
        if r_ref is not None:
            val = val + ALPHA * r_ref[...]
        if ln is None:
            outs[0][...] = val.astype(out_dtype)
        else:
            du, dg, db = _ln_bwd(val, xh_ref[...].astype(F32), rs_ref[...], g_ref[...])
            outs[0][...] = du
            outs[1][...] = du.astype(BF16)

            @pl.when(pl.program_id(0) == 0)
            def _():
                outs[2][...] = jnp.zeros_like(outs[2])
                outs[3][...] = jnp.zeros_like(outs[3])
            outs[2][...] += dg
            outs[3][...] += db

    row = pl.BlockSpec((tm, D_MODEL), lambda i: (i, 0))
    vec = pl.BlockSpec((1, D_MODEL), lambda i: (0, 0))
    in_specs = [pl.BlockSpec((tm, kp), lambda i: (i, 0)) for _ in range(npart)]
    in_specs.append(pl.BlockSpec(w.shape, lambda i: (0,) * w.ndim, pipeline_mode=pl.Buffered(1)))
    args = list(parts) + [w]
    if resid is not None:
        in_specs.append(row)
        args.append(resid)
    if ln is not None:
        in_specs += [row, pl.BlockSpec((tm, 1), lambda i: (i, 0)), vec]
        args += list(ln)
        out_specs = [row, row, vec, vec]
        out_shape = [jax.ShapeDtypeStruct((t, D_MODEL), F32), jax.ShapeDtypeStruct((t, D_MODEL), BF16),
                     jax.ShapeDtypeStruct((1, D_MODEL), F32), jax.ShapeDtypeStruct((1, D_MODEL), F32)]
    else:
        out_specs = [row]
        out_shape = [jax.ShapeDtypeStruct((t, D_MODEL), out_dtype)]
    res, extra = _call(
        body, grid=(t // tm,), in_specs=in_specs, out_specs=out_specs, out_shape=out_shape, args=tuple(args),
        sem=("arbitrary" if ln is not None else "parallel",), name=name, guest=guest)
    res = res if ln is not None else res[0]
    return res if guest is None else (res, extra)


def _rope_tables(t):
    half = ROPE_DIM // 2
    inv = ROPE_THETA ** (-jnp.arange(0, ROPE_DIM, 2, dtype=F32) / ROPE_DIM)
    ang = jnp.arange(t, dtype=F32)[:, None] * inv[None, :]
    cos, sin = jnp.cos(ang), jnp.sin(ang)
    pad = HEAD_DIM - ROPE_DIM
    c = jnp.concatenate([cos, cos, jnp.ones((t, pad), F32)], axis=1)
    s_hi = jnp.concatenate([jnp.zeros((t, half), F32), sin, jnp.zeros((t, pad), F32)], axis=1)
    s_lo = jnp.concatenate([-sin, jnp.zeros((t, half + pad), F32)], axis=1)
    rep = LANES // HEAD_DIM
    return jnp.tile(c, (1, rep)), jnp.tile(s_hi, (1, rep)), jnp.tile(s_lo, (1, rep))


def _rope(x, c, s_hi, s_lo, sign):
    half = ROPE_DIM // 2
    parts = []
    for j in range(x.shape[1] // LANES):
        xg = x[:, j * LANES:(j + 1) * LANES]
        rot = pltpu.roll(xg, half, 1) * s_hi + pltpu.roll(xg, LANES - half, 1) * s_lo
        parts.append(xg * c + sign * rot)
    return jnp.concatenate(parts, axis=1)


def _qkv_rope(x_bf, w, tables, *, name):
    t = x_bf.shape[0]
    tm = _rows(t)
    n = Q_DIM + 2 * KV_DIM

    def body(x_ref, w_ref, c_ref, sh_ref, sl_ref, q_ref, k_ref, v_ref):
        acc = _dot(x_ref[...], w_ref[...], NN)
        c, sh, sl = c_ref[...], sh_ref[...], sl_ref[...]
        q_ref[...] = (_rope(acc[:, :Q_DIM], c, sh, sl, 1.0) * HEAD_DIM ** -0.5).astype(BF16)
        k_ref[...] = _rope(acc[:, Q_DIM:Q_DIM + KV_DIM], c, sh, sl, 1.0).astype(BF16)
        v_ref[...] = acc[:, Q_DIM + KV_DIM:].astype(BF16)

    tab = pl.BlockSpec((tm, LANES), lambda i: (i, 0))
    return pl.pallas_call(
        body, grid=(t // tm,),
        in_specs=[pl.BlockSpec((tm, D_MODEL), lambda i: (i, 0)), _full((D_MODEL, n)), tab, tab, tab],
        out_specs=[pl.BlockSpec((tm, Q_DIM), lambda i: (i, 0)), pl.BlockSpec((tm, KV_DIM), lambda i: (i, 0)),
                   pl.BlockSpec((tm, KV_DIM), lambda i: (i, 0))],
        out_shape=[jax.ShapeDtypeStruct((t, Q_DIM), BF16), jax.ShapeDtypeStruct((t, KV_DIM), BF16),
                   jax.ShapeDtypeStruct((t, KV_DIM), BF16)],
        compiler_params=_cp("parallel"), name=name)(x_bf, w, *tables)


ATT_ROWS = 256


def _window_specs(t, tq):
    r = tq // WINDOW
    last = t // WINDOW - 1
    return [pl.BlockSpec((WINDOW, KV_DIM), lambda i: (jnp.maximum(i * r - 1, 0), 0)),
            pl.BlockSpec((tq, KV_DIM), lambda i: (i, 0)),
            pl.BlockSpec((WINDOW, KV_DIM), lambda i: (jnp.minimum((i + 1) * r, last), 0))]


def _att_valid(base, t):
    rows = GROUP * WINDOW
    qpos = base + (lax.broadcasted_iota(jnp.int32, (rows, 3 * WINDOW), 0) & (WINDOW - 1))
    kpos = base - WINDOW + lax.broadcasted_iota(jnp.int32, (rows, 3 * WINDOW), 1)
    return (jnp.abs(qpos - kpos) <= WINDOW) & (kpos >= 0) & (kpos < t)


def _stack_heads(x, r0, g):
    return jnp.concatenate(
        [x[r0:r0 + WINDOW, (GROUP * g + h) * HEAD_DIM:(GROUP * g + h + 1) * HEAD_DIM] for h in range(GROUP)], axis=0)


def _sink_column(sink_ref, g):
    return jnp.concatenate([jnp.full((WINDOW, 1), sink_ref[GROUP * g + h], F32) for h in range(GROUP)], axis=0)


def _unstack_heads(pieces):
    return jnp.concatenate([pieces[g][h * WINDOW:(h + 1) * WINDOW] for g in range(N_KV_HEADS) for h in range(GROUP)], axis=1)


def _attn_fwd(q, k, v, sink, *, name, guest=None):
    t = q.shape[0]
    tq = min(ATT_ROWS, t)
    nsb = tq // WINDOW

    def body(sink_ref, q_ref, kp_ref, kc_ref, kn_ref, vp_ref, vc_ref, vn_ref, o_ref, l_ref):
        i = pl.program_id(0)
        qv = q_ref[...]
        kw = jnp.concatenate([kp_ref[...], kc_ref[...], kn_ref[...]], axis=0)
        vw = jnp.concatenate([vp_ref[...], vc_ref[...], vn_ref[...]], axis=0)
        ones = jnp.ones((3 * WINDOW, HEAD_DIM), BF16)
        for sb in range(nsb):
            r0 = sb * WINDOW
            bias =jnp.where(_att_valid(i * tq + r0, t)[:WINDOW], 0.0, -1e30)
            pieces = []
            for hh in range(N_Q_HEADS):
                g, h = hh // GROUP, hh % GROUP
                qh = qv[r0:r0 + WINDOW, hh * HEAD_DIM:(hh + 1) * HEAD_DIM]
                kg = kw[r0:r0 + 3 * WINDOW, g * HEAD_DIM:(g + 1) * HEAD_DIM]
                vg = jnp.concatenate([vw[r0:r0 + 3 * WINDOW, g * HEAD_DIM:(g + 1) * HEAD_DIM], ones], axis=1)
                s = _dot(qh, kg, NT) + bias
                snk = sink_ref[hh]
                m = jnp.maximum(jnp.max(s, axis=1, keepdims=True), snk)
                ov = _dot(jnp.exp(s - m).astype(BF16), vg, NN)
                den = ov[:, HEAD_DIM:HEAD_DIM + 1] + jnp.exp(snk - m)
                pieces.append(ov[:, :HEAD_DIM] * (1.0 / den))
                l_ref[g, sb, h * WINDOW:(h + 1) * WINDOW, :] = m + jnp.log(den)
            o_ref[r0:r0 + WINDOW, :] = jnp.concatenate(pieces, axis=1).astype(BF16)

    return _call(
        body, grid=(t // tq,),
        in_specs=[pl.BlockSpec(memory_space=pltpu.SMEM), pl.BlockSpec((tq, Q_DIM), lambda i: (i, 0))]
        + _window_specs(t, tq) + _window_specs(t, tq),
        out_specs=[pl.BlockSpec((tq, Q_DIM), lambda i: (i, 0)),
                   pl.BlockSpec((N_KV_HEADS, nsb, GROUP * WINDOW, 1), lambda i: (0, i, 0, 0))],
        out_shape=[jax.ShapeDtypeStruct((t, Q_DIM), BF16),
                   jax.ShapeDtypeStruct((N_KV_HEADS, t // WINDOW, GROUP * WINDOW, 1), F32)],
        args=(sink, q, k, k, k, v, v, v), sem=("parallel",), name=name, guest=guest)


def _attn_bwd(q, k, v, o, do, lse, sink, *, name):
    t = q.shape[0]
    tq = min(ATT_ROWS, t)
    nsb = tq // WINDOW

    def body(sink_ref, q_ref, o_ref, do_ref, l_ref, kp_ref, kc_ref, kn_ref, vp_ref, vc_ref, vn_ref,
             dq_ref, dkw_ref, dvw_ref, dsink_ref):
        i = pl.program_id(0)
        qv, ov, dov = q_ref[...], o_ref[...], do_ref[...]
        kw = jnp.concatenate([kp_ref[...], kc_ref[...], kn_ref[...]], axis=0)
        vw = jnp.concatenate([vp_ref[...], vc_ref[...], vn_ref[...]], axis=0)
        lane16 = lax.broadcasted_iota(jnp.int32, (1, N_Q_HEADS), 1)
        dsink = jnp.zeros((1, N_Q_HEADS), F32)
        for sb in range(nsb):
            r0 = sb * WINDOW
            valid = _att_valid(i * tq + r0, t)
            dq_p, dk_p, dv_p = [], [], []
            for g in range(N_KV_HEADS):
                qs, dos = _stack_heads(qv, r0, g), _stack_heads(dov, r0, g)
                delta = jnp.sum(dos.astype(F32) * _stack_heads(ov, r0, g).astype(F32), axis=1, keepdims=True)
                kg = kw[r0:r0 + 3 * WINDOW, g * HEAD_DIM:(g + 1) * HEAD_DIM]
                vg = vw[r0:r0 + 3 * WINDOW, g * HEAD_DIM:(g + 1) * HEAD_DIM]
                lse_col = l_ref[g, sb]
                pr = jnp.where(valid, jnp.exp(_dot(qs, kg, NT) - lse_col), 0.0)
                ds = (pr * (_dot(dos, vg, NT) - delta)).astype(BF16)
                dq_p.append(_dot(ds, kg, NN))
                dk_p.append(_dot(ds, qs, TN))
                dv_p.append(_dot(pr.astype(BF16), dos, TN))
                ps = jnp.exp(_sink_column(sink_ref, g) - lse_col) * delta
                for h in range(GROUP):
                    tot = jnp.sum(ps[h * WINDOW:(h + 1) * WINDOW], axis=0, keepdims=True)
                    dsink = dsink - jnp.where(lane16 == GROUP * g + h, tot, 0.0)
            dq_ref[r0:r0 + WINDOW, :] = _unstack_heads(dq_p).astype(BF16)
            dkw_ref[sb] = jnp.concatenate(dk_p, axis=1)
            dvw_ref[sb] = jnp.concatenate(dv_p, axis=1)

        @pl.when(i == 0)
        def _():
            dsink_ref[...] = jnp.zeros_like(dsink_ref)
        dsink_ref[...] += dsink

    rowq = pl.BlockSpec((tq, Q_DIM), lambda i: (i, 0))
    win = pl.BlockSpec((nsb, 3 * WINDOW, KV_DIM), lambda i: (i, 0, 0))
    wshape = jax.ShapeDtypeStruct((t // WINDOW, 3 * WINDOW, KV_DIM), F32)
    return pl.pallas_call(
        body, grid=(t // tq,),
        in_specs=[pl.BlockSpec(memory_space=pltpu.SMEM), rowq, rowq, rowq,
                  pl.BlockSpec((N_KV_HEADS, nsb, GROUP * WINDOW, 1), lambda i: (0, i, 0, 0))]
        + _window_specs(t, tq) + _window_specs(t, tq),
        out_specs=[rowq, win, win, _full((1, N_Q_HEADS))],
        out_shape=[jax.ShapeDtypeStruct((t, Q_DIM), BF16), wshape, wshape, jax.ShapeDtypeStruct((1, N_Q_HEADS), F32)],
        compiler_params=_cp("arbitrary"), name=name)(sink, q, o, do, lse, k, k, k, v, v, v)


def _attn_post_bwd(dq, dkw, dvw, tables, *, name):
    t = dq.shape[0]
    nb = t // WINDOW
    n = Q_DIM + 2 * KV_DIM

    def body(dq_ref, ka_ref, kb_ref, kc_ref, va_ref, vb_ref, vc_ref, c_ref, sh_ref, sl_ref, o_ref):
        j = pl.program_id(0)
        lo = (j > 0).astype(F32)
        hi = (j < nb - 1).astype(F32)
        c, sh, sl = c_ref[...], sh_ref[...], sl_ref[...]
        dk = ka_ref[...] * hi + kb_ref[...] + kc_ref[...] * lo
        dv = va_ref[...] * hi + vb_ref[...] + vc_ref[...] * lo
        o_ref[:, :Q_DIM] = (_rope(dq_ref[...].astype(F32), c, sh, sl, -1.0) * HEAD_DIM ** -0.5).astype(BF16)
        o_ref[:, Q_DIM:Q_DIM + KV_DIM] = _rope(dk, c, sh, sl, -1.0).astype(BF16)
        o_ref[:, Q_DIM + KV_DIM:] = dv.astype(BF16)

    wins = [pl.BlockSpec((None, WINDOW, KV_DIM), lambda j: (jnp.minimum(j + 1, nb - 1), 0, 0)),
            pl.BlockSpec((None, WINDOW, KV_DIM), lambda j: (j, 1, 0)),
            pl.BlockSpec((None, WINDOW, KV_DIM), lambda j: (jnp.maximum(j - 1, 0), 2, 0))]
    tab = pl.BlockSpec((WINDOW, LANES), lambda j: (j, 0))
    return pl.pallas_call(
        body, grid=(nb,),
        in_specs=[pl.BlockSpec((WINDOW, Q_DIM), lambda j: (j, 0))] + wins + wins + [tab, tab, tab],
        out_specs=pl.BlockSpec((WINDOW, n), lambda j: (j, 0)),
        out_shape=jax.ShapeDtypeStruct((t, n), BF16),
        compiler_params=_cp("parallel"), name=name)(dq, dkw, dkw, dkw, dvw, dvw, dvw, *tables)


HGRN_ROWS = 512
HGRN_UNROLL = 2
HGRN_UNROLL_FWD = 4


def _cum_mask(rev):
    row = lax.broadcasted_iota(jnp.int32, (HGRN_CHUNK, HGRN_CHUNK), 0)
    col = lax.broadcasted_iota(jnp.int32, (HGRN_CHUNK, HGRN_CHUNK), 1)
    return (row <= col) if rev else (row >= col)


def _gates(zq, zf, lb):
    q = zq * _sigmoid(zq)
    sig = _sigmoid(zf)
    f = lb + (1.0 - lb) * sig
    k = (1.0 - lb) * (1.0 - sig)
    return q, sig, f, k


def _intra_exact(q, k, b, mask):
    rows = lax.broadcasted_iota(jnp.int32, (HGRN_CHUNK, 1), 0)
    cols = lax.broadcasted_iota(jnp.int32, (HGRN_CHUNK, HGRN_CHUNK), 1)

    def it(s, a):
        pick = rows == s
        b_s = jnp.sum(jnp.where(pick, b, 0.0), axis=0, keepdims=True)
        k_s = jnp.sum(jnp.where(pick, k, 0.0), axis=0, keepdims=True)
        col = jnp.sum(q * jnp.exp(jnp.minimum(b - b_s, 0.0)) * k_s, axis=1, keepdims=True)
        return jnp.where(cols == s, col, a)

    a = lax.fori_loop(0, HGRN_CHUNK, it, jnp.zeros((HGRN_CHUNK, HGRN_CHUNK), F32))
    return jnp.where(mask, a, 0.0)


def _intra_exact_bwd(q, k, b, da):
    rows = lax.broadcasted_iota(jnp.int32, (HGRN_CHUNK, 1), 0)
    cols = lax.broadcasted_iota(jnp.int32, (HGRN_CHUNK, HGRN_CHUNK), 1)

    def it(s, carry):
        dq, dk = carry
        pick = rows == s
        b_s = jnp.sum(jnp.where(pick, b, 0.0), axis=0, keepdims=True)
        k_s = jnp.sum(jnp.where(pick, k, 0.0), axis=0, keepdims=True)
        w = jnp.sum(jnp.where(cols == s, da, 0.0), axis=1, keepdims=True) * jnp.exp(jnp.minimum(b - b_s, 0.0))
        dq = dq + w * k_s
        dk = jnp.where(pick, jnp.sum(w * q, axis=0, keepdims=True), dk)
        return dq, dk

    z = jnp.zeros((HGRN_CHUNK, HGRN_KEY), F32)
    return lax.fori_loop(0, HGRN_CHUNK, it, (z, z))


def _chunk_cumsum(g, from_end):
    n = g.shape[0]
    row = lax.broadcasted_iota(jnp.int32, g.shape, 0)
    x, s = g, 1
    while s < n:
        if from_end:
            x = x + jnp.where(row < n - s, pltpu.roll(x, n - s, 0), 0.0)
        else:
            x = x + jnp.where(row >= s, pltpu.roll(x, s, 0), 0.0)
        s *= 2
    return x


def _head(a, h):
    return a[:, h * LANES:(h + 1) * LANES]


def _block_is_mild(zf_ref, lb, nch):
    def lowest_total():
        g = jnp.log(lb + (1.0 - lb) * _sigmoid(zf_ref[...]))
        lows = [jnp.min(jnp.sum(g[c * HGRN_CHUNK:(c + 1) * HGRN_CHUNK], axis=0, keepdims=True)) for c in range(nch)]
        return functools.reduce(jnp.minimum, lows)

    floor = jnp.min(HGRN_CHUNK * jnp.log(lb))
    return lax.cond(floor >= FACTORED_DECAY_LIMIT, lambda: floor, lowest_total) >= FACTORED_DECAY_LIMIT


def _hgrn_fwd(z, lb, rev, *, name, guest=None):
    t = z.shape[0]
    rb = min(HGRN_ROWS, t)
    nb, nch = t // rb, rb // HGRN_CHUNK
    heads = range(HGRN_HEADS)

    def blk(n):
        return nb - 1 - n if rev else n

    def body(zq_ref, zf_ref, zv_ref, lb_ref, o_ref, s_ref, st_ref):
        @pl.when(pl.program_id(0) == 0)
        def _():
            st_ref[...] = jnp.zeros_like(st_ref)
        lbv = lb_ref[...]
        cmask = _cum_mask(rev)

        def chunk(c, carry, mild):
            cc = nch - 1 - c if rev else c
            sl = pl.ds(pl.multiple_of(cc * HGRN_CHUNK, HGRN_CHUNK), HGRN_CHUNK)
            q, _, f, k = _gates(zq_ref[sl, :], zf_ref[sl, :], lbv)
            v = zv_ref[sl, :].astype(BF16)
            g = jnp.log(f)
            b = _chunk_cumsum(g, rev)
            tot = jnp.sum(g, axis=0, keepdims=True)
            qd = (q * jnp.exp(b)).astype(BF16)
            kd = (k * jnp.exp(tot - b)).astype(BF16)
            etot = jnp.exp(tot)

            def factored():
                kf = (k * jnp.exp(-b)).astype(BF16)
                return tuple(jnp.where(cmask, _dot(_head(qd, h), _head(kf, h), NT), 0.0) for h in heads)

            def exact():
                return tuple(_intra_exact(_head(q, h), _head(k, h), _head(b, h), cmask) for h in heads)

            a = factored() if mild else lax.cond(jnp.min(tot) >= FACTORED_DECAY_LIMIT, factored, exact)
            for h in heads:
                st = st_ref[h]
                st_bf = st.astype(BF16)
                s_ref[h, cc] = st_bf
                o_ref[sl, h * LANES:(h + 1) * LANES] = (
                    _dot(_head(qd, h), st_bf, NT) + _dot(a[h].astype(BF16), _head(v, h), NN))
                st_ref[h] = st * _head(etot, h) + _dot(_head(v, h), _head(kd, h), TN)
            return carry

        lax.cond(_block_is_mild(zf_ref, lbv, nch),
                 lambda: lax.fori_loop(0, nch, functools.partial(chunk, mild=True), 0, unroll=HGRN_UNROLL_FWD),
                 lambda: lax.fori_loop(0, nch, functools.partial(chunk, mild=False), 0))

    def col(j):
        return pl.BlockSpec((rb, D_MODEL), lambda n: (blk(n), j))

    return _call(
        body, grid=(nb,),
        in_specs=[col(0), col(2 if rev else 1), col(3), _full((1, D_MODEL))],
        out_specs=[col(0), pl.BlockSpec((HGRN_HEADS, nch, LANES, LANES), lambda n: (0, blk(n), 0, 0))],
        out_shape=[jax.ShapeDtypeStruct((t, D_MODEL), F32),
                   jax.ShapeDtypeStruct((HGRN_HEADS, t // HGRN_CHUNK, LANES, LANES), BF16)],
        scratch_shapes=[pltpu.VMEM((HGRN_HEADS, LANES, LANES), F32)],
        args=(z, z, z, lb), sem=("arbitrary",), name=name, guest=guest)


def _hgrn_bwd(z, lb, d_o, states, rev, *, name, other=None):
    t = z.shape[0]
    rb = min(HGRN_ROWS, t)
    nb, nch = t // rb, rb // HGRN_CHUNK
    heads = range(HGRN_HEADS)
    n_other = 0 if other is None else 2
    acc_dtype = BF16

    def blk(n):
        return n if rev else nb - 1 - n

    def body(zq_ref, zf_ref, zv_ref, lb_ref, do_ref, s_ref, *rest):
        oq_ref, ov_ref = rest[:n_other] if other is not None else (None, None)
        dq_ref, dzf_ref, dv_ref, dlb_ref, dst_ref = rest[n_other:]

        @pl.when(pl.program_id(0) == 0)
        def _():
            dst_ref[...] = jnp.zeros_like(dst_ref)
            dlb_ref[...] = jnp.zeros_like(dlb_ref)
        lbv = lb_ref[...]
        cmask = _cum_mask(rev)

        def chunk(c, carry, mild):
            cc = c if rev else nch - 1 - c
            sl = pl.ds(pl.multiple_of(cc * HGRN_CHUNK, HGRN_CHUNK), HGRN_CHUNK)
            zq = zq_ref[sl, :]
            q, sig, f, k = _gates(zq, zf_ref[sl, :], lbv)
            v = zv_ref[sl, :].astype(BF16)
            g = jnp.log(f)
            b = _chunk_cumsum(g, rev)
            tot = jnp.sum(g, axis=0, keepdims=True)
            eb = jnp.exp(b)
            etb = jnp.exp(tot - b)
            etot = jnp.exp(tot)
            qd = (q * eb).astype(BF16)
            kd = (k * etb).astype(BF16)
            do = do_ref[sl, :].astype(BF16)
            da = [jnp.where(cmask, _dot(_head(do, h), _head(v, h), NT), 0.0) for h in heads]

            def factored():
                ke = jnp.exp(-b)
                kf = (k * ke).astype(BF16)
                res = []
                for h in heads:
                    qh, kh = _head(qd, h), _head(kf, h)
                    da_bf = da[h].astype(BF16)
                    dqf, dkf = _dot(da_bf, kh, NN), _dot(da_bf, qh, TN)
                    res.append((jnp.where(cmask, _dot(qh, kh, NT), 0.0), dqf * _head(eb, h), dkf * _head(ke, h),
                                qh.astype(F32) * dqf - kh.astype(F32) * dkf))
                return tuple(res)

            def exact():
                res = []
                for h in heads:
                    qh, kh, bh = _head(q, h), _head(k, h), _head(b, h)
                    dq_i, dk_i = _intra_exact_bwd(qh, kh, bh, da[h])
                    res.append((_intra_exact(qh, kh, bh, cmask), dq_i, dk_i, qh * dq_i - kh * dk_i))
                return tuple(res)

            intra = factored() if mild else lax.cond(jnp.min(tot) >= FACTORED_DECAY_LIMIT, factored, exact)
            dq_p, dk_p, db_p, dtot_p = [], [], [], []
            for h in heads:
                a, dq_i, dk_i, db_i = intra[h]
                doh, vh, qh, kh = _head(do, h), _head(v, h), _head(q, h), _head(k, h)
                st0 = s_ref[h, cc]
                dst = dst_ref[h]
                dst_bf = dst.astype(BF16)
                dv = _dot(a.astype(BF16), doh, TN) + _dot(_head(kd, h), dst_bf, NT)
                if ov_ref is not None:
                    dv = dv + ov_ref[sl, h * LANES:(h + 1) * LANES]
                dv_ref[sl, h * LANES:(h + 1) * LANES] = dv.astype(acc_dtype)
                dq_x = _dot(doh, st0, NN) * _head(eb, h)
                dk_x = _dot(vh, dst_bf, NN) * _head(etb, h)
                dtot_p.append(jnp.sum(kh * dk_x, axis=0, keepdims=True)
                              + _head(etot, h) * jnp.sum(dst * st0.astype(F32), axis=0, keepdims=True))
                dst_ref[h] = dst * _head(etot, h) + _dot(doh, _head(qd, h), TN)
                dq_p.append(dq_i + dq_x)
                dk_p.append(dk_i + dk_x)
                db_p.append(db_i + qh * dq_x - kh * dk_x)
            dq, dk, db = (jnp.concatenate(x, axis=1) for x in (dq_p, dk_p, db_p))
            dg = _chunk_cumsum(db, not rev) + jnp.concatenate(dtot_p, axis=1)
            sq = _sigmoid(zq)
            dq_pre = dq * sq * (1.0 + zq * (1.0 - sq))
            if oq_ref is not None:
                dq_pre = dq_pre + oq_ref[sl, :]
            dq_ref[sl, :] = dq_pre.astype(acc_dtype)
            dfk = dg / f - dk
            dzf_ref[sl, :] = (dfk * (1.0 - lbv) * sig * (1.0 - sig)).astype(BF16)
            dlb_ref[...] += jnp.sum(dfk * (1.0 - sig), axis=0, keepdims=True)
            return carry

        lax.cond(_block_is_mild(zf_ref, lbv, nch),
                 lambda: lax.fori_loop(0, nch, functools.partial(chunk, mild=True), 0, unroll=HGRN_UNROLL),
                 lambda: lax.fori_loop(0, nch, functools.partial(chunk, mild=False), 0))

    def col(j):
        return pl.BlockSpec((rb, D_MODEL), lambda n: (blk(n), j))

    return pl.pallas_call(
        body, grid=(nb,),
        in_specs=[col(0), col(2 if rev else 1), col(3), _full((1, D_MODEL)), col(0),
                  pl.BlockSpec((HGRN_HEADS, nch, LANES, LANES), lambda n: (0, blk(n), 0, 0))] + [col(0)] * n_other,
        out_specs=[col(0), col(0), col(0), _full((1, D_MODEL))],
        out_shape=[jax.ShapeDtypeStruct((t, D_MODEL), acc_dtype), jax.ShapeDtypeStruct((t, D_MODEL), BF16),
                   jax.ShapeDtypeStruct((t, D_MODEL), acc_dtype), jax.ShapeDtypeStruct((1, D_MODEL), F32)],
        scratch_shapes=[pltpu.VMEM((HGRN_HEADS, LANES, LANES), F32)],
        compiler_params=_cp("arbitrary"), name=name)(z, z, z, lb, d_o, states, *(other or ()))


def _hgrn_post_fwd(o_f, o_b, z, norm_g, *, name):
    t = o_f.shape[0]
    tm = _rows(t)

    def body(of_ref, ob_ref, gate_ref, ng_ref, y_ref):
        ng = ng_ref[...]
        for h in range(HGRN_HEADS):
            sl = slice(h * LANES, (h + 1) * LANES)
            o = of_ref[:, sl] + ob_ref[:, sl]
            r = lax.rsqrt(jnp.mean(o * o, axis=1, keepdims=True) + LN_EPS)
            gt = gate_ref[:, sl]
            y_ref[:, sl] = (o * r * ng * (gt * _sigmoid(gt))).astype(BF16)

    row = pl.BlockSpec((tm, D_MODEL), lambda i: (i, 0))
    return pl.pallas_call(
        body, grid=(t // tm,),
        in_specs=[row, row, pl.BlockSpec((tm, D_MODEL), lambda i: (i, 4)), _full((1, LANES))],
        out_specs=row, out_shape=jax.ShapeDtypeStruct((t, D_MODEL), BF16),
        compiler_params=_cp("parallel"), name=name)(o_f, o_b, z, norm_g)


def _hgrn_post_bwd(dy, o_f, o_b, z, norm_g, *, name):
    t = o_f.shape[0]
    tm = _rows(t)

    def body(dy_ref, of_ref, ob_ref, gate_ref, ng_ref, do_ref, dgate_ref, dng_ref):
        ng = ng_ref[...]
        dng = jnp.zeros((1, LANES), F32)
        for h in range(HGRN_HEADS):
            sl = slice(h * LANES, (h + 1) * LANES)
            o = of_ref[:, sl] + ob_ref[:, sl]
            r = lax.rsqrt(jnp.mean(o * o, axis=1, keepdims=True) + LN_EPS)
            gt = gate_ref[:, sl]
            sg = _sigmoid(gt)
            dyv = dy_ref[:, sl].astype(F32)
            xn = o * r
            dgate_ref[:, sl] = (dyv * xn * ng * sg * (1.0 + gt * (1.0 - sg))).astype(BF16)
            dn = dyv * gt * sg
            dng = dng + jnp.sum(dn * xn, axis=0, keepdims=True)
            dxn = dn * ng
            do_ref[:, sl] = r * (dxn - xn * jnp.mean(dxn * xn, axis=1, keepdims=True))

        @pl.when(pl.program_id(0) == 0)
        def _():
            dng_ref[...] = jnp.zeros_like(dng_ref)
        dng_ref[...] += dng

    row = pl.BlockSpec((tm, D_MODEL), lambda i: (i, 0))
    return pl.pallas_call(
        body, grid=(t // tm,),
        in_specs=[row, row, row, pl.BlockSpec((tm, D_MODEL), lambda i: (i, 4)), _full((1, LANES))],
        out_specs=[row, row, _full((1, LANES))],
        out_shape=[jax.ShapeDtypeStruct((t, D_MODEL), F32), jax.ShapeDtypeStruct((t, D_MODEL), BF16),
                   jax.ShapeDtypeStruct((1, LANES), F32)],
        compiler_params=_cp("arbitrary"), name=name)(dy, o_f, o_b, z, norm_g)


def _lower_bounds(logits2d, *, name):
    def body(l_ref, lb_ref):
        l = l_ref[...]
        e = jnp.exp(l - jnp.max(l, axis=0, keepdims=True))
        sm = e / jnp.sum(e, axis=0, keepdims=True)
        s1, s2, s3 = sm[1:2], sm[2:3], sm[3:4]
        lb_ref[...] = jnp.concatenate([jnp.zeros_like(s1), s1, s1 + s2, s1 + s2 + s3], axis=0)

    return pl.pallas_call(body, out_shape=jax.ShapeDtypeStruct(logits2d.shape, F32), name=name)(logits2d)


def _lower_bounds_bwd(logits2d, dlb, *, name):
    def body(l_ref, d_ref, o_ref):
        l = l_ref[...]
        e = jnp.exp(l - jnp.max(l, axis=0, keepdims=True))
        sm = e / jnp.sum(e, axis=0, keepdims=True)
        d = d_ref[...]
        d1, d2, d3 = d[1:2], d[2:3], d[3:4]
        dsm = jnp.concatenate([jnp.zeros_like(d1), d1 + d2 + d3, d2 + d3, d3], axis=0)
        o_ref[...] = sm * (dsm - jnp.sum(sm * dsm, axis=0, keepdims=True))

    return pl.pallas_call(body, out_shape=jax.ShapeDtypeStruct(logits2d.shape, F32), name=name)(logits2d, dlb)


def _adamw(w, g, m, v, *, name):
    r, c = w.shape
    tr = r if r <= 512 else _pick_rows(r)

    def body(w_ref, g_ref, m_ref, v_ref, d_ref, nm_ref, nv_ref):
        gv = g_ref[...]
        nm = ADAM_B1 * m_ref[...] + (1.0 - ADAM_B1) * gv
        nv = ADAM_B2 * v_ref[...] + (1.0 - ADAM_B2) * (gv * gv)
        m_hat = nm / (1.0 - ADAM_B1 ** ADAM_STEP)
        v_hat = nv / (1.0 - ADAM_B2 ** ADAM_STEP)
        d_ref[...] = -ADAM_LR * (m_hat / (jnp.sqrt(v_hat) + ADAM_EPS) + ADAM_WD * w_ref[...])
        nm_ref[...] = nm
        nv_ref[...] = nv

    blk = pl.BlockSpec((tr, c), lambda i: (i, 0))
    shp = jax.ShapeDtypeStruct((r, c), F32)
    return pl.pallas_call(body, grid=(r // tr,), in_specs=[blk] * 4, out_specs=[blk] * 3, out_shape=[shp] * 3,
                          compiler_params=_cp("parallel"), name=name)(w, g, m, v)


def _pick_rows(r, cap=512):
    best = 8
    for d in range(8, cap + 1, 8):
        if r % d == 0:
            best = d
    assert r % best == 0, r
    return best


def _place():
    x, y, c = lax.axis_index("x"), lax.axis_index("y"), lax.axis_index("c")
    chips = [(1 - x, y), (x, 1 - y), (1 - x, 1 - y)]
    return x, y, c, chips


def _remote(src, dst, send_sem, recv_sem, to):
    return pltpu.make_async_remote_copy(src_ref=src, dst_ref=dst, send_sem=send_sem, recv_sem=recv_sem,
                                        device_id=to, device_id_type=MESH)


def _allreduce_small(block, *, name):
    m, n = block.shape

    def body(x_ref, sum_ref, all_ref, send_sems, recv_sems):
        x, y, c, chips = _place()
        me, sibling = (x, y, c), (x, y, 1 - c)

        def rows(px, py, pc):
            return all_ref.at[pl.ds((4 * px + 2 * py + pc) * m, m), :]

        all_ref[pl.ds((4 * x + 2 * y + c) * m, m), :] = x_ref[...]
        first = [_remote(x_ref, rows(*me), send_sems.at[0], recv_sems.at[0], sibling)]
        first += [_remote(x_ref, rows(*me), send_sems.at[1 + j], recv_sems.at[1 + j], (*chip, c)) for j, chip in enumerate(chips)]
        for cp in first:
            cp.start()
        passed = [_remote(rows(*chip, c), rows(*chip, c), send_sems.at[4 + j], recv_sems.at[4 + j], sibling)
                  for j, chip in enumerate(chips)]
        for j, chip in enumerate(chips):
            _remote(x_ref, rows(*chip, c), send_sems.at[1 + j], recv_sems.at[1 + j], me).wait_recv()
            passed[j].start()
        _remote(x_ref, rows(*sibling), send_sems.at[0], recv_sems.at[0], me).wait_recv()
        for j, chip in enumerate(chips):
            _remote(x_ref, rows(*chip, 1 - c), send_sems.at[4 + j], recv_sems.at[4 + j], me).wait_recv()
        for cp in first + passed:
            cp.wait_send()
        acc = all_ref[pl.ds(0, m), :]
        for d in range(1, 8):
            acc = acc + all_ref[pl.ds(d * m, m), :]
        sum_ref[...] = acc

    vmem = pl.BlockSpec(memory_space=pltpu.VMEM)
    return pl.pallas_call(
        body, in_specs=[vmem], out_specs=vmem, out_shape=jax.ShapeDtypeStruct((m, n), F32),
        scratch_shapes=[pltpu.VMEM((8 * m, n), F32), pltpu.SemaphoreType.DMA((7,)), pltpu.SemaphoreType.DMA((7,))],
        name=name)(block)


def _add_own_half(g, recv, c, *, name):
    n, r, w = g.shape
    hr = r // 2
    tr = _pick_rows(hr, 1024)
    nr = hr // tr

    def body(c_ref, g_ref, r_ref, o_ref):
        o_ref[...] = (g_ref[...] + r_ref[...]).astype(BF16)

    return pl.pallas_call(
        body,
        grid_spec=pltpu.PrefetchScalarGridSpec(
            num_scalar_prefetch=1, grid=(n, nr),
            in_specs=[pl.BlockSpec((None, tr, w), lambda s, i, c_ref: (s, c_ref[0] * nr + i, 0)),
                      pl.BlockSpec((None, tr, w), lambda s, i, c_ref: (s, i, 0))],
            out_specs=pl.BlockSpec((None, tr, w), lambda s, i, c_ref: (s, i, 0))),
        out_shape=jax.ShapeDtypeStruct((n, hr, w), BF16),
        compiler_params=_cp("parallel", "parallel"), name=name)(c, g, recv)


def _sum_chips(q, *, name):
    n, h, w = q.shape
    tr = _pick_rows(h, 1024)

    def body(a, b, c, d, o_ref):
        o_ref[...] = ((a[...].astype(F32) + b[...].astype(F32)) + c[...].astype(F32)) + d[...].astype(F32)

    specs = [pl.BlockSpec((None, tr, w), functools.partial(lambda i, s: (s, i, 0), s=s)) for s in range(n)]
    return pl.pallas_call(
        body, grid=(h // tr,), in_specs=specs, out_specs=pl.BlockSpec((tr, w), lambda i: (i, 0)),
        out_shape=jax.ShapeDtypeStruct((h, w), F32), compiler_params=_cp("parallel"), name=name)(q, q, q, q)


PACK_W = 1024
BIG = (("att_w_qkv", 2), ("att_w_o", 1), ("hgrn_w_in", 2), ("hgrn_w_o", 1),
       ("ffn_w_in", 2), ("ffn_w_out", 1), ("ple_w_gate", 1), ("ple_w_proj", 2))
LB_ROWS = 32


SMALL =("ln_mix_g", "ln_mix_b", "ln_ffn_g", "ln_ffn_b")
SMALL_ROWS = 32


def _pack_small(vals, lb):
    rows = [vals[n] for n in SMALL] + [lb.reshape(-1, PACK_W)]
    for n in ("att_sink", "hgrn_norm_g"):
        flat = vals[n].reshape(1, -1)
        rows.append(jnp.pad(flat, ((0, 0), (0, PACK_W - flat.shape[1]))))
    buf = jnp.concatenate(rows, axis=0)
    return jnp.pad(buf, ((0, SMALL_ROWS - buf.shape[0]), (0, 0)))


def _unpack_small(buf, lb_shape):
    out, r0 = {}, 0
    for n in SMALL:
        out[n] = buf[r0:r0 + DEPTH]
        r0 += DEPTH
    nlb = lb_shape[0] * lb_shape[1] * lb_shape[2] // PACK_W
    out["hgrn_lb_logits"] = buf[r0:r0 + nlb].reshape(lb_shape)
    r0 += nlb
    out["att_sink"] = buf[r0, :2 * N_Q_HEADS].reshape(2, N_Q_HEADS)
    out["hgrn_norm_g"] = buf[r0 + 1, :2 * LANES].reshape(2, LANES)
    return out


WEIGHTS = ("att_w_qkv", "att_sink", "att_w_o", "hgrn_w_in", "hgrn_lb_logits", "hgrn_norm_g", "hgrn_w_o", "ln_mix_g",
           "ln_mix_b", "ffn_w_in", "ffn_w_out", "ln_ffn_g", "ln_ffn_b", "ple_w_gate", "ple_w_proj")


def _gather_guest(packed):
    r, w = packed.shape
    hr = r // 2

    def copies(src_ref, out_ref, send_sems, recv_sems):
        x, y, c, chips = _place()
        sibling = (x, y, 1 - c)

        def half(cx, cy, hc):
            return out_ref.at[2 * cx + cy, pl.ds(hc * hr, hr), :]

        my_half = src_ref.at[pl.ds(c * hr, hr), :]
        first = [_remote(my_half, half(x, y, c), send_sems.at[j], recv_sems.at[j], (*chip, c)) for j, chip in enumerate(chips)]
        passed = [_remote(half(*chip, c), half(*chip, c), send_sems.at[3 + j], recv_sems.at[3 + j], sibling)
                  for j, chip in enumerate(chips)]
        from_chips = [_remote(my_half, half(*chip, c), send_sems.at[j], recv_sems.at[j], (*chip, c)) for j, chip in enumerate(chips)]
        from_sibling = [_remote(my_half, half(*chip, 1 - c), send_sems.at[3 + j], recv_sems.at[3 + j], sibling)
                        for j, chip in enumerate(chips)]
        return first, passed, from_chips, from_sibling

    def start(gi, go, gs):
        for cp in copies(gi[0], go[0], *gs)[0]:
            cp.start()

    def finish(gi, go, gs):
        first, passed, from_chips, from_sibling = copies(gi[0], go[0], *gs)
        for arrival, onward in zip(from_chips, passed):
            arrival.wait_recv()
            onward.start()
        for arrival in from_sibling:
            arrival.wait_recv()
        for cp in first + passed:
            cp.wait_send()

    return _Guest((packed,), (jax.ShapeDtypeStruct((N_CHIPS, r, w), packed.dtype),),
                  (pltpu.SemaphoreType.DMA((6,)), pltpu.SemaphoreType.DMA((6,))), start, finish)


def _pair_exchange_guest(g):
    n, r, w = g.shape
    hr = r // 2

    def copy(g_ref, out_ref, send_sem, recv_sem):
        x, y, c, _ = _place()
        return _remote(g_ref.at[:, pl.ds((1 - c) * hr, hr), :], out_ref, send_sem, recv_sem, (x, y, 1 - c))

    return _Guest((g,), (jax.ShapeDtypeStruct((n, hr, w), g.dtype),), (pltpu.SemaphoreType.DMA, pltpu.SemaphoreType.DMA),
                  lambda gi, go, gs: copy(gi[0], go[0], *gs).start(),
                  lambda gi, go, gs: copy(gi[0], go[0], *gs).wait())


def _chip_scatter_guest(part):
    n, h, w = part.shape

    def copies(p_ref, out_ref, send_sems, recv_sems):
        x, y, c, chips = _place()
        me = 2 * x + y
        sends = [_remote(p_ref.at[2 * cx + cy], out_ref.at[me], send_sems.at[j], recv_sems.at[j], (cx, cy, c))
                 for j, (cx, cy) in enumerate(chips)]
        arrivals = [_remote(p_ref.at[me], out_ref.at[2 * cx + cy], send_sems.at[j], recv_sems.at[j], (cx, cy, c))
                    for j, (cx, cy) in enumerate(chips)]
        return sends, arrivals

    def start(gi, go, gs):
        for cp in copies(gi[0], go[0], *gs)[0]:
            cp.start()

    def finish(gi, go, gs):
        sends, arrivals = copies(gi[0], go[0], *gs)
        for cp in arrivals:
            cp.wait_recv()
        for cp in sends:
            cp.wait_send()

    return _Guest((part,), (jax.ShapeDtypeStruct((n, h, w), part.dtype),),
                  (pltpu.SemaphoreType.DMA((3,)), pltpu.SemaphoreType.DMA((3,))), start, finish)


def _pair_share_guest(halves):
    n = len(halves)

    def copies(k, h_ref, out_ref, send_sems, recv_sems):
        x, y, c, _ = _place()
        send = _remote(h_ref, out_ref.at[c], send_sems.at[k], recv_sems.at[k], (x, y, 1 - c))
        arrival = _remote(h_ref, out_ref.at[1 - c], send_sems.at[k], recv_sems.at[k], (x, y, 1 - c))
        return send, arrival

    def start(gi, go, gs):
        for k in range(n):
            copies(k, gi[k], go[k], *gs)[0].start()

    def finish(gi, go, gs):
        for k in range(n):
            send, arrival = copies(k, gi[k], go[k], *gs)
            send.wait_send()
            arrival.wait_recv()

    return _Guest(tuple(halves), tuple(jax.ShapeDtypeStruct((2,) + a.shape, a.dtype) for a in halves),
                  (pltpu.SemaphoreType.DMA((n,)), pltpu.SemaphoreType.DMA((n,))), start, finish)


def _own(stack, idx):
    return lax.dynamic_index_in_dim(stack, idx, axis=0, keepdims=False)


def _put(buf, block, idx):
    return lax.dynamic_update_slice_in_dim(buf, block[None], idx, axis=0)


AXIS = dict(BIG)
SHARD_MAJOR = ("ffn_w_in",)


def _layer_parts(i):
    j = i // 2
    mixer = (("att_w_qkv", j), ("att_w_o", j)) if i % 2 == 0 else (("hgrn_w_in", j), ("hgrn_w_o", j))
    return mixer + (("ffn_w_in", i), ("ffn_w_out", i), ("ple_w_gate", i), ("ple_w_proj", i))


def _gather_groups():
    first = _layer_parts(0)
    return [first[:1], first[1:] + _layer_parts(1), _layer_parts(2), _layer_parts(3)]


def _pack_parts(shards, parts):
    return jnp.concatenate([shards[n][l].reshape(-1, PACK_W) for n, l in parts], axis=0)


def _gathered_parts(buf, parts, shapes):
    out, r0 = {}, 0
    for n, l in parts:
        r, c = shapes[n][1:]
        nr = r * c // PACK_W
        sh = buf[:, r0:r0 + nr].reshape(N_CHIPS, r, c)
        if n in SHARD_MAJOR:
            out[(n, l)] = sh
        else:
            out[(n, l)] = sh.reshape(N_CHIPS * r, c) if AXIS[n] == 1 else sh.transpose(1, 0, 2).reshape(r, N_CHIPS * c)
        r0 += nr
    return out, r0


def _unpack_layer(buf, i, shapes):
    out, r0 = {}, 0
    for n, _ in _layer_parts(i):
        r, c = shapes[n][1:]
        nr = r * c // PACK_W
        out[n] = buf[r0:r0 + nr].reshape(r, c)
        r0 += nr
    return out


def _layer_slabs(full, i, shapes):
    parts = []
    for n, _ in _layer_parts(i):
        r, c = shapes[n][1:]
        g = full[n]
        if AXIS[n] == 1 or n in SHARD_MAJOR:
            g = g.reshape(N_CHIPS, -1, PACK_W)
        else:
            g = g.reshape(r, N_CHIPS, c).transpose(1, 0, 2).reshape(N_CHIPS, -1, PACK_W)
        parts.append(g)
    return jnp.concatenate(parts, axis=1)


def kernel(x, p, att_w_qkv, att_sink, att_w_o, hgrn_w_in, hgrn_lb_logits, hgrn_norm_g, hgrn_w_o, ln_mix_g, ln_mix_b, ffn_w_in, ffn_w_out, ln_ffn_g, ln_ffn_b, ple_w_gate, ple_w_proj, loss_target, m_att_w_qkv, m_att_sink, m_att_w_o, m_hgrn_w_in, m_hgrn_lb_logits, m_hgrn_norm_g, m_hgrn_w_o, m_ln_mix_g, m_ln_mix_b, m_ffn_w_in, m_ffn_w_out, m_ln_ffn_g, m_ln_ffn_b, m_ple_w_gate, m_ple_w_proj, v_att_w_qkv, v_att_sink, v_att_w_o, v_hgrn_w_in, v_hgrn_lb_logits, v_hgrn_norm_g, v_hgrn_w_o, v_ln_mix_g, v_ln_mix_b, v_ffn_w_in, v_ffn_w_out, v_ln_ffn_g, v_ln_ffn_b, v_ple_w_gate, v_ple_w_proj):
    wts = dict(att_w_qkv=att_w_qkv, att_sink=att_sink, att_w_o=att_w_o, hgrn_w_in=hgrn_w_in, hgrn_lb_logits=hgrn_lb_logits,
               hgrn_norm_g=hgrn_norm_g, hgrn_w_o=hgrn_w_o, ln_mix_g=ln_mix_g, ln_mix_b=ln_mix_b, ffn_w_in=ffn_w_in,
               ffn_w_out=ffn_w_out, ln_ffn_g=ln_ffn_g, ln_ffn_b=ln_ffn_b, ple_w_gate=ple_w_gate, ple_w_proj=ple_w_proj)
    mom = dict(att_w_qkv=m_att_w_qkv, att_sink=m_att_sink, att_w_o=m_att_w_o, hgrn_w_in=m_hgrn_w_in, hgrn_lb_logits=m_hgrn_lb_logits,
               hgrn_norm_g=m_hgrn_norm_g, hgrn_w_o=m_hgrn_w_o, ln_mix_g=m_ln_mix_g, ln_mix_b=m_ln_mix_b, ffn_w_in=m_ffn_w_in,
               ffn_w_out=m_ffn_w_out, ln_ffn_g=m_ln_ffn_g, ln_ffn_b=m_ln_ffn_b, ple_w_gate=m_ple_w_gate, ple_w_proj=m_ple_w_proj)
    var = dict(att_w_qkv=v_att_w_qkv, att_sink=v_att_sink, att_w_o=v_att_w_o, hgrn_w_in=v_hgrn_w_in, hgrn_lb_logits=v_hgrn_lb_logits,
               hgrn_norm_g=v_hgrn_norm_g, hgrn_w_o=v_hgrn_w_o, ln_mix_g=v_ln_mix_g, ln_mix_b=v_ln_mix_b, ffn_w_in=v_ffn_w_in,
               ffn_w_out=v_ffn_w_out, ln_ffn_g=v_ln_ffn_g, ln_ffn_b=v_ln_ffn_b, ple_w_gate=v_ple_w_gate, ple_w_proj=v_ple_w_proj)
    shapes = {n: wts[n].shape for n, _ in BIG}
    chip = 2 * lax.axis_index("x") + lax.axis_index("y")
    core = lax.axis_index("c").reshape(1).astype(jnp.int32)
    xin, target = x[0], loss_target[0]
    t = xin.shape[0]
    row = lambda a, i: a[i][None]

    bf_shards = {n: wts[n].astype(BF16) for n, _ in BIG}
    lb_sh = hgrn_lb_logits.shape
    lb_bits = lax.bitcast_convert_type(hgrn_lb_logits.reshape(-1), BF16).reshape(-1, PACK_W)
    groups = _gather_groups()
    packed = [_pack_parts(bf_shards, parts) for parts in groups]
    packed[0] = jnp.concatenate([packed[0], lb_bits, jnp.zeros((LB_ROWS - lb_bits.shape[0], PACK_W), BF16)], axis=0)

    gathered = _put(_run_guest(_gather_guest(packed[0]), name="gather_first")[0], packed[0], chip)
    wfull, r_big = _gathered_parts(gathered, groups[0], shapes)
    lb_rows = gathered[:, r_big:r_big + lb_bits.shape[0]].reshape(N_CHIPS, -1, 2)
    lb_full = lax.bitcast_convert_type(lb_rows, F32).reshape(N_CHIPS, lb_sh[0], lb_sh[1], lb_sh[2])
    lb_full = lb_full.transpose(1, 2, 0, 3).reshape(lb_sh[0], lb_sh[1], N_CHIPS * lb_sh[2])
    logits2d = lb_full.reshape(DEPTH, 2 * D_MODEL)
    lb_all = _lower_bounds(logits2d, name="lb_fwd").reshape(DEPTH, 2, 1, D_MODEL)
    tables = _rope_tables(t)

    saved, weights = [], []
    xf, xb = xin, xin.astype(BF16)
    for i in range(DEPTH):
        j = i // 2
        nxt = _gather_guest(packed[i + 1]) if i + 1 < DEPTH else None
        s = dict(xin_bf=xb)
        if i % 2 == 0:
            q, k, v = _qkv_rope(xb, wfull[("att_w_qkv", j)], tables, name=f"qkv_rope_{i}")
            (o, lse), got = _attn_fwd(q, k, v, att_sink[j], name=f"attn_fwd_{i}", guest=nxt)
            s.update(q=q, k=k, v=v, o=o, lse=lse)
            mix_in = o
        else:
            z = _mm_nn(xb, wfull[("hgrn_w_in", j)], out_dtype=F32, name=f"hgrn_in_{i}")
            (o_f, s_f), _ = _hgrn_fwd(z, lb_all[i, 0], False, name=f"hgrn_scan_f_{i}")
            (o_b, s_b), _ = _hgrn_fwd(z, lb_all[i, 1], True, name=f"hgrn_scan_b_{i}")
            og = _hgrn_post_fwd(o_f, o_b, z, row(hgrn_norm_g, j), name=f"hgrn_post_{i}")
            s.update(z=z, o_f=o_f, o_b=o_b, s_f=s_f, s_b=s_b, og=og)
            mix_in = og
        in_mixer = nxt is not None and i % 2 == 0
        if in_mixer:
            wfull.update(_gathered_parts(_put(got[0], packed[i + 1], chip), groups[i + 1], shapes)[0])
        w = {n: wfull[(n, l)] for n, l in _layer_parts(i)}
        w_o = w["att_w_o" if i % 2 == 0 else "hgrn_w_o"]
        x1, x1b, xh1, rs1 = _mm_res_ln(mix_in, w_o, xf, row(ln_mix_g, i), row(ln_mix_b, i), name=f"mix_out_ln_{i}")
        (g, u, h), got = _ffn_in(x1b, w["ffn_w_in"], name=f"ffn_in_{i}", guest=None if in_mixer else nxt)
        if nxt is not None and not in_mixer:
            wfull.update(_gathered_parts(_put(got[0], packed[i + 1], chip), groups[i + 1], shapes)[0])
        x2, x2b, xh2, rs2 = _mm_res_ln(h, w["ffn_w_out"], x1, row(ln_ffn_g, i), row(ln_ffn_b, i), name=f"ffn_out_ln_{i}")
        xf, xb, gate, pp = _ple_fwd(x2, x2b, p, i, w["ple_w_gate"], w["ple_w_proj"], name=f"ple_fwd_{i}")
        s.update(x1b=x1b, xh1=xh1, rs1=rs1, g=g, u=u, h=h, x2b=x2b, xh2=xh2, rs2=rs2, gate=gate, pp=pp)
        saved.append(s)
        weights.append(w)

    loss, dx = _loss_head(xf, target, name="loss_head")
    loss = lax.psum(loss[0, 0], ("x", "y", "c"))

    gs = {n: [None] * DEPTH for n in SMALL}
    gs.update(att_sink=[None] * 2, hgrn_norm_g=[None] * 2)
    dlb = [jnp.zeros((2, D_MODEL), F32)] * DEPTH
    halves = [None] * DEPTH
    slabs = None
    for i in reversed(range(DEPTH)):
        j = i // 2
        s, w = saved[i], weights[i]
        gw = {}
        res, got = _ple_bwd(dx, s["gate"], s["pp"], w["ple_w_gate"], s["xh2"], s["rs2"], row(ln_ffn_g, i), name=f"ple_bwd_{i}",
                            guest=None if slabs is None else _pair_exchange_guest(slabs))
        du2, du2b, dpre, dpp, gs["ln_ffn_g"][i], gs["ln_ffn_b"][i] = res
        part = None if slabs is None else _add_own_half(slabs, got[0], core, name=f"grad_pair_add_{i + 1}")
        gw["ple_w_gate"] = _mm_tn(s["x2b"], dpre, name=f"dw_ple_gate_{i}")
        gw["ple_w_proj"] = _mm_tn_p(p, i, dpp, name=f"dw_ple_proj_{i}")
        dgg, dgu = _ffn_out_bwd(du2b, w["ffn_w_out"], s["g"], s["u"], name=f"ffn_out_bwd_{i}")
        gw["ffn_w_out"] = _mm_tn(s["h"], du2b, name=f"dw_ffn_out_{i}")
        res = _mm_nt([dgg, dgu], w["ffn_w_in"], resid=du2, ln=(s["xh1"], s["rs1"], row(ln_mix_g, i)),
                     name=f"ffn_in_bwd_{i}", guest=None if part is None else _chip_scatter_guest(part))
        (du1, du1b, gs["ln_mix_g"][i], gs["ln_mix_b"][i]), got = res if part is not None else (res, None)
        if part is not None:
            halves[i + 1] = _sum_chips(_put(got[0], _own(part, chip), chip), name=f"grad_chip_sum_{i + 1}")
        shard_w = shapes["ffn_w_in"][2]
        gw["ffn_w_in"] = jnp.concatenate(
            [_mm_tn(s["x1b"], dgg, name=f"dw_ffn_gate_{i}", shard_cols=shard_w),
             _mm_tn(s["x1b"], dgu, name=f"dw_ffn_up_{i}", shard_cols=shard_w)], axis=0)
        if i % 2 == 0:
            gw["att_w_o"] = _mm_tn(s["o"], du1b, name=f"dw_att_o_{i}")
            do = _mm_nt([du1b], w["att_w_o"], out_dtype=BF16, name=f"att_o_bwd_{i}")
            dq, dkw, dvw, gs["att_sink"][j] = _attn_bwd(s["q"], s["k"], s["v"], s["o"], do, s["lse"], att_sink[j], name=f"attn_bwd_{i}")
            dqkv = _attn_post_bwd(dq, dkw, dvw, tables, name=f"attn_post_bwd_{i}")
            gw["att_w_qkv"] = _mm_tn(s["xin_bf"], dqkv, name=f"dw_att_qkv_{i}")
            dx = _mm_nt([dqkv], w["att_w_qkv"], resid=du1, name=f"qkv_bwd_{i}")
        else:
            gw["hgrn_w_o"] = _mm_tn(s["og"], du1b, name=f"dw_hgrn_o_{i}")
            dy = _mm_nt([du1b], w["hgrn_w_o"], out_dtype=BF16, name=f"hgrn_o_bwd_{i}")
            d_o, dgate, gs["hgrn_norm_g"][j] = _hgrn_post_bwd(dy, s["o_f"], s["o_b"], s["z"], row(hgrn_norm_g, j), name=f"hgrn_post_bwd_{i}")
            dq_f, dzf_f, dv_f, dlb_f = _hgrn_bwd(s["z"], lb_all[i, 0], d_o, s["s_f"], False, name=f"hgrn_scan_f_bwd_{i}")
            dq, dzf_b, dv, dlb_b = _hgrn_bwd(s["z"], lb_all[i, 1], d_o, s["s_b"], True, name=f"hgrn_scan_b_bwd_{i}",
                                             other=(dq_f, dv_f))
            dlb[i] = jnp.stack([dlb_f.reshape(D_MODEL), dlb_b.reshape(D_MODEL)])
            dz = [dq, dzf_f, dzf_b, dv, dgate]
            gw["hgrn_w_in"] = jnp.concatenate(
                [_mm_tn(s["xin_bf"], part, name=f"dw_hgrn_in_{i}_{n}") for n, part in enumerate(dz)], axis=1)
            dx = _mm_nt(dz, w["hgrn_w_in"], resid=du1, name=f"hgrn_in_bwd_{i}")
        slabs = _layer_slabs(gw, i, shapes)

    from_sibling = _run_guest(_pair_exchange_guest(slabs), name="grad_pair_exchange_0")[0]
    part = _add_own_half(slabs, from_sibling, core, name="grad_pair_add_0")
    from_chips = _run_guest(_chip_scatter_guest(part), name="grad_chip_scatter_0")[0]
    halves[0] = _sum_chips(_put(from_chips, _own(part, chip), chip), name="grad_chip_sum_0")
    shared = _run_guest(_pair_share_guest(halves), name="grad_pair_share")
    reduced = [_put(buf, half, lax.axis_index("c")) for buf, half in zip(shared, halves)]

    g_layers = [_unpack_layer(reduced[i].reshape(-1, PACK_W), i, shapes) for i in range(DEPTH)]
    grads = {}
    for n, _ in BIG:
        per_layer = [g_layers[i][n] for i in range(DEPTH) if n in g_layers[i]]
        grads[n] = jnp.stack(per_layer)

    gsmall = {n: jnp.concatenate(v, axis=0) for n, v in gs.items()}
    dlogits = _lower_bounds_bwd(logits2d, jnp.stack(dlb).reshape(DEPTH, 2 * D_MODEL), name="lb_bwd").reshape(DEPTH, 2, D_MODEL)
    g_small_full = _unpack_small(_allreduce_small(_pack_small(gsmall, dlogits), name="allreduce_small"),
                                 (lb_sh[0], lb_sh[1], N_CHIPS * lb_sh[2]))
    grads.update({n: g_small_full[n] for n in SMALL + ("att_sink", "hgrn_norm_g")})
    grads["hgrn_lb_logits"] = lax.dynamic_slice_in_dim(g_small_full["hgrn_lb_logits"], chip * lb_sh[2], lb_sh[2], axis=2)

    delta, new_m, new_v = {}, {}, {}
    for n, _ in BIG:
        two_d = lambda a: a.reshape(-1, a.shape[-1])
        d, nm, nv = _adamw(two_d(wts[n]), two_d(grads[n]), two_d(mom[n]), two_d(var[n]), name=f"adamw_{n}")
        delta[n], new_m[n], new_v[n] = d.reshape(shapes[n]), nm.reshape(shapes[n]), nv.reshape(shapes[n])
    packs = [_pack_small(src, src["hgrn_lb_logits"]) for src in (wts, grads, mom, var)]
    outs = _adamw(*packs, name="adamw_small")
    for dst, buf in zip((delta, new_m, new_v), outs):
        dst.update(_unpack_small(buf, lb_sh))

    return (loss, dx[None], *[grads[n] for n in WEIGHTS], *[delta[n] for n in WEIGHTS],
            *[new_m[n] for n in WEIGHTS], *[new_v[n] for n in WEIGHTS])
```
